```python
import math
import jax, jax.numpy as jnp
from jax import lax
import numpy as np

D_MODEL = 1024
BATCH = 8
SEQ = 4096
DEPTH = 2

GRID_W = 64
CTX_LEN = 256
EPS = 1e-6
F32 = jnp.float32
HEAD_DIM = 64
ROPE_BASE = 10000.0

WA_HEADS = 4
WA_KV_HEADS = 2
WA_WINDOW = 128
WA_BLOCK = 128
NA_HEADS = 4
NA_MAX_KH = 8
NA_KW = 16
NA_QBW = 16
NA_KBW = NA_QBW + NA_KW
SSM_HEADS = 8
SSM_HEAD_DIM = 64
SSM_INNER = SSM_HEADS * SSM_HEAD_DIM
SSM_GROUPS = 2
SSM_STATE = 128
SSM_CONV = 7
SSM_CHUNK = 128
D_FF = ((8 * D_MODEL + 3 * 256 - 1) // (3 * 256)) * 256

QA_COLS = WA_HEADS * HEAD_DIM
QB_COLS = NA_HEADS * HEAD_DIM
Z_COLS = SSM_INNER
Q_SIDE = QA_COLS + QB_COLS + Z_COLS
KA_COLS = WA_KV_HEADS * HEAD_DIM
KB_COLS = NA_HEADS * HEAD_DIM
XBC_COLS = SSM_INNER + 2 * SSM_GROUPS * SSM_STATE
DT_COLS = 2 * SSM_HEADS
IN_COLS = Q_SIDE + 2 * KA_COLS + 2 * KB_COLS + XBC_COLS + DT_COLS
MIX_WIDTH = QA_COLS + QB_COLS + SSM_INNER

kernel_name = 'hymba_style_window_natten_ssd_prefix_dit'


def rms_norm(x, g):
    xf = x.astype(F32)
    y = xf * lax.rsqrt(jnp.mean(xf * xf, axis=-1, keepdims=True) + EPS)
    return (y * g.astype(F32)).astype(x.dtype)


def modulate(h, shift, scale):
    return h * (1 + scale) + shift


def split_cols(p, sizes):
    out, off = [], 0
    for s in sizes:
        out.append(p[..., off:off + s])
        off += s
    return out


def rope_2d(x, rows, cols):
    d = x.shape[-1]
    half = d // 2
    quarter = half // 2
    inv_freq = ROPE_BASE ** (-jnp.arange(quarter, dtype=F32) / quarter)
    xf = x.astype(F32)

    def rot(xp, pos):
        ang = pos.astype(F32)[:, None] * inv_freq[None, :]
        cos = jnp.cos(ang)[None, :, None, :]
        sin = jnp.sin(ang)[None, :, None, :]
        x1, x2 = xp[..., :quarter], xp[..., quarter:]
        return jnp.concatenate([x1 * cos - x2 * sin, x2 * cos + x1 * sin], axis=-1)

    return jnp.concatenate([rot(xf[..., :half], rows), rot(xf[..., half:], cols)], axis=-1).astype(x.dtype)


def window_attention(q, k, v, k_ctx, v_ctx, sink):
    b, L, H, d = q.shape
    G = k.shape[2]
    rep = H // G
    blk = WA_BLOCK
    nb = L // blk
    Lc = k_ctx.shape[1]
    scale = d ** -0.5
    qb = q.reshape(b, nb, blk, G, rep, d)

    def band(t):
        tp = jnp.pad(t, ((0, 0), (blk, blk), (0, 0), (0, 0))).reshape(b, nb + 2, blk, G, d)
        return jnp.concatenate([tp[:, 0:nb], tp[:, 1:nb + 1], tp[:, 2:nb + 2]], axis=2)

    kb, vb = band(k), band(v)
    qpos = jnp.arange(nb)[:, None] * blk + jnp.arange(blk)[None, :]
    kpos = (jnp.arange(nb)[:, None] - 1) * blk + jnp.arange(3 * blk)[None, :]
    mask = ((jnp.abs(qpos[:, :, None] - kpos[:, None, :]) <= WA_WINDOW)
            & (kpos[:, None, :] >= 0) & (kpos[:, None, :] < L))
    s_loc = jnp.einsum('bnqgrd,bnkgd->bngrqk', qb, kb).astype(F32) * scale
    s_loc = jnp.where(mask[None, :, None, None], s_loc, -jnp.inf)
    s_ctx = jnp.einsum('bnqgrd,bcgd->bngrqc', qb, k_ctx).astype(F32) * scale
    s_sink = jnp.broadcast_to(sink.astype(F32).reshape(G, rep)[None, None, :, :, None, None],
                              s_loc.shape[:-1] + (1,))
    p = jax.nn.softmax(jnp.concatenate([s_loc, s_ctx, s_sink], axis=-1), axis=-1).astype(v.dtype)
    nk = 3 * blk
    o = (jnp.einsum('bngrqk,bnkgd->bnqgrd', p[..., :nk], vb)
         + jnp.einsum('bngrqc,bcgd->bnqgrd', p[..., nk:nk + Lc], v_ctx))
    return o.reshape(b, L, H * d)


def context_attention(q, k, v, sink):
    b, Lc, H, d = q.shape
    G = k.shape[2]
    rep = H // G
    qg = q.reshape(b, Lc, G, rep, d)
    s = jnp.einsum('bqgrd,bkgd->bgrqk', qg, k).astype(F32) * d ** -0.5
    if sink is not None:
        s_sink = jnp.broadcast_to(sink.astype(F32).reshape(G, rep)[None, :, :, None, None], s.shape[:-1] + (1,))
        s = jnp.concatenate([s, s_sink], axis=-1)
    p = jax.nn.softmax(s, axis=-1)[..., :Lc].astype(v.dtype)
    o = jnp.einsum('bgrqk,bkgd->bqgrd', p, v)
    return o.reshape(b, Lc, H * d)


def neighbourhood_attention(q, k, v, k_ctx, v_ctx, rpb, grid_rows):
    b, L, H, d = q.shape
    kh = min(NA_MAX_KH, grid_rows)
    ncb = GRID_W // NA_QBW
    scale = d ** -0.5
    r = jnp.arange(grid_rows)
    row_idx = jnp.clip(r - kh // 2, 0, grid_rows - kh)[:, None] + jnp.arange(kh)[None, :]
    cb = jnp.arange(ncb)
    col_idx = jnp.clip(cb * NA_QBW - NA_KW // 2, 0, GRID_W - NA_KBW)[:, None] + jnp.arange(NA_KBW)[None, :]
    qcol = cb[:, None] * NA_QBW + jnp.arange(NA_QBW)[None, :]
    cstart = jnp.clip(qcol - NA_KW // 2, 0, GRID_W - NA_KW)
    cmask = (col_idx[:, None, :] >= cstart[:, :, None]) & (col_idx[:, None, :] < cstart[:, :, None] + NA_KW)
    dy = row_idx - r[:, None] + (NA_MAX_KH - 1)
    dx = jnp.clip(col_idx[:, None, :] - qcol[:, :, None], -(NA_KW - 1), NA_KW - 1) + (NA_KW - 1)
    bias = rpb.astype(F32)[:, dy[:, None, None, :, None], dx[None, :, :, None, :]]
    bias = jnp.moveaxis(bias, 0, 2)
    qg = q.reshape(b, grid_rows, ncb, NA_QBW, H, d)
    kg = k.reshape(b, grid_rows, GRID_W, H, d)
    vg = v.reshape(b, grid_rows, GRID_W, H, d)
    ri = row_idx[:, None, :, None]
    ci = col_idx[None, :, None, :]
    kwin = kg[:, ri, ci]
    vwin = vg[:, ri, ci]
    s = jnp.einsum('brcqhd,brcyxhd->brchqyx', qg, kwin).astype(F32) * scale + bias[None]
    s = jnp.where(cmask[None, None, :, None, :, None, :], s, -jnp.inf)
    nloc = kh * NA_KBW
    s = s.reshape(b, grid_rows, ncb, H, NA_QBW, nloc)
    s_ctx = jnp.einsum('brcqhd,bkhd->brchqk', qg, k_ctx).astype(F32) * scale
    p = jax.nn.softmax(jnp.concatenate([s, s_ctx], axis=-1), axis=-1).astype(v.dtype)
    p_loc = p[..., :nloc].reshape(b, grid_rows, ncb, H, NA_QBW, kh, NA_KBW)
    o = (jnp.einsum('brchqyx,brcyxhd->brcqhd', p_loc, vwin)
         + jnp.einsum('brchqk,bkhd->brcqhd', p[..., nloc:], v_ctx))
    return o.reshape(b, L, H * d)


def depthwise_conv(x, w, bias):
    k = w.shape[0]
    y = lax.conv_general_dilated(x, w[:, None, :].astype(x.dtype), window_strides=(1,),
                                 padding=[(k // 2, k // 2)], dimension_numbers=('NWC', 'WIO', 'NWC'),
                                 feature_group_count=x.shape[-1])
    return y + bias


def ssm_prepare(xbc_raw, dt_raw, conv_w, conv_b, dt_bias):
    xbc = jax.nn.silu(depthwise_conv(xbc_raw, conv_w, conv_b))
    xs, bm, cm = split_cols(xbc, (SSM_INNER, SSM_GROUPS * SSM_STATE, SSM_GROUPS * SSM_STATE))
    b, L = xs.shape[:2]
    rep = SSM_HEADS // SSM_GROUPS
    xs = xs.reshape(b, L, SSM_HEADS, SSM_HEAD_DIM)
    bm = jnp.repeat(bm.reshape(b, L, SSM_GROUPS, SSM_STATE), rep, axis=2)
    cm = jnp.repeat(cm.reshape(b, L, SSM_GROUPS, SSM_STATE), rep, axis=2)
    dt = jax.nn.softplus(dt_raw.astype(F32) + dt_bias.astype(F32).reshape(2 * SSM_HEADS))
    return xs, bm, cm, dt.reshape(b, L, 2, SSM_HEADS)


def ssd_scan(x, dt, A, Bm, Cm, h0, with_output):
    b, L, H, P = x.shape
    N = Bm.shape[-1]
    Q = SSM_CHUNK
    nc = L // Q
    xc = x.astype(F32).reshape(b, nc, Q, H, P)
    dtc = dt.astype(F32).reshape(b, nc, Q, H)
    bc = Bm.astype(F32).reshape(b, nc, Q, H, N)
    cc = Cm.astype(F32).reshape(b, nc, Q, H, N)
    acum = jnp.cumsum(dtc * A, axis=2)
    decay_to_end = jnp.exp(acum[:, :, -1:, :] - acum)
    states = jnp.einsum('bcjhn,bcjh,bcjhp->bchpn', bc, decay_to_end * dtc, xc)
    chunk_decay = jnp.exp(acum[:, :, -1, :])

    def step(h, inp):
        st, dec = inp
        return h * dec[:, :, None, None] + st, h

    h_final, h_enter = lax.scan(step, h0, (jnp.moveaxis(states, 1, 0), jnp.moveaxis(chunk_decay, 1, 0)))
    if not with_output:
        return h_final
    h_enter = jnp.moveaxis(h_enter, 0, 1)
    seg = acum[:, :, :, None, :] - acum[:, :, None, :, :]
    lower = jnp.tril(jnp.ones((Q, Q), dtype=bool))
    decay_ij = jnp.exp(jnp.where(lower[None, None, :, :, None], seg, -jnp.inf))
    w = jnp.einsum('bcihn,bcjhn->bcijh', cc, bc) * decay_ij * dtc[:, :, None, :, :]
    y = (jnp.einsum('bcijh,bcjhp->bcihp', w, xc)
         + jnp.einsum('bcihn,bchpn->bcihp', cc, h_enter) * jnp.exp(acum)[..., None])
    return y.reshape(b, L, H, P), h_final


def ssd_bidir(xs, bm, cm, dt, A, h0_f, h0_b, with_output):
    rev = lambda t: jnp.flip(t, axis=1)
    fwd = ssd_scan(xs, dt[:, :, 0], A[0], bm, cm, h0_f, with_output)
    bwd = ssd_scan(rev(xs), rev(dt[:, :, 1]), A[1], rev(bm), rev(cm), h0_b, with_output)
    if not with_output:
        return fwd, bwd
    (y_f, h_f), (y_b, h_b) = fwd, bwd
    return y_f + rev(y_b), h_f, h_b


def ssm_output(y, xs, z, d_skip, g):
    b, L = y.shape[:2]
    y = y + d_skip.astype(F32)[:, None] * xs.astype(F32)
    y = y.reshape(b, L, SSM_INNER) * jax.nn.silu(z.astype(F32))
    return rms_norm(y, g).astype(z.dtype)


def swiglu(h, w_in, w_out):
    gate, up = jnp.split(h @ w_in, 2, axis=-1)
    return (jax.nn.silu(gate) * up) @ w_out


def hybrid_layer(xl, xc, sc, scc, rows, cols, grid_rows, w_mod, b_mod, g_mix, w_in, wa_sink, na_rpb,
                 conv_w, conv_b, dt_bias, a_log, d_skip, ssm_g, w_out, g_ffn, w_ffn_in, w_ffn_out, ctx_out):
    D = D_MODEL
    b, L, _ = xl.shape
    Lc = xc.shape[1]
    hd = HEAD_DIM
    mod_l = sc @ w_mod + b_mod
    sh1, sc1, gt1, sh2, sc2, gt2 = [m[:, None, :] for m in jnp.split(mod_l, 6, axis=-1)]
    n_ctx_mod = 6 if ctx_out else 2
    mods_c = jnp.split(scc @ w_mod[:, :n_ctx_mod * D] + b_mod[:n_ctx_mod * D], n_ctx_mod)

    hl = modulate(rms_norm(xl, g_mix), sh1, sc1)
    hc = modulate(rms_norm(xc, g_mix), mods_c[0], mods_c[1])
    kv_sizes = (KA_COLS, KA_COLS, KB_COLS, KB_COLS, XBC_COLS, DT_COLS)
    qa, qb, z, ka, va, kb, vb, xbc, dtr = split_cols(hl @ w_in, (QA_COLS, QB_COLS, Z_COLS) + kv_sizes)
    if ctx_out:
        qa_c, qb_c, z_c, ka_c, va_c, kb_c, vb_c, xbc_c, dtr_c = split_cols(hc @ w_in, (QA_COLS, QB_COLS, Z_COLS) + kv_sizes)
    else:
        ka_c, va_c, kb_c, vb_c, xbc_c, dtr_c = split_cols(hc @ w_in[:, Q_SIDE:], kv_sizes)

    ka_c = ka_c.reshape(b, Lc, WA_KV_HEADS, hd)
    va_c = va_c.reshape(b, Lc, WA_KV_HEADS, hd)
    o_a = window_attention(rope_2d(qa.reshape(b, L, WA_HEADS, hd), rows, cols),
                           rope_2d(ka.reshape(b, L, WA_KV_HEADS, hd), rows, cols),
                           va.reshape(b, L, WA_KV_HEADS, hd), ka_c, va_c, wa_sink)
    kb_c = kb_c.reshape(b, Lc, NA_HEADS, hd)
    vb_c = vb_c.reshape(b, Lc, NA_HEADS, hd)
    o_b = neighbourhood_attention(qb.reshape(b, L, NA_HEADS, hd), kb.reshape(b, L, NA_HEADS, hd),
                                  vb.reshape(b, L, NA_HEADS, hd), kb_c, vb_c, na_rpb, grid_rows)
    A = -jnp.exp(a_log.astype(F32))
    xs_c, bm_c, cm_c, dt_c = ssm_prepare(xbc_c, dtr_c, conv_w, conv_b, dt_bias)
    h0 = jnp.zeros((b, SSM_HEADS, SSM_HEAD_DIM, SSM_STATE), F32)
    if ctx_out:
        y_c, h_f, h_b = ssd_bidir(xs_c, bm_c, cm_c, dt_c, A, h0, h0, True)
    else:
        h_f, h_b = ssd_bidir(xs_c, bm_c, cm_c, dt_c, A, h0, h0, False)
    xs, bm, cm, dt = ssm_prepare(xbc, dtr, conv_w, conv_b, dt_bias)
    y_l, _, _ = ssd_bidir(xs, bm, cm, dt, A, h_f, h_b, True)
    o_c = ssm_output(y_l, xs, z, d_skip, ssm_g)

    mix = jnp.concatenate([o_a, o_b, o_c.astype(o_a.dtype)], axis=-1) @ w_out
    xl = xl + gt1 * mix
    xl = xl + gt2 * swiglu(modulate(rms_norm(xl, g_ffn), sh2, sc2), w_ffn_in, w_ffn_out)
    if not ctx_out:
        return xl, None

    o_ac = context_attention(qa_c.reshape(b, Lc, WA_HEADS, hd), ka_c, va_c, wa_sink)
    o_bc = context_attention(qb_c.reshape(b, Lc, NA_HEADS, hd), kb_c, vb_c, None)
    o_cc = ssm_output(y_c, xs_c, z_c, d_skip, ssm_g)
    mix_c = jnp.concatenate([o_ac, o_bc, o_cc.astype(o_ac.dtype)], axis=-1) @ w_out
    xc = xc + mods_c[2] * mix_c
    xc = xc + mods_c[5] * swiglu(modulate(rms_norm(xc, g_ffn), mods_c[3], mods_c[4]), w_ffn_in, w_ffn_out)
    return xl, xc


def _fwd_setup_inputs(seed: int = 0) -> dict:
    key = jax.random.key(seed)
    ks = jax.random.split(key, 24)
    nrm = jax.random.normal
    D = D_MODEL
    dt0 = jnp.exp(jax.random.uniform(ks[12], (DEPTH, 2, SSM_HEADS), minval=math.log(1e-3), maxval=math.log(0.1)))
    return {
        'x': nrm(ks[0], (BATCH, SEQ, D), F32),
        'c': nrm(ks[1], (BATCH, D), F32),
        'ctx': nrm(ks[2], (BATCH, CTX_LEN, D), F32),
        'c_ctx': nrm(ks[3], (D,), F32),
        'w_mod': nrm(ks[4], (DEPTH, D, 6 * D), F32) * (0.5 * D ** -0.5),
        'b_mod': nrm(ks[5], (DEPTH, 6 * D), F32) * 0.01,
        'g_mix': 1.0 + 0.05 * nrm(ks[6], (DEPTH, D), F32),
        'w_in': nrm(ks[7], (DEPTH, D, IN_COLS), F32) * D ** -0.5,
        'wa_sink': nrm(ks[8], (DEPTH, WA_HEADS), F32) * 0.5,
        'na_rpb': nrm(ks[9], (DEPTH, NA_HEADS, 2 * NA_MAX_KH - 1, 2 * NA_KW - 1), F32) * 0.1,
        'ssm_conv_w': nrm(ks[10], (DEPTH, SSM_CONV, XBC_COLS), F32) * SSM_CONV ** -0.5,
        'ssm_conv_b': nrm(ks[11], (DEPTH, XBC_COLS), F32) * 0.01,
        'ssm_dt_bias': dt0 + jnp.log(-jnp.expm1(-dt0)),
        'ssm_a_log': jnp.log(jax.random.uniform(ks[13], (DEPTH, 2, SSM_HEADS), minval=1.0, maxval=16.0)),
        'ssm_d': 1.0 + 0.1 * nrm(ks[14], (DEPTH, SSM_HEADS), F32),
        'ssm_norm_g': 1.0 + 0.05 * nrm(ks[15], (DEPTH, SSM_INNER), F32),
        'w_out': nrm(ks[16], (DEPTH, MIX_WIDTH, D), F32) * MIX_WIDTH ** -0.5,
        'g_ffn': 1.0 + 0.05 * nrm(ks[17], (DEPTH, D), F32),
        'w_ffn_in': nrm(ks[18], (DEPTH, D, 2 * D_FF), F32) * D ** -0.5,
        'w_ffn_out': nrm(ks[19], (DEPTH, D_FF, D), F32) * D_FF ** -0.5,
        'g_final': 1.0 + 0.05 * nrm(ks[20], (D,), F32),
    }


def _fwd_reference(x, c, ctx, c_ctx, w_mod, b_mod, g_mix, w_in, wa_sink, na_rpb, ssm_conv_w, ssm_conv_b,
              ssm_dt_bias, ssm_a_log, ssm_d, ssm_norm_g, w_out, g_ffn, w_ffn_in, w_ffn_out, g_final):
    L = x.shape[1]
    grid_rows = L // GRID_W
    t = jnp.arange(L)
    rows, cols = t // GRID_W, t % GRID_W
    sc = jax.nn.silu(c)
    scc = jax.nn.silu(c_ctx)
    xl, xc = x, ctx
    for i in range(DEPTH):
        xl, xc = hybrid_layer(xl, xc, sc, scc, rows, cols, grid_rows, w_mod[i], b_mod[i], g_mix[i], w_in[i],
                              wa_sink[i], na_rpb[i], ssm_conv_w[i], ssm_conv_b[i], ssm_dt_bias[i], ssm_a_log[i],
                              ssm_d[i], ssm_norm_g[i], w_out[i], g_ffn[i], w_ffn_in[i], w_ffn_out[i],
                              ctx_out=(i < DEPTH - 1))
    return rms_norm(xl, g_final)


import jax as _jax
import jax.numpy as _jnp

TWIN_FORMAT = 'train_step'
FWD_PARAMS = ['x', 'c', 'ctx', 'c_ctx', 'w_mod', 'b_mod', 'g_mix', 'w_in', 'wa_sink', 'na_rpb', 'ssm_conv_w', 'ssm_conv_b', 'ssm_dt_bias', 'ssm_a_log', 'ssm_d', 'ssm_norm_g', 'w_out', 'g_ffn', 'w_ffn_in', 'w_ffn_out', 'g_final']
TWIN_WEIGHTS = ['c_ctx', 'w_mod', 'b_mod', 'g_mix', 'w_in', 'wa_sink', 'na_rpb', 'ssm_conv_w', 'ssm_conv_b', 'ssm_dt_bias', 'ssm_a_log', 'ssm_d', 'ssm_norm_g', 'w_out', 'g_ffn', 'w_ffn_in', 'w_ffn_out', 'g_final']
TWIN_DIFF_INPUT = 'x'
TWIN_INPUTS = ['x', 'c', 'ctx', 'c_ctx', 'w_mod', 'b_mod', 'g_mix', 'w_in', 'wa_sink', 'na_rpb', 'ssm_conv_w', 'ssm_conv_b', 'ssm_dt_bias', 'ssm_a_log', 'ssm_d', 'ssm_norm_g', 'w_out', 'g_ffn', 'w_ffn_in', 'w_ffn_out', 'g_final', 'loss_target', 'm_c_ctx', 'm_w_mod', 'm_b_mod', 'm_g_mix', 'm_w_in', 'm_wa_sink', 'm_na_rpb', 'm_ssm_conv_w', 'm_ssm_conv_b', 'm_ssm_dt_bias', 'm_ssm_a_log', 'm_ssm_d', 'm_ssm_norm_g', 'm_w_out', 'm_g_ffn', 'm_w_ffn_in', 'm_w_ffn_out', 'm_g_final', 'v_c_ctx', 'v_w_mod', 'v_b_mod', 'v_g_mix', 'v_w_in', 'v_wa_sink', 'v_na_rpb', 'v_ssm_conv_w', 'v_ssm_conv_b', 'v_ssm_dt_bias', 'v_ssm_a_log', 'v_ssm_d', 'v_ssm_norm_g', 'v_w_out', 'v_g_ffn', 'v_w_ffn_in', 'v_w_ffn_out', 'v_g_final']
TWIN_OUTPUTS = ['loss', 'grad_x', 'grad_c_ctx', 'grad_w_mod', 'grad_b_mod', 'grad_g_mix', 'grad_w_in', 'grad_wa_sink', 'grad_na_rpb', 'grad_ssm_conv_w', 'grad_ssm_conv_b', 'grad_ssm_dt_bias', 'grad_ssm_a_log', 'grad_ssm_d', 'grad_ssm_norm_g', 'grad_w_out', 'grad_g_ffn', 'grad_w_ffn_in', 'grad_w_ffn_out', 'grad_g_final', 'delta_c_ctx', 'delta_w_mod', 'delta_b_mod', 'delta_g_mix', 'delta_w_in', 'delta_wa_sink', 'delta_na_rpb', 'delta_ssm_conv_w', 'delta_ssm_conv_b', 'delta_ssm_dt_bias', 'delta_ssm_a_log', 'delta_ssm_d', 'delta_ssm_norm_g', 'delta_w_out', 'delta_g_ffn', 'delta_w_ffn_in', 'delta_w_ffn_out', 'delta_g_final', 'new_m_c_ctx', 'new_m_w_mod', 'new_m_b_mod', 'new_m_g_mix', 'new_m_w_in', 'new_m_wa_sink', 'new_m_na_rpb', 'new_m_ssm_conv_w', 'new_m_ssm_conv_b', 'new_m_ssm_dt_bias', 'new_m_ssm_a_log', 'new_m_ssm_d', 'new_m_ssm_norm_g', 'new_m_w_out', 'new_m_g_ffn', 'new_m_w_ffn_in', 'new_m_w_ffn_out', 'new_m_g_final', 'new_v_c_ctx', 'new_v_w_mod', 'new_v_b_mod', 'new_v_g_mix', 'new_v_w_in', 'new_v_wa_sink', 'new_v_na_rpb', 'new_v_ssm_conv_w', 'new_v_ssm_conv_b', 'new_v_ssm_dt_bias', 'new_v_ssm_a_log', 'new_v_ssm_d', 'new_v_ssm_norm_g', 'new_v_w_out', 'new_v_g_ffn', 'new_v_w_ffn_in', 'new_v_w_ffn_out', 'new_v_g_final']
TWIN_LEAF_KINDS = {'loss': 'loss', 'grad_x': 'grad_x', 'grad_c_ctx': 'grad_w', 'grad_w_mod': 'grad_w', 'grad_b_mod': 'grad_w', 'grad_g_mix': 'grad_w', 'grad_w_in': 'grad_w', 'grad_wa_sink': 'grad_w', 'grad_na_rpb': 'grad_w', 'grad_ssm_conv_w': 'grad_w', 'grad_ssm_conv_b': 'grad_w', 'grad_ssm_dt_bias': 'grad_w', 'grad_ssm_a_log': 'grad_w', 'grad_ssm_d': 'grad_w', 'grad_ssm_norm_g': 'grad_w', 'grad_w_out': 'grad_w', 'grad_g_ffn': 'grad_w', 'grad_w_ffn_in': 'grad_w', 'grad_w_ffn_out': 'grad_w', 'grad_g_final': 'grad_w', 'delta_c_ctx': 'delta_w', 'delta_w_mod': 'delta_w', 'delta_b_mod': 'delta_w', 'delta_g_mix': 'delta_w', 'delta_w_in': 'delta_w', 'delta_wa_sink': 'delta_w', 'delta_na_rpb': 'delta_w', 'delta_ssm_conv_w': 'delta_w', 'delta_ssm_conv_b': 'delta_w', 'delta_ssm_dt_bias': 'delta_w', 'delta_ssm_a_log': 'delta_w', 'delta_ssm_d': 'delta_w', 'delta_ssm_norm_g': 'delta_w', 'delta_w_out': 'delta_w', 'delta_g_ffn': 'delta_w', 'delta_w_ffn_in': 'delta_w', 'delta_w_ffn_out': 'delta_w', 'delta_g_final': 'delta_w', 'new_m_c_ctx': 'new_m', 'new_m_w_mod': 'new_m', 'new_m_b_mod': 'new_m', 'new_m_g_mix': 'new_m', 'new_m_w_in': 'new_m', 'new_m_wa_sink': 'new_m', 'new_m_na_rpb': 'new_m', 'new_m_ssm_conv_w': 'new_m', 'new_m_ssm_conv_b': 'new_m', 'new_m_ssm_dt_bias': 'new_m', 'new_m_ssm_a_log': 'new_m', 'new_m_ssm_d': 'new_m', 'new_m_ssm_norm_g': 'new_m', 'new_m_w_out': 'new_m', 'new_m_g_ffn': 'new_m', 'new_m_w_ffn_in': 'new_m', 'new_m_w_ffn_out': 'new_m', 'new_m_g_final': 'new_m', 'new_v_c_ctx': 'new_v', 'new_v_w_mod': 'new_v', 'new_v_b_mod': 'new_v', 'new_v_g_mix': 'new_v', 'new_v_w_in': 'new_v', 'new_v_wa_sink': 'new_v', 'new_v_na_rpb': 'new_v', 'new_v_ssm_conv_w': 'new_v', 'new_v_ssm_conv_b': 'new_v', 'new_v_ssm_dt_bias': 'new_v', 'new_v_ssm_a_log': 'new_v', 'new_v_ssm_d': 'new_v', 'new_v_ssm_norm_g': 'new_v', 'new_v_w_out': 'new_v', 'new_v_g_ffn': 'new_v', 'new_v_w_ffn_in': 'new_v', 'new_v_w_ffn_out': 'new_v', 'new_v_g_final': 'new_v'}


def _forward(args):
    return _fwd_reference(*[args[k] for k in FWD_PARAMS])


def _output_shape():
    def fwd():
        inp = _fwd_setup_inputs(0)
        return _fwd_reference(*[inp[k] for k in FWD_PARAMS])
    out = _jax.eval_shape(fwd)
    return out.shape, out.dtype

N_MICROBATCH = 1
ADAM_LR = 0.001
ADAM_B1 = 0.9
ADAM_B2 = 0.999
ADAM_EPS = 1e-08
ADAM_WD = 0.01
ADAM_STEP = 10
PER_EXAMPLE_BATCH_AXIS = {'x': 0, 'c': 0, 'ctx': 0, 'loss_target': 0}
SHARED_INPUTS = []
_WEIGHT_DTYPES = {'c_ctx': _jnp.float32, 'w_mod': _jnp.float32, 'b_mod': _jnp.float32, 'g_mix': _jnp.float32, 'w_in': _jnp.float32, 'wa_sink': _jnp.float32, 'na_rpb': _jnp.float32, 'ssm_conv_w': _jnp.float32, 'ssm_conv_b': _jnp.float32, 'ssm_dt_bias': _jnp.float32, 'ssm_a_log': _jnp.float32, 'ssm_d': _jnp.float32, 'ssm_norm_g': _jnp.float32, 'w_out': _jnp.float32, 'g_ffn': _jnp.float32, 'w_ffn_in': _jnp.float32, 'w_ffn_out': _jnp.float32, 'g_final': _jnp.float32}
MOMENT_SCALE = {'c_ctx': 1.366725e-02, 'w_mod': 6.459761e-02, 'b_mod': 1.192834e-01, 'g_mix': 5.273272e-02, 'w_in': 3.470901e-02, 'wa_sink': 1.239527e-04, 'na_rpb': 1.374704e-03, 'ssm_conv_w': 3.938052e-02, 'ssm_conv_b': 5.823410e-02, 'ssm_dt_bias': 1.017853e-01, 'ssm_a_log': 1.621405e-01, 'ssm_d': 2.371641e-01, 'ssm_norm_g': 5.409144e-02, 'w_out': 4.095769e-02, 'g_ffn': 5.156053e-02, 'w_ffn_in': 2.286814e-02, 'w_ffn_out': 3.740475e-02, 'g_final': 3.213572e+01}


def _to_microbatches(a, axis):
    t = _jnp.moveaxis(a, axis, 0)
    t = t.reshape((N_MICROBATCH, t.shape[0] // N_MICROBATCH) + t.shape[1:])
    return _jnp.moveaxis(t, 1, axis + 1)


def setup_inputs(seed: int = 0) -> dict:
    inp = _fwd_setup_inputs(seed)
    key = _jax.random.fold_in(_jax.random.key(seed), 7919)
    shape, _ = _output_shape()
    out = dict(inp)
    out["loss_target"] = _jax.random.normal(_jax.random.fold_in(key, 0), shape, _jnp.float32)
    for i, name in enumerate(TWIN_WEIGHTS):
        w = inp[name].astype(_jnp.float32)
        if MOMENT_SCALE is None:
            s = _jnp.sqrt(_jnp.mean(_jnp.square(w)) + 1e-30)
        else:
            s = MOMENT_SCALE[name]
        km, kv = _jax.random.split(_jax.random.fold_in(key, i + 1))
        out[name] = w
        out["m_" + name] = s * _jax.random.normal(km, w.shape, _jnp.float32)
        out["v_" + name] = (s * s) * _jax.random.uniform(kv, w.shape, _jnp.float32, 0.5, 1.5)
    if N_MICROBATCH > 1:
        for name, axis in PER_EXAMPLE_BATCH_AXIS.items():
            out[name] = _to_microbatches(out[name], axis)
    return {'x': out['x'], 'c': out['c'], 'ctx': out['ctx'], 'c_ctx': out['c_ctx'], 'w_mod': out['w_mod'], 'b_mod': out['b_mod'], 'g_mix': out['g_mix'], 'w_in': out['w_in'], 'wa_sink': out['wa_sink'], 'na_rpb': out['na_rpb'], 'ssm_conv_w': out['ssm_conv_w'], 'ssm_conv_b': out['ssm_conv_b'], 'ssm_dt_bias': out['ssm_dt_bias'], 'ssm_a_log': out['ssm_a_log'], 'ssm_d': out['ssm_d'], 'ssm_norm_g': out['ssm_norm_g'], 'w_out': out['w_out'], 'g_ffn': out['g_ffn'], 'w_ffn_in': out['w_ffn_in'], 'w_ffn_out': out['w_ffn_out'], 'g_final': out['g_final'], 'loss_target': out['loss_target'], 'm_c_ctx': out['m_c_ctx'], 'm_w_mod': out['m_w_mod'], 'm_b_mod': out['m_b_mod'], 'm_g_mix': out['m_g_mix'], 'm_w_in': out['m_w_in'], 'm_wa_sink': out['m_wa_sink'], 'm_na_rpb': out['m_na_rpb'], 'm_ssm_conv_w': out['m_ssm_conv_w'], 'm_ssm_conv_b': out['m_ssm_conv_b'], 'm_ssm_dt_bias': out['m_ssm_dt_bias'], 'm_ssm_a_log': out['m_ssm_a_log'], 'm_ssm_d': out['m_ssm_d'], 'm_ssm_norm_g': out['m_ssm_norm_g'], 'm_w_out': out['m_w_out'], 'm_g_ffn': out['m_g_ffn'], 'm_w_ffn_in': out['m_w_ffn_in'], 'm_w_ffn_out': out['m_w_ffn_out'], 'm_g_final': out['m_g_final'], 'v_c_ctx': out['v_c_ctx'], 'v_w_mod': out['v_w_mod'], 'v_b_mod': out['v_b_mod'], 'v_g_mix': out['v_g_mix'], 'v_w_in': out['v_w_in'], 'v_wa_sink': out['v_wa_sink'], 'v_na_rpb': out['v_na_rpb'], 'v_ssm_conv_w': out['v_ssm_conv_w'], 'v_ssm_conv_b': out['v_ssm_conv_b'], 'v_ssm_dt_bias': out['v_ssm_dt_bias'], 'v_ssm_a_log': out['v_ssm_a_log'], 'v_ssm_d': out['v_ssm_d'], 'v_ssm_norm_g': out['v_ssm_norm_g'], 'v_w_out': out['v_w_out'], 'v_g_ffn': out['v_g_ffn'], 'v_w_ffn_in': out['v_w_ffn_in'], 'v_w_ffn_out': out['v_w_ffn_out'], 'v_g_final': out['v_g_final']}


def _loss(weights, diff, rest, loss_target):
    with _jax.named_scope("forward"):
        args = {**rest, TWIN_DIFF_INPUT: diff, **{k: w.astype(_WEIGHT_DTYPES[k]) for k, w in weights.items()}}
        y = _forward(args)
    with _jax.named_scope("loss_head"):
        err = _jnp.square(y.astype(_jnp.float32) - loss_target)
        return 0.5 * _jnp.sum(_jnp.mean(err, axis=-1)) if err.ndim else 0.5 * err


def _adamw(w, g, m, v):
    m = ADAM_B1 * m + (1.0 - ADAM_B1) * g
    v = ADAM_B2 * v + (1.0 - ADAM_B2) * _jnp.square(g)
    m_hat = m / (1.0 - ADAM_B1 ** ADAM_STEP)
    v_hat = v / (1.0 - ADAM_B2 ** ADAM_STEP)
    delta = -ADAM_LR * (m_hat / (_jnp.sqrt(v_hat) + ADAM_EPS) + ADAM_WD * w)
    return delta, m, v


def reference(x, c, ctx, c_ctx, w_mod, b_mod, g_mix, w_in, wa_sink, na_rpb, ssm_conv_w, ssm_conv_b, ssm_dt_bias, ssm_a_log, ssm_d, ssm_norm_g, w_out, g_ffn, w_ffn_in, w_ffn_out, g_final, loss_target, m_c_ctx, m_w_mod, m_b_mod, m_g_mix, m_w_in, m_wa_sink, m_na_rpb, m_ssm_conv_w, m_ssm_conv_b, m_ssm_dt_bias, m_ssm_a_log, m_ssm_d, m_ssm_norm_g, m_w_out, m_g_ffn, m_w_ffn_in, m_w_ffn_out, m_g_final, v_c_ctx, v_w_mod, v_b_mod, v_g_mix, v_w_in, v_wa_sink, v_na_rpb, v_ssm_conv_w, v_ssm_conv_b, v_ssm_dt_bias, v_ssm_a_log, v_ssm_d, v_ssm_norm_g, v_w_out, v_g_ffn, v_w_ffn_in, v_w_ffn_out, v_g_final):
    given = dict(x=x, c=c, ctx=ctx, c_ctx=c_ctx, w_mod=w_mod, b_mod=b_mod, g_mix=g_mix, w_in=w_in, wa_sink=wa_sink, na_rpb=na_rpb, ssm_conv_w=ssm_conv_w, ssm_conv_b=ssm_conv_b, ssm_dt_bias=ssm_dt_bias, ssm_a_log=ssm_a_log, ssm_d=ssm_d, ssm_norm_g=ssm_norm_g, w_out=w_out, g_ffn=g_ffn, w_ffn_in=w_ffn_in, w_ffn_out=w_ffn_out, g_final=g_final, loss_target=loss_target, m_c_ctx=m_c_ctx, m_w_mod=m_w_mod, m_b_mod=m_b_mod, m_g_mix=m_g_mix, m_w_in=m_w_in, m_wa_sink=m_wa_sink, m_na_rpb=m_na_rpb, m_ssm_conv_w=m_ssm_conv_w, m_ssm_conv_b=m_ssm_conv_b, m_ssm_dt_bias=m_ssm_dt_bias, m_ssm_a_log=m_ssm_a_log, m_ssm_d=m_ssm_d, m_ssm_norm_g=m_ssm_norm_g, m_w_out=m_w_out, m_g_ffn=m_g_ffn, m_w_ffn_in=m_w_ffn_in, m_w_ffn_out=m_w_ffn_out, m_g_final=m_g_final, v_c_ctx=v_c_ctx, v_w_mod=v_w_mod, v_b_mod=v_b_mod, v_g_mix=v_g_mix, v_w_in=v_w_in, v_wa_sink=v_wa_sink, v_na_rpb=v_na_rpb, v_ssm_conv_w=v_ssm_conv_w, v_ssm_conv_b=v_ssm_conv_b, v_ssm_dt_bias=v_ssm_dt_bias, v_ssm_a_log=v_ssm_a_log, v_ssm_d=v_ssm_d, v_ssm_norm_g=v_ssm_norm_g, v_w_out=v_w_out, v_g_ffn=v_g_ffn, v_w_ffn_in=v_w_ffn_in, v_w_ffn_out=v_w_ffn_out, v_g_final=v_g_final)
    weights = {n: given[n] for n in TWIN_WEIGHTS}
    shared = {n: given[n] for n in SHARED_INPUTS}
    per_example = {n: given[n] for n in ['x', 'c', 'ctx']}
    grad_fn = _jax.value_and_grad(_loss, argnums=(0, 1))

    def one_microbatch(ex, loss_target):
        ex = dict(ex)
        diff = ex.pop(TWIN_DIFF_INPUT)
        return grad_fn(weights, diff, {**shared, **ex}, loss_target)

    if N_MICROBATCH == 1:
        loss, (grad_w, grad_x) = one_microbatch(per_example, given["loss_target"])
    else:
        def body(carry, xs):
            loss_sum, grad_sum = carry
            l_k, (gw_k, gx_k) = one_microbatch(xs[0], xs[1])
            with _jax.named_scope("update"):
                return (loss_sum + l_k, _jax.tree.map(_jnp.add, grad_sum, gw_k)), gx_k

        init = (_jnp.zeros((), _jnp.float32), _jax.tree.map(_jnp.zeros_like, weights))
        (loss, grad_w), grad_x = _jax.lax.scan(body, init, (per_example, given["loss_target"]))
    with _jax.named_scope("update"):
        delta_w, new_m, new_v = {}, {}, {}
        for n in TWIN_WEIGHTS:
            delta_w[n], new_m[n], new_v[n] = _adamw(weights[n], grad_w[n], given["m_" + n], given["v_" + n])
    return (loss, grad_x, *[grad_w[n] for n in TWIN_WEIGHTS], *[delta_w[n] for n in TWIN_WEIGHTS],
            *[new_m[n] for n in TWIN_WEIGHTS], *[new_v[n] for n in TWIN_WEIGHTS])
```

```python
import functools
import math

import numpy as np
import jax
import jax.numpy as jnp
from jax import lax
from jax.experimental import pallas as pl
from jax.experimental.pallas import tpu as pltpu

F32 = jnp.float32
MXU = jnp.bfloat16
_INTERPRET = False
VMEM_LIMIT = 60 * 1024 * 1024

D = 1024
LC = 256
GW = 64
HD = 64
EPS = 1e-6
NEG = -1e30
NDEV = 8
Q = 128
NSTATE = 128
DFF = 2816
IN_COLS = 2832
NP_IN = 3072
C_QA, C_QB, C_Z, C_KA, C_VA, C_KB, C_VB, C_XBC, C_DT = 0, 256, 512, 1024, 1152, 1280, 1536, 1792, 2816
ADAM_LR, ADAM_B1, ADAM_B2, ADAM_EPS, ADAM_WD, ADAM_STEP = 0.001, 0.9, 0.999, 1e-08, 0.01, 10
MESH_T = pl.DeviceIdType.MESH


def _dg(a, b, ca, cb):
    return lax.dot_general(a.astype(MXU), b.astype(MXU), (((ca,), (cb,)), ((), ())), preferred_element_type=F32)


@jax.custom_vjp
def mm(a, b):
    return _dg(a, b, 1, 0)


def _mm_f(a, b):
    return _dg(a, b, 1, 0), (a, b)


def _mm_b(res, g):
    a, b = res
    return _dg(g, b, 1, 1).astype(a.dtype), _dg(a, g, 0, 0).astype(b.dtype)


mm.defvjp(_mm_f, _mm_b)


@jax.custom_vjp
def mm_nt(a, b):
    return _dg(a, b, 1, 1)


def _mmnt_f(a, b):
    return _dg(a, b, 1, 1), (a, b)


def _mmnt_b(res, g):
    a, b = res
    return _dg(g, b, 1, 0).astype(a.dtype), _dg(g, a, 0, 0).astype(b.dtype)


mm_nt.defvjp(_mmnt_f, _mmnt_b)


@jax.custom_vjp
def mm_tn(a, b):
    return _dg(a, b, 0, 0)


def _mmtn_f(a, b):
    return _dg(a, b, 0, 0), (a, b)


def _mmtn_b(res, g):
    a, b = res
    return _dg(b, g, 1, 1).astype(a.dtype), _dg(a, g, 1, 0).astype(b.dtype)


mm_tn.defvjp(_mmtn_f, _mmtn_b)


def _exact(a, b):
    return lax.dot_general(a, b, (((1,), (0,)), ((), ())), precision=lax.Precision.HIGHEST,
                           preferred_element_type=F32)


def _pc(body, name, out_shape, grid=None, in_specs=None, out_specs=None, scratch=()):
    kw = {}
    if grid is not None:
        kw = dict(grid=grid, in_specs=in_specs, out_specs=out_specs)
    elif in_specs is not None:
        kw = dict(in_specs=in_specs, out_specs=out_specs)
    return pl.pallas_call(body, name=name, out_shape=out_shape, scratch_shapes=list(scratch),
                          compiler_params=pltpu.CompilerParams(vmem_limit_bytes=VMEM_LIMIT),
                          interpret=_INTERPRET, **kw)


def _vm():
    return pl.BlockSpec(memory_space=pltpu.VMEM)


def _sds(shape, dt=F32):
    return jax.ShapeDtypeStruct(shape, dt)


def _iota(shape, dim):
    return lax.broadcasted_iota(jnp.int32, shape, dim)


def _silu(x):
    return x * jax.nn.sigmoid(x)


def _softplus(x):
    return jnp.maximum(x, 0.0) + jnp.log1p(jnp.exp(-jnp.abs(x)))


def _normmod(x, g, sh, sc):
    r = lax.rsqrt(jnp.mean(x * x, axis=-1, keepdims=True) + EPS)
    return (x * r * g) * (1.0 + sc) + sh


def _rope(x, cos, sin, rm):
    return x * cos + _exact(x, rm) * sin


def _acc_init(first, refs):
    @pl.when(first)
    def _():
        for r in refs:
            r[...] = jnp.zeros_like(r)


def in_fwd(X, g, sh, sc, W, cos, sin, rm, L):
    T = X.shape[0]
    TR = 256
    nlt = L // TR

    def body(x_ref, g_ref, sh_ref, sc_ref, w_ref, cos_ref, sin_ref, rm_ref,
             qa, qb, z, ka, va, kb, vb, xbc, dt, hout):
        h = _normmod(x_ref[...], g_ref[...], sh_ref[0], sc_ref[0]).astype(MXU)
        hout[...] = h
        y = jnp.dot(h, w_ref[...], preferred_element_type=F32)
        cs, sn, r = cos_ref[...], sin_ref[...], rm_ref[...]
        qa[...] = _rope(y[:, C_QA:C_QB], cs, sn, r).astype(MXU)
        qb[...] = y[:, C_QB:C_Z].astype(MXU)
        z[...] = y[:, C_Z:C_KA]
        ka[...] = _rope(y[:, C_KA:C_VA], cs[:, :128], sn[:, :128], r[:128, :128]).astype(MXU)
        va[...] = y[:, C_VA:C_KB].astype(MXU)
        kb[...] = y[:, C_KB:C_VB].astype(MXU)
        vb[...] = y[:, C_VB:C_XBC].astype(MXU)
        xbc[...] = y[:, C_XBC:C_DT]
        dt[...] = y[:, C_DT:C_DT + 128]

    row = lambda w: pl.BlockSpec((TR, w), lambda i: (i, 0))
    cls = pl.BlockSpec((1, 1, D), lambda i: (i // nlt, 0, 0))
    widths = [(256, MXU), (256, MXU), (512, F32), (128, MXU), (128, MXU), (256, MXU), (256, MXU), (1024, F32),
              (128, F32), (D, MXU)]
    return _pc(body, "in_fwd", [_sds((T, w), d) for w, d in widths], grid=(T // TR,),
               in_specs=[row(D), pl.BlockSpec((1, D), lambda i: (0, 0)), cls, cls, _vm(), row(256), row(256), _vm()],
               out_specs=[row(w) for w, _ in widths])(X, g, sh, sc, W, cos, sin, rm)


def in_bwd(X, g, sh, sc, W, cos, sin, rm, dxres, dqa, dqb, dz, dka, dva, dkb, dvb, dxbc, ddt2, L):
    T = X.shape[0]
    TR = 256
    nlt = L // TR

    def body(x_ref, g_ref, sh_ref, sc_ref, w_ref, cos_ref, sin_ref, rm_ref, dxres_ref, dqa_r, dqb_r, dz_r, dka_r,
             dva_r, dkb_r, dvb_r, dxbc_r, ddt0_r, ddt1_r, dx_o, dy_o, dg_o, dsh_o, dsc_o):
        i = pl.program_id(0)
        cs, sn, r = cos_ref[...], sin_ref[...], rm_ref[...]
        _, vq = jax.vjp(lambda t: _rope(t, cs, sn, r), dqa_r[...])
        _, vk = jax.vjp(lambda t: _rope(t, cs[:, :128], sn[:, :128], r[:128, :128]), dka_r[...])
        dyqa, = vq(dqa_r[...])
        dyka, = vk(dka_r[...])
        ddt = ddt0_r[0] + ddt1_r[0]
        dy = jnp.concatenate([dyqa, dqb_r[...], dz_r[...], dyka, dva_r[...], dkb_r[...], dvb_r[...], dxbc_r[...],
                              ddt, jnp.zeros((TR, NP_IN - C_DT - 128), F32)], axis=1).astype(MXU)
        dy_o[...] = dy
        dh = lax.dot_general(dy, w_ref[...], (((1,), (1,)), ((), ())), preferred_element_type=F32)
        _, vp = jax.vjp(_normmod, x_ref[...], g_ref[...], sh_ref[0], sc_ref[0])
        dx, dg, dsh, dsc = vp(dh)
        dx_o[...] = dx + dxres_ref[...]
        _acc_init(i == 0, [dg_o])
        _acc_init((i == 0) | (i == nlt), [dsh_o, dsc_o])
        dg_o[...] += dg
        dsh_o[0] += dsh
        dsc_o[0] += dsc

    row = lambda w: pl.BlockSpec((TR, w), lambda i: (i, 0))
    cls = pl.BlockSpec((1, 1, D), lambda i: (i // nlt, 0, 0))
    vec = pl.BlockSpec((1, D), lambda i: (0, 0))
    dts = lambda d: pl.BlockSpec((1, TR, 128), lambda i: (d, i, 0))
    return _pc(body, "in_bwd",
               [_sds((T, D)), _sds((T, NP_IN), MXU), _sds((1, D)), _sds((2, 1, D)), _sds((2, 1, D))],
               grid=(T // TR,),
               in_specs=[row(D), vec, cls, cls, _vm(), row(256), row(256), _vm(), row(D), row(256), row(256), row(512),
                         row(128), row(128), row(256), row(256), row(1024), dts(0), dts(1)],
               out_specs=[row(D), row(NP_IN), vec, cls, cls])(
        X, g, sh, sc, W, cos, sin, rm, dxres, dqa, dqb, dz, dka, dva, dkb, dvb, dxbc, ddt2, ddt2)


def tn_mm(A, G, bn, out_dtype):
    T, K = A.shape
    N = G.shape[1]
    bt = T // 4
    nt = T // bt

    def body(a_ref, g_ref, o_ref, acc):
        t = pl.program_id(1)
        _acc_init(t == 0, [acc])
        acc[...] += lax.dot_general(a_ref[...], g_ref[...], (((0,), (0,)), ((), ())), preferred_element_type=F32)

        @pl.when(t == nt - 1)
        def _():
            o_ref[...] = acc[...].astype(out_dtype)

    return _pc(body, "tn_mm", _sds((K, N), out_dtype), grid=(N // bn, nt),
               in_specs=[pl.BlockSpec((bt, K), lambda n, t: (t, 0)), pl.BlockSpec((bt, bn), lambda n, t: (t, n))],
               out_specs=pl.BlockSpec((K, bn), lambda n, t: (0, n)),
               scratch=[pltpu.VMEM((K, bn), F32)])(A, G)


def _ssm_out(yf, yb, xs, z, dsk, gs):
    y = (yf + yb + dsk * xs) * _silu(z)
    r = lax.rsqrt(jnp.mean(y * y, axis=-1, keepdims=True) + EPS)
    return y * r * gs


def out_fwd(oa, ob, y2, act, z, dsk, gs, W, X, gate, L):
    T = X.shape[0]
    TR = 256
    nlt = L // TR

    def body(oa_r, ob_r, yf_r, yb_r, xs_r, z_r, dsk_r, gs_r, w_ref, x_ref, gt_ref, x1_o, cat_o):
        oc = _ssm_out(yf_r[0], yb_r[0], xs_r[...], z_r[...], dsk_r[...], gs_r[...])
        cat = jnp.concatenate([oa_r[...], ob_r[...], oc], axis=1).astype(MXU)
        cat_o[...] = cat
        x1_o[...] = x_ref[...] + gt_ref[0] * jnp.dot(cat, w_ref[...], preferred_element_type=F32)

    row = lambda w: pl.BlockSpec((TR, w), lambda i: (i, 0))
    ys = lambda d: pl.BlockSpec((1, TR, 512), lambda i: (d, i, 0))
    cls = pl.BlockSpec((1, 1, D), lambda i: (i // nlt, 0, 0))
    v512 = pl.BlockSpec((1, 512), lambda i: (0, 0))
    return _pc(body, "out_fwd", [_sds((T, D)), _sds((T, D), MXU)], grid=(T // TR,),
               in_specs=[row(256), row(256), ys(0), ys(1), row(512), row(512), v512, v512, _vm(), row(D), cls],
               out_specs=[row(D), row(D)])(oa, ob, y2, y2, act, z, dsk, gs, W, X, gate)


def out_bwd(oa, ob, y2, act, z, dsk, gs, W, gate, dX1, L):
    T = dX1.shape[0]
    TR = 256
    nlt = L // TR

    def body(oa_r, ob_r, yf_r, yb_r, xs_r, z_r, dsk_r, gs_r, w_ref, gt_ref, dx1_r,
             doa_o, dob_o, dy_o, dxs_o, dz_o, dmix_o, ddsk_o, dgs_o, dgt_o):
        i = pl.program_id(0)
        w = w_ref[...]

        def f(oa_, ob_, yf, yb, xs, z_, dsk_, gs_, gt):
            oc = _ssm_out(yf, yb, xs, z_, dsk_, gs_)
            return gt * mm(jnp.concatenate([oa_, ob_, oc], axis=1), w)

        _, vjp = jax.vjp(f, oa_r[...], ob_r[...], yf_r[0], yb_r[0], xs_r[...], z_r[...], dsk_r[...], gs_r[...],
                         gt_ref[0])
        dx1 = dx1_r[...]
        doa, dob, dyf, _, dxs, dz, ddsk, dgs, dgt = vjp(dx1)
        doa_o[...] = doa
        dob_o[...] = dob
        dy_o[...] = dyf
        dxs_o[...] = dxs
        dz_o[...] = dz
        dmix_o[...] = (gt_ref[0] * dx1).astype(MXU)
        _acc_init(i == 0, [ddsk_o, dgs_o])
        _acc_init((i == 0) | (i == nlt), [dgt_o])
        ddsk_o[...] += ddsk
        dgs_o[...] += dgs
        dgt_o[0] += dgt

    row = lambda w: pl.BlockSpec((TR, w), lambda i: (i, 0))
    ys = lambda d: pl.BlockSpec((1, TR, 512), lambda i: (d, i, 0))
    cls = pl.BlockSpec((1, 1, D), lambda i: (i // nlt, 0, 0))
    v512 = pl.BlockSpec((1, 512), lambda i: (0, 0))
    return _pc(body, "out_bwd",
               [_sds((T, 256)), _sds((T, 256)), _sds((T, 512)), _sds((T, 512)), _sds((T, 512)), _sds((T, D), MXU),
                _sds((1, 512)), _sds((1, 512)), _sds((2, 1, D))],
               grid=(T // TR,),
               in_specs=[row(256), row(256), ys(0), ys(1), row(512), row(512), v512, v512, _vm(), cls, row(D)],
               out_specs=[row(256), row(256), row(512), row(512), row(512), row(D), v512, v512, cls])(
        oa, ob, y2, y2, act, z, dsk, gs, W, gate, dX1)


def _ffn_core(x, g, sh, sc, gt, wg, wu, wo, eg, eu):
    h = _normmod(x, g, sh, sc)
    a = mm(h, wg) + eg
    u = mm(h, wu) + eu
    act = _silu(a) * u
    return x + gt * mm(act, wo), (h, act)


def ffn_fwd(X, g, sh, sc, gate, Win, Wout, L):
    T = X.shape[0]
    TR = 256
    nlt = L // TR

    def body(x_ref, g_ref, sh_ref, sc_ref, gt_ref, wi_ref, wo_ref, o_ref):
        h = _normmod(x_ref[...], g_ref[...], sh_ref[0], sc_ref[0]).astype(MXU)
        a = jnp.dot(h, wi_ref[:, 0:DFF], preferred_element_type=F32)
        u = jnp.dot(h, wi_ref[:, DFF:2 * DFF], preferred_element_type=F32)
        act = (_silu(a) * u).astype(MXU)
        o_ref[...] = x_ref[...] + gt_ref[0] * jnp.dot(act, wo_ref[...], preferred_element_type=F32)

    row = lambda w: pl.BlockSpec((TR, w), lambda i: (i, 0))
    cls = pl.BlockSpec((1, 1, D), lambda i: (i // nlt, 0, 0))
    vec = pl.BlockSpec((1, D), lambda i: (0, 0))
    return _pc(body, "ffn_fwd", _sds((T, D)), grid=(T // TR,),
               in_specs=[row(D), vec, cls, cls, cls, _vm(), _vm()], out_specs=row(D))(X, g, sh, sc, gate, Win, Wout)


def ffn_bwd(X, g, sh, sc, gate, Win, Wout, dX2, L):
    T = X.shape[0]
    TR = 128
    nlt = L // TR

    def body(x_ref, g_ref, sh_ref, sc_ref, gt_ref, wi_ref, wo_ref, dx2_r,
             dx_o, h_o, du_o, act_o, dout_o, dg_o, dsh_o, dsc_o, dgt_o):
        i = pl.program_id(0)
        wg, wu, wo = wi_ref[:, 0:DFF], wi_ref[:, DFF:2 * DFF], wo_ref[...]
        zero = jnp.zeros((TR, DFF), F32)
        f = lambda x, g_, sh_, sc_, gt, eg, eu: _ffn_core(x, g_, sh_, sc_, gt, wg, wu, wo, eg, eu)
        _, vjp, (h, act) = jax.vjp(f, x_ref[...], g_ref[...], sh_ref[0], sc_ref[0], gt_ref[0], zero, zero,
                                   has_aux=True)
        dx2 = dx2_r[...]
        dx, dg, dsh, dsc, dgt, da, du = vjp(dx2)
        dx_o[...] = dx
        h_o[...] = h.astype(MXU)
        du_o[...] = jnp.concatenate([da, du], axis=1).astype(MXU)
        act_o[...] = act.astype(MXU)
        dout_o[...] = (gt_ref[0] * dx2).astype(MXU)
        _acc_init(i == 0, [dg_o])
        _acc_init((i == 0) | (i == nlt), [dsh_o, dsc_o, dgt_o])
        dg_o[...] += dg
        dsh_o[0] += dsh
        dsc_o[0] += dsc
        dgt_o[0] += dgt

    row = lambda w: pl.BlockSpec((TR, w), lambda i: (i, 0))
    cls = pl.BlockSpec((1, 1, D), lambda i: (i // nlt, 0, 0))
    vec = pl.BlockSpec((1, D), lambda i: (0, 0))
    return _pc(body, "ffn_bwd",
               [_sds((T, D)), _sds((T, D), MXU), _sds((T, 2 * DFF), MXU), _sds((T, DFF), MXU), _sds((T, D), MXU),
                _sds((1, D)), _sds((2, 1, D)), _sds((2, 1, D)), _sds((2, 1, D))],
               grid=(T // TR,),
               in_specs=[row(D), vec, cls, cls, cls, _vm(), _vm(), row(D)],
               out_specs=[row(D), row(D), row(2 * DFF), row(DFF), row(D), vec, cls, cls, cls])(
        X, g, sh, sc, gate, Win, Wout, dX2)


def loss_head(X2, g, tgt, L):
    TR = 256

    def body(x_ref, g_ref, t_ref, loss_o, dx_o, dg_o):
        i = pl.program_id(0)

        def f(x, g_):
            y = x * lax.rsqrt(jnp.mean(x * x, axis=-1, keepdims=True) + EPS) * g_
            return 0.5 * jnp.sum(jnp.mean(jnp.square(y - t_ref[...]), axis=-1, keepdims=True), axis=0, keepdims=True)

        val, vjp = jax.vjp(f, x_ref[...], g_ref[...])
        dx, dg = vjp(jnp.ones((1, 1), F32))
        dx_o[...] = dx
        _acc_init(i == 0, [loss_o, dg_o])
        loss_o[...] += jnp.broadcast_to(val, (8, 128))
        dg_o[...] += dg

    row = pl.BlockSpec((TR, D), lambda i: (i, 0))
    vec = pl.BlockSpec((1, D), lambda i: (0, 0))
    return _pc(body, "loss_head", [_sds((8, 128)), _sds((L, D)), _sds((1, D))], grid=(L // TR,),
               in_specs=[row, vec, row], out_specs=[pl.BlockSpec((8, 128), lambda i: (0, 0)), row, vec])(X2, g, tgt)


def _softmax_av(q, ks, vs, biases, sink):
    lane = _iota(q.shape, 1)
    outs = []
    for u in range(2):
        mu = (lane < HD) if u == 0 else (lane >= HD)
        qu = jnp.where(mu, q, 0.0)
        ss = []
        for k, b in zip(ks, biases):
            s = mm_nt(qu, k) * (HD ** -0.5)
            ss.append(s if b is None else s + b[u])
        m = functools.reduce(jnp.maximum, [jnp.max(s, axis=1, keepdims=True) for s in ss])
        if sink is not None:
            m = jnp.maximum(m, sink[u])
        m = lax.stop_gradient(m)
        es = [jnp.exp(s - m) for s in ss]
        den = functools.reduce(lambda a, b_: a + b_, [jnp.sum(e, axis=1, keepdims=True) for e in es])
        if sink is not None:
            den = den + jnp.exp(sink[u] - m)
        inv = 1.0 / den
        outs.append(functools.reduce(lambda a, b_: a + b_, [mm(e * inv, v) for e, v in zip(es, vs)]))
    return jnp.where(lane < HD, outs[0], outs[1])


def _wa_block(q, kp, kc, kn, vp, vc, vn, kx, vx, sk, n, L):
    kb = jnp.concatenate([kp, kc, kn], axis=0)
    vb = jnp.concatenate([vp, vc, vn], axis=0)
    qpos = n * Q + _iota((Q, 3 * Q), 0)
    kpos = (n - 1) * Q + _iota((Q, 3 * Q), 1)
    valid = (jnp.abs(qpos - kpos) <= Q) & (kpos >= 0) & (kpos < L)
    bias = jnp.where(valid, 0.0, NEG)
    sink = [jnp.mean(sk[u], axis=1, keepdims=True) for u in range(2)]
    return _softmax_av(q, [kb, kx], [vb, vx], [(bias, bias), None], sink)


def _wa_specs(L):
    nb = L // Q
    qs = pl.BlockSpec((Q, 128), lambda p, n: (n, p))
    kprev = pl.BlockSpec((Q, 128), lambda p, n: (jnp.maximum(n - 1, 0), 0))
    kcur = pl.BlockSpec((Q, 128), lambda p, n: (n, 0))
    knext = pl.BlockSpec((Q, 128), lambda p, n: (jnp.minimum(n + 1, nb - 1), 0))
    kctx = pl.BlockSpec((LC, 128), lambda p, n: (L // LC, 0))
    sks = pl.BlockSpec((1, 2, 1, 128), lambda p, n: (p, 0, 0, 0))
    return nb, qs, [kprev, kcur, knext], kctx, sks


def wa_fwd(QA, KA, VA, sinkp, L):
    nb, qs, kband, kctx, sks = _wa_specs(L)

    def body(q_r, kp, kc, kn, vp, vc, vn, kx, vx, sk_r, o_ref):
        n = pl.program_id(1)
        f = lambda t: t[...].astype(F32)
        o_ref[...] = _wa_block(f(q_r), f(kp), f(kc), f(kn), f(vp), f(vc), f(vn), f(kx), f(vx),
                               [sk_r[0, 0], sk_r[0, 1]], n, L)

    return _pc(body, "wa_fwd", _sds((L, 256)), grid=(2, nb),
               in_specs=[qs] + kband + kband + [kctx, kctx, sks], out_specs=qs)(
        QA, KA, KA, KA, VA, VA, VA, KA, VA, sinkp)


def wa_bwd(QA, KA, VA, sinkp, dO, L):
    nb, qs, kband, kctx, sks = _wa_specs(L)

    def body(q_r, kp, kc, kn, vp, vc, vn, kx, vx, sk_r, do_r, dq_o, dk_o, dv_o, dkx_o, dvx_o, dsk_o):
        p, n = pl.program_id(0), pl.program_id(1)
        f = lambda t: t[...].astype(F32)
        fn = lambda q, a, b, c, d, e, g, kx_, vx_, s0, s1: _wa_block(q, a, b, c, d, e, g, kx_, vx_, [s0, s1], n, L)
        _, vjp = jax.vjp(fn, f(q_r), f(kp), f(kc), f(kn), f(vp), f(vc), f(vn), f(kx), f(vx), sk_r[0, 0], sk_r[0, 1])
        dq, dkp, dkc, dkn, dvp, dvc, dvn, dkx, dvx, ds0, ds1 = vjp(do_r[...])
        dq_o[...] = dq
        _acc_init((p == 0) & (n == 0), [dk_o, dv_o, dkx_o, dvx_o])
        _acc_init(n == 0, [dsk_o])
        rows = pl.ds(pl.multiple_of(n * Q, Q), 3 * Q)
        dk_o[rows, :] += jnp.concatenate([dkp, dkc, dkn], axis=0)
        dv_o[rows, :] += jnp.concatenate([dvp, dvc, dvn], axis=0)
        dkx_o[...] += dkx
        dvx_o[...] += dvx
        dsk_o[0, 0] += ds0
        dsk_o[0, 1] += ds1

    full = lambda r: pl.BlockSpec((r, 128), lambda p, n: (0, 0))
    return _pc(body, "wa_bwd",
               [_sds((L, 256)), _sds((L + 2 * Q, 128)), _sds((L + 2 * Q, 128)), _sds((LC, 128)), _sds((LC, 128)),
                _sds((2, 2, 1, 128))],
               grid=(2, nb), in_specs=[qs] + kband + kband + [kctx, kctx, sks, qs],
               out_specs=[qs, full(L + 2 * Q), full(L + 2 * Q), full(LC), full(LC), sks])(
        QA, KA, KA, KA, VA, VA, VA, KA, VA, sinkp, dO)


def _ctx_block(q, kx, vx, s0, s1):
    sink = [jnp.mean(s0, axis=1, keepdims=True), jnp.mean(s1, axis=1, keepdims=True)]
    return _softmax_av(q, [kx], [vx], [None], sink)


def ctx_fwd(Qx, Kx, Vx, sinkp, shared, L):
    cq = pl.BlockSpec((LC, 128), lambda p: (L // LC, p))
    ck = pl.BlockSpec((LC, 128), lambda p: (L // LC, 0 if shared else p))
    sks = pl.BlockSpec((1, 2, 1, 128), lambda p: (p, 0, 0, 0))

    def body(q_r, k_r, v_r, sk_r, o_ref):
        f = lambda t: t[...].astype(F32)
        o_ref[...] = _ctx_block(f(q_r), f(k_r), f(v_r), sk_r[0, 0], sk_r[0, 1])

    return _pc(body, "ctx_fwd", _sds((LC, 256)), grid=(2,), in_specs=[cq, ck, ck, sks],
               out_specs=pl.BlockSpec((LC, 128), lambda p: (0, p)))(Qx, Kx, Vx, sinkp)


def ctx_bwd(Qx, Kx, Vx, sinkp, dO, shared, L):
    cq = pl.BlockSpec((LC, 128), lambda p: (L // LC, p))
    ck = pl.BlockSpec((LC, 128), lambda p: (L // LC, 0 if shared else p))
    sks = pl.BlockSpec((1, 2, 1, 128), lambda p: (p, 0, 0, 0))
    op = pl.BlockSpec((LC, 128), lambda p: (0, p))
    ok = pl.BlockSpec((LC, 128), lambda p: (0, 0 if shared else p))
    dos = pl.BlockSpec((LC, 128), lambda p: (L // LC, p))

    def body(q_r, k_r, v_r, sk_r, do_r, dq_o, dk_o, dv_o, dsk_o):
        p = pl.program_id(0)
        f = lambda t: t[...].astype(F32)
        _, vjp = jax.vjp(_ctx_block, f(q_r), f(k_r), f(v_r), sk_r[0, 0], sk_r[0, 1])
        dq, dk, dv, ds0, ds1 = vjp(do_r[...])
        dq_o[...] = dq
        _acc_init((p == 0) if shared else (p >= 0), [dk_o, dv_o])
        dk_o[...] += dk
        dv_o[...] += dv
        dsk_o[0, 0] = ds0
        dsk_o[0, 1] = ds1

    kw = 128 if shared else 256
    return _pc(body, "ctx_bwd", [_sds((LC, 256)), _sds((LC, kw)), _sds((LC, kw)), _sds((2, 2, 1, 128))],
               grid=(2,), in_specs=[cq, ck, ck, sks, dos], out_specs=[op, ok, ok, sks])(Qx, Kx, Vx, sinkp, dO)


def _na_row(q, kw, vw, kx, vx, b0, b1):
    return _softmax_av(q, [kw, kx], [vw, vx], [(b0, b1), None], None)


def _na_geom(rb, j, R):
    r = rb * 8 + j
    s = jnp.clip(r - 4, 0, R - 8)
    cls = jnp.where(r < 4, r, jnp.where(r > R - 4, r - (R - 8), 4))
    return pl.ds(pl.multiple_of(s * GW, GW), 8 * GW), cls


def na_fwd(QB, KB, VB, biasd, L):
    R = L // GW
    qs = pl.BlockSpec((8 * GW, 128), lambda p, rb: (rb, p))
    kfull = pl.BlockSpec((L, 128), lambda p, rb: (0, p))
    kctx = pl.BlockSpec((LC, 128), lambda p, rb: (L // LC, p))
    bs = pl.BlockSpec((2, 8, GW, 8 * GW), lambda p, rb: (p, 0, 0, 0))

    def body(q_r, k_r, v_r, kx_r, vx_r, b_r, o_ref):
        rb = pl.program_id(1)
        kx, vx = kx_r[...].astype(F32), vx_r[...].astype(F32)
        for j in range(8):
            win, cls = _na_geom(rb, j, R)
            o_ref[j * GW:(j + 1) * GW, :] = _na_row(
                q_r[j * GW:(j + 1) * GW, :].astype(F32), k_r[win, :].astype(F32), v_r[win, :].astype(F32), kx, vx,
                b_r[0, cls], b_r[1, cls])

    return _pc(body, "na_fwd", _sds((L, 256)), grid=(2, R // 8), in_specs=[qs, kfull, kfull, kctx, kctx, bs],
               out_specs=qs)(QB, KB, VB, KB, VB, biasd)


def na_bwd(QB, KB, VB, biasd, dO, L):
    R = L // GW
    qs = pl.BlockSpec((8 * GW, 128), lambda p, rb: (rb, p))
    kfull = pl.BlockSpec((L, 128), lambda p, rb: (0, p))
    kctx = pl.BlockSpec((LC, 128), lambda p, rb: (L // LC, p))
    bs = pl.BlockSpec((2, 8, GW, 8 * GW), lambda p, rb: (p, 0, 0, 0))
    oc = pl.BlockSpec((LC, 128), lambda p, rb: (0, p))

    def body(q_r, k_r, v_r, kx_r, vx_r, b_r, do_r, dq_o, dk_o, dv_o, dkx_o, dvx_o, db_o):
        rb = pl.program_id(1)
        _acc_init(rb == 0, [dk_o, dv_o, dkx_o, dvx_o, db_o])
        kx, vx = kx_r[...].astype(F32), vx_r[...].astype(F32)
        for j in range(8):
            win, cls = _na_geom(rb, j, R)
            rows = slice(j * GW, (j + 1) * GW)
            _, vjp = jax.vjp(_na_row, q_r[rows, :].astype(F32), k_r[win, :].astype(F32), v_r[win, :].astype(F32),
                             kx, vx, b_r[0, cls], b_r[1, cls])
            dq, dkw, dvw, dkx, dvx, db0, db1 = vjp(do_r[rows, :])
            dq_o[rows, :] = dq
            dk_o[win, :] += dkw
            dv_o[win, :] += dvw
            dkx_o[...] += dkx
            dvx_o[...] += dvx
            db_o[0, cls] += db0
            db_o[1, cls] += db1

    return _pc(body, "na_bwd",
               [_sds((L, 256)), _sds((L, 256)), _sds((L, 256)), _sds((LC, 256)), _sds((LC, 256)),
                _sds((4, 8, GW, 8 * GW))],
               grid=(2, R // 8), in_specs=[qs, kfull, kfull, kctx, kctx, bs, qs],
               out_specs=[qs, kfull, kfull, oc, oc, bs])(QB, KB, VB, KB, VB, biasd, dO)


def exact_mm_call(A, B):
    def body(a_ref, b_ref, o_ref):
        o_ref[...] = _exact(a_ref[...], b_ref[...])

    return _pc(body, "exact_mm", _sds((A.shape[0], B.shape[1])))(A, B)


def _conv_shift(x, d, L):
    T = x.shape[0]
    if d == 0:
        return x
    t = _iota(x.shape, 0)
    src = t + d
    ok = (src >= 0) & (src < T) & ((src >= L) == (t >= L))
    return jnp.where(ok, pltpu.roll(x, (-d) % T, 0), 0.0)


def conv_fwd(XBC, w8, b, L):
    T = XBC.shape[0]

    def body(x_ref, w_ref, b_ref, o_ref):
        x = x_ref[...]
        pre = b_ref[...] + functools.reduce(
            lambda a, c: a + c, [_conv_shift(x, k - 3, L) * w_ref[k:k + 1, :] for k in range(7)])
        o_ref[...] = _silu(pre)

    col = pl.BlockSpec((T, 128), lambda j: (0, j))
    return _pc(body, "conv_fwd", _sds((T, 1024)), grid=(8,),
               in_specs=[col, pl.BlockSpec((8, 128), lambda j: (0, j)), pl.BlockSpec((1, 128), lambda j: (0, j))],
               out_specs=col)(XBC, w8, b)


def conv_bwd(XBC, w8, b, dS, dxs_skip, L):
    T = XBC.shape[0]

    def body(x_ref, w_ref, b_ref, d0_r, d1_r, dsk_r, dx_o, dw_o, db_o):
        j = pl.program_id(0)
        x = x_ref[...]
        xs = [_conv_shift(x, k - 3, L) for k in range(7)]
        pre = b_ref[...] + functools.reduce(lambda a, c: a + c, [xs[k] * w_ref[k:k + 1, :] for k in range(7)])
        _, vjp = jax.vjp(_silu, pre)
        dact = d0_r[0] + d1_r[0] + jnp.where(j < 4, dsk_r[...], 0.0)
        dpre, = vjp(dact)
        dx_o[...] = functools.reduce(
            lambda a, c: a + c, [_conv_shift(dpre, 3 - k, L) * w_ref[k:k + 1, :] for k in range(7)])
        dw_o[...] = jnp.concatenate([jnp.sum(dpre * xs[k], axis=0, keepdims=True) for k in range(7)]
                                    + [jnp.zeros((1, 128), F32)], axis=0)
        db_o[...] = jnp.sum(dpre, axis=0, keepdims=True)

    col = pl.BlockSpec((T, 128), lambda j: (0, j))
    w_s = pl.BlockSpec((8, 128), lambda j: (0, j))
    b_s = pl.BlockSpec((1, 128), lambda j: (0, j))
    ds = lambda d: pl.BlockSpec((1, T, 128), lambda j: (d, 0, j))
    return _pc(body, "conv_bwd", [_sds((T, 1024)), _sds((8, 1024)), _sds((1, 1024))], grid=(8,),
               in_specs=[col, w_s, b_s, ds(0), ds(1), pl.BlockSpec((T, 128), lambda j: (0, jnp.minimum(j, 3)))],
               out_specs=[col, w_s, b_s])(XBC, w8, b, dS, dS, dxs_skip)


def _ssd_chunk(xs, bs, cs, dtraw, dtb, alog, hs, tri, d):
    dt = _softplus(dtraw + dtb)
    a = dt * (-jnp.exp(alog))
    acum = _exact(tri, a)
    tot = jnp.sum(a, axis=0, keepdims=True)
    wcol = jnp.exp(tot - acum) * dt
    ea = jnp.exp(acum)
    cd = jnp.exp(tot)
    acum_t, dt_t = acum.T, dt.T
    lane = _iota((Q, 128), 1)
    srow = _iota((128, Q), 0)
    lane1 = _iota((1, 128), 1)
    prow = _iota((128, NSTATE), 0)
    mask = tri > 0.5
    cbs = [mm_nt(cs[g], bs[g]) for g in range(2)]
    ys, hn = [], []
    for j in range(4):
        g = j // 2
        x = xs[j]
        yi, st, eac, cdl = [], [], [], []
        for u in range(2):
            slot = d * 8 + 2 * j + u
            col = lambda m: jnp.sum(jnp.where(lane == slot, m, 0.0), axis=1, keepdims=True)
            rowv = lambda m: jnp.sum(jnp.where(srow == slot, m, 0.0), axis=0, keepdims=True)
            seg = col(acum) - rowv(acum_t)
            dcy = jnp.where(mask, jnp.exp(jnp.where(mask, seg, 0.0)), 0.0)
            yi.append(mm(cbs[g] * dcy * rowv(dt_t), x))
            st.append(mm_tn(x, bs[g] * col(wcol)))
            eac.append(col(ea))
            cdl.append(jnp.sum(jnp.where(lane1 == slot, cd, 0.0), axis=1, keepdims=True))
        yin = mm_nt(cs[g], hs[j])
        ys.append(jnp.where(lane < HD, yi[0] + yin * eac[0], yi[1] + yin * eac[1]))
        hn.append(hs[j] * jnp.where(prow < HD, cdl[0], cdl[1]) + jnp.where(prow < HD, st[0], st[1]))
    return ys, hn


def _ssd_chunk_idx(d, s, nlc, nch):
    return jnp.where(d == 0, (s + nlc) % nch, nch - 1 - s)


def ssd_fwd(ACT, DT, dtb, alog, tri2, L):
    T = ACT.shape[0]
    nlc, nch = L // Q, T // Q

    def body(a_ref, dt_ref, dtb_ref, al_ref, tri_ref, y_o, hs_o, hst):
        d, s = pl.program_id(0), pl.program_id(1)
        _acc_init(s == 0, [hst])
        a = a_ref[...]
        xs = [a[:, 128 * j:128 * (j + 1)] for j in range(4)]
        bs = [a[:, 512 + 128 * g:640 + 128 * g] for g in range(2)]
        cs = [a[:, 768 + 128 * g:896 + 128 * g] for g in range(2)]
        hs = [hst[j] for j in range(4)]
        hs_o[0, 0] = hst[...]
        ys, hn = _ssd_chunk(xs, bs, cs, dt_ref[...], dtb_ref[...], al_ref[...], hs, tri_ref[0], d)
        y_o[0] = jnp.concatenate(ys, axis=1)
        for j in range(4):
            hst[j] = hn[j]

    ck = lambda w: pl.BlockSpec((Q, w), lambda d, s: (_ssd_chunk_idx(d, s, nlc, nch), 0))
    v128 = pl.BlockSpec((1, 128), lambda d, s: (0, 0))
    return _pc(body, "ssd_fwd", [_sds((2, T, 512)), _sds((2, nch, 4, 128, NSTATE))], grid=(2, nch),
               in_specs=[ck(1024), ck(128), v128, v128, pl.BlockSpec((1, Q, Q), lambda d, s: (d, 0, 0))],
               out_specs=[pl.BlockSpec((1, Q, 512), lambda d, s: (d, _ssd_chunk_idx(d, s, nlc, nch), 0)),
                          pl.BlockSpec((1, 1, 4, 128, NSTATE), lambda d, s: (d, s, 0, 0, 0))],
               scratch=[pltpu.VMEM((4, 128, NSTATE), F32)])(ACT, DT, dtb, alog, tri2)


def ssd_bwd(ACT, DT, dtb, alog, tri2, HS, dY, L):
    T = ACT.shape[0]
    nlc, nch = L // Q, T // Q

    def body(a_ref, dt_ref, dtb_ref, al_ref, tri_ref, hs_ref, dy_ref, da_o, ddt_o, ddtb_o, dal_o, dh):
        d, sr = pl.program_id(0), pl.program_id(1)
        _acc_init(sr == 0, [dh, ddtb_o, dal_o])
        a = a_ref[...]
        xs = [a[:, 128 * j:128 * (j + 1)] for j in range(4)]
        bs = [a[:, 512 + 128 * g:640 + 128 * g] for g in range(2)]
        cs = [a[:, 768 + 128 * g:896 + 128 * g] for g in range(2)]
        hs = [hs_ref[0, 0, j] for j in range(4)]
        tri = tri_ref[0]
        fn = lambda xs_, bs_, cs_, dtr, dtb_, al, hs_: _ssd_chunk(xs_, bs_, cs_, dtr, dtb_, al, hs_, tri, d)
        _, vjp = jax.vjp(fn, xs, bs, cs, dt_ref[...], dtb_ref[...], al_ref[...], hs)
        dy = dy_ref[...]
        dys = [dy[:, 128 * j:128 * (j + 1)] for j in range(4)]
        dxs, dbs, dcs, ddt, ddtb, dal, dhs = vjp((dys, [dh[j] for j in range(4)]))
        da_o[0] = jnp.concatenate(dxs + dbs + dcs, axis=1)
        ddt_o[0] = ddt
        ddtb_o[0] += ddtb
        dal_o[0] += dal
        for j in range(4):
            dh[j] = dhs[j]

    cidx = lambda d, sr: _ssd_chunk_idx(d, nch - 1 - sr, nlc, nch)
    ck = lambda w: pl.BlockSpec((Q, w), lambda d, sr: (cidx(d, sr), 0))
    v128 = pl.BlockSpec((1, 128), lambda d, sr: (0, 0))
    o128 = pl.BlockSpec((1, 1, 128), lambda d, sr: (d, 0, 0))
    return _pc(body, "ssd_bwd", [_sds((2, T, 1024)), _sds((2, T, 128)), _sds((2, 1, 128)), _sds((2, 1, 128))],
               grid=(2, nch),
               in_specs=[ck(1024), ck(128), v128, v128, pl.BlockSpec((1, Q, Q), lambda d, sr: (d, 0, 0)),
                         pl.BlockSpec((1, 1, 4, 128, NSTATE), lambda d, sr: (d, nch - 1 - sr, 0, 0, 0)), ck(512)],
               out_specs=[pl.BlockSpec((1, Q, 1024), lambda d, sr: (d, cidx(d, sr), 0)),
                          pl.BlockSpec((1, Q, 128), lambda d, sr: (d, cidx(d, sr), 0)), o128, o128],
               scratch=[pltpu.VMEM((4, 128, NSTATE), F32)])(ACT, DT, dtb, alog, tri2, HS, dY)


_QA_PERM = np.concatenate([np.arange(HD * h, HD * h + HD) for h in (0, 2, 1, 3)])
_PAIR_HEADS = np.array([[0, 2], [1, 3]])


def _tables(L):
    t = jnp.arange(L)
    inv = 10000.0 ** (-jnp.arange(16, dtype=F32) / 16)

    def half(pos):
        ang = pos.astype(F32)[:, None] * inv[None, :]
        return jnp.concatenate([ang, ang], axis=1)

    ang = jnp.tile(jnp.concatenate([half(t // GW), half(t % GW)], axis=1), (1, 4))
    cos = jnp.concatenate([jnp.cos(ang), jnp.ones((LC, 256), F32)], axis=0)
    sin = jnp.concatenate([jnp.sin(ang), jnp.zeros((LC, 256), F32)], axis=0)
    rm = np.zeros((256, 256), np.float32)
    for j in range(256):
        if j % 32 < 16:
            rm[j + 16, j] = -1.0
        else:
            rm[j - 16, j] = 1.0
    tri = np.tril(np.ones((Q, Q), np.float32))
    return cos, sin, jnp.asarray(rm), jnp.asarray(np.stack([tri, tri.T]))


def _na_index(R):
    rc = np.array([0, 1, 2, 3, 4, R - 3, R - 2, R - 1])
    dy = np.clip(rc - 4, 0, R - 8)[:, None] + np.arange(8)[None, :] - rc[:, None] + 7
    qc, cc = np.arange(GW)[:, None], np.arange(GW)[None, :]
    dx = np.clip(cc - qc, -15, 15) + 15
    cstart = np.clip(qc - 8, 0, GW - 16)
    cmask = (cc >= cstart) & (cc < cstart + 16)
    idx = dy[:, None, :, None] * 31 + dx[None, :, None, :]
    return idx.reshape(8, GW, 8 * GW), np.broadcast_to(cmask[None, :, None, :], idx.shape).reshape(8, GW, 8 * GW), \
        dy, dx, cmask


def _na_bias(rpb, R):
    idx, cm, _, _, _ = _na_index(R)
    return jnp.where(cm[None], rpb.reshape(4, 15 * 31)[:, idx], NEG)


def _na_bias_grad(dbias, R):
    _, _, dy, dx, cmask = _na_index(R)
    e1 = np.zeros((GW * GW, 128), np.float32)
    e1[np.arange(GW * GW), dx.reshape(-1)] = cmask.reshape(-1)
    a1 = dbias.reshape(4, 8, GW, 8, GW).transpose(0, 1, 3, 2, 4).reshape(256, GW * GW)
    v = exact_mm_call(a1, jnp.asarray(e1))[:, :31].reshape(4, 64, 31)
    e2 = np.zeros((64, 128), np.float32)
    e2[np.arange(64), dy.reshape(-1)] = 1.0
    a2 = jnp.pad(v.transpose(0, 2, 1).reshape(124, 64), ((0, 4), (0, 0)))
    return exact_mm_call(a2, jnp.asarray(e2))[:124, :15].reshape(4, 31, 15).transpose(0, 2, 1)


def _lanes(v, n=128):
    v = v.reshape(1, -1)
    return jnp.pad(v, ((0, 0), (0, n - v.shape[1])))


def _cls2(a, b):
    return jnp.stack([a, b]).reshape(2, 1, D)


def _layer_consts(p):
    w_in = p["w_in"]
    win_p = jnp.concatenate([w_in[:, _QA_PERM], w_in[:, 256:], jnp.zeros((D, NP_IN - IN_COLS), w_in.dtype)], axis=1)
    wout_p = jnp.concatenate([p["w_out"][_QA_PERM, :], p["w_out"][256:, :]], axis=0)
    sinkp = jnp.broadcast_to(p["wa_sink"][_PAIR_HEADS][:, :, None, None], (2, 2, 1, 128))
    return dict(
        win=win_p, wout=wout_p, wfi=p["w_ffn_in"], wfo=p["w_ffn_out"], sinkp=sinkp,
        nosink=jnp.full((2, 2, 1, 128), NEG, F32),
        w8=jnp.concatenate([p["ssm_conv_w"], jnp.zeros((1, 1024), F32)], axis=0),
        cb=p["ssm_conv_b"].reshape(1, 1024), dtb=_lanes(p["ssm_dt_bias"]), alog=_lanes(p["ssm_a_log"]),
        dsk=jnp.repeat(p["ssm_d"], HD).reshape(1, 512), gs=p["ssm_norm_g"].reshape(1, 512),
        gmix=p["g_mix"].reshape(1, D), gffn=p["g_ffn"].reshape(1, D))


def _mods(mod2):
    return [_cls2(mod2[0, D * k:D * (k + 1)], mod2[1, D * k:D * (k + 1)]) for k in range(6)]


def _layer_fwd(X, mod2, c, rpb, tabs, L, ctx_out):
    cos, sin, rm, tri2 = tabs
    sh1, sc1, gt1, sh2, sc2, gt2 = _mods(mod2)
    biasd = _na_bias(rpb, L // GW)
    qa, qb, z, ka, va, kb, vb, xbc, dt, h1 = in_fwd(X, c["gmix"], sh1, sc1, c["win"], cos, sin, rm, L)
    oa = wa_fwd(qa, ka, va, c["sinkp"], L)
    ob = na_fwd(qb, kb, vb, biasd, L)
    if ctx_out:
        oa_c = ctx_fwd(qa, ka, va, c["sinkp"], True, L)
        ob_c = ctx_fwd(qb, kb, vb, c["nosink"], False, L)
    else:
        oa_c = ob_c = jnp.zeros((LC, 256), F32)
    oa = jnp.concatenate([oa, oa_c], axis=0)
    ob = jnp.concatenate([ob, ob_c], axis=0)
    act = conv_fwd(xbc, c["w8"], c["cb"], L)
    y2, hs = ssd_fwd(act, dt, c["dtb"], c["alog"], tri2, L)
    X1, cat = out_fwd(oa, ob, y2, act, z, c["dsk"], c["gs"], c["wout"], X, gt1, L)
    X2 = ffn_fwd(X1, c["gffn"], sh2, sc2, gt2, c["wfi"], c["wfo"], L)
    saved = dict(X=X, X1=X1, qa=qa, qb=qb, z=z, ka=ka, va=va, kb=kb, vb=vb, xbc=xbc, dt=dt, h1=h1, oa=oa, ob=ob,
                 act=act, y2=y2, hs=hs, cat=cat, biasd=biasd)
    return X2, saved


def _layer_bwd(dX2, s, mod2, c, tabs, L, ctx_out):
    cos, sin, rm, tri2 = tabs
    sh1, sc1, gt1, sh2, sc2, gt2 = _mods(mod2)
    R = L // GW
    dX1, h2, dU, actf, dOut, dgffn, dsh2, dsc2, dgt2 = ffn_bwd(s["X1"], c["gffn"], sh2, sc2, gt2, c["wfi"], c["wfo"],
                                                               dX2, L)
    g = {}
    g["w_ffn_in"] = tn_mm(h2, dU, 1408, MXU)
    g["w_ffn_out"] = tn_mm(actf, dOut, 512, MXU)
    doa, dob, dy, dxs_skip, dz, dmix, ddsk, dgs, dgt1 = out_bwd(s["oa"], s["ob"], s["y2"], s["act"], s["z"], c["dsk"],
                                                                c["gs"], c["wout"], gt1, dX1, L)
    dwout = tn_mm(s["cat"], dmix, 512, MXU)
    g["w_out"] = jnp.concatenate([dwout[_QA_PERM, :], dwout[256:, :]], axis=0)
    dS, ddt2, ddtb, dal = ssd_bwd(s["act"], s["dt"], c["dtb"], c["alog"], tri2, s["hs"], dy, L)
    dxbc, dw8, dcb = conv_bwd(s["xbc"], c["w8"], c["cb"], dS, dxs_skip, L)
    dqa, dkpad, dvpad, dkxa, dvxa, dska = wa_bwd(s["qa"], s["ka"], s["va"], c["sinkp"], doa, L)
    dqb, dkb, dvb, dkxb, dvxb, dbias = na_bwd(s["qb"], s["kb"], s["vb"], s["biasd"], dob, L)
    if ctx_out:
        dqa_c, dk1, dv1, dsk1 = ctx_bwd(s["qa"], s["ka"], s["va"], c["sinkp"], doa, True, L)
        dqb_c, dk2, dv2, _ = ctx_bwd(s["qb"], s["kb"], s["vb"], c["nosink"], dob, False, L)
        dkxa, dvxa, dska = dkxa + dk1, dvxa + dv1, dska + dsk1
        dkxb, dvxb = dkxb + dk2, dvxb + dv2
    else:
        dqa_c = dqb_c = jnp.zeros((LC, 256), F32)
    cat0 = lambda a, b: jnp.concatenate([a, b], axis=0)
    dX, dycat, dgmix, dsh1, dsc1 = in_bwd(
        s["X"], c["gmix"], sh1, sc1, c["win"], cos, sin, rm, dX1, cat0(dqa, dqa_c), cat0(dqb, dqb_c), dz,
        cat0(dkpad[Q:L + Q], dkxa), cat0(dvpad[Q:L + Q], dvxa), cat0(dkb, dkxb), cat0(dvb, dvxb), dxbc, ddt2, L)
    dwin = tn_mm(s["h1"], dycat, 1024, MXU)
    g["w_in"] = jnp.concatenate([dwin[:, _QA_PERM], dwin[:, 256:IN_COLS]], axis=1)
    g["g_mix"] = dgmix.reshape(D)
    g["g_ffn"] = dgffn.reshape(D)
    sk = jnp.sum(dska, axis=(2, 3))
    g["wa_sink"] = jnp.zeros((4,), F32).at[_PAIR_HEADS.reshape(-1)].set(sk.reshape(-1))
    g["na_rpb"] = _na_bias_grad(dbias, R)
    g["ssm_conv_w"] = dw8[:7]
    g["ssm_conv_b"] = dcb.reshape(1024)
    g["ssm_dt_bias"] = (ddtb[0] + ddtb[1])[0, :16].reshape(2, 8)
    g["ssm_a_log"] = (dal[0] + dal[1])[0, :16].reshape(2, 8)
    g["ssm_d"] = jnp.sum(ddsk.reshape(8, HD), axis=1)
    g["ssm_norm_g"] = dgs.reshape(512)
    dmod2 = jnp.concatenate([dsh1, dsc1, dgt1, dsh2, dsc2, dgt2], axis=2).reshape(2, 6 * D)
    return dX, g, dmod2


def local_step(x, ctx, tgt, mods, layers, g_final, L):
    tabs = _tables(L)
    X = jnp.concatenate([x, ctx], axis=0)
    consts = [_layer_consts(p) for p in layers]
    saved = []
    for i in range(2):
        X, s = _layer_fwd(X, mods[i], consts[i], layers[i]["na_rpb"], tabs, L, ctx_out=(i == 0))
        saved.append(s)
    loss8, dxl, dgfin = loss_head(X, g_final.reshape(1, D), tgt, L)
    dX = jnp.concatenate([dxl, jnp.zeros((LC, D), F32)], axis=0)
    grads, dmods = [None, None], [None, None]
    for i in (1, 0):
        dX, grads[i], dmods[i] = _layer_bwd(dX, saved[i], mods[i], consts[i], tabs, L, ctx_out=(i == 0))
    return loss8[0, 0], dX[:L], grads, jnp.stack(dmods), dgfin.reshape(D)


def _place():
    x, y, c = lax.axis_index("x"), lax.axis_index("y"), lax.axis_index("c")
    return x, y, c


def _slot(b):
    return 4 * b[0] + 2 * b[1] + b[2]


def _any():
    return pl.BlockSpec(memory_space=pl.ANY)


def all_gather(xs, name):
    n = len(xs)

    def body(*refs):
        x_refs, o_refs = refs[:n], refs[n:2 * n]
        send_sems, recv_sems, local_sems = refs[2 * n:]
        x, y, c = _place()
        me, sib = (x, y, c), (x, y, 1 - c)
        chips = [(1 - x, y), (x, 1 - y), (1 - x, 1 - y)]

        def copy(t, k, blk, to, src=None):
            dst = o_refs[t].at[_slot(blk)]
            return pltpu.make_async_remote_copy(
                src_ref=dst if src is None else src, dst_ref=dst, send_sem=send_sems.at[7 * t + k],
                recv_sem=recv_sems.at[7 * t + k], device_id=to, device_id_type=MESH_T)

        mine = [pltpu.make_async_copy(x_refs[t], o_refs[t].at[_slot(me)], local_sems.at[t]) for t in range(n)]
        for cp in mine:
            cp.start()
        first = []
        for t in range(n):
            first.append(copy(t, 0, me, sib, src=x_refs[t]))
            first += [copy(t, 1 + j, me, (*chip, c), src=x_refs[t]) for j, chip in enumerate(chips)]
        for cp in first:
            cp.start()
        passed = []
        for j, chip in enumerate(chips):
            for t in range(n):
                copy(t, 1 + j, (*chip, c), me).wait_recv()
                cp = copy(t, 4 + j, (*chip, c), sib)
                cp.start()
                passed.append(cp)
        for t in range(n):
            copy(t, 0, sib, me).wait_recv()
            for j, chip in enumerate(chips):
                copy(t, 4 + j, (*chip, 1 - c), me).wait_recv()
        for cp in first + passed:
            cp.wait_send()
        for cp in mine:
            cp.wait()

    return pl.pallas_call(
        body, name=name, out_shape=[_sds((NDEV,) + a.shape, a.dtype) for a in xs],
        in_specs=[_any()] * n, out_specs=[_any()] * n,
        scratch_shapes=[pltpu.SemaphoreType.DMA((7 * n,)), pltpu.SemaphoreType.DMA((7 * n,)),
                        pltpu.SemaphoreType.DMA((n,))],
        interpret=_INTERPRET)(*xs)


def all_to_all(xs, name):
    n = len(xs)

    def body(*refs):
        x_refs, o_refs = refs[:n], refs[n:2 * n]
        send_sems, recv_sems, local_sems = refs[2 * n:]
        x, y, c = _place()
        me = (x, y, c)
        flip = lambda v, b: (1 - v) if b else v
        peers = [(flip(x, k >> 2 & 1), flip(y, k >> 1 & 1), flip(c, k & 1)) for k in range(1, NDEV)]
        mine = [pltpu.make_async_copy(x_refs[t].at[_slot(me)], o_refs[t].at[_slot(me)], local_sems.at[t])
                for t in range(n)]
        for cp in mine:
            cp.start()

        def copy(t, k, src_slot, dst_slot, to):
            return pltpu.make_async_remote_copy(
                src_ref=x_refs[t].at[src_slot], dst_ref=o_refs[t].at[dst_slot], send_sem=send_sems.at[7 * t + k],
                recv_sem=recv_sems.at[7 * t + k], device_id=to, device_id_type=MESH_T)

        sends = [copy(t, k, _slot(p), _slot(me), p) for t in range(n) for k, p in enumerate(peers)]
        for cp in sends:
            cp.start()
        for t in range(n):
            for k, p in enumerate(peers):
                copy(t, k, _slot(p), _slot(p), me).wait_recv()
        for cp in sends:
            cp.wait_send()
        for cp in mine:
            cp.wait()

    return pl.pallas_call(
        body, name=name, out_shape=[_sds(a.shape, a.dtype) for a in xs],
        in_specs=[_any()] * n, out_specs=[_any()] * n,
        scratch_shapes=[pltpu.SemaphoreType.DMA((7 * n,)), pltpu.SemaphoreType.DMA((7 * n,)),
                        pltpu.SemaphoreType.DMA((n,))],
        interpret=_INTERPRET)(*xs)


def adam_reduce(P, w, m, v, name):
    n, R, C = P.shape
    br = R // 4 if R % 64 == 0 else R

    def body(p_ref, w_ref, m_ref, v_ref, g_o, d_o, m_o, v_o):
        g = p_ref[0].astype(F32)
        for k in range(1, n):
            g = g + p_ref[k].astype(F32)
        m1 = ADAM_B1 * m_ref[...] + (1.0 - ADAM_B1) * g
        v1 = ADAM_B2 * v_ref[...] + (1.0 - ADAM_B2) * jnp.square(g)
        m_hat = m1 / (1.0 - ADAM_B1 ** ADAM_STEP)
        v_hat = v1 / (1.0 - ADAM_B2 ** ADAM_STEP)
        g_o[...] = g
        d_o[...] = -ADAM_LR * (m_hat / (jnp.sqrt(v_hat) + ADAM_EPS) + ADAM_WD * w_ref[...])
        m_o[...] = m1
        v_o[...] = v1

    blk = pl.BlockSpec((br, C), lambda i: (i, 0))
    return _pc(body, name, [_sds((R, C))] * 4, grid=(R // br,),
               in_specs=[pl.BlockSpec((n, br, C), lambda i: (0, i, 0)), blk, blk, blk], out_specs=[blk] * 4)(P, w, m, v)


def mod_fwd(scin, wmod, bcol):
    def body(s_ref, w_ref, b_ref, o_ref):
        o_ref[0] = mm(_silu(s_ref[...]), w_ref[0]) + b_ref[0]

    return _pc(body, "mod_fwd", _sds((2, 16, 768)), grid=(2,),
               in_specs=[pl.BlockSpec((16, D), lambda l: (0, 0)), pl.BlockSpec((1, D, 768), lambda l: (l, 0, 0)),
                         pl.BlockSpec((1, 1, 768), lambda l: (l, 0, 0))],
               out_specs=pl.BlockSpec((1, 16, 768), lambda l: (l, 0, 0)))(scin, wmod, bcol)


def mod_bwd(scin, wmod, G):
    def body(s_ref, w_ref, g_ref, dw_o, ds_o):
        _, vjp = jax.vjp(lambda s, w: mm(_silu(s), w), s_ref[...], w_ref[0])
        ds, dw = vjp(g_ref[0])
        dw_o[0] = dw
        _acc_init(pl.program_id(0) == 0, [ds_o])
        ds_o[...] += ds

    full = pl.BlockSpec((16, D), lambda l: (0, 0))
    wsp = pl.BlockSpec((1, D, 768), lambda l: (l, 0, 0))
    return _pc(body, "mod_bwd", [_sds((2, D, 768)), _sds((16, D))], grid=(2,),
               in_specs=[full, wsp, pl.BlockSpec((1, 16, 768), lambda l: (l, 0, 0))], out_specs=[wsp, full])(
        scin, wmod, G)


_SMALL = ["b_mod", "g_mix", "wa_sink", "na_rpb", "ssm_conv_w", "ssm_conv_b", "ssm_dt_bias", "ssm_a_log", "ssm_d",
          "ssm_norm_g", "g_ffn", "g_final", "dmod_s", "dmod_c"]


def _pack(parts):
    rows = []
    for a in parts:
        f = a.reshape(-1).astype(F32)
        rows.append(jnp.pad(f, (0, (-f.shape[0]) % 1024)).reshape(-1, 128))
    return jnp.concatenate(rows, axis=0)


def _unpack(packed, shapes):
    out, r = [], 0
    for s in shapes:
        nel = int(np.prod(s))
        nr = -(-nel // 1024) * 8
        out.append(packed[r:r + nr].reshape(-1)[:nel].reshape(s))
        r += nr
    return out


def kernel(x, c, ctx, c_ctx, w_mod, b_mod, g_mix, w_in, wa_sink, na_rpb, ssm_conv_w, ssm_conv_b, ssm_dt_bias, ssm_a_log, ssm_d, ssm_norm_g, w_out, g_ffn, w_ffn_in, w_ffn_out, g_final, loss_target, m_c_ctx, m_w_mod, m_b_mod, m_g_mix, m_w_in, m_wa_sink, m_na_rpb, m_ssm_conv_w, m_ssm_conv_b, m_ssm_dt_bias, m_ssm_a_log, m_ssm_d, m_ssm_norm_g, m_w_out, m_g_ffn, m_w_ffn_in, m_w_ffn_out, m_g_final, v_c_ctx, v_w_mod, v_b_mod, v_g_mix, v_w_in, v_wa_sink, v_na_rpb, v_ssm_conv_w, v_ssm_conv_b, v_ssm_dt_bias, v_ssm_a_log, v_ssm_d, v_ssm_norm_g, v_w_out, v_g_ffn, v_w_ffn_in, v_w_ffn_out, v_g_final):
    L = x.shape[1]
    px, py, pc = _place()
    me = 4 * px + 2 * py + pc
    W = dict(c_ctx=c_ctx, w_mod=w_mod, b_mod=b_mod, g_mix=g_mix, w_in=w_in, wa_sink=wa_sink, na_rpb=na_rpb,
             ssm_conv_w=ssm_conv_w, ssm_conv_b=ssm_conv_b, ssm_dt_bias=ssm_dt_bias, ssm_a_log=ssm_a_log, ssm_d=ssm_d,
             ssm_norm_g=ssm_norm_g, w_out=w_out, g_ffn=g_ffn, w_ffn_in=w_ffn_in, w_ffn_out=w_ffn_out, g_final=g_final)
    M = dict(c_ctx=m_c_ctx, w_mod=m_w_mod, b_mod=m_b_mod, g_mix=m_g_mix, w_in=m_w_in, wa_sink=m_wa_sink,
             na_rpb=m_na_rpb, ssm_conv_w=m_ssm_conv_w, ssm_conv_b=m_ssm_conv_b, ssm_dt_bias=m_ssm_dt_bias,
             ssm_a_log=m_ssm_a_log, ssm_d=m_ssm_d, ssm_norm_g=m_ssm_norm_g, w_out=m_w_out, g_ffn=m_g_ffn,
             w_ffn_in=m_w_ffn_in, w_ffn_out=m_w_ffn_out, g_final=m_g_final)
    V = dict(c_ctx=v_c_ctx, w_mod=v_w_mod, b_mod=v_b_mod, g_mix=v_g_mix, w_in=v_w_in, wa_sink=v_wa_sink,
             na_rpb=v_na_rpb, ssm_conv_w=v_ssm_conv_w, ssm_conv_b=v_ssm_conv_b, ssm_dt_bias=v_ssm_dt_bias,
             ssm_a_log=v_ssm_a_log, ssm_d=v_ssm_d, ssm_norm_g=v_ssm_norm_g, w_out=v_w_out, g_ffn=v_g_ffn,
             w_ffn_in=v_w_ffn_in, w_ffn_out=v_w_ffn_out, g_final=v_g_final)

    c_all, conv_all = all_gather([c, ssm_conv_w], "gather_small")
    big = all_gather([w_in.astype(MXU), w_out.astype(MXU), w_ffn_in.astype(MXU), w_ffn_out.astype(MXU)],
                     "gather_weights")
    w_in_f = big[0].transpose(1, 2, 0, 3).reshape(2, D, IN_COLS)
    w_out_f = big[1].transpose(1, 0, 2, 3).reshape(2, D, D)
    w_fi_f = big[2].transpose(1, 2, 0, 3).reshape(2, D, 2 * DFF)
    w_fo_f = big[3].transpose(1, 0, 2, 3).reshape(2, DFF, D)
    conv_f = conv_all.transpose(1, 2, 0, 3).reshape(2, 7, 1024)

    scin = jnp.concatenate([c_all.reshape(NDEV, D), c_ctx.reshape(1, D), jnp.zeros((7, D), F32)], axis=0)
    bcol = lax.dynamic_slice_in_dim(b_mod, me * 768, 768, axis=1).reshape(2, 1, 768)
    mod_all, = all_gather([mod_fwd(scin, w_mod, bcol)], "gather_mod")
    mod_rows = mod_all.transpose(1, 2, 0, 3).reshape(2, 16, 6 * D)
    mods = jnp.stack([lax.dynamic_index_in_dim(mod_rows, me, axis=1, keepdims=False), mod_rows[:, 8]], axis=1)

    layers = [dict(w_in=w_in_f[i], w_out=w_out_f[i], w_ffn_in=w_fi_f[i], w_ffn_out=w_fo_f[i], g_mix=g_mix[i],
                   wa_sink=wa_sink[i], na_rpb=na_rpb[i], ssm_conv_w=conv_f[i], ssm_conv_b=ssm_conv_b[i],
                   ssm_dt_bias=ssm_dt_bias[i], ssm_a_log=ssm_a_log[i], ssm_d=ssm_d[i], ssm_norm_g=ssm_norm_g[i],
                   g_ffn=g_ffn[i]) for i in range(2)]
    loss, dx, grads, dmods, dgfin = local_step(x[0], ctx[0], loss_target[0], mods, layers, g_final, L)
    loss = lax.psum(loss, ("x", "y", "c"))

    stk = lambda n: jnp.stack([grads[0][n], grads[1][n]])
    small = dict(b_mod=dmods[:, 0] + dmods[:, 1], g_final=dgfin, dmod_s=dmods[:, 0], dmod_c=dmods[:, 1])
    for nme in _SMALL:
        if nme not in small:
            small[nme] = stk(nme)
    shapes = [small[nme].shape for nme in _SMALL]
    zero_like = lambda nme: jnp.zeros(small[nme].shape, F32)
    own = lambda S, nme: S[nme] if (nme in S and S[nme].shape == small[nme].shape) else zero_like(nme)
    gath, = all_gather([_pack([small[nme] for nme in _SMALL])], "gather_grads")
    sm = adam_reduce(gath, _pack([own(W, nme) for nme in _SMALL]), _pack([own(M, nme) for nme in _SMALL]),
                     _pack([own(V, nme) for nme in _SMALL]), "adam_small")
    res = {nme: vals for nme, vals in zip(_SMALL, zip(*[_unpack(a, shapes) for a in sm]))}

    cols = lambda a: lax.dynamic_slice_in_dim(a, me * 768, 768, axis=-1)
    gparts = [_unpack(gath[d], shapes) for d in range(NDEV)]
    dmod_s_all = jnp.stack([gparts[d][_SMALL.index("dmod_s")] for d in range(NDEV)], axis=1)
    G = jnp.concatenate([cols(dmod_s_all), cols(res["dmod_c"][0])[:, None, :], jnp.zeros((2, 7, 768), F32)], axis=1)
    dwmod, dscin = mod_bwd(scin, w_mod, G)
    cc_g, = all_gather([dscin[8].reshape(8, 128)], "gather_cctx")
    out = {}
    out["c_ctx"] = [a.reshape(D) for a in adam_reduce(cc_g, c_ctx.reshape(8, 128), m_c_ctx.reshape(8, 128),
                                                      v_c_ctx.reshape(8, 128), "adam_cctx")]
    out["w_mod"] = [a.reshape(2, D, 768) for a in adam_reduce(
        dwmod.reshape(1, 2 * D, 768), w_mod.reshape(2 * D, 768), m_w_mod.reshape(2 * D, 768),
        v_w_mod.reshape(2 * D, 768), "adam_wmod")]
    gconv = lax.dynamic_slice_in_dim(res["ssm_conv_w"][0], me * 128, 128, axis=2)
    out["ssm_conv_w"] = [a.reshape(2, 7, 128) for a in adam_reduce(
        gconv.reshape(1, 14, 128), ssm_conv_w.reshape(14, 128), m_ssm_conv_w.reshape(14, 128),
        v_ssm_conv_w.reshape(14, 128), "adam_conv")]
    for nme in _SMALL:
        if nme not in ("ssm_conv_w", "dmod_s", "dmod_c"):
            out[nme] = list(res[nme])

    gin = stk("w_in").reshape(2, D, NDEV, IN_COLS // NDEV).transpose(2, 0, 1, 3)
    gfi = stk("w_ffn_in").reshape(2, D, NDEV, 2 * DFF // NDEV).transpose(2, 0, 1, 3)
    gout = stk("w_out").reshape(2, NDEV, D // NDEV, D).transpose(1, 0, 2, 3)
    gfo = stk("w_ffn_out").reshape(2, NDEV, DFF // NDEV, D).transpose(1, 0, 2, 3)
    rin, rout, rfi, rfo = all_to_all([gin, gout, gfi, gfo], "exchange_grads")
    for nme, r in (("w_in", rin), ("w_out", rout), ("w_ffn_in", rfi), ("w_ffn_out", rfo)):
        shp = W[nme].shape
        r2 = (shp[0] * shp[1], shp[2])
        out[nme] = [a.reshape(shp) for a in adam_reduce(r.reshape((NDEV,) + r2), W[nme].reshape(r2),
                                                        M[nme].reshape(r2), V[nme].reshape(r2), "adam_" + nme)]
    order = ["c_ctx", "w_mod", "b_mod", "g_mix", "w_in", "wa_sink", "na_rpb", "ssm_conv_w", "ssm_conv_b",
             "ssm_dt_bias", "ssm_a_log", "ssm_d", "ssm_norm_g", "w_out", "g_ffn", "w_ffn_in", "w_ffn_out", "g_final"]
    return (loss, dx.reshape(1, L, D), *[out[nme][0] for nme in order], *[out[nme][1] for nme in order],
            *[out[nme][2] for nme in order], *[out[nme][3] for nme in order])
```

```python
import functools
import math

import numpy as np
import jax
import jax.numpy as jnp
from jax import lax
from jax.experimental import pallas as pl
from jax.experimental.pallas import tpu as pltpu

F32 = jnp.float32
MXU = jnp.bfloat16
_INTERPRET = False
VMEM_LIMIT = 60 * 1024 * 1024

D = 1024
LC = 256
GW = 64
HD = 64
EPS = 1e-6
NEG = -1e30
NDEV = 8
Q = 128
NSTATE = 128
DFF = 2816
IN_COLS = 2832
NP_IN = 3072
C_QA, C_QB, C_Z, C_KA, C_VA, C_KB, C_VB, C_XBC, C_DT = 0, 256, 512, 1024, 1152, 1280, 1536, 1792, 2816
ADAM_LR, ADAM_B1, ADAM_B2, ADAM_EPS, ADAM_WD, ADAM_STEP = 0.001, 0.9, 0.999, 1e-08, 0.01, 10
MESH_T = pl.DeviceIdType.MESH


def _dg(a, b, ca, cb):
    return lax.dot_general(a.astype(MXU), b.astype(MXU), (((ca,), (cb,)), ((), ())), preferred_element_type=F32)


@jax.custom_vjp
def mm(a, b):
    return _dg(a, b, 1, 0)


def _mm_f(a, b):
    return _dg(a, b, 1, 0), (a, b)


def _mm_b(res, g):
    a, b = res
    return _dg(g, b, 1, 1).astype(a.dtype), _dg(a, g, 0, 0).astype(b.dtype)


mm.defvjp(_mm_f, _mm_b)


@jax.custom_vjp
def mm_nt(a, b):
    return _dg(a, b, 1, 1)


def _mmnt_f(a, b):
    return _dg(a, b, 1, 1), (a, b)


def _mmnt_b(res, g):
    a, b = res
    return _dg(g, b, 1, 0).astype(a.dtype), _dg(g, a, 0, 0).astype(b.dtype)


mm_nt.defvjp(_mmnt_f, _mmnt_b)


@jax.custom_vjp
def mm_tn(a, b):
    return _dg(a, b, 0, 0)


def _mmtn_f(a, b):
    return _dg(a, b, 0, 0), (a, b)


def _mmtn_b(res, g):
    a, b = res
    return _dg(b, g, 1, 1).astype(a.dtype), _dg(a, g, 1, 0).astype(b.dtype)


mm_tn.defvjp(_mmtn_f, _mmtn_b)


def _exact(a, b):
    return lax.dot_general(a, b, (((1,), (0,)), ((), ())), precision=lax.Precision.HIGHEST,
                           preferred_element_type=F32)


def _pc(body, name, out_shape, grid=None, in_specs=None, out_specs=None, scratch=()):
    kw = {}
    if grid is not None:
        kw = dict(grid=grid, in_specs=in_specs, out_specs=out_specs)
    elif in_specs is not None:
        kw = dict(in_specs=in_specs, out_specs=out_specs)
    return pl.pallas_call(body, name=name, out_shape=out_shape, scratch_shapes=list(scratch),
                          compiler_params=pltpu.CompilerParams(vmem_limit_bytes=VMEM_LIMIT),
                          interpret=_INTERPRET, **kw)


def _vm():
    return pl.BlockSpec(memory_space=pltpu.VMEM)


def _sds(shape, dt=F32):
    return jax.ShapeDtypeStruct(shape, dt)


def _iota(shape, dim):
    return lax.broadcasted_iota(jnp.int32, shape, dim)


def _silu(x):
    return x * jax.nn.sigmoid(x)


def _softplus(x):
    return jnp.maximum(x, 0.0) + jnp.log1p(jnp.exp(-jnp.abs(x)))


def _normmod(x, g, sh, sc):
    r = lax.rsqrt(jnp.mean(x * x, axis=-1, keepdims=True) + EPS)
    return (x * r * g) * (1.0 + sc) + sh


def _rope(x, cos, sin, rm):
    return x * cos + _exact(x, rm) * sin


def _acc_init(first, refs):
    @pl.when(first)
    def _():
        for r in refs:
            r[...] = jnp.zeros_like(r)


def in_fwd(X, g, sh, sc, W, cos, sin, rm, L):
    T = X.shape[0]
    TR = 256
    nlt = L // TR

    def body(x_ref, g_ref, sh_ref, sc_ref, w_ref, cos_ref, sin_ref, rm_ref,
             qa, qb, z, ka, va, kb, vb, xbc, dt, hout):
        h = _normmod(x_ref[...], g_ref[...], sh_ref[0], sc_ref[0]).astype(MXU)
        hout[...] = h
        y = jnp.dot(h, w_ref[...], preferred_element_type=F32)
        cs, sn, r = cos_ref[...], sin_ref[...], rm_ref[...]
        qa[...] = _rope(y[:, C_QA:C_QB], cs, sn, r).astype(MXU)
        qb[...] = y[:, C_QB:C_Z].astype(MXU)
        z[...] = y[:, C_Z:C_KA]
        ka[...] = _rope(y[:, C_KA:C_VA], cs[:, :128], sn[:, :128], r[:128, :128]).astype(MXU)
        va[...] = y[:, C_VA:C_KB].astype(MXU)
        kb[...] = y[:, C_KB:C_VB].astype(MXU)
        vb[...] = y[:, C_VB:C_XBC].astype(MXU)
        xbc[...] = y[:, C_XBC:C_DT]
        dt[...] = y[:, C_DT:C_DT + 128]

    row = lambda w: pl.BlockSpec((TR, w), lambda i: (i, 0))
    cls = pl.BlockSpec((1, 1, D), lambda i: (i // nlt, 0, 0))
    widths = [(256, MXU), (256, MXU), (512, F32), (128, MXU), (128, MXU), (256, MXU), (256, MXU), (1024, F32),
              (128, F32), (D, MXU)]
    return _pc(body, "in_fwd", [_sds((T, w), d) for w, d in widths], grid=(T // TR,),
               in_specs=[row(D), pl.BlockSpec((1, D), lambda i: (0, 0)), cls, cls, _vm(), row(256), row(256), _vm()],
               out_specs=[row(w) for w, _ in widths])(X, g, sh, sc, W, cos, sin, rm)


def in_bwd(X, g, sh, sc, W, cos, sin, rm, dxres, dqa, dqb, dz, dka, dva, dkb, dvb, dxbc, ddt2, L):
    T = X.shape[0]
    TR = 256
    nlt = L // TR

    def body(x_ref, g_ref, sh_ref, sc_ref, w_ref, cos_ref, sin_ref, rm_ref, dxres_ref, dqa_r, dqb_r, dz_r, dka_r,
             dva_r, dkb_r, dvb_r, dxbc_r, ddt0_r, ddt1_r, dx_o, dy_o, dg_o, dsh_o, dsc_o):
        i = pl.program_id(0)
        cs, sn, r = cos_ref[...], sin_ref[...], rm_ref[...]
        _, vq = jax.vjp(lambda t: _rope(t, cs, sn, r), dqa_r[...])
        _, vk = jax.vjp(lambda t: _rope(t, cs[:, :128], sn[:, :128], r[:128, :128]), dka_r[...])
        dyqa, = vq(dqa_r[...])
        dyka, = vk(dka_r[...])
        ddt = ddt0_r[0] + ddt1_r[0]
        dy = jnp.concatenate([dyqa, dqb_r[...], dz_r[...], dyka, dva_r[...], dkb_r[...], dvb_r[...], dxbc_r[...],
                              ddt, jnp.zeros((TR, NP_IN - C_DT - 128), F32)], axis=1).astype(MXU)
        dy_o[...] = dy
        dh = lax.dot_general(dy, w_ref[...], (((1,), (1,)), ((), ())), preferred_element_type=F32)
        _, vp = jax.vjp(_normmod, x_ref[...], g_ref[...], sh_ref[0], sc_ref[0])
        dx, dg, dsh, dsc = vp(dh)
        dx_o[...] = dx + dxres_ref[...]
        _acc_init(i == 0, [dg_o])
        _acc_init((i == 0) | (i == nlt), [dsh_o, dsc_o])
        dg_o[...] += dg
        dsh_o[0] += dsh
        dsc_o[0] += dsc

    row = lambda w: pl.BlockSpec((TR, w), lambda i: (i, 0))
    cls = pl.BlockSpec((1, 1, D), lambda i: (i // nlt, 0, 0))
    vec = pl.BlockSpec((1, D), lambda i: (0, 0))
    dts = lambda d: pl.BlockSpec((1, TR, 128), lambda i: (d, i, 0))
    return _pc(body, "in_bwd",
               [_sds((T, D)), _sds((T, NP_IN), MXU), _sds((1, D)), _sds((2, 1, D)), _sds((2, 1, D))],
               grid=(T // TR,),
               in_specs=[row(D), vec, cls, cls, _vm(), row(256), row(256), _vm(), row(D), row(256), row(256), row(512),
                         row(128), row(128), row(256), row(256), row(1024), dts(0), dts(1)],
               out_specs=[row(D), row(NP_IN), vec, cls, cls])(
        X, g, sh, sc, W, cos, sin, rm, dxres, dqa, dqb, dz, dka, dva, dkb, dvb, dxbc, ddt2, ddt2)


def tn_mm(A, G, bn, out_dtype):
    T, K = A.shape
    N = G.shape[1]
    bt = T // 4
    nt = T // bt

    def body(a_ref, g_ref, o_ref, acc):
        t = pl.program_id(1)
        _acc_init(t == 0, [acc])
        acc[...] += lax.dot_general(a_ref[...], g_ref[...], (((0,), (0,)), ((), ())), preferred_element_type=F32)

        @pl.when(t == nt - 1)
        def _():
            o_ref[...] = acc[...].astype(out_dtype)

    return _pc(body, "tn_mm", _sds((K, N), out_dtype), grid=(N // bn, nt),
               in_specs=[pl.BlockSpec((bt, K), lambda n, t: (t, 0)), pl.BlockSpec((bt, bn), lambda n, t: (t, n))],
               out_specs=pl.BlockSpec((K, bn), lambda n, t: (0, n)),
               scratch=[pltpu.VMEM((K, bn), F32)])(A, G)


def _ssm_out(yf, yb, xs, z, dsk, gs):
    y = (yf + yb + dsk * xs) * _silu(z)
    r = lax.rsqrt(jnp.mean(y * y, axis=-1, keepdims=True) + EPS)
    return y * r * gs


def out_fwd(oa, ob, y2, act, z, dsk, gs, W, X, gate, L):
    T = X.shape[0]
    TR = 256
    nlt = L // TR

    def body(oa_r, ob_r, yf_r, yb_r, xs_r, z_r, dsk_r, gs_r, w_ref, x_ref, gt_ref, x1_o, cat_o):
        oc = _ssm_out(yf_r[0], yb_r[0], xs_r[...], z_r[...], dsk_r[...], gs_r[...])
        cat = jnp.concatenate([oa_r[...], ob_r[...], oc], axis=1).astype(MXU)
        cat_o[...] = cat
        x1_o[...] = x_ref[...] + gt_ref[0] * jnp.dot(cat, w_ref[...], preferred_element_type=F32)

    row = lambda w: pl.BlockSpec((TR, w), lambda i: (i, 0))
    ys = lambda d: pl.BlockSpec((1, TR, 512), lambda i: (d, i, 0))
    cls = pl.BlockSpec((1, 1, D), lambda i: (i // nlt, 0, 0))
    v512 = pl.BlockSpec((1, 512), lambda i: (0, 0))
    return _pc(body, "out_fwd", [_sds((T, D)), _sds((T, D), MXU)], grid=(T // TR,),
               in_specs=[row(256), row(256), ys(0), ys(1), row(512), row(512), v512, v512, _vm(), row(D), cls],
               out_specs=[row(D), row(D)])(oa, ob, y2, y2, act, z, dsk, gs, W, X, gate)


def out_bwd(oa, ob, y2, act, z, dsk, gs, W, gate, dX1, L):
    T = dX1.shape[0]
    TR = 256
    nlt = L // TR

    def body(oa_r, ob_r, yf_r, yb_r, xs_r, z_r, dsk_r, gs_r, w_ref, gt_ref, dx1_r,
             doa_o, dob_o, dy_o, dxs_o, dz_o, dmix_o, ddsk_o, dgs_o, dgt_o):
        i = pl.program_id(0)
        w = w_ref[...]

        def f(oa_, ob_, yf, yb, xs, z_, dsk_, gs_, gt):
            oc = _ssm_out(yf, yb, xs, z_, dsk_, gs_)
            return gt * mm(jnp.concatenate([oa_, ob_, oc], axis=1), w)

        _, vjp = jax.vjp(f, oa_r[...], ob_r[...], yf_r[0], yb_r[0], xs_r[...], z_r[...], dsk_r[...], gs_r[...],
                         gt_ref[0])
        dx1 = dx1_r[...]
        doa, dob, dyf, _, dxs, dz, ddsk, dgs, dgt = vjp(dx1)
        doa_o[...] = doa
        dob_o[...] = dob
        dy_o[...] = dyf
        dxs_o[...] = dxs
        dz_o[...] = dz
        dmix_o[...] = (gt_ref[0] * dx1).astype(MXU)
        _acc_init(i == 0, [ddsk_o, dgs_o])
        _acc_init((i == 0) | (i == nlt), [dgt_o])
        ddsk_o[...] += ddsk
        dgs_o[...] += dgs
        dgt_o[0] += dgt

    row = lambda w: pl.BlockSpec((TR, w), lambda i: (i, 0))
    ys = lambda d: pl.BlockSpec((1, TR, 512), lambda i: (d, i, 0))
    cls = pl.BlockSpec((1, 1, D), lambda i: (i // nlt, 0, 0))
    v512 = pl.BlockSpec((1, 512), lambda i: (0, 0))
    return _pc(body, "out_bwd",
               [_sds((T, 256)), _sds((T, 256)), _sds((T, 512)), _sds((T, 512)), _sds((T, 512)), _sds((T, D), MXU),
                _sds((1, 512)), _sds((1, 512)), _sds((2, 1, D))],
               grid=(T // TR,),
               in_specs=[row(256), row(256), ys(0), ys(1), row(512), row(512), v512, v512, _vm(), cls, row(D)],
               out_specs=[row(256), row(256), row(512), row(512), row(512), row(D), v512, v512, cls])(
        oa, ob, y2, y2, act, z, dsk, gs, W, gate, dX1)


def _ffn_core(x, g, sh, sc, gt, wg, wu, wo, eg, eu):
    h = _normmod(x, g, sh, sc)
    a = mm(h, wg) + eg
    u = mm(h, wu) + eu
    act = _silu(a) * u
    return x + gt * mm(act, wo), (h, act)


def ffn_fwd(X, g, sh, sc, gate, Win, Wout, L):
    T = X.shape[0]
    TR = 256
    nlt = L // TR

    def body(x_ref, g_ref, sh_ref, sc_ref, gt_ref, wi_ref, wo_ref, o_ref):
        h = _normmod(x_ref[...], g_ref[...], sh_ref[0], sc_ref[0]).astype(MXU)
        a = jnp.dot(h, wi_ref[:, 0:DFF], preferred_element_type=F32)
        u = jnp.dot(h, wi_ref[:, DFF:2 * DFF], preferred_element_type=F32)
        act = (_silu(a) * u).astype(MXU)
        o_ref[...] = x_ref[...] + gt_ref[0] * jnp.dot(act, wo_ref[...], preferred_element_type=F32)

    row = lambda w: pl.BlockSpec((TR, w), lambda i: (i, 0))
    cls = pl.BlockSpec((1, 1, D), lambda i: (i // nlt, 0, 0))
    vec = pl.BlockSpec((1, D), lambda i: (0, 0))
    return _pc(body, "ffn_fwd", _sds((T, D)), grid=(T // TR,),
               in_specs=[row(D), vec, cls, cls, cls, _vm(), _vm()], out_specs=row(D))(X, g, sh, sc, gate, Win, Wout)


def ffn_bwd(X, g, sh, sc, gate, Win, Wout, dX2, L):
    T = X.shape[0]
    TR = 256
    nlt = L // TR

    def body(x_ref, g_ref, sh_ref, sc_ref, gt_ref, wi_ref, wo_ref, dx2_r,
             dx_o, h_o, du_o, act_o, dout_o, dg_o, dsh_o, dsc_o, dgt_o):
        i = pl.program_id(0)
        wg, wu, wo = wi_ref[:, 0:DFF], wi_ref[:, DFF:2 * DFF], wo_ref[...]
        zero = jnp.zeros((TR, DFF), F32)
        f = lambda x, g_, sh_, sc_, gt, eg, eu: _ffn_core(x, g_, sh_, sc_, gt, wg, wu, wo, eg, eu)
        _, vjp, (h, act) = jax.vjp(f, x_ref[...], g_ref[...], sh_ref[0], sc_ref[0], gt_ref[0], zero, zero,
                                   has_aux=True)
        dx2 = dx2_r[...]
        dx, dg, dsh, dsc, dgt, da, du = vjp(dx2)
        dx_o[...] = dx
        h_o[...] = h.astype(MXU)
        du_o[...] = jnp.concatenate([da, du], axis=1).astype(MXU)
        act_o[...] = act.astype(MXU)
        dout_o[...] = (gt_ref[0] * dx2).astype(MXU)
        _acc_init(i == 0, [dg_o])
        _acc_init((i == 0) | (i == nlt), [dsh_o, dsc_o, dgt_o])
        dg_o[...] += dg
        dsh_o[0] += dsh
        dsc_o[0] += dsc
        dgt_o[0] += dgt

    row = lambda w: pl.BlockSpec((TR, w), lambda i: (i, 0))
    cls = pl.BlockSpec((1, 1, D), lambda i: (i // nlt, 0, 0))
    vec = pl.BlockSpec((1, D), lambda i: (0, 0))
    return _pc(body, "ffn_bwd",
               [_sds((T, D)), _sds((T, D), MXU), _sds((T, 2 * DFF), MXU), _sds((T, DFF), MXU), _sds((T, D), MXU),
                _sds((1, D)), _sds((2, 1, D)), _sds((2, 1, D)), _sds((2, 1, D))],
               grid=(T // TR,),
               in_specs=[row(D), vec, cls, cls, cls, _vm(), _vm(), row(D)],
               out_specs=[row(D), row(D), row(2 * DFF), row(DFF), row(D), vec, cls, cls, cls])(
        X, g, sh, sc, gate, Win, Wout, dX2)


def loss_head(X2, g, tgt, L):
    TR = 256

    def body(x_ref, g_ref, t_ref, loss_o, dx_o, dg_o):
        i = pl.program_id(0)

        def f(x, g_):
            y = x * lax.rsqrt(jnp.mean(x * x, axis=-1, keepdims=True) + EPS) * g_
            return 0.5 * jnp.sum(jnp.mean(jnp.square(y - t_ref[...]), axis=-1, keepdims=True), axis=0, keepdims=True)

        val, vjp = jax.vjp(f, x_ref[...], g_ref[...])
        dx, dg = vjp(jnp.ones((1, 1), F32))
        dx_o[...] = dx
        _acc_init(i == 0, [loss_o, dg_o])
        loss_o[...] += jnp.broadcast_to(val, (8, 128))
        dg_o[...] += dg

    row = pl.BlockSpec((TR, D), lambda i: (i, 0))
    vec = pl.BlockSpec((1, D), lambda i: (0, 0))
    return _pc(body, "loss_head", [_sds((8, 128)), _sds((L, D)), _sds((1, D))], grid=(L // TR,),
               in_specs=[row, vec, row], out_specs=[pl.BlockSpec((8, 128), lambda i: (0, 0)), row, vec])(X2, g, tgt)


def _stack_impl(q):
    lane = _iota(q.shape, 1)
    return jnp.concatenate([jnp.where(lane < HD, q, 0.0), jnp.where(lane >= HD, q, 0.0)], axis=0)


def _unstack_impl(o):
    M = o.shape[0] // 2
    return jnp.where(_iota((M, o.shape[1]), 1) < HD, o[:M], o[M:])


@jax.custom_vjp
def _stack(q):
    return _stack_impl(q)


_stack.defvjp(lambda q: (_stack_impl(q), None), lambda _, g: (_unstack_impl(g),))


@jax.custom_vjp
def _unstack(o):
    return _unstack_impl(o)


_unstack.defvjp(lambda o: (_unstack_impl(o), None), lambda _, g: (_stack_impl(g),))


def _softmax_av(q, ks, vs, biases, sink):
    q2 = _stack(q)
    ss = []
    for k, b in zip(ks, biases):
        s = mm_nt(q2, k) * (HD ** -0.5)
        ss.append(s if b is None else s + b)
    m = functools.reduce(jnp.maximum, [jnp.max(s, axis=1, keepdims=True) for s in ss])
    if sink is not None:
        m = jnp.maximum(m, sink)
    m = lax.stop_gradient(m)
    es = [jnp.exp(s - m) for s in ss]
    den = functools.reduce(lambda a, b_: a + b_, [jnp.sum(e, axis=1, keepdims=True) for e in es])
    if sink is not None:
        den = den + jnp.exp(sink - m)
    inv = 1.0 / den
    return _unstack(functools.reduce(lambda a, b_: a + b_, [mm(e * inv, v) for e, v in zip(es, vs)]))


def _sink_col(s0, s1, M):
    return jnp.concatenate([jnp.broadcast_to(jnp.mean(s0, axis=1, keepdims=True), (M, 1)),
                            jnp.broadcast_to(jnp.mean(s1, axis=1, keepdims=True), (M, 1))], axis=0)


def _stack4_impl(q):
    lane = _iota((q.shape[0], 128), 1)
    parts = []
    for p in range(2):
        qp = q[:, 128 * p:128 * (p + 1)]
        parts += [jnp.where(lane < HD, qp, 0.0), jnp.where(lane >= HD, qp, 0.0)]
    return jnp.concatenate(parts, axis=0)


def _unstack4_impl(o):
    M = o.shape[0] // 4
    lane = _iota((M, 128), 1)
    return jnp.concatenate([jnp.where(lane < HD, o[0:M], o[M:2 * M]),
                            jnp.where(lane < HD, o[2 * M:3 * M], o[3 * M:4 * M])], axis=1)


@jax.custom_vjp
def _stack4(q):
    return _stack4_impl(q)


_stack4.defvjp(lambda q: (_stack4_impl(q), None), lambda _, g: (_unstack4_impl(g),))


@jax.custom_vjp
def _unstack4(o):
    return _unstack4_impl(o)


_unstack4.defvjp(lambda o: (_unstack4_impl(o), None), lambda _, g: (_stack4_impl(g),))


def _wa_block(q, kp, kc, kn, vp, vc, vn, kx, vx, sks, n, L):
    kb = jnp.concatenate([kp, kc, kn], axis=0)
    vb = jnp.concatenate([vp, vc, vn], axis=0)
    qpos = n * Q + (_iota((4 * Q, 3 * Q), 0) & (Q - 1))
    kpos = (n - 1) * Q + _iota((4 * Q, 3 * Q), 1)
    valid = (jnp.abs(qpos - kpos) <= Q) & (kpos >= 0) & (kpos < L)
    bias = jnp.where(valid, 0.0, NEG)
    sink = jnp.concatenate([jnp.broadcast_to(jnp.mean(s_, axis=1, keepdims=True), (Q, 1)) for s_ in sks], axis=0)
    q4 = _stack4(q)
    sc = HD ** -0.5
    sl = mm_nt(q4, kb) * sc + bias
    sx = mm_nt(q4, kx) * sc
    m = lax.stop_gradient(jnp.maximum(jnp.maximum(jnp.max(sl, axis=1, keepdims=True),
                                                  jnp.max(sx, axis=1, keepdims=True)), sink))
    el, ex = jnp.exp(sl - m), jnp.exp(sx - m)
    inv = 1.0 / (jnp.sum(el, axis=1, keepdims=True) + jnp.sum(ex, axis=1, keepdims=True) + jnp.exp(sink - m))
    return _unstack4(mm(el * inv, vb) + mm(ex * inv, vx))


def _wa_specs(L):
    nb = L // Q
    qs = pl.BlockSpec((Q, 256), lambda n: (n, 0))
    kprev = pl.BlockSpec((Q, 128), lambda n: (jnp.maximum(n - 1, 0), 0))
    kcur = pl.BlockSpec((Q, 128), lambda n: (n, 0))
    knext = pl.BlockSpec((Q, 128), lambda n: (jnp.minimum(n + 1, nb - 1), 0))
    kctx = pl.BlockSpec((LC, 128), lambda n: (L // LC, 0))
    sks = pl.BlockSpec((2, 2, 1, 128), lambda n: (0, 0, 0, 0))
    return nb, qs, [kprev, kcur, knext], kctx, sks


def wa_fwd(QA, KA, VA, sinkp, L):
    nb, qs, kband, kctx, sks = _wa_specs(L)

    def body(q_r, kp, kc, kn, vp, vc, vn, kx, vx, sk_r, o_ref):
        n = pl.program_id(0)
        f = lambda t: t[...].astype(F32)
        o_ref[...] = _wa_block(f(q_r), f(kp), f(kc), f(kn), f(vp), f(vc), f(vn), f(kx), f(vx),
                               [sk_r[0, 0], sk_r[0, 1], sk_r[1, 0], sk_r[1, 1]], n, L)

    return _pc(body, "wa_fwd", _sds((L, 256)), grid=(nb,),
               in_specs=[qs] + kband + kband + [kctx, kctx, sks], out_specs=qs)(
        QA, KA, KA, KA, VA, VA, VA, KA, VA, sinkp)


def wa_bwd(QA, KA, VA, sinkp, dO, L):
    nb, qs, kband, kctx, sks = _wa_specs(L)

    def body(q_r, kp, kc, kn, vp, vc, vn, kx, vx, sk_r, do_r, dq_o, dk_o, dv_o, dkx_o, dvx_o, dsk_o):
        n = pl.program_id(0)
        f = lambda t: t[...].astype(F32)
        fn = lambda q, a, b, c, d, e, g, kx_, vx_, s_: _wa_block(q, a, b, c, d, e, g, kx_, vx_, s_, n, L)
        _, vjp = jax.vjp(fn, f(q_r), f(kp), f(kc), f(kn), f(vp), f(vc), f(vn), f(kx), f(vx),
                         [sk_r[0, 0], sk_r[0, 1], sk_r[1, 0], sk_r[1, 1]])
        dq, dkp, dkc, dkn, dvp, dvc, dvn, dkx, dvx, ds = vjp(do_r[...])
        dq_o[...] = dq
        _acc_init(n == 0, [dk_o, dv_o, dkx_o, dvx_o, dsk_o])
        rows = pl.ds(pl.multiple_of(n * Q, Q), 3 * Q)
        dk_o[rows, :] += jnp.concatenate([dkp, dkc, dkn], axis=0)
        dv_o[rows, :] += jnp.concatenate([dvp, dvc, dvn], axis=0)
        dkx_o[...] += dkx
        dvx_o[...] += dvx
        for i_ in range(4):
            dsk_o[i_ // 2, i_ % 2] += ds[i_]

    full = lambda r: pl.BlockSpec((r, 128), lambda n: (0, 0))
    return _pc(body, "wa_bwd",
               [_sds((L, 256)), _sds((L + 2 * Q, 128)), _sds((L + 2 * Q, 128)), _sds((LC, 128)), _sds((LC, 128)),
                _sds((2, 2, 1, 128))],
               grid=(nb,), in_specs=[qs] + kband + kband + [kctx, kctx, sks, qs],
               out_specs=[qs, full(L + 2 * Q), full(L + 2 * Q), full(LC), full(LC), sks])(
        QA, KA, KA, KA, VA, VA, VA, KA, VA, sinkp, dO)


def _ctx_block(q, kx, vx, s0, s1):
    return _softmax_av(q, [kx], [vx], [None], _sink_col(s0, s1, LC))


def ctx_fwd(Qx, Kx, Vx, sinkp, shared, L):
    cq = pl.BlockSpec((LC, 128), lambda p: (L // LC, p))
    ck = pl.BlockSpec((LC, 128), lambda p: (L // LC, 0 if shared else p))
    sks = pl.BlockSpec((1, 2, 1, 128), lambda p: (p, 0, 0, 0))

    def body(q_r, k_r, v_r, sk_r, o_ref):
        f = lambda t: t[...].astype(F32)
        o_ref[...] = _ctx_block(f(q_r), f(k_r), f(v_r), sk_r[0, 0], sk_r[0, 1])

    return _pc(body, "ctx_fwd", _sds((LC, 256)), grid=(2,), in_specs=[cq, ck, ck, sks],
               out_specs=pl.BlockSpec((LC, 128), lambda p: (0, p)))(Qx, Kx, Vx, sinkp)


def ctx_bwd(Qx, Kx, Vx, sinkp, dO, shared, L):
    cq = pl.BlockSpec((LC, 128), lambda p: (L // LC, p))
    ck = pl.BlockSpec((LC, 128), lambda p: (L // LC, 0 if shared else p))
    sks = pl.BlockSpec((1, 2, 1, 128), lambda p: (p, 0, 0, 0))
    op = pl.BlockSpec((LC, 128), lambda p: (0, p))
    ok = pl.BlockSpec((LC, 128), lambda p: (0, 0 if shared else p))
    dos = pl.BlockSpec((LC, 128), lambda p: (L // LC, p))

    def body(q_r, k_r, v_r, sk_r, do_r, dq_o, dk_o, dv_o, dsk_o):
        p = pl.program_id(0)
        f = lambda t: t[...].astype(F32)
        _, vjp = jax.vjp(_ctx_block, f(q_r), f(k_r), f(v_r), sk_r[0, 0], sk_r[0, 1])
        dq, dk, dv, ds0, ds1 = vjp(do_r[...])
        dq_o[...] = dq
        _acc_init((p == 0) if shared else (p >= 0), [dk_o, dv_o])
        dk_o[...] += dk
        dv_o[...] += dv
        dsk_o[0, 0] = ds0
        dsk_o[0, 1] = ds1

    kw = 128 if shared else 256
    return _pc(body, "ctx_bwd", [_sds((LC, 256)), _sds((LC, kw)), _sds((LC, kw)), _sds((2, 2, 1, 128))],
               grid=(2,), in_specs=[cq, ck, ck, sks, dos], out_specs=[op, ok, ok, sks])(Qx, Kx, Vx, sinkp, dO)


def _na_rows(qs, kws, vws, kx, vx, bs):
    sc = HD ** -0.5
    q2 = [_stack(q) for q in qs]
    sl = [mm_nt(a, k) * sc + b for a, k, b in zip(q2, kws, bs)]
    sx = [mm_nt(a, kx) * sc for a in q2]
    m = [lax.stop_gradient(jnp.maximum(jnp.max(a, axis=1, keepdims=True), jnp.max(b, axis=1, keepdims=True)))
         for a, b in zip(sl, sx)]
    el = [jnp.exp(a - c) for a, c in zip(sl, m)]
    ex = [jnp.exp(a - c) for a, c in zip(sx, m)]
    inv = [1.0 / (jnp.sum(a, axis=1, keepdims=True) + jnp.sum(b, axis=1, keepdims=True)) for a, b in zip(el, ex)]
    o2 = [mm(a * i, v) + mm(b * i, vx) for a, b, i, v in zip(el, ex, inv, vws)]
    return [_unstack(o) for o in o2]


def _na_geom(rb, j, R):
    r = rb * 8 + j
    s = jnp.clip(r - 4, 0, R - 8)
    cls = jnp.where(r < 4, r, jnp.where(r > R - 4, r - (R - 8), 4))
    return pl.ds(pl.multiple_of(s * GW, GW), 8 * GW), cls


def _na_load(q_r, k_r, v_r, b_r, rb, R):
    geo = [_na_geom(rb, j, R) for j in range(8)]
    qs = [q_r[j * GW:(j + 1) * GW, :].astype(F32) for j in range(8)]
    kws = [k_r[win, :].astype(F32) for win, _ in geo]
    vws = [v_r[win, :].astype(F32) for win, _ in geo]
    bs = [jnp.concatenate([b_r[0, cls], b_r[1, cls]], axis=0) for _, cls in geo]
    return geo, qs, kws, vws, bs


def na_fwd(QB, KB, VB, biasd, L):
    R = L // GW
    qs = pl.BlockSpec((8 * GW, 128), lambda p, rb: (rb, p))
    kfull = pl.BlockSpec((L, 128), lambda p, rb: (0, p))
    kctx = pl.BlockSpec((LC, 128), lambda p, rb: (L // LC, p))
    bs = pl.BlockSpec((2, 8, GW, 8 * GW), lambda p, rb: (p, 0, 0, 0))

    def body(q_r, k_r, v_r, kx_r, vx_r, b_r, o_ref):
        _, qs_, kws, vws, bs_ = _na_load(q_r, k_r, v_r, b_r, pl.program_id(1), R)
        outs = _na_rows(qs_, kws, vws, kx_r[...].astype(F32), vx_r[...].astype(F32), bs_)
        o_ref[...] = jnp.concatenate(outs, axis=0)

    return _pc(body, "na_fwd", _sds((L, 256)), grid=(2, R // 8), in_specs=[qs, kfull, kfull, kctx, kctx, bs],
               out_specs=qs)(QB, KB, VB, KB, VB, biasd)


def na_bwd(QB, KB, VB, biasd, dO, L):
    R = L // GW
    qs = pl.BlockSpec((8 * GW, 128), lambda p, rb: (rb, p))
    kfull = pl.BlockSpec((L, 128), lambda p, rb: (0, p))
    kctx = pl.BlockSpec((LC, 128), lambda p, rb: (L // LC, p))
    bs = pl.BlockSpec((2, 8, GW, 8 * GW), lambda p, rb: (p, 0, 0, 0))
    oc = pl.BlockSpec((LC, 128), lambda p, rb: (0, p))

    def body(q_r, k_r, v_r, kx_r, vx_r, b_r, do_r, dq_o, dk_o, dv_o, dkx_o, dvx_o, db_o):
        rb = pl.program_id(1)
        _acc_init(rb == 0, [dk_o, dv_o, dkx_o, dvx_o, db_o])
        geo, qs_, kws, vws, bs_ = _na_load(q_r, k_r, v_r, b_r, rb, R)
        _, vjp = jax.vjp(_na_rows, qs_, kws, vws, kx_r[...].astype(F32), vx_r[...].astype(F32), bs_)
        dqs, dkws, dvws, dkx, dvx, dbs = vjp([do_r[j * GW:(j + 1) * GW, :] for j in range(8)])
        dq_o[...] = jnp.concatenate(dqs, axis=0)
        dkx_o[...] += dkx
        dvx_o[...] += dvx
        for j, (win, cls) in enumerate(geo):
            dk_o[win, :] += dkws[j]
            dv_o[win, :] += dvws[j]
            db_o[0, cls] += dbs[j][:GW]
            db_o[1, cls] += dbs[j][GW:]

    return _pc(body, "na_bwd",
               [_sds((L, 256)), _sds((L, 256)), _sds((L, 256)), _sds((LC, 256)), _sds((LC, 256)),
                _sds((4, 8, GW, 8 * GW))],
               grid=(2, R // 8), in_specs=[qs, kfull, kfull, kctx, kctx, bs, qs],
               out_specs=[qs, kfull, kfull, oc, oc, bs])(QB, KB, VB, KB, VB, biasd, dO)


def exact_mm_call(A, B):
    def body(a_ref, b_ref, o_ref):
        o_ref[...] = _exact(a_ref[...], b_ref[...])

    return _pc(body, "exact_mm", _sds((A.shape[0], B.shape[1])))(A, B)


def _conv_shift(x, d, L):
    T = x.shape[0]
    if d == 0:
        return x
    t = _iota(x.shape, 0)
    src = t + d
    ok = (src >= 0) & (src < T) & ((src >= L) == (t >= L))
    return jnp.where(ok, pltpu.roll(x, (-d) % T, 0), 0.0)


def conv_fwd(XBC, w8, b, L):
    T = XBC.shape[0]

    def body(x_ref, w_ref, b_ref, o_ref):
        x = x_ref[...]
        pre = b_ref[...] + functools.reduce(
            lambda a, c: a + c, [_conv_shift(x, k - 3, L) * w_ref[k:k + 1, :] for k in range(7)])
        o_ref[...] = _silu(pre)

    col = pl.BlockSpec((T, 128), lambda j: (0, j))
    return _pc(body, "conv_fwd", _sds((T, 1024)), grid=(8,),
               in_specs=[col, pl.BlockSpec((8, 128), lambda j: (0, j)), pl.BlockSpec((1, 128), lambda j: (0, j))],
               out_specs=col)(XBC, w8, b)


def conv_bwd(XBC, w8, b, dS, dxs_skip, L):
    T = XBC.shape[0]

    def body(x_ref, w_ref, b_ref, d0_r, d1_r, dsk_r, dx_o, dw_o, db_o):
        j = pl.program_id(0)
        x = x_ref[...]
        xs = [_conv_shift(x, k - 3, L) for k in range(7)]
        pre = b_ref[...] + functools.reduce(lambda a, c: a + c, [xs[k] * w_ref[k:k + 1, :] for k in range(7)])
        _, vjp = jax.vjp(_silu, pre)
        dact = d0_r[0] + d1_r[0] + jnp.where(j < 4, dsk_r[...], 0.0)
        dpre, = vjp(dact)
        dx_o[...] = functools.reduce(
            lambda a, c: a + c, [_conv_shift(dpre, 3 - k, L) * w_ref[k:k + 1, :] for k in range(7)])
        dw_o[...] = jnp.concatenate([jnp.sum(dpre * xs[k], axis=0, keepdims=True) for k in range(7)]
                                    + [jnp.zeros((1, 128), F32)], axis=0)
        db_o[...] = jnp.sum(dpre, axis=0, keepdims=True)

    col = pl.BlockSpec((T, 128), lambda j: (0, j))
    w_s = pl.BlockSpec((8, 128), lambda j: (0, j))
    b_s = pl.BlockSpec((1, 128), lambda j: (0, j))
    ds = lambda d: pl.BlockSpec((1, T, 128), lambda j: (d, 0, j))
    return _pc(body, "conv_bwd", [_sds((T, 1024)), _sds((8, 1024)), _sds((1, 1024))], grid=(8,),
               in_specs=[col, w_s, b_s, ds(0), ds(1), pl.BlockSpec((T, 128), lambda j: (0, jnp.minimum(j, 3)))],
               out_specs=[col, w_s, b_s])(XBC, w8, b, dS, dS, dxs_skip)


def _ssd_chunk(xs, bs, cs, dtraw, dtb, alog, hs, tri, d):
    dt = _softplus(dtraw + dtb)
    a = dt * (-jnp.exp(alog))
    acum = _exact(tri, a)
    tot = jnp.sum(a, axis=0, keepdims=True)
    wcol = jnp.exp(tot - acum) * dt
    ea = jnp.exp(acum)
    cd = jnp.exp(tot)
    acum_t, dt_t = acum.T, dt.T
    lane = _iota((Q, 128), 1)
    srow = _iota((128, Q), 0)
    lane1 = _iota((1, 128), 1)
    prow = _iota((128, NSTATE), 0)
    mask = tri > 0.5
    cbs = [mm_nt(cs[g], bs[g]) for g in range(2)]
    ys, hn = [], []
    for j in range(4):
        g = j // 2
        x = xs[j]
        yi, st, eac, cdl = [], [], [], []
        for u in range(2):
            slot = d * 8 + 2 * j + u
            col = lambda m: jnp.sum(jnp.where(lane == slot, m, 0.0), axis=1, keepdims=True)
            rowv = lambda m: jnp.sum(jnp.where(srow == slot, m, 0.0), axis=0, keepdims=True)
            seg = col(acum) - rowv(acum_t)
            dcy = jnp.where(mask, jnp.exp(jnp.where(mask, seg, 0.0)), 0.0)
            yi.append(mm(cbs[g] * dcy * rowv(dt_t), x))
            st.append(mm_tn(x, bs[g] * col(wcol)))
            eac.append(col(ea))
            cdl.append(jnp.sum(jnp.where(lane1 == slot, cd, 0.0), axis=1, keepdims=True))
        yin = mm_nt(cs[g], hs[j])
        ys.append(jnp.where(lane < HD, yi[0] + yin * eac[0], yi[1] + yin * eac[1]))
        hn.append(hs[j] * jnp.where(prow < HD, cdl[0], cdl[1]) + jnp.where(prow < HD, st[0], st[1]))
    return ys, hn


def _ssd_chunk_idx(d, s, nlc, nch):
    return jnp.where(d == 0, (s + nlc) % nch, nch - 1 - s)


def ssd_fwd(ACT, DT, dtb, alog, tri2, L):
    T = ACT.shape[0]
    nlc, nch = L // Q, T // Q

    def body(a_ref, dt_ref, dtb_ref, al_ref, tri_ref, y_o, hs_o, hst):
        d, s = pl.program_id(0), pl.program_id(1)
        _acc_init(s == 0, [hst])
        a = a_ref[...]
        xs = [a[:, 128 * j:128 * (j + 1)] for j in range(4)]
        bs = [a[:, 512 + 128 * g:640 + 128 * g] for g in range(2)]
        cs = [a[:, 768 + 128 * g:896 + 128 * g] for g in range(2)]
        hs = [hst[j] for j in range(4)]
        hs_o[0, 0] = hst[...]
        ys, hn = _ssd_chunk(xs, bs, cs, dt_ref[...], dtb_ref[...], al_ref[...], hs, tri_ref[0], d)
        y_o[0] = jnp.concatenate(ys, axis=1)
        for j in range(4):
            hst[j] = hn[j]

    ck = lambda w: pl.BlockSpec((Q, w), lambda d, s: (_ssd_chunk_idx(d, s, nlc, nch), 0))
    v128 = pl.BlockSpec((1, 128), lambda d, s: (0, 0))
    return _pc(body, "ssd_fwd", [_sds((2, T, 512)), _sds((2, nch, 4, 128, NSTATE))], grid=(2, nch),
               in_specs=[ck(1024), ck(128), v128, v128, pl.BlockSpec((1, Q, Q), lambda d, s: (d, 0, 0))],
               out_specs=[pl.BlockSpec((1, Q, 512), lambda d, s: (d, _ssd_chunk_idx(d, s, nlc, nch), 0)),
                          pl.BlockSpec((1, 1, 4, 128, NSTATE), lambda d, s: (d, s, 0, 0, 0))],
               scratch=[pltpu.VMEM((4, 128, NSTATE), F32)])(ACT, DT, dtb, alog, tri2)


def ssd_bwd(ACT, DT, dtb, alog, tri2, HS, dY, L):
    T = ACT.shape[0]
    nlc, nch = L // Q, T // Q

    def body(a_ref, dt_ref, dtb_ref, al_ref, tri_ref, hs_ref, dy_ref, da_o, ddt_o, ddtb_o, dal_o, dh):
        d, sr = pl.program_id(0), pl.program_id(1)
        _acc_init(sr == 0, [dh, ddtb_o, dal_o])
        a = a_ref[...]
        xs = [a[:, 128 * j:128 * (j + 1)] for j in range(4)]
        bs = [a[:, 512 + 128 * g:640 + 128 * g] for g in range(2)]
        cs = [a[:, 768 + 128 * g:896 + 128 * g] for g in range(2)]
        hs = [hs_ref[0, 0, j] for j in range(4)]
        tri = tri_ref[0]
        fn = lambda xs_, bs_, cs_, dtr, dtb_, al, hs_: _ssd_chunk(xs_, bs_, cs_, dtr, dtb_, al, hs_, tri, d)
        _, vjp = jax.vjp(fn, xs, bs, cs, dt_ref[...], dtb_ref[...], al_ref[...], hs)
        dy = dy_ref[...]
        dys = [dy[:, 128 * j:128 * (j + 1)] for j in range(4)]
        dxs, dbs, dcs, ddt, ddtb, dal, dhs = vjp((dys, [dh[j] for j in range(4)]))
        da_o[0] = jnp.concatenate(dxs + dbs + dcs, axis=1)
        ddt_o[0] = ddt
        ddtb_o[0] += ddtb
        dal_o[0] += dal
        for j in range(4):
            dh[j] = dhs[j]

    cidx = lambda d, sr: _ssd_chunk_idx(d, nch - 1 - sr, nlc, nch)
    ck = lambda w: pl.BlockSpec((Q, w), lambda d, sr: (cidx(d, sr), 0))
    v128 = pl.BlockSpec((1, 128), lambda d, sr: (0, 0))
    o128 = pl.BlockSpec((1, 1, 128), lambda d, sr: (d, 0, 0))
    return _pc(body, "ssd_bwd", [_sds((2, T, 1024)), _sds((2, T, 128)), _sds((2, 1, 128)), _sds((2, 1, 128))],
               grid=(2, nch),
               in_specs=[ck(1024), ck(128), v128, v128, pl.BlockSpec((1, Q, Q), lambda d, sr: (d, 0, 0)),
                         pl.BlockSpec((1, 1, 4, 128, NSTATE), lambda d, sr: (d, nch - 1 - sr, 0, 0, 0)), ck(512)],
               out_specs=[pl.BlockSpec((1, Q, 1024), lambda d, sr: (d, cidx(d, sr), 0)),
                          pl.BlockSpec((1, Q, 128), lambda d, sr: (d, cidx(d, sr), 0)), o128, o128],
               scratch=[pltpu.VMEM((4, 128, NSTATE), F32)])(ACT, DT, dtb, alog, tri2, HS, dY)


_QA_PERM = np.concatenate([np.arange(HD * h, HD * h + HD) for h in (0, 2, 1, 3)])
_PAIR_HEADS = np.array([[0, 2], [1, 3]])


def _tables(L):
    t = jnp.arange(L)
    inv = 10000.0 ** (-jnp.arange(16, dtype=F32) / 16)

    def half(pos):
        ang = pos.astype(F32)[:, None] * inv[None, :]
        return jnp.concatenate([ang, ang], axis=1)

    ang = jnp.tile(jnp.concatenate([half(t // GW), half(t % GW)], axis=1), (1, 4))
    cos = jnp.concatenate([jnp.cos(ang), jnp.ones((LC, 256), F32)], axis=0)
    sin = jnp.concatenate([jnp.sin(ang), jnp.zeros((LC, 256), F32)], axis=0)
    rm = np.zeros((256, 256), np.float32)
    for j in range(256):
        if j % 32 < 16:
            rm[j + 16, j] = -1.0
        else:
            rm[j - 16, j] = 1.0
    tri = np.tril(np.ones((Q, Q), np.float32))
    return cos, sin, jnp.asarray(rm), jnp.asarray(np.stack([tri, tri.T]))


def _na_index(R):
    rc = np.array([0, 1, 2, 3, 4, R - 3, R - 2, R - 1])
    dy = np.clip(rc - 4, 0, R - 8)[:, None] + np.arange(8)[None, :] - rc[:, None] + 7
    qc, cc = np.arange(GW)[:, None], np.arange(GW)[None, :]
    dx = np.clip(cc - qc, -15, 15) + 15
    cstart = np.clip(qc - 8, 0, GW - 16)
    cmask = (cc >= cstart) & (cc < cstart + 16)
    idx = dy[:, None, :, None] * 31 + dx[None, :, None, :]
    return idx.reshape(8, GW, 8 * GW), np.broadcast_to(cmask[None, :, None, :], idx.shape).reshape(8, GW, 8 * GW), \
        dy, dx, cmask


def _na_bias(rpb, R):
    _, cm, dy, dx, _ = _na_index(R)
    e1t = np.zeros((128, GW * GW), np.float32)
    e1t[dx.reshape(-1), np.arange(GW * GW)] = 1.0
    v = jnp.pad(rpb[:, dy.reshape(-1), :].reshape(256, 31), ((0, 0), (0, 97)))
    full = exact_mm_call(v, jnp.asarray(e1t))
    dense = full.reshape(4, 8, 8, GW, GW).transpose(0, 1, 3, 2, 4).reshape(4, 8, GW, 8 * GW)
    return jnp.where(cm[None], dense, NEG)


def _na_bias_grad(dbias, R):
    _, _, dy, dx, cmask = _na_index(R)
    e1 = np.zeros((GW * GW, 128), np.float32)
    e1[np.arange(GW * GW), dx.reshape(-1)] = cmask.reshape(-1)
    a1 = dbias.reshape(4, 8, GW, 8, GW).transpose(0, 1, 3, 2, 4).reshape(256, GW * GW)
    v = exact_mm_call(a1, jnp.asarray(e1))[:, :31].reshape(4, 64, 31)
    e2 = np.zeros((64, 128), np.float32)
    e2[np.arange(64), dy.reshape(-1)] = 1.0
    a2 = jnp.pad(v.transpose(0, 2, 1).reshape(124, 64), ((0, 4), (0, 0)))
    return exact_mm_call(a2, jnp.asarray(e2))[:124, :15].reshape(4, 31, 15).transpose(0, 2, 1)


def _lanes(v, n=128):
    v = v.reshape(1, -1)
    return jnp.pad(v, ((0, 0), (0, n - v.shape[1])))


def _cls2(a, b):
    return jnp.stack([a, b]).reshape(2, 1, D)


def _layer_consts(p):
    w_in = p["w_in"]
    win_p = jnp.concatenate([w_in[:, _QA_PERM], w_in[:, 256:], jnp.zeros((D, NP_IN - IN_COLS), w_in.dtype)], axis=1)
    wout_p = jnp.concatenate([p["w_out"][_QA_PERM, :], p["w_out"][256:, :]], axis=0)
    sinkp = jnp.broadcast_to(p["wa_sink"][_PAIR_HEADS][:, :, None, None], (2, 2, 1, 128))
    return dict(
        win=win_p, wout=wout_p, wfi=p["w_ffn_in"], wfo=p["w_ffn_out"], sinkp=sinkp,
        nosink=jnp.full((2, 2, 1, 128), NEG, F32),
        w8=jnp.concatenate([p["ssm_conv_w"], jnp.zeros((1, 1024), F32)], axis=0),
        cb=p["ssm_conv_b"].reshape(1, 1024), dtb=_lanes(p["ssm_dt_bias"]), alog=_lanes(p["ssm_a_log"]),
        dsk=jnp.repeat(p["ssm_d"], HD).reshape(1, 512), gs=p["ssm_norm_g"].reshape(1, 512),
        gmix=p["g_mix"].reshape(1, D), gffn=p["g_ffn"].reshape(1, D))


def _mods(mod2):
    return [_cls2(mod2[0, D * k:D * (k + 1)], mod2[1, D * k:D * (k + 1)]) for k in range(6)]


def _layer_fwd(X, mod2, c, rpb, tabs, L, ctx_out):
    cos, sin, rm, tri2 = tabs
    sh1, sc1, gt1, sh2, sc2, gt2 = _mods(mod2)
    biasd = _na_bias(rpb, L // GW)
    qa, qb, z, ka, va, kb, vb, xbc, dt, h1 = in_fwd(X, c["gmix"], sh1, sc1, c["win"], cos, sin, rm, L)
    oa = wa_fwd(qa, ka, va, c["sinkp"], L)
    ob = na_fwd(qb, kb, vb, biasd, L)
    if ctx_out:
        oa_c = ctx_fwd(qa, ka, va, c["sinkp"], True, L)
        ob_c = ctx_fwd(qb, kb, vb, c["nosink"], False, L)
    else:
        oa_c = ob_c = jnp.zeros((LC, 256), F32)
    oa = jnp.concatenate([oa, oa_c], axis=0)
    ob = jnp.concatenate([ob, ob_c], axis=0)
    act = conv_fwd(xbc, c["w8"], c["cb"], L)
    y2, hs = ssd_fwd(act, dt, c["dtb"], c["alog"], tri2, L)
    X1, cat = out_fwd(oa, ob, y2, act, z, c["dsk"], c["gs"], c["wout"], X, gt1, L)
    X2 = ffn_fwd(X1, c["gffn"], sh2, sc2, gt2, c["wfi"], c["wfo"], L)
    saved = dict(X=X, X1=X1, qa=qa, qb=qb, z=z, ka=ka, va=va, kb=kb, vb=vb, xbc=xbc, dt=dt, h1=h1, oa=oa, ob=ob,
                 act=act, y2=y2, hs=hs, cat=cat, biasd=biasd)
    return X2, saved


def _layer_bwd(dX2, s, mod2, c, tabs, L, ctx_out):
    cos, sin, rm, tri2 = tabs
    sh1, sc1, gt1, sh2, sc2, gt2 = _mods(mod2)
    R = L // GW
    dX1, h2, dU, actf, dOut, dgffn, dsh2, dsc2, dgt2 = ffn_bwd(s["X1"], c["gffn"], sh2, sc2, gt2, c["wfi"], c["wfo"],
                                                               dX2, L)
    g = {}
    g["w_ffn_in"] = tn_mm(h2, dU, 1408, MXU)
    g["w_ffn_out"] = tn_mm(actf, dOut, 512, MXU)
    doa, dob, dy, dxs_skip, dz, dmix, ddsk, dgs, dgt1 = out_bwd(s["oa"], s["ob"], s["y2"], s["act"], s["z"], c["dsk"],
                                                                c["gs"], c["wout"], gt1, dX1, L)
    dwout = tn_mm(s["cat"], dmix, 512, MXU)
    g["w_out"] = jnp.concatenate([dwout[_QA_PERM, :], dwout[256:, :]], axis=0)
    dS, ddt2, ddtb, dal = ssd_bwd(s["act"], s["dt"], c["dtb"], c["alog"], tri2, s["hs"], dy, L)
    dxbc, dw8, dcb = conv_bwd(s["xbc"], c["w8"], c["cb"], dS, dxs_skip, L)
    dqa, dkpad, dvpad, dkxa, dvxa, dska = wa_bwd(s["qa"], s["ka"], s["va"], c["sinkp"], doa, L)
    dqb, dkb, dvb, dkxb, dvxb, dbias = na_bwd(s["qb"], s["kb"], s["vb"], s["biasd"], dob, L)
    if ctx_out:
        dqa_c, dk1, dv1, dsk1 = ctx_bwd(s["qa"], s["ka"], s["va"], c["sinkp"], doa, True, L)
        dqb_c, dk2, dv2, _ = ctx_bwd(s["qb"], s["kb"], s["vb"], c["nosink"], dob, False, L)
        dkxa, dvxa, dska = dkxa + dk1, dvxa + dv1, dska + dsk1
        dkxb, dvxb = dkxb + dk2, dvxb + dv2
    else:
        dqa_c = dqb_c = jnp.zeros((LC, 256), F32)
    cat0 = lambda a, b: jnp.concatenate([a, b], axis=0)
    dX, dycat, dgmix, dsh1, dsc1 = in_bwd(
        s["X"], c["gmix"], sh1, sc1, c["win"], cos, sin, rm, dX1, cat0(dqa, dqa_c), cat0(dqb, dqb_c), dz,
        cat0(dkpad[Q:L + Q], dkxa), cat0(dvpad[Q:L + Q], dvxa), cat0(dkb, dkxb), cat0(dvb, dvxb), dxbc, ddt2, L)
    dwin = tn_mm(s["h1"], dycat, 1024, MXU)
    g["w_in"] = jnp.concatenate([dwin[:, _QA_PERM], dwin[:, 256:IN_COLS]], axis=1)
    g["g_mix"] = dgmix.reshape(D)
    g["g_ffn"] = dgffn.reshape(D)
    sk = jnp.sum(dska, axis=(2, 3))
    g["wa_sink"] = jnp.zeros((4,), F32).at[_PAIR_HEADS.reshape(-1)].set(sk.reshape(-1))
    g["na_rpb"] = _na_bias_grad(dbias, R)
    g["ssm_conv_w"] = dw8[:7]
    g["ssm_conv_b"] = dcb.reshape(1024)
    g["ssm_dt_bias"] = (ddtb[0] + ddtb[1])[0, :16].reshape(2, 8)
    g["ssm_a_log"] = (dal[0] + dal[1])[0, :16].reshape(2, 8)
    g["ssm_d"] = jnp.sum(ddsk.reshape(8, HD), axis=1)
    g["ssm_norm_g"] = dgs.reshape(512)
    dmod2 = jnp.concatenate([dsh1, dsc1, dgt1, dsh2, dsc2, dgt2], axis=2).reshape(2, 6 * D)
    return dX, g, dmod2


def local_step(x, ctx, tgt, mods, layers, g_final, L):
    tabs = _tables(L)
    X = jnp.concatenate([x, ctx], axis=0)
    consts = [_layer_consts(p) for p in layers]
    saved = []
    for i in range(2):
        X, s = _layer_fwd(X, mods[i], consts[i], layers[i]["na_rpb"], tabs, L, ctx_out=(i == 0))
        saved.append(s)
    loss8, dxl, dgfin = loss_head(X, g_final.reshape(1, D), tgt, L)
    dX = jnp.concatenate([dxl, jnp.zeros((LC, D), F32)], axis=0)
    grads, dmods = [None, None], [None, None]
    for i in (1, 0):
        dX, grads[i], dmods[i] = _layer_bwd(dX, saved[i], mods[i], consts[i], tabs, L, ctx_out=(i == 0))
    return loss8[0, 0], dX[:L], grads, jnp.stack(dmods), dgfin.reshape(D)


def _place():
    x, y, c = lax.axis_index("x"), lax.axis_index("y"), lax.axis_index("c")
    return x, y, c


def _slot(b):
    return 4 * b[0] + 2 * b[1] + b[2]


def _any():
    return pl.BlockSpec(memory_space=pl.ANY)


def all_gather(xs, name):
    n = len(xs)

    def body(*refs):
        x_refs, o_refs = refs[:n], refs[n:2 * n]
        send_sems, recv_sems, local_sems = refs[2 * n:]
        x, y, c = _place()
        me, sib = (x, y, c), (x, y, 1 - c)
        chips = [(1 - x, y), (x, 1 - y), (1 - x, 1 - y)]

        def copy(t, k, blk, to, src=None):
            dst = o_refs[t].at[_slot(blk)]
            return pltpu.make_async_remote_copy(
                src_ref=dst if src is None else src, dst_ref=dst, send_sem=send_sems.at[7 * t + k],
                recv_sem=recv_sems.at[7 * t + k], device_id=to, device_id_type=MESH_T)

        mine = [pltpu.make_async_copy(x_refs[t], o_refs[t].at[_slot(me)], local_sems.at[t]) for t in range(n)]
        for cp in mine:
            cp.start()
        first = []
        for t in range(n):
            first.append(copy(t, 0, me, sib, src=x_refs[t]))
            first += [copy(t, 1 + j, me, (*chip, c), src=x_refs[t]) for j, chip in enumerate(chips)]
        for cp in first:
            cp.start()
        passed = []
        for j, chip in enumerate(chips):
            for t in range(n):
                copy(t, 1 + j, (*chip, c), me).wait_recv()
                cp = copy(t, 4 + j, (*chip, c), sib)
                cp.start()
                passed.append(cp)
        for t in range(n):
            copy(t, 0, sib, me).wait_recv()
            for j, chip in enumerate(chips):
                copy(t, 4 + j, (*chip, 1 - c), me).wait_recv()
        for cp in first + passed:
            cp.wait_send()
        for cp in mine:
            cp.wait()

    return pl.pallas_call(
        body, name=name, out_shape=[_sds((NDEV,) + a.shape, a.dtype) for a in xs],
        in_specs=[_any()] * n, out_specs=[_any()] * n,
        scratch_shapes=[pltpu.SemaphoreType.DMA((7 * n,)), pltpu.SemaphoreType.DMA((7 * n,)),
                        pltpu.SemaphoreType.DMA((n,))],
        interpret=_INTERPRET)(*xs)


def all_to_all(xs, name):
    n = len(xs)

    def body(*refs):
        x_refs, o_refs = refs[:n], refs[n:2 * n]
        send_sems, recv_sems, local_sems = refs[2 * n:]
        x, y, c = _place()
        me = (x, y, c)
        flip = lambda v, b: (1 - v) if b else v
        peers = [(flip(x, k >> 2 & 1), flip(y, k >> 1 & 1), flip(c, k & 1)) for k in range(1, NDEV)]
        mine = [pltpu.make_async_copy(x_refs[t].at[_slot(me)], o_refs[t].at[_slot(me)], local_sems.at[t])
                for t in range(n)]
        for cp in mine:
            cp.start()

        def copy(t, k, src_slot, dst_slot, to):
            return pltpu.make_async_remote_copy(
                src_ref=x_refs[t].at[src_slot], dst_ref=o_refs[t].at[dst_slot], send_sem=send_sems.at[7 * t + k],
                recv_sem=recv_sems.at[7 * t + k], device_id=to, device_id_type=MESH_T)

        sends = [copy(t, k, _slot(p), _slot(me), p) for t in range(n) for k, p in enumerate(peers)]
        for cp in sends:
            cp.start()
        for t in range(n):
            for k, p in enumerate(peers):
                copy(t, k, _slot(p), _slot(p), me).wait_recv()
        for cp in sends:
            cp.wait_send()
        for cp in mine:
            cp.wait()

    return pl.pallas_call(
        body, name=name, out_shape=[_sds(a.shape, a.dtype) for a in xs],
        in_specs=[_any()] * n, out_specs=[_any()] * n,
        scratch_shapes=[pltpu.SemaphoreType.DMA((7 * n,)), pltpu.SemaphoreType.DMA((7 * n,)),
                        pltpu.SemaphoreType.DMA((n,))],
        interpret=_INTERPRET)(*xs)


def adam_reduce(P, w, m, v, name):
    n, R, C = P.shape
    br = R // 4 if R % 64 == 0 else R

    def body(p_ref, w_ref, m_ref, v_ref, g_o, d_o, m_o, v_o):
        g = p_ref[0].astype(F32)
        for k in range(1, n):
            g = g + p_ref[k].astype(F32)
        m1 = ADAM_B1 * m_ref[...] + (1.0 - ADAM_B1) * g
        v1 = ADAM_B2 * v_ref[...] + (1.0 - ADAM_B2) * jnp.square(g)
        m_hat = m1 / (1.0 - ADAM_B1 ** ADAM_STEP)
        v_hat = v1 / (1.0 - ADAM_B2 ** ADAM_STEP)
        g_o[...] = g
        d_o[...] = -ADAM_LR * (m_hat / (jnp.sqrt(v_hat) + ADAM_EPS) + ADAM_WD * w_ref[...])
        m_o[...] = m1
        v_o[...] = v1

    blk = pl.BlockSpec((br, C), lambda i: (i, 0))
    return _pc(body, name, [_sds((R, C))] * 4, grid=(R // br,),
               in_specs=[pl.BlockSpec((n, br, C), lambda i: (0, i, 0)), blk, blk, blk], out_specs=[blk] * 4)(P, w, m, v)


def mod_fwd(scin, wmod, bcol):
    def body(s_ref, w_ref, b_ref, o_ref):
        o_ref[0] = mm(_silu(s_ref[...]), w_ref[0]) + b_ref[0]

    return _pc(body, "mod_fwd", _sds((2, 16, 768)), grid=(2,),
               in_specs=[pl.BlockSpec((16, D), lambda l: (0, 0)), pl.BlockSpec((1, D, 768), lambda l: (l, 0, 0)),
                         pl.BlockSpec((1, 1, 768), lambda l: (l, 0, 0))],
               out_specs=pl.BlockSpec((1, 16, 768), lambda l: (l, 0, 0)))(scin, wmod, bcol)


def mod_bwd(scin, wmod, G):
    def body(s_ref, w_ref, g_ref, dw_o, ds_o):
        _, vjp = jax.vjp(lambda s, w: mm(_silu(s), w), s_ref[...], w_ref[0])
        ds, dw = vjp(g_ref[0])
        dw_o[0] = dw
        _acc_init(pl.program_id(0) == 0, [ds_o])
        ds_o[...] += ds

    full = pl.BlockSpec((16, D), lambda l: (0, 0))
    wsp = pl.BlockSpec((1, D, 768), lambda l: (l, 0, 0))
    return _pc(body, "mod_bwd", [_sds((2, D, 768)), _sds((16, D))], grid=(2,),
               in_specs=[full, wsp, pl.BlockSpec((1, 16, 768), lambda l: (l, 0, 0))], out_specs=[wsp, full])(
        scin, wmod, G)


_SMALL = ["b_mod", "g_mix", "wa_sink", "na_rpb", "ssm_conv_w", "ssm_conv_b", "ssm_dt_bias", "ssm_a_log", "ssm_d",
          "ssm_norm_g", "g_ffn", "g_final", "dmod_s", "dmod_c"]


def _pack(parts):
    rows = []
    for a in parts:
        f = a.reshape(-1).astype(F32)
        rows.append(jnp.pad(f, (0, (-f.shape[0]) % 1024)).reshape(-1, 128))
    return jnp.concatenate(rows, axis=0)


def _unpack(packed, shapes):
    out, r = [], 0
    for s in shapes:
        nel = int(np.prod(s))
        nr = -(-nel // 1024) * 8
        out.append(packed[r:r + nr].reshape(-1)[:nel].reshape(s))
        r += nr
    return out


def kernel(x, c, ctx, c_ctx, w_mod, b_mod, g_mix, w_in, wa_sink, na_rpb, ssm_conv_w, ssm_conv_b, ssm_dt_bias, ssm_a_log, ssm_d, ssm_norm_g, w_out, g_ffn, w_ffn_in, w_ffn_out, g_final, loss_target, m_c_ctx, m_w_mod, m_b_mod, m_g_mix, m_w_in, m_wa_sink, m_na_rpb, m_ssm_conv_w, m_ssm_conv_b, m_ssm_dt_bias, m_ssm_a_log, m_ssm_d, m_ssm_norm_g, m_w_out, m_g_ffn, m_w_ffn_in, m_w_ffn_out, m_g_final, v_c_ctx, v_w_mod, v_b_mod, v_g_mix, v_w_in, v_wa_sink, v_na_rpb, v_ssm_conv_w, v_ssm_conv_b, v_ssm_dt_bias, v_ssm_a_log, v_ssm_d, v_ssm_norm_g, v_w_out, v_g_ffn, v_w_ffn_in, v_w_ffn_out, v_g_final):
    L = x.shape[1]
    px, py, pc = _place()
    me = 4 * px + 2 * py + pc
    W = dict(c_ctx=c_ctx, w_mod=w_mod, b_mod=b_mod, g_mix=g_mix, w_in=w_in, wa_sink=wa_sink, na_rpb=na_rpb,
             ssm_conv_w=ssm_conv_w, ssm_conv_b=ssm_conv_b, ssm_dt_bias=ssm_dt_bias, ssm_a_log=ssm_a_log, ssm_d=ssm_d,
             ssm_norm_g=ssm_norm_g, w_out=w_out, g_ffn=g_ffn, w_ffn_in=w_ffn_in, w_ffn_out=w_ffn_out, g_final=g_final)
    M = dict(c_ctx=m_c_ctx, w_mod=m_w_mod, b_mod=m_b_mod, g_mix=m_g_mix, w_in=m_w_in, wa_sink=m_wa_sink,
             na_rpb=m_na_rpb, ssm_conv_w=m_ssm_conv_w, ssm_conv_b=m_ssm_conv_b, ssm_dt_bias=m_ssm_dt_bias,
             ssm_a_log=m_ssm_a_log, ssm_d=m_ssm_d, ssm_norm_g=m_ssm_norm_g, w_out=m_w_out, g_ffn=m_g_ffn,
             w_ffn_in=m_w_ffn_in, w_ffn_out=m_w_ffn_out, g_final=m_g_final)
    V = dict(c_ctx=v_c_ctx, w_mod=v_w_mod, b_mod=v_b_mod, g_mix=v_g_mix, w_in=v_w_in, wa_sink=v_wa_sink,
             na_rpb=v_na_rpb, ssm_conv_w=v_ssm_conv_w, ssm_conv_b=v_ssm_conv_b, ssm_dt_bias=v_ssm_dt_bias,
             ssm_a_log=v_ssm_a_log, ssm_d=v_ssm_d, ssm_norm_g=v_ssm_norm_g, w_out=v_w_out, g_ffn=v_g_ffn,
             w_ffn_in=v_w_ffn_in, w_ffn_out=v_w_ffn_out, g_final=v_g_final)

    c_all, conv_all = all_gather([c, ssm_conv_w], "gather_small")
    big = all_gather([w_in.astype(MXU), w_out.astype(MXU), w_ffn_in.astype(MXU), w_ffn_out.astype(MXU)],
                     "gather_weights")
    w_in_f = big[0].transpose(1, 2, 0, 3).reshape(2, D, IN_COLS)
    w_out_f = big[1].transpose(1, 0, 2, 3).reshape(2, D, D)
    w_fi_f = big[2].transpose(1, 2, 0, 3).reshape(2, D, 2 * DFF)
    w_fo_f = big[3].transpose(1, 0, 2, 3).reshape(2, DFF, D)
    conv_f = conv_all.transpose(1, 2, 0, 3).reshape(2, 7, 1024)

    scin = jnp.concatenate([c_all.reshape(NDEV, D), c_ctx.reshape(1, D), jnp.zeros((7, D), F32)], axis=0)
    bcol = lax.dynamic_slice_in_dim(b_mod, me * 768, 768, axis=1).reshape(2, 1, 768)
    mod_all, = all_gather([mod_fwd(scin, w_mod, bcol)], "gather_mod")
    mod_rows = mod_all.transpose(1, 2, 0, 3).reshape(2, 16, 6 * D)
    mods = jnp.stack([lax.dynamic_index_in_dim(mod_rows, me, axis=1, keepdims=False), mod_rows[:, 8]], axis=1)

    layers = [dict(w_in=w_in_f[i], w_out=w_out_f[i], w_ffn_in=w_fi_f[i], w_ffn_out=w_fo_f[i], g_mix=g_mix[i],
                   wa_sink=wa_sink[i], na_rpb=na_rpb[i], ssm_conv_w=conv_f[i], ssm_conv_b=ssm_conv_b[i],
                   ssm_dt_bias=ssm_dt_bias[i], ssm_a_log=ssm_a_log[i], ssm_d=ssm_d[i], ssm_norm_g=ssm_norm_g[i],
                   g_ffn=g_ffn[i]) for i in range(2)]
    loss, dx, grads, dmods, dgfin = local_step(x[0], ctx[0], loss_target[0], mods, layers, g_final, L)
    loss = lax.psum(loss, ("x", "y", "c"))

    stk = lambda n: jnp.stack([grads[0][n], grads[1][n]])
    small = dict(b_mod=dmods[:, 0] + dmods[:, 1], g_final=dgfin, dmod_s=dmods[:, 0], dmod_c=dmods[:, 1])
    for nme in _SMALL:
        if nme not in small:
            small[nme] = stk(nme)
    shapes = [small[nme].shape for nme in _SMALL]
    zero_like = lambda nme: jnp.zeros(small[nme].shape, F32)
    own = lambda S, nme: S[nme] if (nme in S and S[nme].shape == small[nme].shape) else zero_like(nme)
    gath, = all_gather([_pack([small[nme] for nme in _SMALL])], "gather_grads")
    sm = adam_reduce(gath, _pack([own(W, nme) for nme in _SMALL]), _pack([own(M, nme) for nme in _SMALL]),
                     _pack([own(V, nme) for nme in _SMALL]), "adam_small")
    res = {nme: vals for nme, vals in zip(_SMALL, zip(*[_unpack(a, shapes) for a in sm]))}

    cols = lambda a: lax.dynamic_slice_in_dim(a, me * 768, 768, axis=-1)
    gparts = [_unpack(gath[d], shapes) for d in range(NDEV)]
    dmod_s_all = jnp.stack([gparts[d][_SMALL.index("dmod_s")] for d in range(NDEV)], axis=1)
    G = jnp.concatenate([cols(dmod_s_all), cols(res["dmod_c"][0])[:, None, :], jnp.zeros((2, 7, 768), F32)], axis=1)
    dwmod, dscin = mod_bwd(scin, w_mod, G)
    cc_g, = all_gather([dscin[8].reshape(8, 128)], "gather_cctx")
    out = {}
    out["c_ctx"] = [a.reshape(D) for a in adam_reduce(cc_g, c_ctx.reshape(8, 128), m_c_ctx.reshape(8, 128),
                                                      v_c_ctx.reshape(8, 128), "adam_cctx")]
    out["w_mod"] = [a.reshape(2, D, 768) for a in adam_reduce(
        dwmod.reshape(1, 2 * D, 768), w_mod.reshape(2 * D, 768), m_w_mod.reshape(2 * D, 768),
        v_w_mod.reshape(2 * D, 768), "adam_wmod")]
    gconv = lax.dynamic_slice_in_dim(res["ssm_conv_w"][0], me * 128, 128, axis=2)
    out["ssm_conv_w"] = [a.reshape(2, 7, 128) for a in adam_reduce(
        gconv.reshape(1, 14, 128), ssm_conv_w.reshape(14, 128), m_ssm_conv_w.reshape(14, 128),
        v_ssm_conv_w.reshape(14, 128), "adam_conv")]
    for nme in _SMALL:
        if nme not in ("ssm_conv_w", "dmod_s", "dmod_c"):
            out[nme] = list(res[nme])

    gin = stk("w_in").reshape(2, D, NDEV, IN_COLS // NDEV).transpose(2, 0, 1, 3)
    gfi = stk("w_ffn_in").reshape(2, D, NDEV, 2 * DFF // NDEV).transpose(2, 0, 1, 3)
    gout = stk("w_out").reshape(2, NDEV, D // NDEV, D).transpose(1, 0, 2, 3)
    gfo = stk("w_ffn_out").reshape(2, NDEV, DFF // NDEV, D).transpose(1, 0, 2, 3)
    rin, rout, rfi, rfo = all_to_all([gin, gout, gfi, gfo], "exchange_grads")
    for nme, r in (("w_in", rin), ("w_out", rout), ("w_ffn_in", rfi), ("w_ffn_out", rfo)):
        shp = W[nme].shape
        r2 = (shp[0] * shp[1], shp[2])
        out[nme] = [a.reshape(shp) for a in adam_reduce(r.reshape((NDEV,) + r2), W[nme].reshape(r2),
                                                        M[nme].reshape(r2), V[nme].reshape(r2), "adam_" + nme)]
    order = ["c_ctx", "w_mod", "b_mod", "g_mix", "w_in", "wa_sink", "na_rpb", "ssm_conv_w", "ssm_conv_b",
             "ssm_dt_bias", "ssm_a_log", "ssm_d", "ssm_norm_g", "w_out", "g_ffn", "w_ffn_in", "w_ffn_out", "g_final"]
    return (loss, dx.reshape(1, L, D), *[out[nme][0] for nme in order], *[out[nme][1] for nme in order],
            *[out[nme][2] for nme in order], *[out[nme][3] for nme in order])
```

```python
import functools
import math

import numpy as np
import jax
import jax.numpy as jnp
from jax import lax
from jax.experimental import pallas as pl
from jax.experimental.pallas import tpu as pltpu

F32 = jnp.float32
MXU = jnp.bfloat16
_INTERPRET = False
VMEM_LIMIT = 60 * 1024 * 1024

D = 1024
LC = 256
GW = 64
HD = 64
EPS = 1e-6
NEG = -1e30
NDEV = 8
Q = 128
NSTATE = 128
DFF = 2816
IN_COLS = 2832
NP_IN = 3072
C_QA, C_QB, C_Z, C_KA, C_VA, C_KB, C_VB, C_XBC, C_DT = 0, 256, 512, 1024, 1152, 1280, 1536, 1792, 2816
ADAM_LR, ADAM_B1, ADAM_B2, ADAM_EPS, ADAM_WD, ADAM_STEP = 0.001, 0.9, 0.999, 1e-08, 0.01, 10
MESH_T = pl.DeviceIdType.MESH


def _dg(a, b, ca, cb):
    return lax.dot_general(a.astype(MXU), b.astype(MXU), (((ca,), (cb,)), ((), ())), preferred_element_type=F32)


@jax.custom_vjp
def mm(a, b):
    return _dg(a, b, 1, 0)


def _mm_f(a, b):
    return _dg(a, b, 1, 0), (a, b)


def _mm_b(res, g):
    a, b = res
    return _dg(g, b, 1, 1).astype(a.dtype), _dg(a, g, 0, 0).astype(b.dtype)


mm.defvjp(_mm_f, _mm_b)


@jax.custom_vjp
def mm_nt(a, b):
    return _dg(a, b, 1, 1)


def _mmnt_f(a, b):
    return _dg(a, b, 1, 1), (a, b)


def _mmnt_b(res, g):
    a, b = res
    return _dg(g, b, 1, 0).astype(a.dtype), _dg(g, a, 0, 0).astype(b.dtype)


mm_nt.defvjp(_mmnt_f, _mmnt_b)


@jax.custom_vjp
def mm_tn(a, b):
    return _dg(a, b, 0, 0)


def _mmtn_f(a, b):
    return _dg(a, b, 0, 0), (a, b)


def _mmtn_b(res, g):
    a, b = res
    return _dg(b, g, 1, 1).astype(a.dtype), _dg(a, g, 1, 0).astype(b.dtype)


mm_tn.defvjp(_mmtn_f, _mmtn_b)


def _exact(a, b):
    return lax.dot_general(a, b, (((1,), (0,)), ((), ())), precision=lax.Precision.HIGHEST,
                           preferred_element_type=F32)


def _pc(body, name, out_shape, grid=None, in_specs=None, out_specs=None, scratch=(), sends=()):
    params = pltpu.CompilerParams(vmem_limit_bytes=VMEM_LIMIT)
    if not sends:
        kw = {}
        if grid is not None:
            kw = dict(grid=grid, in_specs=in_specs, out_specs=out_specs)
        elif in_specs is not None:
            kw = dict(in_specs=in_specs, out_specs=out_specs)
        return pl.pallas_call(body, name=name, out_shape=out_shape, scratch_shapes=list(scratch),
                              compiler_params=params, interpret=_INTERPRET, **kw)
    n, nin, nout, nscr = len(sends), len(in_specs), len(out_shape), len(scratch)

    def body2(*refs):
        cin, xs = refs[:nin], refs[nin:nin + n]
        couts, os_ = refs[nin + n:nin + n + nout], refs[nin + n + nout:nin + 2 * n + nout]
        cscr, sems = refs[nin + 2 * n + nout:nin + 2 * n + nout + nscr], refs[nin + 2 * n + nout + nscr:]
        ids = [pl.program_id(a) for a in range(len(grid))]
        first = functools.reduce(lambda a, b: a & b, [i == 0 for i in ids])
        last = functools.reduce(lambda a, b: a & b, [i == g - 1 for i, g in zip(ids, grid)])

        @pl.when(first)
        def _():
            _a2a_start(xs, os_, *sems)

        body(*cin, *couts, *cscr)

        @pl.when(last)
        def _():
            _a2a_wait(xs, os_, *sems)

    call = pl.pallas_call(
        body2, name=name, out_shape=list(out_shape) + [_sds(a.shape, a.dtype) for a in sends],
        grid=grid, in_specs=list(in_specs) + [_any()] * n, out_specs=list(out_specs) + [_any()] * n,
        scratch_shapes=list(scratch) + _a2a_sems(n), compiler_params=params, interpret=_INTERPRET)

    def run(*args):
        res = call(*args, *sends)
        return res[:nout], res[nout:]

    return run


def _vm():
    return pl.BlockSpec(memory_space=pltpu.VMEM)


def _sds(shape, dt=F32):
    return jax.ShapeDtypeStruct(shape, dt)


def _iota(shape, dim):
    return lax.broadcasted_iota(jnp.int32, shape, dim)


def _silu(x):
    return x * jax.nn.sigmoid(x)


def _softplus(x):
    return jnp.maximum(x, 0.0) + jnp.log1p(jnp.exp(-jnp.abs(x)))


def _normmod(x, g, sh, sc):
    r = lax.rsqrt(jnp.mean(x * x, axis=-1, keepdims=True) + EPS)
    return (x * r * g) * (1.0 + sc) + sh


def _rope(x, cos, sin, rm):
    return x * cos + _exact(x, rm) * sin


def _acc_init(first, refs):
    @pl.when(first)
    def _():
        for r in refs:
            r[...] = jnp.zeros_like(r)


def in_fwd(X, g, sh, sc, W, cos, sin, rm, L):
    T = X.shape[0]
    TR = 256
    nlt = L // TR

    def body(x_ref, g_ref, sh_ref, sc_ref, w_ref, cos_ref, sin_ref, rm_ref,
             qa, qb, z, ka, va, kb, vb, xbc, dt, hout):
        h = _normmod(x_ref[...], g_ref[...], sh_ref[0], sc_ref[0]).astype(MXU)
        hout[...] = h
        y = jnp.dot(h, w_ref[...], preferred_element_type=F32)
        cs, sn, r = cos_ref[...], sin_ref[...], rm_ref[...]
        qa[...] = _rope(y[:, C_QA:C_QB], cs, sn, r).astype(MXU)
        qb[...] = y[:, C_QB:C_Z].astype(MXU)
        z[...] = y[:, C_Z:C_KA]
        ka[...] = _rope(y[:, C_KA:C_VA], cs[:, :128], sn[:, :128], r[:128, :128]).astype(MXU)
        va[...] = y[:, C_VA:C_KB].astype(MXU)
        kb[...] = y[:, C_KB:C_VB].astype(MXU)
        vb[...] = y[:, C_VB:C_XBC].astype(MXU)
        xbc[...] = y[:, C_XBC:C_DT]
        dt[...] = y[:, C_DT:C_DT + 128]

    row = lambda w: pl.BlockSpec((TR, w), lambda i: (i, 0))
    cls = pl.BlockSpec((1, 1, D), lambda i: (i // nlt, 0, 0))
    widths = [(256, MXU), (256, MXU), (512, F32), (128, MXU), (128, MXU), (256, MXU), (256, MXU), (1024, F32),
              (128, F32), (D, MXU)]
    return _pc(body, "in_fwd", [_sds((T, w), d) for w, d in widths], grid=(T // TR,),
               in_specs=[row(D), pl.BlockSpec((1, D), lambda i: (0, 0)), cls, cls, _vm(), row(256), row(256), _vm()],
               out_specs=[row(w) for w, _ in widths])(X, g, sh, sc, W, cos, sin, rm)


def in_bwd(X, g, sh, sc, W, cos, sin, rm, dxres, dqa, dqb, dz, dka, dva, dkb, dvb, dxbc, ddt2, L):
    T = X.shape[0]
    TR = 256
    nlt = L // TR

    def body(x_ref, g_ref, sh_ref, sc_ref, w_ref, cos_ref, sin_ref, rm_ref, dxres_ref, dqa_r, dqb_r, dz_r, dka_r,
             dva_r, dkb_r, dvb_r, dxbc_r, ddt0_r, ddt1_r, dx_o, dy_o, dg_o, dsh_o, dsc_o):
        i = pl.program_id(0)
        cs, sn, r = cos_ref[...], sin_ref[...], rm_ref[...]
        _, vq = jax.vjp(lambda t: _rope(t, cs, sn, r), dqa_r[...])
        _, vk = jax.vjp(lambda t: _rope(t, cs[:, :128], sn[:, :128], r[:128, :128]), dka_r[...])
        dyqa, = vq(dqa_r[...])
        dyka, = vk(dka_r[...])
        ddt = ddt0_r[0] + ddt1_r[0]
        dy = jnp.concatenate([dyqa, dqb_r[...], dz_r[...], dyka, dva_r[...], dkb_r[...], dvb_r[...], dxbc_r[...],
                              ddt, jnp.zeros((TR, NP_IN - C_DT - 128), F32)], axis=1).astype(MXU)
        dy_o[...] = dy
        dh = lax.dot_general(dy, w_ref[...], (((1,), (1,)), ((), ())), preferred_element_type=F32)
        _, vp = jax.vjp(_normmod, x_ref[...], g_ref[...], sh_ref[0], sc_ref[0])
        dx, dg, dsh, dsc = vp(dh)
        dx_o[...] = dx + dxres_ref[...]
        _acc_init(i == 0, [dg_o])
        _acc_init((i == 0) | (i == nlt), [dsh_o, dsc_o])
        dg_o[...] += dg
        dsh_o[0] += dsh
        dsc_o[0] += dsc

    row = lambda w: pl.BlockSpec((TR, w), lambda i: (i, 0))
    cls = pl.BlockSpec((1, 1, D), lambda i: (i // nlt, 0, 0))
    vec = pl.BlockSpec((1, D), lambda i: (0, 0))
    dts = lambda d: pl.BlockSpec((1, TR, 128), lambda i: (d, i, 0))
    return _pc(body, "in_bwd",
               [_sds((T, D)), _sds((T, NP_IN), MXU), _sds((1, D)), _sds((2, 1, D)), _sds((2, 1, D))],
               grid=(T // TR,),
               in_specs=[row(D), vec, cls, cls, _vm(), row(256), row(256), _vm(), row(D), row(256), row(256), row(512),
                         row(128), row(128), row(256), row(256), row(1024), dts(0), dts(1)],
               out_specs=[row(D), row(NP_IN), vec, cls, cls])(
        X, g, sh, sc, W, cos, sin, rm, dxres, dqa, dqb, dz, dka, dva, dkb, dvb, dxbc, ddt2, ddt2)


def tn_mm(A, G, bn, out_dtype):
    T, K = A.shape
    N = G.shape[1]
    bt = T // 4
    nt = T // bt

    def body(a_ref, g_ref, o_ref, acc):
        t = pl.program_id(1)
        _acc_init(t == 0, [acc])
        acc[...] += lax.dot_general(a_ref[...], g_ref[...], (((0,), (0,)), ((), ())), preferred_element_type=F32)

        @pl.when(t == nt - 1)
        def _():
            o_ref[...] = acc[...].astype(out_dtype)

    return _pc(body, "tn_mm", _sds((K, N), out_dtype), grid=(N // bn, nt),
               in_specs=[pl.BlockSpec((bt, K), lambda n, t: (t, 0)), pl.BlockSpec((bt, bn), lambda n, t: (t, n))],
               out_specs=pl.BlockSpec((K, bn), lambda n, t: (0, n)),
               scratch=[pltpu.VMEM((K, bn), F32)])(A, G)


def _ssm_out(yf, yb, xs, z, dsk, gs):
    y = (yf + yb + dsk * xs) * _silu(z)
    r = lax.rsqrt(jnp.mean(y * y, axis=-1, keepdims=True) + EPS)
    return y * r * gs


def out_fwd(oa, ob, y2, act, z, dsk, gs, W, X, gate, L):
    T = X.shape[0]
    TR = 256
    nlt = L // TR

    def body(oa_r, ob_r, yf_r, yb_r, xs_r, z_r, dsk_r, gs_r, w_ref, x_ref, gt_ref, x1_o, cat_o):
        oc = _ssm_out(yf_r[0], yb_r[0], xs_r[...], z_r[...], dsk_r[...], gs_r[...])
        cat = jnp.concatenate([oa_r[...], ob_r[...], oc], axis=1).astype(MXU)
        cat_o[...] = cat
        x1_o[...] = x_ref[...] + gt_ref[0] * jnp.dot(cat, w_ref[...], preferred_element_type=F32)

    row = lambda w: pl.BlockSpec((TR, w), lambda i: (i, 0))
    ys = lambda d: pl.BlockSpec((1, TR, 512), lambda i: (d, i, 0))
    cls = pl.BlockSpec((1, 1, D), lambda i: (i // nlt, 0, 0))
    v512 = pl.BlockSpec((1, 512), lambda i: (0, 0))
    return _pc(body, "out_fwd", [_sds((T, D)), _sds((T, D), MXU)], grid=(T // TR,),
               in_specs=[row(256), row(256), ys(0), ys(1), row(512), row(512), v512, v512, _vm(), row(D), cls],
               out_specs=[row(D), row(D)])(oa, ob, y2, y2, act, z, dsk, gs, W, X, gate)


def out_bwd(oa, ob, y2, act, z, dsk, gs, W, gate, dX1, L):
    T = dX1.shape[0]
    TR = 256
    nlt = L // TR

    def body(oa_r, ob_r, yf_r, yb_r, xs_r, z_r, dsk_r, gs_r, w_ref, gt_ref, dx1_r,
             doa_o, dob_o, dy_o, dxs_o, dz_o, dmix_o, ddsk_o, dgs_o, dgt_o):
        i = pl.program_id(0)
        w = w_ref[...]

        def f(oa_, ob_, yf, yb, xs, z_, dsk_, gs_, gt):
            oc = _ssm_out(yf, yb, xs, z_, dsk_, gs_)
            return gt * mm(jnp.concatenate([oa_, ob_, oc], axis=1), w)

        _, vjp = jax.vjp(f, oa_r[...], ob_r[...], yf_r[0], yb_r[0], xs_r[...], z_r[...], dsk_r[...], gs_r[...],
                         gt_ref[0])
        dx1 = dx1_r[...]
        doa, dob, dyf, _, dxs, dz, ddsk, dgs, dgt = vjp(dx1)
        doa_o[...] = doa
        dob_o[...] = dob
        dy_o[...] = dyf
        dxs_o[...] = dxs
        dz_o[...] = dz
        dmix_o[...] = (gt_ref[0] * dx1).astype(MXU)
        _acc_init(i == 0, [ddsk_o, dgs_o])
        _acc_init((i == 0) | (i == nlt), [dgt_o])
        ddsk_o[...] += ddsk
        dgs_o[...] += dgs
        dgt_o[0] += dgt

    row = lambda w: pl.BlockSpec((TR, w), lambda i: (i, 0))
    ys = lambda d: pl.BlockSpec((1, TR, 512), lambda i: (d, i, 0))
    cls = pl.BlockSpec((1, 1, D), lambda i: (i // nlt, 0, 0))
    v512 = pl.BlockSpec((1, 512), lambda i: (0, 0))
    return _pc(body, "out_bwd",
               [_sds((T, 256)), _sds((T, 256)), _sds((T, 512)), _sds((T, 512)), _sds((T, 512)), _sds((T, D), MXU),
                _sds((1, 512)), _sds((1, 512)), _sds((2, 1, D))],
               grid=(T // TR,),
               in_specs=[row(256), row(256), ys(0), ys(1), row(512), row(512), v512, v512, _vm(), cls, row(D)],
               out_specs=[row(256), row(256), row(512), row(512), row(512), row(D), v512, v512, cls])(
        oa, ob, y2, y2, act, z, dsk, gs, W, gate, dX1)


def _ffn_core(x, g, sh, sc, gt, wg, wu, wo, eg, eu):
    h = _normmod(x, g, sh, sc)
    a = mm(h, wg) + eg
    u = mm(h, wu) + eu
    act = _silu(a) * u
    return x + gt * mm(act, wo), (h, act)


def ffn_fwd(X, g, sh, sc, gate, Win, Wout, L):
    T = X.shape[0]
    TR = 256
    nlt = L // TR

    def body(x_ref, g_ref, sh_ref, sc_ref, gt_ref, wi_ref, wo_ref, o_ref):
        h = _normmod(x_ref[...], g_ref[...], sh_ref[0], sc_ref[0]).astype(MXU)
        a = jnp.dot(h, wi_ref[:, 0:DFF], preferred_element_type=F32)
        u = jnp.dot(h, wi_ref[:, DFF:2 * DFF], preferred_element_type=F32)
        act = (_silu(a) * u).astype(MXU)
        o_ref[...] = x_ref[...] + gt_ref[0] * jnp.dot(act, wo_ref[...], preferred_element_type=F32)

    row = lambda w: pl.BlockSpec((TR, w), lambda i: (i, 0))
    cls = pl.BlockSpec((1, 1, D), lambda i: (i // nlt, 0, 0))
    vec = pl.BlockSpec((1, D), lambda i: (0, 0))
    return _pc(body, "ffn_fwd", _sds((T, D)), grid=(T // TR,),
               in_specs=[row(D), vec, cls, cls, cls, _vm(), _vm()], out_specs=row(D))(X, g, sh, sc, gate, Win, Wout)


def ffn_bwd(X, g, sh, sc, gate, Win, Wout, dX2, L, sends=()):
    T = X.shape[0]
    TR = 256
    nlt = L // TR

    def body(x_ref, g_ref, sh_ref, sc_ref, gt_ref, wi_ref, wo_ref, dx2_r,
             dx_o, h_o, du_o, act_o, dout_o, dg_o, dsh_o, dsc_o, dgt_o):
        i = pl.program_id(0)
        wg, wu, wo = wi_ref[:, 0:DFF], wi_ref[:, DFF:2 * DFF], wo_ref[...]
        zero = jnp.zeros((TR, DFF), F32)
        f = lambda x, g_, sh_, sc_, gt, eg, eu: _ffn_core(x, g_, sh_, sc_, gt, wg, wu, wo, eg, eu)
        _, vjp, (h, act) = jax.vjp(f, x_ref[...], g_ref[...], sh_ref[0], sc_ref[0], gt_ref[0], zero, zero,
                                   has_aux=True)
        dx2 = dx2_r[...]
        dx, dg, dsh, dsc, dgt, da, du = vjp(dx2)
        dx_o[...] = dx
        h_o[...] = h.astype(MXU)
        du_o[...] = jnp.concatenate([da, du], axis=1).astype(MXU)
        act_o[...] = act.astype(MXU)
        dout_o[...] = (gt_ref[0] * dx2).astype(MXU)
        _acc_init(i == 0, [dg_o])
        _acc_init((i == 0) | (i == nlt), [dsh_o, dsc_o, dgt_o])
        dg_o[...] += dg
        dsh_o[0] += dsh
        dsc_o[0] += dsc
        dgt_o[0] += dgt

    row = lambda w: pl.BlockSpec((TR, w), lambda i: (i, 0))
    cls = pl.BlockSpec((1, 1, D), lambda i: (i // nlt, 0, 0))
    vec = pl.BlockSpec((1, D), lambda i: (0, 0))
    return _pc(body, "ffn_bwd",
               [_sds((T, D)), _sds((T, D), MXU), _sds((T, 2 * DFF), MXU), _sds((T, DFF), MXU), _sds((T, D), MXU),
                _sds((1, D)), _sds((2, 1, D)), _sds((2, 1, D)), _sds((2, 1, D))],
               grid=(T // TR,),
               in_specs=[row(D), vec, cls, cls, cls, _vm(), _vm(), row(D)],
               out_specs=[row(D), row(D), row(2 * DFF), row(DFF), row(D), vec, cls, cls, cls], sends=sends)(
        X, g, sh, sc, gate, Win, Wout, dX2)


def loss_head(X2, g, tgt, L):
    TR = 256

    def body(x_ref, g_ref, t_ref, loss_o, dx_o, dg_o):
        i = pl.program_id(0)

        def f(x, g_):
            y = x * lax.rsqrt(jnp.mean(x * x, axis=-1, keepdims=True) + EPS) * g_
            return 0.5 * jnp.sum(jnp.mean(jnp.square(y - t_ref[...]), axis=-1, keepdims=True), axis=0, keepdims=True)

        val, vjp = jax.vjp(f, x_ref[...], g_ref[...])
        dx, dg = vjp(jnp.ones((1, 1), F32))
        dx_o[...] = dx
        _acc_init(i == 0, [loss_o, dg_o])
        loss_o[...] += jnp.broadcast_to(val, (8, 128))
        dg_o[...] += dg

    row = pl.BlockSpec((TR, D), lambda i: (i, 0))
    vec = pl.BlockSpec((1, D), lambda i: (0, 0))
    return _pc(body, "loss_head", [_sds((8, 128)), _sds((L, D)), _sds((1, D))], grid=(L // TR,),
               in_specs=[row, vec, row], out_specs=[pl.BlockSpec((8, 128), lambda i: (0, 0)), row, vec])(X2, g, tgt)


def _stack_impl(q):
    lane = _iota(q.shape, 1)
    return jnp.concatenate([jnp.where(lane < HD, q, 0.0), jnp.where(lane >= HD, q, 0.0)], axis=0)


def _unstack_impl(o):
    M = o.shape[0] // 2
    return jnp.where(_iota((M, o.shape[1]), 1) < HD, o[:M], o[M:])


@jax.custom_vjp
def _stack(q):
    return _stack_impl(q)


_stack.defvjp(lambda q: (_stack_impl(q), None), lambda _, g: (_unstack_impl(g),))


@jax.custom_vjp
def _unstack(o):
    return _unstack_impl(o)


_unstack.defvjp(lambda o: (_unstack_impl(o), None), lambda _, g: (_stack_impl(g),))


def _softmax_av(q, ks, vs, biases, sink):
    q2 = _stack(q)
    ss = []
    for k, b in zip(ks, biases):
        s = mm_nt(q2, k) * (HD ** -0.5)
        ss.append(s if b is None else s + b)
    m = functools.reduce(jnp.maximum, [jnp.max(s, axis=1, keepdims=True) for s in ss])
    if sink is not None:
        m = jnp.maximum(m, sink)
    m = lax.stop_gradient(m)
    es = [jnp.exp(s - m) for s in ss]
    den = functools.reduce(lambda a, b_: a + b_, [jnp.sum(e, axis=1, keepdims=True) for e in es])
    if sink is not None:
        den = den + jnp.exp(sink - m)
    inv = 1.0 / den
    return _unstack(functools.reduce(lambda a, b_: a + b_, [mm(e * inv, v) for e, v in zip(es, vs)]))


def _sink_col(s0, s1, M):
    return jnp.concatenate([jnp.broadcast_to(jnp.mean(s0, axis=1, keepdims=True), (M, 1)),
                            jnp.broadcast_to(jnp.mean(s1, axis=1, keepdims=True), (M, 1))], axis=0)


def _stack4_impl(q):
    lane = _iota((q.shape[0], 128), 1)
    parts = []
    for p in range(2):
        qp = q[:, 128 * p:128 * (p + 1)]
        parts += [jnp.where(lane < HD, qp, 0.0), jnp.where(lane >= HD, qp, 0.0)]
    return jnp.concatenate(parts, axis=0)


def _unstack4_impl(o):
    M = o.shape[0] // 4
    lane = _iota((M, 128), 1)
    return jnp.concatenate([jnp.where(lane < HD, o[0:M], o[M:2 * M]),
                            jnp.where(lane < HD, o[2 * M:3 * M], o[3 * M:4 * M])], axis=1)


@jax.custom_vjp
def _stack4(q):
    return _stack4_impl(q)


_stack4.defvjp(lambda q: (_stack4_impl(q), None), lambda _, g: (_unstack4_impl(g),))


@jax.custom_vjp
def _unstack4(o):
    return _unstack4_impl(o)


_unstack4.defvjp(lambda o: (_unstack4_impl(o), None), lambda _, g: (_stack4_impl(g),))


def _wa_block(q, kp, kc, kn, vp, vc, vn, kx, vx, sks, n, L):
    kb = jnp.concatenate([kp, kc, kn], axis=0)
    vb = jnp.concatenate([vp, vc, vn], axis=0)
    qpos = n * Q + (_iota((4 * Q, 3 * Q), 0) & (Q - 1))
    kpos = (n - 1) * Q + _iota((4 * Q, 3 * Q), 1)
    valid = (jnp.abs(qpos - kpos) <= Q) & (kpos >= 0) & (kpos < L)
    bias = jnp.where(valid, 0.0, NEG)
    sink = jnp.concatenate([jnp.broadcast_to(jnp.mean(s_, axis=1, keepdims=True), (Q, 1)) for s_ in sks], axis=0)
    q4 = _stack4(q)
    sc = HD ** -0.5
    sl = mm_nt(q4, kb) * sc + bias
    sx = mm_nt(q4, kx) * sc
    m = lax.stop_gradient(jnp.maximum(jnp.maximum(jnp.max(sl, axis=1, keepdims=True),
                                                  jnp.max(sx, axis=1, keepdims=True)), sink))
    el, ex = jnp.exp(sl - m), jnp.exp(sx - m)
    inv = 1.0 / (jnp.sum(el, axis=1, keepdims=True) + jnp.sum(ex, axis=1, keepdims=True) + jnp.exp(sink - m))
    return _unstack4(mm(el * inv, vb) + mm(ex * inv, vx))


def _wa_specs(L):
    nb = L // Q
    qs = pl.BlockSpec((Q, 256), lambda n: (n, 0))
    kprev = pl.BlockSpec((Q, 128), lambda n: (jnp.maximum(n - 1, 0), 0))
    kcur = pl.BlockSpec((Q, 128), lambda n: (n, 0))
    knext = pl.BlockSpec((Q, 128), lambda n: (jnp.minimum(n + 1, nb - 1), 0))
    kctx = pl.BlockSpec((LC, 128), lambda n: (L // LC, 0))
    sks = pl.BlockSpec((2, 2, 1, 128), lambda n: (0, 0, 0, 0))
    return nb, qs, [kprev, kcur, knext], kctx, sks


def wa_fwd(QA, KA, VA, sinkp, L):
    nb, qs, kband, kctx, sks = _wa_specs(L)

    def body(q_r, kp, kc, kn, vp, vc, vn, kx, vx, sk_r, o_ref):
        n = pl.program_id(0)
        f = lambda t: t[...].astype(F32)
        o_ref[...] = _wa_block(f(q_r), f(kp), f(kc), f(kn), f(vp), f(vc), f(vn), f(kx), f(vx),
                               [sk_r[0, 0], sk_r[0, 1], sk_r[1, 0], sk_r[1, 1]], n, L)

    return _pc(body, "wa_fwd", _sds((L, 256)), grid=(nb,),
               in_specs=[qs] + kband + kband + [kctx, kctx, sks], out_specs=qs)(
        QA, KA, KA, KA, VA, VA, VA, KA, VA, sinkp)


def wa_bwd(QA, KA, VA, sinkp, dO, L, sends=()):
    nb, qs, kband, kctx, sks = _wa_specs(L)

    def body(q_r, kp, kc, kn, vp, vc, vn, kx, vx, sk_r, do_r, dq_o, dk_o, dv_o, dkx_o, dvx_o, dsk_o):
        n = pl.program_id(0)
        f = lambda t: t[...].astype(F32)
        fn = lambda q, a, b, c, d, e, g, kx_, vx_, s_: _wa_block(q, a, b, c, d, e, g, kx_, vx_, s_, n, L)
        _, vjp = jax.vjp(fn, f(q_r), f(kp), f(kc), f(kn), f(vp), f(vc), f(vn), f(kx), f(vx),
                         [sk_r[0, 0], sk_r[0, 1], sk_r[1, 0], sk_r[1, 1]])
        dq, dkp, dkc, dkn, dvp, dvc, dvn, dkx, dvx, ds = vjp(do_r[...])
        dq_o[...] = dq
        _acc_init(n == 0, [dk_o, dv_o, dkx_o, dvx_o, dsk_o])
        rows = pl.ds(pl.multiple_of(n * Q, Q), 3 * Q)
        dk_o[rows, :] += jnp.concatenate([dkp, dkc, dkn], axis=0)
        dv_o[rows, :] += jnp.concatenate([dvp, dvc, dvn], axis=0)
        dkx_o[...] += dkx
        dvx_o[...] += dvx
        for i_ in range(4):
            dsk_o[i_ // 2, i_ % 2] += ds[i_]

    full = lambda r: pl.BlockSpec((r, 128), lambda n: (0, 0))
    return _pc(body, "wa_bwd",
               [_sds((L, 256)), _sds((L + 2 * Q, 128)), _sds((L + 2 * Q, 128)), _sds((LC, 128)), _sds((LC, 128)),
                _sds((2, 2, 1, 128))],
               grid=(nb,), in_specs=[qs] + kband + kband + [kctx, kctx, sks, qs],
               out_specs=[qs, full(L + 2 * Q), full(L + 2 * Q), full(LC), full(LC), sks], sends=sends)(
        QA, KA, KA, KA, VA, VA, VA, KA, VA, sinkp, dO)


def _ctx_block(q, kx, vx, s0, s1):
    return _softmax_av(q, [kx], [vx], [None], _sink_col(s0, s1, LC))


def ctx_fwd(Qx, Kx, Vx, sinkp, shared, L):
    cq = pl.BlockSpec((LC, 128), lambda p: (L // LC, p))
    ck = pl.BlockSpec((LC, 128), lambda p: (L // LC, 0 if shared else p))
    sks = pl.BlockSpec((1, 2, 1, 128), lambda p: (p, 0, 0, 0))

    def body(q_r, k_r, v_r, sk_r, o_ref):
        f = lambda t: t[...].astype(F32)
        o_ref[...] = _ctx_block(f(q_r), f(k_r), f(v_r), sk_r[0, 0], sk_r[0, 1])

    return _pc(body, "ctx_fwd", _sds((LC, 256)), grid=(2,), in_specs=[cq, ck, ck, sks],
               out_specs=pl.BlockSpec((LC, 128), lambda p: (0, p)))(Qx, Kx, Vx, sinkp)


def ctx_bwd(Qx, Kx, Vx, sinkp, dO, shared, L):
    cq = pl.BlockSpec((LC, 128), lambda p: (L // LC, p))
    ck = pl.BlockSpec((LC, 128), lambda p: (L // LC, 0 if shared else p))
    sks = pl.BlockSpec((1, 2, 1, 128), lambda p: (p, 0, 0, 0))
    op = pl.BlockSpec((LC, 128), lambda p: (0, p))
    ok = pl.BlockSpec((LC, 128), lambda p: (0, 0 if shared else p))
    dos = pl.BlockSpec((LC, 128), lambda p: (L // LC, p))

    def body(q_r, k_r, v_r, sk_r, do_r, dq_o, dk_o, dv_o, dsk_o):
        p = pl.program_id(0)
        f = lambda t: t[...].astype(F32)
        _, vjp = jax.vjp(_ctx_block, f(q_r), f(k_r), f(v_r), sk_r[0, 0], sk_r[0, 1])
        dq, dk, dv, ds0, ds1 = vjp(do_r[...])
        dq_o[...] = dq
        _acc_init((p == 0) if shared else (p >= 0), [dk_o, dv_o])
        dk_o[...] += dk
        dv_o[...] += dv
        dsk_o[0, 0] = ds0
        dsk_o[0, 1] = ds1

    kw = 128 if shared else 256
    return _pc(body, "ctx_bwd", [_sds((LC, 256)), _sds((LC, kw)), _sds((LC, kw)), _sds((2, 2, 1, 128))],
               grid=(2,), in_specs=[cq, ck, ck, sks, dos], out_specs=[op, ok, ok, sks])(Qx, Kx, Vx, sinkp, dO)


def _na_rows(qs, kws, vws, kx, vx, bs):
    sc = HD ** -0.5
    q2 = [_stack(q) for q in qs]
    sl = [mm_nt(a, k) * sc + b for a, k, b in zip(q2, kws, bs)]
    sx = [mm_nt(a, kx) * sc for a in q2]
    m = [lax.stop_gradient(jnp.maximum(jnp.max(a, axis=1, keepdims=True), jnp.max(b, axis=1, keepdims=True)))
         for a, b in zip(sl, sx)]
    el = [jnp.exp(a - c) for a, c in zip(sl, m)]
    ex = [jnp.exp(a - c) for a, c in zip(sx, m)]
    inv = [1.0 / (jnp.sum(a, axis=1, keepdims=True) + jnp.sum(b, axis=1, keepdims=True)) for a, b in zip(el, ex)]
    o2 = [mm(a * i, v) + mm(b * i, vx) for a, b, i, v in zip(el, ex, inv, vws)]
    return [_unstack(o) for o in o2]


def _na_geom(rb, j, R):
    r = rb * 8 + j
    s = jnp.clip(r - 4, 0, R - 8)
    cls = jnp.where(r < 4, r, jnp.where(r > R - 4, r - (R - 8), 4))
    return pl.ds(pl.multiple_of(s * GW, GW), 8 * GW), cls


def _na_load(q_r, k_r, v_r, b_r, rb, R):
    geo = [_na_geom(rb, j, R) for j in range(8)]
    qs = [q_r[j * GW:(j + 1) * GW, :].astype(F32) for j in range(8)]
    kws = [k_r[win, :].astype(F32) for win, _ in geo]
    vws = [v_r[win, :].astype(F32) for win, _ in geo]
    bs = [jnp.concatenate([b_r[0, cls], b_r[1, cls]], axis=0) for _, cls in geo]
    return geo, qs, kws, vws, bs


def na_fwd(QB, KB, VB, biasd, L):
    R = L // GW
    qs = pl.BlockSpec((8 * GW, 128), lambda p, rb: (rb, p))
    kfull = pl.BlockSpec((L, 128), lambda p, rb: (0, p))
    kctx = pl.BlockSpec((LC, 128), lambda p, rb: (L // LC, p))
    bs = pl.BlockSpec((2, 8, GW, 8 * GW), lambda p, rb: (p, 0, 0, 0))

    def body(q_r, k_r, v_r, kx_r, vx_r, b_r, o_ref):
        _, qs_, kws, vws, bs_ = _na_load(q_r, k_r, v_r, b_r, pl.program_id(1), R)
        outs = _na_rows(qs_, kws, vws, kx_r[...].astype(F32), vx_r[...].astype(F32), bs_)
        o_ref[...] = jnp.concatenate(outs, axis=0)

    return _pc(body, "na_fwd", _sds((L, 256)), grid=(2, R // 8), in_specs=[qs, kfull, kfull, kctx, kctx, bs],
               out_specs=qs)(QB, KB, VB, KB, VB, biasd)


def na_bwd(QB, KB, VB, biasd, dO, L):
    R = L // GW
    qs = pl.BlockSpec((8 * GW, 128), lambda p, rb: (rb, p))
    kfull = pl.BlockSpec((L, 128), lambda p, rb: (0, p))
    kctx = pl.BlockSpec((LC, 128), lambda p, rb: (L // LC, p))
    bs = pl.BlockSpec((2, 8, GW, 8 * GW), lambda p, rb: (p, 0, 0, 0))
    oc = pl.BlockSpec((LC, 128), lambda p, rb: (0, p))

    def body(q_r, k_r, v_r, kx_r, vx_r, b_r, do_r, dq_o, dk_o, dv_o, dkx_o, dvx_o, db_o):
        rb = pl.program_id(1)
        _acc_init(rb == 0, [dk_o, dv_o, dkx_o, dvx_o, db_o])
        geo, qs_, kws, vws, bs_ = _na_load(q_r, k_r, v_r, b_r, rb, R)
        _, vjp = jax.vjp(_na_rows, qs_, kws, vws, kx_r[...].astype(F32), vx_r[...].astype(F32), bs_)
        dqs, dkws, dvws, dkx, dvx, dbs = vjp([do_r[j * GW:(j + 1) * GW, :] for j in range(8)])
        dq_o[...] = jnp.concatenate(dqs, axis=0)
        dkx_o[...] += dkx
        dvx_o[...] += dvx
        for j, (win, cls) in enumerate(geo):
            dk_o[win, :] += dkws[j]
            dv_o[win, :] += dvws[j]
            db_o[0, cls] += dbs[j][:GW]
            db_o[1, cls] += dbs[j][GW:]

    return _pc(body, "na_bwd",
               [_sds((L, 256)), _sds((L, 256)), _sds((L, 256)), _sds((LC, 256)), _sds((LC, 256)),
                _sds((4, 8, GW, 8 * GW))],
               grid=(2, R // 8), in_specs=[qs, kfull, kfull, kctx, kctx, bs, qs],
               out_specs=[qs, kfull, kfull, oc, oc, bs])(QB, KB, VB, KB, VB, biasd, dO)


def exact_mm_call(A, B):
    def body(a_ref, b_ref, o_ref):
        o_ref[...] = _exact(a_ref[...], b_ref[...])

    return _pc(body, "exact_mm", _sds((A.shape[0], B.shape[1])))(A, B)


def _conv_shift(x, d, L):
    T = x.shape[0]
    if d == 0:
        return x
    t = _iota(x.shape, 0)
    src = t + d
    ok = (src >= 0) & (src < T) & ((src >= L) == (t >= L))
    return jnp.where(ok, pltpu.roll(x, (-d) % T, 0), 0.0)


def conv_fwd(XBC, w8, b, L):
    T = XBC.shape[0]

    def body(x_ref, w_ref, b_ref, o_ref):
        x = x_ref[...]
        pre = b_ref[...] + functools.reduce(
            lambda a, c: a + c, [_conv_shift(x, k - 3, L) * w_ref[k:k + 1, :] for k in range(7)])
        o_ref[...] = _silu(pre)

    col = pl.BlockSpec((T, 128), lambda j: (0, j))
    return _pc(body, "conv_fwd", _sds((T, 1024)), grid=(8,),
               in_specs=[col, pl.BlockSpec((8, 128), lambda j: (0, j)), pl.BlockSpec((1, 128), lambda j: (0, j))],
               out_specs=col)(XBC, w8, b)


def conv_bwd(XBC, w8, b, dS, dxs_skip, L):
    T = XBC.shape[0]

    def body(x_ref, w_ref, b_ref, d0_r, d1_r, dsk_r, dx_o, dw_o, db_o):
        j = pl.program_id(0)
        x = x_ref[...]
        xs = [_conv_shift(x, k - 3, L) for k in range(7)]
        pre = b_ref[...] + functools.reduce(lambda a, c: a + c, [xs[k] * w_ref[k:k + 1, :] for k in range(7)])
        _, vjp = jax.vjp(_silu, pre)
        dact = d0_r[0] + d1_r[0] + jnp.where(j < 4, dsk_r[...], 0.0)
        dpre, = vjp(dact)
        dx_o[...] = functools.reduce(
            lambda a, c: a + c, [_conv_shift(dpre, 3 - k, L) * w_ref[k:k + 1, :] for k in range(7)])
        dw_o[...] = jnp.concatenate([jnp.sum(dpre * xs[k], axis=0, keepdims=True) for k in range(7)]
                                    + [jnp.zeros((1, 128), F32)], axis=0)
        db_o[...] = jnp.sum(dpre, axis=0, keepdims=True)

    col = pl.BlockSpec((T, 128), lambda j: (0, j))
    w_s = pl.BlockSpec((8, 128), lambda j: (0, j))
    b_s = pl.BlockSpec((1, 128), lambda j: (0, j))
    ds = lambda d: pl.BlockSpec((1, T, 128), lambda j: (d, 0, j))
    return _pc(body, "conv_bwd", [_sds((T, 1024)), _sds((8, 1024)), _sds((1, 1024))], grid=(8,),
               in_specs=[col, w_s, b_s, ds(0), ds(1), pl.BlockSpec((T, 128), lambda j: (0, jnp.minimum(j, 3)))],
               out_specs=[col, w_s, b_s])(XBC, w8, b, dS, dS, dxs_skip)


def _ssd_chunk(xs, bs, cs, dtraw, dtb, alog, hs, tri, d):
    dt = _softplus(dtraw + dtb)
    a = dt * (-jnp.exp(alog))
    acum = _exact(tri, a)
    tot = jnp.sum(a, axis=0, keepdims=True)
    wcol = jnp.exp(tot - acum) * dt
    ea = jnp.exp(acum)
    cd = jnp.exp(tot)
    acum_t, dt_t = acum.T, dt.T
    lane = _iota((Q, 128), 1)
    srow = _iota((128, Q), 0)
    lane1 = _iota((1, 128), 1)
    prow = _iota((128, NSTATE), 0)
    mask = tri > 0.5
    cbs = [mm_nt(cs[g], bs[g]) for g in range(2)]
    ys, hn = [], []
    for j in range(4):
        g = j // 2
        x = xs[j]
        yi, st, eac, cdl = [], [], [], []
        for u in range(2):
            slot = d * 8 + 2 * j + u
            col = lambda m: jnp.sum(jnp.where(lane == slot, m, 0.0), axis=1, keepdims=True)
            rowv = lambda m: jnp.sum(jnp.where(srow == slot, m, 0.0), axis=0, keepdims=True)
            seg = col(acum) - rowv(acum_t)
            dcy = jnp.where(mask, jnp.exp(jnp.where(mask, seg, 0.0)), 0.0)
            yi.append(mm(cbs[g] * dcy * rowv(dt_t), x))
            st.append(mm_tn(x, bs[g] * col(wcol)))
            eac.append(col(ea))
            cdl.append(jnp.sum(jnp.where(lane1 == slot, cd, 0.0), axis=1, keepdims=True))
        yin = mm_nt(cs[g], hs[j])
        ys.append(jnp.where(lane < HD, yi[0] + yin * eac[0], yi[1] + yin * eac[1]))
        hn.append(hs[j] * jnp.where(prow < HD, cdl[0], cdl[1]) + jnp.where(prow < HD, st[0], st[1]))
    return ys, hn


def _ssd_chunk_idx(d, s, nlc, nch):
    return jnp.where(d == 0, (s + nlc) % nch, nch - 1 - s)


def ssd_fwd(ACT, DT, dtb, alog, tri2, L):
    T = ACT.shape[0]
    nlc, nch = L // Q, T // Q

    def body(a_ref, dt_ref, dtb_ref, al_ref, tri_ref, y_o, hs_o, hst):
        d, s = pl.program_id(0), pl.program_id(1)
        _acc_init(s == 0, [hst])
        a = a_ref[...]
        xs = [a[:, 128 * j:128 * (j + 1)] for j in range(4)]
        bs = [a[:, 512 + 128 * g:640 + 128 * g] for g in range(2)]
        cs = [a[:, 768 + 128 * g:896 + 128 * g] for g in range(2)]
        hs = [hst[j] for j in range(4)]
        hs_o[0, 0] = hst[...]
        ys, hn = _ssd_chunk(xs, bs, cs, dt_ref[...], dtb_ref[...], al_ref[...], hs, tri_ref[0], d)
        y_o[0] = jnp.concatenate(ys, axis=1)
        for j in range(4):
            hst[j] = hn[j]

    ck = lambda w: pl.BlockSpec((Q, w), lambda d, s: (_ssd_chunk_idx(d, s, nlc, nch), 0))
    v128 = pl.BlockSpec((1, 128), lambda d, s: (0, 0))
    return _pc(body, "ssd_fwd", [_sds((2, T, 512)), _sds((2, nch, 4, 128, NSTATE))], grid=(2, nch),
               in_specs=[ck(1024), ck(128), v128, v128, pl.BlockSpec((1, Q, Q), lambda d, s: (d, 0, 0))],
               out_specs=[pl.BlockSpec((1, Q, 512), lambda d, s: (d, _ssd_chunk_idx(d, s, nlc, nch), 0)),
                          pl.BlockSpec((1, 1, 4, 128, NSTATE), lambda d, s: (d, s, 0, 0, 0))],
               scratch=[pltpu.VMEM((4, 128, NSTATE), F32)])(ACT, DT, dtb, alog, tri2)


def ssd_bwd(ACT, DT, dtb, alog, tri2, HS, dY, L, sends=()):
    T = ACT.shape[0]
    nlc, nch = L // Q, T // Q

    def body(a_ref, dt_ref, dtb_ref, al_ref, tri_ref, hs_ref, dy_ref, da_o, ddt_o, ddtb_o, dal_o, dh):
        d, sr = pl.program_id(0), pl.program_id(1)
        _acc_init(sr == 0, [dh, ddtb_o, dal_o])
        a = a_ref[...]
        xs = [a[:, 128 * j:128 * (j + 1)] for j in range(4)]
        bs = [a[:, 512 + 128 * g:640 + 128 * g] for g in range(2)]
        cs = [a[:, 768 + 128 * g:896 + 128 * g] for g in range(2)]
        hs = [hs_ref[0, 0, j] for j in range(4)]
        tri = tri_ref[0]
        fn = lambda xs_, bs_, cs_, dtr, dtb_, al, hs_: _ssd_chunk(xs_, bs_, cs_, dtr, dtb_, al, hs_, tri, d)
        _, vjp = jax.vjp(fn, xs, bs, cs, dt_ref[...], dtb_ref[...], al_ref[...], hs)
        dy = dy_ref[...]
        dys = [dy[:, 128 * j:128 * (j + 1)] for j in range(4)]
        dxs, dbs, dcs, ddt, ddtb, dal, dhs = vjp((dys, [dh[j] for j in range(4)]))
        da_o[0] = jnp.concatenate(dxs + dbs + dcs, axis=1)
        ddt_o[0] = ddt
        ddtb_o[0] += ddtb
        dal_o[0] += dal
        for j in range(4):
            dh[j] = dhs[j]

    cidx = lambda d, sr: _ssd_chunk_idx(d, nch - 1 - sr, nlc, nch)
    ck = lambda w: pl.BlockSpec((Q, w), lambda d, sr: (cidx(d, sr), 0))
    v128 = pl.BlockSpec((1, 128), lambda d, sr: (0, 0))
    o128 = pl.BlockSpec((1, 1, 128), lambda d, sr: (d, 0, 0))
    return _pc(body, "ssd_bwd", [_sds((2, T, 1024)), _sds((2, T, 128)), _sds((2, 1, 128)), _sds((2, 1, 128))],
               grid=(2, nch),
               in_specs=[ck(1024), ck(128), v128, v128, pl.BlockSpec((1, Q, Q), lambda d, sr: (d, 0, 0)),
                         pl.BlockSpec((1, 1, 4, 128, NSTATE), lambda d, sr: (d, nch - 1 - sr, 0, 0, 0)), ck(512)],
               out_specs=[pl.BlockSpec((1, Q, 1024), lambda d, sr: (d, cidx(d, sr), 0)),
                          pl.BlockSpec((1, Q, 128), lambda d, sr: (d, cidx(d, sr), 0)), o128, o128],
               scratch=[pltpu.VMEM((4, 128, NSTATE), F32)], sends=sends)(ACT, DT, dtb, alog, tri2, HS, dY)


_QA_PERM = np.concatenate([np.arange(HD * h, HD * h + HD) for h in (0, 2, 1, 3)])
_PAIR_HEADS = np.array([[0, 2], [1, 3]])


def _tables(L):
    t = jnp.arange(L)
    inv = 10000.0 ** (-jnp.arange(16, dtype=F32) / 16)

    def half(pos):
        ang = pos.astype(F32)[:, None] * inv[None, :]
        return jnp.concatenate([ang, ang], axis=1)

    ang = jnp.tile(jnp.concatenate([half(t // GW), half(t % GW)], axis=1), (1, 4))
    cos = jnp.concatenate([jnp.cos(ang), jnp.ones((LC, 256), F32)], axis=0)
    sin = jnp.concatenate([jnp.sin(ang), jnp.zeros((LC, 256), F32)], axis=0)
    rm = np.zeros((256, 256), np.float32)
    for j in range(256):
        if j % 32 < 16:
            rm[j + 16, j] = -1.0
        else:
            rm[j - 16, j] = 1.0
    tri = np.tril(np.ones((Q, Q), np.float32))
    return cos, sin, jnp.asarray(rm), jnp.asarray(np.stack([tri, tri.T]))


def _na_index(R):
    rc = np.array([0, 1, 2, 3, 4, R - 3, R - 2, R - 1])
    dy = np.clip(rc - 4, 0, R - 8)[:, None] + np.arange(8)[None, :] - rc[:, None] + 7
    qc, cc = np.arange(GW)[:, None], np.arange(GW)[None, :]
    dx = np.clip(cc - qc, -15, 15) + 15
    cstart = np.clip(qc - 8, 0, GW - 16)
    cmask = (cc >= cstart) & (cc < cstart + 16)
    idx = dy[:, None, :, None] * 31 + dx[None, :, None, :]
    return idx.reshape(8, GW, 8 * GW), np.broadcast_to(cmask[None, :, None, :], idx.shape).reshape(8, GW, 8 * GW), \
        dy, dx, cmask


def _na_bias(rpb, R):
    _, cm, dy, dx, _ = _na_index(R)
    e1t = np.zeros((128, GW * GW), np.float32)
    e1t[dx.reshape(-1), np.arange(GW * GW)] = 1.0
    v = jnp.pad(rpb[:, dy.reshape(-1), :].reshape(256, 31), ((0, 0), (0, 97)))
    full = exact_mm_call(v, jnp.asarray(e1t))
    dense = full.reshape(4, 8, 8, GW, GW).transpose(0, 1, 3, 2, 4).reshape(4, 8, GW, 8 * GW)
    return jnp.where(cm[None], dense, NEG)


def _na_bias_grad(dbias, R):
    _, _, dy, dx, cmask = _na_index(R)
    e1 = np.zeros((GW * GW, 128), np.float32)
    e1[np.arange(GW * GW), dx.reshape(-1)] = cmask.reshape(-1)
    a1 = dbias.reshape(4, 8, GW, 8, GW).transpose(0, 1, 3, 2, 4).reshape(256, GW * GW)
    v = exact_mm_call(a1, jnp.asarray(e1))[:, :31].reshape(4, 64, 31)
    e2 = np.zeros((64, 128), np.float32)
    e2[np.arange(64), dy.reshape(-1)] = 1.0
    a2 = jnp.pad(v.transpose(0, 2, 1).reshape(124, 64), ((0, 4), (0, 0)))
    return exact_mm_call(a2, jnp.asarray(e2))[:124, :15].reshape(4, 31, 15).transpose(0, 2, 1)


def _lanes(v, n=128):
    v = v.reshape(1, -1)
    return jnp.pad(v, ((0, 0), (0, n - v.shape[1])))


def _cls2(a, b):
    return jnp.stack([a, b]).reshape(2, 1, D)


def _layer_consts(p):
    w_in = p["w_in"]
    win_p = jnp.concatenate([w_in[:, _QA_PERM], w_in[:, 256:], jnp.zeros((D, NP_IN - IN_COLS), w_in.dtype)], axis=1)
    wout_p = jnp.concatenate([p["w_out"][_QA_PERM, :], p["w_out"][256:, :]], axis=0)
    sinkp = jnp.broadcast_to(p["wa_sink"][_PAIR_HEADS][:, :, None, None], (2, 2, 1, 128))
    return dict(
        win=win_p, wout=wout_p, wfi=p["w_ffn_in"], wfo=p["w_ffn_out"], sinkp=sinkp,
        nosink=jnp.full((2, 2, 1, 128), NEG, F32),
        w8=jnp.concatenate([p["ssm_conv_w"], jnp.zeros((1, 1024), F32)], axis=0),
        cb=p["ssm_conv_b"].reshape(1, 1024), dtb=_lanes(p["ssm_dt_bias"]), alog=_lanes(p["ssm_a_log"]),
        dsk=jnp.repeat(p["ssm_d"], HD).reshape(1, 512), gs=p["ssm_norm_g"].reshape(1, 512),
        gmix=p["g_mix"].reshape(1, D), gffn=p["g_ffn"].reshape(1, D))


def _mods(mod2):
    return [_cls2(mod2[0, D * k:D * (k + 1)], mod2[1, D * k:D * (k + 1)]) for k in range(6)]


def _layer_fwd(X, mod2, c, rpb, tabs, L, ctx_out):
    cos, sin, rm, tri2 = tabs
    sh1, sc1, gt1, sh2, sc2, gt2 = _mods(mod2)
    biasd = _na_bias(rpb, L // GW)
    qa, qb, z, ka, va, kb, vb, xbc, dt, h1 = in_fwd(X, c["gmix"], sh1, sc1, c["win"], cos, sin, rm, L)
    oa = wa_fwd(qa, ka, va, c["sinkp"], L)
    ob = na_fwd(qb, kb, vb, biasd, L)
    if ctx_out:
        oa_c = ctx_fwd(qa, ka, va, c["sinkp"], True, L)
        ob_c = ctx_fwd(qb, kb, vb, c["nosink"], False, L)
    else:
        oa_c = ob_c = jnp.zeros((LC, 256), F32)
    oa = jnp.concatenate([oa, oa_c], axis=0)
    ob = jnp.concatenate([ob, ob_c], axis=0)
    act = conv_fwd(xbc, c["w8"], c["cb"], L)
    y2, hs = ssd_fwd(act, dt, c["dtb"], c["alog"], tri2, L)
    X1, cat = out_fwd(oa, ob, y2, act, z, c["dsk"], c["gs"], c["wout"], X, gt1, L)
    X2 = ffn_fwd(X1, c["gffn"], sh2, sc2, gt2, c["wfi"], c["wfo"], L)
    saved = dict(X=X, X1=X1, qa=qa, qb=qb, z=z, ka=ka, va=va, kb=kb, vb=vb, xbc=xbc, dt=dt, h1=h1, oa=oa, ob=ob,
                 act=act, y2=y2, hs=hs, cat=cat, biasd=biasd)
    return X2, saved


def _col_blocks(gw):
    return gw.reshape(gw.shape[0], NDEV, gw.shape[1] // NDEV).transpose(1, 0, 2)


def _row_blocks(gw):
    return gw.reshape(NDEV, gw.shape[0] // NDEV, gw.shape[1])


def _layer_bwd(dX2, s, mod2, c, tabs, L, ctx_out, carry):
    cos, sin, rm, tri2 = tabs
    sh1, sc1, gt1, sh2, sc2, gt2 = _mods(mod2)
    R = L // GW
    res = ffn_bwd(s["X1"], c["gffn"], sh2, sc2, gt2, c["wfi"], c["wfo"], dX2, L, sends=carry)
    (dX1, h2, dU, actf, dOut, dgffn, dsh2, dsc2, dgt2), got = res if carry else (res, ())
    g = {}
    gfi = _col_blocks(tn_mm(h2, dU, 1408, MXU))
    gfo = _row_blocks(tn_mm(actf, dOut, 512, MXU))
    doa, dob, dy, dxs_skip, dz, dmix, ddsk, dgs, dgt1 = out_bwd(s["oa"], s["ob"], s["y2"], s["act"], s["z"], c["dsk"],
                                                                c["gs"], c["wout"], gt1, dX1, L)
    dwout = tn_mm(s["cat"], dmix, 512, MXU)
    gout = _row_blocks(jnp.concatenate([dwout[_QA_PERM, :], dwout[256:, :]], axis=0))
    (dS, ddt2, ddtb, dal), (g["w_ffn_in"], g["w_ffn_out"]) = ssd_bwd(
        s["act"], s["dt"], c["dtb"], c["alog"], tri2, s["hs"], dy, L, sends=(gfi, gfo))
    dxbc, dw8, dcb = conv_bwd(s["xbc"], c["w8"], c["cb"], dS, dxs_skip, L)
    (dqa, dkpad, dvpad, dkxa, dvxa, dska), (g["w_out"],) = wa_bwd(s["qa"], s["ka"], s["va"], c["sinkp"], doa, L,
                                                                  sends=(gout,))
    dqb, dkb, dvb, dkxb, dvxb, dbias = na_bwd(s["qb"], s["kb"], s["vb"], s["biasd"], dob, L)
    if ctx_out:
        dqa_c, dk1, dv1, dsk1 = ctx_bwd(s["qa"], s["ka"], s["va"], c["sinkp"], doa, True, L)
        dqb_c, dk2, dv2, _ = ctx_bwd(s["qb"], s["kb"], s["vb"], c["nosink"], dob, False, L)
        dkxa, dvxa, dska = dkxa + dk1, dvxa + dv1, dska + dsk1
        dkxb, dvxb = dkxb + dk2, dvxb + dv2
    else:
        dqa_c = dqb_c = jnp.zeros((LC, 256), F32)
    cat0 = lambda a, b: jnp.concatenate([a, b], axis=0)
    dX, dycat, dgmix, dsh1, dsc1 = in_bwd(
        s["X"], c["gmix"], sh1, sc1, c["win"], cos, sin, rm, dX1, cat0(dqa, dqa_c), cat0(dqb, dqb_c), dz,
        cat0(dkpad[Q:L + Q], dkxa), cat0(dvpad[Q:L + Q], dvxa), cat0(dkb, dkxb), cat0(dvb, dvxb), dxbc, ddt2, L)
    dwin = tn_mm(s["h1"], dycat, 1024, MXU)
    gin = _col_blocks(jnp.concatenate([dwin[:, _QA_PERM], dwin[:, 256:IN_COLS]], axis=1))
    g["g_mix"] = dgmix.reshape(D)
    g["g_ffn"] = dgffn.reshape(D)
    sk = jnp.sum(dska, axis=(2, 3))
    g["wa_sink"] = jnp.zeros((4,), F32).at[_PAIR_HEADS.reshape(-1)].set(sk.reshape(-1))
    g["na_rpb"] = _na_bias_grad(dbias, R)
    g["ssm_conv_w"] = dw8[:7]
    g["ssm_conv_b"] = dcb.reshape(1024)
    g["ssm_dt_bias"] = (ddtb[0] + ddtb[1])[0, :16].reshape(2, 8)
    g["ssm_a_log"] = (dal[0] + dal[1])[0, :16].reshape(2, 8)
    g["ssm_d"] = jnp.sum(ddsk.reshape(8, HD), axis=1)
    g["ssm_norm_g"] = dgs.reshape(512)
    dmod2 = jnp.concatenate([dsh1, dsc1, dgt1, dsh2, dsc2, dgt2], axis=2).reshape(2, 6 * D)
    return dX, g, dmod2, gin, got


def local_step(x, ctx, tgt, mods, layers, g_final, L):
    tabs = _tables(L)
    X = jnp.concatenate([x, ctx], axis=0)
    consts = [_layer_consts(p) for p in layers]
    saved = []
    for i in range(2):
        X, s = _layer_fwd(X, mods[i], consts[i], layers[i]["na_rpb"], tabs, L, ctx_out=(i == 0))
        saved.append(s)
    loss8, dxl, dgfin = loss_head(X, g_final.reshape(1, D), tgt, L)
    dX = jnp.concatenate([dxl, jnp.zeros((LC, D), F32)], axis=0)
    grads, dmods = [None, None], [None, None]
    dX, grads[1], dmods[1], gin1, _ = _layer_bwd(dX, saved[1], mods[1], consts[1], tabs, L, False, ())
    dX, grads[0], dmods[0], gin0, (grads[1]["w_in"],) = _layer_bwd(dX, saved[0], mods[0], consts[0], tabs, L, True,
                                                                   (gin1,))
    grads[0]["w_in"], = all_to_all([gin0], "exchange_last")
    return loss8[0, 0], dX[:L], grads, jnp.stack(dmods), dgfin.reshape(D)


def _place():
    x, y, c = lax.axis_index("x"), lax.axis_index("y"), lax.axis_index("c")
    return x, y, c


def _slot(b):
    return 4 * b[0] + 2 * b[1] + b[2]


def _any():
    return pl.BlockSpec(memory_space=pl.ANY)


def all_gather(xs, name):
    n = len(xs)

    def body(*refs):
        x_refs, o_refs = refs[:n], refs[n:2 * n]
        send_sems, recv_sems, local_sems = refs[2 * n:]
        x, y, c = _place()
        me, sib = (x, y, c), (x, y, 1 - c)
        chips = [(1 - x, y), (x, 1 - y), (1 - x, 1 - y)]

        def copy(t, k, blk, to, src=None):
            dst = o_refs[t].at[_slot(blk)]
            return pltpu.make_async_remote_copy(
                src_ref=dst if src is None else src, dst_ref=dst, send_sem=send_sems.at[7 * t + k],
                recv_sem=recv_sems.at[7 * t + k], device_id=to, device_id_type=MESH_T)

        mine = [pltpu.make_async_copy(x_refs[t], o_refs[t].at[_slot(me)], local_sems.at[t]) for t in range(n)]
        for cp in mine:
            cp.start()
        first = []
        for t in range(n):
            first.append(copy(t, 0, me, sib, src=x_refs[t]))
            first += [copy(t, 1 + j, me, (*chip, c), src=x_refs[t]) for j, chip in enumerate(chips)]
        for cp in first:
            cp.start()
        passed = []
        for j, chip in enumerate(chips):
            for t in range(n):
                copy(t, 1 + j, (*chip, c), me).wait_recv()
                cp = copy(t, 4 + j, (*chip, c), sib)
                cp.start()
                passed.append(cp)
        for t in range(n):
            copy(t, 0, sib, me).wait_recv()
            for j, chip in enumerate(chips):
                copy(t, 4 + j, (*chip, 1 - c), me).wait_recv()
        for cp in first + passed:
            cp.wait_send()
        for cp in mine:
            cp.wait()

    return pl.pallas_call(
        body, name=name, out_shape=[_sds((NDEV,) + a.shape, a.dtype) for a in xs],
        in_specs=[_any()] * n, out_specs=[_any()] * n,
        scratch_shapes=[pltpu.SemaphoreType.DMA((7 * n,)), pltpu.SemaphoreType.DMA((7 * n,)),
                        pltpu.SemaphoreType.DMA((n,))],
        interpret=_INTERPRET)(*xs)


def all_to_all(xs, name):
    n = len(xs)

    def body(*refs):
        _a2a_start(refs[:n], refs[n:2 * n], *refs[2 * n:])
        _a2a_wait(refs[:n], refs[n:2 * n], *refs[2 * n:])

    return pl.pallas_call(
        body, name=name, out_shape=[_sds(a.shape, a.dtype) for a in xs],
        in_specs=[_any()] * n, out_specs=[_any()] * n, scratch_shapes=_a2a_sems(n), interpret=_INTERPRET)(*xs)


def _a2a_sems(n):
    return [pltpu.SemaphoreType.DMA((7 * n,)), pltpu.SemaphoreType.DMA((7 * n,)), pltpu.SemaphoreType.DMA((n,))]


def _a2a_copies(x_refs, o_refs, send_sems, recv_sems, local_sems):
    n = len(x_refs)
    x, y, c = _place()
    me = (x, y, c)
    flip = lambda v, b: (1 - v) if b else v
    peers = [(flip(x, k >> 2 & 1), flip(y, k >> 1 & 1), flip(c, k & 1)) for k in range(1, NDEV)]
    mine = [pltpu.make_async_copy(x_refs[t].at[_slot(me)], o_refs[t].at[_slot(me)], local_sems.at[t])
            for t in range(n)]

    def copy(t, k, src_slot, dst_slot, to):
        return pltpu.make_async_remote_copy(
            src_ref=x_refs[t].at[src_slot], dst_ref=o_refs[t].at[dst_slot], send_sem=send_sems.at[7 * t + k],
            recv_sem=recv_sems.at[7 * t + k], device_id=to, device_id_type=MESH_T)

    sends = [copy(t, k, _slot(p), _slot(me), p) for t in range(n) for k, p in enumerate(peers)]
    recvs = [copy(t, k, _slot(p), _slot(p), me) for t in range(n) for k, p in enumerate(peers)]
    return mine, sends, recvs


def _a2a_start(x_refs, o_refs, send_sems, recv_sems, local_sems):
    mine, sends, _ = _a2a_copies(x_refs, o_refs, send_sems, recv_sems, local_sems)
    for cp in mine + sends:
        cp.start()


def _a2a_wait(x_refs, o_refs, send_sems, recv_sems, local_sems):
    mine, sends, recvs = _a2a_copies(x_refs, o_refs, send_sems, recv_sems, local_sems)
    for cp in recvs:
        cp.wait_recv()
    for cp in sends:
        cp.wait_send()
    for cp in mine:
        cp.wait()


def adam_reduce(P, w, m, v, name):
    n, R, C = P.shape
    br = R // 4 if R % 64 == 0 else R

    def body(p_ref, w_ref, m_ref, v_ref, g_o, d_o, m_o, v_o):
        g = p_ref[0].astype(F32)
        for k in range(1, n):
            g = g + p_ref[k].astype(F32)
        m1 = ADAM_B1 * m_ref[...] + (1.0 - ADAM_B1) * g
        v1 = ADAM_B2 * v_ref[...] + (1.0 - ADAM_B2) * jnp.square(g)
        m_hat = m1 / (1.0 - ADAM_B1 ** ADAM_STEP)
        v_hat = v1 / (1.0 - ADAM_B2 ** ADAM_STEP)
        g_o[...] = g
        d_o[...] = -ADAM_LR * (m_hat / (jnp.sqrt(v_hat) + ADAM_EPS) + ADAM_WD * w_ref[...])
        m_o[...] = m1
        v_o[...] = v1

    blk = pl.BlockSpec((br, C), lambda i: (i, 0))
    return _pc(body, name, [_sds((R, C))] * 4, grid=(R // br,),
               in_specs=[pl.BlockSpec((n, br, C), lambda i: (0, i, 0)), blk, blk, blk], out_specs=[blk] * 4)(P, w, m, v)


def adam_layers(P0, P1, w, m, v, name):
    n, R, C = P0.shape
    br = R // 4 if R % 64 == 0 else R
    nb = R // br

    def body(p0_ref, p1_ref, w_ref, m_ref, v_ref, g_o, d_o, m_o, v_o):
        def total(p_ref):
            g = p_ref[0].astype(F32)
            for k in range(1, n):
                g = g + p_ref[k].astype(F32)
            return g

        g = jnp.where(pl.program_id(0) == 0, total(p0_ref), total(p1_ref))
        m1 = ADAM_B1 * m_ref[0] + (1.0 - ADAM_B1) * g
        v1 = ADAM_B2 * v_ref[0] + (1.0 - ADAM_B2) * jnp.square(g)
        m_hat = m1 / (1.0 - ADAM_B1 ** ADAM_STEP)
        v_hat = v1 / (1.0 - ADAM_B2 ** ADAM_STEP)
        g_o[0] = g
        d_o[0] = -ADAM_LR * (m_hat / (jnp.sqrt(v_hat) + ADAM_EPS) + ADAM_WD * w_ref[0])
        m_o[0] = m1
        v_o[0] = v1

    blk = pl.BlockSpec((1, br, C), lambda l, i: (l, i, 0))
    p0 = pl.BlockSpec((n, br, C), lambda l, i: (0, jnp.where(l == 0, i, nb - 1), 0))
    p1 = pl.BlockSpec((n, br, C), lambda l, i: (0, jnp.where(l == 1, i, 0), 0))
    return _pc(body, name, [_sds((2, R, C))] * 4, grid=(2, nb), in_specs=[p0, p1, blk, blk, blk],
               out_specs=[blk] * 4)(P0, P1, w, m, v)


def mod_fwd(scin, wmod, bcol):
    def body(s_ref, w_ref, b_ref, o_ref):
        o_ref[0] = mm(_silu(s_ref[...]), w_ref[0]) + b_ref[0]

    return _pc(body, "mod_fwd", _sds((2, 16, 768)), grid=(2,),
               in_specs=[pl.BlockSpec((16, D), lambda l: (0, 0)), pl.BlockSpec((1, D, 768), lambda l: (l, 0, 0)),
                         pl.BlockSpec((1, 1, 768), lambda l: (l, 0, 0))],
               out_specs=pl.BlockSpec((1, 16, 768), lambda l: (l, 0, 0)))(scin, wmod, bcol)


def mod_bwd(scin, wmod, G):
    def body(s_ref, w_ref, g_ref, dw_o, ds_o):
        _, vjp = jax.vjp(lambda s, w: mm(_silu(s), w), s_ref[...], w_ref[0])
        ds, dw = vjp(g_ref[0])
        dw_o[0] = dw
        _acc_init(pl.program_id(0) == 0, [ds_o])
        ds_o[...] += ds

    full = pl.BlockSpec((16, D), lambda l: (0, 0))
    wsp = pl.BlockSpec((1, D, 768), lambda l: (l, 0, 0))
    return _pc(body, "mod_bwd", [_sds((2, D, 768)), _sds((16, D))], grid=(2,),
               in_specs=[full, wsp, pl.BlockSpec((1, 16, 768), lambda l: (l, 0, 0))], out_specs=[wsp, full])(
        scin, wmod, G)


_SMALL = ["b_mod", "g_mix", "wa_sink", "na_rpb", "ssm_conv_w", "ssm_conv_b", "ssm_dt_bias", "ssm_a_log", "ssm_d",
          "ssm_norm_g", "g_ffn", "g_final", "dmod_s", "dmod_c"]


def _pack(parts):
    rows = []
    for a in parts:
        f = a.reshape(-1).astype(F32)
        rows.append(jnp.pad(f, (0, (-f.shape[0]) % 1024)).reshape(-1, 128))
    return jnp.concatenate(rows, axis=0)


def _unpack(packed, shapes):
    out, r = [], 0
    for s in shapes:
        nel = int(np.prod(s))
        nr = -(-nel // 1024) * 8
        out.append(packed[r:r + nr].reshape(-1)[:nel].reshape(s))
        r += nr
    return out


def kernel(x, c, ctx, c_ctx, w_mod, b_mod, g_mix, w_in, wa_sink, na_rpb, ssm_conv_w, ssm_conv_b, ssm_dt_bias, ssm_a_log, ssm_d, ssm_norm_g, w_out, g_ffn, w_ffn_in, w_ffn_out, g_final, loss_target, m_c_ctx, m_w_mod, m_b_mod, m_g_mix, m_w_in, m_wa_sink, m_na_rpb, m_ssm_conv_w, m_ssm_conv_b, m_ssm_dt_bias, m_ssm_a_log, m_ssm_d, m_ssm_norm_g, m_w_out, m_g_ffn, m_w_ffn_in, m_w_ffn_out, m_g_final, v_c_ctx, v_w_mod, v_b_mod, v_g_mix, v_w_in, v_wa_sink, v_na_rpb, v_ssm_conv_w, v_ssm_conv_b, v_ssm_dt_bias, v_ssm_a_log, v_ssm_d, v_ssm_norm_g, v_w_out, v_g_ffn, v_w_ffn_in, v_w_ffn_out, v_g_final):
    L = x.shape[1]
    px, py, pc = _place()
    me = 4 * px + 2 * py + pc
    W = dict(c_ctx=c_ctx, w_mod=w_mod, b_mod=b_mod, g_mix=g_mix, w_in=w_in, wa_sink=wa_sink, na_rpb=na_rpb,
             ssm_conv_w=ssm_conv_w, ssm_conv_b=ssm_conv_b, ssm_dt_bias=ssm_dt_bias, ssm_a_log=ssm_a_log, ssm_d=ssm_d,
             ssm_norm_g=ssm_norm_g, w_out=w_out, g_ffn=g_ffn, w_ffn_in=w_ffn_in, w_ffn_out=w_ffn_out, g_final=g_final)
    M = dict(c_ctx=m_c_ctx, w_mod=m_w_mod, b_mod=m_b_mod, g_mix=m_g_mix, w_in=m_w_in, wa_sink=m_wa_sink,
             na_rpb=m_na_rpb, ssm_conv_w=m_ssm_conv_w, ssm_conv_b=m_ssm_conv_b, ssm_dt_bias=m_ssm_dt_bias,
             ssm_a_log=m_ssm_a_log, ssm_d=m_ssm_d, ssm_norm_g=m_ssm_norm_g, w_out=m_w_out, g_ffn=m_g_ffn,
             w_ffn_in=m_w_ffn_in, w_ffn_out=m_w_ffn_out, g_final=m_g_final)
    V = dict(c_ctx=v_c_ctx, w_mod=v_w_mod, b_mod=v_b_mod, g_mix=v_g_mix, w_in=v_w_in, wa_sink=v_wa_sink,
             na_rpb=v_na_rpb, ssm_conv_w=v_ssm_conv_w, ssm_conv_b=v_ssm_conv_b, ssm_dt_bias=v_ssm_dt_bias,
             ssm_a_log=v_ssm_a_log, ssm_d=v_ssm_d, ssm_norm_g=v_ssm_norm_g, w_out=v_w_out, g_ffn=v_g_ffn,
             w_ffn_in=v_w_ffn_in, w_ffn_out=v_w_ffn_out, g_final=v_g_final)

    c_all, conv_all = all_gather([c, ssm_conv_w], "gather_small")
    big = all_gather([w_in.astype(MXU), w_out.astype(MXU), w_ffn_in.astype(MXU), w_ffn_out.astype(MXU)],
                     "gather_weights")
    w_in_f = big[0].transpose(1, 2, 0, 3).reshape(2, D, IN_COLS)
    w_out_f = big[1].transpose(1, 0, 2, 3).reshape(2, D, D)
    w_fi_f = big[2].transpose(1, 2, 0, 3).reshape(2, D, 2 * DFF)
    w_fo_f = big[3].transpose(1, 0, 2, 3).reshape(2, DFF, D)
    conv_f = conv_all.transpose(1, 2, 0, 3).reshape(2, 7, 1024)

    scin = jnp.concatenate([c_all.reshape(NDEV, D), c_ctx.reshape(1, D), jnp.zeros((7, D), F32)], axis=0)
    bcol = lax.dynamic_slice_in_dim(b_mod, me * 768, 768, axis=1).reshape(2, 1, 768)
    mod_all, = all_gather([mod_fwd(scin, w_mod, bcol)], "gather_mod")
    mod_rows = mod_all.transpose(1, 2, 0, 3).reshape(2, 16, 6 * D)
    mods = jnp.stack([lax.dynamic_index_in_dim(mod_rows, me, axis=1, keepdims=False), mod_rows[:, 8]], axis=1)

    layers = [dict(w_in=w_in_f[i], w_out=w_out_f[i], w_ffn_in=w_fi_f[i], w_ffn_out=w_fo_f[i], g_mix=g_mix[i],
                   wa_sink=wa_sink[i], na_rpb=na_rpb[i], ssm_conv_w=conv_f[i], ssm_conv_b=ssm_conv_b[i],
                   ssm_dt_bias=ssm_dt_bias[i], ssm_a_log=ssm_a_log[i], ssm_d=ssm_d[i], ssm_norm_g=ssm_norm_g[i],
                   g_ffn=g_ffn[i]) for i in range(2)]
    loss, dx, grads, dmods, dgfin = local_step(x[0], ctx[0], loss_target[0], mods, layers, g_final, L)
    loss = lax.psum(loss, ("x", "y", "c"))

    stk = lambda n: jnp.stack([grads[0][n], grads[1][n]])
    small = dict(b_mod=dmods[:, 0] + dmods[:, 1], g_final=dgfin, dmod_s=dmods[:, 0], dmod_c=dmods[:, 1])
    for nme in _SMALL:
        if nme not in small:
            small[nme] = stk(nme)
    shapes = [small[nme].shape for nme in _SMALL]
    zero_like = lambda nme: jnp.zeros(small[nme].shape, F32)
    own = lambda S, nme: S[nme] if (nme in S and S[nme].shape == small[nme].shape) else zero_like(nme)
    gath, = all_gather([_pack([small[nme] for nme in _SMALL])], "gather_grads")
    sm = adam_reduce(gath, _pack([own(W, nme) for nme in _SMALL]), _pack([own(M, nme) for nme in _SMALL]),
                     _pack([own(V, nme) for nme in _SMALL]), "adam_small")
    res = {nme: vals for nme, vals in zip(_SMALL, zip(*[_unpack(a, shapes) for a in sm]))}

    cols = lambda a: lax.dynamic_slice_in_dim(a, me * 768, 768, axis=-1)
    gparts = [_unpack(gath[d], shapes) for d in range(NDEV)]
    dmod_s_all = jnp.stack([gparts[d][_SMALL.index("dmod_s")] for d in range(NDEV)], axis=1)
    G = jnp.concatenate([cols(dmod_s_all), cols(res["dmod_c"][0])[:, None, :], jnp.zeros((2, 7, 768), F32)], axis=1)
    dwmod, dscin = mod_bwd(scin, w_mod, G)
    cc_g, = all_gather([dscin[8].reshape(8, 128)], "gather_cctx")
    out = {}
    out["c_ctx"] = [a.reshape(D) for a in adam_reduce(cc_g, c_ctx.reshape(8, 128), m_c_ctx.reshape(8, 128),
                                                      v_c_ctx.reshape(8, 128), "adam_cctx")]
    out["w_mod"] = [a.reshape(2, D, 768) for a in adam_reduce(
        dwmod.reshape(1, 2 * D, 768), w_mod.reshape(2 * D, 768), m_w_mod.reshape(2 * D, 768),
        v_w_mod.reshape(2 * D, 768), "adam_wmod")]
    gconv = lax.dynamic_slice_in_dim(res["ssm_conv_w"][0], me * 128, 128, axis=2)
    out["ssm_conv_w"] = [a.reshape(2, 7, 128) for a in adam_reduce(
        gconv.reshape(1, 14, 128), ssm_conv_w.reshape(14, 128), m_ssm_conv_w.reshape(14, 128),
        v_ssm_conv_w.reshape(14, 128), "adam_conv")]
    for nme in _SMALL:
        if nme not in ("ssm_conv_w", "dmod_s", "dmod_c"):
            out[nme] = list(res[nme])

    for nme in ("w_in", "w_out", "w_ffn_in", "w_ffn_out"):
        out[nme] = list(adam_layers(grads[0][nme], grads[1][nme], W[nme], M[nme], V[nme], "adam_" + nme))
    order = ["c_ctx", "w_mod", "b_mod", "g_mix", "w_in", "wa_sink", "na_rpb", "ssm_conv_w", "ssm_conv_b",
             "ssm_dt_bias", "ssm_a_log", "ssm_d", "ssm_norm_g", "w_out", "g_ffn", "w_ffn_in", "w_ffn_out", "g_final"]
    return (loss, dx.reshape(1, L, D), *[out[nme][0] for nme in order], *[out[nme][1] for nme in order],
            *[out[nme][2] for nme in order], *[out[nme][3] for nme in order])
```

```python
import functools
import math

import numpy as np
import jax
import jax.numpy as jnp
from jax import lax
from jax.experimental import pallas as pl
from jax.experimental.pallas import tpu as pltpu

F32 = jnp.float32
MXU = jnp.bfloat16
_INTERPRET = False
VMEM_LIMIT = 60 * 1024 * 1024

D = 1024
LC = 256
GW = 64
HD = 64
EPS = 1e-6
NEG = -1e30
NDEV = 8
Q = 128
NSTATE = 128
DFF = 2816
IN_COLS = 2832
NP_IN = 3072
C_QA, C_QB, C_Z, C_KA, C_VA, C_KB, C_VB, C_XBC, C_DT = 0, 256, 512, 1024, 1152, 1280, 1536, 1792, 2816
ADAM_LR, ADAM_B1, ADAM_B2, ADAM_EPS, ADAM_WD, ADAM_STEP = 0.001, 0.9, 0.999, 1e-08, 0.01, 10
MESH_T = pl.DeviceIdType.MESH


def _dg(a, b, ca, cb):
    return lax.dot_general(a.astype(MXU), b.astype(MXU), (((ca,), (cb,)), ((), ())), preferred_element_type=F32)


@jax.custom_vjp
def mm(a, b):
    return _dg(a, b, 1, 0)


def _mm_f(a, b):
    return _dg(a, b, 1, 0), (a, b)


def _mm_b(res, g):
    a, b = res
    return _dg(g, b, 1, 1).astype(a.dtype), _dg(a, g, 0, 0).astype(b.dtype)


mm.defvjp(_mm_f, _mm_b)


@jax.custom_vjp
def mm_nt(a, b):
    return _dg(a, b, 1, 1)


def _mmnt_f(a, b):
    return _dg(a, b, 1, 1), (a, b)


def _mmnt_b(res, g):
    a, b = res
    return _dg(g, b, 1, 0).astype(a.dtype), _dg(g, a, 0, 0).astype(b.dtype)


mm_nt.defvjp(_mmnt_f, _mmnt_b)


@jax.custom_vjp
def mm_tn(a, b):
    return _dg(a, b, 0, 0)


def _mmtn_f(a, b):
    return _dg(a, b, 0, 0), (a, b)


def _mmtn_b(res, g):
    a, b = res
    return _dg(b, g, 1, 1).astype(a.dtype), _dg(a, g, 1, 0).astype(b.dtype)


mm_tn.defvjp(_mmtn_f, _mmtn_b)


def _exact(a, b):
    return lax.dot_general(a, b, (((1,), (0,)), ((), ())), precision=lax.Precision.HIGHEST,
                           preferred_element_type=F32)


def _pc(body, name, out_shape, grid=None, in_specs=None, out_specs=None, scratch=(), sends=(), gather=False):
    params = pltpu.CompilerParams(vmem_limit_bytes=VMEM_LIMIT)
    if sends and not isinstance(out_shape, (list, tuple)):
        out_shape, out_specs = [out_shape], [out_specs]
    start, wait = (_ag_start, _ag_wait) if gather else (_a2a_start, _a2a_wait)
    if not sends:
        kw = {}
        if grid is not None:
            kw = dict(grid=grid, in_specs=in_specs, out_specs=out_specs)
        elif in_specs is not None:
            kw = dict(in_specs=in_specs, out_specs=out_specs)
        return pl.pallas_call(body, name=name, out_shape=out_shape, scratch_shapes=list(scratch),
                              compiler_params=params, interpret=_INTERPRET, **kw)
    n, nin, nout, nscr = len(sends), len(in_specs), len(out_shape), len(scratch)

    def body2(*refs):
        cin, xs = refs[:nin], refs[nin:nin + n]
        couts, os_ = refs[nin + n:nin + n + nout], refs[nin + n + nout:nin + 2 * n + nout]
        cscr, sems = refs[nin + 2 * n + nout:nin + 2 * n + nout + nscr], refs[nin + 2 * n + nout + nscr:]
        ids = [pl.program_id(a) for a in range(len(grid))]
        first = functools.reduce(lambda a, b: a & b, [i == 0 for i in ids])
        last = functools.reduce(lambda a, b: a & b, [i == g - 1 for i, g in zip(ids, grid)])

        @pl.when(first)
        def _():
            start(xs, os_, *sems)

        body(*cin, *couts, *cscr)

        @pl.when(last)
        def _():
            wait(xs, os_, *sems)

    call = pl.pallas_call(
        body2, name=name,
        out_shape=list(out_shape) + [_sds(((NDEV,) if gather else ()) + a.shape, a.dtype) for a in sends],
        grid=grid, in_specs=list(in_specs) + [_any()] * n, out_specs=list(out_specs) + [_any()] * n,
        scratch_shapes=list(scratch) + _a2a_sems(n), compiler_params=params, interpret=_INTERPRET)

    def run(*args):
        res = call(*args, *sends)
        return res[:nout], res[nout:]

    return run


def _vm():
    return pl.BlockSpec(memory_space=pltpu.VMEM)


def _sds(shape, dt=F32):
    return jax.ShapeDtypeStruct(shape, dt)


def _iota(shape, dim):
    return lax.broadcasted_iota(jnp.int32, shape, dim)


def _silu(x):
    return x * jax.nn.sigmoid(x)


def _softplus(x):
    return jnp.maximum(x, 0.0) + jnp.log1p(jnp.exp(-jnp.abs(x)))


def _normmod(x, g, sh, sc):
    r = lax.rsqrt(jnp.mean(x * x, axis=-1, keepdims=True) + EPS)
    return (x * r * g) * (1.0 + sc) + sh


def _rope(x, cos, sin, rm):
    return x * cos + _exact(x, rm) * sin


def _acc_init(first, refs):
    @pl.when(first)
    def _():
        for r in refs:
            r[...] = jnp.zeros_like(r)


def in_fwd(X, g, sh, sc, W, cos, sin, rm, L, sends=()):
    T = X.shape[0]
    TR = 256
    nlt = L // TR

    def body(x_ref, g_ref, sh_ref, sc_ref, w_ref, cos_ref, sin_ref, rm_ref,
             qa, qb, z, ka, va, kb, vb, xbc, dt, hout):
        h = _normmod(x_ref[...], g_ref[...], sh_ref[0], sc_ref[0]).astype(MXU)
        hout[...] = h
        y = jnp.dot(h, w_ref[...], preferred_element_type=F32)
        cs, sn, r = cos_ref[...], sin_ref[...], rm_ref[...]
        qa[...] = _rope(y[:, C_QA:C_QB], cs, sn, r).astype(MXU)
        qb[...] = y[:, C_QB:C_Z].astype(MXU)
        z[...] = y[:, C_Z:C_KA]
        ka[...] = _rope(y[:, C_KA:C_VA], cs[:, :128], sn[:, :128], r[:128, :128]).astype(MXU)
        va[...] = y[:, C_VA:C_KB].astype(MXU)
        kb[...] = y[:, C_KB:C_VB].astype(MXU)
        vb[...] = y[:, C_VB:C_XBC].astype(MXU)
        xbc[...] = y[:, C_XBC:C_DT]
        dt[...] = y[:, C_DT:C_DT + 128]

    row = lambda w: pl.BlockSpec((TR, w), lambda i: (i, 0))
    cls = pl.BlockSpec((1, 1, D), lambda i: (i // nlt, 0, 0))
    widths = [(256, MXU), (256, MXU), (512, F32), (128, MXU), (128, MXU), (256, MXU), (256, MXU), (1024, F32),
              (128, F32), (D, MXU)]
    return _pc(body, "in_fwd", [_sds((T, w), d) for w, d in widths], grid=(T // TR,),
               in_specs=[row(D), pl.BlockSpec((1, D), lambda i: (0, 0)), cls, cls, _vm(), row(256), row(256), _vm()],
               out_specs=[row(w) for w, _ in widths], sends=sends, gather=True)(X, g, sh, sc, W, cos, sin, rm)


def in_bwd(X, g, sh, sc, W, cos, sin, rm, dxres, dqa, dqb, dz, dka, dva, dkb, dvb, dxbc, ddt2, L):
    T = X.shape[0]
    TR = 256
    nlt = L // TR

    def body(x_ref, g_ref, sh_ref, sc_ref, w_ref, cos_ref, sin_ref, rm_ref, dxres_ref, dqa_r, dqb_r, dz_r, dka_r,
             dva_r, dkb_r, dvb_r, dxbc_r, ddt0_r, ddt1_r, dx_o, dy_o, dg_o, dsh_o, dsc_o):
        i = pl.program_id(0)
        cs, sn, r = cos_ref[...], sin_ref[...], rm_ref[...]
        _, vq = jax.vjp(lambda t: _rope(t, cs, sn, r), dqa_r[...])
        _, vk = jax.vjp(lambda t: _rope(t, cs[:, :128], sn[:, :128], r[:128, :128]), dka_r[...])
        dyqa, = vq(dqa_r[...])
        dyka, = vk(dka_r[...])
        ddt = ddt0_r[0] + ddt1_r[0]
        dy = jnp.concatenate([dyqa, dqb_r[...], dz_r[...], dyka, dva_r[...], dkb_r[...], dvb_r[...], dxbc_r[...],
                              ddt, jnp.zeros((TR, NP_IN - C_DT - 128), F32)], axis=1).astype(MXU)
        dy_o[...] = dy
        dh = lax.dot_general(dy, w_ref[...], (((1,), (1,)), ((), ())), preferred_element_type=F32)
        _, vp = jax.vjp(_normmod, x_ref[...], g_ref[...], sh_ref[0], sc_ref[0])
        dx, dg, dsh, dsc = vp(dh)
        dx_o[...] = dx + dxres_ref[...]
        _acc_init(i == 0, [dg_o])
        _acc_init((i == 0) | (i == nlt), [dsh_o, dsc_o])
        dg_o[...] += dg
        dsh_o[0] += dsh
        dsc_o[0] += dsc

    row = lambda w: pl.BlockSpec((TR, w), lambda i: (i, 0))
    cls = pl.BlockSpec((1, 1, D), lambda i: (i // nlt, 0, 0))
    vec = pl.BlockSpec((1, D), lambda i: (0, 0))
    dts = lambda d: pl.BlockSpec((1, TR, 128), lambda i: (d, i, 0))
    return _pc(body, "in_bwd",
               [_sds((T, D)), _sds((T, NP_IN), MXU), _sds((1, D)), _sds((2, 1, D)), _sds((2, 1, D))],
               grid=(T // TR,),
               in_specs=[row(D), vec, cls, cls, _vm(), row(256), row(256), _vm(), row(D), row(256), row(256), row(512),
                         row(128), row(128), row(256), row(256), row(1024), dts(0), dts(1)],
               out_specs=[row(D), row(NP_IN), vec, cls, cls])(
        X, g, sh, sc, W, cos, sin, rm, dxres, dqa, dqb, dz, dka, dva, dkb, dvb, dxbc, ddt2, ddt2)


def tn_mm(A, G, bn, out_dtype):
    T, K = A.shape
    N = G.shape[1]
    bt = T // 4
    nt = T // bt

    def body(a_ref, g_ref, o_ref, acc):
        t = pl.program_id(1)
        _acc_init(t == 0, [acc])
        acc[...] += lax.dot_general(a_ref[...], g_ref[...], (((0,), (0,)), ((), ())), preferred_element_type=F32)

        @pl.when(t == nt - 1)
        def _():
            o_ref[...] = acc[...].astype(out_dtype)

    return _pc(body, "tn_mm", _sds((K, N), out_dtype), grid=(N // bn, nt),
               in_specs=[pl.BlockSpec((bt, K), lambda n, t: (t, 0)), pl.BlockSpec((bt, bn), lambda n, t: (t, n))],
               out_specs=pl.BlockSpec((K, bn), lambda n, t: (0, n)),
               scratch=[pltpu.VMEM((K, bn), F32)])(A, G)


def _ssm_out(yf, yb, xs, z, dsk, gs):
    y = (yf + yb + dsk * xs) * _silu(z)
    r = lax.rsqrt(jnp.mean(y * y, axis=-1, keepdims=True) + EPS)
    return y * r * gs


def out_fwd(oa, ob, y2, act, z, dsk, gs, W, X, gate, L):
    T = X.shape[0]
    TR = 256
    nlt = L // TR

    def body(oa_r, ob_r, yf_r, yb_r, xs_r, z_r, dsk_r, gs_r, w_ref, x_ref, gt_ref, x1_o, cat_o):
        oc = _ssm_out(yf_r[0], yb_r[0], xs_r[...], z_r[...], dsk_r[...], gs_r[...])
        cat = jnp.concatenate([oa_r[...], ob_r[...], oc], axis=1).astype(MXU)
        cat_o[...] = cat
        x1_o[...] = x_ref[...] + gt_ref[0] * jnp.dot(cat, w_ref[...], preferred_element_type=F32)

    row = lambda w: pl.BlockSpec((TR, w), lambda i: (i, 0))
    ys = lambda d: pl.BlockSpec((1, TR, 512), lambda i: (d, i, 0))
    cls = pl.BlockSpec((1, 1, D), lambda i: (i // nlt, 0, 0))
    v512 = pl.BlockSpec((1, 512), lambda i: (0, 0))
    return _pc(body, "out_fwd", [_sds((T, D)), _sds((T, D), MXU)], grid=(T // TR,),
               in_specs=[row(256), row(256), ys(0), ys(1), row(512), row(512), v512, v512, _vm(), row(D), cls],
               out_specs=[row(D), row(D)])(oa, ob, y2, y2, act, z, dsk, gs, W, X, gate)


def out_bwd(oa, ob, y2, act, z, dsk, gs, W, gate, dX1, L):
    T = dX1.shape[0]
    TR = 256
    nlt = L // TR

    def body(oa_r, ob_r, yf_r, yb_r, xs_r, z_r, dsk_r, gs_r, w_ref, gt_ref, dx1_r,
             doa_o, dob_o, dy_o, dxs_o, dz_o, dmix_o, ddsk_o, dgs_o, dgt_o):
        i = pl.program_id(0)
        w = w_ref[...]

        def f(oa_, ob_, yf, yb, xs, z_, dsk_, gs_, gt):
            oc = _ssm_out(yf, yb, xs, z_, dsk_, gs_)
            return gt * mm(jnp.concatenate([oa_, ob_, oc], axis=1), w)

        _, vjp = jax.vjp(f, oa_r[...], ob_r[...], yf_r[0], yb_r[0], xs_r[...], z_r[...], dsk_r[...], gs_r[...],
                         gt_ref[0])
        dx1 = dx1_r[...]
        doa, dob, dyf, _, dxs, dz, ddsk, dgs, dgt = vjp(dx1)
        doa_o[...] = doa
        dob_o[...] = dob
        dy_o[...] = dyf
        dxs_o[...] = dxs
        dz_o[...] = dz
        dmix_o[...] = (gt_ref[0] * dx1).astype(MXU)
        _acc_init(i == 0, [ddsk_o, dgs_o])
        _acc_init((i == 0) | (i == nlt), [dgt_o])
        ddsk_o[...] += ddsk
        dgs_o[...] += dgs
        dgt_o[0] += dgt

    row = lambda w: pl.BlockSpec((TR, w), lambda i: (i, 0))
    ys = lambda d: pl.BlockSpec((1, TR, 512), lambda i: (d, i, 0))
    cls = pl.BlockSpec((1, 1, D), lambda i: (i // nlt, 0, 0))
    v512 = pl.BlockSpec((1, 512), lambda i: (0, 0))
    return _pc(body, "out_bwd",
               [_sds((T, 256)), _sds((T, 256)), _sds((T, 512)), _sds((T, 512)), _sds((T, 512)), _sds((T, D), MXU),
                _sds((1, 512)), _sds((1, 512)), _sds((2, 1, D))],
               grid=(T // TR,),
               in_specs=[row(256), row(256), ys(0), ys(1), row(512), row(512), v512, v512, _vm(), cls, row(D)],
               out_specs=[row(256), row(256), row(512), row(512), row(512), row(D), v512, v512, cls])(
        oa, ob, y2, y2, act, z, dsk, gs, W, gate, dX1)


def _ffn_core(x, g, sh, sc, gt, wg, wu, wo, eg, eu):
    h = _normmod(x, g, sh, sc)
    a = mm(h, wg) + eg
    u = mm(h, wu) + eu
    act = _silu(a) * u
    return x + gt * mm(act, wo), (h, act)


def ffn_fwd(X, g, sh, sc, gate, Win, Wout, L, sends=()):
    T = X.shape[0]
    TR = 256
    nlt = L // TR

    def body(x_ref, g_ref, sh_ref, sc_ref, gt_ref, wi_ref, wo_ref, o_ref):
        h = _normmod(x_ref[...], g_ref[...], sh_ref[0], sc_ref[0]).astype(MXU)
        a = jnp.dot(h, wi_ref[:, 0:DFF], preferred_element_type=F32)
        u = jnp.dot(h, wi_ref[:, DFF:2 * DFF], preferred_element_type=F32)
        act = (_silu(a) * u).astype(MXU)
        o_ref[...] = x_ref[...] + gt_ref[0] * jnp.dot(act, wo_ref[...], preferred_element_type=F32)

    row = lambda w: pl.BlockSpec((TR, w), lambda i: (i, 0))
    cls = pl.BlockSpec((1, 1, D), lambda i: (i // nlt, 0, 0))
    vec = pl.BlockSpec((1, D), lambda i: (0, 0))
    return _pc(body, "ffn_fwd", _sds((T, D)), grid=(T // TR,),
               in_specs=[row(D), vec, cls, cls, cls, _vm(), _vm()], out_specs=row(D), sends=sends, gather=True)(
        X, g, sh, sc, gate, Win, Wout)


def ffn_bwd(X, g, sh, sc, gate, Win, Wout, dX2, L, sends=()):
    T = X.shape[0]
    TR = 256
    nlt = L // TR

    def body(x_ref, g_ref, sh_ref, sc_ref, gt_ref, wi_ref, wo_ref, dx2_r,
             dx_o, h_o, du_o, act_o, dout_o, dg_o, dsh_o, dsc_o, dgt_o):
        i = pl.program_id(0)
        wg, wu, wo = wi_ref[:, 0:DFF], wi_ref[:, DFF:2 * DFF], wo_ref[...]
        zero = jnp.zeros((TR, DFF), F32)
        f = lambda x, g_, sh_, sc_, gt, eg, eu: _ffn_core(x, g_, sh_, sc_, gt, wg, wu, wo, eg, eu)
        _, vjp, (h, act) = jax.vjp(f, x_ref[...], g_ref[...], sh_ref[0], sc_ref[0], gt_ref[0], zero, zero,
                                   has_aux=True)
        dx2 = dx2_r[...]
        dx, dg, dsh, dsc, dgt, da, du = vjp(dx2)
        dx_o[...] = dx
        h_o[...] = h.astype(MXU)
        du_o[...] = jnp.concatenate([da, du], axis=1).astype(MXU)
        act_o[...] = act.astype(MXU)
        dout_o[...] = (gt_ref[0] * dx2).astype(MXU)
        _acc_init(i == 0, [dg_o])
        _acc_init((i == 0) | (i == nlt), [dsh_o, dsc_o, dgt_o])
        dg_o[...] += dg
        dsh_o[0] += dsh
        dsc_o[0] += dsc
        dgt_o[0] += dgt

    row = lambda w: pl.BlockSpec((TR, w), lambda i: (i, 0))
    cls = pl.BlockSpec((1, 1, D), lambda i: (i // nlt, 0, 0))
    vec = pl.BlockSpec((1, D), lambda i: (0, 0))
    return _pc(body, "ffn_bwd",
               [_sds((T, D)), _sds((T, D), MXU), _sds((T, 2 * DFF), MXU), _sds((T, DFF), MXU), _sds((T, D), MXU),
                _sds((1, D)), _sds((2, 1, D)), _sds((2, 1, D)), _sds((2, 1, D))],
               grid=(T // TR,),
               in_specs=[row(D), vec, cls, cls, cls, _vm(), _vm(), row(D)],
               out_specs=[row(D), row(D), row(2 * DFF), row(DFF), row(D), vec, cls, cls, cls], sends=sends)(
        X, g, sh, sc, gate, Win, Wout, dX2)


def loss_head(X2, g, tgt, L):
    TR = 256

    def body(x_ref, g_ref, t_ref, loss_o, dx_o, dg_o):
        i = pl.program_id(0)

        def f(x, g_):
            y = x * lax.rsqrt(jnp.mean(x * x, axis=-1, keepdims=True) + EPS) * g_
            return 0.5 * jnp.sum(jnp.mean(jnp.square(y - t_ref[...]), axis=-1, keepdims=True), axis=0, keepdims=True)

        val, vjp = jax.vjp(f, x_ref[...], g_ref[...])
        dx, dg = vjp(jnp.ones((1, 1), F32))
        dx_o[...] = dx
        _acc_init(i == 0, [loss_o, dg_o])
        loss_o[...] += jnp.broadcast_to(val, (8, 128))
        dg_o[...] += dg

    row = pl.BlockSpec((TR, D), lambda i: (i, 0))
    vec = pl.BlockSpec((1, D), lambda i: (0, 0))
    return _pc(body, "loss_head", [_sds((8, 128)), _sds((L, D)), _sds((1, D))], grid=(L // TR,),
               in_specs=[row, vec, row], out_specs=[pl.BlockSpec((8, 128), lambda i: (0, 0)), row, vec])(X2, g, tgt)


def _stack_impl(q):
    lane = _iota(q.shape, 1)
    return jnp.concatenate([jnp.where(lane < HD, q, 0.0), jnp.where(lane >= HD, q, 0.0)], axis=0)


def _unstack_impl(o):
    M = o.shape[0] // 2
    return jnp.where(_iota((M, o.shape[1]), 1) < HD, o[:M], o[M:])


@jax.custom_vjp
def _stack(q):
    return _stack_impl(q)


_stack.defvjp(lambda q: (_stack_impl(q), None), lambda _, g: (_unstack_impl(g),))


@jax.custom_vjp
def _unstack(o):
    return _unstack_impl(o)


_unstack.defvjp(lambda o: (_unstack_impl(o), None), lambda _, g: (_stack_impl(g),))


def _softmax_av(q, ks, vs, biases, sink):
    q2 = _stack(q)
    ss = []
    for k, b in zip(ks, biases):
        s = mm_nt(q2, k) * (HD ** -0.5)
        ss.append(s if b is None else s + b)
    m = functools.reduce(jnp.maximum, [jnp.max(s, axis=1, keepdims=True) for s in ss])
    if sink is not None:
        m = jnp.maximum(m, sink)
    m = lax.stop_gradient(m)
    es = [jnp.exp(s - m) for s in ss]
    den = functools.reduce(lambda a, b_: a + b_, [jnp.sum(e, axis=1, keepdims=True) for e in es])
    if sink is not None:
        den = den + jnp.exp(sink - m)
    inv = 1.0 / den
    return _unstack(functools.reduce(lambda a, b_: a + b_, [mm(e * inv, v) for e, v in zip(es, vs)]))


def _sink_col(s0, s1, M):
    return jnp.concatenate([jnp.broadcast_to(jnp.mean(s0, axis=1, keepdims=True), (M, 1)),
                            jnp.broadcast_to(jnp.mean(s1, axis=1, keepdims=True), (M, 1))], axis=0)


def _stack4_impl(q):
    lane = _iota((q.shape[0], 128), 1)
    parts = []
    for p in range(2):
        qp = q[:, 128 * p:128 * (p + 1)]
        parts += [jnp.where(lane < HD, qp, 0.0), jnp.where(lane >= HD, qp, 0.0)]
    return jnp.concatenate(parts, axis=0)


def _unstack4_impl(o):
    M = o.shape[0] // 4
    lane = _iota((M, 128), 1)
    return jnp.concatenate([jnp.where(lane < HD, o[0:M], o[M:2 * M]),
                            jnp.where(lane < HD, o[2 * M:3 * M], o[3 * M:4 * M])], axis=1)


@jax.custom_vjp
def _stack4(q):
    return _stack4_impl(q)


_stack4.defvjp(lambda q: (_stack4_impl(q), None), lambda _, g: (_unstack4_impl(g),))


@jax.custom_vjp
def _unstack4(o):
    return _unstack4_impl(o)


_unstack4.defvjp(lambda o: (_unstack4_impl(o), None), lambda _, g: (_stack4_impl(g),))


def _wa_block(q, kp, kc, kn, vp, vc, vn, kx, vx, sks, n, L):
    kb = jnp.concatenate([kp, kc, kn], axis=0)
    vb = jnp.concatenate([vp, vc, vn], axis=0)
    qpos = n * Q + (_iota((4 * Q, 3 * Q), 0) & (Q - 1))
    kpos = (n - 1) * Q + _iota((4 * Q, 3 * Q), 1)
    valid = (jnp.abs(qpos - kpos) <= Q) & (kpos >= 0) & (kpos < L)
    bias = jnp.where(valid, 0.0, NEG)
    sink = jnp.concatenate([jnp.broadcast_to(jnp.mean(s_, axis=1, keepdims=True), (Q, 1)) for s_ in sks], axis=0)
    q4 = _stack4(q)
    sc = HD ** -0.5
    sl = mm_nt(q4, kb) * sc + bias
    sx = mm_nt(q4, kx) * sc
    m = lax.stop_gradient(jnp.maximum(jnp.maximum(jnp.max(sl, axis=1, keepdims=True),
                                                  jnp.max(sx, axis=1, keepdims=True)), sink))
    el, ex = jnp.exp(sl - m), jnp.exp(sx - m)
    inv = 1.0 / (jnp.sum(el, axis=1, keepdims=True) + jnp.sum(ex, axis=1, keepdims=True) + jnp.exp(sink - m))
    return _unstack4(mm(el * inv, vb) + mm(ex * inv, vx))


def _wa_specs(L):
    nb = L // Q
    qs = pl.BlockSpec((Q, 256), lambda n: (n, 0))
    kprev = pl.BlockSpec((Q, 128), lambda n: (jnp.maximum(n - 1, 0), 0))
    kcur = pl.BlockSpec((Q, 128), lambda n: (n, 0))
    knext = pl.BlockSpec((Q, 128), lambda n: (jnp.minimum(n + 1, nb - 1), 0))
    kctx = pl.BlockSpec((LC, 128), lambda n: (L // LC, 0))
    sks = pl.BlockSpec((2, 2, 1, 128), lambda n: (0, 0, 0, 0))
    return nb, qs, [kprev, kcur, knext], kctx, sks


def wa_fwd(QA, KA, VA, sinkp, L, sends=()):
    nb, qs, kband, kctx, sks = _wa_specs(L)

    def body(q_r, kp, kc, kn, vp, vc, vn, kx, vx, sk_r, o_ref):
        n = pl.program_id(0)
        f = lambda t: t[...].astype(F32)
        o_ref[...] = _wa_block(f(q_r), f(kp), f(kc), f(kn), f(vp), f(vc), f(vn), f(kx), f(vx),
                               [sk_r[0, 0], sk_r[0, 1], sk_r[1, 0], sk_r[1, 1]], n, L)

    return _pc(body, "wa_fwd", _sds((L, 256)), grid=(nb,),
               in_specs=[qs] + kband + kband + [kctx, kctx, sks], out_specs=qs, sends=sends, gather=True)(
        QA, KA, KA, KA, VA, VA, VA, KA, VA, sinkp)


def wa_bwd(QA, KA, VA, sinkp, dO, L, sends=()):
    nb, qs, kband, kctx, sks = _wa_specs(L)

    def body(q_r, kp, kc, kn, vp, vc, vn, kx, vx, sk_r, do_r, dq_o, dk_o, dv_o, dkx_o, dvx_o, dsk_o):
        n = pl.program_id(0)
        f = lambda t: t[...].astype(F32)
        fn = lambda q, a, b, c, d, e, g, kx_, vx_, s_: _wa_block(q, a, b, c, d, e, g, kx_, vx_, s_, n, L)
        _, vjp = jax.vjp(fn, f(q_r), f(kp), f(kc), f(kn), f(vp), f(vc), f(vn), f(kx), f(vx),
                         [sk_r[0, 0], sk_r[0, 1], sk_r[1, 0], sk_r[1, 1]])
        dq, dkp, dkc, dkn, dvp, dvc, dvn, dkx, dvx, ds = vjp(do_r[...])
        dq_o[...] = dq
        _acc_init(n == 0, [dk_o, dv_o, dkx_o, dvx_o, dsk_o])
        rows = pl.ds(pl.multiple_of(n * Q, Q), 3 * Q)
        dk_o[rows, :] += jnp.concatenate([dkp, dkc, dkn], axis=0)
        dv_o[rows, :] += jnp.concatenate([dvp, dvc, dvn], axis=0)
        dkx_o[...] += dkx
        dvx_o[...] += dvx
        for i_ in range(4):
            dsk_o[i_ // 2, i_ % 2] += ds[i_]

    full = lambda r: pl.BlockSpec((r, 128), lambda n: (0, 0))
    return _pc(body, "wa_bwd",
               [_sds((L, 256)), _sds((L + 2 * Q, 128)), _sds((L + 2 * Q, 128)), _sds((LC, 128)), _sds((LC, 128)),
                _sds((2, 2, 1, 128))],
               grid=(nb,), in_specs=[qs] + kband + kband + [kctx, kctx, sks, qs],
               out_specs=[qs, full(L + 2 * Q), full(L + 2 * Q), full(LC), full(LC), sks], sends=sends)(
        QA, KA, KA, KA, VA, VA, VA, KA, VA, sinkp, dO)


def _ctx_block(q, kx, vx, s0, s1):
    return _softmax_av(q, [kx], [vx], [None], _sink_col(s0, s1, LC))


def ctx_fwd(Qx, Kx, Vx, sinkp, shared, L):
    cq = pl.BlockSpec((LC, 128), lambda p: (L // LC, p))
    ck = pl.BlockSpec((LC, 128), lambda p: (L // LC, 0 if shared else p))
    sks = pl.BlockSpec((1, 2, 1, 128), lambda p: (p, 0, 0, 0))

    def body(q_r, k_r, v_r, sk_r, o_ref):
        f = lambda t: t[...].astype(F32)
        o_ref[...] = _ctx_block(f(q_r), f(k_r), f(v_r), sk_r[0, 0], sk_r[0, 1])

    return _pc(body, "ctx_fwd", _sds((LC, 256)), grid=(2,), in_specs=[cq, ck, ck, sks],
               out_specs=pl.BlockSpec((LC, 128), lambda p: (0, p)))(Qx, Kx, Vx, sinkp)


def ctx_bwd(Qx, Kx, Vx, sinkp, dO, shared, L):
    cq = pl.BlockSpec((LC, 128), lambda p: (L // LC, p))
    ck = pl.BlockSpec((LC, 128), lambda p: (L // LC, 0 if shared else p))
    sks = pl.BlockSpec((1, 2, 1, 128), lambda p: (p, 0, 0, 0))
    op = pl.BlockSpec((LC, 128), lambda p: (0, p))
    ok = pl.BlockSpec((LC, 128), lambda p: (0, 0 if shared else p))
    dos = pl.BlockSpec((LC, 128), lambda p: (L // LC, p))

    def body(q_r, k_r, v_r, sk_r, do_r, dq_o, dk_o, dv_o, dsk_o):
        p = pl.program_id(0)
        f = lambda t: t[...].astype(F32)
        _, vjp = jax.vjp(_ctx_block, f(q_r), f(k_r), f(v_r), sk_r[0, 0], sk_r[0, 1])
        dq, dk, dv, ds0, ds1 = vjp(do_r[...])
        dq_o[...] = dq
        _acc_init((p == 0) if shared else (p >= 0), [dk_o, dv_o])
        dk_o[...] += dk
        dv_o[...] += dv
        dsk_o[0, 0] = ds0
        dsk_o[0, 1] = ds1

    kw = 128 if shared else 256
    return _pc(body, "ctx_bwd", [_sds((LC, 256)), _sds((LC, kw)), _sds((LC, kw)), _sds((2, 2, 1, 128))],
               grid=(2,), in_specs=[cq, ck, ck, sks, dos], out_specs=[op, ok, ok, sks])(Qx, Kx, Vx, sinkp, dO)


def _na_rows(qs, kws, vws, kx, vx, bs):
    sc = HD ** -0.5
    q2 = [_stack(q) for q in qs]
    sl = [mm_nt(a, k) * sc + b for a, k, b in zip(q2, kws, bs)]
    sx = [mm_nt(a, kx) * sc for a in q2]
    m = [lax.stop_gradient(jnp.maximum(jnp.max(a, axis=1, keepdims=True), jnp.max(b, axis=1, keepdims=True)))
         for a, b in zip(sl, sx)]
    el = [jnp.exp(a - c) for a, c in zip(sl, m)]
    ex = [jnp.exp(a - c) for a, c in zip(sx, m)]
    inv = [1.0 / (jnp.sum(a, axis=1, keepdims=True) + jnp.sum(b, axis=1, keepdims=True)) for a, b in zip(el, ex)]
    o2 = [mm(a * i, v) + mm(b * i, vx) for a, b, i, v in zip(el, ex, inv, vws)]
    return [_unstack(o) for o in o2]


def _na_geom(rb, j, R):
    r = rb * 8 + j
    s = jnp.clip(r - 4, 0, R - 8)
    cls = jnp.where(r < 4, r, jnp.where(r > R - 4, r - (R - 8), 4))
    return pl.ds(pl.multiple_of(s * GW, GW), 8 * GW), cls


def _na_load(q_r, k_r, v_r, b_r, rb, R):
    geo = [_na_geom(rb, j, R) for j in range(8)]
    qs = [q_r[j * GW:(j + 1) * GW, :].astype(F32) for j in range(8)]
    kws = [k_r[win, :].astype(F32) for win, _ in geo]
    vws = [v_r[win, :].astype(F32) for win, _ in geo]
    bs = [jnp.concatenate([b_r[0, cls], b_r[1, cls]], axis=0) for _, cls in geo]
    return geo, qs, kws, vws, bs


def na_fwd(QB, KB, VB, biasd, L):
    R = L // GW
    qs = pl.BlockSpec((8 * GW, 128), lambda p, rb: (rb, p))
    kfull = pl.BlockSpec((L, 128), lambda p, rb: (0, p))
    kctx = pl.BlockSpec((LC, 128), lambda p, rb: (L // LC, p))
    bs = pl.BlockSpec((2, 8, GW, 8 * GW), lambda p, rb: (p, 0, 0, 0))

    def body(q_r, k_r, v_r, kx_r, vx_r, b_r, o_ref):
        _, qs_, kws, vws, bs_ = _na_load(q_r, k_r, v_r, b_r, pl.program_id(1), R)
        outs = _na_rows(qs_, kws, vws, kx_r[...].astype(F32), vx_r[...].astype(F32), bs_)
        o_ref[...] = jnp.concatenate(outs, axis=0)

    return _pc(body, "na_fwd", _sds((L, 256)), grid=(2, R // 8), in_specs=[qs, kfull, kfull, kctx, kctx, bs],
               out_specs=qs)(QB, KB, VB, KB, VB, biasd)


def na_bwd(QB, KB, VB, biasd, dO, L):
    R = L // GW
    qs = pl.BlockSpec((8 * GW, 128), lambda p, rb: (rb, p))
    kfull = pl.BlockSpec((L, 128), lambda p, rb: (0, p))
    kctx = pl.BlockSpec((LC, 128), lambda p, rb: (L // LC, p))
    bs = pl.BlockSpec((2, 8, GW, 8 * GW), lambda p, rb: (p, 0, 0, 0))
    oc = pl.BlockSpec((LC, 128), lambda p, rb: (0, p))

    def body(q_r, k_r, v_r, kx_r, vx_r, b_r, do_r, dq_o, dk_o, dv_o, dkx_o, dvx_o, db_o):
        rb = pl.program_id(1)
        _acc_init(rb == 0, [dk_o, dv_o, dkx_o, dvx_o, db_o])
        geo, qs_, kws, vws, bs_ = _na_load(q_r, k_r, v_r, b_r, rb, R)
        _, vjp = jax.vjp(_na_rows, qs_, kws, vws, kx_r[...].astype(F32), vx_r[...].astype(F32), bs_)
        dqs, dkws, dvws, dkx, dvx, dbs = vjp([do_r[j * GW:(j + 1) * GW, :] for j in range(8)])
        dq_o[...] = jnp.concatenate(dqs, axis=0)
        dkx_o[...] += dkx
        dvx_o[...] += dvx
        for j, (win, cls) in enumerate(geo):
            dk_o[win, :] += dkws[j]
            dv_o[win, :] += dvws[j]
            db_o[0, cls] += dbs[j][:GW]
            db_o[1, cls] += dbs[j][GW:]

    return _pc(body, "na_bwd",
               [_sds((L, 256)), _sds((L, 256)), _sds((L, 256)), _sds((LC, 256)), _sds((LC, 256)),
                _sds((4, 8, GW, 8 * GW))],
               grid=(2, R // 8), in_specs=[qs, kfull, kfull, kctx, kctx, bs, qs],
               out_specs=[qs, kfull, kfull, oc, oc, bs])(QB, KB, VB, KB, VB, biasd, dO)


def exact_mm_call(A, B):
    def body(a_ref, b_ref, o_ref):
        o_ref[...] = _exact(a_ref[...], b_ref[...])

    return _pc(body, "exact_mm", _sds((A.shape[0], B.shape[1])))(A, B)


def _conv_shift(x, d, L):
    T = x.shape[0]
    if d == 0:
        return x
    t = _iota(x.shape, 0)
    src = t + d
    ok = (src >= 0) & (src < T) & ((src >= L) == (t >= L))
    return jnp.where(ok, pltpu.roll(x, (-d) % T, 0), 0.0)


def conv_fwd(XBC, w8, b, L):
    T = XBC.shape[0]

    def body(x_ref, w_ref, b_ref, o_ref):
        x = x_ref[...]
        pre = b_ref[...] + functools.reduce(
            lambda a, c: a + c, [_conv_shift(x, k - 3, L) * w_ref[k:k + 1, :] for k in range(7)])
        o_ref[...] = _silu(pre)

    col = pl.BlockSpec((T, 128), lambda j: (0, j))
    return _pc(body, "conv_fwd", _sds((T, 1024)), grid=(8,),
               in_specs=[col, pl.BlockSpec((8, 128), lambda j: (0, j)), pl.BlockSpec((1, 128), lambda j: (0, j))],
               out_specs=col)(XBC, w8, b)


def conv_bwd(XBC, w8, b, dS, dxs_skip, L):
    T = XBC.shape[0]

    def body(x_ref, w_ref, b_ref, d0_r, d1_r, dsk_r, dx_o, dw_o, db_o):
        j = pl.program_id(0)
        x = x_ref[...]
        xs = [_conv_shift(x, k - 3, L) for k in range(7)]
        pre = b_ref[...] + functools.reduce(lambda a, c: a + c, [xs[k] * w_ref[k:k + 1, :] for k in range(7)])
        _, vjp = jax.vjp(_silu, pre)
        dact = d0_r[0] + d1_r[0] + jnp.where(j < 4, dsk_r[...], 0.0)
        dpre, = vjp(dact)
        dx_o[...] = functools.reduce(
            lambda a, c: a + c, [_conv_shift(dpre, 3 - k, L) * w_ref[k:k + 1, :] for k in range(7)])
        dw_o[...] = jnp.concatenate([jnp.sum(dpre * xs[k], axis=0, keepdims=True) for k in range(7)]
                                    + [jnp.zeros((1, 128), F32)], axis=0)
        db_o[...] = jnp.sum(dpre, axis=0, keepdims=True)

    col = pl.BlockSpec((T, 128), lambda j: (0, j))
    w_s = pl.BlockSpec((8, 128), lambda j: (0, j))
    b_s = pl.BlockSpec((1, 128), lambda j: (0, j))
    ds = lambda d: pl.BlockSpec((1, T, 128), lambda j: (d, 0, j))
    return _pc(body, "conv_bwd", [_sds((T, 1024)), _sds((8, 1024)), _sds((1, 1024))], grid=(8,),
               in_specs=[col, w_s, b_s, ds(0), ds(1), pl.BlockSpec((T, 128), lambda j: (0, jnp.minimum(j, 3)))],
               out_specs=[col, w_s, b_s])(XBC, w8, b, dS, dS, dxs_skip)


def _ssd_chunk(xs, bs, cs, dtraw, dtb, alog, hs, tri, d):
    dt = _softplus(dtraw + dtb)
    a = dt * (-jnp.exp(alog))
    acum = _exact(tri, a)
    tot = jnp.sum(a, axis=0, keepdims=True)
    wcol = jnp.exp(tot - acum) * dt
    ea = jnp.exp(acum)
    cd = jnp.exp(tot)
    acum_t, dt_t = acum.T, dt.T
    lane = _iota((Q, 128), 1)
    srow = _iota((128, Q), 0)
    lane1 = _iota((1, 128), 1)
    prow = _iota((128, NSTATE), 0)
    mask = tri > 0.5
    cbs = [mm_nt(cs[g], bs[g]) for g in range(2)]
    ys, hn = [], []
    for j in range(4):
        g = j // 2
        x = xs[j]
        yi, st, eac, cdl = [], [], [], []
        for u in range(2):
            slot = d * 8 + 2 * j + u
            col = lambda m: jnp.sum(jnp.where(lane == slot, m, 0.0), axis=1, keepdims=True)
            rowv = lambda m: jnp.sum(jnp.where(srow == slot, m, 0.0), axis=0, keepdims=True)
            seg = col(acum) - rowv(acum_t)
            dcy = jnp.where(mask, jnp.exp(jnp.where(mask, seg, 0.0)), 0.0)
            yi.append(mm(cbs[g] * dcy * rowv(dt_t), x))
            st.append(mm_tn(x, bs[g] * col(wcol)))
            eac.append(col(ea))
            cdl.append(jnp.sum(jnp.where(lane1 == slot, cd, 0.0), axis=1, keepdims=True))
        yin = mm_nt(cs[g], hs[j])
        ys.append(jnp.where(lane < HD, yi[0] + yin * eac[0], yi[1] + yin * eac[1]))
        hn.append(hs[j] * jnp.where(prow < HD, cdl[0], cdl[1]) + jnp.where(prow < HD, st[0], st[1]))
    return ys, hn


def _ssd_chunk_idx(d, s, nlc, nch):
    return jnp.where(d == 0, (s + nlc) % nch, nch - 1 - s)


def ssd_fwd(ACT, DT, dtb, alog, tri2, L, sends=()):
    T = ACT.shape[0]
    nlc, nch = L // Q, T // Q

    def body(a_ref, dt_ref, dtb_ref, al_ref, tri_ref, y_o, hs_o, hst):
        d, s = pl.program_id(0), pl.program_id(1)
        _acc_init(s == 0, [hst])
        a = a_ref[...]
        xs = [a[:, 128 * j:128 * (j + 1)] for j in range(4)]
        bs = [a[:, 512 + 128 * g:640 + 128 * g] for g in range(2)]
        cs = [a[:, 768 + 128 * g:896 + 128 * g] for g in range(2)]
        hs = [hst[j] for j in range(4)]
        hs_o[0, 0] = hst[...]
        ys, hn = _ssd_chunk(xs, bs, cs, dt_ref[...], dtb_ref[...], al_ref[...], hs, tri_ref[0], d)
        y_o[0] = jnp.concatenate(ys, axis=1)
        for j in range(4):
            hst[j] = hn[j]

    ck = lambda w: pl.BlockSpec((Q, w), lambda d, s: (_ssd_chunk_idx(d, s, nlc, nch), 0))
    v128 = pl.BlockSpec((1, 128), lambda d, s: (0, 0))
    return _pc(body, "ssd_fwd", [_sds((2, T, 512)), _sds((2, nch, 4, 128, NSTATE))], grid=(2, nch),
               in_specs=[ck(1024), ck(128), v128, v128, pl.BlockSpec((1, Q, Q), lambda d, s: (d, 0, 0))],
               out_specs=[pl.BlockSpec((1, Q, 512), lambda d, s: (d, _ssd_chunk_idx(d, s, nlc, nch), 0)),
                          pl.BlockSpec((1, 1, 4, 128, NSTATE), lambda d, s: (d, s, 0, 0, 0))],
               scratch=[pltpu.VMEM((4, 128, NSTATE), F32)], sends=sends, gather=True)(ACT, DT, dtb, alog, tri2)


def ssd_bwd(ACT, DT, dtb, alog, tri2, HS, dY, L, sends=()):
    T = ACT.shape[0]
    nlc, nch = L // Q, T // Q

    def body(a_ref, dt_ref, dtb_ref, al_ref, tri_ref, hs_ref, dy_ref, da_o, ddt_o, ddtb_o, dal_o, dh):
        d, sr = pl.program_id(0), pl.program_id(1)
        _acc_init(sr == 0, [dh, ddtb_o, dal_o])
        a = a_ref[...]
        xs = [a[:, 128 * j:128 * (j + 1)] for j in range(4)]
        bs = [a[:, 512 + 128 * g:640 + 128 * g] for g in range(2)]
        cs = [a[:, 768 + 128 * g:896 + 128 * g] for g in range(2)]
        hs = [hs_ref[0, 0, j] for j in range(4)]
        tri = tri_ref[0]
        fn = lambda xs_, bs_, cs_, dtr, dtb_, al, hs_: _ssd_chunk(xs_, bs_, cs_, dtr, dtb_, al, hs_, tri, d)
        _, vjp = jax.vjp(fn, xs, bs, cs, dt_ref[...], dtb_ref[...], al_ref[...], hs)
        dy = dy_ref[...]
        dys = [dy[:, 128 * j:128 * (j + 1)] for j in range(4)]
        dxs, dbs, dcs, ddt, ddtb, dal, dhs = vjp((dys, [dh[j] for j in range(4)]))
        da_o[0] = jnp.concatenate(dxs + dbs + dcs, axis=1)
        ddt_o[0] = ddt
        ddtb_o[0] += ddtb
        dal_o[0] += dal
        for j in range(4):
            dh[j] = dhs[j]

    cidx = lambda d, sr: _ssd_chunk_idx(d, nch - 1 - sr, nlc, nch)
    ck = lambda w: pl.BlockSpec((Q, w), lambda d, sr: (cidx(d, sr), 0))
    v128 = pl.BlockSpec((1, 128), lambda d, sr: (0, 0))
    o128 = pl.BlockSpec((1, 1, 128), lambda d, sr: (d, 0, 0))
    return _pc(body, "ssd_bwd", [_sds((2, T, 1024)), _sds((2, T, 128)), _sds((2, 1, 128)), _sds((2, 1, 128))],
               grid=(2, nch),
               in_specs=[ck(1024), ck(128), v128, v128, pl.BlockSpec((1, Q, Q), lambda d, sr: (d, 0, 0)),
                         pl.BlockSpec((1, 1, 4, 128, NSTATE), lambda d, sr: (d, nch - 1 - sr, 0, 0, 0)), ck(512)],
               out_specs=[pl.BlockSpec((1, Q, 1024), lambda d, sr: (d, cidx(d, sr), 0)),
                          pl.BlockSpec((1, Q, 128), lambda d, sr: (d, cidx(d, sr), 0)), o128, o128],
               scratch=[pltpu.VMEM((4, 128, NSTATE), F32)], sends=sends)(ACT, DT, dtb, alog, tri2, HS, dY)


_QA_PERM = np.concatenate([np.arange(HD * h, HD * h + HD) for h in (0, 2, 1, 3)])
_PAIR_HEADS = np.array([[0, 2], [1, 3]])


def _tables(L):
    t = jnp.arange(L)
    inv = 10000.0 ** (-jnp.arange(16, dtype=F32) / 16)

    def half(pos):
        ang = pos.astype(F32)[:, None] * inv[None, :]
        return jnp.concatenate([ang, ang], axis=1)

    ang = jnp.tile(jnp.concatenate([half(t // GW), half(t % GW)], axis=1), (1, 4))
    cos = jnp.concatenate([jnp.cos(ang), jnp.ones((LC, 256), F32)], axis=0)
    sin = jnp.concatenate([jnp.sin(ang), jnp.zeros((LC, 256), F32)], axis=0)
    rm = np.zeros((256, 256), np.float32)
    for j in range(256):
        if j % 32 < 16:
            rm[j + 16, j] = -1.0
        else:
            rm[j - 16, j] = 1.0
    tri = np.tril(np.ones((Q, Q), np.float32))
    return cos, sin, jnp.asarray(rm), jnp.asarray(np.stack([tri, tri.T]))


def _na_index(R):
    rc = np.array([0, 1, 2, 3, 4, R - 3, R - 2, R - 1])
    dy = np.clip(rc - 4, 0, R - 8)[:, None] + np.arange(8)[None, :] - rc[:, None] + 7
    qc, cc = np.arange(GW)[:, None], np.arange(GW)[None, :]
    dx = np.clip(cc - qc, -15, 15) + 15
    cstart = np.clip(qc - 8, 0, GW - 16)
    cmask = (cc >= cstart) & (cc < cstart + 16)
    idx = dy[:, None, :, None] * 31 + dx[None, :, None, :]
    return idx.reshape(8, GW, 8 * GW), np.broadcast_to(cmask[None, :, None, :], idx.shape).reshape(8, GW, 8 * GW), \
        dy, dx, cmask


def _na_bias(rpb, R):
    _, cm, dy, dx, _ = _na_index(R)
    e1t = np.zeros((128, GW * GW), np.float32)
    e1t[dx.reshape(-1), np.arange(GW * GW)] = 1.0
    v = jnp.pad(rpb[:, dy.reshape(-1), :].reshape(256, 31), ((0, 0), (0, 97)))
    full = exact_mm_call(v, jnp.asarray(e1t))
    dense = full.reshape(4, 8, 8, GW, GW).transpose(0, 1, 3, 2, 4).reshape(4, 8, GW, 8 * GW)
    return jnp.where(cm[None], dense, NEG)


def _na_bias_grad(dbias, R):
    _, _, dy, dx, cmask = _na_index(R)
    e1 = np.zeros((GW * GW, 128), np.float32)
    e1[np.arange(GW * GW), dx.reshape(-1)] = cmask.reshape(-1)
    a1 = dbias.reshape(4, 8, GW, 8, GW).transpose(0, 1, 3, 2, 4).reshape(256, GW * GW)
    v = exact_mm_call(a1, jnp.asarray(e1))[:, :31].reshape(4, 64, 31)
    e2 = np.zeros((64, 128), np.float32)
    e2[np.arange(64), dy.reshape(-1)] = 1.0
    a2 = jnp.pad(v.transpose(0, 2, 1).reshape(124, 64), ((0, 4), (0, 0)))
    return exact_mm_call(a2, jnp.asarray(e2))[:124, :15].reshape(4, 31, 15).transpose(0, 2, 1)


def _lanes(v, n=128):
    v = v.reshape(1, -1)
    return jnp.pad(v, ((0, 0), (0, n - v.shape[1])))


def _cls2(a, b):
    return jnp.stack([a, b]).reshape(2, 1, D)


def _win_p(g):
    w = g.transpose(1, 0, 2).reshape(D, IN_COLS)
    return jnp.concatenate([w[:, _QA_PERM], w[:, 256:], jnp.zeros((D, NP_IN - IN_COLS), w.dtype)], axis=1)


def _wout_p(g):
    w = g.reshape(D, D)
    return jnp.concatenate([w[_QA_PERM, :], w[256:, :]], axis=0)


def _wfi(ga, gb):
    return jnp.concatenate([ga, gb], axis=1).transpose(1, 0, 2).reshape(D, 2 * DFF)


def _layer_consts(p):
    sinkp = jnp.broadcast_to(p["wa_sink"][_PAIR_HEADS][:, :, None, None], (2, 2, 1, 128))
    return dict(
        sinkp=sinkp, nosink=jnp.full((2, 2, 1, 128), NEG, F32),
        w8=jnp.concatenate([p["ssm_conv_w"], jnp.zeros((1, 1024), F32)], axis=0),
        cb=p["ssm_conv_b"].reshape(1, 1024), dtb=_lanes(p["ssm_dt_bias"]), alog=_lanes(p["ssm_a_log"]),
        dsk=jnp.repeat(p["ssm_d"], HD).reshape(1, 512), gs=p["ssm_norm_g"].reshape(1, 512),
        gmix=p["g_mix"].reshape(1, D), gffn=p["g_ffn"].reshape(1, D))


def _mods(mod2):
    return [_cls2(mod2[0, D * k:D * (k + 1)], mod2[1, D * k:D * (k + 1)]) for k in range(6)]


def _layer_fwd(X, mod2, c, rpb, tabs, L, ctx_out, ffn_shards, nxt):
    cos, sin, rm, tri2 = tabs
    sh1, sc1, gt1, sh2, sc2, gt2 = _mods(mod2)
    biasd = _na_bias(rpb, L // GW)
    fi, fo = ffn_shards
    (qa, qb, z, ka, va, kb, vb, xbc, dt, h1), (gfo,) = in_fwd(X, c["gmix"], sh1, sc1, c["win"], cos, sin, rm, L,
                                                              sends=(fo,))
    (oa,), (gfa,) = wa_fwd(qa, ka, va, c["sinkp"], L, sends=(fi[:D // 2],))
    ob = na_fwd(qb, kb, vb, biasd, L)
    if ctx_out:
        oa_c = ctx_fwd(qa, ka, va, c["sinkp"], True, L)
        ob_c = ctx_fwd(qb, kb, vb, c["nosink"], False, L)
    else:
        oa_c = ob_c = jnp.zeros((LC, 256), F32)
    oa = jnp.concatenate([oa, oa_c], axis=0)
    ob = jnp.concatenate([ob, ob_c], axis=0)
    act = conv_fwd(xbc, c["w8"], c["cb"], L)
    (y2, hs), (gfb,) = ssd_fwd(act, dt, c["dtb"], c["alog"], tri2, L, sends=(fi[D // 2:],))
    X1, cat = out_fwd(oa, ob, y2, act, z, c["dsk"], c["gs"], c["wout"], X, gt1, L)
    c = dict(c, wfi=_wfi(gfa, gfb), wfo=gfo.reshape(DFF, D))
    res = ffn_fwd(X1, c["gffn"], sh2, sc2, gt2, c["wfi"], c["wfo"], L, sends=nxt)
    (X2,), got = res if nxt else ((res,), ())
    saved = dict(X=X, X1=X1, qa=qa, qb=qb, z=z, ka=ka, va=va, kb=kb, vb=vb, xbc=xbc, dt=dt, h1=h1, oa=oa, ob=ob,
                 act=act, y2=y2, hs=hs, cat=cat, biasd=biasd)
    return X2, saved, c, got


def _col_blocks(gw):
    return gw.reshape(gw.shape[0], NDEV, gw.shape[1] // NDEV).transpose(1, 0, 2)


def _row_blocks(gw):
    return gw.reshape(NDEV, gw.shape[0] // NDEV, gw.shape[1])


def _layer_bwd(dX2, s, mod2, c, tabs, L, ctx_out, carry):
    cos, sin, rm, tri2 = tabs
    sh1, sc1, gt1, sh2, sc2, gt2 = _mods(mod2)
    R = L // GW
    res = ffn_bwd(s["X1"], c["gffn"], sh2, sc2, gt2, c["wfi"], c["wfo"], dX2, L, sends=carry)
    (dX1, h2, dU, actf, dOut, dgffn, dsh2, dsc2, dgt2), got = res if carry else (res, ())
    g = {}
    gfi = _col_blocks(tn_mm(h2, dU, 1408, MXU))
    gfo = _row_blocks(tn_mm(actf, dOut, 512, MXU))
    doa, dob, dy, dxs_skip, dz, dmix, ddsk, dgs, dgt1 = out_bwd(s["oa"], s["ob"], s["y2"], s["act"], s["z"], c["dsk"],
                                                                c["gs"], c["wout"], gt1, dX1, L)
    dwout = tn_mm(s["cat"], dmix, 512, MXU)
    gout = _row_blocks(jnp.concatenate([dwout[_QA_PERM, :], dwout[256:, :]], axis=0))
    (dS, ddt2, ddtb, dal), (g["w_ffn_in"], g["w_ffn_out"]) = ssd_bwd(
        s["act"], s["dt"], c["dtb"], c["alog"], tri2, s["hs"], dy, L, sends=(gfi, gfo))
    dxbc, dw8, dcb = conv_bwd(s["xbc"], c["w8"], c["cb"], dS, dxs_skip, L)
    (dqa, dkpad, dvpad, dkxa, dvxa, dska), (g["w_out"],) = wa_bwd(s["qa"], s["ka"], s["va"], c["sinkp"], doa, L,
                                                                  sends=(gout,))
    dqb, dkb, dvb, dkxb, dvxb, dbias = na_bwd(s["qb"], s["kb"], s["vb"], s["biasd"], dob, L)
    if ctx_out:
        dqa_c, dk1, dv1, dsk1 = ctx_bwd(s["qa"], s["ka"], s["va"], c["sinkp"], doa, True, L)
        dqb_c, dk2, dv2, _ = ctx_bwd(s["qb"], s["kb"], s["vb"], c["nosink"], dob, False, L)
        dkxa, dvxa, dska = dkxa + dk1, dvxa + dv1, dska + dsk1
        dkxb, dvxb = dkxb + dk2, dvxb + dv2
    else:
        dqa_c = dqb_c = jnp.zeros((LC, 256), F32)
    cat0 = lambda a, b: jnp.concatenate([a, b], axis=0)
    dX, dycat, dgmix, dsh1, dsc1 = in_bwd(
        s["X"], c["gmix"], sh1, sc1, c["win"], cos, sin, rm, dX1, cat0(dqa, dqa_c), cat0(dqb, dqb_c), dz,
        cat0(dkpad[Q:L + Q], dkxa), cat0(dvpad[Q:L + Q], dvxa), cat0(dkb, dkxb), cat0(dvb, dvxb), dxbc, ddt2, L)
    dwin = tn_mm(s["h1"], dycat, 1024, MXU)
    gin = _col_blocks(jnp.concatenate([dwin[:, _QA_PERM], dwin[:, 256:IN_COLS]], axis=1))
    g["g_mix"] = dgmix.reshape(D)
    g["g_ffn"] = dgffn.reshape(D)
    sk = jnp.sum(dska, axis=(2, 3))
    g["wa_sink"] = jnp.zeros((4,), F32).at[_PAIR_HEADS.reshape(-1)].set(sk.reshape(-1))
    g["na_rpb"] = _na_bias_grad(dbias, R)
    g["ssm_conv_w"] = dw8[:7]
    g["ssm_conv_b"] = dcb.reshape(1024)
    g["ssm_dt_bias"] = (ddtb[0] + ddtb[1])[0, :16].reshape(2, 8)
    g["ssm_a_log"] = (dal[0] + dal[1])[0, :16].reshape(2, 8)
    g["ssm_d"] = jnp.sum(ddsk.reshape(8, HD), axis=1)
    g["ssm_norm_g"] = dgs.reshape(512)
    dmod2 = jnp.concatenate([dsh1, dsc1, dgt1, dsh2, dsc2, dgt2], axis=2).reshape(2, 6 * D)
    return dX, g, dmod2, gin, got


def local_step(x, ctx, tgt, mods, layers, shards, g_final, L):
    tabs = _tables(L)
    X = jnp.concatenate([x, ctx], axis=0)
    consts = [_layer_consts(p) for p in layers]
    saved = []
    got = all_gather([shards["w_in"][0], shards["w_out"][0]], "gather_first")
    for i in range(2):
        consts[i] = dict(consts[i], win=_win_p(got[0]), wout=_wout_p(got[1]))
        nxt = (shards["w_in"][1], shards["w_out"][1]) if i == 0 else ()
        X, s, consts[i], got = _layer_fwd(X, mods[i], consts[i], layers[i]["na_rpb"], tabs, L, i == 0,
                                          (shards["w_ffn_in"][i], shards["w_ffn_out"][i]), nxt)
        saved.append(s)
    loss8, dxl, dgfin = loss_head(X, g_final.reshape(1, D), tgt, L)
    dX = jnp.concatenate([dxl, jnp.zeros((LC, D), F32)], axis=0)
    grads, dmods = [None, None], [None, None]
    dX, grads[1], dmods[1], gin1, _ = _layer_bwd(dX, saved[1], mods[1], consts[1], tabs, L, False, ())
    dX, grads[0], dmods[0], gin0, (grads[1]["w_in"],) = _layer_bwd(dX, saved[0], mods[0], consts[0], tabs, L, True,
                                                                   (gin1,))
    grads[0]["w_in"], = all_to_all([gin0], "exchange_last")
    return loss8[0, 0], dX[:L], grads, jnp.stack(dmods), dgfin.reshape(D)


def _place():
    x, y, c = lax.axis_index("x"), lax.axis_index("y"), lax.axis_index("c")
    return x, y, c


def _slot(b):
    return 4 * b[0] + 2 * b[1] + b[2]


def _any():
    return pl.BlockSpec(memory_space=pl.ANY)


def all_gather(xs, name):
    n = len(xs)

    def body(*refs):
        x_refs, o_refs = refs[:n], refs[n:2 * n]
        send_sems, recv_sems, local_sems = refs[2 * n:]
        x, y, c = _place()
        me, sib = (x, y, c), (x, y, 1 - c)
        chips = [(1 - x, y), (x, 1 - y), (1 - x, 1 - y)]

        def copy(t, k, blk, to, src=None):
            dst = o_refs[t].at[_slot(blk)]
            return pltpu.make_async_remote_copy(
                src_ref=dst if src is None else src, dst_ref=dst, send_sem=send_sems.at[7 * t + k],
                recv_sem=recv_sems.at[7 * t + k], device_id=to, device_id_type=MESH_T)

        mine = [pltpu.make_async_copy(x_refs[t], o_refs[t].at[_slot(me)], local_sems.at[t]) for t in range(n)]
        for cp in mine:
            cp.start()
        first = []
        for t in range(n):
            first.append(copy(t, 0, me, sib, src=x_refs[t]))
            first += [copy(t, 1 + j, me, (*chip, c), src=x_refs[t]) for j, chip in enumerate(chips)]
        for cp in first:
            cp.start()
        passed = []
        for j, chip in enumerate(chips):
            for t in range(n):
                copy(t, 1 + j, (*chip, c), me).wait_recv()
                cp = copy(t, 4 + j, (*chip, c), sib)
                cp.start()
                passed.append(cp)
        for t in range(n):
            copy(t, 0, sib, me).wait_recv()
            for j, chip in enumerate(chips):
                copy(t, 4 + j, (*chip, 1 - c), me).wait_recv()
        for cp in first + passed:
            cp.wait_send()
        for cp in mine:
            cp.wait()

    return pl.pallas_call(
        body, name=name, out_shape=[_sds((NDEV,) + a.shape, a.dtype) for a in xs],
        in_specs=[_any()] * n, out_specs=[_any()] * n,
        scratch_shapes=[pltpu.SemaphoreType.DMA((7 * n,)), pltpu.SemaphoreType.DMA((7 * n,)),
                        pltpu.SemaphoreType.DMA((n,))],
        interpret=_INTERPRET)(*xs)


def all_to_all(xs, name):
    n = len(xs)

    def body(*refs):
        _a2a_start(refs[:n], refs[n:2 * n], *refs[2 * n:])
        _a2a_wait(refs[:n], refs[n:2 * n], *refs[2 * n:])

    return pl.pallas_call(
        body, name=name, out_shape=[_sds(a.shape, a.dtype) for a in xs],
        in_specs=[_any()] * n, out_specs=[_any()] * n, scratch_shapes=_a2a_sems(n), interpret=_INTERPRET)(*xs)


def _a2a_sems(n):
    return [pltpu.SemaphoreType.DMA((7 * n,)), pltpu.SemaphoreType.DMA((7 * n,)), pltpu.SemaphoreType.DMA((n,))]


def _a2a_copies(x_refs, o_refs, send_sems, recv_sems, local_sems):
    n = len(x_refs)
    x, y, c = _place()
    me = (x, y, c)
    flip = lambda v, b: (1 - v) if b else v
    peers = [(flip(x, k >> 2 & 1), flip(y, k >> 1 & 1), flip(c, k & 1)) for k in range(1, NDEV)]
    mine = [pltpu.make_async_copy(x_refs[t].at[_slot(me)], o_refs[t].at[_slot(me)], local_sems.at[t])
            for t in range(n)]

    def copy(t, k, src_slot, dst_slot, to):
        return pltpu.make_async_remote_copy(
            src_ref=x_refs[t].at[src_slot], dst_ref=o_refs[t].at[dst_slot], send_sem=send_sems.at[7 * t + k],
            recv_sem=recv_sems.at[7 * t + k], device_id=to, device_id_type=MESH_T)

    sends = [copy(t, k, _slot(p), _slot(me), p) for t in range(n) for k, p in enumerate(peers)]
    recvs = [copy(t, k, _slot(p), _slot(p), me) for t in range(n) for k, p in enumerate(peers)]
    return mine, sends, recvs


def _ag_copies(x_refs, o_refs, send_sems, recv_sems, local_sems):
    n = len(x_refs)
    x, y, c = _place()
    me = (x, y, c)
    flip = lambda v, b: (1 - v) if b else v
    peers = [(flip(x, k >> 2 & 1), flip(y, k >> 1 & 1), flip(c, k & 1)) for k in range(1, NDEV)]
    mine = [pltpu.make_async_copy(x_refs[t], o_refs[t].at[_slot(me)], local_sems.at[t]) for t in range(n)]

    def copy(t, k, dst_slot, to):
        return pltpu.make_async_remote_copy(
            src_ref=x_refs[t], dst_ref=o_refs[t].at[dst_slot], send_sem=send_sems.at[7 * t + k],
            recv_sem=recv_sems.at[7 * t + k], device_id=to, device_id_type=MESH_T)

    sends = [copy(t, k, _slot(me), p) for t in range(n) for k, p in enumerate(peers)]
    recvs = [copy(t, k, _slot(p), me) for t in range(n) for k, p in enumerate(peers)]
    return mine, sends, recvs


def _ag_start(x_refs, o_refs, send_sems, recv_sems, local_sems):
    mine, sends, _ = _ag_copies(x_refs, o_refs, send_sems, recv_sems, local_sems)
    for cp in mine + sends:
        cp.start()


def _ag_wait(x_refs, o_refs, send_sems, recv_sems, local_sems):
    mine, sends, recvs = _ag_copies(x_refs, o_refs, send_sems, recv_sems, local_sems)
    for cp in recvs:
        cp.wait_recv()
    for cp in sends:
        cp.wait_send()
    for cp in mine:
        cp.wait()


def _a2a_start(x_refs, o_refs, send_sems, recv_sems, local_sems):
    mine, sends, _ = _a2a_copies(x_refs, o_refs, send_sems, recv_sems, local_sems)
    for cp in mine + sends:
        cp.start()


def _a2a_wait(x_refs, o_refs, send_sems, recv_sems, local_sems):
    mine, sends, recvs = _a2a_copies(x_refs, o_refs, send_sems, recv_sems, local_sems)
    for cp in recvs:
        cp.wait_recv()
    for cp in sends:
        cp.wait_send()
    for cp in mine:
        cp.wait()


def adam_reduce(P, w, m, v, name):
    n, R, C = P.shape
    br = R // 4 if R % 64 == 0 else R

    def body(p_ref, w_ref, m_ref, v_ref, g_o, d_o, m_o, v_o):
        g = p_ref[0].astype(F32)
        for k in range(1, n):
            g = g + p_ref[k].astype(F32)
        m1 = ADAM_B1 * m_ref[...] + (1.0 - ADAM_B1) * g
        v1 = ADAM_B2 * v_ref[...] + (1.0 - ADAM_B2) * jnp.square(g)
        m_hat = m1 / (1.0 - ADAM_B1 ** ADAM_STEP)
        v_hat = v1 / (1.0 - ADAM_B2 ** ADAM_STEP)
        g_o[...] = g
        d_o[...] = -ADAM_LR * (m_hat / (jnp.sqrt(v_hat) + ADAM_EPS) + ADAM_WD * w_ref[...])
        m_o[...] = m1
        v_o[...] = v1

    blk = pl.BlockSpec((br, C), lambda i: (i, 0))
    return _pc(body, name, [_sds((R, C))] * 4, grid=(R // br,),
               in_specs=[pl.BlockSpec((n, br, C), lambda i: (0, i, 0)), blk, blk, blk], out_specs=[blk] * 4)(P, w, m, v)


def adam_layers(P0, P1, w, m, v, name):
    n, R, C = P0.shape
    br = R // 4 if R % 64 == 0 else R
    nb = R // br

    def body(p0_ref, p1_ref, w_ref, m_ref, v_ref, g_o, d_o, m_o, v_o):
        def total(p_ref):
            g = p_ref[0].astype(F32)
            for k in range(1, n):
                g = g + p_ref[k].astype(F32)
            return g

        g = jnp.where(pl.program_id(0) == 0, total(p0_ref), total(p1_ref))
        m1 = ADAM_B1 * m_ref[0] + (1.0 - ADAM_B1) * g
        v1 = ADAM_B2 * v_ref[0] + (1.0 - ADAM_B2) * jnp.square(g)
        m_hat = m1 / (1.0 - ADAM_B1 ** ADAM_STEP)
        v_hat = v1 / (1.0 - ADAM_B2 ** ADAM_STEP)
        g_o[0] = g
        d_o[0] = -ADAM_LR * (m_hat / (jnp.sqrt(v_hat) + ADAM_EPS) + ADAM_WD * w_ref[0])
        m_o[0] = m1
        v_o[0] = v1

    blk = pl.BlockSpec((1, br, C), lambda l, i: (l, i, 0))
    p0 = pl.BlockSpec((n, br, C), lambda l, i: (0, jnp.where(l == 0, i, nb - 1), 0))
    p1 = pl.BlockSpec((n, br, C), lambda l, i: (0, jnp.where(l == 1, i, 0), 0))
    return _pc(body, name, [_sds((2, R, C))] * 4, grid=(2, nb), in_specs=[p0, p1, blk, blk, blk],
               out_specs=[blk] * 4)(P0, P1, w, m, v)


def mod_fwd(scin, wmod, bcol):
    def body(s_ref, w_ref, b_ref, o_ref):
        o_ref[0] = mm(_silu(s_ref[...]), w_ref[0]) + b_ref[0]

    return _pc(body, "mod_fwd", _sds((2, 16, 768)), grid=(2,),
               in_specs=[pl.BlockSpec((16, D), lambda l: (0, 0)), pl.BlockSpec((1, D, 768), lambda l: (l, 0, 0)),
                         pl.BlockSpec((1, 1, 768), lambda l: (l, 0, 0))],
               out_specs=pl.BlockSpec((1, 16, 768), lambda l: (l, 0, 0)))(scin, wmod, bcol)


def mod_bwd(scin, wmod, G):
    def body(s_ref, w_ref, g_ref, dw_o, ds_o):
        _, vjp = jax.vjp(lambda s, w: mm(_silu(s), w), s_ref[...], w_ref[0])
        ds, dw = vjp(g_ref[0])
        dw_o[0] = dw
        _acc_init(pl.program_id(0) == 0, [ds_o])
        ds_o[...] += ds

    full = pl.BlockSpec((16, D), lambda l: (0, 0))
    wsp = pl.BlockSpec((1, D, 768), lambda l: (l, 0, 0))
    return _pc(body, "mod_bwd", [_sds((2, D, 768)), _sds((16, D))], grid=(2,),
               in_specs=[full, wsp, pl.BlockSpec((1, 16, 768), lambda l: (l, 0, 0))], out_specs=[wsp, full])(
        scin, wmod, G)


_SMALL = ["b_mod", "g_mix", "wa_sink", "na_rpb", "ssm_conv_w", "ssm_conv_b", "ssm_dt_bias", "ssm_a_log", "ssm_d",
          "ssm_norm_g", "g_ffn", "g_final", "dmod_s", "dmod_c"]


def _pack(parts):
    rows = []
    for a in parts:
        f = a.reshape(-1).astype(F32)
        rows.append(jnp.pad(f, (0, (-f.shape[0]) % 1024)).reshape(-1, 128))
    return jnp.concatenate(rows, axis=0)


def _unpack(packed, shapes):
    out, r = [], 0
    for s in shapes:
        nel = int(np.prod(s))
        nr = -(-nel // 1024) * 8
        out.append(packed[r:r + nr].reshape(-1)[:nel].reshape(s))
        r += nr
    return out


def kernel(x, c, ctx, c_ctx, w_mod, b_mod, g_mix, w_in, wa_sink, na_rpb, ssm_conv_w, ssm_conv_b, ssm_dt_bias, ssm_a_log, ssm_d, ssm_norm_g, w_out, g_ffn, w_ffn_in, w_ffn_out, g_final, loss_target, m_c_ctx, m_w_mod, m_b_mod, m_g_mix, m_w_in, m_wa_sink, m_na_rpb, m_ssm_conv_w, m_ssm_conv_b, m_ssm_dt_bias, m_ssm_a_log, m_ssm_d, m_ssm_norm_g, m_w_out, m_g_ffn, m_w_ffn_in, m_w_ffn_out, m_g_final, v_c_ctx, v_w_mod, v_b_mod, v_g_mix, v_w_in, v_wa_sink, v_na_rpb, v_ssm_conv_w, v_ssm_conv_b, v_ssm_dt_bias, v_ssm_a_log, v_ssm_d, v_ssm_norm_g, v_w_out, v_g_ffn, v_w_ffn_in, v_w_ffn_out, v_g_final):
    L = x.shape[1]
    px, py, pc = _place()
    me = 4 * px + 2 * py + pc
    W = dict(c_ctx=c_ctx, w_mod=w_mod, b_mod=b_mod, g_mix=g_mix, w_in=w_in, wa_sink=wa_sink, na_rpb=na_rpb,
             ssm_conv_w=ssm_conv_w, ssm_conv_b=ssm_conv_b, ssm_dt_bias=ssm_dt_bias, ssm_a_log=ssm_a_log, ssm_d=ssm_d,
             ssm_norm_g=ssm_norm_g, w_out=w_out, g_ffn=g_ffn, w_ffn_in=w_ffn_in, w_ffn_out=w_ffn_out, g_final=g_final)
    M = dict(c_ctx=m_c_ctx, w_mod=m_w_mod, b_mod=m_b_mod, g_mix=m_g_mix, w_in=m_w_in, wa_sink=m_wa_sink,
             na_rpb=m_na_rpb, ssm_conv_w=m_ssm_conv_w, ssm_conv_b=m_ssm_conv_b, ssm_dt_bias=m_ssm_dt_bias,
             ssm_a_log=m_ssm_a_log, ssm_d=m_ssm_d, ssm_norm_g=m_ssm_norm_g, w_out=m_w_out, g_ffn=m_g_ffn,
             w_ffn_in=m_w_ffn_in, w_ffn_out=m_w_ffn_out, g_final=m_g_final)
    V = dict(c_ctx=v_c_ctx, w_mod=v_w_mod, b_mod=v_b_mod, g_mix=v_g_mix, w_in=v_w_in, wa_sink=v_wa_sink,
             na_rpb=v_na_rpb, ssm_conv_w=v_ssm_conv_w, ssm_conv_b=v_ssm_conv_b, ssm_dt_bias=v_ssm_dt_bias,
             ssm_a_log=v_ssm_a_log, ssm_d=v_ssm_d, ssm_norm_g=v_ssm_norm_g, w_out=v_w_out, g_ffn=v_g_ffn,
             w_ffn_in=v_w_ffn_in, w_ffn_out=v_w_ffn_out, g_final=v_g_final)

    c_all, conv_all = all_gather([c, ssm_conv_w], "gather_small")
    shards = dict(w_in=w_in.astype(MXU), w_out=w_out.astype(MXU), w_ffn_in=w_ffn_in.astype(MXU),
                  w_ffn_out=w_ffn_out.astype(MXU))
    conv_f = conv_all.transpose(1, 2, 0, 3).reshape(2, 7, 1024)

    scin = jnp.concatenate([c_all.reshape(NDEV, D), c_ctx.reshape(1, D), jnp.zeros((7, D), F32)], axis=0)
    bcol = lax.dynamic_slice_in_dim(b_mod, me * 768, 768, axis=1).reshape(2, 1, 768)
    mod_all, = all_gather([mod_fwd(scin, w_mod, bcol)], "gather_mod")
    mod_rows = mod_all.transpose(1, 2, 0, 3).reshape(2, 16, 6 * D)
    mods = jnp.stack([lax.dynamic_index_in_dim(mod_rows, me, axis=1, keepdims=False), mod_rows[:, 8]], axis=1)

    layers = [dict(g_mix=g_mix[i], wa_sink=wa_sink[i], na_rpb=na_rpb[i], ssm_conv_w=conv_f[i],
                   ssm_conv_b=ssm_conv_b[i], ssm_dt_bias=ssm_dt_bias[i], ssm_a_log=ssm_a_log[i], ssm_d=ssm_d[i],
                   ssm_norm_g=ssm_norm_g[i], g_ffn=g_ffn[i]) for i in range(2)]
    loss, dx, grads, dmods, dgfin = local_step(x[0], ctx[0], loss_target[0], mods, layers, shards, g_final, L)
    loss = lax.psum(loss, ("x", "y", "c"))

    stk = lambda n: jnp.stack([grads[0][n], grads[1][n]])
    small = dict(b_mod=dmods[:, 0] + dmods[:, 1], g_final=dgfin, dmod_s=dmods[:, 0], dmod_c=dmods[:, 1])
    for nme in _SMALL:
        if nme not in small:
            small[nme] = stk(nme)
    shapes = [small[nme].shape for nme in _SMALL]
    zero_like = lambda nme: jnp.zeros(small[nme].shape, F32)
    own = lambda S, nme: S[nme] if (nme in S and S[nme].shape == small[nme].shape) else zero_like(nme)
    gath, = all_gather([_pack([small[nme] for nme in _SMALL])], "gather_grads")
    sm = adam_reduce(gath, _pack([own(W, nme) for nme in _SMALL]), _pack([own(M, nme) for nme in _SMALL]),
                     _pack([own(V, nme) for nme in _SMALL]), "adam_small")
    res = {nme: vals for nme, vals in zip(_SMALL, zip(*[_unpack(a, shapes) for a in sm]))}

    cols = lambda a: lax.dynamic_slice_in_dim(a, me * 768, 768, axis=-1)
    gparts = [_unpack(gath[d], shapes) for d in range(NDEV)]
    dmod_s_all = jnp.stack([gparts[d][_SMALL.index("dmod_s")] for d in range(NDEV)], axis=1)
    G = jnp.concatenate([cols(dmod_s_all), cols(res["dmod_c"][0])[:, None, :], jnp.zeros((2, 7, 768), F32)], axis=1)
    dwmod, dscin = mod_bwd(scin, w_mod, G)
    cc_g, = all_gather([dscin[8].reshape(8, 128)], "gather_cctx")
    out = {}
    out["c_ctx"] = [a.reshape(D) for a in adam_reduce(cc_g, c_ctx.reshape(8, 128), m_c_ctx.reshape(8, 128),
                                                      v_c_ctx.reshape(8, 128), "adam_cctx")]
    out["w_mod"] = [a.reshape(2, D, 768) for a in adam_reduce(
        dwmod.reshape(1, 2 * D, 768), w_mod.reshape(2 * D, 768), m_w_mod.reshape(2 * D, 768),
        v_w_mod.reshape(2 * D, 768), "adam_wmod")]
    gconv = lax.dynamic_slice_in_dim(res["ssm_conv_w"][0], me * 128, 128, axis=2)
    out["ssm_conv_w"] = [a.reshape(2, 7, 128) for a in adam_reduce(
        gconv.reshape(1, 14, 128), ssm_conv_w.reshape(14, 128), m_ssm_conv_w.reshape(14, 128),
        v_ssm_conv_w.reshape(14, 128), "adam_conv")]
    for nme in _SMALL:
        if nme not in ("ssm_conv_w", "dmod_s", "dmod_c"):
            out[nme] = list(res[nme])

    for nme in ("w_in", "w_out", "w_ffn_in", "w_ffn_out"):
        out[nme] = list(adam_layers(grads[0][nme], grads[1][nme], W[nme], M[nme], V[nme], "adam_" + nme))
    order = ["c_ctx", "w_mod", "b_mod", "g_mix", "w_in", "wa_sink", "na_rpb", "ssm_conv_w", "ssm_conv_b",
             "ssm_dt_bias", "ssm_a_log", "ssm_d", "ssm_norm_g", "w_out", "g_ffn", "w_ffn_in", "w_ffn_out", "g_final"]
    return (loss, dx.reshape(1, L, D), *[out[nme][0] for nme in order], *[out[nme][1] for nme in order],
            *[out[nme][2] for nme in order], *[out[nme][3] for nme in order])
```

```python
import functools
import math

import numpy as np
import jax
import jax.numpy as jnp
from jax import lax
from jax.experimental import pallas as pl
from jax.experimental.pallas import tpu as pltpu

F32 = jnp.float32
MXU = jnp.bfloat16
_INTERPRET = False
VMEM_LIMIT = 60 * 1024 * 1024

D = 1024
LC = 256
GW = 64
HD = 64
EPS = 1e-6
NEG = -1e30
NDEV = 8
Q = 128
NSTATE = 128
DFF = 2816
IN_COLS = 2832
NP_IN = 3072
C_QA, C_QB, C_Z, C_KA, C_VA, C_KB, C_VB, C_XBC, C_DT = 0, 256, 512, 1024, 1152, 1280, 1536, 1792, 2816
ADAM_LR, ADAM_B1, ADAM_B2, ADAM_EPS, ADAM_WD, ADAM_STEP = 0.001, 0.9, 0.999, 1e-08, 0.01, 10
MESH_T = pl.DeviceIdType.MESH


def _dg(a, b, ca, cb):
    return lax.dot_general(a.astype(MXU), b.astype(MXU), (((ca,), (cb,)), ((), ())), preferred_element_type=F32)


@jax.custom_vjp
def mm(a, b):
    return _dg(a, b, 1, 0)


def _mm_f(a, b):
    return _dg(a, b, 1, 0), (a, b)


def _mm_b(res, g):
    a, b = res
    return _dg(g, b, 1, 1).astype(a.dtype), _dg(a, g, 0, 0).astype(b.dtype)


mm.defvjp(_mm_f, _mm_b)


@jax.custom_vjp
def mm_nt(a, b):
    return _dg(a, b, 1, 1)


def _mmnt_f(a, b):
    return _dg(a, b, 1, 1), (a, b)


def _mmnt_b(res, g):
    a, b = res
    return _dg(g, b, 1, 0).astype(a.dtype), _dg(g, a, 0, 0).astype(b.dtype)


mm_nt.defvjp(_mmnt_f, _mmnt_b)


@jax.custom_vjp
def mm_tn(a, b):
    return _dg(a, b, 0, 0)


def _mmtn_f(a, b):
    return _dg(a, b, 0, 0), (a, b)


def _mmtn_b(res, g):
    a, b = res
    return _dg(b, g, 1, 1).astype(a.dtype), _dg(a, g, 1, 0).astype(b.dtype)


mm_tn.defvjp(_mmtn_f, _mmtn_b)


@jax.custom_vjp
def mmw(a, w):
    return _dg(a, w, 1, 0)


mmw.defvjp(lambda a, w: (_dg(a, w, 1, 0), w), lambda w, g: (_dg(g, w, 1, 1), None))


@jax.custom_vjp
def mmw_nt(a, w):
    return _dg(a, w, 1, 1)


mmw_nt.defvjp(lambda a, w: (_dg(a, w, 1, 1), w), lambda w, g: (_dg(g, w, 1, 0), None))


def _exact(a, b):
    return lax.dot_general(a, b, (((1,), (0,)), ((), ())), precision=lax.Precision.HIGHEST,
                           preferred_element_type=F32)


def _pc(body, name, out_shape, grid=None, in_specs=None, out_specs=None, scratch=(), sends=(), gather=False):
    params = pltpu.CompilerParams(vmem_limit_bytes=VMEM_LIMIT)
    if sends and not isinstance(out_shape, (list, tuple)):
        out_shape, out_specs = [out_shape], [out_specs]
    start, wait = (_ag_start, _ag_wait) if gather else (_a2a_start, _a2a_wait)
    if not sends:
        kw = {}
        if grid is not None:
            kw = dict(grid=grid, in_specs=in_specs, out_specs=out_specs)
        elif in_specs is not None:
            kw = dict(in_specs=in_specs, out_specs=out_specs)
        return pl.pallas_call(body, name=name, out_shape=out_shape, scratch_shapes=list(scratch),
                              compiler_params=params, interpret=_INTERPRET, **kw)
    n, nin, nout, nscr = len(sends), len(in_specs), len(out_shape), len(scratch)

    def body2(*refs):
        cin, xs = refs[:nin], refs[nin:nin + n]
        couts, os_ = refs[nin + n:nin + n + nout], refs[nin + n + nout:nin + 2 * n + nout]
        cscr, sems = refs[nin + 2 * n + nout:nin + 2 * n + nout + nscr], refs[nin + 2 * n + nout + nscr:]
        ids = [pl.program_id(a) for a in range(len(grid))]
        first = functools.reduce(lambda a, b: a & b, [i == 0 for i in ids])
        last = functools.reduce(lambda a, b: a & b, [i == g - 1 for i, g in zip(ids, grid)])

        @pl.when(first)
        def _():
            start(xs, os_, *sems)

        body(*cin, *couts, *cscr)

        @pl.when(last)
        def _():
            wait(xs, os_, *sems)

    call = pl.pallas_call(
        body2, name=name,
        out_shape=list(out_shape) + [_sds(((NDEV,) if gather else ()) + a.shape, a.dtype) for a in sends],
        grid=grid, in_specs=list(in_specs) + [_any()] * n, out_specs=list(out_specs) + [_any()] * n,
        scratch_shapes=list(scratch) + _a2a_sems(n), compiler_params=params, interpret=_INTERPRET)

    def run(*args):
        res = call(*args, *sends)
        return res[:nout], res[nout:]

    return run


def _vm():
    return pl.BlockSpec(memory_space=pltpu.VMEM)


def _sds(shape, dt=F32):
    return jax.ShapeDtypeStruct(shape, dt)


def _iota(shape, dim):
    return lax.broadcasted_iota(jnp.int32, shape, dim)


def _silu(x):
    return x * jax.nn.sigmoid(x)


def _softplus(x):
    return jnp.maximum(x, 0.0) + jnp.log1p(jnp.exp(-jnp.abs(x)))


def _normmod(x, g, sh, sc):
    r = lax.rsqrt(jnp.mean(x * x, axis=-1, keepdims=True) + EPS)
    return (x * r * g) * (1.0 + sc) + sh


def _rope(x, cos, sin, rm):
    return x * cos + _exact(x, rm) * sin


def _swap12(x):
    lane = _iota(x.shape, 1)
    up, down = pltpu.roll(x, 192, 1), pltpu.roll(x, 64, 1)
    return jnp.where((lane >= 64) & (lane < 128), up, jnp.where((lane >= 128) & (lane < 192), down, x))


def _acc_init(first, refs):
    @pl.when(first)
    def _():
        for r in refs:
            r[...] = jnp.zeros_like(r)


def in_fwd(X, g, sh, sc, W, cos, sin, rm, L, sends=()):
    T = X.shape[0]
    TR = 256
    nlt = L // TR

    def body(x_ref, g_ref, sh_ref, sc_ref, w_ref, cos_ref, sin_ref, rm_ref,
             qa, qb, z, ka, va, kb, vb, xbc, dt, hout):
        h = _normmod(x_ref[...], g_ref[...], sh_ref[0], sc_ref[0]).astype(MXU)
        hout[...] = h
        y = lax.dot_general(h, w_ref[...], (((1,), (1,)), ((), ())), preferred_element_type=F32)
        cs, sn, r = cos_ref[...], sin_ref[...], rm_ref[...]
        qa[...] = _rope(_swap12(y[:, C_QA:C_QB]), cs, sn, r).astype(MXU)
        qb[...] = y[:, C_QB:C_Z].astype(MXU)
        z[...] = y[:, C_Z:C_KA]
        ka[...] = _rope(y[:, C_KA:C_VA], cs[:, :128], sn[:, :128], r[:128, :128]).astype(MXU)
        va[...] = y[:, C_VA:C_KB].astype(MXU)
        kb[...] = y[:, C_KB:C_VB].astype(MXU)
        vb[...] = y[:, C_VB:C_XBC].astype(MXU)
        xbc[...] = y[:, C_XBC:C_DT]
        dt[...] = y[:, C_DT:C_DT + 128]

    row = lambda w: pl.BlockSpec((TR, w), lambda i: (i, 0))
    cls = pl.BlockSpec((1, 1, D), lambda i: (i // nlt, 0, 0))
    widths = [(256, MXU), (256, MXU), (512, F32), (128, MXU), (128, MXU), (256, MXU), (256, MXU), (1024, F32),
              (128, F32), (D, MXU)]
    return _pc(body, "in_fwd", [_sds((T, w), d) for w, d in widths], grid=(T // TR,),
               in_specs=[row(D), pl.BlockSpec((1, D), lambda i: (0, 0)), cls, cls, _vm(), row(256), row(256), _vm()],
               out_specs=[row(w) for w, _ in widths], sends=sends, gather=True)(X, g, sh, sc, W, cos, sin, rm)


def in_bwd(X, g, sh, sc, W, cos, sin, rm, dxres, dqa, dqb, dz, dka, dva, dkb, dvb, dxbc, ddt2, L):
    T = X.shape[0]
    TR = 256
    nlt = L // TR

    def body(x_ref, g_ref, sh_ref, sc_ref, w_ref, cos_ref, sin_ref, rm_ref, dxres_ref, dqa_r, dqb_r, dz_r, dka_r,
             dva_r, dkb_r, dvb_r, dxbc_r, ddt0_r, ddt1_r, dx_o, dy_o, dg_o, dsh_o, dsc_o):
        i = pl.program_id(0)
        cs, sn, r = cos_ref[...], sin_ref[...], rm_ref[...]
        _, vq = jax.vjp(lambda t: _rope(t, cs, sn, r), dqa_r[...])
        _, vk = jax.vjp(lambda t: _rope(t, cs[:, :128], sn[:, :128], r[:128, :128]), dka_r[...])
        dyqa = _swap12(vq(dqa_r[...])[0])
        dyka, = vk(dka_r[...])
        ddt = ddt0_r[0] + ddt1_r[0]
        dy = jnp.concatenate([dyqa, dqb_r[...], dz_r[...], dyka, dva_r[...], dkb_r[...], dvb_r[...], dxbc_r[...],
                              ddt, jnp.zeros((TR, NP_IN - C_DT - 128), F32)], axis=1).astype(MXU)
        dy_o[...] = dy
        dh = jnp.dot(dy, w_ref[...], preferred_element_type=F32)
        _, vp = jax.vjp(_normmod, x_ref[...], g_ref[...], sh_ref[0], sc_ref[0])
        dx, dg, dsh, dsc = vp(dh)
        dx_o[...] = dx + dxres_ref[...]
        _acc_init(i == 0, [dg_o])
        _acc_init((i == 0) | (i == nlt), [dsh_o, dsc_o])
        dg_o[...] += dg
        dsh_o[0] += dsh
        dsc_o[0] += dsc

    row = lambda w: pl.BlockSpec((TR, w), lambda i: (i, 0))
    cls = pl.BlockSpec((1, 1, D), lambda i: (i // nlt, 0, 0))
    vec = pl.BlockSpec((1, D), lambda i: (0, 0))
    dts = lambda d: pl.BlockSpec((1, TR, 128), lambda i: (d, i, 0))
    return _pc(body, "in_bwd",
               [_sds((T, D)), _sds((T, NP_IN), MXU), _sds((1, D)), _sds((2, 1, D)), _sds((2, 1, D))],
               grid=(T // TR,),
               in_specs=[row(D), vec, cls, cls, _vm(), row(256), row(256), _vm(), row(D), row(256), row(256), row(512),
                         row(128), row(128), row(256), row(256), row(1024), dts(0), dts(1)],
               out_specs=[row(D), row(NP_IN), vec, cls, cls])(
        X, g, sh, sc, W, cos, sin, rm, dxres, dqa, dqb, dz, dka, dva, dkb, dvb, dxbc, ddt2, ddt2)


def tn_mm(A, G, bk, bn, out_dtype):
    T, K = A.shape
    N = G.shape[1]
    bt = T // 4
    nt = T // bt

    def body(a_ref, g_ref, o_ref, acc):
        t = pl.program_id(2)
        _acc_init(t == 0, [acc])
        acc[...] += lax.dot_general(a_ref[...], g_ref[...], (((0,), (0,)), ((), ())), preferred_element_type=F32)

        @pl.when(t == nt - 1)
        def _():
            o_ref[...] = acc[...].astype(out_dtype)

    return _pc(body, "tn_mm", _sds((K, N), out_dtype), grid=(K // bk, N // bn, nt),
               in_specs=[pl.BlockSpec((bt, bk), lambda k, n, t: (t, k)), pl.BlockSpec((bt, bn), lambda k, n, t: (t, n))],
               out_specs=pl.BlockSpec((bk, bn), lambda k, n, t: (k, n)),
               scratch=[pltpu.VMEM((bk, bn), F32)])(A, G)


def _ssm_out(yf, yb, xs, z, dsk, gs):
    y = (yf + yb + dsk * xs) * _silu(z)
    r = lax.rsqrt(jnp.mean(y * y, axis=-1, keepdims=True) + EPS)
    return y * r * gs


def out_fwd(oa, ob, y2, act, z, dsk, gs, W, X, gate, L):
    T = X.shape[0]
    TR = 256
    nlt = L // TR

    def body(oa_r, ob_r, yf_r, yb_r, xs_r, z_r, dsk_r, gs_r, w_ref, x_ref, gt_ref, x1_o, cat_o):
        oc = _ssm_out(yf_r[0], yb_r[0], xs_r[...], z_r[...], dsk_r[...], gs_r[...])
        cat = jnp.concatenate([_swap12(oa_r[...]), ob_r[...], oc], axis=1).astype(MXU)
        cat_o[...] = cat
        x1_o[...] = x_ref[...] + gt_ref[0] * jnp.dot(cat, w_ref[...], preferred_element_type=F32)

    row = lambda w: pl.BlockSpec((TR, w), lambda i: (i, 0))
    ys = lambda d: pl.BlockSpec((1, TR, 512), lambda i: (d, i, 0))
    cls = pl.BlockSpec((1, 1, D), lambda i: (i // nlt, 0, 0))
    v512 = pl.BlockSpec((1, 512), lambda i: (0, 0))
    return _pc(body, "out_fwd", [_sds((T, D)), _sds((T, D), MXU)], grid=(T // TR,),
               in_specs=[row(256), row(256), ys(0), ys(1), row(512), row(512), v512, v512, _vm(), row(D), cls],
               out_specs=[row(D), row(D)])(oa, ob, y2, y2, act, z, dsk, gs, W, X, gate)


def out_bwd(oa, ob, y2, act, z, dsk, gs, W, gate, dX1, L):
    T = dX1.shape[0]
    TR = 256
    nlt = L // TR

    def body(oa_r, ob_r, yf_r, yb_r, xs_r, z_r, dsk_r, gs_r, w_ref, gt_ref, dx1_r,
             doa_o, dob_o, dy_o, dxs_o, dz_o, dmix_o, ddsk_o, dgs_o, dgt_o):
        i = pl.program_id(0)
        w = w_ref[...]

        def f(oa_, ob_, yf, yb, xs, z_, dsk_, gs_, gt):
            oc = _ssm_out(yf, yb, xs, z_, dsk_, gs_)
            return gt * mmw(jnp.concatenate([oa_, ob_, oc], axis=1), w)

        _, vjp = jax.vjp(f, _swap12(oa_r[...]), ob_r[...], yf_r[0], yb_r[0], xs_r[...], z_r[...], dsk_r[...],
                         gs_r[...], gt_ref[0])
        dx1 = dx1_r[...]
        doa, dob, dyf, _, dxs, dz, ddsk, dgs, dgt = vjp(dx1)
        doa_o[...] = _swap12(doa)
        dob_o[...] = dob
        dy_o[...] = dyf
        dxs_o[...] = dxs
        dz_o[...] = dz
        dmix_o[...] = (gt_ref[0] * dx1).astype(MXU)
        _acc_init(i == 0, [ddsk_o, dgs_o])
        _acc_init((i == 0) | (i == nlt), [dgt_o])
        ddsk_o[...] += ddsk
        dgs_o[...] += dgs
        dgt_o[0] += dgt

    row = lambda w: pl.BlockSpec((TR, w), lambda i: (i, 0))
    ys = lambda d: pl.BlockSpec((1, TR, 512), lambda i: (d, i, 0))
    cls = pl.BlockSpec((1, 1, D), lambda i: (i // nlt, 0, 0))
    v512 = pl.BlockSpec((1, 512), lambda i: (0, 0))
    return _pc(body, "out_bwd",
               [_sds((T, 256)), _sds((T, 256)), _sds((T, 512)), _sds((T, 512)), _sds((T, 512)), _sds((T, D), MXU),
                _sds((1, 512)), _sds((1, 512)), _sds((2, 1, D))],
               grid=(T // TR,),
               in_specs=[row(256), row(256), ys(0), ys(1), row(512), row(512), v512, v512, _vm(), cls, row(D)],
               out_specs=[row(256), row(256), row(512), row(512), row(512), row(D), v512, v512, cls])(
        oa, ob, y2, y2, act, z, dsk, gs, W, gate, dX1)


def _ffn_core(x, g, sh, sc, gt, wg, wu, wo, eg, eu):
    h = _normmod(x, g, sh, sc)
    a = mmw_nt(h, wg) + eg
    u = mmw_nt(h, wu) + eu
    act = _silu(a) * u
    return x + gt * mmw(act, wo), (h, act)


def ffn_fwd(X, g, sh, sc, gate, Win, Wout, L, sends=()):
    T = X.shape[0]
    TR = 256
    nlt = L // TR

    def body(x_ref, g_ref, sh_ref, sc_ref, gt_ref, wi_ref, wo_ref, o_ref):
        h = _normmod(x_ref[...], g_ref[...], sh_ref[0], sc_ref[0]).astype(MXU)
        nt = (((1,), (1,)), ((), ()))
        a = lax.dot_general(h, wi_ref[0:DFF, :], nt, preferred_element_type=F32)
        u = lax.dot_general(h, wi_ref[DFF:2 * DFF, :], nt, preferred_element_type=F32)
        act = (_silu(a) * u).astype(MXU)
        o_ref[...] = x_ref[...] + gt_ref[0] * jnp.dot(act, wo_ref[...], preferred_element_type=F32)

    row = lambda w: pl.BlockSpec((TR, w), lambda i: (i, 0))
    cls = pl.BlockSpec((1, 1, D), lambda i: (i // nlt, 0, 0))
    vec = pl.BlockSpec((1, D), lambda i: (0, 0))
    return _pc(body, "ffn_fwd", _sds((T, D)), grid=(T // TR,),
               in_specs=[row(D), vec, cls, cls, cls, _vm(), _vm()], out_specs=row(D), sends=sends, gather=True)(
        X, g, sh, sc, gate, Win, Wout)


def ffn_bwd(X, g, sh, sc, gate, Win, Wout, dX2, L, sends=()):
    T = X.shape[0]
    TR = 256
    nlt = L // TR

    def body(x_ref, g_ref, sh_ref, sc_ref, gt_ref, wi_ref, wo_ref, dx2_r,
             dx_o, h_o, du_o, act_o, dout_o, dg_o, dsh_o, dsc_o, dgt_o):
        i = pl.program_id(0)
        wg, wu, wo = wi_ref[0:DFF, :], wi_ref[DFF:2 * DFF, :], wo_ref[...]
        zero = jnp.zeros((TR, DFF), F32)
        f = lambda x, g_, sh_, sc_, gt, eg, eu: _ffn_core(x, g_, sh_, sc_, gt, wg, wu, wo, eg, eu)
        _, vjp, (h, act) = jax.vjp(f, x_ref[...], g_ref[...], sh_ref[0], sc_ref[0], gt_ref[0], zero, zero,
                                   has_aux=True)
        dx2 = dx2_r[...]
        dx, dg, dsh, dsc, dgt, da, du = vjp(dx2)
        dx_o[...] = dx
        h_o[...] = h.astype(MXU)
        du_o[...] = jnp.concatenate([da, du], axis=1).astype(MXU)
        act_o[...] = act.astype(MXU)
        dout_o[...] = (gt_ref[0] * dx2).astype(MXU)
        _acc_init(i == 0, [dg_o])
        _acc_init((i == 0) | (i == nlt), [dsh_o, dsc_o, dgt_o])
        dg_o[...] += dg
        dsh_o[0] += dsh
        dsc_o[0] += dsc
        dgt_o[0] += dgt

    row = lambda w: pl.BlockSpec((TR, w), lambda i: (i, 0))
    cls = pl.BlockSpec((1, 1, D), lambda i: (i // nlt, 0, 0))
    vec = pl.BlockSpec((1, D), lambda i: (0, 0))
    one = lambda w: pl.BlockSpec((TR, w), lambda i: (i, 0), pipeline_mode=pl.Buffered(1))
    return _pc(body, "ffn_bwd",
               [_sds((T, D)), _sds((T, D), MXU), _sds((T, 2 * DFF), MXU), _sds((T, DFF), MXU), _sds((T, D), MXU),
                _sds((1, D)), _sds((2, 1, D)), _sds((2, 1, D)), _sds((2, 1, D))],
               grid=(T // TR,),
               in_specs=[row(D), vec, cls, cls, cls, _vm(), _vm(), row(D)],
               out_specs=[row(D), row(D), one(2 * DFF), one(DFF), row(D), vec, cls, cls, cls], sends=sends)(
        X, g, sh, sc, gate, Win, Wout, dX2)


def loss_head(X2, g, tgt, L):
    TR = 256

    def body(x_ref, g_ref, t_ref, loss_o, dx_o, dg_o):
        i = pl.program_id(0)

        def f(x, g_):
            y = x * lax.rsqrt(jnp.mean(x * x, axis=-1, keepdims=True) + EPS) * g_
            return 0.5 * jnp.sum(jnp.mean(jnp.square(y - t_ref[...]), axis=-1, keepdims=True), axis=0, keepdims=True)

        val, vjp = jax.vjp(f, x_ref[...], g_ref[...])
        dx, dg = vjp(jnp.ones((1, 1), F32))
        dx_o[...] = dx
        _acc_init(i == 0, [loss_o, dg_o])
        loss_o[...] += jnp.broadcast_to(val, (8, 128))
        dg_o[...] += dg

    row = pl.BlockSpec((TR, D), lambda i: (i, 0))
    vec = pl.BlockSpec((1, D), lambda i: (0, 0))
    return _pc(body, "loss_head", [_sds((8, 128)), _sds((L, D)), _sds((1, D))], grid=(L // TR,),
               in_specs=[row, vec, row], out_specs=[pl.BlockSpec((8, 128), lambda i: (0, 0)), row, vec])(X2, g, tgt)


def _stack_impl(q):
    lane = _iota(q.shape, 1)
    return jnp.concatenate([jnp.where(lane < HD, q, 0.0), jnp.where(lane >= HD, q, 0.0)], axis=0)


def _unstack_impl(o):
    M = o.shape[0] // 2
    return jnp.where(_iota((M, o.shape[1]), 1) < HD, o[:M], o[M:])


@jax.custom_vjp
def _stack(q):
    return _stack_impl(q)


_stack.defvjp(lambda q: (_stack_impl(q), None), lambda _, g: (_unstack_impl(g),))


@jax.custom_vjp
def _unstack(o):
    return _unstack_impl(o)


_unstack.defvjp(lambda o: (_unstack_impl(o), None), lambda _, g: (_stack_impl(g),))


def _softmax_av(q, ks, vs, biases, sink):
    q2 = _stack(q)
    ss = []
    for k, b in zip(ks, biases):
        s = mm_nt(q2, k) * (HD ** -0.5)
        ss.append(s if b is None else s + b)
    m = functools.reduce(jnp.maximum, [jnp.max(s, axis=1, keepdims=True) for s in ss])
    if sink is not None:
        m = jnp.maximum(m, sink)
    m = lax.stop_gradient(m)
    es = [jnp.exp(s - m) for s in ss]
    den = functools.reduce(lambda a, b_: a + b_, [jnp.sum(e, axis=1, keepdims=True) for e in es])
    if sink is not None:
        den = den + jnp.exp(sink - m)
    inv = 1.0 / den
    return _unstack(functools.reduce(lambda a, b_: a + b_, [mm(e * inv, v) for e, v in zip(es, vs)]))


def _sink_col(s0, s1, M):
    return jnp.concatenate([jnp.broadcast_to(jnp.mean(s0, axis=1, keepdims=True), (M, 1)),
                            jnp.broadcast_to(jnp.mean(s1, axis=1, keepdims=True), (M, 1))], axis=0)


def _stack4_impl(q):
    lane = _iota((q.shape[0], 128), 1)
    parts = []
    for p in range(2):
        qp = q[:, 128 * p:128 * (p + 1)]
        parts += [jnp.where(lane < HD, qp, 0.0), jnp.where(lane >= HD, qp, 0.0)]
    return jnp.concatenate(parts, axis=0)


def _unstack4_impl(o):
    M = o.shape[0] // 4
    lane = _iota((M, 128), 1)
    return jnp.concatenate([jnp.where(lane < HD, o[0:M], o[M:2 * M]),
                            jnp.where(lane < HD, o[2 * M:3 * M], o[3 * M:4 * M])], axis=1)


@jax.custom_vjp
def _stack4(q):
    return _stack4_impl(q)


_stack4.defvjp(lambda q: (_stack4_impl(q), None), lambda _, g: (_unstack4_impl(g),))


@jax.custom_vjp
def _unstack4(o):
    return _unstack4_impl(o)


_unstack4.defvjp(lambda o: (_unstack4_impl(o), None), lambda _, g: (_stack4_impl(g),))


def _wa_block(q, kp, kc, kn, vp, vc, vn, kx, vx, sks, n, L):
    kb = jnp.concatenate([kp, kc, kn], axis=0)
    vb = jnp.concatenate([vp, vc, vn], axis=0)
    qpos = n * Q + (_iota((4 * Q, 3 * Q), 0) & (Q - 1))
    kpos = (n - 1) * Q + _iota((4 * Q, 3 * Q), 1)
    valid = (jnp.abs(qpos - kpos) <= Q) & (kpos >= 0) & (kpos < L)
    bias = jnp.where(valid, 0.0, NEG)
    sink = jnp.concatenate([jnp.broadcast_to(jnp.mean(s_, axis=1, keepdims=True), (Q, 1)) for s_ in sks], axis=0)
    q4 = _stack4(q)
    sc = HD ** -0.5
    sl = mm_nt(q4, kb) * sc + bias
    sx = mm_nt(q4, kx) * sc
    m = lax.stop_gradient(jnp.maximum(jnp.maximum(jnp.max(sl, axis=1, keepdims=True),
                                                  jnp.max(sx, axis=1, keepdims=True)), sink))
    el, ex = jnp.exp(sl - m), jnp.exp(sx - m)
    inv = 1.0 / (jnp.sum(el, axis=1, keepdims=True) + jnp.sum(ex, axis=1, keepdims=True) + jnp.exp(sink - m))
    return _unstack4(mm(el * inv, vb) + mm(ex * inv, vx))


def _wa_specs(L):
    nb = L // Q
    qs = pl.BlockSpec((Q, 256), lambda n: (n, 0))
    kprev = pl.BlockSpec((Q, 128), lambda n: (jnp.maximum(n - 1, 0), 0))
    kcur = pl.BlockSpec((Q, 128), lambda n: (n, 0))
    knext = pl.BlockSpec((Q, 128), lambda n: (jnp.minimum(n + 1, nb - 1), 0))
    kctx = pl.BlockSpec((LC, 128), lambda n: (L // LC, 0))
    sks = pl.BlockSpec((2, 2, 1, 128), lambda n: (0, 0, 0, 0))
    return nb, qs, [kprev, kcur, knext], kctx, sks


def wa_fwd(QA, KA, VA, sinkp, L, sends=()):
    nb, qs, kband, kctx, sks = _wa_specs(L)

    def body(q_r, kp, kc, kn, vp, vc, vn, kx, vx, sk_r, o_ref):
        n = pl.program_id(0)
        f = lambda t: t[...].astype(F32)
        o_ref[...] = _wa_block(f(q_r), f(kp), f(kc), f(kn), f(vp), f(vc), f(vn), f(kx), f(vx),
                               [sk_r[0, 0], sk_r[0, 1], sk_r[1, 0], sk_r[1, 1]], n, L)

    return _pc(body, "wa_fwd", _sds((L, 256)), grid=(nb,),
               in_specs=[qs] + kband + kband + [kctx, kctx, sks], out_specs=qs, sends=sends, gather=True)(
        QA, KA, KA, KA, VA, VA, VA, KA, VA, sinkp)


def wa_bwd(QA, KA, VA, sinkp, dO, L, sends=()):
    nb, qs, kband, kctx, sks = _wa_specs(L)

    def body(q_r, kp, kc, kn, vp, vc, vn, kx, vx, sk_r, do_r, dq_o, dk_o, dv_o, dkx_o, dvx_o, dsk_o):
        n = pl.program_id(0)
        f = lambda t: t[...].astype(F32)
        fn = lambda q, a, b, c, d, e, g, kx_, vx_, s_: _wa_block(q, a, b, c, d, e, g, kx_, vx_, s_, n, L)
        _, vjp = jax.vjp(fn, f(q_r), f(kp), f(kc), f(kn), f(vp), f(vc), f(vn), f(kx), f(vx),
                         [sk_r[0, 0], sk_r[0, 1], sk_r[1, 0], sk_r[1, 1]])
        dq, dkp, dkc, dkn, dvp, dvc, dvn, dkx, dvx, ds = vjp(do_r[...])
        dq_o[...] = dq
        _acc_init(n == 0, [dk_o, dv_o, dkx_o, dvx_o, dsk_o])
        rows = pl.ds(pl.multiple_of(n * Q, Q), 3 * Q)
        dk_o[rows, :] += jnp.concatenate([dkp, dkc, dkn], axis=0)
        dv_o[rows, :] += jnp.concatenate([dvp, dvc, dvn], axis=0)
        dkx_o[...] += dkx
        dvx_o[...] += dvx
        for i_ in range(4):
            dsk_o[i_ // 2, i_ % 2] += ds[i_]

    full = lambda r: pl.BlockSpec((r, 128), lambda n: (0, 0))
    return _pc(body, "wa_bwd",
               [_sds((L, 256)), _sds((L + 2 * Q, 128)), _sds((L + 2 * Q, 128)), _sds((LC, 128)), _sds((LC, 128)),
                _sds((2, 2, 1, 128))],
               grid=(nb,), in_specs=[qs] + kband + kband + [kctx, kctx, sks, qs],
               out_specs=[qs, full(L + 2 * Q), full(L + 2 * Q), full(LC), full(LC), sks], sends=sends)(
        QA, KA, KA, KA, VA, VA, VA, KA, VA, sinkp, dO)


def _ctx_block(q, kx, vx, s0, s1):
    return _softmax_av(q, [kx], [vx], [None], _sink_col(s0, s1, LC))


def ctx_fwd(Qx, Kx, Vx, sinkp, shared, L):
    cq = pl.BlockSpec((LC, 128), lambda p: (L // LC, p))
    ck = pl.BlockSpec((LC, 128), lambda p: (L // LC, 0 if shared else p))
    sks = pl.BlockSpec((1, 2, 1, 128), lambda p: (p, 0, 0, 0))

    def body(q_r, k_r, v_r, sk_r, o_ref):
        f = lambda t: t[...].astype(F32)
        o_ref[...] = _ctx_block(f(q_r), f(k_r), f(v_r), sk_r[0, 0], sk_r[0, 1])

    return _pc(body, "ctx_fwd", _sds((LC, 256)), grid=(2,), in_specs=[cq, ck, ck, sks],
               out_specs=pl.BlockSpec((LC, 128), lambda p: (0, p)))(Qx, Kx, Vx, sinkp)


def ctx_bwd(Qx, Kx, Vx, sinkp, dO, shared, L):
    cq = pl.BlockSpec((LC, 128), lambda p: (L // LC, p))
    ck = pl.BlockSpec((LC, 128), lambda p: (L // LC, 0 if shared else p))
    sks = pl.BlockSpec((1, 2, 1, 128), lambda p: (p, 0, 0, 0))
    op = pl.BlockSpec((LC, 128), lambda p: (0, p))
    ok = pl.BlockSpec((LC, 128), lambda p: (0, 0 if shared else p))
    dos = pl.BlockSpec((LC, 128), lambda p: (L // LC, p))

    def body(q_r, k_r, v_r, sk_r, do_r, dq_o, dk_o, dv_o, dsk_o):
        p = pl.program_id(0)
        f = lambda t: t[...].astype(F32)
        _, vjp = jax.vjp(_ctx_block, f(q_r), f(k_r), f(v_r), sk_r[0, 0], sk_r[0, 1])
        dq, dk, dv, ds0, ds1 = vjp(do_r[...])
        dq_o[...] = dq
        _acc_init((p == 0) if shared else (p >= 0), [dk_o, dv_o])
        dk_o[...] += dk
        dv_o[...] += dv
        dsk_o[0, 0] = ds0
        dsk_o[0, 1] = ds1

    kw = 128 if shared else 256
    return _pc(body, "ctx_bwd", [_sds((LC, 256)), _sds((LC, kw)), _sds((LC, kw)), _sds((2, 2, 1, 128))],
               grid=(2,), in_specs=[cq, ck, ck, sks, dos], out_specs=[op, ok, ok, sks])(Qx, Kx, Vx, sinkp, dO)


def _na_rows(qs, kws, vws, kx, vx, bs):
    sc = HD ** -0.5
    q2 = [_stack(q) for q in qs]
    sl = [mm_nt(a, k) * sc + b for a, k, b in zip(q2, kws, bs)]
    sx = [mm_nt(a, kx) * sc for a in q2]
    m = [lax.stop_gradient(jnp.maximum(jnp.max(a, axis=1, keepdims=True), jnp.max(b, axis=1, keepdims=True)))
         for a, b in zip(sl, sx)]
    el = [jnp.exp(a - c) for a, c in zip(sl, m)]
    ex = [jnp.exp(a - c) for a, c in zip(sx, m)]
    inv = [1.0 / (jnp.sum(a, axis=1, keepdims=True) + jnp.sum(b, axis=1, keepdims=True)) for a, b in zip(el, ex)]
    o2 = [mm(a * i, v) + mm(b * i, vx) for a, b, i, v in zip(el, ex, inv, vws)]
    return [_unstack(o) for o in o2]


def _na_geom(rb, j, R):
    r = rb * 8 + j
    s = jnp.clip(r - 4, 0, R - 8)
    cls = jnp.where(r < 4, r, jnp.where(r > R - 4, r - (R - 8), 4))
    return pl.ds(pl.multiple_of(s * GW, GW), 8 * GW), cls


def _na_load(q_r, k_r, v_r, b_r, rb, R):
    geo = [_na_geom(rb, j, R) for j in range(8)]
    qs = [q_r[j * GW:(j + 1) * GW, :].astype(F32) for j in range(8)]
    kws = [k_r[win, :].astype(F32) for win, _ in geo]
    vws = [v_r[win, :].astype(F32) for win, _ in geo]
    bs = [jnp.concatenate([b_r[0, cls], b_r[1, cls]], axis=0) for _, cls in geo]
    return geo, qs, kws, vws, bs


def na_fwd(QB, KB, VB, biasd, L):
    R = L // GW
    qs = pl.BlockSpec((8 * GW, 128), lambda p, rb: (rb, p))
    kfull = pl.BlockSpec((L, 128), lambda p, rb: (0, p))
    kctx = pl.BlockSpec((LC, 128), lambda p, rb: (L // LC, p))
    bs = pl.BlockSpec((2, 8, GW, 8 * GW), lambda p, rb: (p, 0, 0, 0))

    def body(q_r, k_r, v_r, kx_r, vx_r, b_r, o_ref):
        _, qs_, kws, vws, bs_ = _na_load(q_r, k_r, v_r, b_r, pl.program_id(1), R)
        outs = _na_rows(qs_, kws, vws, kx_r[...].astype(F32), vx_r[...].astype(F32), bs_)
        o_ref[...] = jnp.concatenate(outs, axis=0)

    return _pc(body, "na_fwd", _sds((L, 256)), grid=(2, R // 8), in_specs=[qs, kfull, kfull, kctx, kctx, bs],
               out_specs=qs)(QB, KB, VB, KB, VB, biasd)


def na_bwd(QB, KB, VB, biasd, dO, L):
    R = L // GW
    qs = pl.BlockSpec((8 * GW, 128), lambda p, rb: (rb, p))
    kfull = pl.BlockSpec((L, 128), lambda p, rb: (0, p))
    kctx = pl.BlockSpec((LC, 128), lambda p, rb: (L // LC, p))
    bs = pl.BlockSpec((2, 8, GW, 8 * GW), lambda p, rb: (p, 0, 0, 0))
    oc = pl.BlockSpec((LC, 128), lambda p, rb: (0, p))

    def body(q_r, k_r, v_r, kx_r, vx_r, b_r, do_r, dq_o, dk_o, dv_o, dkx_o, dvx_o, db_o):
        rb = pl.program_id(1)
        _acc_init(rb == 0, [dk_o, dv_o, dkx_o, dvx_o, db_o])
        geo, qs_, kws, vws, bs_ = _na_load(q_r, k_r, v_r, b_r, rb, R)
        _, vjp = jax.vjp(_na_rows, qs_, kws, vws, kx_r[...].astype(F32), vx_r[...].astype(F32), bs_)
        dqs, dkws, dvws, dkx, dvx, dbs = vjp([do_r[j * GW:(j + 1) * GW, :] for j in range(8)])
        dq_o[...] = jnp.concatenate(dqs, axis=0)
        dkx_o[...] += dkx
        dvx_o[...] += dvx
        for j, (win, cls) in enumerate(geo):
            dk_o[win, :] += dkws[j]
            dv_o[win, :] += dvws[j]
            db_o[0, cls] += dbs[j][:GW]
            db_o[1, cls] += dbs[j][GW:]

    return _pc(body, "na_bwd",
               [_sds((L, 256)), _sds((L, 256)), _sds((L, 256)), _sds((LC, 256)), _sds((LC, 256)),
                _sds((4, 8, GW, 8 * GW))],
               grid=(2, R // 8), in_specs=[qs, kfull, kfull, kctx, kctx, bs, qs],
               out_specs=[qs, kfull, kfull, oc, oc, bs])(QB, KB, VB, KB, VB, biasd, dO)


def exact_mm_call(A, B):
    def body(a_ref, b_ref, o_ref):
        o_ref[...] = _exact(a_ref[...], b_ref[...])

    return _pc(body, "exact_mm", _sds((A.shape[0], B.shape[1])))(A, B)


def _conv_shift(x, d, L):
    T = x.shape[0]
    if d == 0:
        return x
    t = _iota(x.shape, 0)
    src = t + d
    ok = (src >= 0) & (src < T) & ((src >= L) == (t >= L))
    return jnp.where(ok, pltpu.roll(x, (-d) % T, 0), 0.0)


def conv_fwd(XBC, w8, b, L):
    T = XBC.shape[0]

    def body(x_ref, w_ref, b_ref, o_ref):
        x = x_ref[...]
        pre = b_ref[...] + functools.reduce(
            lambda a, c: a + c, [_conv_shift(x, k - 3, L) * w_ref[k:k + 1, :] for k in range(7)])
        o_ref[...] = _silu(pre)

    col = pl.BlockSpec((T, 128), lambda j: (0, j))
    return _pc(body, "conv_fwd", _sds((T, 1024)), grid=(8,),
               in_specs=[col, pl.BlockSpec((8, 128), lambda j: (0, j)), pl.BlockSpec((1, 128), lambda j: (0, j))],
               out_specs=col)(XBC, w8, b)


def conv_bwd(XBC, w8, b, dS, dxs_skip, L):
    T = XBC.shape[0]

    def body(x_ref, w_ref, b_ref, d0_r, d1_r, dsk_r, dx_o, dw_o, db_o):
        j = pl.program_id(0)
        x = x_ref[...]
        xs = [_conv_shift(x, k - 3, L) for k in range(7)]
        pre = b_ref[...] + functools.reduce(lambda a, c: a + c, [xs[k] * w_ref[k:k + 1, :] for k in range(7)])
        _, vjp = jax.vjp(_silu, pre)
        dact = d0_r[0] + d1_r[0] + jnp.where(j < 4, dsk_r[...], 0.0)
        dpre, = vjp(dact)
        dx_o[...] = functools.reduce(
            lambda a, c: a + c, [_conv_shift(dpre, 3 - k, L) * w_ref[k:k + 1, :] for k in range(7)])
        dw_o[...] = jnp.concatenate([jnp.sum(dpre * xs[k], axis=0, keepdims=True) for k in range(7)]
                                    + [jnp.zeros((1, 128), F32)], axis=0)
        db_o[...] = jnp.sum(dpre, axis=0, keepdims=True)

    col = pl.BlockSpec((T, 128), lambda j: (0, j))
    w_s = pl.BlockSpec((8, 128), lambda j: (0, j))
    b_s = pl.BlockSpec((1, 128), lambda j: (0, j))
    ds = lambda d: pl.BlockSpec((1, T, 128), lambda j: (d, 0, j))
    return _pc(body, "conv_bwd", [_sds((T, 1024)), _sds((8, 1024)), _sds((1, 1024))], grid=(8,),
               in_specs=[col, w_s, b_s, ds(0), ds(1), pl.BlockSpec((T, 128), lambda j: (0, jnp.minimum(j, 3)))],
               out_specs=[col, w_s, b_s])(XBC, w8, b, dS, dS, dxs_skip)


def _ssd_chunk(xs, bs, cs, dtraw, dtb, alog, hs, tri, d):
    dt = _softplus(dtraw + dtb)
    a = dt * (-jnp.exp(alog))
    acum = _exact(tri, a)
    tot = jnp.sum(a, axis=0, keepdims=True)
    wcol = jnp.exp(tot - acum) * dt
    ea = jnp.exp(acum)
    cd = jnp.exp(tot)
    acum_t, dt_t = acum.T, dt.T
    lane = _iota((Q, 128), 1)
    srow = _iota((128, Q), 0)
    lane1 = _iota((1, 128), 1)
    prow = _iota((128, NSTATE), 0)
    mask = tri > 0.5
    cbs = [mm_nt(cs[g], bs[g]) for g in range(2)]
    ys, hn = [], []
    for j in range(4):
        g = j // 2
        x = xs[j]
        yi, st, eac, cdl = [], [], [], []
        for u in range(2):
            slot = d * 8 + 2 * j + u
            col = lambda m: jnp.sum(jnp.where(lane == slot, m, 0.0), axis=1, keepdims=True)
            rowv = lambda m: jnp.sum(jnp.where(srow == slot, m, 0.0), axis=0, keepdims=True)
            seg = col(acum) - rowv(acum_t)
            dcy = jnp.where(mask, jnp.exp(jnp.where(mask, seg, 0.0)), 0.0)
            yi.append(mm(cbs[g] * dcy * rowv(dt_t), x))
            st.append(mm_tn(x, bs[g] * col(wcol)))
            eac.append(col(ea))
            cdl.append(jnp.sum(jnp.where(lane1 == slot, cd, 0.0), axis=1, keepdims=True))
        yin = mm_nt(cs[g], hs[j])
        ys.append(jnp.where(lane < HD, yi[0] + yin * eac[0], yi[1] + yin * eac[1]))
        hn.append(hs[j] * jnp.where(prow < HD, cdl[0], cdl[1]) + jnp.where(prow < HD, st[0], st[1]))
    return ys, hn


def _ssd_chunk_idx(d, s, nlc, nch):
    return jnp.where(d == 0, (s + nlc) % nch, nch - 1 - s)


def ssd_fwd(ACT, DT, dtb, alog, tri2, L, sends=()):
    T = ACT.shape[0]
    nlc, nch = L // Q, T // Q

    def body(a_ref, dt_ref, dtb_ref, al_ref, tri_ref, y_o, hs_o, hst):
        d, s = pl.program_id(0), pl.program_id(1)
        _acc_init(s == 0, [hst])
        a = a_ref[...]
        xs = [a[:, 128 * j:128 * (j + 1)] for j in range(4)]
        bs = [a[:, 512 + 128 * g:640 + 128 * g] for g in range(2)]
        cs = [a[:, 768 + 128 * g:896 + 128 * g] for g in range(2)]
        hs = [hst[j] for j in range(4)]
        hs_o[0, 0] = hst[...]
        ys, hn = _ssd_chunk(xs, bs, cs, dt_ref[...], dtb_ref[...], al_ref[...], hs, tri_ref[0], d)
        y_o[0] = jnp.concatenate(ys, axis=1)
        for j in range(4):
            hst[j] = hn[j]

    ck = lambda w: pl.BlockSpec((Q, w), lambda d, s: (_ssd_chunk_idx(d, s, nlc, nch), 0))
    v128 = pl.BlockSpec((1, 128), lambda d, s: (0, 0))
    return _pc(body, "ssd_fwd", [_sds((2, T, 512)), _sds((2, nch, 4, 128, NSTATE))], grid=(2, nch),
               in_specs=[ck(1024), ck(128), v128, v128, pl.BlockSpec((1, Q, Q), lambda d, s: (d, 0, 0))],
               out_specs=[pl.BlockSpec((1, Q, 512), lambda d, s: (d, _ssd_chunk_idx(d, s, nlc, nch), 0)),
                          pl.BlockSpec((1, 1, 4, 128, NSTATE), lambda d, s: (d, s, 0, 0, 0))],
               scratch=[pltpu.VMEM((4, 128, NSTATE), F32)], sends=sends, gather=True)(ACT, DT, dtb, alog, tri2)


def ssd_bwd(ACT, DT, dtb, alog, tri2, HS, dY, L, sends=()):
    T = ACT.shape[0]
    nlc, nch = L // Q, T // Q

    def body(a_ref, dt_ref, dtb_ref, al_ref, tri_ref, hs_ref, dy_ref, da_o, ddt_o, ddtb_o, dal_o, dh):
        d, sr = pl.program_id(0), pl.program_id(1)
        _acc_init(sr == 0, [dh, ddtb_o, dal_o])
        a = a_ref[...]
        xs = [a[:, 128 * j:128 * (j + 1)] for j in range(4)]
        bs = [a[:, 512 + 128 * g:640 + 128 * g] for g in range(2)]
        cs = [a[:, 768 + 128 * g:896 + 128 * g] for g in range(2)]
        hs = [hs_ref[0, 0, j] for j in range(4)]
        tri = tri_ref[0]
        fn = lambda xs_, bs_, cs_, dtr, dtb_, al, hs_: _ssd_chunk(xs_, bs_, cs_, dtr, dtb_, al, hs_, tri, d)
        _, vjp = jax.vjp(fn, xs, bs, cs, dt_ref[...], dtb_ref[...], al_ref[...], hs)
        dy = dy_ref[...]
        dys = [dy[:, 128 * j:128 * (j + 1)] for j in range(4)]
        dxs, dbs, dcs, ddt, ddtb, dal, dhs = vjp((dys, [dh[j] for j in range(4)]))
        da_o[0] = jnp.concatenate(dxs + dbs + dcs, axis=1)
        ddt_o[0] = ddt
        ddtb_o[0] += ddtb
        dal_o[0] += dal
        for j in range(4):
            dh[j] = dhs[j]

    cidx = lambda d, sr: _ssd_chunk_idx(d, nch - 1 - sr, nlc, nch)
    ck = lambda w: pl.BlockSpec((Q, w), lambda d, sr: (cidx(d, sr), 0))
    v128 = pl.BlockSpec((1, 128), lambda d, sr: (0, 0))
    o128 = pl.BlockSpec((1, 1, 128), lambda d, sr: (d, 0, 0))
    return _pc(body, "ssd_bwd", [_sds((2, T, 1024)), _sds((2, T, 128)), _sds((2, 1, 128)), _sds((2, 1, 128))],
               grid=(2, nch),
               in_specs=[ck(1024), ck(128), v128, v128, pl.BlockSpec((1, Q, Q), lambda d, sr: (d, 0, 0)),
                         pl.BlockSpec((1, 1, 4, 128, NSTATE), lambda d, sr: (d, nch - 1 - sr, 0, 0, 0)), ck(512)],
               out_specs=[pl.BlockSpec((1, Q, 1024), lambda d, sr: (d, cidx(d, sr), 0)),
                          pl.BlockSpec((1, Q, 128), lambda d, sr: (d, cidx(d, sr), 0)), o128, o128],
               scratch=[pltpu.VMEM((4, 128, NSTATE), F32)], sends=sends)(ACT, DT, dtb, alog, tri2, HS, dY)


_PAIR_HEADS = np.array([[0, 2], [1, 3]])


def _tables(L):
    t = jnp.arange(L)
    inv = 10000.0 ** (-jnp.arange(16, dtype=F32) / 16)

    def half(pos):
        ang = pos.astype(F32)[:, None] * inv[None, :]
        return jnp.concatenate([ang, ang], axis=1)

    ang = jnp.tile(jnp.concatenate([half(t // GW), half(t % GW)], axis=1), (1, 4))
    cos = jnp.concatenate([jnp.cos(ang), jnp.ones((LC, 256), F32)], axis=0)
    sin = jnp.concatenate([jnp.sin(ang), jnp.zeros((LC, 256), F32)], axis=0)
    rm = np.zeros((256, 256), np.float32)
    for j in range(256):
        if j % 32 < 16:
            rm[j + 16, j] = -1.0
        else:
            rm[j - 16, j] = 1.0
    tri = np.tril(np.ones((Q, Q), np.float32))
    return cos, sin, jnp.asarray(rm), jnp.asarray(np.stack([tri, tri.T]))


def _na_index(R):
    rc = np.array([0, 1, 2, 3, 4, R - 3, R - 2, R - 1])
    dy = np.clip(rc - 4, 0, R - 8)[:, None] + np.arange(8)[None, :] - rc[:, None] + 7
    qc, cc = np.arange(GW)[:, None], np.arange(GW)[None, :]
    dx = np.clip(cc - qc, -15, 15) + 15
    cstart = np.clip(qc - 8, 0, GW - 16)
    cmask = (cc >= cstart) & (cc < cstart + 16)
    idx = dy[:, None, :, None] * 31 + dx[None, :, None, :]
    return idx.reshape(8, GW, 8 * GW), np.broadcast_to(cmask[None, :, None, :], idx.shape).reshape(8, GW, 8 * GW), \
        dy, dx, cmask


def _na_bias(rpb, R):
    _, cm, dy, dx, _ = _na_index(R)
    e1t = np.zeros((128, GW * GW), np.float32)
    e1t[dx.reshape(-1), np.arange(GW * GW)] = 1.0
    v = jnp.pad(rpb[:, dy.reshape(-1), :].reshape(256, 31), ((0, 0), (0, 97)))
    full = exact_mm_call(v, jnp.asarray(e1t))
    dense = full.reshape(4, 8, 8, GW, GW).transpose(0, 1, 3, 2, 4).reshape(4, 8, GW, 8 * GW)
    return jnp.where(cm[None], dense, NEG)


def _na_bias_grad(dbias, R):
    _, _, dy, dx, cmask = _na_index(R)
    e1 = np.zeros((GW * GW, 128), np.float32)
    e1[np.arange(GW * GW), dx.reshape(-1)] = cmask.reshape(-1)
    a1 = dbias.reshape(4, 8, GW, 8, GW).transpose(0, 1, 3, 2, 4).reshape(256, GW * GW)
    v = exact_mm_call(a1, jnp.asarray(e1))[:, :31].reshape(4, 64, 31)
    e2 = np.zeros((64, 128), np.float32)
    e2[np.arange(64), dy.reshape(-1)] = 1.0
    a2 = jnp.pad(v.transpose(0, 2, 1).reshape(124, 64), ((0, 4), (0, 0)))
    return exact_mm_call(a2, jnp.asarray(e2))[:124, :15].reshape(4, 31, 15).transpose(0, 2, 1)


def _lanes(v, n=128):
    v = v.reshape(1, -1)
    return jnp.pad(v, ((0, 0), (0, n - v.shape[1])))


def _cls2(a, b):
    return jnp.stack([a, b]).reshape(2, 1, D)


def _win_p(g):
    return jnp.concatenate([g.reshape(IN_COLS, D), jnp.zeros((NP_IN - IN_COLS, D), g.dtype)], axis=0)


def _wfi(ga, gb):
    return jnp.stack([ga, gb], axis=1).reshape(2 * DFF, D)


def _layer_consts(p):
    sinkp = jnp.broadcast_to(p["wa_sink"][_PAIR_HEADS][:, :, None, None], (2, 2, 1, 128))
    return dict(
        sinkp=sinkp, nosink=jnp.full((2, 2, 1, 128), NEG, F32),
        w8=jnp.concatenate([p["ssm_conv_w"], jnp.zeros((1, 1024), F32)], axis=0),
        cb=p["ssm_conv_b"].reshape(1, 1024), dtb=_lanes(p["ssm_dt_bias"]), alog=_lanes(p["ssm_a_log"]),
        dsk=jnp.repeat(p["ssm_d"], HD).reshape(1, 512), gs=p["ssm_norm_g"].reshape(1, 512),
        gmix=p["g_mix"].reshape(1, D), gffn=p["g_ffn"].reshape(1, D))


def _mods(mod2):
    return [_cls2(mod2[0, D * k:D * (k + 1)], mod2[1, D * k:D * (k + 1)]) for k in range(6)]


def _layer_fwd(X, mod2, c, rpb, tabs, L, ctx_out, ffn_shards, nxt):
    cos, sin, rm, tri2 = tabs
    sh1, sc1, gt1, sh2, sc2, gt2 = _mods(mod2)
    biasd = _na_bias(rpb, L // GW)
    fi, fo = ffn_shards
    (qa, qb, z, ka, va, kb, vb, xbc, dt, h1), (gfo,) = in_fwd(X, c["gmix"], sh1, sc1, c["win"], cos, sin, rm, L,
                                                              sends=(fo,))
    (oa,), (gfa,) = wa_fwd(qa, ka, va, c["sinkp"], L, sends=(fi[:DFF // NDEV],))
    ob = na_fwd(qb, kb, vb, biasd, L)
    if ctx_out:
        oa_c = ctx_fwd(qa, ka, va, c["sinkp"], True, L)
        ob_c = ctx_fwd(qb, kb, vb, c["nosink"], False, L)
    else:
        oa_c = ob_c = jnp.zeros((LC, 256), F32)
    oa = jnp.concatenate([oa, oa_c], axis=0)
    ob = jnp.concatenate([ob, ob_c], axis=0)
    act = conv_fwd(xbc, c["w8"], c["cb"], L)
    (y2, hs), (gfb,) = ssd_fwd(act, dt, c["dtb"], c["alog"], tri2, L, sends=(fi[DFF // NDEV:],))
    X1, cat = out_fwd(oa, ob, y2, act, z, c["dsk"], c["gs"], c["wout"], X, gt1, L)
    c = dict(c, wfi=_wfi(gfa, gfb), wfo=gfo.reshape(DFF, D))
    res = ffn_fwd(X1, c["gffn"], sh2, sc2, gt2, c["wfi"], c["wfo"], L, sends=nxt)
    (X2,), got = res if nxt else ((res,), ())
    saved = dict(X=X, X1=X1, qa=qa, qb=qb, z=z, ka=ka, va=va, kb=kb, vb=vb, xbc=xbc, dt=dt, h1=h1, oa=oa, ob=ob,
                 act=act, y2=y2, hs=hs, cat=cat, biasd=biasd)
    return X2, saved, c, got


def _row_blocks(gw):
    return gw.reshape(NDEV, gw.shape[0] // NDEV, gw.shape[1])


def _layer_bwd(dX2, s, mod2, c, tabs, L, ctx_out, carry):
    cos, sin, rm, tri2 = tabs
    sh1, sc1, gt1, sh2, sc2, gt2 = _mods(mod2)
    R = L // GW
    res = ffn_bwd(s["X1"], c["gffn"], sh2, sc2, gt2, c["wfi"], c["wfo"], dX2, L, sends=carry)
    (dX1, h2, dU, actf, dOut, dgffn, dsh2, dsc2, dgt2), got = res if carry else (res, ())
    g = {}
    gfi = _row_blocks(tn_mm(dU, h2, 1408, 1024, MXU))
    gfo = _row_blocks(tn_mm(actf, dOut, 1408, 1024, MXU))
    doa, dob, dy, dxs_skip, dz, dmix, ddsk, dgs, dgt1 = out_bwd(s["oa"], s["ob"], s["y2"], s["act"], s["z"], c["dsk"],
                                                                c["gs"], c["wout"], gt1, dX1, L)
    gout = _row_blocks(tn_mm(s["cat"], dmix, 1024, 512, MXU))
    (dS, ddt2, ddtb, dal), (g["w_ffn_in"], g["w_ffn_out"]) = ssd_bwd(
        s["act"], s["dt"], c["dtb"], c["alog"], tri2, s["hs"], dy, L, sends=(gfi, gfo))
    dxbc, dw8, dcb = conv_bwd(s["xbc"], c["w8"], c["cb"], dS, dxs_skip, L)
    (dqa, dkpad, dvpad, dkxa, dvxa, dska), (g["w_out"],) = wa_bwd(s["qa"], s["ka"], s["va"], c["sinkp"], doa, L,
                                                                  sends=(gout,))
    dqb, dkb, dvb, dkxb, dvxb, dbias = na_bwd(s["qb"], s["kb"], s["vb"], s["biasd"], dob, L)
    if ctx_out:
        dqa_c, dk1, dv1, dsk1 = ctx_bwd(s["qa"], s["ka"], s["va"], c["sinkp"], doa, True, L)
        dqb_c, dk2, dv2, _ = ctx_bwd(s["qb"], s["kb"], s["vb"], c["nosink"], dob, False, L)
        dkxa, dvxa, dska = dkxa + dk1, dvxa + dv1, dska + dsk1
        dkxb, dvxb = dkxb + dk2, dvxb + dv2
    else:
        dqa_c = dqb_c = jnp.zeros((LC, 256), F32)
    cat0 = lambda a, b: jnp.concatenate([a, b], axis=0)
    dX, dycat, dgmix, dsh1, dsc1 = in_bwd(
        s["X"], c["gmix"], sh1, sc1, c["win"], cos, sin, rm, dX1, cat0(dqa, dqa_c), cat0(dqb, dqb_c), dz,
        cat0(dkpad[Q:L + Q], dkxa), cat0(dvpad[Q:L + Q], dvxa), cat0(dkb, dkxb), cat0(dvb, dvxb), dxbc, ddt2, L)
    gin = _row_blocks(tn_mm(dycat, s["h1"], 1024, 1024, MXU)[:IN_COLS])
    g["g_mix"] = dgmix.reshape(D)
    g["g_ffn"] = dgffn.reshape(D)
    sk = jnp.sum(dska, axis=(2, 3))
    g["wa_sink"] = jnp.zeros((4,), F32).at[_PAIR_HEADS.reshape(-1)].set(sk.reshape(-1))
    g["na_rpb"] = _na_bias_grad(dbias, R)
    g["ssm_conv_w"] = dw8[:7]
    g["ssm_conv_b"] = dcb.reshape(1024)
    g["ssm_dt_bias"] = (ddtb[0] + ddtb[1])[0, :16].reshape(2, 8)
    g["ssm_a_log"] = (dal[0] + dal[1])[0, :16].reshape(2, 8)
    g["ssm_d"] = jnp.sum(ddsk.reshape(8, HD), axis=1)
    g["ssm_norm_g"] = dgs.reshape(512)
    dmod2 = jnp.concatenate([dsh1, dsc1, dgt1, dsh2, dsc2, dgt2], axis=2).reshape(2, 6 * D)
    return dX, g, dmod2, gin, got


def local_step(x, ctx, tgt, mods, layers, shards, g_final, L):
    tabs = _tables(L)
    X = jnp.concatenate([x, ctx], axis=0)
    consts = [_layer_consts(p) for p in layers]
    saved = []
    got = all_gather([shards["w_in"][0], shards["w_out"][0]], "gather_first")
    for i in range(2):
        consts[i] = dict(consts[i], win=_win_p(got[0]), wout=got[1].reshape(D, D))
        nxt = (shards["w_in"][1], shards["w_out"][1]) if i == 0 else ()
        X, s, consts[i], got = _layer_fwd(X, mods[i], consts[i], layers[i]["na_rpb"], tabs, L, i == 0,
                                          (shards["w_ffn_in"][i], shards["w_ffn_out"][i]), nxt)
        saved.append(s)
    loss8, dxl, dgfin = loss_head(X, g_final.reshape(1, D), tgt, L)
    dX = jnp.concatenate([dxl, jnp.zeros((LC, D), F32)], axis=0)
    grads, dmods = [None, None], [None, None]
    dX, grads[1], dmods[1], gin1, _ = _layer_bwd(dX, saved[1], mods[1], consts[1], tabs, L, False, ())
    dX, grads[0], dmods[0], gin0, (grads[1]["w_in"],) = _layer_bwd(dX, saved[0], mods[0], consts[0], tabs, L, True,
                                                                   (gin1,))
    grads[0]["w_in"], = all_to_all([gin0], "exchange_last")
    return loss8[0, 0], dX[:L], grads, jnp.stack(dmods), dgfin.reshape(D)


def _place():
    x, y, c = lax.axis_index("x"), lax.axis_index("y"), lax.axis_index("c")
    return x, y, c


def _slot(b):
    return 4 * b[0] + 2 * b[1] + b[2]


def _any():
    return pl.BlockSpec(memory_space=pl.ANY)


def all_gather(xs, name):
    n = len(xs)

    def body(*refs):
        x_refs, o_refs = refs[:n], refs[n:2 * n]
        send_sems, recv_sems, local_sems = refs[2 * n:]
        x, y, c = _place()
        me, sib = (x, y, c), (x, y, 1 - c)
        chips = [(1 - x, y), (x, 1 - y), (1 - x, 1 - y)]

        def copy(t, k, blk, to, src=None):
            dst = o_refs[t].at[_slot(blk)]
            return pltpu.make_async_remote_copy(
                src_ref=dst if src is None else src, dst_ref=dst, send_sem=send_sems.at[7 * t + k],
                recv_sem=recv_sems.at[7 * t + k], device_id=to, device_id_type=MESH_T)

        mine = [pltpu.make_async_copy(x_refs[t], o_refs[t].at[_slot(me)], local_sems.at[t]) for t in range(n)]
        for cp in mine:
            cp.start()
        first = []
        for t in range(n):
            first.append(copy(t, 0, me, sib, src=x_refs[t]))
            first += [copy(t, 1 + j, me, (*chip, c), src=x_refs[t]) for j, chip in enumerate(chips)]
        for cp in first:
            cp.start()
        passed = []
        for j, chip in enumerate(chips):
            for t in range(n):
                copy(t, 1 + j, (*chip, c), me).wait_recv()
                cp = copy(t, 4 + j, (*chip, c), sib)
                cp.start()
                passed.append(cp)
        for t in range(n):
            copy(t, 0, sib, me).wait_recv()
            for j, chip in enumerate(chips):
                copy(t, 4 + j, (*chip, 1 - c), me).wait_recv()
        for cp in first + passed:
            cp.wait_send()
        for cp in mine:
            cp.wait()

    return pl.pallas_call(
        body, name=name, out_shape=[_sds((NDEV,) + a.shape, a.dtype) for a in xs],
        in_specs=[_any()] * n, out_specs=[_any()] * n,
        scratch_shapes=[pltpu.SemaphoreType.DMA((7 * n,)), pltpu.SemaphoreType.DMA((7 * n,)),
                        pltpu.SemaphoreType.DMA((n,))],
        interpret=_INTERPRET)(*xs)


def all_to_all(xs, name):
    n = len(xs)

    def body(*refs):
        _a2a_start(refs[:n], refs[n:2 * n], *refs[2 * n:])
        _a2a_wait(refs[:n], refs[n:2 * n], *refs[2 * n:])

    return pl.pallas_call(
        body, name=name, out_shape=[_sds(a.shape, a.dtype) for a in xs],
        in_specs=[_any()] * n, out_specs=[_any()] * n, scratch_shapes=_a2a_sems(n), interpret=_INTERPRET)(*xs)


def _a2a_sems(n):
    return [pltpu.SemaphoreType.DMA((7 * n,)), pltpu.SemaphoreType.DMA((7 * n,)), pltpu.SemaphoreType.DMA((n,))]


def _a2a_copies(x_refs, o_refs, send_sems, recv_sems, local_sems):
    n = len(x_refs)
    x, y, c = _place()
    me = (x, y, c)
    flip = lambda v, b: (1 - v) if b else v
    peers = [(flip(x, k >> 2 & 1), flip(y, k >> 1 & 1), flip(c, k & 1)) for k in range(1, NDEV)]
    mine = [pltpu.make_async_copy(x_refs[t].at[_slot(me)], o_refs[t].at[_slot(me)], local_sems.at[t])
            for t in range(n)]

    def copy(t, k, src_slot, dst_slot, to):
        return pltpu.make_async_remote_copy(
            src_ref=x_refs[t].at[src_slot], dst_ref=o_refs[t].at[dst_slot], send_sem=send_sems.at[7 * t + k],
            recv_sem=recv_sems.at[7 * t + k], device_id=to, device_id_type=MESH_T)

    sends = [copy(t, k, _slot(p), _slot(me), p) for t in range(n) for k, p in enumerate(peers)]
    recvs = [copy(t, k, _slot(p), _slot(p), me) for t in range(n) for k, p in enumerate(peers)]
    return mine, sends, recvs


def _ag_copies(x_refs, o_refs, send_sems, recv_sems, local_sems):
    n = len(x_refs)
    x, y, c = _place()
    me = (x, y, c)
    flip = lambda v, b: (1 - v) if b else v
    peers = [(flip(x, k >> 2 & 1), flip(y, k >> 1 & 1), flip(c, k & 1)) for k in range(1, NDEV)]
    mine = [pltpu.make_async_copy(x_refs[t], o_refs[t].at[_slot(me)], local_sems.at[t]) for t in range(n)]

    def copy(t, k, dst_slot, to):
        return pltpu.make_async_remote_copy(
            src_ref=x_refs[t], dst_ref=o_refs[t].at[dst_slot], send_sem=send_sems.at[7 * t + k],
            recv_sem=recv_sems.at[7 * t + k], device_id=to, device_id_type=MESH_T)

    sends = [copy(t, k, _slot(me), p) for t in range(n) for k, p in enumerate(peers)]
    recvs = [copy(t, k, _slot(p), me) for t in range(n) for k, p in enumerate(peers)]
    return mine, sends, recvs


def _ag_start(x_refs, o_refs, send_sems, recv_sems, local_sems):
    mine, sends, _ = _ag_copies(x_refs, o_refs, send_sems, recv_sems, local_sems)
    for cp in mine + sends:
        cp.start()


def _ag_wait(x_refs, o_refs, send_sems, recv_sems, local_sems):
    mine, sends, recvs = _ag_copies(x_refs, o_refs, send_sems, recv_sems, local_sems)
    for cp in recvs:
        cp.wait_recv()
    for cp in sends:
        cp.wait_send()
    for cp in mine:
        cp.wait()


def _a2a_start(x_refs, o_refs, send_sems, recv_sems, local_sems):
    mine, sends, _ = _a2a_copies(x_refs, o_refs, send_sems, recv_sems, local_sems)
    for cp in mine + sends:
        cp.start()


def _a2a_wait(x_refs, o_refs, send_sems, recv_sems, local_sems):
    mine, sends, recvs = _a2a_copies(x_refs, o_refs, send_sems, recv_sems, local_sems)
    for cp in recvs:
        cp.wait_recv()
    for cp in sends:
        cp.wait_send()
    for cp in mine:
        cp.wait()


def adam_reduce(P, w, m, v, name):
    n, R, C = P.shape
    br = R // 4 if R % 64 == 0 else R

    def body(p_ref, w_ref, m_ref, v_ref, g_o, d_o, m_o, v_o):
        g = p_ref[0].astype(F32)
        for k in range(1, n):
            g = g + p_ref[k].astype(F32)
        m1 = ADAM_B1 * m_ref[...] + (1.0 - ADAM_B1) * g
        v1 = ADAM_B2 * v_ref[...] + (1.0 - ADAM_B2) * jnp.square(g)
        m_hat = m1 / (1.0 - ADAM_B1 ** ADAM_STEP)
        v_hat = v1 / (1.0 - ADAM_B2 ** ADAM_STEP)
        g_o[...] = g
        d_o[...] = -ADAM_LR * (m_hat / (jnp.sqrt(v_hat) + ADAM_EPS) + ADAM_WD * w_ref[...])
        m_o[...] = m1
        v_o[...] = v1

    blk = pl.BlockSpec((br, C), lambda i: (i, 0))
    return _pc(body, name, [_sds((R, C))] * 4, grid=(R // br,),
               in_specs=[pl.BlockSpec((n, br, C), lambda i: (0, i, 0)), blk, blk, blk], out_specs=[blk] * 4)(P, w, m, v)


def adam_layers(P0, P1, w, m, v, name):
    n, R, C = P0.shape
    br = R // 4 if R % 64 == 0 else R
    nb = R // br

    def body(p0_ref, p1_ref, w_ref, m_ref, v_ref, g_o, d_o, m_o, v_o):
        def total(p_ref):
            g = p_ref[0].astype(F32)
            for k in range(1, n):
                g = g + p_ref[k].astype(F32)
            return g

        g = jnp.where(pl.program_id(0) == 0, total(p0_ref), total(p1_ref))
        m1 = ADAM_B1 * m_ref[0] + (1.0 - ADAM_B1) * g
        v1 = ADAM_B2 * v_ref[0] + (1.0 - ADAM_B2) * jnp.square(g)
        m_hat = m1 / (1.0 - ADAM_B1 ** ADAM_STEP)
        v_hat = v1 / (1.0 - ADAM_B2 ** ADAM_STEP)
        g_o[0] = g
        d_o[0] = -ADAM_LR * (m_hat / (jnp.sqrt(v_hat) + ADAM_EPS) + ADAM_WD * w_ref[0])
        m_o[0] = m1
        v_o[0] = v1

    blk = pl.BlockSpec((1, br, C), lambda l, i: (l, i, 0))
    p0 = pl.BlockSpec((n, br, C), lambda l, i: (0, jnp.where(l == 0, i, nb - 1), 0))
    p1 = pl.BlockSpec((n, br, C), lambda l, i: (0, jnp.where(l == 1, i, 0), 0))
    return _pc(body, name, [_sds((2, R, C))] * 4, grid=(2, nb), in_specs=[p0, p1, blk, blk, blk],
               out_specs=[blk] * 4)(P0, P1, w, m, v)


def mod_fwd(scin, wmod, bcol):
    def body(s_ref, w_ref, b_ref, o_ref):
        o_ref[0] = mm(_silu(s_ref[...]), w_ref[0]) + b_ref[0]

    return _pc(body, "mod_fwd", _sds((2, 16, 768)), grid=(2,),
               in_specs=[pl.BlockSpec((16, D), lambda l: (0, 0)), pl.BlockSpec((1, D, 768), lambda l: (l, 0, 0)),
                         pl.BlockSpec((1, 1, 768), lambda l: (l, 0, 0))],
               out_specs=pl.BlockSpec((1, 16, 768), lambda l: (l, 0, 0)))(scin, wmod, bcol)


def mod_bwd(scin, wmod, G):
    def body(s_ref, w_ref, g_ref, dw_o, ds_o):
        _, vjp = jax.vjp(lambda s, w: mm(_silu(s), w), s_ref[...], w_ref[0])
        ds, dw = vjp(g_ref[0])
        dw_o[0] = dw
        _acc_init(pl.program_id(0) == 0, [ds_o])
        ds_o[...] += ds

    full = pl.BlockSpec((16, D), lambda l: (0, 0))
    wsp = pl.BlockSpec((1, D, 768), lambda l: (l, 0, 0))
    return _pc(body, "mod_bwd", [_sds((2, D, 768)), _sds((16, D))], grid=(2,),
               in_specs=[full, wsp, pl.BlockSpec((1, 16, 768), lambda l: (l, 0, 0))], out_specs=[wsp, full])(
        scin, wmod, G)


_SMALL = ["b_mod", "g_mix", "wa_sink", "na_rpb", "ssm_conv_w", "ssm_conv_b", "ssm_dt_bias", "ssm_a_log", "ssm_d",
          "ssm_norm_g", "g_ffn", "g_final", "dmod_s", "dmod_c"]


def _pack(parts):
    rows = []
    for a in parts:
        f = a.reshape(-1).astype(F32)
        rows.append(jnp.pad(f, (0, (-f.shape[0]) % 1024)).reshape(-1, 128))
    return jnp.concatenate(rows, axis=0)


def _unpack(packed, shapes):
    out, r = [], 0
    for s in shapes:
        nel = int(np.prod(s))
        nr = -(-nel // 1024) * 8
        out.append(packed[r:r + nr].reshape(-1)[:nel].reshape(s))
        r += nr
    return out


def kernel(x, c, ctx, c_ctx, w_mod, b_mod, g_mix, w_in, wa_sink, na_rpb, ssm_conv_w, ssm_conv_b, ssm_dt_bias, ssm_a_log, ssm_d, ssm_norm_g, w_out, g_ffn, w_ffn_in, w_ffn_out, g_final, loss_target, m_c_ctx, m_w_mod, m_b_mod, m_g_mix, m_w_in, m_wa_sink, m_na_rpb, m_ssm_conv_w, m_ssm_conv_b, m_ssm_dt_bias, m_ssm_a_log, m_ssm_d, m_ssm_norm_g, m_w_out, m_g_ffn, m_w_ffn_in, m_w_ffn_out, m_g_final, v_c_ctx, v_w_mod, v_b_mod, v_g_mix, v_w_in, v_wa_sink, v_na_rpb, v_ssm_conv_w, v_ssm_conv_b, v_ssm_dt_bias, v_ssm_a_log, v_ssm_d, v_ssm_norm_g, v_w_out, v_g_ffn, v_w_ffn_in, v_w_ffn_out, v_g_final):
    L = x.shape[1]
    px, py, pc = _place()
    me = 4 * px + 2 * py + pc
    W = dict(c_ctx=c_ctx, w_mod=w_mod, b_mod=b_mod, g_mix=g_mix, w_in=w_in, wa_sink=wa_sink, na_rpb=na_rpb,
             ssm_conv_w=ssm_conv_w, ssm_conv_b=ssm_conv_b, ssm_dt_bias=ssm_dt_bias, ssm_a_log=ssm_a_log, ssm_d=ssm_d,
             ssm_norm_g=ssm_norm_g, w_out=w_out, g_ffn=g_ffn, w_ffn_in=w_ffn_in, w_ffn_out=w_ffn_out, g_final=g_final)
    M = dict(c_ctx=m_c_ctx, w_mod=m_w_mod, b_mod=m_b_mod, g_mix=m_g_mix, w_in=m_w_in, wa_sink=m_wa_sink,
             na_rpb=m_na_rpb, ssm_conv_w=m_ssm_conv_w, ssm_conv_b=m_ssm_conv_b, ssm_dt_bias=m_ssm_dt_bias,
             ssm_a_log=m_ssm_a_log, ssm_d=m_ssm_d, ssm_norm_g=m_ssm_norm_g, w_out=m_w_out, g_ffn=m_g_ffn,
             w_ffn_in=m_w_ffn_in, w_ffn_out=m_w_ffn_out, g_final=m_g_final)
    V = dict(c_ctx=v_c_ctx, w_mod=v_w_mod, b_mod=v_b_mod, g_mix=v_g_mix, w_in=v_w_in, wa_sink=v_wa_sink,
             na_rpb=v_na_rpb, ssm_conv_w=v_ssm_conv_w, ssm_conv_b=v_ssm_conv_b, ssm_dt_bias=v_ssm_dt_bias,
             ssm_a_log=v_ssm_a_log, ssm_d=v_ssm_d, ssm_norm_g=v_ssm_norm_g, w_out=v_w_out, g_ffn=v_g_ffn,
             w_ffn_in=v_w_ffn_in, w_ffn_out=v_w_ffn_out, g_final=v_g_final)

    c_all, conv_all = all_gather([c, ssm_conv_w], "gather_small")
    tr = lambda a: a.transpose(0, 2, 1)
    shards = dict(w_in=tr(w_in).astype(MXU), w_out=w_out.astype(MXU), w_ffn_in=tr(w_ffn_in).astype(MXU),
                  w_ffn_out=w_ffn_out.astype(MXU))
    conv_f = conv_all.transpose(1, 2, 0, 3).reshape(2, 7, 1024)

    scin = jnp.concatenate([c_all.reshape(NDEV, D), c_ctx.reshape(1, D), jnp.zeros((7, D), F32)], axis=0)
    bcol = lax.dynamic_slice_in_dim(b_mod, me * 768, 768, axis=1).reshape(2, 1, 768)
    mod_all, = all_gather([mod_fwd(scin, w_mod, bcol)], "gather_mod")
    mod_rows = mod_all.transpose(1, 2, 0, 3).reshape(2, 16, 6 * D)
    mods = jnp.stack([lax.dynamic_index_in_dim(mod_rows, me, axis=1, keepdims=False), mod_rows[:, 8]], axis=1)

    layers = [dict(g_mix=g_mix[i], wa_sink=wa_sink[i], na_rpb=na_rpb[i], ssm_conv_w=conv_f[i],
                   ssm_conv_b=ssm_conv_b[i], ssm_dt_bias=ssm_dt_bias[i], ssm_a_log=ssm_a_log[i], ssm_d=ssm_d[i],
                   ssm_norm_g=ssm_norm_g[i], g_ffn=g_ffn[i]) for i in range(2)]
    loss, dx, grads, dmods, dgfin = local_step(x[0], ctx[0], loss_target[0], mods, layers, shards, g_final, L)
    loss = lax.psum(loss, ("x", "y", "c"))

    stk = lambda n: jnp.stack([grads[0][n], grads[1][n]])
    small = dict(b_mod=dmods[:, 0] + dmods[:, 1], g_final=dgfin, dmod_s=dmods[:, 0], dmod_c=dmods[:, 1])
    for nme in _SMALL:
        if nme not in small:
            small[nme] = stk(nme)
    shapes = [small[nme].shape for nme in _SMALL]
    zero_like = lambda nme: jnp.zeros(small[nme].shape, F32)
    own = lambda S, nme: S[nme] if (nme in S and S[nme].shape == small[nme].shape) else zero_like(nme)
    gath, = all_gather([_pack([small[nme] for nme in _SMALL])], "gather_grads")
    sm = adam_reduce(gath, _pack([own(W, nme) for nme in _SMALL]), _pack([own(M, nme) for nme in _SMALL]),
                     _pack([own(V, nme) for nme in _SMALL]), "adam_small")
    res = {nme: vals for nme, vals in zip(_SMALL, zip(*[_unpack(a, shapes) for a in sm]))}

    cols = lambda a: lax.dynamic_slice_in_dim(a, me * 768, 768, axis=-1)
    gparts = [_unpack(gath[d], shapes) for d in range(NDEV)]
    dmod_s_all = jnp.stack([gparts[d][_SMALL.index("dmod_s")] for d in range(NDEV)], axis=1)
    G = jnp.concatenate([cols(dmod_s_all), cols(res["dmod_c"][0])[:, None, :], jnp.zeros((2, 7, 768), F32)], axis=1)
    dwmod, dscin = mod_bwd(scin, w_mod, G)
    cc_g, = all_gather([dscin[8].reshape(8, 128)], "gather_cctx")
    out = {}
    out["c_ctx"] = [a.reshape(D) for a in adam_reduce(cc_g, c_ctx.reshape(8, 128), m_c_ctx.reshape(8, 128),
                                                      v_c_ctx.reshape(8, 128), "adam_cctx")]
    out["w_mod"] = [a.reshape(2, D, 768) for a in adam_reduce(
        dwmod.reshape(1, 2 * D, 768), w_mod.reshape(2 * D, 768), m_w_mod.reshape(2 * D, 768),
        v_w_mod.reshape(2 * D, 768), "adam_wmod")]
    gconv = lax.dynamic_slice_in_dim(res["ssm_conv_w"][0], me * 128, 128, axis=2)
    out["ssm_conv_w"] = [a.reshape(2, 7, 128) for a in adam_reduce(
        gconv.reshape(1, 14, 128), ssm_conv_w.reshape(14, 128), m_ssm_conv_w.reshape(14, 128),
        v_ssm_conv_w.reshape(14, 128), "adam_conv")]
    for nme in _SMALL:
        if nme not in ("ssm_conv_w", "dmod_s", "dmod_c"):
            out[nme] = list(res[nme])

    for nme in ("w_out", "w_ffn_out"):
        out[nme] = list(adam_layers(grads[0][nme], grads[1][nme], W[nme], M[nme], V[nme], "adam_" + nme))
    for nme in ("w_in", "w_ffn_in"):
        out[nme] = [tr(a) for a in adam_layers(grads[0][nme], grads[1][nme], tr(W[nme]), tr(M[nme]), tr(V[nme]),
                                               "adam_" + nme)]
    order = ["c_ctx", "w_mod", "b_mod", "g_mix", "w_in", "wa_sink", "na_rpb", "ssm_conv_w", "ssm_conv_b",
             "ssm_dt_bias", "ssm_a_log", "ssm_d", "ssm_norm_g", "w_out", "g_ffn", "w_ffn_in", "w_ffn_out", "g_final"]
    return (loss, dx.reshape(1, L, D), *[out[nme][0] for nme in order], *[out[nme][1] for nme in order],
            *[out[nme][2] for nme in order], *[out[nme][3] for nme in order])
```

```python
import functools
import math

import numpy as np
import jax
import jax.numpy as jnp
from jax import lax
from jax.experimental import pallas as pl
from jax.experimental.pallas import tpu as pltpu

F32 = jnp.float32
MXU = jnp.bfloat16
_INTERPRET = False
VMEM_LIMIT = 60 * 1024 * 1024

D = 1024
LC = 256
GW = 64
HD = 64
EPS = 1e-6
NEG = -1e30
NDEV = 8
Q = 128
NSTATE = 128
DFF = 2816
IN_COLS = 2832
NP_IN = 3072
C_QA, C_QB, C_Z, C_KA, C_VA, C_KB, C_VB, C_XBC, C_DT = 0, 256, 512, 1024, 1152, 1280, 1536, 1792, 2816
ADAM_LR, ADAM_B1, ADAM_B2, ADAM_EPS, ADAM_WD, ADAM_STEP = 0.001, 0.9, 0.999, 1e-08, 0.01, 10
MESH_T = pl.DeviceIdType.MESH


def _dg(a, b, ca, cb):
    return lax.dot_general(a.astype(MXU), b.astype(MXU), (((ca,), (cb,)), ((), ())), preferred_element_type=F32)


@jax.custom_vjp
def mm(a, b):
    return _dg(a, b, 1, 0)


def _mm_f(a, b):
    return _dg(a, b, 1, 0), (a, b)


def _mm_b(res, g):
    a, b = res
    return _dg(g, b, 1, 1).astype(a.dtype), _dg(a, g, 0, 0).astype(b.dtype)


mm.defvjp(_mm_f, _mm_b)


@jax.custom_vjp
def mm_nt(a, b):
    return _dg(a, b, 1, 1)


def _mmnt_f(a, b):
    return _dg(a, b, 1, 1), (a, b)


def _mmnt_b(res, g):
    a, b = res
    return _dg(g, b, 1, 0).astype(a.dtype), _dg(g, a, 0, 0).astype(b.dtype)


mm_nt.defvjp(_mmnt_f, _mmnt_b)


@jax.custom_vjp
def mm_tn(a, b):
    return _dg(a, b, 0, 0)


def _mmtn_f(a, b):
    return _dg(a, b, 0, 0), (a, b)


def _mmtn_b(res, g):
    a, b = res
    return _dg(b, g, 1, 1).astype(a.dtype), _dg(a, g, 1, 0).astype(b.dtype)


mm_tn.defvjp(_mmtn_f, _mmtn_b)


@jax.custom_vjp
def mmw(a, w):
    return _dg(a, w, 1, 0)


mmw.defvjp(lambda a, w: (_dg(a, w, 1, 0), w), lambda w, g: (_dg(g, w, 1, 1), None))


@jax.custom_vjp
def mmw_nt(a, w):
    return _dg(a, w, 1, 1)


mmw_nt.defvjp(lambda a, w: (_dg(a, w, 1, 1), w), lambda w, g: (_dg(g, w, 1, 0), None))


def _exact(a, b):
    return lax.dot_general(a, b, (((1,), (0,)), ((), ())), precision=lax.Precision.HIGHEST,
                           preferred_element_type=F32)


def _pc(body, name, out_shape, grid=None, in_specs=None, out_specs=None, scratch=(), sends=(), gather=False):
    params = pltpu.CompilerParams(vmem_limit_bytes=VMEM_LIMIT)
    if sends and not isinstance(out_shape, (list, tuple)):
        out_shape, out_specs = [out_shape], [out_specs]
    start, wait = (_ag_start, _ag_wait) if gather else (_a2a_start, _a2a_wait)
    if not sends:
        kw = {}
        if grid is not None:
            kw = dict(grid=grid, in_specs=in_specs, out_specs=out_specs)
        elif in_specs is not None:
            kw = dict(in_specs=in_specs, out_specs=out_specs)
        return pl.pallas_call(body, name=name, out_shape=out_shape, scratch_shapes=list(scratch),
                              compiler_params=params, interpret=_INTERPRET, **kw)
    n, nin, nout, nscr = len(sends), len(in_specs), len(out_shape), len(scratch)

    def body2(*refs):
        cin, xs = refs[:nin], refs[nin:nin + n]
        couts, os_ = refs[nin + n:nin + n + nout], refs[nin + n + nout:nin + 2 * n + nout]
        cscr, sems = refs[nin + 2 * n + nout:nin + 2 * n + nout + nscr], refs[nin + 2 * n + nout + nscr:]
        ids = [pl.program_id(a) for a in range(len(grid))]
        first = functools.reduce(lambda a, b: a & b, [i == 0 for i in ids])
        last = functools.reduce(lambda a, b: a & b, [i == g - 1 for i, g in zip(ids, grid)])

        @pl.when(first)
        def _():
            start(xs, os_, *sems)

        body(*cin, *couts, *cscr)

        @pl.when(last)
        def _():
            wait(xs, os_, *sems)

    call = pl.pallas_call(
        body2, name=name,
        out_shape=list(out_shape) + [_sds(((NDEV,) if gather else ()) + a.shape, a.dtype) for a in sends],
        grid=grid, in_specs=list(in_specs) + [_any()] * n, out_specs=list(out_specs) + [_any()] * n,
        scratch_shapes=list(scratch) + _a2a_sems(n), compiler_params=params, interpret=_INTERPRET)

    def run(*args):
        res = call(*args, *sends)
        return res[:nout], res[nout:]

    return run


def _vm():
    return pl.BlockSpec(memory_space=pltpu.VMEM)


def _sds(shape, dt=F32):
    return jax.ShapeDtypeStruct(shape, dt)


def _iota(shape, dim):
    return lax.broadcasted_iota(jnp.int32, shape, dim)


def _silu(x):
    return x * jax.nn.sigmoid(x)


def _softplus(x):
    return jnp.maximum(x, 0.0) + jnp.log1p(jnp.exp(-jnp.abs(x)))


def _normmod(x, g, sh, sc):
    r = lax.rsqrt(jnp.mean(x * x, axis=-1, keepdims=True) + EPS)
    return (x * r * g) * (1.0 + sc) + sh


def _rope(x, cos, sin, rm):
    return x * cos + _exact(x, rm) * sin


def _swap12(x):
    lane = _iota(x.shape, 1)
    up, down = pltpu.roll(x, 192, 1), pltpu.roll(x, 64, 1)
    return jnp.where((lane >= 64) & (lane < 128), up, jnp.where((lane >= 128) & (lane < 192), down, x))


def _acc_init(first, refs):
    @pl.when(first)
    def _():
        for r in refs:
            r[...] = jnp.zeros_like(r)


def _stream(X, TR, nlt):
    if not isinstance(X, tuple):
        return (X,), [pl.BlockSpec((TR, D), lambda i: (i, 0))], lambda refs: refs[0][...]
    specs = [pl.BlockSpec((TR, D), lambda i: (jnp.minimum(i, nlt - 1), 0)), pl.BlockSpec((TR, D), lambda i: (0, 0))]
    return X, specs, lambda refs: jnp.where(pl.program_id(0) < nlt, refs[0][...], refs[1][...])


def in_fwd(X, g, sh, sc, W, cos, sin, rm, L, sends=()):
    T = L + LC
    TR = 256
    nlt = L // TR
    xs, xspecs, xread = _stream(X, TR, nlt)

    def body(*refs):
        (g_ref, sh_ref, sc_ref, w_ref, cos_ref, sin_ref, rm_ref,
         qa, qb, z, ka, va, kb, vb, xbc, dt, hout) = refs[len(xs):]
        h = _normmod(xread(refs), g_ref[...], sh_ref[0], sc_ref[0]).astype(MXU)
        hout[...] = h
        y = lax.dot_general(h, w_ref[...], (((1,), (1,)), ((), ())), preferred_element_type=F32)
        cs, sn, r = cos_ref[...], sin_ref[...], rm_ref[...]
        qa[...] = _rope(_swap12(y[:, C_QA:C_QB]), cs, sn, r).astype(MXU)
        qb[...] = y[:, C_QB:C_Z].astype(MXU)
        z[...] = y[:, C_Z:C_KA]
        ka[...] = _rope(y[:, C_KA:C_VA], cs[:, :128], sn[:, :128], r[:128, :128]).astype(MXU)
        va[...] = y[:, C_VA:C_KB].astype(MXU)
        kb[...] = y[:, C_KB:C_VB].astype(MXU)
        vb[...] = y[:, C_VB:C_XBC].astype(MXU)
        xbc[...] = y[:, C_XBC:C_DT]
        dt[...] = y[:, C_DT:C_DT + 128]

    row = lambda w: pl.BlockSpec((TR, w), lambda i: (i, 0))
    cls = pl.BlockSpec((1, 1, D), lambda i: (i // nlt, 0, 0))
    widths = [(256, MXU), (256, MXU), (512, F32), (128, MXU), (128, MXU), (256, MXU), (256, MXU), (1024, F32),
              (128, F32), (D, MXU)]
    return _pc(body, "in_fwd", [_sds((T, w), d) for w, d in widths], grid=(T // TR,),
               in_specs=xspecs + [pl.BlockSpec((1, D), lambda i: (0, 0)), cls, cls, _vm(), row(256), row(256), _vm()],
               out_specs=[row(w) for w, _ in widths], sends=sends, gather=True)(*xs, g, sh, sc, W, cos, sin, rm)


def in_bwd(X, g, sh, sc, W, cos, sin, rm, dxres, dqa, dqb, dz, dka, dva, dkb, dvb, dxbc, ddt2, L, latent_only):
    T = L + LC
    TR = 256
    nlt = L // TR
    xs, xspecs, xread = _stream(X, TR, nlt)

    def body(*refs):
        (g_ref, sh_ref, sc_ref, w_ref, cos_ref, sin_ref, rm_ref, dxres_ref, dqa_r, dqb_r, dz_r, dka_r,
         dva_r, dkb_r, dvb_r, dxbc_r, ddt0_r, ddt1_r, dx_o, dy_o, dg_o, dsh_o, dsc_o) = refs[len(xs):]
        i = pl.program_id(0)
        cs, sn, r = cos_ref[...], sin_ref[...], rm_ref[...]
        _, vq = jax.vjp(lambda t: _rope(t, cs, sn, r), dqa_r[...])
        _, vk = jax.vjp(lambda t: _rope(t, cs[:, :128], sn[:, :128], r[:128, :128]), dka_r[...])
        dyqa = _swap12(vq(dqa_r[...])[0])
        dyka, = vk(dka_r[...])
        ddt = ddt0_r[0] + ddt1_r[0]
        dy = jnp.concatenate([dyqa, dqb_r[...], dz_r[...], dyka, dva_r[...], dkb_r[...], dvb_r[...], dxbc_r[...],
                              ddt, jnp.zeros((TR, NP_IN - C_DT - 128), F32)], axis=1).astype(MXU)
        dy_o[...] = dy
        dh = jnp.dot(dy, w_ref[...], preferred_element_type=F32)
        _, vp = jax.vjp(_normmod, xread(refs), g_ref[...], sh_ref[0], sc_ref[0])
        dx, dg, dsh, dsc = vp(dh)
        if latent_only:
            @pl.when(i < nlt)
            def _():
                dx_o[...] = dx + dxres_ref[...]
        else:
            dx_o[...] = dx + dxres_ref[...]
        _acc_init(i == 0, [dg_o])
        _acc_init((i == 0) | (i == nlt), [dsh_o, dsc_o])
        dg_o[...] += dg
        dsh_o[0] += dsh
        dsc_o[0] += dsc

    row = lambda w: pl.BlockSpec((TR, w), lambda i: (i, 0))
    cls = pl.BlockSpec((1, 1, D), lambda i: (i // nlt, 0, 0))
    vec = pl.BlockSpec((1, D), lambda i: (0, 0))
    dts = lambda d: pl.BlockSpec((1, TR, 128), lambda i: (d, i, 0))
    dxs = pl.BlockSpec((TR, D), lambda i: (jnp.minimum(i, nlt - 1), 0)) if latent_only else row(D)
    return _pc(body, "in_bwd",
               [_sds((L if latent_only else T, D)), _sds((T, NP_IN), MXU), _sds((1, D)), _sds((2, 1, D)),
                _sds((2, 1, D))],
               grid=(T // TR,),
               in_specs=xspecs + [vec, cls, cls, _vm(), row(256), row(256), _vm(), row(D), row(256), row(256),
                                  row(512), row(128), row(128), row(256), row(256), row(1024), dts(0), dts(1)],
               out_specs=[dxs, row(NP_IN), vec, cls, cls])(
        *xs, g, sh, sc, W, cos, sin, rm, dxres, dqa, dqb, dz, dka, dva, dkb, dvb, dxbc, ddt2, ddt2)


def tn_mm(A, G, bk, bn, out_dtype):
    T, K = A.shape
    N = G.shape[1]
    bt = T
    nt = T // bt

    def body(a_ref, g_ref, o_ref, acc):
        t = pl.program_id(2)
        _acc_init(t == 0, [acc])
        acc[...] += lax.dot_general(a_ref[...], g_ref[...], (((0,), (0,)), ((), ())), preferred_element_type=F32)

        @pl.when(t == nt - 1)
        def _():
            o_ref[...] = acc[...].astype(out_dtype)

    return _pc(body, "tn_mm", _sds((K, N), out_dtype), grid=(K // bk, N // bn, nt),
               in_specs=[pl.BlockSpec((bt, bk), lambda k, n, t: (t, k)), pl.BlockSpec((bt, bn), lambda k, n, t: (t, n))],
               out_specs=pl.BlockSpec((bk, bn), lambda k, n, t: (k, n)),
               scratch=[pltpu.VMEM((bk, bn), F32)])(A, G)


def _ssm_out(yf, yb, xs, z, dsk, gs):
    y = (yf + yb + dsk * xs) * _silu(z)
    r = lax.rsqrt(jnp.mean(y * y, axis=-1, keepdims=True) + EPS)
    return y * r * gs


def out_fwd(oa, ob, y2, act, z, dsk, gs, W, X, gate, L):
    T = L + LC
    TR = 256
    nlt = L // TR
    xs, xspecs, xread = _stream(X, TR, nlt)

    def body(*refs):
        oa_r, ob_r, yf_r, yb_r, xs_r, z_r, dsk_r, gs_r, w_ref, gt_ref, x1_o, cat_o = refs[len(xs):]
        oc = _ssm_out(yf_r[0], yb_r[0], xs_r[...], z_r[...], dsk_r[...], gs_r[...])
        cat = jnp.concatenate([_swap12(oa_r[...]), ob_r[...], oc], axis=1).astype(MXU)
        cat_o[...] = cat
        x1_o[...] = xread(refs) + gt_ref[0] * jnp.dot(cat, w_ref[...], preferred_element_type=F32)

    row = lambda w: pl.BlockSpec((TR, w), lambda i: (i, 0))
    ys = lambda d: pl.BlockSpec((1, TR, 512), lambda i: (d, i, 0))
    cls = pl.BlockSpec((1, 1, D), lambda i: (i // nlt, 0, 0))
    v512 = pl.BlockSpec((1, 512), lambda i: (0, 0))
    return _pc(body, "out_fwd", [_sds((T, D)), _sds((T, D), MXU)], grid=(T // TR,),
               in_specs=xspecs + [row(256), row(256), ys(0), ys(1), row(512), row(512), v512, v512, _vm(), cls],
               out_specs=[row(D), row(D)])(*xs, oa, ob, y2, y2, act, z, dsk, gs, W, gate)


def out_bwd(oa, ob, y2, act, z, dsk, gs, W, gate, dX1, L):
    T = dX1.shape[0]
    TR = 256
    nlt = L // TR

    def body(oa_r, ob_r, yf_r, yb_r, xs_r, z_r, dsk_r, gs_r, w_ref, gt_ref, dx1_r,
             doa_o, dob_o, dy_o, dxs_o, dz_o, dmix_o, ddsk_o, dgs_o, dgt_o):
        i = pl.program_id(0)
        w = w_ref[...]

        def f(oa_, ob_, yf, yb, xs, z_, dsk_, gs_, gt):
            oc = _ssm_out(yf, yb, xs, z_, dsk_, gs_)
            return gt * mmw(jnp.concatenate([oa_, ob_, oc], axis=1), w)

        _, vjp = jax.vjp(f, _swap12(oa_r[...]), ob_r[...], yf_r[0], yb_r[0], xs_r[...], z_r[...], dsk_r[...],
                         gs_r[...], gt_ref[0])
        dx1 = dx1_r[...]
        doa, dob, dyf, _, dxs, dz, ddsk, dgs, dgt = vjp(dx1)
        doa_o[...] = _swap12(doa)
        dob_o[...] = dob
        dy_o[...] = dyf
        dxs_o[...] = dxs
        dz_o[...] = dz
        dmix_o[...] = (gt_ref[0] * dx1).astype(MXU)
        _acc_init(i == 0, [ddsk_o, dgs_o])
        _acc_init((i == 0) | (i == nlt), [dgt_o])
        ddsk_o[...] += ddsk
        dgs_o[...] += dgs
        dgt_o[0] += dgt

    row = lambda w: pl.BlockSpec((TR, w), lambda i: (i, 0))
    ys = lambda d: pl.BlockSpec((1, TR, 512), lambda i: (d, i, 0))
    cls = pl.BlockSpec((1, 1, D), lambda i: (i // nlt, 0, 0))
    v512 = pl.BlockSpec((1, 512), lambda i: (0, 0))
    return _pc(body, "out_bwd",
               [_sds((T, 256)), _sds((T, 256)), _sds((T, 512)), _sds((T, 512)), _sds((T, 512)), _sds((T, D), MXU),
                _sds((1, 512)), _sds((1, 512)), _sds((2, 1, D))],
               grid=(T // TR,),
               in_specs=[row(256), row(256), ys(0), ys(1), row(512), row(512), v512, v512, _vm(), cls, row(D)],
               out_specs=[row(256), row(256), row(512), row(512), row(512), row(D), v512, v512, cls])(
        oa, ob, y2, y2, act, z, dsk, gs, W, gate, dX1)


def ffn_fwd(X, g, sh, sc, gate, Win, Wout, L, sends=()):
    T = X.shape[0]
    TR = 256
    nlt = L // TR

    def body(x_ref, g_ref, sh_ref, sc_ref, gt_ref, wi_ref, wo_ref, o_ref):
        h = _normmod(x_ref[...], g_ref[...], sh_ref[0], sc_ref[0]).astype(MXU)
        nt = (((1,), (1,)), ((), ()))
        a = lax.dot_general(h, wi_ref[0:DFF, :], nt, preferred_element_type=F32)
        u = lax.dot_general(h, wi_ref[DFF:2 * DFF, :], nt, preferred_element_type=F32)
        act = (_silu(a) * u).astype(MXU)
        o_ref[...] = x_ref[...] + gt_ref[0] * jnp.dot(act, wo_ref[...], preferred_element_type=F32)

    row = lambda w: pl.BlockSpec((TR, w), lambda i: (i, 0))
    cls = pl.BlockSpec((1, 1, D), lambda i: (i // nlt, 0, 0))
    vec = pl.BlockSpec((1, D), lambda i: (0, 0))
    return _pc(body, "ffn_fwd", _sds((T, D)), grid=(T // TR,),
               in_specs=[row(D), vec, cls, cls, cls, _vm(), _vm()], out_specs=row(D), sends=sends, gather=True)(
        X, g, sh, sc, gate, Win, Wout)


def ffn_bwd(X, g, sh, sc, gate, Win, Wout, dX2, L, sends=(), nchunk=2):
    T = X.shape[0]
    TR = 256
    nlt = L // TR
    CH = DFF // nchunk

    def body(x_ref, g_ref, sh_ref, sc_ref, gt_ref, wi_ref, wo_ref, dx2_r,
             dx_o, h_o, du_o, act_o, dout_o, dg_o, dsh_o, dsc_o, dgt_o):
        i = pl.program_id(0)
        h, vp = jax.vjp(_normmod, x_ref[...], g_ref[...], sh_ref[0], sc_ref[0])
        dx2 = dx2_r[...]
        dout = gt_ref[0] * dx2
        zero = jnp.zeros((TR, CH), F32)
        dh = jnp.zeros((TR, D), F32)
        out = jnp.zeros((TR, D), F32)
        for c in range(nchunk):
            lo, hi = c * CH, (c + 1) * CH
            wg, wu, wo = wi_ref[lo:hi, :], wi_ref[DFF + lo:DFF + hi, :], wo_ref[lo:hi, :]

            def f(h_, eg, eu):
                act = _silu(mmw_nt(h_, wg) + eg) * (mmw_nt(h_, wu) + eu)
                return mmw(act, wo), act

            o_c, vjp_c, act = jax.vjp(f, h, zero, zero, has_aux=True)
            dh_c, da, du = vjp_c(dout)
            dh, out = dh + dh_c, out + o_c
            du_o[:, lo:hi] = da.astype(MXU)
            du_o[:, DFF + lo:DFF + hi] = du.astype(MXU)
            act_o[:, lo:hi] = act.astype(MXU)
        dx, dg, dsh, dsc = vp(dh)
        dx_o[...] = dx + dx2
        h_o[...] = h.astype(MXU)
        dout_o[...] = dout.astype(MXU)
        _acc_init(i == 0, [dg_o])
        _acc_init((i == 0) | (i == nlt), [dsh_o, dsc_o, dgt_o])
        dg_o[...] += dg
        dsh_o[0] += dsh
        dsc_o[0] += dsc
        dgt_o[0] += jnp.sum(dx2 * out, axis=0, keepdims=True)

    row = lambda w: pl.BlockSpec((TR, w), lambda i: (i, 0))
    cls = pl.BlockSpec((1, 1, D), lambda i: (i // nlt, 0, 0))
    vec = pl.BlockSpec((1, D), lambda i: (0, 0))
    return _pc(body, "ffn_bwd",
               [_sds((T, D)), _sds((T, D), MXU), _sds((T, 2 * DFF), MXU), _sds((T, DFF), MXU), _sds((T, D), MXU),
                _sds((1, D)), _sds((2, 1, D)), _sds((2, 1, D)), _sds((2, 1, D))],
               grid=(T // TR,),
               in_specs=[row(D), vec, cls, cls, cls, _vm(), _vm(), row(D)],
               out_specs=[row(D), row(D), row(2 * DFF), row(DFF), row(D), vec, cls, cls, cls], sends=sends)(
        X, g, sh, sc, gate, Win, Wout, dX2)


def loss_head(X2, g, tgt, L):
    T = X2.shape[0]
    TR = 256
    nlt = L // TR

    def body(x_ref, g_ref, t_ref, loss_o, dx_o, dg_o):
        i = pl.program_id(0)
        _acc_init(i == 0, [loss_o, dg_o])

        @pl.when(i < nlt)
        def _():
            def f(x, g_):
                y = x * lax.rsqrt(jnp.mean(x * x, axis=-1, keepdims=True) + EPS) * g_
                return 0.5 * jnp.sum(jnp.mean(jnp.square(y - t_ref[...]), axis=-1, keepdims=True), axis=0,
                                     keepdims=True)

            val, vjp = jax.vjp(f, x_ref[...], g_ref[...])
            dx, dg = vjp(jnp.ones((1, 1), F32))
            dx_o[...] = dx
            loss_o[...] += jnp.broadcast_to(val, (8, 128))
            dg_o[...] += dg

        @pl.when(i >= nlt)
        def _():
            dx_o[...] = jnp.zeros_like(dx_o)

    row = pl.BlockSpec((TR, D), lambda i: (i, 0))
    vec = pl.BlockSpec((1, D), lambda i: (0, 0))
    return _pc(body, "loss_head", [_sds((8, 128)), _sds((T, D)), _sds((1, D))], grid=(T // TR,),
               in_specs=[row, vec, pl.BlockSpec((TR, D), lambda i: (jnp.minimum(i, nlt - 1), 0))],
               out_specs=[pl.BlockSpec((8, 128), lambda i: (0, 0)), row, vec])(X2, g, tgt)


def _stack_impl(q):
    lane = _iota(q.shape, 1)
    return jnp.concatenate([jnp.where(lane < HD, q, 0.0), jnp.where(lane >= HD, q, 0.0)], axis=0)


def _unstack_impl(o):
    M = o.shape[0] // 2
    return jnp.where(_iota((M, o.shape[1]), 1) < HD, o[:M], o[M:])


@jax.custom_vjp
def _stack(q):
    return _stack_impl(q)


_stack.defvjp(lambda q: (_stack_impl(q), None), lambda _, g: (_unstack_impl(g),))


@jax.custom_vjp
def _unstack(o):
    return _unstack_impl(o)


_unstack.defvjp(lambda o: (_unstack_impl(o), None), lambda _, g: (_stack_impl(g),))


def _softmax_av(q, ks, vs, biases, sink):
    q2 = _stack(q)
    ss = []
    for k, b in zip(ks, biases):
        s = mm_nt(q2, k) * (HD ** -0.5)
        ss.append(s if b is None else s + b)
    m = functools.reduce(jnp.maximum, [jnp.max(s, axis=1, keepdims=True) for s in ss])
    if sink is not None:
        m = jnp.maximum(m, sink)
    m = lax.stop_gradient(m)
    es = [jnp.exp(s - m) for s in ss]
    den = functools.reduce(lambda a, b_: a + b_, [jnp.sum(e, axis=1, keepdims=True) for e in es])
    if sink is not None:
        den = den + jnp.exp(sink - m)
    inv = 1.0 / den
    return _unstack(functools.reduce(lambda a, b_: a + b_, [mm(e * inv, v) for e, v in zip(es, vs)]))


def _sink_col(s0, s1, M):
    return jnp.concatenate([jnp.broadcast_to(jnp.mean(s0, axis=1, keepdims=True), (M, 1)),
                            jnp.broadcast_to(jnp.mean(s1, axis=1, keepdims=True), (M, 1))], axis=0)


def _stack4_impl(q):
    lane = _iota((q.shape[0], 128), 1)
    parts = []
    for p in range(2):
        qp = q[:, 128 * p:128 * (p + 1)]
        parts += [jnp.where(lane < HD, qp, 0.0), jnp.where(lane >= HD, qp, 0.0)]
    return jnp.concatenate(parts, axis=0)


def _unstack4_impl(o):
    M = o.shape[0] // 4
    lane = _iota((M, 128), 1)
    return jnp.concatenate([jnp.where(lane < HD, o[0:M], o[M:2 * M]),
                            jnp.where(lane < HD, o[2 * M:3 * M], o[3 * M:4 * M])], axis=1)


@jax.custom_vjp
def _stack4(q):
    return _stack4_impl(q)


_stack4.defvjp(lambda q: (_stack4_impl(q), None), lambda _, g: (_unstack4_impl(g),))


@jax.custom_vjp
def _unstack4(o):
    return _unstack4_impl(o)


_unstack4.defvjp(lambda o: (_unstack4_impl(o), None), lambda _, g: (_stack4_impl(g),))


def _wa_block(q, kp, kc, kn, vp, vc, vn, kx, vx, sks, n, L):
    kb = jnp.concatenate([kp, kc, kn], axis=0)
    vb = jnp.concatenate([vp, vc, vn], axis=0)
    qpos = n * Q + (_iota((4 * Q, 3 * Q), 0) & (Q - 1))
    kpos = (n - 1) * Q + _iota((4 * Q, 3 * Q), 1)
    valid = (jnp.abs(qpos - kpos) <= Q) & (kpos >= 0) & (kpos < L)
    bias = jnp.where(valid, 0.0, NEG)
    sink = jnp.concatenate([jnp.broadcast_to(jnp.mean(s_, axis=1, keepdims=True), (Q, 1)) for s_ in sks], axis=0)
    q4 = _stack4(q)
    sc = HD ** -0.5
    sl = mm_nt(q4, kb) * sc + bias
    sx = mm_nt(q4, kx) * sc
    m = lax.stop_gradient(jnp.maximum(jnp.maximum(jnp.max(sl, axis=1, keepdims=True),
                                                  jnp.max(sx, axis=1, keepdims=True)), sink))
    el, ex = jnp.exp(sl - m), jnp.exp(sx - m)
    inv = 1.0 / (jnp.sum(el, axis=1, keepdims=True) + jnp.sum(ex, axis=1, keepdims=True) + jnp.exp(sink - m))
    return _unstack4(mm(el * inv, vb) + mm(ex * inv, vx))


def _wa_specs(L):
    nb = L // Q
    qs = pl.BlockSpec((Q, 256), lambda n: (n, 0))
    kprev = pl.BlockSpec((Q, 128), lambda n: (jnp.maximum(n - 1, 0), 0))
    kcur = pl.BlockSpec((Q, 128), lambda n: (n, 0))
    knext = pl.BlockSpec((Q, 128), lambda n: (jnp.minimum(n + 1, nb - 1), 0))
    kctx = pl.BlockSpec((LC, 128), lambda n: (L // LC, 0))
    sks = pl.BlockSpec((2, 2, 1, 128), lambda n: (0, 0, 0, 0))
    return nb, qs, [kprev, kcur, knext], kctx, sks


def wa_fwd(QA, KA, VA, sinkp, L, sends=()):
    nb, qs, kband, kctx, sks = _wa_specs(L)

    def body(q_r, kp, kc, kn, vp, vc, vn, kx, vx, sk_r, o_ref):
        n = pl.program_id(0)
        f = lambda t: t[...].astype(F32)
        o_ref[...] = _wa_block(f(q_r), f(kp), f(kc), f(kn), f(vp), f(vc), f(vn), f(kx), f(vx),
                               [sk_r[0, 0], sk_r[0, 1], sk_r[1, 0], sk_r[1, 1]], n, L)

    return _pc(body, "wa_fwd", _sds((L, 256)), grid=(nb,),
               in_specs=[qs] + kband + kband + [kctx, kctx, sks], out_specs=qs, sends=sends, gather=True)(
        QA, KA, KA, KA, VA, VA, VA, KA, VA, sinkp)


def wa_bwd(QA, KA, VA, sinkp, dO, L, sends=()):
    nb, qs, kband, kctx, sks = _wa_specs(L)

    def body(q_r, kp, kc, kn, vp, vc, vn, kx, vx, sk_r, do_r, dq_o, dk_o, dv_o, dkx_o, dvx_o, dsk_o):
        n = pl.program_id(0)
        f = lambda t: t[...].astype(F32)
        fn = lambda q, a, b, c, d, e, g, kx_, vx_, s_: _wa_block(q, a, b, c, d, e, g, kx_, vx_, s_, n, L)
        _, vjp = jax.vjp(fn, f(q_r), f(kp), f(kc), f(kn), f(vp), f(vc), f(vn), f(kx), f(vx),
                         [sk_r[0, 0], sk_r[0, 1], sk_r[1, 0], sk_r[1, 1]])
        dq, dkp, dkc, dkn, dvp, dvc, dvn, dkx, dvx, ds = vjp(do_r[...])
        dq_o[...] = dq
        _acc_init(n == 0, [dk_o, dv_o, dkx_o, dvx_o, dsk_o])
        rows = pl.ds(pl.multiple_of(n * Q, Q), 3 * Q)
        dk_o[rows, :] += jnp.concatenate([dkp, dkc, dkn], axis=0)
        dv_o[rows, :] += jnp.concatenate([dvp, dvc, dvn], axis=0)
        dkx_o[...] += dkx
        dvx_o[...] += dvx
        for i_ in range(4):
            dsk_o[i_ // 2, i_ % 2] += ds[i_]

    full = lambda r: pl.BlockSpec((r, 128), lambda n: (0, 0))
    return _pc(body, "wa_bwd",
               [_sds((L, 256)), _sds((L + 2 * Q, 128)), _sds((L + 2 * Q, 128)), _sds((LC, 128)), _sds((LC, 128)),
                _sds((2, 2, 1, 128))],
               grid=(nb,), in_specs=[qs] + kband + kband + [kctx, kctx, sks, qs],
               out_specs=[qs, full(L + 2 * Q), full(L + 2 * Q), full(LC), full(LC), sks], sends=sends)(
        QA, KA, KA, KA, VA, VA, VA, KA, VA, sinkp, dO)


def _ctx_block(q, kx, vx, s0, s1):
    return _softmax_av(q, [kx], [vx], [None], _sink_col(s0, s1, LC))


def ctx_fwd(Qx, Kx, Vx, sinkp, shared, L):
    cq = pl.BlockSpec((LC, 128), lambda p: (L // LC, p))
    ck = pl.BlockSpec((LC, 128), lambda p: (L // LC, 0 if shared else p))
    sks = pl.BlockSpec((1, 2, 1, 128), lambda p: (p, 0, 0, 0))

    def body(q_r, k_r, v_r, sk_r, o_ref):
        f = lambda t: t[...].astype(F32)
        o_ref[...] = _ctx_block(f(q_r), f(k_r), f(v_r), sk_r[0, 0], sk_r[0, 1])

    return _pc(body, "ctx_fwd", _sds((LC, 256)), grid=(2,), in_specs=[cq, ck, ck, sks],
               out_specs=pl.BlockSpec((LC, 128), lambda p: (0, p)))(Qx, Kx, Vx, sinkp)


def ctx_bwd(Qx, Kx, Vx, sinkp, dO, shared, L):
    cq = pl.BlockSpec((LC, 128), lambda p: (L // LC, p))
    ck = pl.BlockSpec((LC, 128), lambda p: (L // LC, 0 if shared else p))
    sks = pl.BlockSpec((1, 2, 1, 128), lambda p: (p, 0, 0, 0))
    op = pl.BlockSpec((LC, 128), lambda p: (0, p))
    ok = pl.BlockSpec((LC, 128), lambda p: (0, 0 if shared else p))
    dos = pl.BlockSpec((LC, 128), lambda p: (L // LC, p))

    def body(q_r, k_r, v_r, sk_r, do_r, dq_o, dk_o, dv_o, dsk_o):
        p = pl.program_id(0)
        f = lambda t: t[...].astype(F32)
        _, vjp = jax.vjp(_ctx_block, f(q_r), f(k_r), f(v_r), sk_r[0, 0], sk_r[0, 1])
        dq, dk, dv, ds0, ds1 = vjp(do_r[...])
        dq_o[...] = dq
        _acc_init((p == 0) if shared else (p >= 0), [dk_o, dv_o])
        dk_o[...] += dk
        dv_o[...] += dv
        dsk_o[0, 0] = ds0
        dsk_o[0, 1] = ds1

    kw = 128 if shared else 256
    return _pc(body, "ctx_bwd", [_sds((LC, 256)), _sds((LC, kw)), _sds((LC, kw)), _sds((2, 2, 1, 128))],
               grid=(2,), in_specs=[cq, ck, ck, sks, dos], out_specs=[op, ok, ok, sks])(Qx, Kx, Vx, sinkp, dO)


def _na_rows(qs, kws, vws, kx, vx, bs):
    sc = HD ** -0.5
    q2 = [_stack(q) for q in qs]
    sl = [mm_nt(a, k) * sc + b for a, k, b in zip(q2, kws, bs)]
    sx = [mm_nt(a, kx) * sc for a in q2]
    m = [lax.stop_gradient(jnp.maximum(jnp.max(a, axis=1, keepdims=True), jnp.max(b, axis=1, keepdims=True)))
         for a, b in zip(sl, sx)]
    el = [jnp.exp(a - c) for a, c in zip(sl, m)]
    ex = [jnp.exp(a - c) for a, c in zip(sx, m)]
    inv = [1.0 / (jnp.sum(a, axis=1, keepdims=True) + jnp.sum(b, axis=1, keepdims=True)) for a, b in zip(el, ex)]
    o2 = [mm(a * i, v) + mm(b * i, vx) for a, b, i, v in zip(el, ex, inv, vws)]
    return [_unstack(o) for o in o2]


def _na_geom(rb, j, R):
    r = rb * 8 + j
    s = jnp.clip(r - 4, 0, R - 8)
    cls = jnp.where(r < 4, r, jnp.where(r > R - 4, r - (R - 8), 4))
    return pl.ds(pl.multiple_of(s * GW, GW), 8 * GW), cls


def _na_load(q_r, k_r, v_r, b_r, rb, R):
    geo = [_na_geom(rb, j, R) for j in range(8)]
    qs = [q_r[j * GW:(j + 1) * GW, :].astype(F32) for j in range(8)]
    kws = [k_r[win, :].astype(F32) for win, _ in geo]
    vws = [v_r[win, :].astype(F32) for win, _ in geo]
    bs = [jnp.concatenate([b_r[0, cls], b_r[1, cls]], axis=0) for _, cls in geo]
    return geo, qs, kws, vws, bs


def na_fwd(QB, KB, VB, biasd, L):
    R = L // GW
    qs = pl.BlockSpec((8 * GW, 128), lambda p, rb: (rb, p))
    kfull = pl.BlockSpec((L, 128), lambda p, rb: (0, p))
    kctx = pl.BlockSpec((LC, 128), lambda p, rb: (L // LC, p))
    bs = pl.BlockSpec((2, 8, GW, 8 * GW), lambda p, rb: (p, 0, 0, 0))

    def body(q_r, k_r, v_r, kx_r, vx_r, b_r, o_ref):
        _, qs_, kws, vws, bs_ = _na_load(q_r, k_r, v_r, b_r, pl.program_id(1), R)
        outs = _na_rows(qs_, kws, vws, kx_r[...].astype(F32), vx_r[...].astype(F32), bs_)
        o_ref[...] = jnp.concatenate(outs, axis=0)

    return _pc(body, "na_fwd", _sds((L, 256)), grid=(2, R // 8), in_specs=[qs, kfull, kfull, kctx, kctx, bs],
               out_specs=qs)(QB, KB, VB, KB, VB, biasd)


def na_bwd(QB, KB, VB, biasd, dO, L):
    R = L // GW
    qs = pl.BlockSpec((8 * GW, 128), lambda p, rb: (rb, p))
    kfull = pl.BlockSpec((L, 128), lambda p, rb: (0, p))
    kctx = pl.BlockSpec((LC, 128), lambda p, rb: (L // LC, p))
    bs = pl.BlockSpec((2, 8, GW, 8 * GW), lambda p, rb: (p, 0, 0, 0))
    oc = pl.BlockSpec((LC, 128), lambda p, rb: (0, p))

    def body(q_r, k_r, v_r, kx_r, vx_r, b_r, do_r, dq_o, dk_o, dv_o, dkx_o, dvx_o, db_o):
        rb = pl.program_id(1)
        _acc_init(rb == 0, [dk_o, dv_o, dkx_o, dvx_o, db_o])
        geo, qs_, kws, vws, bs_ = _na_load(q_r, k_r, v_r, b_r, rb, R)
        _, vjp = jax.vjp(_na_rows, qs_, kws, vws, kx_r[...].astype(F32), vx_r[...].astype(F32), bs_)
        dqs, dkws, dvws, dkx, dvx, dbs = vjp([do_r[j * GW:(j + 1) * GW, :] for j in range(8)])
        dq_o[...] = jnp.concatenate(dqs, axis=0)
        dkx_o[...] += dkx
        dvx_o[...] += dvx
        for j, (win, cls) in enumerate(geo):
            dk_o[win, :] += dkws[j]
            dv_o[win, :] += dvws[j]
            db_o[0, cls] += dbs[j][:GW]
            db_o[1, cls] += dbs[j][GW:]

    return _pc(body, "na_bwd",
               [_sds((L, 256)), _sds((L, 256)), _sds((L, 256)), _sds((LC, 256)), _sds((LC, 256)),
                _sds((4, 8, GW, 8 * GW))],
               grid=(2, R // 8), in_specs=[qs, kfull, kfull, kctx, kctx, bs, qs],
               out_specs=[qs, kfull, kfull, oc, oc, bs])(QB, KB, VB, KB, VB, biasd, dO)


def exact_mm_call(A, B):
    def body(a_ref, b_ref, o_ref):
        o_ref[...] = _exact(a_ref[...], b_ref[...])

    return _pc(body, "exact_mm", _sds((A.shape[0], B.shape[1])))(A, B)


def _conv_shift(x, d, L):
    T = x.shape[0]
    if d == 0:
        return x
    t = _iota(x.shape, 0)
    src = t + d
    ok = (src >= 0) & (src < T) & ((src >= L) == (t >= L))
    return jnp.where(ok, pltpu.roll(x, (-d) % T, 0), 0.0)


def conv_fwd(XBC, w8, b, L):
    T = XBC.shape[0]

    def body(x_ref, w_ref, b_ref, o_ref):
        x = x_ref[...]
        pre = b_ref[...] + functools.reduce(
            lambda a, c: a + c, [_conv_shift(x, k - 3, L) * w_ref[k:k + 1, :] for k in range(7)])
        o_ref[...] = _silu(pre)

    col = pl.BlockSpec((T, 128), lambda j: (0, j))
    return _pc(body, "conv_fwd", _sds((T, 1024)), grid=(8,),
               in_specs=[col, pl.BlockSpec((8, 128), lambda j: (0, j)), pl.BlockSpec((1, 128), lambda j: (0, j))],
               out_specs=col)(XBC, w8, b)


def conv_bwd(XBC, w8, b, dS, dxs_skip, L):
    T = XBC.shape[0]

    def body(x_ref, w_ref, b_ref, d0_r, d1_r, dsk_r, dx_o, dw_o, db_o):
        j = pl.program_id(0)
        x = x_ref[...]
        xs = [_conv_shift(x, k - 3, L) for k in range(7)]
        pre = b_ref[...] + functools.reduce(lambda a, c: a + c, [xs[k] * w_ref[k:k + 1, :] for k in range(7)])
        _, vjp = jax.vjp(_silu, pre)
        dact = d0_r[0] + d1_r[0] + jnp.where(j < 4, dsk_r[...], 0.0)
        dpre, = vjp(dact)
        dx_o[...] = functools.reduce(
            lambda a, c: a + c, [_conv_shift(dpre, 3 - k, L) * w_ref[k:k + 1, :] for k in range(7)])
        dw_o[...] = jnp.concatenate([jnp.sum(dpre * xs[k], axis=0, keepdims=True) for k in range(7)]
                                    + [jnp.zeros((1, 128), F32)], axis=0)
        db_o[...] = jnp.sum(dpre, axis=0, keepdims=True)

    col = pl.BlockSpec((T, 128), lambda j: (0, j))
    w_s = pl.BlockSpec((8, 128), lambda j: (0, j))
    b_s = pl.BlockSpec((1, 128), lambda j: (0, j))
    ds = lambda d: pl.BlockSpec((1, T, 128), lambda j: (d, 0, j))
    return _pc(body, "conv_bwd", [_sds((T, 1024)), _sds((8, 1024)), _sds((1, 1024))], grid=(8,),
               in_specs=[col, w_s, b_s, ds(0), ds(1), pl.BlockSpec((T, 128), lambda j: (0, jnp.minimum(j, 3)))],
               out_specs=[col, w_s, b_s])(XBC, w8, b, dS, dS, dxs_skip)


def _ssd_chunk(xs, bs, cs, dtraw, dtb, alog, hs, tri, d):
    dt = _softplus(dtraw + dtb)
    a = dt * (-jnp.exp(alog))
    acum = _exact(tri, a)
    tot = jnp.sum(a, axis=0, keepdims=True)
    wcol = jnp.exp(tot - acum) * dt
    ea = jnp.exp(acum)
    cd = jnp.exp(tot)
    acum_t, dt_t = acum.T, dt.T
    lane = _iota((Q, 128), 1)
    srow = _iota((128, Q), 0)
    lane1 = _iota((1, 128), 1)
    prow = _iota((128, NSTATE), 0)
    mask = tri > 0.5
    cbs = [mm_nt(cs[g], bs[g]) for g in range(2)]
    ys, hn = [], []
    for j in range(4):
        g = j // 2
        x = xs[j]
        yi, st, eac, cdl = [], [], [], []
        for u in range(2):
            slot = d * 8 + 2 * j + u
            col = lambda m: jnp.sum(jnp.where(lane == slot, m, 0.0), axis=1, keepdims=True)
            rowv = lambda m: jnp.sum(jnp.where(srow == slot, m, 0.0), axis=0, keepdims=True)
            seg = col(acum) - rowv(acum_t)
            dcy = jnp.where(mask, jnp.exp(jnp.where(mask, seg, 0.0)), 0.0)
            yi.append(mm(cbs[g] * dcy * rowv(dt_t), x))
            st.append(mm_tn(x, bs[g] * col(wcol)))
            eac.append(col(ea))
            cdl.append(jnp.sum(jnp.where(lane1 == slot, cd, 0.0), axis=1, keepdims=True))
        yin = mm_nt(cs[g], hs[j])
        ys.append(jnp.where(lane < HD, yi[0] + yin * eac[0], yi[1] + yin * eac[1]))
        hn.append(hs[j] * jnp.where(prow < HD, cdl[0], cdl[1]) + jnp.where(prow < HD, st[0], st[1]))
    return ys, hn


def _ssd_chunk_idx(d, s, nlc, nch):
    return jnp.where(d == 0, (s + nlc) % nch, nch - 1 - s)


def ssd_fwd(ACT, DT, dtb, alog, tri2, L, sends=()):
    T = ACT.shape[0]
    nlc, nch = L // Q, T // Q

    def body(a_ref, dt_ref, dtb_ref, al_ref, tri_ref, y_o, hs_o, hst):
        d, s = pl.program_id(0), pl.program_id(1)
        _acc_init(s == 0, [hst])
        a = a_ref[...]
        xs = [a[:, 128 * j:128 * (j + 1)] for j in range(4)]
        bs = [a[:, 512 + 128 * g:640 + 128 * g] for g in range(2)]
        cs = [a[:, 768 + 128 * g:896 + 128 * g] for g in range(2)]
        hs = [hst[j] for j in range(4)]
        hs_o[0, 0] = hst[...]
        ys, hn = _ssd_chunk(xs, bs, cs, dt_ref[...], dtb_ref[...], al_ref[...], hs, tri_ref[0], d)
        y_o[0] = jnp.concatenate(ys, axis=1)
        for j in range(4):
            hst[j] = hn[j]

    ck = lambda w: pl.BlockSpec((Q, w), lambda d, s: (_ssd_chunk_idx(d, s, nlc, nch), 0))
    v128 = pl.BlockSpec((1, 128), lambda d, s: (0, 0))
    return _pc(body, "ssd_fwd", [_sds((2, T, 512)), _sds((2, nch, 4, 128, NSTATE))], grid=(2, nch),
               in_specs=[ck(1024), ck(128), v128, v128, pl.BlockSpec((1, Q, Q), lambda d, s: (d, 0, 0))],
               out_specs=[pl.BlockSpec((1, Q, 512), lambda d, s: (d, _ssd_chunk_idx(d, s, nlc, nch), 0)),
                          pl.BlockSpec((1, 1, 4, 128, NSTATE), lambda d, s: (d, s, 0, 0, 0))],
               scratch=[pltpu.VMEM((4, 128, NSTATE), F32)], sends=sends, gather=True)(ACT, DT, dtb, alog, tri2)


def ssd_bwd(ACT, DT, dtb, alog, tri2, HS, dY, L, sends=()):
    T = ACT.shape[0]
    nlc, nch = L // Q, T // Q

    def body(a_ref, dt_ref, dtb_ref, al_ref, tri_ref, hs_ref, dy_ref, da_o, ddt_o, ddtb_o, dal_o, dh):
        d, sr = pl.program_id(0), pl.program_id(1)
        _acc_init(sr == 0, [dh, ddtb_o, dal_o])
        a = a_ref[...]
        xs = [a[:, 128 * j:128 * (j + 1)] for j in range(4)]
        bs = [a[:, 512 + 128 * g:640 + 128 * g] for g in range(2)]
        cs = [a[:, 768 + 128 * g:896 + 128 * g] for g in range(2)]
        hs = [hs_ref[0, 0, j] for j in range(4)]
        tri = tri_ref[0]
        fn = lambda xs_, bs_, cs_, dtr, dtb_, al, hs_: _ssd_chunk(xs_, bs_, cs_, dtr, dtb_, al, hs_, tri, d)
        _, vjp = jax.vjp(fn, xs, bs, cs, dt_ref[...], dtb_ref[...], al_ref[...], hs)
        dy = dy_ref[...]
        dys = [dy[:, 128 * j:128 * (j + 1)] for j in range(4)]
        dxs, dbs, dcs, ddt, ddtb, dal, dhs = vjp((dys, [dh[j] for j in range(4)]))
        da_o[0] = jnp.concatenate(dxs + dbs + dcs, axis=1)
        ddt_o[0] = ddt
        ddtb_o[0] += ddtb
        dal_o[0] += dal
        for j in range(4):
            dh[j] = dhs[j]

    cidx = lambda d, sr: _ssd_chunk_idx(d, nch - 1 - sr, nlc, nch)
    ck = lambda w: pl.BlockSpec((Q, w), lambda d, sr: (cidx(d, sr), 0))
    v128 = pl.BlockSpec((1, 128), lambda d, sr: (0, 0))
    o128 = pl.BlockSpec((1, 1, 128), lambda d, sr: (d, 0, 0))
    return _pc(body, "ssd_bwd", [_sds((2, T, 1024)), _sds((2, T, 128)), _sds((2, 1, 128)), _sds((2, 1, 128))],
               grid=(2, nch),
               in_specs=[ck(1024), ck(128), v128, v128, pl.BlockSpec((1, Q, Q), lambda d, sr: (d, 0, 0)),
                         pl.BlockSpec((1, 1, 4, 128, NSTATE), lambda d, sr: (d, nch - 1 - sr, 0, 0, 0)), ck(512)],
               out_specs=[pl.BlockSpec((1, Q, 1024), lambda d, sr: (d, cidx(d, sr), 0)),
                          pl.BlockSpec((1, Q, 128), lambda d, sr: (d, cidx(d, sr), 0)), o128, o128],
               scratch=[pltpu.VMEM((4, 128, NSTATE), F32)], sends=sends)(ACT, DT, dtb, alog, tri2, HS, dY)


_PAIR_HEADS = np.array([[0, 2], [1, 3]])


def _tables(L):
    t = jnp.arange(L)
    inv = 10000.0 ** (-jnp.arange(16, dtype=F32) / 16)

    def half(pos):
        ang = pos.astype(F32)[:, None] * inv[None, :]
        return jnp.concatenate([ang, ang], axis=1)

    ang = jnp.tile(jnp.concatenate([half(t // GW), half(t % GW)], axis=1), (1, 4))
    cos = jnp.concatenate([jnp.cos(ang), jnp.ones((LC, 256), F32)], axis=0)
    sin = jnp.concatenate([jnp.sin(ang), jnp.zeros((LC, 256), F32)], axis=0)
    rm = np.zeros((256, 256), np.float32)
    for j in range(256):
        if j % 32 < 16:
            rm[j + 16, j] = -1.0
        else:
            rm[j - 16, j] = 1.0
    tri = np.tril(np.ones((Q, Q), np.float32))
    return cos, sin, jnp.asarray(rm), jnp.asarray(np.stack([tri, tri.T]))


def _na_index(R):
    rc = np.array([0, 1, 2, 3, 4, R - 3, R - 2, R - 1])
    dy = np.clip(rc - 4, 0, R - 8)[:, None] + np.arange(8)[None, :] - rc[:, None] + 7
    qc, cc = np.arange(GW)[:, None], np.arange(GW)[None, :]
    dx = np.clip(cc - qc, -15, 15) + 15
    cstart = np.clip(qc - 8, 0, GW - 16)
    cmask = (cc >= cstart) & (cc < cstart + 16)
    idx = dy[:, None, :, None] * 31 + dx[None, :, None, :]
    return idx.reshape(8, GW, 8 * GW), np.broadcast_to(cmask[None, :, None, :], idx.shape).reshape(8, GW, 8 * GW), \
        dy, dx, cmask


def _na_bias(rpb, R):
    _, cm, dy, dx, _ = _na_index(R)
    e1t = np.zeros((128, GW * GW), np.float32)
    e1t[dx.reshape(-1), np.arange(GW * GW)] = 1.0
    v = jnp.pad(rpb[:, dy.reshape(-1), :].reshape(256, 31), ((0, 0), (0, 97)))
    full = exact_mm_call(v, jnp.asarray(e1t))
    dense = full.reshape(4, 8, 8, GW, GW).transpose(0, 1, 3, 2, 4).reshape(4, 8, GW, 8 * GW)
    return jnp.where(cm[None], dense, NEG)


def _na_bias_grad(dbias, R):
    _, _, dy, dx, cmask = _na_index(R)
    e1 = np.zeros((GW * GW, 128), np.float32)
    e1[np.arange(GW * GW), dx.reshape(-1)] = cmask.reshape(-1)
    a1 = dbias.reshape(4, 8, GW, 8, GW).transpose(0, 1, 3, 2, 4).reshape(256, GW * GW)
    v = exact_mm_call(a1, jnp.asarray(e1))[:, :31].reshape(4, 64, 31)
    e2 = np.zeros((64, 128), np.float32)
    e2[np.arange(64), dy.reshape(-1)] = 1.0
    a2 = jnp.pad(v.transpose(0, 2, 1).reshape(124, 64), ((0, 4), (0, 0)))
    return exact_mm_call(a2, jnp.asarray(e2))[:124, :15].reshape(4, 31, 15).transpose(0, 2, 1)


def _lanes(v, n=128):
    v = v.reshape(1, -1)
    return jnp.pad(v, ((0, 0), (0, n - v.shape[1])))


def _cls2(a, b):
    return jnp.stack([a, b]).reshape(2, 1, D)


def _win_p(g):
    return jnp.concatenate([g.reshape(IN_COLS, D), jnp.zeros((NP_IN - IN_COLS, D), g.dtype)], axis=0)


def _wfi(ga, gb):
    return jnp.stack([ga, gb], axis=1).reshape(2 * DFF, D)


def _layer_consts(p):
    sinkp = jnp.broadcast_to(p["wa_sink"][_PAIR_HEADS][:, :, None, None], (2, 2, 1, 128))
    return dict(
        sinkp=sinkp, nosink=jnp.full((2, 2, 1, 128), NEG, F32),
        w8=jnp.concatenate([p["ssm_conv_w"], jnp.zeros((1, 1024), F32)], axis=0),
        cb=p["ssm_conv_b"].reshape(1, 1024), dtb=_lanes(p["ssm_dt_bias"]), alog=_lanes(p["ssm_a_log"]),
        dsk=jnp.repeat(p["ssm_d"], HD).reshape(1, 512), gs=p["ssm_norm_g"].reshape(1, 512),
        gmix=p["g_mix"].reshape(1, D), gffn=p["g_ffn"].reshape(1, D))


def _mods(mod2):
    return [_cls2(mod2[0, D * k:D * (k + 1)], mod2[1, D * k:D * (k + 1)]) for k in range(6)]


def _layer_fwd(X, mod2, c, rpb, tabs, L, ctx_out, ffn_shards, nxt):
    cos, sin, rm, tri2 = tabs
    sh1, sc1, gt1, sh2, sc2, gt2 = _mods(mod2)
    biasd = _na_bias(rpb, L // GW)
    fi, fo = ffn_shards
    (qa, qb, z, ka, va, kb, vb, xbc, dt, h1), (gfo,) = in_fwd(X, c["gmix"], sh1, sc1, c["win"], cos, sin, rm, L,
                                                              sends=(fo,))
    (oa,), (gfa,) = wa_fwd(qa, ka, va, c["sinkp"], L, sends=(fi[:DFF // NDEV],))
    ob = na_fwd(qb, kb, vb, biasd, L)
    if ctx_out:
        oa_c = ctx_fwd(qa, ka, va, c["sinkp"], True, L)
        ob_c = ctx_fwd(qb, kb, vb, c["nosink"], False, L)
    else:
        oa_c = ob_c = jnp.zeros((LC, 256), F32)
    oa = jnp.concatenate([oa, oa_c], axis=0)
    ob = jnp.concatenate([ob, ob_c], axis=0)
    act = conv_fwd(xbc, c["w8"], c["cb"], L)
    (y2, hs), (gfb,) = ssd_fwd(act, dt, c["dtb"], c["alog"], tri2, L, sends=(fi[DFF // NDEV:],))
    X1, cat = out_fwd(oa, ob, y2, act, z, c["dsk"], c["gs"], c["wout"], X, gt1, L)
    c = dict(c, wfi=_wfi(gfa, gfb), wfo=gfo.reshape(DFF, D))
    res = ffn_fwd(X1, c["gffn"], sh2, sc2, gt2, c["wfi"], c["wfo"], L, sends=nxt)
    (X2,), got = res if nxt else ((res,), ())
    saved = dict(X=X, X1=X1, qa=qa, qb=qb, z=z, ka=ka, va=va, kb=kb, vb=vb, xbc=xbc, dt=dt, h1=h1, oa=oa, ob=ob,
                 act=act, y2=y2, hs=hs, cat=cat, biasd=biasd)
    return X2, saved, c, got


def _row_blocks(gw):
    return gw.reshape(NDEV, gw.shape[0] // NDEV, gw.shape[1])


def _layer_bwd(dX2, s, mod2, c, tabs, L, ctx_out, carry):
    cos, sin, rm, tri2 = tabs
    sh1, sc1, gt1, sh2, sc2, gt2 = _mods(mod2)
    R = L // GW
    res = ffn_bwd(s["X1"], c["gffn"], sh2, sc2, gt2, c["wfi"], c["wfo"], dX2, L, sends=carry)
    (dX1, h2, dU, actf, dOut, dgffn, dsh2, dsc2, dgt2), got = res if carry else (res, ())
    g = {}
    gfi = _row_blocks(tn_mm(dU, h2, 1408, 1024, MXU))
    gfo = _row_blocks(tn_mm(actf, dOut, 1408, 1024, MXU))
    doa, dob, dy, dxs_skip, dz, dmix, ddsk, dgs, dgt1 = out_bwd(s["oa"], s["ob"], s["y2"], s["act"], s["z"], c["dsk"],
                                                                c["gs"], c["wout"], gt1, dX1, L)
    gout = _row_blocks(tn_mm(s["cat"], dmix, 1024, 512, MXU))
    (dS, ddt2, ddtb, dal), (g["w_ffn_in"], g["w_ffn_out"]) = ssd_bwd(
        s["act"], s["dt"], c["dtb"], c["alog"], tri2, s["hs"], dy, L, sends=(gfi, gfo))
    dxbc, dw8, dcb = conv_bwd(s["xbc"], c["w8"], c["cb"], dS, dxs_skip, L)
    (dqa, dkpad, dvpad, dkxa, dvxa, dska), (g["w_out"],) = wa_bwd(s["qa"], s["ka"], s["va"], c["sinkp"], doa, L,
                                                                  sends=(gout,))
    dqb, dkb, dvb, dkxb, dvxb, dbias = na_bwd(s["qb"], s["kb"], s["vb"], s["biasd"], dob, L)
    if ctx_out:
        dqa_c, dk1, dv1, dsk1 = ctx_bwd(s["qa"], s["ka"], s["va"], c["sinkp"], doa, True, L)
        dqb_c, dk2, dv2, _ = ctx_bwd(s["qb"], s["kb"], s["vb"], c["nosink"], dob, False, L)
        dkxa, dvxa, dska = dkxa + dk1, dvxa + dv1, dska + dsk1
        dkxb, dvxb = dkxb + dk2, dvxb + dv2
    else:
        dqa_c = dqb_c = jnp.zeros((LC, 256), F32)
    cat0 = lambda a, b: jnp.concatenate([a, b], axis=0)
    dX, dycat, dgmix, dsh1, dsc1 = in_bwd(
        s["X"], c["gmix"], sh1, sc1, c["win"], cos, sin, rm, dX1, cat0(dqa, dqa_c), cat0(dqb, dqb_c), dz,
        cat0(dkpad[Q:L + Q], dkxa), cat0(dvpad[Q:L + Q], dvxa), cat0(dkb, dkxb), cat0(dvb, dvxb), dxbc, ddt2, L,
        latent_only=ctx_out)
    gin = _row_blocks(tn_mm(dycat, s["h1"], 1024, 1024, MXU)[:IN_COLS])
    g["g_mix"] = dgmix.reshape(D)
    g["g_ffn"] = dgffn.reshape(D)
    sk = jnp.sum(dska, axis=(2, 3))
    g["wa_sink"] = jnp.zeros((4,), F32).at[_PAIR_HEADS.reshape(-1)].set(sk.reshape(-1))
    g["na_rpb"] = _na_bias_grad(dbias, R)
    g["ssm_conv_w"] = dw8[:7]
    g["ssm_conv_b"] = dcb.reshape(1024)
    g["ssm_dt_bias"] = (ddtb[0] + ddtb[1])[0, :16].reshape(2, 8)
    g["ssm_a_log"] = (dal[0] + dal[1])[0, :16].reshape(2, 8)
    g["ssm_d"] = jnp.sum(ddsk.reshape(8, HD), axis=1)
    g["ssm_norm_g"] = dgs.reshape(512)
    dmod2 = jnp.concatenate([dsh1, dsc1, dgt1, dsh2, dsc2, dgt2], axis=2).reshape(2, 6 * D)
    return dX, g, dmod2, gin, got


def local_step(x, ctx, tgt, mods, layers, shards, g_final, L):
    tabs = _tables(L)
    X = (x, ctx)
    consts = [_layer_consts(p) for p in layers]
    saved = []
    got = all_gather([shards["w_in"][0], shards["w_out"][0]], "gather_first")
    for i in range(2):
        consts[i] = dict(consts[i], win=_win_p(got[0]), wout=got[1].reshape(D, D))
        nxt = (shards["w_in"][1], shards["w_out"][1]) if i == 0 else ()
        X, s, consts[i], got = _layer_fwd(X, mods[i], consts[i], layers[i]["na_rpb"], tabs, L, i == 0,
                                          (shards["w_ffn_in"][i], shards["w_ffn_out"][i]), nxt)
        saved.append(s)
    loss8, dX, dgfin = loss_head(X, g_final.reshape(1, D), tgt, L)
    grads, dmods = [None, None], [None, None]
    dX, grads[1], dmods[1], gin1, _ = _layer_bwd(dX, saved[1], mods[1], consts[1], tabs, L, False, ())
    dX, grads[0], dmods[0], gin0, (grads[1]["w_in"],) = _layer_bwd(dX, saved[0], mods[0], consts[0], tabs, L, True,
                                                                   (gin1,))
    grads[0]["w_in"], = all_to_all([gin0], "exchange_last")
    return loss8[0, 0], dX, grads, jnp.stack(dmods), dgfin.reshape(D)


def _place():
    x, y, c = lax.axis_index("x"), lax.axis_index("y"), lax.axis_index("c")
    return x, y, c


def _slot(b):
    return 4 * b[0] + 2 * b[1] + b[2]


def _any():
    return pl.BlockSpec(memory_space=pl.ANY)


def all_gather(xs, name):
    n = len(xs)

    def body(*refs):
        x_refs, o_refs = refs[:n], refs[n:2 * n]
        send_sems, recv_sems, local_sems = refs[2 * n:]
        x, y, c = _place()
        me, sib = (x, y, c), (x, y, 1 - c)
        chips = [(1 - x, y), (x, 1 - y), (1 - x, 1 - y)]

        def copy(t, k, blk, to, src=None):
            dst = o_refs[t].at[_slot(blk)]
            return pltpu.make_async_remote_copy(
                src_ref=dst if src is None else src, dst_ref=dst, send_sem=send_sems.at[7 * t + k],
                recv_sem=recv_sems.at[7 * t + k], device_id=to, device_id_type=MESH_T)

        mine = [pltpu.make_async_copy(x_refs[t], o_refs[t].at[_slot(me)], local_sems.at[t]) for t in range(n)]
        for cp in mine:
            cp.start()
        first = []
        for t in range(n):
            first.append(copy(t, 0, me, sib, src=x_refs[t]))
            first += [copy(t, 1 + j, me, (*chip, c), src=x_refs[t]) for j, chip in enumerate(chips)]
        for cp in first:
            cp.start()
        passed = []
        for j, chip in enumerate(chips):
            for t in range(n):
                copy(t, 1 + j, (*chip, c), me).wait_recv()
                cp = copy(t, 4 + j, (*chip, c), sib)
                cp.start()
                passed.append(cp)
        for t in range(n):
            copy(t, 0, sib, me).wait_recv()
            for j, chip in enumerate(chips):
                copy(t, 4 + j, (*chip, 1 - c), me).wait_recv()
        for cp in first + passed:
            cp.wait_send()
        for cp in mine:
            cp.wait()

    return pl.pallas_call(
        body, name=name, out_shape=[_sds((NDEV,) + a.shape, a.dtype) for a in xs],
        in_specs=[_any()] * n, out_specs=[_any()] * n,
        scratch_shapes=[pltpu.SemaphoreType.DMA((7 * n,)), pltpu.SemaphoreType.DMA((7 * n,)),
                        pltpu.SemaphoreType.DMA((n,))],
        interpret=_INTERPRET)(*xs)


def all_to_all(xs, name):
    n = len(xs)

    def body(*refs):
        _a2a_start(refs[:n], refs[n:2 * n], *refs[2 * n:])
        _a2a_wait(refs[:n], refs[n:2 * n], *refs[2 * n:])

    return pl.pallas_call(
        body, name=name, out_shape=[_sds(a.shape, a.dtype) for a in xs],
        in_specs=[_any()] * n, out_specs=[_any()] * n, scratch_shapes=_a2a_sems(n), interpret=_INTERPRET)(*xs)


def _a2a_sems(n):
    return [pltpu.SemaphoreType.DMA((7 * n,)), pltpu.SemaphoreType.DMA((7 * n,)), pltpu.SemaphoreType.DMA((n,))]


def _a2a_copies(x_refs, o_refs, send_sems, recv_sems, local_sems):
    n = len(x_refs)
    x, y, c = _place()
    me = (x, y, c)
    flip = lambda v, b: (1 - v) if b else v
    peers = [(flip(x, k >> 2 & 1), flip(y, k >> 1 & 1), flip(c, k & 1)) for k in range(1, NDEV)]
    mine = [pltpu.make_async_copy(x_refs[t].at[_slot(me)], o_refs[t].at[_slot(me)], local_sems.at[t])
            for t in range(n)]

    def copy(t, k, src_slot, dst_slot, to):
        return pltpu.make_async_remote_copy(
            src_ref=x_refs[t].at[src_slot], dst_ref=o_refs[t].at[dst_slot], send_sem=send_sems.at[7 * t + k],
            recv_sem=recv_sems.at[7 * t + k], device_id=to, device_id_type=MESH_T)

    sends = [copy(t, k, _slot(p), _slot(me), p) for t in range(n) for k, p in enumerate(peers)]
    recvs = [copy(t, k, _slot(p), _slot(p), me) for t in range(n) for k, p in enumerate(peers)]
    return mine, sends, recvs


def _ag_copies(x_refs, o_refs, send_sems, recv_sems, local_sems):
    n = len(x_refs)
    x, y, c = _place()
    me = (x, y, c)
    flip = lambda v, b: (1 - v) if b else v
    peers = [(flip(x, k >> 2 & 1), flip(y, k >> 1 & 1), flip(c, k & 1)) for k in range(1, NDEV)]
    mine = [pltpu.make_async_copy(x_refs[t], o_refs[t].at[_slot(me)], local_sems.at[t]) for t in range(n)]

    def copy(t, k, dst_slot, to):
        return pltpu.make_async_remote_copy(
            src_ref=x_refs[t], dst_ref=o_refs[t].at[dst_slot], send_sem=send_sems.at[7 * t + k],
            recv_sem=recv_sems.at[7 * t + k], device_id=to, device_id_type=MESH_T)

    sends = [copy(t, k, _slot(me), p) for t in range(n) for k, p in enumerate(peers)]
    recvs = [copy(t, k, _slot(p), me) for t in range(n) for k, p in enumerate(peers)]
    return mine, sends, recvs


def _ag_start(x_refs, o_refs, send_sems, recv_sems, local_sems):
    mine, sends, _ = _ag_copies(x_refs, o_refs, send_sems, recv_sems, local_sems)
    for cp in mine + sends:
        cp.start()


def _ag_wait(x_refs, o_refs, send_sems, recv_sems, local_sems):
    mine, sends, recvs = _ag_copies(x_refs, o_refs, send_sems, recv_sems, local_sems)
    for cp in recvs:
        cp.wait_recv()
    for cp in sends:
        cp.wait_send()
    for cp in mine:
        cp.wait()


def _a2a_start(x_refs, o_refs, send_sems, recv_sems, local_sems):
    mine, sends, _ = _a2a_copies(x_refs, o_refs, send_sems, recv_sems, local_sems)
    for cp in mine + sends:
        cp.start()


def _a2a_wait(x_refs, o_refs, send_sems, recv_sems, local_sems):
    mine, sends, recvs = _a2a_copies(x_refs, o_refs, send_sems, recv_sems, local_sems)
    for cp in recvs:
        cp.wait_recv()
    for cp in sends:
        cp.wait_send()
    for cp in mine:
        cp.wait()


def adam_reduce(P, w, m, v, name):
    n, R, C = P.shape
    br = R // 4 if R % 64 == 0 else R

    def body(p_ref, w_ref, m_ref, v_ref, g_o, d_o, m_o, v_o):
        g = p_ref[0].astype(F32)
        for k in range(1, n):
            g = g + p_ref[k].astype(F32)
        m1 = ADAM_B1 * m_ref[...] + (1.0 - ADAM_B1) * g
        v1 = ADAM_B2 * v_ref[...] + (1.0 - ADAM_B2) * jnp.square(g)
        m_hat = m1 / (1.0 - ADAM_B1 ** ADAM_STEP)
        v_hat = v1 / (1.0 - ADAM_B2 ** ADAM_STEP)
        g_o[...] = g
        d_o[...] = -ADAM_LR * (m_hat / (jnp.sqrt(v_hat) + ADAM_EPS) + ADAM_WD * w_ref[...])
        m_o[...] = m1
        v_o[...] = v1

    blk = pl.BlockSpec((br, C), lambda i: (i, 0))
    return _pc(body, name, [_sds((R, C))] * 4, grid=(R // br,),
               in_specs=[pl.BlockSpec((n, br, C), lambda i: (0, i, 0)), blk, blk, blk], out_specs=[blk] * 4)(P, w, m, v)


def adam_layers(P0, P1, w, m, v, name):
    n, R, C = P0.shape
    br = R // 4 if R % 64 == 0 else R
    nb = R // br

    def body(p0_ref, p1_ref, w_ref, m_ref, v_ref, g_o, d_o, m_o, v_o):
        def total(p_ref):
            g = p_ref[0].astype(F32)
            for k in range(1, n):
                g = g + p_ref[k].astype(F32)
            return g

        g = jnp.where(pl.program_id(0) == 0, total(p0_ref), total(p1_ref))
        m1 = ADAM_B1 * m_ref[0] + (1.0 - ADAM_B1) * g
        v1 = ADAM_B2 * v_ref[0] + (1.0 - ADAM_B2) * jnp.square(g)
        m_hat = m1 / (1.0 - ADAM_B1 ** ADAM_STEP)
        v_hat = v1 / (1.0 - ADAM_B2 ** ADAM_STEP)
        g_o[0] = g
        d_o[0] = -ADAM_LR * (m_hat / (jnp.sqrt(v_hat) + ADAM_EPS) + ADAM_WD * w_ref[0])
        m_o[0] = m1
        v_o[0] = v1

    blk = pl.BlockSpec((1, br, C), lambda l, i: (l, i, 0))
    p0 = pl.BlockSpec((n, br, C), lambda l, i: (0, jnp.where(l == 0, i, nb - 1), 0))
    p1 = pl.BlockSpec((n, br, C), lambda l, i: (0, jnp.where(l == 1, i, 0), 0))
    return _pc(body, name, [_sds((2, R, C))] * 4, grid=(2, nb), in_specs=[p0, p1, blk, blk, blk],
               out_specs=[blk] * 4)(P0, P1, w, m, v)


def mod_fwd(scin, wmod, bcol):
    def body(s_ref, w_ref, b_ref, o_ref):
        o_ref[0] = mm(_silu(s_ref[...]), w_ref[0]) + b_ref[0]

    return _pc(body, "mod_fwd", _sds((2, 16, 768)), grid=(2,),
               in_specs=[pl.BlockSpec((16, D), lambda l: (0, 0)), pl.BlockSpec((1, D, 768), lambda l: (l, 0, 0)),
                         pl.BlockSpec((1, 1, 768), lambda l: (l, 0, 0))],
               out_specs=pl.BlockSpec((1, 16, 768), lambda l: (l, 0, 0)))(scin, wmod, bcol)


def mod_bwd(scin, wmod, G):
    def body(s_ref, w_ref, g_ref, dw_o, ds_o):
        _, vjp = jax.vjp(lambda s, w: mm(_silu(s), w), s_ref[...], w_ref[0])
        ds, dw = vjp(g_ref[0])
        dw_o[0] = dw
        _acc_init(pl.program_id(0) == 0, [ds_o])
        ds_o[...] += ds

    full = pl.BlockSpec((16, D), lambda l: (0, 0))
    wsp = pl.BlockSpec((1, D, 768), lambda l: (l, 0, 0))
    return _pc(body, "mod_bwd", [_sds((2, D, 768)), _sds((16, D))], grid=(2,),
               in_specs=[full, wsp, pl.BlockSpec((1, 16, 768), lambda l: (l, 0, 0))], out_specs=[wsp, full])(
        scin, wmod, G)


_SMALL = ["b_mod", "g_mix", "wa_sink", "na_rpb", "ssm_conv_w", "ssm_conv_b", "ssm_dt_bias", "ssm_a_log", "ssm_d",
          "ssm_norm_g", "g_ffn", "g_final", "dmod_s", "dmod_c"]


def _pack(parts):
    rows = []
    for a in parts:
        f = a.reshape(-1).astype(F32)
        rows.append(jnp.pad(f, (0, (-f.shape[0]) % 1024)).reshape(-1, 128))
    return jnp.concatenate(rows, axis=0)


def _unpack(packed, shapes):
    out, r = [], 0
    for s in shapes:
        nel = int(np.prod(s))
        nr = -(-nel // 1024) * 8
        out.append(packed[r:r + nr].reshape(-1)[:nel].reshape(s))
        r += nr
    return out


def kernel(x, c, ctx, c_ctx, w_mod, b_mod, g_mix, w_in, wa_sink, na_rpb, ssm_conv_w, ssm_conv_b, ssm_dt_bias, ssm_a_log, ssm_d, ssm_norm_g, w_out, g_ffn, w_ffn_in, w_ffn_out, g_final, loss_target, m_c_ctx, m_w_mod, m_b_mod, m_g_mix, m_w_in, m_wa_sink, m_na_rpb, m_ssm_conv_w, m_ssm_conv_b, m_ssm_dt_bias, m_ssm_a_log, m_ssm_d, m_ssm_norm_g, m_w_out, m_g_ffn, m_w_ffn_in, m_w_ffn_out, m_g_final, v_c_ctx, v_w_mod, v_b_mod, v_g_mix, v_w_in, v_wa_sink, v_na_rpb, v_ssm_conv_w, v_ssm_conv_b, v_ssm_dt_bias, v_ssm_a_log, v_ssm_d, v_ssm_norm_g, v_w_out, v_g_ffn, v_w_ffn_in, v_w_ffn_out, v_g_final):
    L = x.shape[1]
    px, py, pc = _place()
    me = 4 * px + 2 * py + pc
    W = dict(c_ctx=c_ctx, w_mod=w_mod, b_mod=b_mod, g_mix=g_mix, w_in=w_in, wa_sink=wa_sink, na_rpb=na_rpb,
             ssm_conv_w=ssm_conv_w, ssm_conv_b=ssm_conv_b, ssm_dt_bias=ssm_dt_bias, ssm_a_log=ssm_a_log, ssm_d=ssm_d,
             ssm_norm_g=ssm_norm_g, w_out=w_out, g_ffn=g_ffn, w_ffn_in=w_ffn_in, w_ffn_out=w_ffn_out, g_final=g_final)
    M = dict(c_ctx=m_c_ctx, w_mod=m_w_mod, b_mod=m_b_mod, g_mix=m_g_mix, w_in=m_w_in, wa_sink=m_wa_sink,
             na_rpb=m_na_rpb, ssm_conv_w=m_ssm_conv_w, ssm_conv_b=m_ssm_conv_b, ssm_dt_bias=m_ssm_dt_bias,
             ssm_a_log=m_ssm_a_log, ssm_d=m_ssm_d, ssm_norm_g=m_ssm_norm_g, w_out=m_w_out, g_ffn=m_g_ffn,
             w_ffn_in=m_w_ffn_in, w_ffn_out=m_w_ffn_out, g_final=m_g_final)
    V = dict(c_ctx=v_c_ctx, w_mod=v_w_mod, b_mod=v_b_mod, g_mix=v_g_mix, w_in=v_w_in, wa_sink=v_wa_sink,
             na_rpb=v_na_rpb, ssm_conv_w=v_ssm_conv_w, ssm_conv_b=v_ssm_conv_b, ssm_dt_bias=v_ssm_dt_bias,
             ssm_a_log=v_ssm_a_log, ssm_d=v_ssm_d, ssm_norm_g=v_ssm_norm_g, w_out=v_w_out, g_ffn=v_g_ffn,
             w_ffn_in=v_w_ffn_in, w_ffn_out=v_w_ffn_out, g_final=v_g_final)

    c_all, conv_all = all_gather([c, ssm_conv_w], "gather_small")
    tr = lambda a: a.transpose(0, 2, 1)
    shards = dict(w_in=tr(w_in).astype(MXU), w_out=w_out.astype(MXU), w_ffn_in=tr(w_ffn_in).astype(MXU),
                  w_ffn_out=w_ffn_out.astype(MXU))
    conv_f = conv_all.transpose(1, 2, 0, 3).reshape(2, 7, 1024)

    scin = jnp.concatenate([c_all.reshape(NDEV, D), c_ctx.reshape(1, D), jnp.zeros((7, D), F32)], axis=0)
    bcol = lax.dynamic_slice_in_dim(b_mod, me * 768, 768, axis=1).reshape(2, 1, 768)
    mod_all, = all_gather([mod_fwd(scin, w_mod, bcol)], "gather_mod")
    mod_rows = mod_all.transpose(1, 2, 0, 3).reshape(2, 16, 6 * D)
    mods = jnp.stack([lax.dynamic_index_in_dim(mod_rows, me, axis=1, keepdims=False), mod_rows[:, 8]], axis=1)

    layers = [dict(g_mix=g_mix[i], wa_sink=wa_sink[i], na_rpb=na_rpb[i], ssm_conv_w=conv_f[i],
                   ssm_conv_b=ssm_conv_b[i], ssm_dt_bias=ssm_dt_bias[i], ssm_a_log=ssm_a_log[i], ssm_d=ssm_d[i],
                   ssm_norm_g=ssm_norm_g[i], g_ffn=g_ffn[i]) for i in range(2)]
    loss, dx, grads, dmods, dgfin = local_step(x[0], ctx[0], loss_target[0], mods, layers, shards, g_final, L)
    loss = lax.psum(loss, ("x", "y", "c"))

    stk = lambda n: jnp.stack([grads[0][n], grads[1][n]])
    small = dict(b_mod=dmods[:, 0] + dmods[:, 1], g_final=dgfin, dmod_s=dmods[:, 0], dmod_c=dmods[:, 1])
    for nme in _SMALL:
        if nme not in small:
            small[nme] = stk(nme)
    shapes = [small[nme].shape for nme in _SMALL]
    zero_like = lambda nme: jnp.zeros(small[nme].shape, F32)
    own = lambda S, nme: S[nme] if (nme in S and S[nme].shape == small[nme].shape) else zero_like(nme)
    gath, = all_gather([_pack([small[nme] for nme in _SMALL])], "gather_grads")
    sm = adam_reduce(gath, _pack([own(W, nme) for nme in _SMALL]), _pack([own(M, nme) for nme in _SMALL]),
                     _pack([own(V, nme) for nme in _SMALL]), "adam_small")
    res = {nme: vals for nme, vals in zip(_SMALL, zip(*[_unpack(a, shapes) for a in sm]))}

    cols = lambda a: lax.dynamic_slice_in_dim(a, me * 768, 768, axis=-1)
    gparts = [_unpack(gath[d], shapes) for d in range(NDEV)]
    dmod_s_all = jnp.stack([gparts[d][_SMALL.index("dmod_s")] for d in range(NDEV)], axis=1)
    G = jnp.concatenate([cols(dmod_s_all), cols(res["dmod_c"][0])[:, None, :], jnp.zeros((2, 7, 768), F32)], axis=1)
    dwmod, dscin = mod_bwd(scin, w_mod, G)
    cc_g, = all_gather([dscin[8].reshape(8, 128)], "gather_cctx")
    out = {}
    out["c_ctx"] = [a.reshape(D) for a in adam_reduce(cc_g, c_ctx.reshape(8, 128), m_c_ctx.reshape(8, 128),
                                                      v_c_ctx.reshape(8, 128), "adam_cctx")]
    out["w_mod"] = [a.reshape(2, D, 768) for a in adam_reduce(
        dwmod.reshape(1, 2 * D, 768), w_mod.reshape(2 * D, 768), m_w_mod.reshape(2 * D, 768),
        v_w_mod.reshape(2 * D, 768), "adam_wmod")]
    gconv = lax.dynamic_slice_in_dim(res["ssm_conv_w"][0], me * 128, 128, axis=2)
    out["ssm_conv_w"] = [a.reshape(2, 7, 128) for a in adam_reduce(
        gconv.reshape(1, 14, 128), ssm_conv_w.reshape(14, 128), m_ssm_conv_w.reshape(14, 128),
        v_ssm_conv_w.reshape(14, 128), "adam_conv")]
    for nme in _SMALL:
        if nme not in ("ssm_conv_w", "dmod_s", "dmod_c"):
            out[nme] = list(res[nme])

    for nme in ("w_out", "w_ffn_out"):
        out[nme] = list(adam_layers(grads[0][nme], grads[1][nme], W[nme], M[nme], V[nme], "adam_" + nme))
    for nme in ("w_in", "w_ffn_in"):
        out[nme] = [tr(a) for a in adam_layers(grads[0][nme], grads[1][nme], tr(W[nme]), tr(M[nme]), tr(V[nme]),
                                               "adam_" + nme)]
    order = ["c_ctx", "w_mod", "b_mod", "g_mix", "w_in", "wa_sink", "na_rpb", "ssm_conv_w", "ssm_conv_b",
             "ssm_dt_bias", "ssm_a_log", "ssm_d", "ssm_norm_g", "w_out", "g_ffn", "w_ffn_in", "w_ffn_out", "g_final"]
    return (loss, dx.reshape(1, L, D), *[out[nme][0] for nme in order], *[out[nme][1] for nme in order],
            *[out[nme][2] for nme in order], *[out[nme][3] for nme in order])
```

```python
import functools
import math

import numpy as np
import jax
import jax.numpy as jnp
from jax import lax
from jax.experimental import pallas as pl
from jax.experimental.pallas import tpu as pltpu

F32 = jnp.float32
MXU = jnp.bfloat16
_INTERPRET = False
VMEM_LIMIT = 60 * 1024 * 1024

D = 1024
LC = 256
GW = 64
HD = 64
EPS = 1e-6
NEG = -1e30
NDEV = 8
Q = 128
NSTATE = 128
DFF = 2816
IN_COLS = 2832
NP_IN = 3072
C_QA, C_QB, C_Z, C_KA, C_VA, C_KB, C_VB, C_XBC, C_DT = 0, 256, 512, 1024, 1152, 1280, 1536, 1792, 2816
ADAM_LR, ADAM_B1, ADAM_B2, ADAM_EPS, ADAM_WD, ADAM_STEP = 0.001, 0.9, 0.999, 1e-08, 0.01, 10
MESH_T = pl.DeviceIdType.MESH


def _dg(a, b, ca, cb):
    return lax.dot_general(a.astype(MXU), b.astype(MXU), (((ca,), (cb,)), ((), ())), preferred_element_type=F32)


@jax.custom_vjp
def mm(a, b):
    return _dg(a, b, 1, 0)


def _mm_f(a, b):
    return _dg(a, b, 1, 0), (a, b)


def _mm_b(res, g):
    a, b = res
    return _dg(g, b, 1, 1).astype(a.dtype), _dg(a, g, 0, 0).astype(b.dtype)


mm.defvjp(_mm_f, _mm_b)


@jax.custom_vjp
def mm_nt(a, b):
    return _dg(a, b, 1, 1)


def _mmnt_f(a, b):
    return _dg(a, b, 1, 1), (a, b)


def _mmnt_b(res, g):
    a, b = res
    return _dg(g, b, 1, 0).astype(a.dtype), _dg(g, a, 0, 0).astype(b.dtype)


mm_nt.defvjp(_mmnt_f, _mmnt_b)


@jax.custom_vjp
def mm_tn(a, b):
    return _dg(a, b, 0, 0)


def _mmtn_f(a, b):
    return _dg(a, b, 0, 0), (a, b)


def _mmtn_b(res, g):
    a, b = res
    return _dg(b, g, 1, 1).astype(a.dtype), _dg(a, g, 1, 0).astype(b.dtype)


mm_tn.defvjp(_mmtn_f, _mmtn_b)


@jax.custom_vjp
def mmw(a, w):
    return _dg(a, w, 1, 0)


mmw.defvjp(lambda a, w: (_dg(a, w, 1, 0), w), lambda w, g: (_dg(g, w, 1, 1), None))


@jax.custom_vjp
def mmw_nt(a, w):
    return _dg(a, w, 1, 1)


mmw_nt.defvjp(lambda a, w: (_dg(a, w, 1, 1), w), lambda w, g: (_dg(g, w, 1, 0), None))


def _exact(a, b):
    return lax.dot_general(a, b, (((1,), (0,)), ((), ())), precision=lax.Precision.HIGHEST,
                           preferred_element_type=F32)


def _pc(body, name, out_shape, grid=None, in_specs=None, out_specs=None, scratch=(), sends=(), gather=False):
    params = pltpu.CompilerParams(vmem_limit_bytes=VMEM_LIMIT)
    if sends and not isinstance(out_shape, (list, tuple)):
        out_shape, out_specs = [out_shape], [out_specs]
    start, wait = (_ag_start, _ag_wait) if gather else (_a2a_start, _a2a_wait)
    if not sends:
        kw = {}
        if grid is not None:
            kw = dict(grid=grid, in_specs=in_specs, out_specs=out_specs)
        elif in_specs is not None:
            kw = dict(in_specs=in_specs, out_specs=out_specs)
        return pl.pallas_call(body, name=name, out_shape=out_shape, scratch_shapes=list(scratch),
                              compiler_params=params, interpret=_INTERPRET, **kw)
    n, nin, nout, nscr = len(sends), len(in_specs), len(out_shape), len(scratch)

    def body2(*refs):
        cin, xs = refs[:nin], refs[nin:nin + n]
        couts, os_ = refs[nin + n:nin + n + nout], refs[nin + n + nout:nin + 2 * n + nout]
        cscr, sems = refs[nin + 2 * n + nout:nin + 2 * n + nout + nscr], refs[nin + 2 * n + nout + nscr:]
        ids = [pl.program_id(a) for a in range(len(grid))]
        first = functools.reduce(lambda a, b: a & b, [i == 0 for i in ids])
        last = functools.reduce(lambda a, b: a & b, [i == g - 1 for i, g in zip(ids, grid)])

        @pl.when(first)
        def _():
            start(xs, os_, *sems)

        body(*cin, *couts, *cscr)

        @pl.when(last)
        def _():
            wait(xs, os_, *sems)

    call = pl.pallas_call(
        body2, name=name,
        out_shape=list(out_shape) + [_sds(((NDEV,) if gather else ()) + a.shape, a.dtype) for a in sends],
        grid=grid, in_specs=list(in_specs) + [_any()] * n, out_specs=list(out_specs) + [_any()] * n,
        scratch_shapes=list(scratch) + _a2a_sems(n), compiler_params=params, interpret=_INTERPRET)

    def run(*args):
        res = call(*args, *sends)
        return res[:nout], res[nout:]

    return run


def _vm():
    return pl.BlockSpec(memory_space=pltpu.VMEM)


def _sds(shape, dt=F32):
    return jax.ShapeDtypeStruct(shape, dt)


def _iota(shape, dim):
    return lax.broadcasted_iota(jnp.int32, shape, dim)


def _silu(x):
    return x * jax.nn.sigmoid(x)


def _softplus(x):
    return jnp.maximum(x, 0.0) + jnp.log1p(jnp.exp(-jnp.abs(x)))


def _normmod(x, g, sh, sc):
    r = lax.rsqrt(jnp.mean(x * x, axis=-1, keepdims=True) + EPS)
    return (x * r * g) * (1.0 + sc) + sh


def _rope(x, cos, sin, rm):
    return x * cos + _exact(x, rm) * sin


def _swap12(x):
    lane = _iota(x.shape, 1)
    up, down = pltpu.roll(x, 192, 1), pltpu.roll(x, 64, 1)
    return jnp.where((lane >= 64) & (lane < 128), up, jnp.where((lane >= 128) & (lane < 192), down, x))


def _acc_init(first, refs):
    @pl.when(first)
    def _():
        for r in refs:
            r[...] = jnp.zeros_like(r)


def _stream(X, TR, nlt):
    if not isinstance(X, tuple):
        return (X,), [pl.BlockSpec((TR, D), lambda i: (i, 0))], lambda refs: refs[0][...]
    specs = [pl.BlockSpec((TR, D), lambda i: (jnp.minimum(i, nlt - 1), 0)), pl.BlockSpec((TR, D), lambda i: (0, 0))]
    return X, specs, lambda refs: jnp.where(pl.program_id(0) < nlt, refs[0][...], refs[1][...])


def in_fwd(X, g, sh, sc, W, cos, sin, rm, L, sends=()):
    T = L + LC
    TR = 256
    nlt = L // TR
    xs, xspecs, xread = _stream(X, TR, nlt)

    def body(*refs):
        (g_ref, sh_ref, sc_ref, w_ref, cos_ref, sin_ref, rm_ref,
         qa, qb, z, ka, va, kb, vb, xbc, dt, hout) = refs[len(xs):]
        h = _normmod(xread(refs), g_ref[...], sh_ref[0], sc_ref[0]).astype(MXU)
        hout[...] = h
        y = lax.dot_general(h, w_ref[...], (((1,), (1,)), ((), ())), preferred_element_type=F32)
        cs, sn, r = cos_ref[...], sin_ref[...], rm_ref[...]
        qa[...] = _rope(_swap12(y[:, C_QA:C_QB]), cs, sn, r).astype(MXU)
        qb[...] = y[:, C_QB:C_Z].astype(MXU)
        z[...] = y[:, C_Z:C_KA]
        ka[...] = _rope(y[:, C_KA:C_VA], cs[:, :128], sn[:, :128], r[:128, :128]).astype(MXU)
        va[...] = y[:, C_VA:C_KB].astype(MXU)
        kb[...] = y[:, C_KB:C_VB].astype(MXU)
        vb[...] = y[:, C_VB:C_XBC].astype(MXU)
        xbc[...] = y[:, C_XBC:C_DT]
        dt[...] = y[:, C_DT:C_DT + 128]

    row = lambda w: pl.BlockSpec((TR, w), lambda i: (i, 0))
    cls = pl.BlockSpec((1, 1, D), lambda i: (i // nlt, 0, 0))
    widths = [(256, MXU), (256, MXU), (512, F32), (128, MXU), (128, MXU), (256, MXU), (256, MXU), (1024, F32),
              (128, F32), (D, MXU)]
    return _pc(body, "in_fwd", [_sds((T, w), d) for w, d in widths], grid=(T // TR,),
               in_specs=xspecs + [pl.BlockSpec((1, D), lambda i: (0, 0)), cls, cls, _vm(), row(256), row(256), _vm()],
               out_specs=[row(w) for w, _ in widths], sends=sends, gather=True)(*xs, g, sh, sc, W, cos, sin, rm)


def in_bwd(X, g, sh, sc, W, cos, sin, rm, dxres, dqa, dqb, dz, dka, dva, dkb, dvb, dxbc, ddt2, L, latent_only):
    T = L + LC
    TR = 256
    nlt = L // TR
    xs, xspecs, xread = _stream(X, TR, nlt)

    def body(*refs):
        (g_ref, sh_ref, sc_ref, w_ref, cos_ref, sin_ref, rm_ref, dxres_ref, dqa_r, dqb_r, dz_r, dka_r,
         dva_r, dkb_r, dvb_r, dxbc_r, ddt0_r, ddt1_r, dx_o, dy_o, dg_o, dsh_o, dsc_o) = refs[len(xs):]
        i = pl.program_id(0)
        cs, sn, r = cos_ref[...], sin_ref[...], rm_ref[...]
        _, vq = jax.vjp(lambda t: _rope(t, cs, sn, r), dqa_r[...])
        _, vk = jax.vjp(lambda t: _rope(t, cs[:, :128], sn[:, :128], r[:128, :128]), dka_r[...])
        dyqa = _swap12(vq(dqa_r[...])[0])
        dyka, = vk(dka_r[...])
        ddt = ddt0_r[0] + ddt1_r[0]
        dy = jnp.concatenate([dyqa, dqb_r[...], dz_r[...], dyka, dva_r[...], dkb_r[...], dvb_r[...], dxbc_r[...],
                              ddt, jnp.zeros((TR, NP_IN - C_DT - 128), F32)], axis=1).astype(MXU)
        dy_o[...] = dy
        dh = jnp.dot(dy, w_ref[...], preferred_element_type=F32)
        _, vp = jax.vjp(_normmod, xread(refs), g_ref[...], sh_ref[0], sc_ref[0])
        dx, dg, dsh, dsc = vp(dh)
        if latent_only:
            @pl.when(i < nlt)
            def _():
                dx_o[...] = dx + dxres_ref[...]
        else:
            dx_o[...] = dx + dxres_ref[...]
        _acc_init(i == 0, [dg_o])
        _acc_init((i == 0) | (i == nlt), [dsh_o, dsc_o])
        dg_o[...] += dg
        dsh_o[0] += dsh
        dsc_o[0] += dsc

    row = lambda w: pl.BlockSpec((TR, w), lambda i: (i, 0))
    cls = pl.BlockSpec((1, 1, D), lambda i: (i // nlt, 0, 0))
    vec = pl.BlockSpec((1, D), lambda i: (0, 0))
    dts = lambda d: pl.BlockSpec((1, TR, 128), lambda i: (d, i, 0))
    dxs = pl.BlockSpec((TR, D), lambda i: (jnp.minimum(i, nlt - 1), 0)) if latent_only else row(D)
    return _pc(body, "in_bwd",
               [_sds((L if latent_only else T, D)), _sds((T, NP_IN), MXU), _sds((1, D)), _sds((2, 1, D)),
                _sds((2, 1, D))],
               grid=(T // TR,),
               in_specs=xspecs + [vec, cls, cls, _vm(), row(256), row(256), _vm(), row(D), row(256), row(256),
                                  row(512), row(128), row(128), row(256), row(256), row(1024), dts(0), dts(1)],
               out_specs=[dxs, row(NP_IN), vec, cls, cls])(
        *xs, g, sh, sc, W, cos, sin, rm, dxres, dqa, dqb, dz, dka, dva, dkb, dvb, dxbc, ddt2, ddt2)


def tn_mm(A, G, bk, bn, out_dtype):
    T, K = A.shape
    N = G.shape[1]
    bt = T
    nt = T // bt

    def body(a_ref, g_ref, o_ref, acc):
        t = pl.program_id(2)
        _acc_init(t == 0, [acc])
        acc[...] += lax.dot_general(a_ref[...], g_ref[...], (((0,), (0,)), ((), ())), preferred_element_type=F32)

        @pl.when(t == nt - 1)
        def _():
            o_ref[...] = acc[...].astype(out_dtype)

    return _pc(body, "tn_mm", _sds((K, N), out_dtype), grid=(K // bk, N // bn, nt),
               in_specs=[pl.BlockSpec((bt, bk), lambda k, n, t: (t, k)), pl.BlockSpec((bt, bn), lambda k, n, t: (t, n))],
               out_specs=pl.BlockSpec((bk, bn), lambda k, n, t: (k, n)),
               scratch=[pltpu.VMEM((bk, bn), F32)])(A, G)


def _ssm_out(yf, yb, xs, z, dsk, gs):
    y = (yf + yb + dsk * xs) * _silu(z)
    r = lax.rsqrt(jnp.mean(y * y, axis=-1, keepdims=True) + EPS)
    return y * r * gs


def out_fwd(oa, ob, y2, act, z, dsk, gs, W, X, gate, L):
    T = L + LC
    TR = 256
    nlt = L // TR
    xs, xspecs, xread = _stream(X, TR, nlt)

    def body(*refs):
        oa_r, ob_r, yf_r, yb_r, xs_r, z_r, dsk_r, gs_r, w_ref, gt_ref, x1_o, cat_o = refs[len(xs):]
        oc = _ssm_out(yf_r[0], yb_r[0], xs_r[...], z_r[...], dsk_r[...], gs_r[...])
        cat = jnp.concatenate([_swap12(oa_r[...]), ob_r[...], oc], axis=1).astype(MXU)
        cat_o[...] = cat
        x1_o[...] = xread(refs) + gt_ref[0] * jnp.dot(cat, w_ref[...], preferred_element_type=F32)

    row = lambda w: pl.BlockSpec((TR, w), lambda i: (i, 0))
    ys = lambda d: pl.BlockSpec((1, TR, 512), lambda i: (d, i, 0))
    cls = pl.BlockSpec((1, 1, D), lambda i: (i // nlt, 0, 0))
    v512 = pl.BlockSpec((1, 512), lambda i: (0, 0))
    return _pc(body, "out_fwd", [_sds((T, D)), _sds((T, D), MXU)], grid=(T // TR,),
               in_specs=xspecs + [row(256), row(256), ys(0), ys(1), row(512), row(512), v512, v512, _vm(), cls],
               out_specs=[row(D), row(D)])(*xs, oa, ob, y2, y2, act, z, dsk, gs, W, gate)


def out_bwd(oa, ob, y2, act, z, dsk, gs, W, gate, dX1, L):
    T = dX1.shape[0]
    TR = 256
    nlt = L // TR

    def body(oa_r, ob_r, yf_r, yb_r, xs_r, z_r, dsk_r, gs_r, w_ref, gt_ref, dx1_r,
             doa_o, dob_o, dy_o, dxs_o, dz_o, dmix_o, ddsk_o, dgs_o, dgt_o):
        i = pl.program_id(0)
        w = w_ref[...]

        def f(oa_, ob_, yf, yb, xs, z_, dsk_, gs_, gt):
            oc = _ssm_out(yf, yb, xs, z_, dsk_, gs_)
            return gt * mmw(jnp.concatenate([oa_, ob_, oc], axis=1), w)

        _, vjp = jax.vjp(f, _swap12(oa_r[...]), ob_r[...], yf_r[0], yb_r[0], xs_r[...], z_r[...], dsk_r[...],
                         gs_r[...], gt_ref[0])
        dx1 = dx1_r[...]
        doa, dob, dyf, _, dxs, dz, ddsk, dgs, dgt = vjp(dx1)
        doa_o[...] = _swap12(doa)
        dob_o[...] = dob
        dy_o[...] = dyf
        dxs_o[...] = dxs
        dz_o[...] = dz
        dmix_o[...] = (gt_ref[0] * dx1).astype(MXU)
        _acc_init(i == 0, [ddsk_o, dgs_o])
        _acc_init((i == 0) | (i == nlt), [dgt_o])
        ddsk_o[...] += ddsk
        dgs_o[...] += dgs
        dgt_o[0] += dgt

    row = lambda w: pl.BlockSpec((TR, w), lambda i: (i, 0))
    ys = lambda d: pl.BlockSpec((1, TR, 512), lambda i: (d, i, 0))
    cls = pl.BlockSpec((1, 1, D), lambda i: (i // nlt, 0, 0))
    v512 = pl.BlockSpec((1, 512), lambda i: (0, 0))
    return _pc(body, "out_bwd",
               [_sds((T, 256)), _sds((T, 256)), _sds((T, 512)), _sds((T, 512)), _sds((T, 512)), _sds((T, D), MXU),
                _sds((1, 512)), _sds((1, 512)), _sds((2, 1, D))],
               grid=(T // TR,),
               in_specs=[row(256), row(256), ys(0), ys(1), row(512), row(512), v512, v512, _vm(), cls, row(D)],
               out_specs=[row(256), row(256), row(512), row(512), row(512), row(D), v512, v512, cls])(
        oa, ob, y2, y2, act, z, dsk, gs, W, gate, dX1)


def ffn_fwd(X, g, sh, sc, gate, Win, Wout, L, sends=()):
    T = X.shape[0]
    TR = 256
    nlt = L // TR

    def body(x_ref, g_ref, sh_ref, sc_ref, gt_ref, wi_ref, wo_ref, o_ref):
        h = _normmod(x_ref[...], g_ref[...], sh_ref[0], sc_ref[0]).astype(MXU)
        nt = (((1,), (1,)), ((), ()))
        a = lax.dot_general(h, wi_ref[0:DFF, :], nt, preferred_element_type=F32)
        u = lax.dot_general(h, wi_ref[DFF:2 * DFF, :], nt, preferred_element_type=F32)
        act = (_silu(a) * u).astype(MXU)
        o_ref[...] = x_ref[...] + gt_ref[0] * jnp.dot(act, wo_ref[...], preferred_element_type=F32)

    row = lambda w: pl.BlockSpec((TR, w), lambda i: (i, 0))
    cls = pl.BlockSpec((1, 1, D), lambda i: (i // nlt, 0, 0))
    vec = pl.BlockSpec((1, D), lambda i: (0, 0))
    return _pc(body, "ffn_fwd", _sds((T, D)), grid=(T // TR,),
               in_specs=[row(D), vec, cls, cls, cls, _vm(), _vm()], out_specs=row(D), sends=sends, gather=True)(
        X, g, sh, sc, gate, Win, Wout)


def ffn_bwd(X, g, sh, sc, gate, Win, Wout, dX2, L, sends=(), nchunk=2):
    T = X.shape[0]
    TR = 256
    nlt = L // TR
    CH = DFF // nchunk

    def body(x_ref, g_ref, sh_ref, sc_ref, gt_ref, wi_ref, wo_ref, dx2_r,
             dx_o, h_o, du_o, act_o, dout_o, dg_o, dsh_o, dsc_o, dgt_o):
        i = pl.program_id(0)
        h, vp = jax.vjp(_normmod, x_ref[...], g_ref[...], sh_ref[0], sc_ref[0])
        dx2 = dx2_r[...]
        dout = gt_ref[0] * dx2
        zero = jnp.zeros((TR, CH), F32)
        dh = jnp.zeros((TR, D), F32)
        out = jnp.zeros((TR, D), F32)
        for c in range(nchunk):
            lo, hi = c * CH, (c + 1) * CH
            wg, wu, wo = wi_ref[lo:hi, :], wi_ref[DFF + lo:DFF + hi, :], wo_ref[lo:hi, :]

            def f(h_, eg, eu):
                act = _silu(mmw_nt(h_, wg) + eg) * (mmw_nt(h_, wu) + eu)
                return mmw(act, wo), act

            o_c, vjp_c, act = jax.vjp(f, h, zero, zero, has_aux=True)
            dh_c, da, du = vjp_c(dout)
            dh, out = dh + dh_c, out + o_c
            du_o[:, lo:hi] = da.astype(MXU)
            du_o[:, DFF + lo:DFF + hi] = du.astype(MXU)
            act_o[:, lo:hi] = act.astype(MXU)
        dx, dg, dsh, dsc = vp(dh)
        dx_o[...] = dx + dx2
        h_o[...] = h.astype(MXU)
        dout_o[...] = dout.astype(MXU)
        _acc_init(i == 0, [dg_o])
        _acc_init((i == 0) | (i == nlt), [dsh_o, dsc_o, dgt_o])
        dg_o[...] += dg
        dsh_o[0] += dsh
        dsc_o[0] += dsc
        dgt_o[0] += jnp.sum(dx2 * out, axis=0, keepdims=True)

    row = lambda w: pl.BlockSpec((TR, w), lambda i: (i, 0))
    cls = pl.BlockSpec((1, 1, D), lambda i: (i // nlt, 0, 0))
    vec = pl.BlockSpec((1, D), lambda i: (0, 0))
    return _pc(body, "ffn_bwd",
               [_sds((T, D)), _sds((T, D), MXU), _sds((T, 2 * DFF), MXU), _sds((T, DFF), MXU), _sds((T, D), MXU),
                _sds((1, D)), _sds((2, 1, D)), _sds((2, 1, D)), _sds((2, 1, D))],
               grid=(T // TR,),
               in_specs=[row(D), vec, cls, cls, cls, _vm(), _vm(), row(D)],
               out_specs=[row(D), row(D), row(2 * DFF), row(DFF), row(D), vec, cls, cls, cls], sends=sends)(
        X, g, sh, sc, gate, Win, Wout, dX2)


def loss_head(X2, g, tgt, L):
    T = X2.shape[0]
    TR = 256
    nlt = L // TR

    def body(x_ref, g_ref, t_ref, loss_o, dx_o, dg_o):
        i = pl.program_id(0)
        _acc_init(i == 0, [loss_o, dg_o])

        @pl.when(i < nlt)
        def _():
            def f(x, g_):
                y = x * lax.rsqrt(jnp.mean(x * x, axis=-1, keepdims=True) + EPS) * g_
                return 0.5 * jnp.sum(jnp.mean(jnp.square(y - t_ref[...]), axis=-1, keepdims=True), axis=0,
                                     keepdims=True)

            val, vjp = jax.vjp(f, x_ref[...], g_ref[...])
            dx, dg = vjp(jnp.ones((1, 1), F32))
            dx_o[...] = dx
            loss_o[...] += jnp.broadcast_to(val, (8, 128))
            dg_o[...] += dg

        @pl.when(i >= nlt)
        def _():
            dx_o[...] = jnp.zeros_like(dx_o)

    row = pl.BlockSpec((TR, D), lambda i: (i, 0))
    vec = pl.BlockSpec((1, D), lambda i: (0, 0))
    return _pc(body, "loss_head", [_sds((8, 128)), _sds((T, D)), _sds((1, D))], grid=(T // TR,),
               in_specs=[row, vec, pl.BlockSpec((TR, D), lambda i: (jnp.minimum(i, nlt - 1), 0))],
               out_specs=[pl.BlockSpec((8, 128), lambda i: (0, 0)), row, vec])(X2, g, tgt)


def _stack_impl(q):
    lane = _iota(q.shape, 1)
    return jnp.concatenate([jnp.where(lane < HD, q, 0.0), jnp.where(lane >= HD, q, 0.0)], axis=0)


def _unstack_impl(o):
    M = o.shape[0] // 2
    return jnp.where(_iota((M, o.shape[1]), 1) < HD, o[:M], o[M:])


@jax.custom_vjp
def _stack(q):
    return _stack_impl(q)


_stack.defvjp(lambda q: (_stack_impl(q), None), lambda _, g: (_unstack_impl(g),))


@jax.custom_vjp
def _unstack(o):
    return _unstack_impl(o)


_unstack.defvjp(lambda o: (_unstack_impl(o), None), lambda _, g: (_stack_impl(g),))


def _softmax_av(q, ks, vs, biases, sink):
    q2 = _stack(q)
    ss = []
    for k, b in zip(ks, biases):
        s = mm_nt(q2, k) * (HD ** -0.5)
        ss.append(s if b is None else s + b)
    m = functools.reduce(jnp.maximum, [jnp.max(s, axis=1, keepdims=True) for s in ss])
    if sink is not None:
        m = jnp.maximum(m, sink)
    m = lax.stop_gradient(m)
    es = [jnp.exp(s - m) for s in ss]
    den = functools.reduce(lambda a, b_: a + b_, [jnp.sum(e, axis=1, keepdims=True) for e in es])
    if sink is not None:
        den = den + jnp.exp(sink - m)
    inv = 1.0 / den
    return _unstack(functools.reduce(lambda a, b_: a + b_, [mm(e * inv, v) for e, v in zip(es, vs)]))


def _sink_col(s0, s1, M):
    return jnp.concatenate([jnp.broadcast_to(jnp.mean(s0, axis=1, keepdims=True), (M, 1)),
                            jnp.broadcast_to(jnp.mean(s1, axis=1, keepdims=True), (M, 1))], axis=0)


def _stack4_impl(q):
    lane = _iota((q.shape[0], 128), 1)
    parts = []
    for p in range(2):
        qp = q[:, 128 * p:128 * (p + 1)]
        parts += [jnp.where(lane < HD, qp, 0.0), jnp.where(lane >= HD, qp, 0.0)]
    return jnp.concatenate(parts, axis=0)


def _unstack4_impl(o):
    M = o.shape[0] // 4
    lane = _iota((M, 128), 1)
    return jnp.concatenate([jnp.where(lane < HD, o[0:M], o[M:2 * M]),
                            jnp.where(lane < HD, o[2 * M:3 * M], o[3 * M:4 * M])], axis=1)


@jax.custom_vjp
def _stack4(q):
    return _stack4_impl(q)


_stack4.defvjp(lambda q: (_stack4_impl(q), None), lambda _, g: (_unstack4_impl(g),))


@jax.custom_vjp
def _unstack4(o):
    return _unstack4_impl(o)


_unstack4.defvjp(lambda o: (_unstack4_impl(o), None), lambda _, g: (_stack4_impl(g),))


def _wa_block(q, kp, kc, kn, vp, vc, vn, kx, vx, sks, n, L):
    kb = jnp.concatenate([kp, kc, kn], axis=0)
    vb = jnp.concatenate([vp, vc, vn], axis=0)
    qpos = n * Q + (_iota((4 * Q, 3 * Q), 0) & (Q - 1))
    kpos = (n - 1) * Q + _iota((4 * Q, 3 * Q), 1)
    valid = (jnp.abs(qpos - kpos) <= Q) & (kpos >= 0) & (kpos < L)
    bias = jnp.where(valid, 0.0, NEG)
    sink = jnp.concatenate([jnp.broadcast_to(jnp.mean(s_, axis=1, keepdims=True), (Q, 1)) for s_ in sks], axis=0)
    q4 = _stack4(q)
    sc = HD ** -0.5
    sl = mm_nt(q4, kb) * sc + bias
    sx = mm_nt(q4, kx) * sc
    m = lax.stop_gradient(jnp.maximum(jnp.maximum(jnp.max(sl, axis=1, keepdims=True),
                                                  jnp.max(sx, axis=1, keepdims=True)), sink))
    el, ex = jnp.exp(sl - m), jnp.exp(sx - m)
    inv = 1.0 / (jnp.sum(el, axis=1, keepdims=True) + jnp.sum(ex, axis=1, keepdims=True) + jnp.exp(sink - m))
    return _unstack4(mm(el * inv, vb) + mm(ex * inv, vx))


def _wa_specs(L):
    nb = L // Q
    qs = pl.BlockSpec((Q, 256), lambda n: (n, 0))
    kprev = pl.BlockSpec((Q, 128), lambda n: (jnp.maximum(n - 1, 0), 0))
    kcur = pl.BlockSpec((Q, 128), lambda n: (n, 0))
    knext = pl.BlockSpec((Q, 128), lambda n: (jnp.minimum(n + 1, nb - 1), 0))
    kctx = pl.BlockSpec((LC, 128), lambda n: (L // LC, 0))
    sks = pl.BlockSpec((2, 2, 1, 128), lambda n: (0, 0, 0, 0))
    return nb, qs, [kprev, kcur, knext], kctx, sks


def wa_fwd(QA, KA, VA, sinkp, L, sends=()):
    nb, qs, kband, kctx, sks = _wa_specs(L)

    def body(q_r, kp, kc, kn, vp, vc, vn, kx, vx, sk_r, o_ref):
        n = pl.program_id(0)
        f = lambda t: t[...].astype(F32)
        o_ref[...] = _wa_block(f(q_r), f(kp), f(kc), f(kn), f(vp), f(vc), f(vn), f(kx), f(vx),
                               [sk_r[0, 0], sk_r[0, 1], sk_r[1, 0], sk_r[1, 1]], n, L)

    return _pc(body, "wa_fwd", _sds((L, 256)), grid=(nb,),
               in_specs=[qs] + kband + kband + [kctx, kctx, sks], out_specs=qs, sends=sends, gather=True)(
        QA, KA, KA, KA, VA, VA, VA, KA, VA, sinkp)


def wa_bwd(QA, KA, VA, sinkp, dO, L, sends=()):
    nb, qs, kband, kctx, sks = _wa_specs(L)

    def body(q_r, kp, kc, kn, vp, vc, vn, kx, vx, sk_r, do_r, dq_o, dk_o, dv_o, dkx_o, dvx_o, dsk_o):
        n = pl.program_id(0)
        f = lambda t: t[...].astype(F32)
        fn = lambda q, a, b, c, d, e, g, kx_, vx_, s_: _wa_block(q, a, b, c, d, e, g, kx_, vx_, s_, n, L)
        _, vjp = jax.vjp(fn, f(q_r), f(kp), f(kc), f(kn), f(vp), f(vc), f(vn), f(kx), f(vx),
                         [sk_r[0, 0], sk_r[0, 1], sk_r[1, 0], sk_r[1, 1]])
        dq, dkp, dkc, dkn, dvp, dvc, dvn, dkx, dvx, ds = vjp(do_r[...])
        dq_o[...] = dq
        _acc_init(n == 0, [dk_o, dv_o, dkx_o, dvx_o, dsk_o])
        rows = pl.ds(pl.multiple_of(n * Q, Q), 3 * Q)
        dk_o[rows, :] += jnp.concatenate([dkp, dkc, dkn], axis=0)
        dv_o[rows, :] += jnp.concatenate([dvp, dvc, dvn], axis=0)
        dkx_o[...] += dkx
        dvx_o[...] += dvx
        for i_ in range(4):
            dsk_o[i_ // 2, i_ % 2] += ds[i_]

    full = lambda r: pl.BlockSpec((r, 128), lambda n: (0, 0))
    return _pc(body, "wa_bwd",
               [_sds((L, 256)), _sds((L + 2 * Q, 128)), _sds((L + 2 * Q, 128)), _sds((LC, 128)), _sds((LC, 128)),
                _sds((2, 2, 1, 128))],
               grid=(nb,), in_specs=[qs] + kband + kband + [kctx, kctx, sks, qs],
               out_specs=[qs, full(L + 2 * Q), full(L + 2 * Q), full(LC), full(LC), sks], sends=sends)(
        QA, KA, KA, KA, VA, VA, VA, KA, VA, sinkp, dO)


def _ctx_block(q, kx, vx, s0, s1):
    return _softmax_av(q, [kx], [vx], [None], _sink_col(s0, s1, LC))


def ctx_fwd(Qx, Kx, Vx, sinkp, shared, L):
    cq = pl.BlockSpec((LC, 128), lambda p: (L // LC, p))
    ck = pl.BlockSpec((LC, 128), lambda p: (L // LC, 0 if shared else p))
    sks = pl.BlockSpec((1, 2, 1, 128), lambda p: (p, 0, 0, 0))

    def body(q_r, k_r, v_r, sk_r, o_ref):
        f = lambda t: t[...].astype(F32)
        o_ref[...] = _ctx_block(f(q_r), f(k_r), f(v_r), sk_r[0, 0], sk_r[0, 1])

    return _pc(body, "ctx_fwd", _sds((LC, 256)), grid=(2,), in_specs=[cq, ck, ck, sks],
               out_specs=pl.BlockSpec((LC, 128), lambda p: (0, p)))(Qx, Kx, Vx, sinkp)


def ctx_bwd(Qx, Kx, Vx, sinkp, dO, shared, L):
    cq = pl.BlockSpec((LC, 128), lambda p: (L // LC, p))
    ck = pl.BlockSpec((LC, 128), lambda p: (L // LC, 0 if shared else p))
    sks = pl.BlockSpec((1, 2, 1, 128), lambda p: (p, 0, 0, 0))
    op = pl.BlockSpec((LC, 128), lambda p: (0, p))
    ok = pl.BlockSpec((LC, 128), lambda p: (0, 0 if shared else p))
    dos = pl.BlockSpec((LC, 128), lambda p: (L // LC, p))

    def body(q_r, k_r, v_r, sk_r, do_r, dq_o, dk_o, dv_o, dsk_o):
        p = pl.program_id(0)
        f = lambda t: t[...].astype(F32)
        _, vjp = jax.vjp(_ctx_block, f(q_r), f(k_r), f(v_r), sk_r[0, 0], sk_r[0, 1])
        dq, dk, dv, ds0, ds1 = vjp(do_r[...])
        dq_o[...] = dq
        _acc_init((p == 0) if shared else (p >= 0), [dk_o, dv_o])
        dk_o[...] += dk
        dv_o[...] += dv
        dsk_o[0, 0] = ds0
        dsk_o[0, 1] = ds1

    kw = 128 if shared else 256
    return _pc(body, "ctx_bwd", [_sds((LC, 256)), _sds((LC, kw)), _sds((LC, kw)), _sds((2, 2, 1, 128))],
               grid=(2,), in_specs=[cq, ck, ck, sks, dos], out_specs=[op, ok, ok, sks])(Qx, Kx, Vx, sinkp, dO)


def _na_rows(qs, kws, vws, kx, vx, bs):
    sc = HD ** -0.5
    q2 = [_stack(q) for q in qs]
    sl = [mm_nt(a, k) * sc + b for a, k, b in zip(q2, kws, bs)]
    sx = [mm_nt(a, kx) * sc for a in q2]
    m = [lax.stop_gradient(jnp.maximum(jnp.max(a, axis=1, keepdims=True), jnp.max(b, axis=1, keepdims=True)))
         for a, b in zip(sl, sx)]
    el = [jnp.exp(a - c) for a, c in zip(sl, m)]
    ex = [jnp.exp(a - c) for a, c in zip(sx, m)]
    inv = [1.0 / (jnp.sum(a, axis=1, keepdims=True) + jnp.sum(b, axis=1, keepdims=True)) for a, b in zip(el, ex)]
    o2 = [mm(a * i, v) + mm(b * i, vx) for a, b, i, v in zip(el, ex, inv, vws)]
    return [_unstack(o) for o in o2]


def _na_geom(rb, j, R):
    r = rb * 8 + j
    s = jnp.clip(r - 4, 0, R - 8)
    cls = jnp.where(r < 4, r, jnp.where(r > R - 4, r - (R - 8), 4))
    return pl.ds(pl.multiple_of(s * GW, GW), 8 * GW), cls


def _na_load(q_r, k_r, v_r, b_r, rb, R):
    geo = [_na_geom(rb, j, R) for j in range(8)]
    qs = [q_r[j * GW:(j + 1) * GW, :].astype(F32) for j in range(8)]
    kws = [k_r[win, :].astype(F32) for win, _ in geo]
    vws = [v_r[win, :].astype(F32) for win, _ in geo]
    bs = [jnp.concatenate([b_r[0, cls], b_r[1, cls]], axis=0) for _, cls in geo]
    return geo, qs, kws, vws, bs


def na_fwd(QB, KB, VB, biasd, L, sends=()):
    R = L // GW
    qs = pl.BlockSpec((8 * GW, 128), lambda p, rb: (rb, p))
    kfull = pl.BlockSpec((L, 128), lambda p, rb: (0, p))
    kctx = pl.BlockSpec((LC, 128), lambda p, rb: (L // LC, p))
    bs = pl.BlockSpec((2, 8, GW, 8 * GW), lambda p, rb: (p, 0, 0, 0))

    def body(q_r, k_r, v_r, kx_r, vx_r, b_r, o_ref):
        _, qs_, kws, vws, bs_ = _na_load(q_r, k_r, v_r, b_r, pl.program_id(1), R)
        outs = _na_rows(qs_, kws, vws, kx_r[...].astype(F32), vx_r[...].astype(F32), bs_)
        o_ref[...] = jnp.concatenate(outs, axis=0)

    return _pc(body, "na_fwd", _sds((L, 256)), grid=(2, R // 8), in_specs=[qs, kfull, kfull, kctx, kctx, bs],
               out_specs=qs, sends=sends, gather=True)(QB, KB, VB, KB, VB, biasd)


def na_bwd(QB, KB, VB, biasd, dO, L):
    R = L // GW
    qs = pl.BlockSpec((8 * GW, 128), lambda p, rb: (rb, p))
    kfull = pl.BlockSpec((L, 128), lambda p, rb: (0, p))
    kctx = pl.BlockSpec((LC, 128), lambda p, rb: (L // LC, p))
    bs = pl.BlockSpec((2, 8, GW, 8 * GW), lambda p, rb: (p, 0, 0, 0))
    oc = pl.BlockSpec((LC, 128), lambda p, rb: (0, p))

    def body(q_r, k_r, v_r, kx_r, vx_r, b_r, do_r, dq_o, dk_o, dv_o, dkx_o, dvx_o, db_o):
        rb = pl.program_id(1)
        _acc_init(rb == 0, [dk_o, dv_o, dkx_o, dvx_o, db_o])
        geo, qs_, kws, vws, bs_ = _na_load(q_r, k_r, v_r, b_r, rb, R)
        _, vjp = jax.vjp(_na_rows, qs_, kws, vws, kx_r[...].astype(F32), vx_r[...].astype(F32), bs_)
        dqs, dkws, dvws, dkx, dvx, dbs = vjp([do_r[j * GW:(j + 1) * GW, :] for j in range(8)])
        dq_o[...] = jnp.concatenate(dqs, axis=0)
        dkx_o[...] += dkx
        dvx_o[...] += dvx
        for j, (win, cls) in enumerate(geo):
            dk_o[win, :] += dkws[j]
            dv_o[win, :] += dvws[j]
            db_o[0, cls] += dbs[j][:GW]
            db_o[1, cls] += dbs[j][GW:]

    return _pc(body, "na_bwd",
               [_sds((L, 256)), _sds((L, 256)), _sds((L, 256)), _sds((LC, 256)), _sds((LC, 256)),
                _sds((4, 8, GW, 8 * GW))],
               grid=(2, R // 8), in_specs=[qs, kfull, kfull, kctx, kctx, bs, qs],
               out_specs=[qs, kfull, kfull, oc, oc, bs])(QB, KB, VB, KB, VB, biasd, dO)


def exact_mm_call(A, B):
    def body(a_ref, b_ref, o_ref):
        o_ref[...] = _exact(a_ref[...], b_ref[...])

    return _pc(body, "exact_mm", _sds((A.shape[0], B.shape[1])))(A, B)


def _conv_shift(x, d, L):
    T = x.shape[0]
    if d == 0:
        return x
    t = _iota(x.shape, 0)
    src = t + d
    ok = (src >= 0) & (src < T) & ((src >= L) == (t >= L))
    return jnp.where(ok, pltpu.roll(x, (-d) % T, 0), 0.0)


def conv_fwd(XBC, w8, b, L):
    T = XBC.shape[0]

    def body(x_ref, w_ref, b_ref, o_ref):
        x = x_ref[...]
        pre = b_ref[...] + functools.reduce(
            lambda a, c: a + c, [_conv_shift(x, k - 3, L) * w_ref[k:k + 1, :] for k in range(7)])
        o_ref[...] = _silu(pre)

    col = pl.BlockSpec((T, 128), lambda j: (0, j))
    return _pc(body, "conv_fwd", _sds((T, 1024)), grid=(8,),
               in_specs=[col, pl.BlockSpec((8, 128), lambda j: (0, j)), pl.BlockSpec((1, 128), lambda j: (0, j))],
               out_specs=col)(XBC, w8, b)


def conv_bwd(XBC, w8, b, dS, dxs_skip, L):
    T = XBC.shape[0]

    def body(x_ref, w_ref, b_ref, d0_r, d1_r, dsk_r, dx_o, dw_o, db_o):
        j = pl.program_id(0)
        x = x_ref[...]
        xs = [_conv_shift(x, k - 3, L) for k in range(7)]
        pre = b_ref[...] + functools.reduce(lambda a, c: a + c, [xs[k] * w_ref[k:k + 1, :] for k in range(7)])
        _, vjp = jax.vjp(_silu, pre)
        dact = d0_r[0] + d1_r[0] + jnp.where(j < 4, dsk_r[...], 0.0)
        dpre, = vjp(dact)
        dx_o[...] = functools.reduce(
            lambda a, c: a + c, [_conv_shift(dpre, 3 - k, L) * w_ref[k:k + 1, :] for k in range(7)])
        dw_o[...] = jnp.concatenate([jnp.sum(dpre * xs[k], axis=0, keepdims=True) for k in range(7)]
                                    + [jnp.zeros((1, 128), F32)], axis=0)
        db_o[...] = jnp.sum(dpre, axis=0, keepdims=True)

    col = pl.BlockSpec((T, 128), lambda j: (0, j))
    w_s = pl.BlockSpec((8, 128), lambda j: (0, j))
    b_s = pl.BlockSpec((1, 128), lambda j: (0, j))
    ds = lambda d: pl.BlockSpec((1, T, 128), lambda j: (d, 0, j))
    return _pc(body, "conv_bwd", [_sds((T, 1024)), _sds((8, 1024)), _sds((1, 1024))], grid=(8,),
               in_specs=[col, w_s, b_s, ds(0), ds(1), pl.BlockSpec((T, 128), lambda j: (0, jnp.minimum(j, 3)))],
               out_specs=[col, w_s, b_s])(XBC, w8, b, dS, dS, dxs_skip)


def _ssd_chunk(xs, bs, cs, dtraw, dtb, alog, hs, tri, d):
    dt = _softplus(dtraw + dtb)
    a = dt * (-jnp.exp(alog))
    acum = _exact(tri, a)
    tot = jnp.sum(a, axis=0, keepdims=True)
    wcol = jnp.exp(tot - acum) * dt
    ea = jnp.exp(acum)
    cd = jnp.exp(tot)
    acum_t, dt_t = acum.T, dt.T
    lane = _iota((Q, 128), 1)
    srow = _iota((128, Q), 0)
    lane1 = _iota((1, 128), 1)
    prow = _iota((128, NSTATE), 0)
    mask = tri > 0.5
    cbs = [mm_nt(cs[g], bs[g]) for g in range(2)]
    ys, hn = [], []
    for j in range(4):
        g = j // 2
        x = xs[j]
        yi, st, eac, cdl = [], [], [], []
        for u in range(2):
            slot = d * 8 + 2 * j + u
            col = lambda m: jnp.sum(jnp.where(lane == slot, m, 0.0), axis=1, keepdims=True)
            rowv = lambda m: jnp.sum(jnp.where(srow == slot, m, 0.0), axis=0, keepdims=True)
            seg = col(acum) - rowv(acum_t)
            dcy = jnp.where(mask, jnp.exp(jnp.where(mask, seg, 0.0)), 0.0)
            yi.append(mm(cbs[g] * dcy * rowv(dt_t), x))
            st.append(mm_tn(x, bs[g] * col(wcol)))
            eac.append(col(ea))
            cdl.append(jnp.sum(jnp.where(lane1 == slot, cd, 0.0), axis=1, keepdims=True))
        yin = mm_nt(cs[g], hs[j])
        ys.append(jnp.where(lane < HD, yi[0] + yin * eac[0], yi[1] + yin * eac[1]))
        hn.append(hs[j] * jnp.where(prow < HD, cdl[0], cdl[1]) + jnp.where(prow < HD, st[0], st[1]))
    return ys, hn


def _ssd_chunk_idx(d, s, nlc, nch):
    return jnp.where(d == 0, (s + nlc) % nch, nch - 1 - s)


def ssd_fwd(ACT, DT, dtb, alog, tri2, L, sends=()):
    T = ACT.shape[0]
    nlc, nch = L // Q, T // Q

    def body(a_ref, dt_ref, dtb_ref, al_ref, tri_ref, y_o, hs_o, hst):
        d, s = pl.program_id(0), pl.program_id(1)
        _acc_init(s == 0, [hst])
        a = a_ref[...]
        xs = [a[:, 128 * j:128 * (j + 1)] for j in range(4)]
        bs = [a[:, 512 + 128 * g:640 + 128 * g] for g in range(2)]
        cs = [a[:, 768 + 128 * g:896 + 128 * g] for g in range(2)]
        hs = [hst[j] for j in range(4)]
        hs_o[0, 0] = hst[...]
        ys, hn = _ssd_chunk(xs, bs, cs, dt_ref[...], dtb_ref[...], al_ref[...], hs, tri_ref[0], d)
        y_o[0] = jnp.concatenate(ys, axis=1)
        for j in range(4):
            hst[j] = hn[j]

    ck = lambda w: pl.BlockSpec((Q, w), lambda d, s: (_ssd_chunk_idx(d, s, nlc, nch), 0))
    v128 = pl.BlockSpec((1, 128), lambda d, s: (0, 0))
    return _pc(body, "ssd_fwd", [_sds((2, T, 512)), _sds((2, nch, 4, 128, NSTATE))], grid=(2, nch),
               in_specs=[ck(1024), ck(128), v128, v128, pl.BlockSpec((1, Q, Q), lambda d, s: (d, 0, 0))],
               out_specs=[pl.BlockSpec((1, Q, 512), lambda d, s: (d, _ssd_chunk_idx(d, s, nlc, nch), 0)),
                          pl.BlockSpec((1, 1, 4, 128, NSTATE), lambda d, s: (d, s, 0, 0, 0))],
               scratch=[pltpu.VMEM((4, 128, NSTATE), F32)], sends=sends, gather=True)(ACT, DT, dtb, alog, tri2)


def ssd_bwd(ACT, DT, dtb, alog, tri2, HS, dY, L, sends=()):
    T = ACT.shape[0]
    nlc, nch = L // Q, T // Q

    def body(a_ref, dt_ref, dtb_ref, al_ref, tri_ref, hs_ref, dy_ref, da_o, ddt_o, ddtb_o, dal_o, dh):
        d, sr = pl.program_id(0), pl.program_id(1)
        _acc_init(sr == 0, [dh, ddtb_o, dal_o])
        a = a_ref[...]
        xs = [a[:, 128 * j:128 * (j + 1)] for j in range(4)]
        bs = [a[:, 512 + 128 * g:640 + 128 * g] for g in range(2)]
        cs = [a[:, 768 + 128 * g:896 + 128 * g] for g in range(2)]
        hs = [hs_ref[0, 0, j] for j in range(4)]
        tri = tri_ref[0]
        fn = lambda xs_, bs_, cs_, dtr, dtb_, al, hs_: _ssd_chunk(xs_, bs_, cs_, dtr, dtb_, al, hs_, tri, d)
        _, vjp = jax.vjp(fn, xs, bs, cs, dt_ref[...], dtb_ref[...], al_ref[...], hs)
        dy = dy_ref[...]
        dys = [dy[:, 128 * j:128 * (j + 1)] for j in range(4)]
        dxs, dbs, dcs, ddt, ddtb, dal, dhs = vjp((dys, [dh[j] for j in range(4)]))
        da_o[0] = jnp.concatenate(dxs + dbs + dcs, axis=1)
        ddt_o[0] = ddt
        ddtb_o[0] += ddtb
        dal_o[0] += dal
        for j in range(4):
            dh[j] = dhs[j]

    cidx = lambda d, sr: _ssd_chunk_idx(d, nch - 1 - sr, nlc, nch)
    ck = lambda w: pl.BlockSpec((Q, w), lambda d, sr: (cidx(d, sr), 0))
    v128 = pl.BlockSpec((1, 128), lambda d, sr: (0, 0))
    o128 = pl.BlockSpec((1, 1, 128), lambda d, sr: (d, 0, 0))
    return _pc(body, "ssd_bwd", [_sds((2, T, 1024)), _sds((2, T, 128)), _sds((2, 1, 128)), _sds((2, 1, 128))],
               grid=(2, nch),
               in_specs=[ck(1024), ck(128), v128, v128, pl.BlockSpec((1, Q, Q), lambda d, sr: (d, 0, 0)),
                         pl.BlockSpec((1, 1, 4, 128, NSTATE), lambda d, sr: (d, nch - 1 - sr, 0, 0, 0)), ck(512)],
               out_specs=[pl.BlockSpec((1, Q, 1024), lambda d, sr: (d, cidx(d, sr), 0)),
                          pl.BlockSpec((1, Q, 128), lambda d, sr: (d, cidx(d, sr), 0)), o128, o128],
               scratch=[pltpu.VMEM((4, 128, NSTATE), F32)], sends=sends)(ACT, DT, dtb, alog, tri2, HS, dY)


_PAIR_HEADS = np.array([[0, 2], [1, 3]])


def _tables(L):
    t = jnp.arange(L)
    inv = 10000.0 ** (-jnp.arange(16, dtype=F32) / 16)

    def half(pos):
        ang = pos.astype(F32)[:, None] * inv[None, :]
        return jnp.concatenate([ang, ang], axis=1)

    ang = jnp.tile(jnp.concatenate([half(t // GW), half(t % GW)], axis=1), (1, 4))
    cos = jnp.concatenate([jnp.cos(ang), jnp.ones((LC, 256), F32)], axis=0)
    sin = jnp.concatenate([jnp.sin(ang), jnp.zeros((LC, 256), F32)], axis=0)
    rm = np.zeros((256, 256), np.float32)
    for j in range(256):
        if j % 32 < 16:
            rm[j + 16, j] = -1.0
        else:
            rm[j - 16, j] = 1.0
    tri = np.tril(np.ones((Q, Q), np.float32))
    return cos, sin, jnp.asarray(rm), jnp.asarray(np.stack([tri, tri.T]))


def _na_index(R):
    rc = np.array([0, 1, 2, 3, 4, R - 3, R - 2, R - 1])
    dy = np.clip(rc - 4, 0, R - 8)[:, None] + np.arange(8)[None, :] - rc[:, None] + 7
    qc, cc = np.arange(GW)[:, None], np.arange(GW)[None, :]
    dx = np.clip(cc - qc, -15, 15) + 15
    cstart = np.clip(qc - 8, 0, GW - 16)
    cmask = (cc >= cstart) & (cc < cstart + 16)
    idx = dy[:, None, :, None] * 31 + dx[None, :, None, :]
    return idx.reshape(8, GW, 8 * GW), np.broadcast_to(cmask[None, :, None, :], idx.shape).reshape(8, GW, 8 * GW), \
        dy, dx, cmask


def _na_bias(rpb, R):
    _, cm, dy, dx, _ = _na_index(R)
    e1t = np.zeros((128, GW * GW), np.float32)
    e1t[dx.reshape(-1), np.arange(GW * GW)] = 1.0
    v = jnp.pad(rpb[:, dy.reshape(-1), :].reshape(256, 31), ((0, 0), (0, 97)))
    full = exact_mm_call(v, jnp.asarray(e1t))
    dense = full.reshape(4, 8, 8, GW, GW).transpose(0, 1, 3, 2, 4).reshape(4, 8, GW, 8 * GW)
    return jnp.where(cm[None], dense, NEG)


def _na_bias_grad(dbias, R):
    _, _, dy, dx, cmask = _na_index(R)
    e1 = np.zeros((GW * GW, 128), np.float32)
    e1[np.arange(GW * GW), dx.reshape(-1)] = cmask.reshape(-1)
    a1 = dbias.reshape(4, 8, GW, 8, GW).transpose(0, 1, 3, 2, 4).reshape(256, GW * GW)
    v = exact_mm_call(a1, jnp.asarray(e1))[:, :31].reshape(4, 64, 31)
    e2 = np.zeros((64, 128), np.float32)
    e2[np.arange(64), dy.reshape(-1)] = 1.0
    a2 = jnp.pad(v.transpose(0, 2, 1).reshape(124, 64), ((0, 4), (0, 0)))
    return exact_mm_call(a2, jnp.asarray(e2))[:124, :15].reshape(4, 31, 15).transpose(0, 2, 1)


def _lanes(v, n=128):
    v = v.reshape(1, -1)
    return jnp.pad(v, ((0, 0), (0, n - v.shape[1])))


def _cls2(a, b):
    return jnp.stack([a, b]).reshape(2, 1, D)


def _win_p(g):
    return jnp.concatenate([g.reshape(IN_COLS, D), jnp.zeros((NP_IN - IN_COLS, D), g.dtype)], axis=0)


def _wfi(ga, gb):
    return jnp.stack([ga, gb], axis=1).reshape(2 * DFF, D)


def _layer_consts(p):
    sinkp = jnp.broadcast_to(p["wa_sink"][_PAIR_HEADS][:, :, None, None], (2, 2, 1, 128))
    return dict(
        sinkp=sinkp, nosink=jnp.full((2, 2, 1, 128), NEG, F32),
        w8=jnp.concatenate([p["ssm_conv_w"], jnp.zeros((1, 1024), F32)], axis=0),
        cb=p["ssm_conv_b"].reshape(1, 1024), dtb=_lanes(p["ssm_dt_bias"]), alog=_lanes(p["ssm_a_log"]),
        dsk=jnp.repeat(p["ssm_d"], HD).reshape(1, 512), gs=p["ssm_norm_g"].reshape(1, 512),
        gmix=p["g_mix"].reshape(1, D), gffn=p["g_ffn"].reshape(1, D))


def _mods(mod2):
    return [_cls2(mod2[0, D * k:D * (k + 1)], mod2[1, D * k:D * (k + 1)]) for k in range(6)]


def _layer_fwd(X, mod2, c, rpb, tabs, L, ctx_out, shards, nxt):
    cos, sin, rm, tri2 = tabs
    sh1, sc1, gt1, sh2, sc2, gt2 = _mods(mod2)
    biasd = _na_bias(rpb, L // GW)
    fi, fo, wo = shards
    (qa, qb, z, ka, va, kb, vb, xbc, dt, h1), (gfo,) = in_fwd(X, c["gmix"], sh1, sc1, c["win"], cos, sin, rm, L,
                                                              sends=(fo,))
    (oa,), (gfa,) = wa_fwd(qa, ka, va, c["sinkp"], L, sends=(fi[:DFF // NDEV],))
    (ob,), (gwo,) = na_fwd(qb, kb, vb, biasd, L, sends=(wo,))
    c = dict(c, wout=gwo.reshape(D, D))
    if ctx_out:
        oa_c = ctx_fwd(qa, ka, va, c["sinkp"], True, L)
        ob_c = ctx_fwd(qb, kb, vb, c["nosink"], False, L)
    else:
        oa_c = ob_c = jnp.zeros((LC, 256), F32)
    oa = jnp.concatenate([oa, oa_c], axis=0)
    ob = jnp.concatenate([ob, ob_c], axis=0)
    act = conv_fwd(xbc, c["w8"], c["cb"], L)
    (y2, hs), (gfb,) = ssd_fwd(act, dt, c["dtb"], c["alog"], tri2, L, sends=(fi[DFF // NDEV:],))
    X1, cat = out_fwd(oa, ob, y2, act, z, c["dsk"], c["gs"], c["wout"], X, gt1, L)
    c = dict(c, wfi=_wfi(gfa, gfb), wfo=gfo.reshape(DFF, D))
    res = ffn_fwd(X1, c["gffn"], sh2, sc2, gt2, c["wfi"], c["wfo"], L, sends=nxt)
    (X2,), got = res if nxt else ((res,), ())
    saved = dict(X=X, X1=X1, qa=qa, qb=qb, z=z, ka=ka, va=va, kb=kb, vb=vb, xbc=xbc, dt=dt, h1=h1, oa=oa, ob=ob,
                 act=act, y2=y2, hs=hs, cat=cat, biasd=biasd)
    return X2, saved, c, got


def _row_blocks(gw):
    return gw.reshape(NDEV, gw.shape[0] // NDEV, gw.shape[1])


def _layer_bwd(dX2, s, mod2, c, tabs, L, ctx_out, carry):
    cos, sin, rm, tri2 = tabs
    sh1, sc1, gt1, sh2, sc2, gt2 = _mods(mod2)
    R = L // GW
    res = ffn_bwd(s["X1"], c["gffn"], sh2, sc2, gt2, c["wfi"], c["wfo"], dX2, L, sends=carry)
    (dX1, h2, dU, actf, dOut, dgffn, dsh2, dsc2, dgt2), got = res if carry else (res, ())
    g = {}
    gfi = _row_blocks(tn_mm(dU, h2, 1408, 1024, MXU))
    gfo = _row_blocks(tn_mm(actf, dOut, 1408, 1024, MXU))
    doa, dob, dy, dxs_skip, dz, dmix, ddsk, dgs, dgt1 = out_bwd(s["oa"], s["ob"], s["y2"], s["act"], s["z"], c["dsk"],
                                                                c["gs"], c["wout"], gt1, dX1, L)
    gout = _row_blocks(tn_mm(s["cat"], dmix, 1024, 512, MXU))
    (dS, ddt2, ddtb, dal), (g["w_ffn_in"], g["w_ffn_out"]) = ssd_bwd(
        s["act"], s["dt"], c["dtb"], c["alog"], tri2, s["hs"], dy, L, sends=(gfi, gfo))
    dxbc, dw8, dcb = conv_bwd(s["xbc"], c["w8"], c["cb"], dS, dxs_skip, L)
    (dqa, dkpad, dvpad, dkxa, dvxa, dska), (g["w_out"],) = wa_bwd(s["qa"], s["ka"], s["va"], c["sinkp"], doa, L,
                                                                  sends=(gout,))
    dqb, dkb, dvb, dkxb, dvxb, dbias = na_bwd(s["qb"], s["kb"], s["vb"], s["biasd"], dob, L)
    if ctx_out:
        dqa_c, dk1, dv1, dsk1 = ctx_bwd(s["qa"], s["ka"], s["va"], c["sinkp"], doa, True, L)
        dqb_c, dk2, dv2, _ = ctx_bwd(s["qb"], s["kb"], s["vb"], c["nosink"], dob, False, L)
        dkxa, dvxa, dska = dkxa + dk1, dvxa + dv1, dska + dsk1
        dkxb, dvxb = dkxb + dk2, dvxb + dv2
    else:
        dqa_c = dqb_c = jnp.zeros((LC, 256), F32)
    cat0 = lambda a, b: jnp.concatenate([a, b], axis=0)
    dX, dycat, dgmix, dsh1, dsc1 = in_bwd(
        s["X"], c["gmix"], sh1, sc1, c["win"], cos, sin, rm, dX1, cat0(dqa, dqa_c), cat0(dqb, dqb_c), dz,
        cat0(dkpad[Q:L + Q], dkxa), cat0(dvpad[Q:L + Q], dvxa), cat0(dkb, dkxb), cat0(dvb, dvxb), dxbc, ddt2, L,
        latent_only=ctx_out)
    gin = _row_blocks(tn_mm(dycat, s["h1"], 1024, 1024, MXU)[:IN_COLS])
    g["g_mix"] = dgmix.reshape(D)
    g["g_ffn"] = dgffn.reshape(D)
    sk = jnp.sum(dska, axis=(2, 3))
    g["wa_sink"] = jnp.zeros((4,), F32).at[_PAIR_HEADS.reshape(-1)].set(sk.reshape(-1))
    g["na_rpb"] = _na_bias_grad(dbias, R)
    g["ssm_conv_w"] = dw8[:7]
    g["ssm_conv_b"] = dcb.reshape(1024)
    g["ssm_dt_bias"] = (ddtb[0] + ddtb[1])[0, :16].reshape(2, 8)
    g["ssm_a_log"] = (dal[0] + dal[1])[0, :16].reshape(2, 8)
    g["ssm_d"] = jnp.sum(ddsk.reshape(8, HD), axis=1)
    g["ssm_norm_g"] = dgs.reshape(512)
    dmod2 = jnp.concatenate([dsh1, dsc1, dgt1, dsh2, dsc2, dgt2], axis=2).reshape(2, 6 * D)
    return dX, g, dmod2, gin, got


def local_step(x, ctx, tgt, mods, layers, shards, g_final, L):
    tabs = _tables(L)
    X = (x, ctx)
    consts = [_layer_consts(p) for p in layers]
    saved = []
    got = all_gather([shards["w_in"][0]], "gather_first")
    for i in range(2):
        consts[i] = dict(consts[i], win=_win_p(got[0]))
        nxt = (shards["w_in"][1],) if i == 0 else ()
        X, s, consts[i], got = _layer_fwd(X, mods[i], consts[i], layers[i]["na_rpb"], tabs, L, i == 0,
                                          (shards["w_ffn_in"][i], shards["w_ffn_out"][i], shards["w_out"][i]), nxt)
        saved.append(s)
    loss8, dX, dgfin = loss_head(X, g_final.reshape(1, D), tgt, L)
    grads, dmods = [None, None], [None, None]
    dX, grads[1], dmods[1], gin1, _ = _layer_bwd(dX, saved[1], mods[1], consts[1], tabs, L, False, ())
    dX, grads[0], dmods[0], gin0, (grads[1]["w_in"],) = _layer_bwd(dX, saved[0], mods[0], consts[0], tabs, L, True,
                                                                   (gin1,))
    return loss8[0, 0], dX, grads, jnp.stack(dmods), dgfin.reshape(D), gin0


def _place():
    x, y, c = lax.axis_index("x"), lax.axis_index("y"), lax.axis_index("c")
    return x, y, c


def _slot(b):
    return 4 * b[0] + 2 * b[1] + b[2]


def _any():
    return pl.BlockSpec(memory_space=pl.ANY)


def all_gather(xs, name):
    n = len(xs)

    def body(*refs):
        x_refs, o_refs = refs[:n], refs[n:2 * n]
        send_sems, recv_sems, local_sems = refs[2 * n:]
        x, y, c = _place()
        me, sib = (x, y, c), (x, y, 1 - c)
        chips = [(1 - x, y), (x, 1 - y), (1 - x, 1 - y)]

        def copy(t, k, blk, to, src=None):
            dst = o_refs[t].at[_slot(blk)]
            return pltpu.make_async_remote_copy(
                src_ref=dst if src is None else src, dst_ref=dst, send_sem=send_sems.at[7 * t + k],
                recv_sem=recv_sems.at[7 * t + k], device_id=to, device_id_type=MESH_T)

        mine = [pltpu.make_async_copy(x_refs[t], o_refs[t].at[_slot(me)], local_sems.at[t]) for t in range(n)]
        for cp in mine:
            cp.start()
        first = []
        for t in range(n):
            first.append(copy(t, 0, me, sib, src=x_refs[t]))
            first += [copy(t, 1 + j, me, (*chip, c), src=x_refs[t]) for j, chip in enumerate(chips)]
        for cp in first:
            cp.start()
        passed = []
        for j, chip in enumerate(chips):
            for t in range(n):
                copy(t, 1 + j, (*chip, c), me).wait_recv()
                cp = copy(t, 4 + j, (*chip, c), sib)
                cp.start()
                passed.append(cp)
        for t in range(n):
            copy(t, 0, sib, me).wait_recv()
            for j, chip in enumerate(chips):
                copy(t, 4 + j, (*chip, 1 - c), me).wait_recv()
        for cp in first + passed:
            cp.wait_send()
        for cp in mine:
            cp.wait()

    return pl.pallas_call(
        body, name=name, out_shape=[_sds((NDEV,) + a.shape, a.dtype) for a in xs],
        in_specs=[_any()] * n, out_specs=[_any()] * n,
        scratch_shapes=[pltpu.SemaphoreType.DMA((7 * n,)), pltpu.SemaphoreType.DMA((7 * n,)),
                        pltpu.SemaphoreType.DMA((n,))],
        interpret=_INTERPRET)(*xs)


def all_to_all(xs, name):
    n = len(xs)

    def body(*refs):
        _a2a_start(refs[:n], refs[n:2 * n], *refs[2 * n:])
        _a2a_wait(refs[:n], refs[n:2 * n], *refs[2 * n:])

    return pl.pallas_call(
        body, name=name, out_shape=[_sds(a.shape, a.dtype) for a in xs],
        in_specs=[_any()] * n, out_specs=[_any()] * n, scratch_shapes=_a2a_sems(n), interpret=_INTERPRET)(*xs)


def _a2a_sems(n):
    return [pltpu.SemaphoreType.DMA((7 * n,)), pltpu.SemaphoreType.DMA((7 * n,)), pltpu.SemaphoreType.DMA((n,))]


def _a2a_copies(x_refs, o_refs, send_sems, recv_sems, local_sems):
    n = len(x_refs)
    x, y, c = _place()
    me = (x, y, c)
    flip = lambda v, b: (1 - v) if b else v
    peers = [(flip(x, k >> 2 & 1), flip(y, k >> 1 & 1), flip(c, k & 1)) for k in range(1, NDEV)]
    mine = [pltpu.make_async_copy(x_refs[t].at[_slot(me)], o_refs[t].at[_slot(me)], local_sems.at[t])
            for t in range(n)]

    def copy(t, k, src_slot, dst_slot, to):
        return pltpu.make_async_remote_copy(
            src_ref=x_refs[t].at[src_slot], dst_ref=o_refs[t].at[dst_slot], send_sem=send_sems.at[7 * t + k],
            recv_sem=recv_sems.at[7 * t + k], device_id=to, device_id_type=MESH_T)

    sends = [copy(t, k, _slot(p), _slot(me), p) for t in range(n) for k, p in enumerate(peers)]
    recvs = [copy(t, k, _slot(p), _slot(p), me) for t in range(n) for k, p in enumerate(peers)]
    return mine, sends, recvs


def _ag_copies(x_refs, o_refs, send_sems, recv_sems, local_sems):
    n = len(x_refs)
    x, y, c = _place()
    me = (x, y, c)
    flip = lambda v, b: (1 - v) if b else v
    peers = [(flip(x, k >> 2 & 1), flip(y, k >> 1 & 1), flip(c, k & 1)) for k in range(1, NDEV)]
    mine = [pltpu.make_async_copy(x_refs[t], o_refs[t].at[_slot(me)], local_sems.at[t]) for t in range(n)]

    def copy(t, k, dst_slot, to):
        return pltpu.make_async_remote_copy(
            src_ref=x_refs[t], dst_ref=o_refs[t].at[dst_slot], send_sem=send_sems.at[7 * t + k],
            recv_sem=recv_sems.at[7 * t + k], device_id=to, device_id_type=MESH_T)

    sends = [copy(t, k, _slot(me), p) for t in range(n) for k, p in enumerate(peers)]
    recvs = [copy(t, k, _slot(p), me) for t in range(n) for k, p in enumerate(peers)]
    return mine, sends, recvs


def _ag_start(x_refs, o_refs, send_sems, recv_sems, local_sems):
    mine, sends, _ = _ag_copies(x_refs, o_refs, send_sems, recv_sems, local_sems)
    for cp in mine + sends:
        cp.start()


def _ag_wait(x_refs, o_refs, send_sems, recv_sems, local_sems):
    mine, sends, recvs = _ag_copies(x_refs, o_refs, send_sems, recv_sems, local_sems)
    for cp in recvs:
        cp.wait_recv()
    for cp in sends:
        cp.wait_send()
    for cp in mine:
        cp.wait()


def _a2a_start(x_refs, o_refs, send_sems, recv_sems, local_sems):
    mine, sends, _ = _a2a_copies(x_refs, o_refs, send_sems, recv_sems, local_sems)
    for cp in mine + sends:
        cp.start()


def _a2a_wait(x_refs, o_refs, send_sems, recv_sems, local_sems):
    mine, sends, recvs = _a2a_copies(x_refs, o_refs, send_sems, recv_sems, local_sems)
    for cp in recvs:
        cp.wait_recv()
    for cp in sends:
        cp.wait_send()
    for cp in mine:
        cp.wait()


def adam_reduce(P, w, m, v, name, sends=()):
    n, R, C = P.shape
    br = R // 4 if R % 64 == 0 else R

    def body(p_ref, w_ref, m_ref, v_ref, g_o, d_o, m_o, v_o):
        g = p_ref[0].astype(F32)
        for k in range(1, n):
            g = g + p_ref[k].astype(F32)
        m1 = ADAM_B1 * m_ref[...] + (1.0 - ADAM_B1) * g
        v1 = ADAM_B2 * v_ref[...] + (1.0 - ADAM_B2) * jnp.square(g)
        m_hat = m1 / (1.0 - ADAM_B1 ** ADAM_STEP)
        v_hat = v1 / (1.0 - ADAM_B2 ** ADAM_STEP)
        g_o[...] = g
        d_o[...] = -ADAM_LR * (m_hat / (jnp.sqrt(v_hat) + ADAM_EPS) + ADAM_WD * w_ref[...])
        m_o[...] = m1
        v_o[...] = v1

    blk = pl.BlockSpec((br, C), lambda i: (i, 0))
    return _pc(body, name, [_sds((R, C))] * 4, grid=(R // br,),
               in_specs=[pl.BlockSpec((n, br, C), lambda i: (0, i, 0)), blk, blk, blk], out_specs=[blk] * 4,
               sends=sends)(P, w, m, v)


def adam_layers(P0, P1, w, m, v, name, sends=()):
    n, R, C = P0.shape
    br = R // 4 if R % 64 == 0 else R
    nb = R // br

    def body(p0_ref, p1_ref, w_ref, m_ref, v_ref, g_o, d_o, m_o, v_o):
        def total(p_ref):
            g = p_ref[0].astype(F32)
            for k in range(1, n):
                g = g + p_ref[k].astype(F32)
            return g

        g = jnp.where(pl.program_id(0) == 0, total(p0_ref), total(p1_ref))
        m1 = ADAM_B1 * m_ref[0] + (1.0 - ADAM_B1) * g
        v1 = ADAM_B2 * v_ref[0] + (1.0 - ADAM_B2) * jnp.square(g)
        m_hat = m1 / (1.0 - ADAM_B1 ** ADAM_STEP)
        v_hat = v1 / (1.0 - ADAM_B2 ** ADAM_STEP)
        g_o[0] = g
        d_o[0] = -ADAM_LR * (m_hat / (jnp.sqrt(v_hat) + ADAM_EPS) + ADAM_WD * w_ref[0])
        m_o[0] = m1
        v_o[0] = v1

    blk = pl.BlockSpec((1, br, C), lambda l, i: (l, i, 0))
    p0 = pl.BlockSpec((n, br, C), lambda l, i: (0, jnp.where(l == 0, i, nb - 1), 0))
    p1 = pl.BlockSpec((n, br, C), lambda l, i: (0, jnp.where(l == 1, i, 0), 0))
    return _pc(body, name, [_sds((2, R, C))] * 4, grid=(2, nb), in_specs=[p0, p1, blk, blk, blk],
               out_specs=[blk] * 4, sends=sends)(P0, P1, w, m, v)


def mod_fwd(scin, wmod, bcol):
    def body(s_ref, w_ref, b_ref, o_ref):
        o_ref[0] = mm(_silu(s_ref[...]), w_ref[0]) + b_ref[0]

    return _pc(body, "mod_fwd", _sds((2, 16, 768)), grid=(2,),
               in_specs=[pl.BlockSpec((16, D), lambda l: (0, 0)), pl.BlockSpec((1, D, 768), lambda l: (l, 0, 0)),
                         pl.BlockSpec((1, 1, 768), lambda l: (l, 0, 0))],
               out_specs=pl.BlockSpec((1, 16, 768), lambda l: (l, 0, 0)))(scin, wmod, bcol)


def mod_bwd(scin, wmod, G):
    def body(s_ref, w_ref, g_ref, dw_o, ds_o):
        _, vjp = jax.vjp(lambda s, w: mm(_silu(s), w), s_ref[...], w_ref[0])
        ds, dw = vjp(g_ref[0])
        dw_o[0] = dw
        _acc_init(pl.program_id(0) == 0, [ds_o])
        ds_o[...] += ds

    full = pl.BlockSpec((16, D), lambda l: (0, 0))
    wsp = pl.BlockSpec((1, D, 768), lambda l: (l, 0, 0))
    return _pc(body, "mod_bwd", [_sds((2, D, 768)), _sds((16, D))], grid=(2,),
               in_specs=[full, wsp, pl.BlockSpec((1, 16, 768), lambda l: (l, 0, 0))], out_specs=[wsp, full])(
        scin, wmod, G)


_SMALL = ["b_mod", "g_mix", "wa_sink", "na_rpb", "ssm_conv_w", "ssm_conv_b", "ssm_dt_bias", "ssm_a_log", "ssm_d",
          "ssm_norm_g", "g_ffn", "g_final", "dmod_s", "dmod_c"]


def _pack(parts):
    rows = []
    for a in parts:
        f = a.reshape(-1).astype(F32)
        rows.append(jnp.pad(f, (0, (-f.shape[0]) % 1024)).reshape(-1, 128))
    return jnp.concatenate(rows, axis=0)


def _unpack(packed, shapes):
    out, r = [], 0
    for s in shapes:
        nel = int(np.prod(s))
        nr = -(-nel // 1024) * 8
        out.append(packed[r:r + nr].reshape(-1)[:nel].reshape(s))
        r += nr
    return out


def kernel(x, c, ctx, c_ctx, w_mod, b_mod, g_mix, w_in, wa_sink, na_rpb, ssm_conv_w, ssm_conv_b, ssm_dt_bias, ssm_a_log, ssm_d, ssm_norm_g, w_out, g_ffn, w_ffn_in, w_ffn_out, g_final, loss_target, m_c_ctx, m_w_mod, m_b_mod, m_g_mix, m_w_in, m_wa_sink, m_na_rpb, m_ssm_conv_w, m_ssm_conv_b, m_ssm_dt_bias, m_ssm_a_log, m_ssm_d, m_ssm_norm_g, m_w_out, m_g_ffn, m_w_ffn_in, m_w_ffn_out, m_g_final, v_c_ctx, v_w_mod, v_b_mod, v_g_mix, v_w_in, v_wa_sink, v_na_rpb, v_ssm_conv_w, v_ssm_conv_b, v_ssm_dt_bias, v_ssm_a_log, v_ssm_d, v_ssm_norm_g, v_w_out, v_g_ffn, v_w_ffn_in, v_w_ffn_out, v_g_final):
    L = x.shape[1]
    px, py, pc = _place()
    me = 4 * px + 2 * py + pc
    W = dict(c_ctx=c_ctx, w_mod=w_mod, b_mod=b_mod, g_mix=g_mix, w_in=w_in, wa_sink=wa_sink, na_rpb=na_rpb,
             ssm_conv_w=ssm_conv_w, ssm_conv_b=ssm_conv_b, ssm_dt_bias=ssm_dt_bias, ssm_a_log=ssm_a_log, ssm_d=ssm_d,
             ssm_norm_g=ssm_norm_g, w_out=w_out, g_ffn=g_ffn, w_ffn_in=w_ffn_in, w_ffn_out=w_ffn_out, g_final=g_final)
    M = dict(c_ctx=m_c_ctx, w_mod=m_w_mod, b_mod=m_b_mod, g_mix=m_g_mix, w_in=m_w_in, wa_sink=m_wa_sink,
             na_rpb=m_na_rpb, ssm_conv_w=m_ssm_conv_w, ssm_conv_b=m_ssm_conv_b, ssm_dt_bias=m_ssm_dt_bias,
             ssm_a_log=m_ssm_a_log, ssm_d=m_ssm_d, ssm_norm_g=m_ssm_norm_g, w_out=m_w_out, g_ffn=m_g_ffn,
             w_ffn_in=m_w_ffn_in, w_ffn_out=m_w_ffn_out, g_final=m_g_final)
    V = dict(c_ctx=v_c_ctx, w_mod=v_w_mod, b_mod=v_b_mod, g_mix=v_g_mix, w_in=v_w_in, wa_sink=v_wa_sink,
             na_rpb=v_na_rpb, ssm_conv_w=v_ssm_conv_w, ssm_conv_b=v_ssm_conv_b, ssm_dt_bias=v_ssm_dt_bias,
             ssm_a_log=v_ssm_a_log, ssm_d=v_ssm_d, ssm_norm_g=v_ssm_norm_g, w_out=v_w_out, g_ffn=v_g_ffn,
             w_ffn_in=v_w_ffn_in, w_ffn_out=v_w_ffn_out, g_final=v_g_final)

    c_all, conv_all = all_gather([c, ssm_conv_w], "gather_small")
    tr = lambda a: a.transpose(0, 2, 1)
    shards = dict(w_in=tr(w_in).astype(MXU), w_out=w_out.astype(MXU), w_ffn_in=tr(w_ffn_in).astype(MXU),
                  w_ffn_out=w_ffn_out.astype(MXU))
    conv_f = conv_all.transpose(1, 2, 0, 3).reshape(2, 7, 1024)

    scin = jnp.concatenate([c_all.reshape(NDEV, D), c_ctx.reshape(1, D), jnp.zeros((7, D), F32)], axis=0)
    bcol = lax.dynamic_slice_in_dim(b_mod, me * 768, 768, axis=1).reshape(2, 1, 768)
    mod_all, = all_gather([mod_fwd(scin, w_mod, bcol)], "gather_mod")
    mod_rows = mod_all.transpose(1, 2, 0, 3).reshape(2, 16, 6 * D)
    mods = jnp.stack([lax.dynamic_index_in_dim(mod_rows, me, axis=1, keepdims=False), mod_rows[:, 8]], axis=1)

    layers = [dict(g_mix=g_mix[i], wa_sink=wa_sink[i], na_rpb=na_rpb[i], ssm_conv_w=conv_f[i],
                   ssm_conv_b=ssm_conv_b[i], ssm_dt_bias=ssm_dt_bias[i], ssm_a_log=ssm_a_log[i], ssm_d=ssm_d[i],
                   ssm_norm_g=ssm_norm_g[i], g_ffn=g_ffn[i]) for i in range(2)]
    loss, dx, grads, dmods, dgfin, gin0 = local_step(x[0], ctx[0], loss_target[0], mods, layers, shards, g_final, L)
    third = IN_COLS // NDEV // 3
    gin0 = [gin0[:, third * k:third * (k + 1)] for k in range(3)]
    loss = lax.psum(loss, ("x", "y", "c"))

    stk = lambda n: jnp.stack([grads[0][n], grads[1][n]])
    small = dict(b_mod=dmods[:, 0] + dmods[:, 1], g_final=dgfin, dmod_s=dmods[:, 0], dmod_c=dmods[:, 1])
    for nme in _SMALL:
        if nme not in small:
            small[nme] = stk(nme)
    shapes = [small[nme].shape for nme in _SMALL]
    zero_like = lambda nme: jnp.zeros(small[nme].shape, F32)
    own = lambda S, nme: S[nme] if (nme in S and S[nme].shape == small[nme].shape) else zero_like(nme)
    gath, = all_gather([_pack([small[nme] for nme in _SMALL])], "gather_grads")
    sm = adam_reduce(gath, _pack([own(W, nme) for nme in _SMALL]), _pack([own(M, nme) for nme in _SMALL]),
                     _pack([own(V, nme) for nme in _SMALL]), "adam_small")
    res = {nme: vals for nme, vals in zip(_SMALL, zip(*[_unpack(a, shapes) for a in sm]))}

    cols = lambda a: lax.dynamic_slice_in_dim(a, me * 768, 768, axis=-1)
    gparts = [_unpack(gath[d], shapes) for d in range(NDEV)]
    dmod_s_all = jnp.stack([gparts[d][_SMALL.index("dmod_s")] for d in range(NDEV)], axis=1)
    G = jnp.concatenate([cols(dmod_s_all), cols(res["dmod_c"][0])[:, None, :], jnp.zeros((2, 7, 768), F32)], axis=1)
    dwmod, dscin = mod_bwd(scin, w_mod, G)
    cc_g, = all_gather([dscin[8].reshape(8, 128)], "gather_cctx")
    out = {}
    out["c_ctx"] = [a.reshape(D) for a in adam_reduce(cc_g, c_ctx.reshape(8, 128), m_c_ctx.reshape(8, 128),
                                                      v_c_ctx.reshape(8, 128), "adam_cctx")]
    res_wmod, (got1,) = adam_reduce(dwmod.reshape(1, 2 * D, 768), w_mod.reshape(2 * D, 768),
                                    m_w_mod.reshape(2 * D, 768), v_w_mod.reshape(2 * D, 768), "adam_wmod",
                                    sends=(gin0[1],))
    out["w_mod"] = [a.reshape(2, D, 768) for a in res_wmod]
    gconv = lax.dynamic_slice_in_dim(res["ssm_conv_w"][0], me * 128, 128, axis=2)
    out["ssm_conv_w"] = [a.reshape(2, 7, 128) for a in adam_reduce(
        gconv.reshape(1, 14, 128), ssm_conv_w.reshape(14, 128), m_ssm_conv_w.reshape(14, 128),
        v_ssm_conv_w.reshape(14, 128), "adam_conv")]
    for nme in _SMALL:
        if nme not in ("ssm_conv_w", "dmod_s", "dmod_c"):
            out[nme] = list(res[nme])

    adam_big = lambda nme, t, **kw: adam_layers(grads[0][nme], grads[1][nme], t(W[nme]), t(M[nme]), t(V[nme]),
                                                "adam_" + nme, **kw)
    same = lambda a: a
    res_fi, (got0,) = adam_big("w_ffn_in", tr, sends=(gin0[0],))
    res_fo, (got2,) = adam_big("w_ffn_out", same, sends=(gin0[2],))
    grads[0]["w_in"] = jnp.concatenate([got0, got1, got2], axis=1)
    out["w_ffn_in"] = [tr(a) for a in res_fi]
    out["w_ffn_out"] = list(res_fo)
    out["w_out"] = list(adam_big("w_out", same))
    out["w_in"] = [tr(a) for a in adam_big("w_in", tr)]
    order = ["c_ctx", "w_mod", "b_mod", "g_mix", "w_in", "wa_sink", "na_rpb", "ssm_conv_w", "ssm_conv_b",
             "ssm_dt_bias", "ssm_a_log", "ssm_d", "ssm_norm_g", "w_out", "g_ffn", "w_ffn_in", "w_ffn_out", "g_final"]
    return (loss, dx.reshape(1, L, D), *[out[nme][0] for nme in order], *[out[nme][1] for nme in order],
            *[out[nme][2] for nme in order], *[out[nme][3] for nme in order])
```

```python
import functools
import math

import numpy as np
import jax
import jax.numpy as jnp
from jax import lax
from jax.experimental import pallas as pl
from jax.experimental.pallas import tpu as pltpu

F32 = jnp.float32
MXU = jnp.bfloat16
_INTERPRET = False
VMEM_LIMIT = 60 * 1024 * 1024

D = 1024
LC = 256
GW = 64
HD = 64
EPS = 1e-6
NEG = -1e30
NDEV = 8
Q = 128
NSTATE = 128
DFF = 2816
IN_COLS = 2832
NP_IN = 3072
C_QA, C_QB, C_Z, C_KA, C_VA, C_KB, C_VB, C_XBC, C_DT = 0, 256, 512, 1024, 1152, 1280, 1536, 1792, 2816
ADAM_LR, ADAM_B1, ADAM_B2, ADAM_EPS, ADAM_WD, ADAM_STEP = 0.001, 0.9, 0.999, 1e-08, 0.01, 10
MESH_T = pl.DeviceIdType.MESH


def _dg(a, b, ca, cb):
    return lax.dot_general(a.astype(MXU), b.astype(MXU), (((ca,), (cb,)), ((), ())), preferred_element_type=F32)


@jax.custom_vjp
def mm(a, b):
    return _dg(a, b, 1, 0)


def _mm_f(a, b):
    return _dg(a, b, 1, 0), (a, b)


def _mm_b(res, g):
    a, b = res
    return _dg(g, b, 1, 1).astype(a.dtype), _dg(a, g, 0, 0).astype(b.dtype)


mm.defvjp(_mm_f, _mm_b)


@jax.custom_vjp
def mm_nt(a, b):
    return _dg(a, b, 1, 1)


def _mmnt_f(a, b):
    return _dg(a, b, 1, 1), (a, b)


def _mmnt_b(res, g):
    a, b = res
    return _dg(g, b, 1, 0).astype(a.dtype), _dg(g, a, 0, 0).astype(b.dtype)


mm_nt.defvjp(_mmnt_f, _mmnt_b)


@jax.custom_vjp
def mm_tn(a, b):
    return _dg(a, b, 0, 0)


def _mmtn_f(a, b):
    return _dg(a, b, 0, 0), (a, b)


def _mmtn_b(res, g):
    a, b = res
    return _dg(b, g, 1, 1).astype(a.dtype), _dg(a, g, 1, 0).astype(b.dtype)


mm_tn.defvjp(_mmtn_f, _mmtn_b)


@jax.custom_vjp
def mmw(a, w):
    return _dg(a, w, 1, 0)


mmw.defvjp(lambda a, w: (_dg(a, w, 1, 0), w), lambda w, g: (_dg(g, w, 1, 1), None))


@jax.custom_vjp
def mmw_nt(a, w):
    return _dg(a, w, 1, 1)


mmw_nt.defvjp(lambda a, w: (_dg(a, w, 1, 1), w), lambda w, g: (_dg(g, w, 1, 0), None))


def _exact(a, b):
    return lax.dot_general(a, b, (((1,), (0,)), ((), ())), precision=lax.Precision.HIGHEST,
                           preferred_element_type=F32)


def _pc(body, name, out_shape, grid=None, in_specs=None, out_specs=None, scratch=(), sends=(), gather=False):
    params = pltpu.CompilerParams(vmem_limit_bytes=VMEM_LIMIT)
    if sends and not isinstance(out_shape, (list, tuple)):
        out_shape, out_specs = [out_shape], [out_specs]
    start, wait = (_ag_start, _ag_wait) if gather else (_a2a_start, _a2a_wait)
    if not sends:
        kw = {}
        if grid is not None:
            kw = dict(grid=grid, in_specs=in_specs, out_specs=out_specs)
        elif in_specs is not None:
            kw = dict(in_specs=in_specs, out_specs=out_specs)
        return pl.pallas_call(body, name=name, out_shape=out_shape, scratch_shapes=list(scratch),
                              compiler_params=params, interpret=_INTERPRET, **kw)
    n, nin, nout, nscr = len(sends), len(in_specs), len(out_shape), len(scratch)

    def body2(*refs):
        cin, xs = refs[:nin], refs[nin:nin + n]
        couts, os_ = refs[nin + n:nin + n + nout], refs[nin + n + nout:nin + 2 * n + nout]
        cscr, sems = refs[nin + 2 * n + nout:nin + 2 * n + nout + nscr], refs[nin + 2 * n + nout + nscr:]
        ids = [pl.program_id(a) for a in range(len(grid))]
        first = functools.reduce(lambda a, b: a & b, [i == 0 for i in ids])
        last = functools.reduce(lambda a, b: a & b, [i == g - 1 for i, g in zip(ids, grid)])

        @pl.when(first)
        def _():
            start(xs, os_, *sems)

        body(*cin, *couts, *cscr)

        @pl.when(last)
        def _():
            wait(xs, os_, *sems)

    call = pl.pallas_call(
        body2, name=name,
        out_shape=list(out_shape) + [_sds(((NDEV,) if gather else ()) + a.shape, a.dtype) for a in sends],
        grid=grid, in_specs=list(in_specs) + [_any()] * n, out_specs=list(out_specs) + [_any()] * n,
        scratch_shapes=list(scratch) + _a2a_sems(n), compiler_params=params, interpret=_INTERPRET)

    def run(*args):
        res = call(*args, *sends)
        return res[:nout], res[nout:]

    return run


def _vm():
    return pl.BlockSpec(memory_space=pltpu.VMEM)


def _sds(shape, dt=F32):
    return jax.ShapeDtypeStruct(shape, dt)


def _iota(shape, dim):
    return lax.broadcasted_iota(jnp.int32, shape, dim)


def _silu(x):
    return x * jax.nn.sigmoid(x)


def _softplus(x):
    return jnp.maximum(x, 0.0) + jnp.log1p(jnp.exp(-jnp.abs(x)))


def _normmod(x, g, sh, sc):
    r = lax.rsqrt(jnp.mean(x * x, axis=-1, keepdims=True) + EPS)
    return (x * r * g) * (1.0 + sc) + sh


def _rope(x, cos, sin, rm):
    return x * cos + _exact(x, rm) * sin


def _swap12(x):
    lane = _iota(x.shape, 1)
    up, down = pltpu.roll(x, 192, 1), pltpu.roll(x, 64, 1)
    return jnp.where((lane >= 64) & (lane < 128), up, jnp.where((lane >= 128) & (lane < 192), down, x))


def _acc_init(first, refs):
    @pl.when(first)
    def _():
        for r in refs:
            r[...] = jnp.zeros_like(r)


def _stream(X, TR, nlt):
    if not isinstance(X, tuple):
        return (X,), [pl.BlockSpec((TR, D), lambda i: (i, 0))], lambda refs: refs[0][...]
    specs = [pl.BlockSpec((TR, D), lambda i: (jnp.minimum(i, nlt - 1), 0)), pl.BlockSpec((TR, D), lambda i: (0, 0))]
    return X, specs, lambda refs: jnp.where(pl.program_id(0) < nlt, refs[0][...], refs[1][...])


def in_fwd(X, g, sh, sc, W, cos, sin, rm, L, sends=()):
    T = L + LC
    TR = 256
    nlt = L // TR
    xs, xspecs, xread = _stream(X, TR, nlt)

    def body(*refs):
        (g_ref, sh_ref, sc_ref, w_ref, cos_ref, sin_ref, rm_ref,
         qa, qb, z, ka, va, kb, vb, xbc, dt, hout) = refs[len(xs):]
        h = _normmod(xread(refs), g_ref[...], sh_ref[0], sc_ref[0]).astype(MXU)
        hout[...] = h
        y = lax.dot_general(h, w_ref[...], (((1,), (1,)), ((), ())), preferred_element_type=F32)
        cs, sn, r = cos_ref[...], sin_ref[...], rm_ref[...]
        qa[...] = _rope(_swap12(y[:, C_QA:C_QB]), cs, sn, r).astype(MXU)
        qb[...] = y[:, C_QB:C_Z].astype(MXU)
        z[...] = y[:, C_Z:C_KA]
        ka[...] = _rope(y[:, C_KA:C_VA], cs[:, :128], sn[:, :128], r[:128, :128]).astype(MXU)
        va[...] = y[:, C_VA:C_KB].astype(MXU)
        kb[...] = y[:, C_KB:C_VB].astype(MXU)
        vb[...] = y[:, C_VB:C_XBC].astype(MXU)
        xbc[...] = y[:, C_XBC:C_DT]
        dt[...] = y[:, C_DT:C_DT + 128]

    row = lambda w: pl.BlockSpec((TR, w), lambda i: (i, 0))
    cls = pl.BlockSpec((1, 1, D), lambda i: (i // nlt, 0, 0))
    widths = [(256, MXU), (256, MXU), (512, F32), (128, MXU), (128, MXU), (256, MXU), (256, MXU), (1024, F32),
              (128, F32), (D, MXU)]
    return _pc(body, "in_fwd", [_sds((T, w), d) for w, d in widths], grid=(T // TR,),
               in_specs=xspecs + [pl.BlockSpec((1, D), lambda i: (0, 0)), cls, cls, _vm(), row(256), row(256), _vm()],
               out_specs=[row(w) for w, _ in widths], sends=sends, gather=True)(*xs, g, sh, sc, W, cos, sin, rm)


def in_bwd(X, g, sh, sc, W, cos, sin, rm, dxres, dqa, dqb, dz, dka, dva, dkb, dvb, dxbc, ddt2, L, latent_only):
    T = L + LC
    TR = 256
    nlt = L // TR
    xs, xspecs, xread = _stream(X, TR, nlt)

    def body(*refs):
        (g_ref, sh_ref, sc_ref, w_ref, cos_ref, sin_ref, rm_ref, dxres_ref, dqa_r, dqb_r, dz_r, dka_r,
         dva_r, dkb_r, dvb_r, dxbc_r, ddt0_r, ddt1_r, dx_o, dy_o, dg_o, dsh_o, dsc_o) = refs[len(xs):]
        i = pl.program_id(0)
        cs, sn, r = cos_ref[...], sin_ref[...], rm_ref[...]
        _, vq = jax.vjp(lambda t: _rope(t, cs, sn, r), dqa_r[...])
        _, vk = jax.vjp(lambda t: _rope(t, cs[:, :128], sn[:, :128], r[:128, :128]), dka_r[...])
        dyqa = _swap12(vq(dqa_r[...])[0])
        dyka, = vk(dka_r[...])
        ddt = ddt0_r[0] + ddt1_r[0]
        dy = jnp.concatenate([dyqa, dqb_r[...], dz_r[...], dyka, dva_r[...], dkb_r[...], dvb_r[...], dxbc_r[...],
                              ddt, jnp.zeros((TR, NP_IN - C_DT - 128), F32)], axis=1).astype(MXU)
        dy_o[...] = dy
        dh = jnp.dot(dy, w_ref[...], preferred_element_type=F32)
        _, vp = jax.vjp(_normmod, xread(refs), g_ref[...], sh_ref[0], sc_ref[0])
        dx, dg, dsh, dsc = vp(dh)
        if latent_only:
            @pl.when(i < nlt)
            def _():
                dx_o[...] = dx + dxres_ref[...]
        else:
            dx_o[...] = dx + dxres_ref[...]
        _acc_init(i == 0, [dg_o])
        _acc_init((i == 0) | (i == nlt), [dsh_o, dsc_o])
        dg_o[...] += dg
        dsh_o[0] += dsh
        dsc_o[0] += dsc

    row = lambda w: pl.BlockSpec((TR, w), lambda i: (i, 0))
    cls = pl.BlockSpec((1, 1, D), lambda i: (i // nlt, 0, 0))
    vec = pl.BlockSpec((1, D), lambda i: (0, 0))
    dts = lambda d: pl.BlockSpec((1, TR, 128), lambda i: (d, i, 0))
    dxs = pl.BlockSpec((TR, D), lambda i: (jnp.minimum(i, nlt - 1), 0)) if latent_only else row(D)
    return _pc(body, "in_bwd",
               [_sds((L if latent_only else T, D)), _sds((T, NP_IN), MXU), _sds((1, D)), _sds((2, 1, D)),
                _sds((2, 1, D))],
               grid=(T // TR,),
               in_specs=xspecs + [vec, cls, cls, _vm(), row(256), row(256), _vm(), row(D), row(256), row(256),
                                  row(512), row(128), row(128), row(256), row(256), row(1024), dts(0), dts(1)],
               out_specs=[dxs, row(NP_IN), vec, cls, cls])(
        *xs, g, sh, sc, W, cos, sin, rm, dxres, dqa, dqb, dz, dka, dva, dkb, dvb, dxbc, ddt2, ddt2)


def tn_mm(A, G, bk, bn, out_dtype, ncol=None, col0=0):
    T, K = A.shape
    N = G.shape[1] if ncol is None else ncol
    first = col0 * (N // bn)
    bt = T
    nt = T // bt

    def body(a_ref, g_ref, o_ref, acc):
        t = pl.program_id(2)
        _acc_init(t == 0, [acc])
        acc[...] += lax.dot_general(a_ref[...], g_ref[...], (((0,), (0,)), ((), ())), preferred_element_type=F32)

        @pl.when(t == nt - 1)
        def _():
            o_ref[...] = acc[...].astype(out_dtype)

    return _pc(body, "tn_mm", _sds((K, N), out_dtype), grid=(K // bk, N // bn, nt),
               in_specs=[pl.BlockSpec((bt, bk), lambda k, n, t: (t, k)),
                         pl.BlockSpec((bt, bn), lambda k, n, t: (t, first + n))],
               out_specs=pl.BlockSpec((bk, bn), lambda k, n, t: (k, n)),
               scratch=[pltpu.VMEM((bk, bn), F32)])(A, G)


def _ssm_out(yf, yb, xs, z, dsk, gs):
    y = (yf + yb + dsk * xs) * _silu(z)
    r = lax.rsqrt(jnp.mean(y * y, axis=-1, keepdims=True) + EPS)
    return y * r * gs


def out_fwd(oa, ob, y2, act, z, dsk, gs, W, X, gate, L, sends=()):
    T = L + LC
    TR = 256
    nlt = L // TR
    xs, xspecs, xread = _stream(X, TR, nlt)

    def body(*refs):
        oa_r, ob_r, yf_r, yb_r, xs_r, z_r, dsk_r, gs_r, w_ref, gt_ref, x1_o, cat_o = refs[len(xs):]
        oc = _ssm_out(yf_r[0], yb_r[0], xs_r[...], z_r[...], dsk_r[...], gs_r[...])
        cat = jnp.concatenate([_swap12(oa_r[...]), ob_r[...], oc], axis=1).astype(MXU)
        cat_o[...] = cat
        x1_o[...] = xread(refs) + gt_ref[0] * jnp.dot(cat, w_ref[...], preferred_element_type=F32)

    row = lambda w: pl.BlockSpec((TR, w), lambda i: (i, 0))
    ys = lambda d: pl.BlockSpec((1, TR, 512), lambda i: (d, i, 0))
    cls = pl.BlockSpec((1, 1, D), lambda i: (i // nlt, 0, 0))
    v512 = pl.BlockSpec((1, 512), lambda i: (0, 0))
    return _pc(body, "out_fwd", [_sds((T, D)), _sds((T, D), MXU)], grid=(T // TR,),
               in_specs=xspecs + [row(256), row(256), ys(0), ys(1), row(512), row(512), v512, v512, _vm(), cls],
               out_specs=[row(D), row(D)], sends=sends, gather=True)(*xs, oa, ob, y2, y2, act, z, dsk, gs, W, gate)


def out_bwd(oa, ob, y2, act, z, dsk, gs, W, gate, dX1, L):
    T = dX1.shape[0]
    TR = 256
    nlt = L // TR

    def body(oa_r, ob_r, yf_r, yb_r, xs_r, z_r, dsk_r, gs_r, w_ref, gt_ref, dx1_r,
             doa_o, dob_o, dy_o, dxs_o, dz_o, dmix_o, ddsk_o, dgs_o, dgt_o):
        i = pl.program_id(0)
        w = w_ref[...]

        def f(oa_, ob_, yf, yb, xs, z_, dsk_, gs_, gt):
            oc = _ssm_out(yf, yb, xs, z_, dsk_, gs_)
            return gt * mmw(jnp.concatenate([oa_, ob_, oc], axis=1), w)

        _, vjp = jax.vjp(f, _swap12(oa_r[...]), ob_r[...], yf_r[0], yb_r[0], xs_r[...], z_r[...], dsk_r[...],
                         gs_r[...], gt_ref[0])
        dx1 = dx1_r[...]
        doa, dob, dyf, _, dxs, dz, ddsk, dgs, dgt = vjp(dx1)
        doa_o[...] = _swap12(doa)
        dob_o[...] = dob
        dy_o[...] = dyf
        dxs_o[...] = dxs
        dz_o[...] = dz
        dmix_o[...] = (gt_ref[0] * dx1).astype(MXU)
        _acc_init(i == 0, [ddsk_o, dgs_o])
        _acc_init((i == 0) | (i == nlt), [dgt_o])
        ddsk_o[...] += ddsk
        dgs_o[...] += dgs
        dgt_o[0] += dgt

    row = lambda w: pl.BlockSpec((TR, w), lambda i: (i, 0))
    ys = lambda d: pl.BlockSpec((1, TR, 512), lambda i: (d, i, 0))
    cls = pl.BlockSpec((1, 1, D), lambda i: (i // nlt, 0, 0))
    v512 = pl.BlockSpec((1, 512), lambda i: (0, 0))
    return _pc(body, "out_bwd",
               [_sds((T, 256)), _sds((T, 256)), _sds((T, 512)), _sds((T, 512)), _sds((T, 512)), _sds((T, D), MXU),
                _sds((1, 512)), _sds((1, 512)), _sds((2, 1, D))],
               grid=(T // TR,),
               in_specs=[row(256), row(256), ys(0), ys(1), row(512), row(512), v512, v512, _vm(), cls, row(D)],
               out_specs=[row(256), row(256), row(512), row(512), row(512), row(D), v512, v512, cls])(
        oa, ob, y2, y2, act, z, dsk, gs, W, gate, dX1)


def ffn_fwd(X, g, sh, sc, gate, Win, Wout, L, sends=()):
    T = X.shape[0]
    TR = 256
    nlt = L // TR

    def body(x_ref, g_ref, sh_ref, sc_ref, gt_ref, wi_ref, wo_ref, o_ref):
        h = _normmod(x_ref[...], g_ref[...], sh_ref[0], sc_ref[0]).astype(MXU)
        nt = (((1,), (1,)), ((), ()))
        a = lax.dot_general(h, wi_ref[0:DFF, :], nt, preferred_element_type=F32)
        u = lax.dot_general(h, wi_ref[DFF:2 * DFF, :], nt, preferred_element_type=F32)
        act = (_silu(a) * u).astype(MXU)
        o_ref[...] = x_ref[...] + gt_ref[0] * jnp.dot(act, wo_ref[...], preferred_element_type=F32)

    row = lambda w: pl.BlockSpec((TR, w), lambda i: (i, 0))
    cls = pl.BlockSpec((1, 1, D), lambda i: (i // nlt, 0, 0))
    vec = pl.BlockSpec((1, D), lambda i: (0, 0))
    return _pc(body, "ffn_fwd", _sds((T, D)), grid=(T // TR,),
               in_specs=[row(D), vec, cls, cls, cls, _vm(), _vm()], out_specs=row(D), sends=sends, gather=True)(
        X, g, sh, sc, gate, Win, Wout)


def ffn_bwd(X, g, sh, sc, gate, Win, Wout, dX2, L, sends=(), nchunk=2):
    T = X.shape[0]
    TR = 256
    nlt = L // TR
    CH = DFF // nchunk

    def body(x_ref, g_ref, sh_ref, sc_ref, gt_ref, wi_ref, wo_ref, dx2_r,
             dx_o, h_o, du_o, act_o, dout_o, dg_o, dsh_o, dsc_o, dgt_o):
        i = pl.program_id(0)
        h, vp = jax.vjp(_normmod, x_ref[...], g_ref[...], sh_ref[0], sc_ref[0])
        dx2 = dx2_r[...]
        dout = gt_ref[0] * dx2
        zero = jnp.zeros((TR, CH), F32)
        dh = jnp.zeros((TR, D), F32)
        out = jnp.zeros((TR, D), F32)
        for c in range(nchunk):
            lo, hi = c * CH, (c + 1) * CH
            wg, wu, wo = wi_ref[lo:hi, :], wi_ref[DFF + lo:DFF + hi, :], wo_ref[lo:hi, :]

            def f(h_, eg, eu):
                act = _silu(mmw_nt(h_, wg) + eg) * (mmw_nt(h_, wu) + eu)
                return mmw(act, wo), act

            o_c, vjp_c, act = jax.vjp(f, h, zero, zero, has_aux=True)
            dh_c, da, du = vjp_c(dout)
            dh, out = dh + dh_c, out + o_c
            du_o[:, lo:hi] = da.astype(MXU)
            du_o[:, DFF + lo:DFF + hi] = du.astype(MXU)
            act_o[:, lo:hi] = act.astype(MXU)
        dx, dg, dsh, dsc = vp(dh)
        dx_o[...] = dx + dx2
        h_o[...] = h.astype(MXU)
        dout_o[...] = dout.astype(MXU)
        _acc_init(i == 0, [dg_o])
        _acc_init((i == 0) | (i == nlt), [dsh_o, dsc_o, dgt_o])
        dg_o[...] += dg
        dsh_o[0] += dsh
        dsc_o[0] += dsc
        dgt_o[0] += jnp.sum(dx2 * out, axis=0, keepdims=True)

    row = lambda w: pl.BlockSpec((TR, w), lambda i: (i, 0))
    cls = pl.BlockSpec((1, 1, D), lambda i: (i // nlt, 0, 0))
    vec = pl.BlockSpec((1, D), lambda i: (0, 0))
    return _pc(body, "ffn_bwd",
               [_sds((T, D)), _sds((T, D), MXU), _sds((T, 2 * DFF), MXU), _sds((T, DFF), MXU), _sds((T, D), MXU),
                _sds((1, D)), _sds((2, 1, D)), _sds((2, 1, D)), _sds((2, 1, D))],
               grid=(T // TR,),
               in_specs=[row(D), vec, cls, cls, cls, _vm(), _vm(), row(D)],
               out_specs=[row(D), row(D), row(2 * DFF), row(DFF), row(D), vec, cls, cls, cls], sends=sends)(
        X, g, sh, sc, gate, Win, Wout, dX2)


def loss_head(X2, g, tgt, L):
    T = X2.shape[0]
    TR = 256
    nlt = L // TR

    def body(x_ref, g_ref, t_ref, loss_o, dx_o, dg_o):
        i = pl.program_id(0)
        _acc_init(i == 0, [loss_o, dg_o])

        @pl.when(i < nlt)
        def _():
            def f(x, g_):
                y = x * lax.rsqrt(jnp.mean(x * x, axis=-1, keepdims=True) + EPS) * g_
                return 0.5 * jnp.sum(jnp.mean(jnp.square(y - t_ref[...]), axis=-1, keepdims=True), axis=0,
                                     keepdims=True)

            val, vjp = jax.vjp(f, x_ref[...], g_ref[...])
            dx, dg = vjp(jnp.ones((1, 1), F32))
            dx_o[...] = dx
            loss_o[...] += jnp.broadcast_to(val, (8, 128))
            dg_o[...] += dg

        @pl.when(i >= nlt)
        def _():
            dx_o[...] = jnp.zeros_like(dx_o)

    row = pl.BlockSpec((TR, D), lambda i: (i, 0))
    vec = pl.BlockSpec((1, D), lambda i: (0, 0))
    return _pc(body, "loss_head", [_sds((8, 128)), _sds((T, D)), _sds((1, D))], grid=(T // TR,),
               in_specs=[row, vec, pl.BlockSpec((TR, D), lambda i: (jnp.minimum(i, nlt - 1), 0))],
               out_specs=[pl.BlockSpec((8, 128), lambda i: (0, 0)), row, vec])(X2, g, tgt)


def _stack_impl(q):
    lane = _iota(q.shape, 1)
    return jnp.concatenate([jnp.where(lane < HD, q, 0.0), jnp.where(lane >= HD, q, 0.0)], axis=0)


def _unstack_impl(o):
    M = o.shape[0] // 2
    return jnp.where(_iota((M, o.shape[1]), 1) < HD, o[:M], o[M:])


@jax.custom_vjp
def _stack(q):
    return _stack_impl(q)


_stack.defvjp(lambda q: (_stack_impl(q), None), lambda _, g: (_unstack_impl(g),))


@jax.custom_vjp
def _unstack(o):
    return _unstack_impl(o)


_unstack.defvjp(lambda o: (_unstack_impl(o), None), lambda _, g: (_stack_impl(g),))


def _softmax_av(q, ks, vs, biases, sink):
    q2 = _stack(q)
    ss = []
    for k, b in zip(ks, biases):
        s = mm_nt(q2, k) * (HD ** -0.5)
        ss.append(s if b is None else s + b)
    m = functools.reduce(jnp.maximum, [jnp.max(s, axis=1, keepdims=True) for s in ss])
    if sink is not None:
        m = jnp.maximum(m, sink)
    m = lax.stop_gradient(m)
    es = [jnp.exp(s - m) for s in ss]
    den = functools.reduce(lambda a, b_: a + b_, [jnp.sum(e, axis=1, keepdims=True) for e in es])
    if sink is not None:
        den = den + jnp.exp(sink - m)
    inv = 1.0 / den
    return _unstack(functools.reduce(lambda a, b_: a + b_, [mm(e * inv, v) for e, v in zip(es, vs)]))


def _sink_col(s0, s1, M):
    return jnp.concatenate([jnp.broadcast_to(jnp.mean(s0, axis=1, keepdims=True), (M, 1)),
                            jnp.broadcast_to(jnp.mean(s1, axis=1, keepdims=True), (M, 1))], axis=0)


def _stack4_impl(q):
    lane = _iota((q.shape[0], 128), 1)
    parts = []
    for p in range(2):
        qp = q[:, 128 * p:128 * (p + 1)]
        parts += [jnp.where(lane < HD, qp, 0.0), jnp.where(lane >= HD, qp, 0.0)]
    return jnp.concatenate(parts, axis=0)


def _unstack4_impl(o):
    M = o.shape[0] // 4
    lane = _iota((M, 128), 1)
    return jnp.concatenate([jnp.where(lane < HD, o[0:M], o[M:2 * M]),
                            jnp.where(lane < HD, o[2 * M:3 * M], o[3 * M:4 * M])], axis=1)


@jax.custom_vjp
def _stack4(q):
    return _stack4_impl(q)


_stack4.defvjp(lambda q: (_stack4_impl(q), None), lambda _, g: (_unstack4_impl(g),))


@jax.custom_vjp
def _unstack4(o):
    return _unstack4_impl(o)


_unstack4.defvjp(lambda o: (_unstack4_impl(o), None), lambda _, g: (_stack4_impl(g),))


def _wa_block(q, kp, kc, kn, vp, vc, vn, kx, vx, sks, n, L):
    kb = jnp.concatenate([kp, kc, kn], axis=0)
    vb = jnp.concatenate([vp, vc, vn], axis=0)
    qpos = n * Q + (_iota((4 * Q, 3 * Q), 0) & (Q - 1))
    kpos = (n - 1) * Q + _iota((4 * Q, 3 * Q), 1)
    valid = (jnp.abs(qpos - kpos) <= Q) & (kpos >= 0) & (kpos < L)
    bias = jnp.where(valid, 0.0, NEG)
    sink = jnp.concatenate([jnp.broadcast_to(jnp.mean(s_, axis=1, keepdims=True), (Q, 1)) for s_ in sks], axis=0)
    q4 = _stack4(q)
    sc = HD ** -0.5
    sl = mm_nt(q4, kb) * sc + bias
    sx = mm_nt(q4, kx) * sc
    m = lax.stop_gradient(jnp.maximum(jnp.maximum(jnp.max(sl, axis=1, keepdims=True),
                                                  jnp.max(sx, axis=1, keepdims=True)), sink))
    el, ex = jnp.exp(sl - m), jnp.exp(sx - m)
    inv = 1.0 / (jnp.sum(el, axis=1, keepdims=True) + jnp.sum(ex, axis=1, keepdims=True) + jnp.exp(sink - m))
    return _unstack4(mm(el * inv, vb) + mm(ex * inv, vx))


def _wa_specs(L):
    nb = L // Q
    qs = pl.BlockSpec((Q, 256), lambda n: (n, 0))
    kprev = pl.BlockSpec((Q, 128), lambda n: (jnp.maximum(n - 1, 0), 0))
    kcur = pl.BlockSpec((Q, 128), lambda n: (n, 0))
    knext = pl.BlockSpec((Q, 128), lambda n: (jnp.minimum(n + 1, nb - 1), 0))
    kctx = pl.BlockSpec((LC, 128), lambda n: (L // LC, 0))
    sks = pl.BlockSpec((2, 2, 1, 128), lambda n: (0, 0, 0, 0))
    return nb, qs, [kprev, kcur, knext], kctx, sks


def wa_fwd(QA, KA, VA, sinkp, L, sends=()):
    nb, qs, kband, kctx, sks = _wa_specs(L)

    def body(q_r, kp, kc, kn, vp, vc, vn, kx, vx, sk_r, o_ref):
        n = pl.program_id(0)
        f = lambda t: t[...].astype(F32)
        o_ref[...] = _wa_block(f(q_r), f(kp), f(kc), f(kn), f(vp), f(vc), f(vn), f(kx), f(vx),
                               [sk_r[0, 0], sk_r[0, 1], sk_r[1, 0], sk_r[1, 1]], n, L)

    return _pc(body, "wa_fwd", _sds((L, 256)), grid=(nb,),
               in_specs=[qs] + kband + kband + [kctx, kctx, sks], out_specs=qs, sends=sends, gather=True)(
        QA, KA, KA, KA, VA, VA, VA, KA, VA, sinkp)


def wa_bwd(QA, KA, VA, sinkp, dO, L, sends=()):
    nb, qs, kband, kctx, sks = _wa_specs(L)

    def body(q_r, kp, kc, kn, vp, vc, vn, kx, vx, sk_r, do_r, dq_o, dk_o, dv_o, dkx_o, dvx_o, dsk_o):
        n = pl.program_id(0)
        f = lambda t: t[...].astype(F32)
        fn = lambda q, a, b, c, d, e, g, kx_, vx_, s_: _wa_block(q, a, b, c, d, e, g, kx_, vx_, s_, n, L)
        _, vjp = jax.vjp(fn, f(q_r), f(kp), f(kc), f(kn), f(vp), f(vc), f(vn), f(kx), f(vx),
                         [sk_r[0, 0], sk_r[0, 1], sk_r[1, 0], sk_r[1, 1]])
        dq, dkp, dkc, dkn, dvp, dvc, dvn, dkx, dvx, ds = vjp(do_r[...])
        dq_o[...] = dq
        _acc_init(n == 0, [dk_o, dv_o, dkx_o, dvx_o, dsk_o])
        rows = pl.ds(pl.multiple_of(n * Q, Q), 3 * Q)
        dk_o[rows, :] += jnp.concatenate([dkp, dkc, dkn], axis=0)
        dv_o[rows, :] += jnp.concatenate([dvp, dvc, dvn], axis=0)
        dkx_o[...] += dkx
        dvx_o[...] += dvx
        for i_ in range(4):
            dsk_o[i_ // 2, i_ % 2] += ds[i_]

    full = lambda r: pl.BlockSpec((r, 128), lambda n: (0, 0))
    return _pc(body, "wa_bwd",
               [_sds((L, 256)), _sds((L + 2 * Q, 128)), _sds((L + 2 * Q, 128)), _sds((LC, 128)), _sds((LC, 128)),
                _sds((2, 2, 1, 128))],
               grid=(nb,), in_specs=[qs] + kband + kband + [kctx, kctx, sks, qs],
               out_specs=[qs, full(L + 2 * Q), full(L + 2 * Q), full(LC), full(LC), sks], sends=sends)(
        QA, KA, KA, KA, VA, VA, VA, KA, VA, sinkp, dO)


def _ctx_block(q, kx, vx, s0, s1):
    return _softmax_av(q, [kx], [vx], [None], _sink_col(s0, s1, LC))


def ctx_fwd(Qx, Kx, Vx, sinkp, shared, L):
    cq = pl.BlockSpec((LC, 128), lambda p: (L // LC, p))
    ck = pl.BlockSpec((LC, 128), lambda p: (L // LC, 0 if shared else p))
    sks = pl.BlockSpec((1, 2, 1, 128), lambda p: (p, 0, 0, 0))

    def body(q_r, k_r, v_r, sk_r, o_ref):
        f = lambda t: t[...].astype(F32)
        o_ref[...] = _ctx_block(f(q_r), f(k_r), f(v_r), sk_r[0, 0], sk_r[0, 1])

    return _pc(body, "ctx_fwd", _sds((LC, 256)), grid=(2,), in_specs=[cq, ck, ck, sks],
               out_specs=pl.BlockSpec((LC, 128), lambda p: (0, p)))(Qx, Kx, Vx, sinkp)


def ctx_bwd(Qx, Kx, Vx, sinkp, dO, shared, L):
    cq = pl.BlockSpec((LC, 128), lambda p: (L // LC, p))
    ck = pl.BlockSpec((LC, 128), lambda p: (L // LC, 0 if shared else p))
    sks = pl.BlockSpec((1, 2, 1, 128), lambda p: (p, 0, 0, 0))
    op = pl.BlockSpec((LC, 128), lambda p: (0, p))
    ok = pl.BlockSpec((LC, 128), lambda p: (0, 0 if shared else p))
    dos = pl.BlockSpec((LC, 128), lambda p: (L // LC, p))

    def body(q_r, k_r, v_r, sk_r, do_r, dq_o, dk_o, dv_o, dsk_o):
        p = pl.program_id(0)
        f = lambda t: t[...].astype(F32)
        _, vjp = jax.vjp(_ctx_block, f(q_r), f(k_r), f(v_r), sk_r[0, 0], sk_r[0, 1])
        dq, dk, dv, ds0, ds1 = vjp(do_r[...])
        dq_o[...] = dq
        _acc_init((p == 0) if shared else (p >= 0), [dk_o, dv_o])
        dk_o[...] += dk
        dv_o[...] += dv
        dsk_o[0, 0] = ds0
        dsk_o[0, 1] = ds1

    kw = 128 if shared else 256
    return _pc(body, "ctx_bwd", [_sds((LC, 256)), _sds((LC, kw)), _sds((LC, kw)), _sds((2, 2, 1, 128))],
               grid=(2,), in_specs=[cq, ck, ck, sks, dos], out_specs=[op, ok, ok, sks])(Qx, Kx, Vx, sinkp, dO)


def _na_rows(qs, kws, vws, kx, vx, bs):
    sc = HD ** -0.5
    q2 = [_stack(q) for q in qs]
    sl = [mm_nt(a, k) * sc + b for a, k, b in zip(q2, kws, bs)]
    sx = [mm_nt(a, kx) * sc for a in q2]
    m = [lax.stop_gradient(jnp.maximum(jnp.max(a, axis=1, keepdims=True), jnp.max(b, axis=1, keepdims=True)))
         for a, b in zip(sl, sx)]
    el = [jnp.exp(a - c) for a, c in zip(sl, m)]
    ex = [jnp.exp(a - c) for a, c in zip(sx, m)]
    inv = [1.0 / (jnp.sum(a, axis=1, keepdims=True) + jnp.sum(b, axis=1, keepdims=True)) for a, b in zip(el, ex)]
    o2 = [mm(a * i, v) + mm(b * i, vx) for a, b, i, v in zip(el, ex, inv, vws)]
    return [_unstack(o) for o in o2]


def _na_geom(rb, j, R):
    r = rb * 8 + j
    s = jnp.clip(r - 4, 0, R - 8)
    cls = jnp.where(r < 4, r, jnp.where(r > R - 4, r - (R - 8), 4))
    return pl.ds(pl.multiple_of(s * GW, GW), 8 * GW), cls


def _na_load(q_r, k_r, v_r, b_r, rb, R):
    geo = [_na_geom(rb, j, R) for j in range(8)]
    qs = [q_r[j * GW:(j + 1) * GW, :].astype(F32) for j in range(8)]
    kws = [k_r[win, :].astype(F32) for win, _ in geo]
    vws = [v_r[win, :].astype(F32) for win, _ in geo]
    bs = [jnp.concatenate([b_r[0, cls], b_r[1, cls]], axis=0) for _, cls in geo]
    return geo, qs, kws, vws, bs


def na_fwd(QB, KB, VB, biasd, L, sends=()):
    R = L // GW
    qs = pl.BlockSpec((8 * GW, 128), lambda p, rb: (rb, p))
    kfull = pl.BlockSpec((L, 128), lambda p, rb: (0, p))
    kctx = pl.BlockSpec((LC, 128), lambda p, rb: (L // LC, p))
    bs = pl.BlockSpec((2, 8, GW, 8 * GW), lambda p, rb: (p, 0, 0, 0))

    def body(q_r, k_r, v_r, kx_r, vx_r, b_r, o_ref):
        _, qs_, kws, vws, bs_ = _na_load(q_r, k_r, v_r, b_r, pl.program_id(1), R)
        outs = _na_rows(qs_, kws, vws, kx_r[...].astype(F32), vx_r[...].astype(F32), bs_)
        o_ref[...] = jnp.concatenate(outs, axis=0)

    return _pc(body, "na_fwd", _sds((L, 256)), grid=(2, R // 8), in_specs=[qs, kfull, kfull, kctx, kctx, bs],
               out_specs=qs, sends=sends, gather=True)(QB, KB, VB, KB, VB, biasd)


def na_bwd(QB, KB, VB, biasd, dO, L):
    R = L // GW
    qs = pl.BlockSpec((8 * GW, 128), lambda p, rb: (rb, p))
    kfull = pl.BlockSpec((L, 128), lambda p, rb: (0, p))
    kctx = pl.BlockSpec((LC, 128), lambda p, rb: (L // LC, p))
    bs = pl.BlockSpec((2, 8, GW, 8 * GW), lambda p, rb: (p, 0, 0, 0))
    oc = pl.BlockSpec((LC, 128), lambda p, rb: (0, p))

    def body(q_r, k_r, v_r, kx_r, vx_r, b_r, do_r, dq_o, dk_o, dv_o, dkx_o, dvx_o, db_o):
        rb = pl.program_id(1)
        _acc_init(rb == 0, [dk_o, dv_o, dkx_o, dvx_o, db_o])
        geo, qs_, kws, vws, bs_ = _na_load(q_r, k_r, v_r, b_r, rb, R)
        _, vjp = jax.vjp(_na_rows, qs_, kws, vws, kx_r[...].astype(F32), vx_r[...].astype(F32), bs_)
        dqs, dkws, dvws, dkx, dvx, dbs = vjp([do_r[j * GW:(j + 1) * GW, :] for j in range(8)])
        dq_o[...] = jnp.concatenate(dqs, axis=0)
        dkx_o[...] += dkx
        dvx_o[...] += dvx
        for j, (win, cls) in enumerate(geo):
            dk_o[win, :] += dkws[j]
            dv_o[win, :] += dvws[j]
            db_o[0, cls] += dbs[j][:GW]
            db_o[1, cls] += dbs[j][GW:]

    return _pc(body, "na_bwd",
               [_sds((L, 256)), _sds((L, 256)), _sds((L, 256)), _sds((LC, 256)), _sds((LC, 256)),
                _sds((4, 8, GW, 8 * GW))],
               grid=(2, R // 8), in_specs=[qs, kfull, kfull, kctx, kctx, bs, qs],
               out_specs=[qs, kfull, kfull, oc, oc, bs])(QB, KB, VB, KB, VB, biasd, dO)


def exact_mm_call(A, B):
    def body(a_ref, b_ref, o_ref):
        o_ref[...] = _exact(a_ref[...], b_ref[...])

    return _pc(body, "exact_mm", _sds((A.shape[0], B.shape[1])))(A, B)


def _conv_shift(x, d, L):
    T = x.shape[0]
    if d == 0:
        return x
    t = _iota(x.shape, 0)
    src = t + d
    ok = (src >= 0) & (src < T) & ((src >= L) == (t >= L))
    return jnp.where(ok, pltpu.roll(x, (-d) % T, 0), 0.0)


def conv_fwd(XBC, w8, b, L):
    T = XBC.shape[0]

    def body(x_ref, w_ref, b_ref, o_ref):
        x = x_ref[...]
        pre = b_ref[...] + functools.reduce(
            lambda a, c: a + c, [_conv_shift(x, k - 3, L) * w_ref[k:k + 1, :] for k in range(7)])
        o_ref[...] = _silu(pre)

    col = pl.BlockSpec((T, 128), lambda j: (0, j))
    return _pc(body, "conv_fwd", _sds((T, 1024)), grid=(8,),
               in_specs=[col, pl.BlockSpec((8, 128), lambda j: (0, j)), pl.BlockSpec((1, 128), lambda j: (0, j))],
               out_specs=col)(XBC, w8, b)


def conv_bwd(XBC, w8, b, dS, dxs_skip, L, sends=()):
    T = XBC.shape[0]

    def body(x_ref, w_ref, b_ref, d0_r, d1_r, dsk_r, dx_o, dw_o, db_o):
        j = pl.program_id(0)
        x = x_ref[...]
        xs = [_conv_shift(x, k - 3, L) for k in range(7)]
        pre = b_ref[...] + functools.reduce(lambda a, c: a + c, [xs[k] * w_ref[k:k + 1, :] for k in range(7)])
        _, vjp = jax.vjp(_silu, pre)
        dact = d0_r[0] + d1_r[0] + jnp.where(j < 4, dsk_r[...], 0.0)
        dpre, = vjp(dact)
        dx_o[...] = functools.reduce(
            lambda a, c: a + c, [_conv_shift(dpre, 3 - k, L) * w_ref[k:k + 1, :] for k in range(7)])
        dw_o[...] = jnp.concatenate([jnp.sum(dpre * xs[k], axis=0, keepdims=True) for k in range(7)]
                                    + [jnp.zeros((1, 128), F32)], axis=0)
        db_o[...] = jnp.sum(dpre, axis=0, keepdims=True)

    col = pl.BlockSpec((T, 128), lambda j: (0, j))
    w_s = pl.BlockSpec((8, 128), lambda j: (0, j))
    b_s = pl.BlockSpec((1, 128), lambda j: (0, j))
    ds = lambda d: pl.BlockSpec((1, T, 128), lambda j: (d, 0, j))
    return _pc(body, "conv_bwd", [_sds((T, 1024)), _sds((8, 1024)), _sds((1, 1024))], grid=(8,),
               in_specs=[col, w_s, b_s, ds(0), ds(1), pl.BlockSpec((T, 128), lambda j: (0, jnp.minimum(j, 3)))],
               out_specs=[col, w_s, b_s], sends=sends)(XBC, w8, b, dS, dS, dxs_skip)


def _ssd_chunk(xs, bs, cs, dtraw, dtb, alog, hs, tri, d):
    dt = _softplus(dtraw + dtb)
    a = dt * (-jnp.exp(alog))
    acum = _exact(tri, a)
    tot = jnp.sum(a, axis=0, keepdims=True)
    wcol = jnp.exp(tot - acum) * dt
    ea = jnp.exp(acum)
    cd = jnp.exp(tot)
    acum_t, dt_t = acum.T, dt.T
    lane = _iota((Q, 128), 1)
    srow = _iota((128, Q), 0)
    lane1 = _iota((1, 128), 1)
    prow = _iota((128, NSTATE), 0)
    mask = tri > 0.5
    cbs = [mm_nt(cs[g], bs[g]) for g in range(2)]
    ys, hn = [], []
    for j in range(4):
        g = j // 2
        x = xs[j]
        yi, st, eac, cdl = [], [], [], []
        for u in range(2):
            slot = d * 8 + 2 * j + u
            col = lambda m: jnp.sum(jnp.where(lane == slot, m, 0.0), axis=1, keepdims=True)
            rowv = lambda m: jnp.sum(jnp.where(srow == slot, m, 0.0), axis=0, keepdims=True)
            seg = col(acum) - rowv(acum_t)
            dcy = jnp.where(mask, jnp.exp(jnp.where(mask, seg, 0.0)), 0.0)
            yi.append(mm(cbs[g] * dcy * rowv(dt_t), x))
            st.append(mm_tn(x, bs[g] * col(wcol)))
            eac.append(col(ea))
            cdl.append(jnp.sum(jnp.where(lane1 == slot, cd, 0.0), axis=1, keepdims=True))
        yin = mm_nt(cs[g], hs[j])
        ys.append(jnp.where(lane < HD, yi[0] + yin * eac[0], yi[1] + yin * eac[1]))
        hn.append(hs[j] * jnp.where(prow < HD, cdl[0], cdl[1]) + jnp.where(prow < HD, st[0], st[1]))
    return ys, hn


def _ssd_chunk_idx(d, s, nlc, nch):
    return jnp.where(d == 0, (s + nlc) % nch, nch - 1 - s)


def ssd_fwd(ACT, DT, dtb, alog, tri2, L, sends=()):
    T = ACT.shape[0]
    nlc, nch = L // Q, T // Q

    def body(a_ref, dt_ref, dtb_ref, al_ref, tri_ref, y_o, hs_o, hst):
        d, s = pl.program_id(0), pl.program_id(1)
        _acc_init(s == 0, [hst])
        a = a_ref[...]
        xs = [a[:, 128 * j:128 * (j + 1)] for j in range(4)]
        bs = [a[:, 512 + 128 * g:640 + 128 * g] for g in range(2)]
        cs = [a[:, 768 + 128 * g:896 + 128 * g] for g in range(2)]
        hs = [hst[j] for j in range(4)]
        hs_o[0, 0] = hst[...]
        ys, hn = _ssd_chunk(xs, bs, cs, dt_ref[...], dtb_ref[...], al_ref[...], hs, tri_ref[0], d)
        y_o[0] = jnp.concatenate(ys, axis=1)
        for j in range(4):
            hst[j] = hn[j]

    ck = lambda w: pl.BlockSpec((Q, w), lambda d, s: (_ssd_chunk_idx(d, s, nlc, nch), 0))
    v128 = pl.BlockSpec((1, 128), lambda d, s: (0, 0))
    return _pc(body, "ssd_fwd", [_sds((2, T, 512)), _sds((2, nch, 4, 128, NSTATE))], grid=(2, nch),
               in_specs=[ck(1024), ck(128), v128, v128, pl.BlockSpec((1, Q, Q), lambda d, s: (d, 0, 0))],
               out_specs=[pl.BlockSpec((1, Q, 512), lambda d, s: (d, _ssd_chunk_idx(d, s, nlc, nch), 0)),
                          pl.BlockSpec((1, 1, 4, 128, NSTATE), lambda d, s: (d, s, 0, 0, 0))],
               scratch=[pltpu.VMEM((4, 128, NSTATE), F32)], sends=sends, gather=True)(ACT, DT, dtb, alog, tri2)


def ssd_bwd(ACT, DT, dtb, alog, tri2, HS, dY, L, sends=()):
    T = ACT.shape[0]
    nlc, nch = L // Q, T // Q

    def body(a_ref, dt_ref, dtb_ref, al_ref, tri_ref, hs_ref, dy_ref, da_o, ddt_o, ddtb_o, dal_o, dh):
        d, sr = pl.program_id(0), pl.program_id(1)
        _acc_init(sr == 0, [dh, ddtb_o, dal_o])
        a = a_ref[...]
        xs = [a[:, 128 * j:128 * (j + 1)] for j in range(4)]
        bs = [a[:, 512 + 128 * g:640 + 128 * g] for g in range(2)]
        cs = [a[:, 768 + 128 * g:896 + 128 * g] for g in range(2)]
        hs = [hs_ref[0, 0, j] for j in range(4)]
        tri = tri_ref[0]
        fn = lambda xs_, bs_, cs_, dtr, dtb_, al, hs_: _ssd_chunk(xs_, bs_, cs_, dtr, dtb_, al, hs_, tri, d)
        _, vjp = jax.vjp(fn, xs, bs, cs, dt_ref[...], dtb_ref[...], al_ref[...], hs)
        dy = dy_ref[...]
        dys = [dy[:, 128 * j:128 * (j + 1)] for j in range(4)]
        dxs, dbs, dcs, ddt, ddtb, dal, dhs = vjp((dys, [dh[j] for j in range(4)]))
        da_o[0] = jnp.concatenate(dxs + dbs + dcs, axis=1)
        ddt_o[0] = ddt
        ddtb_o[0] += ddtb
        dal_o[0] += dal
        for j in range(4):
            dh[j] = dhs[j]

    cidx = lambda d, sr: _ssd_chunk_idx(d, nch - 1 - sr, nlc, nch)
    ck = lambda w: pl.BlockSpec((Q, w), lambda d, sr: (cidx(d, sr), 0))
    v128 = pl.BlockSpec((1, 128), lambda d, sr: (0, 0))
    o128 = pl.BlockSpec((1, 1, 128), lambda d, sr: (d, 0, 0))
    return _pc(body, "ssd_bwd", [_sds((2, T, 1024)), _sds((2, T, 128)), _sds((2, 1, 128)), _sds((2, 1, 128))],
               grid=(2, nch),
               in_specs=[ck(1024), ck(128), v128, v128, pl.BlockSpec((1, Q, Q), lambda d, sr: (d, 0, 0)),
                         pl.BlockSpec((1, 1, 4, 128, NSTATE), lambda d, sr: (d, nch - 1 - sr, 0, 0, 0)), ck(512)],
               out_specs=[pl.BlockSpec((1, Q, 1024), lambda d, sr: (d, cidx(d, sr), 0)),
                          pl.BlockSpec((1, Q, 128), lambda d, sr: (d, cidx(d, sr), 0)), o128, o128],
               scratch=[pltpu.VMEM((4, 128, NSTATE), F32)], sends=sends)(ACT, DT, dtb, alog, tri2, HS, dY)


_PAIR_HEADS = np.array([[0, 2], [1, 3]])


def _tables(L):
    t = jnp.arange(L)
    inv = 10000.0 ** (-jnp.arange(16, dtype=F32) / 16)

    def half(pos):
        ang = pos.astype(F32)[:, None] * inv[None, :]
        return jnp.concatenate([ang, ang], axis=1)

    ang = jnp.tile(jnp.concatenate([half(t // GW), half(t % GW)], axis=1), (1, 4))
    cos = jnp.concatenate([jnp.cos(ang), jnp.ones((LC, 256), F32)], axis=0)
    sin = jnp.concatenate([jnp.sin(ang), jnp.zeros((LC, 256), F32)], axis=0)
    rm = np.zeros((256, 256), np.float32)
    for j in range(256):
        if j % 32 < 16:
            rm[j + 16, j] = -1.0
        else:
            rm[j - 16, j] = 1.0
    tri = np.tril(np.ones((Q, Q), np.float32))
    return cos, sin, jnp.asarray(rm), jnp.asarray(np.stack([tri, tri.T]))


def _na_index(R):
    rc = np.array([0, 1, 2, 3, 4, R - 3, R - 2, R - 1])
    dy = np.clip(rc - 4, 0, R - 8)[:, None] + np.arange(8)[None, :] - rc[:, None] + 7
    qc, cc = np.arange(GW)[:, None], np.arange(GW)[None, :]
    dx = np.clip(cc - qc, -15, 15) + 15
    cstart = np.clip(qc - 8, 0, GW - 16)
    cmask = (cc >= cstart) & (cc < cstart + 16)
    idx = dy[:, None, :, None] * 31 + dx[None, :, None, :]
    return idx.reshape(8, GW, 8 * GW), np.broadcast_to(cmask[None, :, None, :], idx.shape).reshape(8, GW, 8 * GW), \
        dy, dx, cmask


def _na_bias(rpb, R):
    _, cm, dy, dx, _ = _na_index(R)
    e1t = np.zeros((128, GW * GW), np.float32)
    e1t[dx.reshape(-1), np.arange(GW * GW)] = 1.0
    v = jnp.pad(rpb[:, dy.reshape(-1), :].reshape(256, 31), ((0, 0), (0, 97)))
    full = exact_mm_call(v, jnp.asarray(e1t))
    dense = full.reshape(4, 8, 8, GW, GW).transpose(0, 1, 3, 2, 4).reshape(4, 8, GW, 8 * GW)
    return jnp.where(cm[None], dense, NEG)


def _na_bias_grad(dbias, R):
    _, _, dy, dx, cmask = _na_index(R)
    e1 = np.zeros((GW * GW, 128), np.float32)
    e1[np.arange(GW * GW), dx.reshape(-1)] = cmask.reshape(-1)
    a1 = dbias.reshape(4, 8, GW, 8, GW).transpose(0, 1, 3, 2, 4).reshape(256, GW * GW)
    v = exact_mm_call(a1, jnp.asarray(e1))[:, :31].reshape(4, 64, 31)
    e2 = np.zeros((64, 128), np.float32)
    e2[np.arange(64), dy.reshape(-1)] = 1.0
    a2 = jnp.pad(v.transpose(0, 2, 1).reshape(124, 64), ((0, 4), (0, 0)))
    return exact_mm_call(a2, jnp.asarray(e2))[:124, :15].reshape(4, 31, 15).transpose(0, 2, 1)


def _lanes(v, n=128):
    v = v.reshape(1, -1)
    return jnp.pad(v, ((0, 0), (0, n - v.shape[1])))


def _cls2(a, b):
    return jnp.stack([a, b]).reshape(2, 1, D)


def _win_p(g):
    return jnp.concatenate([g.reshape(IN_COLS, D), jnp.zeros((NP_IN - IN_COLS, D), g.dtype)], axis=0)


def _layer_consts(p):
    sinkp = jnp.broadcast_to(p["wa_sink"][_PAIR_HEADS][:, :, None, None], (2, 2, 1, 128))
    return dict(
        sinkp=sinkp, nosink=jnp.full((2, 2, 1, 128), NEG, F32),
        w8=jnp.concatenate([p["ssm_conv_w"], jnp.zeros((1, 1024), F32)], axis=0),
        cb=p["ssm_conv_b"].reshape(1, 1024), dtb=_lanes(p["ssm_dt_bias"]), alog=_lanes(p["ssm_a_log"]),
        dsk=jnp.repeat(p["ssm_d"], HD).reshape(1, 512), gs=p["ssm_norm_g"].reshape(1, 512),
        gmix=p["g_mix"].reshape(1, D), gffn=p["g_ffn"].reshape(1, D))


def _mods(mod2):
    return [_cls2(mod2[0, D * k:D * (k + 1)], mod2[1, D * k:D * (k + 1)]) for k in range(6)]


def _layer_fwd(X, mod2, c, rpb, tabs, L, ctx_out, shards, nxt):
    cos, sin, rm, tri2 = tabs
    sh1, sc1, gt1, sh2, sc2, gt2 = _mods(mod2)
    biasd = _na_bias(rpb, L // GW)
    fi, fo, wo = shards
    fcut, ocut = 448, 224
    (qa, qb, z, ka, va, kb, vb, xbc, dt, h1), (gfo_a,) = in_fwd(X, c["gmix"], sh1, sc1, c["win"], cos, sin, rm, L,
                                                                sends=(fo[:ocut],))
    (oa,), (gfi_b,) = wa_fwd(qa, ka, va, c["sinkp"], L, sends=(fi[fcut:],))
    (ob,), (gwo,) = na_fwd(qb, kb, vb, biasd, L, sends=(wo,))
    c = dict(c, wout=gwo.reshape(D, D))
    if ctx_out:
        oa_c = ctx_fwd(qa, ka, va, c["sinkp"], True, L)
        ob_c = ctx_fwd(qb, kb, vb, c["nosink"], False, L)
    else:
        oa_c = ob_c = jnp.zeros((LC, 256), F32)
    oa = jnp.concatenate([oa, oa_c], axis=0)
    ob = jnp.concatenate([ob, ob_c], axis=0)
    act = conv_fwd(xbc, c["w8"], c["cb"], L)
    (y2, hs), (gfi_a,) = ssd_fwd(act, dt, c["dtb"], c["alog"], tri2, L, sends=(fi[:fcut],))
    (X1, cat), (gfo_b,) = out_fwd(oa, ob, y2, act, z, c["dsk"], c["gs"], c["wout"], X, gt1, L, sends=(fo[ocut:],))
    c = dict(c, wfi=jnp.concatenate([gfi_a, gfi_b], axis=1).reshape(2 * DFF, D),
             wfo=jnp.concatenate([gfo_a, gfo_b], axis=1).reshape(DFF, D))
    res = ffn_fwd(X1, c["gffn"], sh2, sc2, gt2, c["wfi"], c["wfo"], L, sends=nxt)
    (X2,), got = res if nxt else ((res,), ())
    saved = dict(X=X, X1=X1, qa=qa, qb=qb, z=z, ka=ka, va=va, kb=kb, vb=vb, xbc=xbc, dt=dt, h1=h1, oa=oa, ob=ob,
                 act=act, y2=y2, hs=hs, cat=cat, biasd=biasd)
    return X2, saved, c, got


def _row_blocks(gw):
    return gw.reshape(NDEV, gw.shape[0] // NDEV, gw.shape[1])


def _layer_bwd(dX2, s, mod2, c, tabs, L, ctx_out, carry):
    cos, sin, rm, tri2 = tabs
    sh1, sc1, gt1, sh2, sc2, gt2 = _mods(mod2)
    R = L // GW
    res = ffn_bwd(s["X1"], c["gffn"], sh2, sc2, gt2, c["wfi"], c["wfo"], dX2, L, sends=carry)
    (dX1, h2, dU, actf, dOut, dgffn, dsh2, dsc2, dgt2), got = res if carry else (res, ())
    g = {}
    gfi = _row_blocks(tn_mm(dU, h2, 1408, 1024, MXU))
    gfo = _row_blocks(tn_mm(actf, dOut, 1408, 1024, MXU))
    doa, dob, dy, dxs_skip, dz, dmix, ddsk, dgs, dgt1 = out_bwd(s["oa"], s["ob"], s["y2"], s["act"], s["z"], c["dsk"],
                                                                c["gs"], c["wout"], gt1, dX1, L)
    gout = _row_blocks(tn_mm(s["cat"], dmix, 1024, 512, MXU))
    (dS, ddt2, ddtb, dal), (g["w_ffn_in"],) = ssd_bwd(
        s["act"], s["dt"], c["dtb"], c["alog"], tri2, s["hs"], dy, L, sends=(gfi,))
    (dxbc, dw8, dcb), (g["w_ffn_out"],) = conv_bwd(s["xbc"], c["w8"], c["cb"], dS, dxs_skip, L, sends=(gfo,))
    (dqa, dkpad, dvpad, dkxa, dvxa, dska), (g["w_out"],) = wa_bwd(s["qa"], s["ka"], s["va"], c["sinkp"], doa, L,
                                                                  sends=(gout,))
    dqb, dkb, dvb, dkxb, dvxb, dbias = na_bwd(s["qb"], s["kb"], s["vb"], s["biasd"], dob, L)
    if ctx_out:
        dqa_c, dk1, dv1, dsk1 = ctx_bwd(s["qa"], s["ka"], s["va"], c["sinkp"], doa, True, L)
        dqb_c, dk2, dv2, _ = ctx_bwd(s["qb"], s["kb"], s["vb"], c["nosink"], dob, False, L)
        dkxa, dvxa, dska = dkxa + dk1, dvxa + dv1, dska + dsk1
        dkxb, dvxb = dkxb + dk2, dvxb + dv2
    else:
        dqa_c = dqb_c = jnp.zeros((LC, 256), F32)
    cat0 = lambda a, b: jnp.concatenate([a, b], axis=0)
    dX, dycat, dgmix, dsh1, dsc1 = in_bwd(
        s["X"], c["gmix"], sh1, sc1, c["win"], cos, sin, rm, dX1, cat0(dqa, dqa_c), cat0(dqb, dqb_c), dz,
        cat0(dkpad[Q:L + Q], dkxa), cat0(dvpad[Q:L + Q], dvxa), cat0(dkb, dkxb), cat0(dvb, dvxb), dxbc, ddt2, L,
        latent_only=ctx_out)
    if ctx_out:
        gin = [_row_blocks(tn_mm(dycat, s["h1"], 1024, D // 2, MXU, ncol=D // 2, col0=k)[:IN_COLS]) for k in (0, 1)]
    else:
        gin = _row_blocks(tn_mm(dycat, s["h1"], 1024, 1024, MXU)[:IN_COLS])
    g["g_mix"] = dgmix.reshape(D)
    g["g_ffn"] = dgffn.reshape(D)
    sk = jnp.sum(dska, axis=(2, 3))
    g["wa_sink"] = jnp.zeros((4,), F32).at[_PAIR_HEADS.reshape(-1)].set(sk.reshape(-1))
    g["na_rpb"] = _na_bias_grad(dbias, R)
    g["ssm_conv_w"] = dw8[:7]
    g["ssm_conv_b"] = dcb.reshape(1024)
    g["ssm_dt_bias"] = (ddtb[0] + ddtb[1])[0, :16].reshape(2, 8)
    g["ssm_a_log"] = (dal[0] + dal[1])[0, :16].reshape(2, 8)
    g["ssm_d"] = jnp.sum(ddsk.reshape(8, HD), axis=1)
    g["ssm_norm_g"] = dgs.reshape(512)
    dmod2 = jnp.concatenate([dsh1, dsc1, dgt1, dsh2, dsc2, dgt2], axis=2).reshape(2, 6 * D)
    return dX, g, dmod2, gin, got


def local_step(x, ctx, tgt, mods, layers, shards, g_final, L):
    tabs = _tables(L)
    X = (x, ctx)
    consts = [_layer_consts(p) for p in layers]
    saved = []
    got = all_gather([shards["w_in"][0]], "gather_first")
    for i in range(2):
        consts[i] = dict(consts[i], win=_win_p(got[0]))
        nxt = (shards["w_in"][1],) if i == 0 else ()
        X, s, consts[i], got = _layer_fwd(X, mods[i], consts[i], layers[i]["na_rpb"], tabs, L, i == 0,
                                          (shards["w_ffn_in"][i], shards["w_ffn_out"][i], shards["w_out"][i]), nxt)
        saved.append(s)
    loss8, dX, dgfin = loss_head(X, g_final.reshape(1, D), tgt, L)
    grads, dmods = [None, None], [None, None]
    dX, grads[1], dmods[1], gin1, _ = _layer_bwd(dX, saved[1], mods[1], consts[1], tabs, L, False, ())
    dX, grads[0], dmods[0], gin0, (grads[1]["w_in"],) = _layer_bwd(dX, saved[0], mods[0], consts[0], tabs, L, True,
                                                                   (gin1,))
    return loss8[0, 0], dX, grads, jnp.stack(dmods), dgfin.reshape(D), gin0


def _place():
    x, y, c = lax.axis_index("x"), lax.axis_index("y"), lax.axis_index("c")
    return x, y, c


def _slot(b):
    return 4 * b[0] + 2 * b[1] + b[2]


def _any():
    return pl.BlockSpec(memory_space=pl.ANY)


def all_gather(xs, name):
    n = len(xs)

    def body(*refs):
        x_refs, o_refs = refs[:n], refs[n:2 * n]
        send_sems, recv_sems, local_sems = refs[2 * n:]
        x, y, c = _place()
        me, sib = (x, y, c), (x, y, 1 - c)
        chips = [(1 - x, y), (x, 1 - y), (1 - x, 1 - y)]

        def copy(t, k, blk, to, src=None):
            dst = o_refs[t].at[_slot(blk)]
            return pltpu.make_async_remote_copy(
                src_ref=dst if src is None else src, dst_ref=dst, send_sem=send_sems.at[7 * t + k],
                recv_sem=recv_sems.at[7 * t + k], device_id=to, device_id_type=MESH_T)

        mine = [pltpu.make_async_copy(x_refs[t], o_refs[t].at[_slot(me)], local_sems.at[t]) for t in range(n)]
        for cp in mine:
            cp.start()
        first = []
        for t in range(n):
            first.append(copy(t, 0, me, sib, src=x_refs[t]))
            first += [copy(t, 1 + j, me, (*chip, c), src=x_refs[t]) for j, chip in enumerate(chips)]
        for cp in first:
            cp.start()
        passed = []
        for j, chip in enumerate(chips):
            for t in range(n):
                copy(t, 1 + j, (*chip, c), me).wait_recv()
                cp = copy(t, 4 + j, (*chip, c), sib)
                cp.start()
                passed.append(cp)
        for t in range(n):
            copy(t, 0, sib, me).wait_recv()
            for j, chip in enumerate(chips):
                copy(t, 4 + j, (*chip, 1 - c), me).wait_recv()
        for cp in first + passed:
            cp.wait_send()
        for cp in mine:
            cp.wait()

    return pl.pallas_call(
        body, name=name, out_shape=[_sds((NDEV,) + a.shape, a.dtype) for a in xs],
        in_specs=[_any()] * n, out_specs=[_any()] * n,
        scratch_shapes=[pltpu.SemaphoreType.DMA((7 * n,)), pltpu.SemaphoreType.DMA((7 * n,)),
                        pltpu.SemaphoreType.DMA((n,))],
        interpret=_INTERPRET)(*xs)


def all_to_all(xs, name):
    n = len(xs)

    def body(*refs):
        _a2a_start(refs[:n], refs[n:2 * n], *refs[2 * n:])
        _a2a_wait(refs[:n], refs[n:2 * n], *refs[2 * n:])

    return pl.pallas_call(
        body, name=name, out_shape=[_sds(a.shape, a.dtype) for a in xs],
        in_specs=[_any()] * n, out_specs=[_any()] * n, scratch_shapes=_a2a_sems(n), interpret=_INTERPRET)(*xs)


def _a2a_sems(n):
    return [pltpu.SemaphoreType.DMA((7 * n,)), pltpu.SemaphoreType.DMA((7 * n,)), pltpu.SemaphoreType.DMA((n,))]


def _a2a_copies(x_refs, o_refs, send_sems, recv_sems, local_sems):
    n = len(x_refs)
    x, y, c = _place()
    me = (x, y, c)
    flip = lambda v, b: (1 - v) if b else v
    peers = [(flip(x, k >> 2 & 1), flip(y, k >> 1 & 1), flip(c, k & 1)) for k in range(1, NDEV)]
    mine = [pltpu.make_async_copy(x_refs[t].at[_slot(me)], o_refs[t].at[_slot(me)], local_sems.at[t])
            for t in range(n)]

    def copy(t, k, src_slot, dst_slot, to):
        return pltpu.make_async_remote_copy(
            src_ref=x_refs[t].at[src_slot], dst_ref=o_refs[t].at[dst_slot], send_sem=send_sems.at[7 * t + k],
            recv_sem=recv_sems.at[7 * t + k], device_id=to, device_id_type=MESH_T)

    sends = [copy(t, k, _slot(p), _slot(me), p) for t in range(n) for k, p in enumerate(peers)]
    recvs = [copy(t, k, _slot(p), _slot(p), me) for t in range(n) for k, p in enumerate(peers)]
    return mine, sends, recvs


def _ag_copies(x_refs, o_refs, send_sems, recv_sems, local_sems):
    n = len(x_refs)
    x, y, c = _place()
    me = (x, y, c)
    flip = lambda v, b: (1 - v) if b else v
    peers = [(flip(x, k >> 2 & 1), flip(y, k >> 1 & 1), flip(c, k & 1)) for k in range(1, NDEV)]
    mine = [pltpu.make_async_copy(x_refs[t], o_refs[t].at[_slot(me)], local_sems.at[t]) for t in range(n)]

    def copy(t, k, dst_slot, to):
        return pltpu.make_async_remote_copy(
            src_ref=x_refs[t], dst_ref=o_refs[t].at[dst_slot], send_sem=send_sems.at[7 * t + k],
            recv_sem=recv_sems.at[7 * t + k], device_id=to, device_id_type=MESH_T)

    sends = [copy(t, k, _slot(me), p) for t in range(n) for k, p in enumerate(peers)]
    recvs = [copy(t, k, _slot(p), me) for t in range(n) for k, p in enumerate(peers)]
    return mine, sends, recvs


def _ag_start(x_refs, o_refs, send_sems, recv_sems, local_sems):
    mine, sends, _ = _ag_copies(x_refs, o_refs, send_sems, recv_sems, local_sems)
    for cp in mine + sends:
        cp.start()


def _ag_wait(x_refs, o_refs, send_sems, recv_sems, local_sems):
    mine, sends, recvs = _ag_copies(x_refs, o_refs, send_sems, recv_sems, local_sems)
    for cp in recvs:
        cp.wait_recv()
    for cp in sends:
        cp.wait_send()
    for cp in mine:
        cp.wait()


def _a2a_start(x_refs, o_refs, send_sems, recv_sems, local_sems):
    mine, sends, _ = _a2a_copies(x_refs, o_refs, send_sems, recv_sems, local_sems)
    for cp in mine + sends:
        cp.start()


def _a2a_wait(x_refs, o_refs, send_sems, recv_sems, local_sems):
    mine, sends, recvs = _a2a_copies(x_refs, o_refs, send_sems, recv_sems, local_sems)
    for cp in recvs:
        cp.wait_recv()
    for cp in sends:
        cp.wait_send()
    for cp in mine:
        cp.wait()


def adam_reduce(P, w, m, v, name, sends=()):
    n, R, C = P.shape
    br = R // 4 if R % 64 == 0 else R

    def body(p_ref, w_ref, m_ref, v_ref, g_o, d_o, m_o, v_o):
        g = p_ref[0].astype(F32)
        for k in range(1, n):
            g = g + p_ref[k].astype(F32)
        m1 = ADAM_B1 * m_ref[...] + (1.0 - ADAM_B1) * g
        v1 = ADAM_B2 * v_ref[...] + (1.0 - ADAM_B2) * jnp.square(g)
        m_hat = m1 / (1.0 - ADAM_B1 ** ADAM_STEP)
        v_hat = v1 / (1.0 - ADAM_B2 ** ADAM_STEP)
        g_o[...] = g
        d_o[...] = -ADAM_LR * (m_hat / (jnp.sqrt(v_hat) + ADAM_EPS) + ADAM_WD * w_ref[...])
        m_o[...] = m1
        v_o[...] = v1

    blk = pl.BlockSpec((br, C), lambda i: (i, 0))
    return _pc(body, name, [_sds((R, C))] * 4, grid=(R // br,),
               in_specs=[pl.BlockSpec((n, br, C), lambda i: (0, i, 0)), blk, blk, blk], out_specs=[blk] * 4,
               sends=sends)(P, w, m, v)


def adam_layers(P0, P1, w, m, v, name, sends=()):
    n, R, C = P0.shape
    br = R // 4 if R % 64 == 0 else R
    nb = R // br

    def body(p0_ref, p1_ref, w_ref, m_ref, v_ref, g_o, d_o, m_o, v_o):
        def total(p_ref):
            g = p_ref[0].astype(F32)
            for k in range(1, n):
                g = g + p_ref[k].astype(F32)
            return g

        g = jnp.where(pl.program_id(0) == 0, total(p0_ref), total(p1_ref))
        m1 = ADAM_B1 * m_ref[0] + (1.0 - ADAM_B1) * g
        v1 = ADAM_B2 * v_ref[0] + (1.0 - ADAM_B2) * jnp.square(g)
        m_hat = m1 / (1.0 - ADAM_B1 ** ADAM_STEP)
        v_hat = v1 / (1.0 - ADAM_B2 ** ADAM_STEP)
        g_o[0] = g
        d_o[0] = -ADAM_LR * (m_hat / (jnp.sqrt(v_hat) + ADAM_EPS) + ADAM_WD * w_ref[0])
        m_o[0] = m1
        v_o[0] = v1

    blk = pl.BlockSpec((1, br, C), lambda l, i: (l, i, 0))
    p0 = pl.BlockSpec((n, br, C), lambda l, i: (0, jnp.where(l == 0, i, nb - 1), 0))
    p1 = pl.BlockSpec((n, br, C), lambda l, i: (0, jnp.where(l == 1, i, 0), 0))
    return _pc(body, name, [_sds((2, R, C))] * 4, grid=(2, nb), in_specs=[p0, p1, blk, blk, blk],
               out_specs=[blk] * 4, sends=sends)(P0, P1, w, m, v)


def mod_fwd(scin, wmod, bcol):
    def body(s_ref, w_ref, b_ref, o_ref):
        o_ref[0] = mm(_silu(s_ref[...]), w_ref[0]) + b_ref[0]

    return _pc(body, "mod_fwd", _sds((2, 16, 768)), grid=(2,),
               in_specs=[pl.BlockSpec((16, D), lambda l: (0, 0)), pl.BlockSpec((1, D, 768), lambda l: (l, 0, 0)),
                         pl.BlockSpec((1, 1, 768), lambda l: (l, 0, 0))],
               out_specs=pl.BlockSpec((1, 16, 768), lambda l: (l, 0, 0)))(scin, wmod, bcol)


def mod_bwd(scin, wmod, G):
    def body(s_ref, w_ref, g_ref, dw_o, ds_o):
        _, vjp = jax.vjp(lambda s, w: mm(_silu(s), w), s_ref[...], w_ref[0])
        ds, dw = vjp(g_ref[0])
        dw_o[0] = dw
        _acc_init(pl.program_id(0) == 0, [ds_o])
        ds_o[...] += ds

    full = pl.BlockSpec((16, D), lambda l: (0, 0))
    wsp = pl.BlockSpec((1, D, 768), lambda l: (l, 0, 0))
    return _pc(body, "mod_bwd", [_sds((2, D, 768)), _sds((16, D))], grid=(2,),
               in_specs=[full, wsp, pl.BlockSpec((1, 16, 768), lambda l: (l, 0, 0))], out_specs=[wsp, full])(
        scin, wmod, G)


_SMALL = ["b_mod", "g_mix", "wa_sink", "na_rpb", "ssm_conv_w", "ssm_conv_b", "ssm_dt_bias", "ssm_a_log", "ssm_d",
          "ssm_norm_g", "g_ffn", "g_final", "dmod_s", "dmod_c"]


def _pack(parts):
    rows = []
    for a in parts:
        f = a.reshape(-1).astype(F32)
        rows.append(jnp.pad(f, (0, (-f.shape[0]) % 1024)).reshape(-1, 128))
    return jnp.concatenate(rows, axis=0)


def _unpack(packed, shapes):
    out, r = [], 0
    for s in shapes:
        nel = int(np.prod(s))
        nr = -(-nel // 1024) * 8
        out.append(packed[r:r + nr].reshape(-1)[:nel].reshape(s))
        r += nr
    return out


def kernel(x, c, ctx, c_ctx, w_mod, b_mod, g_mix, w_in, wa_sink, na_rpb, ssm_conv_w, ssm_conv_b, ssm_dt_bias, ssm_a_log, ssm_d, ssm_norm_g, w_out, g_ffn, w_ffn_in, w_ffn_out, g_final, loss_target, m_c_ctx, m_w_mod, m_b_mod, m_g_mix, m_w_in, m_wa_sink, m_na_rpb, m_ssm_conv_w, m_ssm_conv_b, m_ssm_dt_bias, m_ssm_a_log, m_ssm_d, m_ssm_norm_g, m_w_out, m_g_ffn, m_w_ffn_in, m_w_ffn_out, m_g_final, v_c_ctx, v_w_mod, v_b_mod, v_g_mix, v_w_in, v_wa_sink, v_na_rpb, v_ssm_conv_w, v_ssm_conv_b, v_ssm_dt_bias, v_ssm_a_log, v_ssm_d, v_ssm_norm_g, v_w_out, v_g_ffn, v_w_ffn_in, v_w_ffn_out, v_g_final):
    L = x.shape[1]
    px, py, pc = _place()
    me = 4 * px + 2 * py + pc
    W = dict(c_ctx=c_ctx, w_mod=w_mod, b_mod=b_mod, g_mix=g_mix, w_in=w_in, wa_sink=wa_sink, na_rpb=na_rpb,
             ssm_conv_w=ssm_conv_w, ssm_conv_b=ssm_conv_b, ssm_dt_bias=ssm_dt_bias, ssm_a_log=ssm_a_log, ssm_d=ssm_d,
             ssm_norm_g=ssm_norm_g, w_out=w_out, g_ffn=g_ffn, w_ffn_in=w_ffn_in, w_ffn_out=w_ffn_out, g_final=g_final)
    M = dict(c_ctx=m_c_ctx, w_mod=m_w_mod, b_mod=m_b_mod, g_mix=m_g_mix, w_in=m_w_in, wa_sink=m_wa_sink,
             na_rpb=m_na_rpb, ssm_conv_w=m_ssm_conv_w, ssm_conv_b=m_ssm_conv_b, ssm_dt_bias=m_ssm_dt_bias,
             ssm_a_log=m_ssm_a_log, ssm_d=m_ssm_d, ssm_norm_g=m_ssm_norm_g, w_out=m_w_out, g_ffn=m_g_ffn,
             w_ffn_in=m_w_ffn_in, w_ffn_out=m_w_ffn_out, g_final=m_g_final)
    V = dict(c_ctx=v_c_ctx, w_mod=v_w_mod, b_mod=v_b_mod, g_mix=v_g_mix, w_in=v_w_in, wa_sink=v_wa_sink,
             na_rpb=v_na_rpb, ssm_conv_w=v_ssm_conv_w, ssm_conv_b=v_ssm_conv_b, ssm_dt_bias=v_ssm_dt_bias,
             ssm_a_log=v_ssm_a_log, ssm_d=v_ssm_d, ssm_norm_g=v_ssm_norm_g, w_out=v_w_out, g_ffn=v_g_ffn,
             w_ffn_in=v_w_ffn_in, w_ffn_out=v_w_ffn_out, g_final=v_g_final)

    c_all, conv_all = all_gather([c, ssm_conv_w], "gather_small")
    tr = lambda a: a.transpose(0, 2, 1)
    shards = dict(w_in=tr(w_in).astype(MXU), w_out=w_out.astype(MXU), w_ffn_in=tr(w_ffn_in).astype(MXU),
                  w_ffn_out=w_ffn_out.astype(MXU))
    conv_f = conv_all.transpose(1, 2, 0, 3).reshape(2, 7, 1024)

    scin = jnp.concatenate([c_all.reshape(NDEV, D), c_ctx.reshape(1, D), jnp.zeros((7, D), F32)], axis=0)
    bcol = lax.dynamic_slice_in_dim(b_mod, me * 768, 768, axis=1).reshape(2, 1, 768)
    mod_all, = all_gather([mod_fwd(scin, w_mod, bcol)], "gather_mod")
    mod_rows = mod_all.transpose(1, 2, 0, 3).reshape(2, 16, 6 * D)
    mods = jnp.stack([lax.dynamic_index_in_dim(mod_rows, me, axis=1, keepdims=False), mod_rows[:, 8]], axis=1)

    layers = [dict(g_mix=g_mix[i], wa_sink=wa_sink[i], na_rpb=na_rpb[i], ssm_conv_w=conv_f[i],
                   ssm_conv_b=ssm_conv_b[i], ssm_dt_bias=ssm_dt_bias[i], ssm_a_log=ssm_a_log[i], ssm_d=ssm_d[i],
                   ssm_norm_g=ssm_norm_g[i], g_ffn=g_ffn[i]) for i in range(2)]
    loss, dx, grads, dmods, dgfin, gin0 = local_step(x[0], ctx[0], loss_target[0], mods, layers, shards, g_final, L)
    loss = lax.psum(loss, ("x", "y", "c"))

    stk = lambda n: jnp.stack([grads[0][n], grads[1][n]])
    small = dict(b_mod=dmods[:, 0] + dmods[:, 1], g_final=dgfin, dmod_s=dmods[:, 0], dmod_c=dmods[:, 1])
    for nme in _SMALL:
        if nme not in small:
            small[nme] = stk(nme)
    shapes = [small[nme].shape for nme in _SMALL]
    zero_like = lambda nme: jnp.zeros(small[nme].shape, F32)
    own = lambda S, nme: S[nme] if (nme in S and S[nme].shape == small[nme].shape) else zero_like(nme)
    gath, = all_gather([_pack([small[nme] for nme in _SMALL])], "gather_grads")
    sm = adam_reduce(gath, _pack([own(W, nme) for nme in _SMALL]), _pack([own(M, nme) for nme in _SMALL]),
                     _pack([own(V, nme) for nme in _SMALL]), "adam_small")
    res = {nme: vals for nme, vals in zip(_SMALL, zip(*[_unpack(a, shapes) for a in sm]))}

    cols = lambda a: lax.dynamic_slice_in_dim(a, me * 768, 768, axis=-1)
    gparts = [_unpack(gath[d], shapes) for d in range(NDEV)]
    dmod_s_all = jnp.stack([gparts[d][_SMALL.index("dmod_s")] for d in range(NDEV)], axis=1)
    G = jnp.concatenate([cols(dmod_s_all), cols(res["dmod_c"][0])[:, None, :], jnp.zeros((2, 7, 768), F32)], axis=1)
    dwmod, dscin = mod_bwd(scin, w_mod, G)
    cc_g, = all_gather([dscin[8].reshape(8, 128)], "gather_cctx")
    out = {}
    out["c_ctx"] = [a.reshape(D) for a in adam_reduce(cc_g, c_ctx.reshape(8, 128), m_c_ctx.reshape(8, 128),
                                                      v_c_ctx.reshape(8, 128), "adam_cctx")]
    res_wmod, (got1,) = adam_reduce(dwmod.reshape(1, 2 * D, 768), w_mod.reshape(2 * D, 768),
                                    m_w_mod.reshape(2 * D, 768), v_w_mod.reshape(2 * D, 768), "adam_wmod",
                                    sends=(gin0[1],))
    out["w_mod"] = [a.reshape(2, D, 768) for a in res_wmod]
    gconv = lax.dynamic_slice_in_dim(res["ssm_conv_w"][0], me * 128, 128, axis=2)
    out["ssm_conv_w"] = [a.reshape(2, 7, 128) for a in adam_reduce(
        gconv.reshape(1, 14, 128), ssm_conv_w.reshape(14, 128), m_ssm_conv_w.reshape(14, 128),
        v_ssm_conv_w.reshape(14, 128), "adam_conv")]
    for nme in _SMALL:
        if nme not in ("ssm_conv_w", "dmod_s", "dmod_c"):
            out[nme] = list(res[nme])

    adam_big = lambda nme, t, **kw: adam_layers(grads[0][nme], grads[1][nme], t(W[nme]), t(M[nme]), t(V[nme]),
                                                "adam_" + nme, **kw)
    same = lambda a: a
    res_fi, (got0,) = adam_big("w_ffn_in", tr, sends=(gin0[0],))
    grads[0]["w_in"] = jnp.concatenate([got0, got1], axis=2)
    out["w_ffn_in"] = [tr(a) for a in res_fi]
    out["w_ffn_out"] = list(adam_big("w_ffn_out", same))
    out["w_out"] = list(adam_big("w_out", same))
    out["w_in"] = [tr(a) for a in adam_big("w_in", tr)]
    order = ["c_ctx", "w_mod", "b_mod", "g_mix", "w_in", "wa_sink", "na_rpb", "ssm_conv_w", "ssm_conv_b",
             "ssm_dt_bias", "ssm_a_log", "ssm_d", "ssm_norm_g", "w_out", "g_ffn", "w_ffn_in", "w_ffn_out", "g_final"]
    return (loss, dx.reshape(1, L, D), *[out[nme][0] for nme in order], *[out[nme][1] for nme in order],
            *[out[nme][2] for nme in order], *[out[nme][3] for nme in order])
```

```python
import functools
import math

import numpy as np
import jax
import jax.numpy as jnp
from jax import lax
from jax.experimental import pallas as pl
from jax.experimental.pallas import tpu as pltpu

F32 = jnp.float32
MXU = jnp.bfloat16
_INTERPRET = False
VMEM_LIMIT = 60 * 1024 * 1024

D = 1024
LC = 256
GW = 64
HD = 64
EPS = 1e-6
NEG = -1e30
NDEV = 8
Q = 128
NSTATE = 128
DFF = 2816
IN_COLS = 2832
NP_IN = 3072
C_QA, C_QB, C_Z, C_KA, C_VA, C_KB, C_VB, C_XBC, C_DT = 0, 256, 512, 1024, 1152, 1280, 1536, 1792, 2816
ADAM_LR, ADAM_B1, ADAM_B2, ADAM_EPS, ADAM_WD, ADAM_STEP = 0.001, 0.9, 0.999, 1e-08, 0.01, 10
MESH_T = pl.DeviceIdType.MESH


def _dg(a, b, ca, cb):
    return lax.dot_general(a.astype(MXU), b.astype(MXU), (((ca,), (cb,)), ((), ())), preferred_element_type=F32)


@jax.custom_vjp
def mm(a, b):
    return _dg(a, b, 1, 0)


def _mm_f(a, b):
    return _dg(a, b, 1, 0), (a, b)


def _mm_b(res, g):
    a, b = res
    return _dg(g, b, 1, 1).astype(a.dtype), _dg(a, g, 0, 0).astype(b.dtype)


mm.defvjp(_mm_f, _mm_b)


@jax.custom_vjp
def mm_nt(a, b):
    return _dg(a, b, 1, 1)


def _mmnt_f(a, b):
    return _dg(a, b, 1, 1), (a, b)


def _mmnt_b(res, g):
    a, b = res
    return _dg(g, b, 1, 0).astype(a.dtype), _dg(g, a, 0, 0).astype(b.dtype)


mm_nt.defvjp(_mmnt_f, _mmnt_b)


@jax.custom_vjp
def mm_tn(a, b):
    return _dg(a, b, 0, 0)


def _mmtn_f(a, b):
    return _dg(a, b, 0, 0), (a, b)


def _mmtn_b(res, g):
    a, b = res
    return _dg(b, g, 1, 1).astype(a.dtype), _dg(a, g, 1, 0).astype(b.dtype)


mm_tn.defvjp(_mmtn_f, _mmtn_b)


@jax.custom_vjp
def mmw(a, w):
    return _dg(a, w, 1, 0)


mmw.defvjp(lambda a, w: (_dg(a, w, 1, 0), w), lambda w, g: (_dg(g, w, 1, 1), None))


@jax.custom_vjp
def mmw_nt(a, w):
    return _dg(a, w, 1, 1)


mmw_nt.defvjp(lambda a, w: (_dg(a, w, 1, 1), w), lambda w, g: (_dg(g, w, 1, 0), None))


def _exact(a, b):
    return lax.dot_general(a, b, (((1,), (0,)), ((), ())), precision=lax.Precision.HIGHEST,
                           preferred_element_type=F32)


def _pc(body, name, out_shape, grid=None, in_specs=None, out_specs=None, scratch=(), sends=(), gather=False):
    params = pltpu.CompilerParams(vmem_limit_bytes=VMEM_LIMIT)
    if sends and not isinstance(out_shape, (list, tuple)):
        out_shape, out_specs = [out_shape], [out_specs]
    start, wait = (_ag_start, _ag_wait) if gather else (_a2a_start, _a2a_wait)
    if not sends:
        kw = {}
        if grid is not None:
            kw = dict(grid=grid, in_specs=in_specs, out_specs=out_specs)
        elif in_specs is not None:
            kw = dict(in_specs=in_specs, out_specs=out_specs)
        return pl.pallas_call(body, name=name, out_shape=out_shape, scratch_shapes=list(scratch),
                              compiler_params=params, interpret=_INTERPRET, **kw)
    n, nin, nout, nscr = len(sends), len(in_specs), len(out_shape), len(scratch)

    def body2(*refs):
        cin, xs = refs[:nin], refs[nin:nin + n]
        couts, os_ = refs[nin + n:nin + n + nout], refs[nin + n + nout:nin + 2 * n + nout]
        cscr, sems = refs[nin + 2 * n + nout:nin + 2 * n + nout + nscr], refs[nin + 2 * n + nout + nscr:]
        ids = [pl.program_id(a) for a in range(len(grid))]
        first = functools.reduce(lambda a, b: a & b, [i == 0 for i in ids])
        last = functools.reduce(lambda a, b: a & b, [i == g - 1 for i, g in zip(ids, grid)])

        @pl.when(first)
        def _():
            start(xs, os_, *sems)

        body(*cin, *couts, *cscr)

        @pl.when(last)
        def _():
            wait(xs, os_, *sems)

    call = pl.pallas_call(
        body2, name=name,
        out_shape=list(out_shape) + [_sds(((NDEV,) if gather else ()) + a.shape, a.dtype) for a in sends],
        grid=grid, in_specs=list(in_specs) + [_any()] * n, out_specs=list(out_specs) + [_any()] * n,
        scratch_shapes=list(scratch) + _a2a_sems(n), compiler_params=params, interpret=_INTERPRET)

    def run(*args):
        res = call(*args, *sends)
        return res[:nout], res[nout:]

    return run


def _vm():
    return pl.BlockSpec(memory_space=pltpu.VMEM)


def _sds(shape, dt=F32):
    return jax.ShapeDtypeStruct(shape, dt)


def _iota(shape, dim):
    return lax.broadcasted_iota(jnp.int32, shape, dim)


def _silu(x):
    return x * jax.nn.sigmoid(x)


def _softplus(x):
    return jnp.maximum(x, 0.0) + jnp.log1p(jnp.exp(-jnp.abs(x)))


def _normmod(x, g, sh, sc):
    r = lax.rsqrt(jnp.mean(x * x, axis=-1, keepdims=True) + EPS)
    return (x * r * g) * (1.0 + sc) + sh


def _rope(x, cos, sin, rm):
    return x * cos + _exact(x, rm) * sin


def _swap12(x):
    lane = _iota(x.shape, 1)
    up, down = pltpu.roll(x, 192, 1), pltpu.roll(x, 64, 1)
    return jnp.where((lane >= 64) & (lane < 128), up, jnp.where((lane >= 128) & (lane < 192), down, x))


def _acc_init(first, refs):
    @pl.when(first)
    def _():
        for r in refs:
            r[...] = jnp.zeros_like(r)


def _stream(X, TR, nlt):
    if not isinstance(X, tuple):
        return (X,), [pl.BlockSpec((TR, D), lambda i: (i, 0))], lambda refs: refs[0][...]
    specs = [pl.BlockSpec((TR, D), lambda i: (jnp.minimum(i, nlt - 1), 0)), pl.BlockSpec((TR, D), lambda i: (0, 0))]
    return X, specs, lambda refs: jnp.where(pl.program_id(0) < nlt, refs[0][...], refs[1][...])


def in_fwd(X, g, sh, sc, W, cos, sin, rm, L, sends=()):
    T = L + LC
    TR = 256
    nlt = L // TR
    xs, xspecs, xread = _stream(X, TR, nlt)

    def body(*refs):
        (g_ref, sh_ref, sc_ref, w_ref, cos_ref, sin_ref, rm_ref,
         qa, qb, z, ka, va, kb, vb, xbc, dt, hout) = refs[len(xs):]
        h = _normmod(xread(refs), g_ref[...], sh_ref[0], sc_ref[0]).astype(MXU)
        hout[...] = h
        y = lax.dot_general(h, w_ref[...], (((1,), (1,)), ((), ())), preferred_element_type=F32)
        cs, sn, r = cos_ref[...], sin_ref[...], rm_ref[...]
        qa[...] = _rope(_swap12(y[:, C_QA:C_QB]), cs, sn, r).astype(MXU)
        qb[...] = y[:, C_QB:C_Z].astype(MXU)
        z[...] = y[:, C_Z:C_KA]
        ka[...] = _rope(y[:, C_KA:C_VA], cs[:, :128], sn[:, :128], r[:128, :128]).astype(MXU)
        va[...] = y[:, C_VA:C_KB].astype(MXU)
        kb[...] = y[:, C_KB:C_VB].astype(MXU)
        vb[...] = y[:, C_VB:C_XBC].astype(MXU)
        xbc[...] = y[:, C_XBC:C_DT]
        dt[...] = y[:, C_DT:C_DT + 128]

    row = lambda w: pl.BlockSpec((TR, w), lambda i: (i, 0))
    cls = pl.BlockSpec((1, 1, D), lambda i: (i // nlt, 0, 0))
    widths = [(256, MXU), (256, MXU), (512, F32), (128, MXU), (128, MXU), (256, MXU), (256, MXU), (1024, F32),
              (128, F32), (D, MXU)]
    return _pc(body, "in_fwd", [_sds((T, w), d) for w, d in widths], grid=(T // TR,),
               in_specs=xspecs + [pl.BlockSpec((1, D), lambda i: (0, 0)), cls, cls, _vm(), row(256), row(256), _vm()],
               out_specs=[row(w) for w, _ in widths], sends=sends, gather=True)(*xs, g, sh, sc, W, cos, sin, rm)


def in_bwd(X, g, sh, sc, W, cos, sin, rm, dxres, dqa, dqb, dz, dka, dva, dkb, dvb, dxbc, ddt2, L, latent_only):
    T = L + LC
    TR = 256
    nlt = L // TR
    xs, xspecs, xread = _stream(X, TR, nlt)

    def body(*refs):
        (g_ref, sh_ref, sc_ref, w_ref, cos_ref, sin_ref, rm_ref, dxres_ref, dqa_r, dqb_r, dz_r, dka_r,
         dva_r, dkb_r, dvb_r, dxbc_r, ddt0_r, ddt1_r, dx_o, dy_o, dg_o, dsh_o, dsc_o) = refs[len(xs):]
        i = pl.program_id(0)
        cs, sn, r = cos_ref[...], sin_ref[...], rm_ref[...]
        _, vq = jax.vjp(lambda t: _rope(t, cs, sn, r), dqa_r[...])
        _, vk = jax.vjp(lambda t: _rope(t, cs[:, :128], sn[:, :128], r[:128, :128]), dka_r[...])
        dyqa = _swap12(vq(dqa_r[...])[0])
        dyka, = vk(dka_r[...])
        ddt = ddt0_r[0] + ddt1_r[0]
        dy = jnp.concatenate([dyqa, dqb_r[...], dz_r[...], dyka, dva_r[...], dkb_r[...], dvb_r[...], dxbc_r[...],
                              ddt, jnp.zeros((TR, NP_IN - C_DT - 128), F32)], axis=1).astype(MXU)
        dy_o[...] = dy
        dh = jnp.dot(dy, w_ref[...], preferred_element_type=F32)
        _, vp = jax.vjp(_normmod, xread(refs), g_ref[...], sh_ref[0], sc_ref[0])
        dx, dg, dsh, dsc = vp(dh)
        if latent_only:
            @pl.when(i < nlt)
            def _():
                dx_o[...] = dx + dxres_ref[...]
        else:
            dx_o[...] = dx + dxres_ref[...]
        _acc_init(i == 0, [dg_o])
        _acc_init((i == 0) | (i == nlt), [dsh_o, dsc_o])
        dg_o[...] += dg
        dsh_o[0] += dsh
        dsc_o[0] += dsc

    row = lambda w: pl.BlockSpec((TR, w), lambda i: (i, 0))
    cls = pl.BlockSpec((1, 1, D), lambda i: (i // nlt, 0, 0))
    vec = pl.BlockSpec((1, D), lambda i: (0, 0))
    dts = lambda d: pl.BlockSpec((1, TR, 128), lambda i: (d, i, 0))
    dxs = pl.BlockSpec((TR, D), lambda i: (jnp.minimum(i, nlt - 1), 0)) if latent_only else row(D)
    return _pc(body, "in_bwd",
               [_sds((L if latent_only else T, D)), _sds((T, NP_IN), MXU), _sds((1, D)), _sds((2, 1, D)),
                _sds((2, 1, D))],
               grid=(T // TR,),
               in_specs=xspecs + [vec, cls, cls, _vm(), row(256), row(256), _vm(), row(D), row(256), row(256),
                                  row(512), row(128), row(128), row(256), row(256), row(1024), dts(0), dts(1)],
               out_specs=[dxs, row(NP_IN), vec, cls, cls])(
        *xs, g, sh, sc, W, cos, sin, rm, dxres, dqa, dqb, dz, dka, dva, dkb, dvb, dxbc, ddt2, ddt2)


def tn_mm(A, G, bk, bn, out_dtype, ncol=None, col0=0):
    T, K = A.shape
    N = G.shape[1] if ncol is None else ncol
    first = col0 * (N // bn)
    bt = T
    nt = T // bt

    def body(a_ref, g_ref, o_ref, acc):
        t = pl.program_id(2)
        _acc_init(t == 0, [acc])
        acc[...] += lax.dot_general(a_ref[...], g_ref[...], (((0,), (0,)), ((), ())), preferred_element_type=F32)

        @pl.when(t == nt - 1)
        def _():
            o_ref[...] = acc[...].astype(out_dtype)

    return _pc(body, "tn_mm", _sds((K, N), out_dtype), grid=(K // bk, N // bn, nt),
               in_specs=[pl.BlockSpec((bt, bk), lambda k, n, t: (t, k)),
                         pl.BlockSpec((bt, bn), lambda k, n, t: (t, first + n))],
               out_specs=pl.BlockSpec((bk, bn), lambda k, n, t: (k, n)),
               scratch=[pltpu.VMEM((bk, bn), F32)])(A, G)


def _ssm_out(yf, yb, xs, z, dsk, gs):
    y = (yf + yb + dsk * xs) * _silu(z)
    r = lax.rsqrt(jnp.mean(y * y, axis=-1, keepdims=True) + EPS)
    return y * r * gs


def out_fwd(oa, ob, y2, act, z, dsk, gs, W, X, gate, L, sends=()):
    T = L + LC
    TR = 256
    nlt = L // TR
    xs, xspecs, xread = _stream(X, TR, nlt)

    def body(*refs):
        oa_r, ob_r, yf_r, yb_r, xs_r, z_r, dsk_r, gs_r, w_ref, gt_ref, x1_o, cat_o = refs[len(xs):]
        oc = _ssm_out(yf_r[0], yb_r[0], xs_r[...], z_r[...], dsk_r[...], gs_r[...])
        cat = jnp.concatenate([_swap12(oa_r[...]), ob_r[...], oc], axis=1).astype(MXU)
        cat_o[...] = cat
        x1_o[...] = xread(refs) + gt_ref[0] * jnp.dot(cat, w_ref[...], preferred_element_type=F32)

    row = lambda w: pl.BlockSpec((TR, w), lambda i: (i, 0))
    ys = lambda d: pl.BlockSpec((1, TR, 512), lambda i: (d, i, 0))
    cls = pl.BlockSpec((1, 1, D), lambda i: (i // nlt, 0, 0))
    v512 = pl.BlockSpec((1, 512), lambda i: (0, 0))
    return _pc(body, "out_fwd", [_sds((T, D)), _sds((T, D), MXU)], grid=(T // TR,),
               in_specs=xspecs + [row(256), row(256), ys(0), ys(1), row(512), row(512), v512, v512, _vm(), cls],
               out_specs=[row(D), row(D)], sends=sends, gather=True)(*xs, oa, ob, y2, y2, act, z, dsk, gs, W, gate)


def out_bwd(oa, ob, y2, act, z, dsk, gs, W, gate, dX1, L):
    T = dX1.shape[0]
    TR = 256
    nlt = L // TR

    def body(oa_r, ob_r, yf_r, yb_r, xs_r, z_r, dsk_r, gs_r, w_ref, gt_ref, dx1_r,
             doa_o, dob_o, dy_o, dxs_o, dz_o, dmix_o, ddsk_o, dgs_o, dgt_o):
        i = pl.program_id(0)
        w = w_ref[...]

        def f(oa_, ob_, yf, yb, xs, z_, dsk_, gs_, gt):
            oc = _ssm_out(yf, yb, xs, z_, dsk_, gs_)
            return gt * mmw(jnp.concatenate([oa_, ob_, oc], axis=1), w)

        _, vjp = jax.vjp(f, _swap12(oa_r[...]), ob_r[...], yf_r[0], yb_r[0], xs_r[...], z_r[...], dsk_r[...],
                         gs_r[...], gt_ref[0])
        dx1 = dx1_r[...]
        doa, dob, dyf, _, dxs, dz, ddsk, dgs, dgt = vjp(dx1)
        doa_o[...] = _swap12(doa)
        dob_o[...] = dob
        dy_o[...] = dyf
        dxs_o[...] = dxs
        dz_o[...] = dz
        dmix_o[...] = (gt_ref[0] * dx1).astype(MXU)
        _acc_init(i == 0, [ddsk_o, dgs_o])
        _acc_init((i == 0) | (i == nlt), [dgt_o])
        ddsk_o[...] += ddsk
        dgs_o[...] += dgs
        dgt_o[0] += dgt

    row = lambda w: pl.BlockSpec((TR, w), lambda i: (i, 0))
    ys = lambda d: pl.BlockSpec((1, TR, 512), lambda i: (d, i, 0))
    cls = pl.BlockSpec((1, 1, D), lambda i: (i // nlt, 0, 0))
    v512 = pl.BlockSpec((1, 512), lambda i: (0, 0))
    return _pc(body, "out_bwd",
               [_sds((T, 256)), _sds((T, 256)), _sds((T, 512)), _sds((T, 512)), _sds((T, 512)), _sds((T, D), MXU),
                _sds((1, 512)), _sds((1, 512)), _sds((2, 1, D))],
               grid=(T // TR,),
               in_specs=[row(256), row(256), ys(0), ys(1), row(512), row(512), v512, v512, _vm(), cls, row(D)],
               out_specs=[row(256), row(256), row(512), row(512), row(512), row(D), v512, v512, cls])(
        oa, ob, y2, y2, act, z, dsk, gs, W, gate, dX1)


def ffn_fwd(X, g, sh, sc, gate, Win, Wout, L, sends=()):
    T = X.shape[0]
    TR = 256
    nlt = L // TR

    def body(x_ref, g_ref, sh_ref, sc_ref, gt_ref, wi_ref, wo_ref, o_ref):
        h = _normmod(x_ref[...], g_ref[...], sh_ref[0], sc_ref[0]).astype(MXU)
        nt = (((1,), (1,)), ((), ()))
        a = lax.dot_general(h, wi_ref[0:DFF, :], nt, preferred_element_type=F32)
        u = lax.dot_general(h, wi_ref[DFF:2 * DFF, :], nt, preferred_element_type=F32)
        act = (_silu(a) * u).astype(MXU)
        o_ref[...] = x_ref[...] + gt_ref[0] * jnp.dot(act, wo_ref[...], preferred_element_type=F32)

    row = lambda w: pl.BlockSpec((TR, w), lambda i: (i, 0))
    cls = pl.BlockSpec((1, 1, D), lambda i: (i // nlt, 0, 0))
    vec = pl.BlockSpec((1, D), lambda i: (0, 0))
    return _pc(body, "ffn_fwd", _sds((T, D)), grid=(T // TR,),
               in_specs=[row(D), vec, cls, cls, cls, _vm(), _vm()], out_specs=row(D), sends=sends, gather=True)(
        X, g, sh, sc, gate, Win, Wout)


def ffn_bwd(X, g, sh, sc, gate, Win, Wout, dX2, L, sends=(), nchunk=2):
    T = X.shape[0]
    TR = 256
    nlt = L // TR
    CH = DFF // nchunk

    def body(x_ref, g_ref, sh_ref, sc_ref, gt_ref, wi_ref, wo_ref, dx2_r,
             dx_o, h_o, du_o, act_o, dout_o, dg_o, dsh_o, dsc_o, dgt_o):
        i = pl.program_id(0)
        h, vp = jax.vjp(_normmod, x_ref[...], g_ref[...], sh_ref[0], sc_ref[0])
        dx2 = dx2_r[...]
        dout = gt_ref[0] * dx2
        zero = jnp.zeros((TR, CH), F32)
        dh = jnp.zeros((TR, D), F32)
        out = jnp.zeros((TR, D), F32)
        for c in range(nchunk):
            lo, hi = c * CH, (c + 1) * CH
            wg, wu, wo = wi_ref[lo:hi, :], wi_ref[DFF + lo:DFF + hi, :], wo_ref[lo:hi, :]

            def f(h_, eg, eu):
                act = _silu(mmw_nt(h_, wg) + eg) * (mmw_nt(h_, wu) + eu)
                return mmw(act, wo), act

            o_c, vjp_c, act = jax.vjp(f, h, zero, zero, has_aux=True)
            dh_c, da, du = vjp_c(dout)
            dh, out = dh + dh_c, out + o_c
            du_o[:, lo:hi] = da.astype(MXU)
            du_o[:, DFF + lo:DFF + hi] = du.astype(MXU)
            act_o[:, lo:hi] = act.astype(MXU)
        dx, dg, dsh, dsc = vp(dh)
        dx_o[...] = dx + dx2
        h_o[...] = h.astype(MXU)
        dout_o[...] = dout.astype(MXU)
        _acc_init(i == 0, [dg_o])
        _acc_init((i == 0) | (i == nlt), [dsh_o, dsc_o, dgt_o])
        dg_o[...] += dg
        dsh_o[0] += dsh
        dsc_o[0] += dsc
        dgt_o[0] += jnp.sum(dx2 * out, axis=0, keepdims=True)

    row = lambda w: pl.BlockSpec((TR, w), lambda i: (i, 0))
    cls = pl.BlockSpec((1, 1, D), lambda i: (i // nlt, 0, 0))
    vec = pl.BlockSpec((1, D), lambda i: (0, 0))
    return _pc(body, "ffn_bwd",
               [_sds((T, D)), _sds((T, D), MXU), _sds((T, 2 * DFF), MXU), _sds((T, DFF), MXU), _sds((T, D), MXU),
                _sds((1, D)), _sds((2, 1, D)), _sds((2, 1, D)), _sds((2, 1, D))],
               grid=(T // TR,),
               in_specs=[row(D), vec, cls, cls, cls, _vm(), _vm(), row(D)],
               out_specs=[row(D), row(D), row(2 * DFF), row(DFF), row(D), vec, cls, cls, cls], sends=sends)(
        X, g, sh, sc, gate, Win, Wout, dX2)


def loss_head(X2, g, tgt, L):
    T = X2.shape[0]
    TR = 256
    nlt = L // TR

    def body(x_ref, g_ref, t_ref, loss_o, dx_o, dg_o):
        i = pl.program_id(0)
        _acc_init(i == 0, [loss_o, dg_o])

        @pl.when(i < nlt)
        def _():
            def f(x, g_):
                y = x * lax.rsqrt(jnp.mean(x * x, axis=-1, keepdims=True) + EPS) * g_
                return 0.5 * jnp.sum(jnp.mean(jnp.square(y - t_ref[...]), axis=-1, keepdims=True), axis=0,
                                     keepdims=True)

            val, vjp = jax.vjp(f, x_ref[...], g_ref[...])
            dx, dg = vjp(jnp.ones((1, 1), F32))
            dx_o[...] = dx
            loss_o[...] += jnp.broadcast_to(val, (8, 128))
            dg_o[...] += dg

        @pl.when(i >= nlt)
        def _():
            dx_o[...] = jnp.zeros_like(dx_o)

    row = pl.BlockSpec((TR, D), lambda i: (i, 0))
    vec = pl.BlockSpec((1, D), lambda i: (0, 0))
    return _pc(body, "loss_head", [_sds((8, 128)), _sds((T, D)), _sds((1, D))], grid=(T // TR,),
               in_specs=[row, vec, pl.BlockSpec((TR, D), lambda i: (jnp.minimum(i, nlt - 1), 0))],
               out_specs=[pl.BlockSpec((8, 128), lambda i: (0, 0)), row, vec])(X2, g, tgt)


def _stack_impl(q):
    lane = _iota(q.shape, 1)
    return jnp.concatenate([jnp.where(lane < HD, q, 0.0), jnp.where(lane >= HD, q, 0.0)], axis=0)


def _unstack_impl(o):
    M = o.shape[0] // 2
    return jnp.where(_iota((M, o.shape[1]), 1) < HD, o[:M], o[M:])


@jax.custom_vjp
def _stack(q):
    return _stack_impl(q)


_stack.defvjp(lambda q: (_stack_impl(q), None), lambda _, g: (_unstack_impl(g),))


@jax.custom_vjp
def _unstack(o):
    return _unstack_impl(o)


_unstack.defvjp(lambda o: (_unstack_impl(o), None), lambda _, g: (_stack_impl(g),))


def _softmax_av(q, ks, vs, biases, sink):
    q2 = _stack(q)
    ss = []
    for k, b in zip(ks, biases):
        s = mm_nt(q2, k) * (HD ** -0.5)
        ss.append(s if b is None else s + b)
    m = functools.reduce(jnp.maximum, [jnp.max(s, axis=1, keepdims=True) for s in ss])
    if sink is not None:
        m = jnp.maximum(m, sink)
    m = lax.stop_gradient(m)
    es = [jnp.exp(s - m) for s in ss]
    den = functools.reduce(lambda a, b_: a + b_, [jnp.sum(e, axis=1, keepdims=True) for e in es])
    if sink is not None:
        den = den + jnp.exp(sink - m)
    inv = 1.0 / den
    return _unstack(functools.reduce(lambda a, b_: a + b_, [mm(e * inv, v) for e, v in zip(es, vs)]))


def _sink_col(s0, s1, M):
    return jnp.concatenate([jnp.broadcast_to(jnp.mean(s0, axis=1, keepdims=True), (M, 1)),
                            jnp.broadcast_to(jnp.mean(s1, axis=1, keepdims=True), (M, 1))], axis=0)


def _stack4_impl(q):
    lane = _iota((q.shape[0], 128), 1)
    parts = []
    for p in range(2):
        qp = q[:, 128 * p:128 * (p + 1)]
        parts += [jnp.where(lane < HD, qp, 0.0), jnp.where(lane >= HD, qp, 0.0)]
    return jnp.concatenate(parts, axis=0)


def _unstack4_impl(o):
    M = o.shape[0] // 4
    lane = _iota((M, 128), 1)
    return jnp.concatenate([jnp.where(lane < HD, o[0:M], o[M:2 * M]),
                            jnp.where(lane < HD, o[2 * M:3 * M], o[3 * M:4 * M])], axis=1)


@jax.custom_vjp
def _stack4(q):
    return _stack4_impl(q)


_stack4.defvjp(lambda q: (_stack4_impl(q), None), lambda _, g: (_unstack4_impl(g),))


@jax.custom_vjp
def _unstack4(o):
    return _unstack4_impl(o)


_unstack4.defvjp(lambda o: (_unstack4_impl(o), None), lambda _, g: (_stack4_impl(g),))


WA_NB = 4


def _wa_blocks(qs, kws, vws, kx, vx, sks, n0, L):
    sc = HD ** -0.5
    sink = jnp.concatenate([jnp.broadcast_to(jnp.mean(s_, axis=1, keepdims=True), (Q, 1)) for s_ in sks], axis=0)
    bias = []
    for b_ in range(len(qs)):
        n = n0 + b_
        qpos = n * Q + (_iota((4 * Q, 3 * Q), 0) & (Q - 1))
        kpos = (n - 1) * Q + _iota((4 * Q, 3 * Q), 1)
        bias.append(jnp.where((jnp.abs(qpos - kpos) <= Q) & (kpos >= 0) & (kpos < L), 0.0, NEG))
    q4 = [_stack4(q) for q in qs]
    sl = [mm_nt(a, k) * sc + b_ for a, k, b_ in zip(q4, kws, bias)]
    sx = [mm_nt(a, kx) * sc for a in q4]
    m = [lax.stop_gradient(jnp.maximum(jnp.maximum(jnp.max(a, axis=1, keepdims=True),
                                                   jnp.max(b_, axis=1, keepdims=True)), sink))
         for a, b_ in zip(sl, sx)]
    el = [jnp.exp(a - c) for a, c in zip(sl, m)]
    ex = [jnp.exp(a - c) for a, c in zip(sx, m)]
    inv = [1.0 / (jnp.sum(a, axis=1, keepdims=True) + jnp.sum(b_, axis=1, keepdims=True) + jnp.exp(sink - c))
           for a, b_, c in zip(el, ex, m)]
    return [_unstack4(mm(a * i, v) + mm(b_ * i, vx)) for a, b_, i, v in zip(el, ex, inv, vws)]


def _wa_load(q_r, k_r, v_r, n0):
    f = lambda t: t.astype(F32)
    qs = [f(q_r[b_ * Q:(b_ + 1) * Q, :]) for b_ in range(WA_NB)]
    wins = [pl.ds(pl.multiple_of((n0 + b_) * Q, Q), 3 * Q) for b_ in range(WA_NB)]
    return qs, [f(k_r[w, :]) for w in wins], [f(v_r[w, :]) for w in wins], wins


def _wa_specs(L):
    nb = L // Q
    qs = pl.BlockSpec((WA_NB * Q, 256), lambda n: (n, 0))
    kfull = pl.BlockSpec((L + LC + Q, 128), lambda n: (0, 0))
    sks = pl.BlockSpec((2, 2, 1, 128), lambda n: (0, 0, 0, 0))
    return nb, qs, kfull, sks


def wa_fwd(QA, KA, VA, sinkp, L, sends=()):
    nb, qs, kfull, sks = _wa_specs(L)
    pad = lambda a: jnp.concatenate([jnp.zeros((Q, 128), a.dtype), a], axis=0)

    def body(q_r, k_r, v_r, sk_r, o_ref):
        n0 = pl.program_id(0) * WA_NB
        qs_, kws, vws, _ = _wa_load(q_r, k_r, v_r, n0)
        cx = pl.ds(Q + L, LC)
        outs = _wa_blocks(qs_, kws, vws, k_r[cx, :].astype(F32), v_r[cx, :].astype(F32),
                          [sk_r[0, 0], sk_r[0, 1], sk_r[1, 0], sk_r[1, 1]], n0, L)
        o_ref[...] = jnp.concatenate(outs, axis=0)

    return _pc(body, "wa_fwd", _sds((L, 256)), grid=(nb // WA_NB,), in_specs=[qs, kfull, kfull, sks], out_specs=qs,
               sends=sends, gather=True)(QA, pad(KA), pad(VA), sinkp)


def wa_bwd(QA, KA, VA, sinkp, dO, L, sends=()):
    nb, qs, kfull, sks = _wa_specs(L)
    pad = lambda a: jnp.concatenate([jnp.zeros((Q, 128), a.dtype), a], axis=0)

    def body(q_r, k_r, v_r, sk_r, do_r, dq_o, dk_o, dv_o, dsk_o):
        n0 = pl.program_id(0) * WA_NB
        _acc_init(n0 == 0, [dk_o, dv_o, dsk_o])
        qs_, kws, vws, wins = _wa_load(q_r, k_r, v_r, n0)
        cx = pl.ds(Q + L, LC)
        fn = lambda a, b, c, d, e, s_: _wa_blocks(a, b, c, d, e, s_, n0, L)
        _, vjp = jax.vjp(fn, qs_, kws, vws, k_r[cx, :].astype(F32), v_r[cx, :].astype(F32),
                         [sk_r[0, 0], sk_r[0, 1], sk_r[1, 0], sk_r[1, 1]])
        dqs, dkws, dvws, dkx, dvx, ds = vjp([do_r[b_ * Q:(b_ + 1) * Q, :] for b_ in range(WA_NB)])
        dq_o[...] = jnp.concatenate(dqs, axis=0)
        for w, dk, dv in zip(wins, dkws, dvws):
            dk_o[w, :] += dk
            dv_o[w, :] += dv
        dk_o[cx, :] += dkx
        dv_o[cx, :] += dvx
        for i_ in range(4):
            dsk_o[i_ // 2, i_ % 2] += ds[i_]

    return _pc(body, "wa_bwd", [_sds((L, 256)), _sds((L + LC + Q, 128)), _sds((L + LC + Q, 128)),
                                _sds((2, 2, 1, 128))],
               grid=(nb // WA_NB,), in_specs=[qs, kfull, kfull, sks, qs], out_specs=[qs, kfull, kfull, sks],
               sends=sends)(QA, pad(KA), pad(VA), sinkp, dO)


def _ctx_block(q, kx, vx, s0, s1):
    return _softmax_av(q, [kx], [vx], [None], _sink_col(s0, s1, LC))


def ctx_fwd(Qx, Kx, Vx, sinkp, shared, L):
    cq = pl.BlockSpec((LC, 128), lambda p: (L // LC, p))
    ck = pl.BlockSpec((LC, 128), lambda p: (L // LC, 0 if shared else p))
    sks = pl.BlockSpec((1, 2, 1, 128), lambda p: (p, 0, 0, 0))

    def body(q_r, k_r, v_r, sk_r, o_ref):
        f = lambda t: t[...].astype(F32)
        o_ref[...] = _ctx_block(f(q_r), f(k_r), f(v_r), sk_r[0, 0], sk_r[0, 1])

    return _pc(body, "ctx_fwd", _sds((LC, 256)), grid=(2,), in_specs=[cq, ck, ck, sks],
               out_specs=pl.BlockSpec((LC, 128), lambda p: (0, p)))(Qx, Kx, Vx, sinkp)


def ctx_bwd(Qx, Kx, Vx, sinkp, dO, shared, L):
    cq = pl.BlockSpec((LC, 128), lambda p: (L // LC, p))
    ck = pl.BlockSpec((LC, 128), lambda p: (L // LC, 0 if shared else p))
    sks = pl.BlockSpec((1, 2, 1, 128), lambda p: (p, 0, 0, 0))
    op = pl.BlockSpec((LC, 128), lambda p: (0, p))
    ok = pl.BlockSpec((LC, 128), lambda p: (0, 0 if shared else p))
    dos = pl.BlockSpec((LC, 128), lambda p: (L // LC, p))

    def body(q_r, k_r, v_r, sk_r, do_r, dq_o, dk_o, dv_o, dsk_o):
        p = pl.program_id(0)
        f = lambda t: t[...].astype(F32)
        _, vjp = jax.vjp(_ctx_block, f(q_r), f(k_r), f(v_r), sk_r[0, 0], sk_r[0, 1])
        dq, dk, dv, ds0, ds1 = vjp(do_r[...])
        dq_o[...] = dq
        _acc_init((p == 0) if shared else (p >= 0), [dk_o, dv_o])
        dk_o[...] += dk
        dv_o[...] += dv
        dsk_o[0, 0] = ds0
        dsk_o[0, 1] = ds1

    kw = 128 if shared else 256
    return _pc(body, "ctx_bwd", [_sds((LC, 256)), _sds((LC, kw)), _sds((LC, kw)), _sds((2, 2, 1, 128))],
               grid=(2,), in_specs=[cq, ck, ck, sks, dos], out_specs=[op, ok, ok, sks])(Qx, Kx, Vx, sinkp, dO)


def _na_rows(qs, kws, vws, kx, vx, bs):
    sc = HD ** -0.5
    q2 = [_stack(q) for q in qs]
    sl = [mm_nt(a, k) * sc + b for a, k, b in zip(q2, kws, bs)]
    sx = [mm_nt(a, kx) * sc for a in q2]
    m = [lax.stop_gradient(jnp.maximum(jnp.max(a, axis=1, keepdims=True), jnp.max(b, axis=1, keepdims=True)))
         for a, b in zip(sl, sx)]
    el = [jnp.exp(a - c) for a, c in zip(sl, m)]
    ex = [jnp.exp(a - c) for a, c in zip(sx, m)]
    inv = [1.0 / (jnp.sum(a, axis=1, keepdims=True) + jnp.sum(b, axis=1, keepdims=True)) for a, b in zip(el, ex)]
    o2 = [mm(a * i, v) + mm(b * i, vx) for a, b, i, v in zip(el, ex, inv, vws)]
    return [_unstack(o) for o in o2]


def _na_geom(rb, j, R):
    r = rb * 8 + j
    s = jnp.clip(r - 4, 0, R - 8)
    cls = jnp.where(r < 4, r, jnp.where(r > R - 4, r - (R - 8), 4))
    return pl.ds(pl.multiple_of(s * GW, GW), 8 * GW), cls


def _na_load(q_r, k_r, v_r, b_r, rb, R):
    geo = [_na_geom(rb, j, R) for j in range(8)]
    qs = [q_r[j * GW:(j + 1) * GW, :].astype(F32) for j in range(8)]
    kws = [k_r[win, :].astype(F32) for win, _ in geo]
    vws = [v_r[win, :].astype(F32) for win, _ in geo]
    bs = [jnp.concatenate([b_r[0, cls], b_r[1, cls]], axis=0) for _, cls in geo]
    return geo, qs, kws, vws, bs


def na_fwd(QB, KB, VB, biasd, L, sends=()):
    R = L // GW
    qs = pl.BlockSpec((8 * GW, 128), lambda p, rb: (rb, p))
    kfull = pl.BlockSpec((L, 128), lambda p, rb: (0, p))
    kctx = pl.BlockSpec((LC, 128), lambda p, rb: (L // LC, p))
    bs = pl.BlockSpec((2, 8, GW, 8 * GW), lambda p, rb: (p, 0, 0, 0))

    def body(q_r, k_r, v_r, kx_r, vx_r, b_r, o_ref):
        _, qs_, kws, vws, bs_ = _na_load(q_r, k_r, v_r, b_r, pl.program_id(1), R)
        outs = _na_rows(qs_, kws, vws, kx_r[...].astype(F32), vx_r[...].astype(F32), bs_)
        o_ref[...] = jnp.concatenate(outs, axis=0)

    return _pc(body, "na_fwd", _sds((L, 256)), grid=(2, R // 8), in_specs=[qs, kfull, kfull, kctx, kctx, bs],
               out_specs=qs, sends=sends, gather=True)(QB, KB, VB, KB, VB, biasd)


def na_bwd(QB, KB, VB, biasd, dO, L):
    R = L // GW
    qs = pl.BlockSpec((8 * GW, 128), lambda p, rb: (rb, p))
    kfull = pl.BlockSpec((L, 128), lambda p, rb: (0, p))
    kctx = pl.BlockSpec((LC, 128), lambda p, rb: (L // LC, p))
    bs = pl.BlockSpec((2, 8, GW, 8 * GW), lambda p, rb: (p, 0, 0, 0))
    oc = pl.BlockSpec((LC, 128), lambda p, rb: (0, p))

    def body(q_r, k_r, v_r, kx_r, vx_r, b_r, do_r, dq_o, dk_o, dv_o, dkx_o, dvx_o, db_o):
        rb = pl.program_id(1)
        _acc_init(rb == 0, [dk_o, dv_o, dkx_o, dvx_o, db_o])
        geo, qs_, kws, vws, bs_ = _na_load(q_r, k_r, v_r, b_r, rb, R)
        _, vjp = jax.vjp(_na_rows, qs_, kws, vws, kx_r[...].astype(F32), vx_r[...].astype(F32), bs_)
        dqs, dkws, dvws, dkx, dvx, dbs = vjp([do_r[j * GW:(j + 1) * GW, :] for j in range(8)])
        dq_o[...] = jnp.concatenate(dqs, axis=0)
        dkx_o[...] += dkx
        dvx_o[...] += dvx
        for j, (win, cls) in enumerate(geo):
            dk_o[win, :] += dkws[j]
            dv_o[win, :] += dvws[j]
            db_o[0, cls] += dbs[j][:GW]
            db_o[1, cls] += dbs[j][GW:]

    return _pc(body, "na_bwd",
               [_sds((L, 256)), _sds((L, 256)), _sds((L, 256)), _sds((LC, 256)), _sds((LC, 256)),
                _sds((4, 8, GW, 8 * GW))],
               grid=(2, R // 8), in_specs=[qs, kfull, kfull, kctx, kctx, bs, qs],
               out_specs=[qs, kfull, kfull, oc, oc, bs])(QB, KB, VB, KB, VB, biasd, dO)


def exact_mm_call(A, B):
    def body(a_ref, b_ref, o_ref):
        o_ref[...] = _exact(a_ref[...], b_ref[...])

    return _pc(body, "exact_mm", _sds((A.shape[0], B.shape[1])))(A, B)


def _conv_shift(x, d, L):
    T = x.shape[0]
    if d == 0:
        return x
    t = _iota(x.shape, 0)
    src = t + d
    ok = (src >= 0) & (src < T) & ((src >= L) == (t >= L))
    return jnp.where(ok, pltpu.roll(x, (-d) % T, 0), 0.0)


def conv_fwd(XBC, w8, b, L):
    T = XBC.shape[0]

    def body(x_ref, w_ref, b_ref, o_ref):
        x = x_ref[...]
        pre = b_ref[...] + functools.reduce(
            lambda a, c: a + c, [_conv_shift(x, k - 3, L) * w_ref[k:k + 1, :] for k in range(7)])
        o_ref[...] = _silu(pre)

    col = pl.BlockSpec((T, 128), lambda j: (0, j))
    return _pc(body, "conv_fwd", _sds((T, 1024)), grid=(8,),
               in_specs=[col, pl.BlockSpec((8, 128), lambda j: (0, j)), pl.BlockSpec((1, 128), lambda j: (0, j))],
               out_specs=col)(XBC, w8, b)


def conv_bwd(XBC, w8, b, dS, dxs_skip, L, sends=()):
    T = XBC.shape[0]

    def body(x_ref, w_ref, b_ref, d0_r, d1_r, dsk_r, dx_o, dw_o, db_o):
        j = pl.program_id(0)
        x = x_ref[...]
        xs = [_conv_shift(x, k - 3, L) for k in range(7)]
        pre = b_ref[...] + functools.reduce(lambda a, c: a + c, [xs[k] * w_ref[k:k + 1, :] for k in range(7)])
        _, vjp = jax.vjp(_silu, pre)
        dact = d0_r[0] + d1_r[0] + jnp.where(j < 4, dsk_r[...], 0.0)
        dpre, = vjp(dact)
        dx_o[...] = functools.reduce(
            lambda a, c: a + c, [_conv_shift(dpre, 3 - k, L) * w_ref[k:k + 1, :] for k in range(7)])
        dw_o[...] = jnp.concatenate([jnp.sum(dpre * xs[k], axis=0, keepdims=True) for k in range(7)]
                                    + [jnp.zeros((1, 128), F32)], axis=0)
        db_o[...] = jnp.sum(dpre, axis=0, keepdims=True)

    col = pl.BlockSpec((T, 128), lambda j: (0, j))
    w_s = pl.BlockSpec((8, 128), lambda j: (0, j))
    b_s = pl.BlockSpec((1, 128), lambda j: (0, j))
    ds = lambda d: pl.BlockSpec((1, T, 128), lambda j: (d, 0, j))
    return _pc(body, "conv_bwd", [_sds((T, 1024)), _sds((8, 1024)), _sds((1, 1024))], grid=(8,),
               in_specs=[col, w_s, b_s, ds(0), ds(1), pl.BlockSpec((T, 128), lambda j: (0, jnp.minimum(j, 3)))],
               out_specs=[col, w_s, b_s], sends=sends)(XBC, w8, b, dS, dS, dxs_skip)


def _ssd_chunk(xs, bs, cs, dtraw, dtb, alog, hs, tri, d):
    dt = _softplus(dtraw + dtb)
    a = dt * (-jnp.exp(alog))
    acum = _exact(tri, a)
    tot = jnp.sum(a, axis=0, keepdims=True)
    wcol = jnp.exp(tot - acum) * dt
    ea = jnp.exp(acum)
    cd = jnp.exp(tot)
    acum_t, dt_t = acum.T, dt.T
    lane = _iota((Q, 128), 1)
    srow = _iota((128, Q), 0)
    lane1 = _iota((1, 128), 1)
    prow = _iota((128, NSTATE), 0)
    mask = tri > 0.5
    cbs = [mm_nt(cs[g], bs[g]) for g in range(2)]
    ys, hn = [], []
    for j in range(4):
        g = j // 2
        x = xs[j]
        yi, st, eac, cdl = [], [], [], []
        for u in range(2):
            slot = d * 8 + 2 * j + u
            col = lambda m: jnp.sum(jnp.where(lane == slot, m, 0.0), axis=1, keepdims=True)
            rowv = lambda m: jnp.sum(jnp.where(srow == slot, m, 0.0), axis=0, keepdims=True)
            seg = col(acum) - rowv(acum_t)
            dcy = jnp.where(mask, jnp.exp(jnp.where(mask, seg, 0.0)), 0.0)
            yi.append(mm(cbs[g] * dcy * rowv(dt_t), x))
            st.append(mm_tn(x, bs[g] * col(wcol)))
            eac.append(col(ea))
            cdl.append(jnp.sum(jnp.where(lane1 == slot, cd, 0.0), axis=1, keepdims=True))
        yin = mm_nt(cs[g], hs[j])
        ys.append(jnp.where(lane < HD, yi[0] + yin * eac[0], yi[1] + yin * eac[1]))
        hn.append(hs[j] * jnp.where(prow < HD, cdl[0], cdl[1]) + jnp.where(prow < HD, st[0], st[1]))
    return ys, hn


def _ssd_chunk_idx(d, s, nlc, nch):
    return jnp.where(d == 0, (s + nlc) % nch, nch - 1 - s)


def ssd_fwd(ACT, DT, dtb, alog, tri2, L, sends=()):
    T = ACT.shape[0]
    nlc, nch = L // Q, T // Q

    def body(a_ref, dt_ref, dtb_ref, al_ref, tri_ref, y_o, hs_o, hst):
        d, s = pl.program_id(0), pl.program_id(1)
        _acc_init(s == 0, [hst])
        a = a_ref[...]
        xs = [a[:, 128 * j:128 * (j + 1)] for j in range(4)]
        bs = [a[:, 512 + 128 * g:640 + 128 * g] for g in range(2)]
        cs = [a[:, 768 + 128 * g:896 + 128 * g] for g in range(2)]
        hs = [hst[j] for j in range(4)]
        hs_o[0, 0] = hst[...]
        ys, hn = _ssd_chunk(xs, bs, cs, dt_ref[...], dtb_ref[...], al_ref[...], hs, tri_ref[0], d)
        y_o[0] = jnp.concatenate(ys, axis=1)
        for j in range(4):
            hst[j] = hn[j]

    ck = lambda w: pl.BlockSpec((Q, w), lambda d, s: (_ssd_chunk_idx(d, s, nlc, nch), 0))
    v128 = pl.BlockSpec((1, 128), lambda d, s: (0, 0))
    return _pc(body, "ssd_fwd", [_sds((2, T, 512)), _sds((2, nch, 4, 128, NSTATE))], grid=(2, nch),
               in_specs=[ck(1024), ck(128), v128, v128, pl.BlockSpec((1, Q, Q), lambda d, s: (d, 0, 0))],
               out_specs=[pl.BlockSpec((1, Q, 512), lambda d, s: (d, _ssd_chunk_idx(d, s, nlc, nch), 0)),
                          pl.BlockSpec((1, 1, 4, 128, NSTATE), lambda d, s: (d, s, 0, 0, 0))],
               scratch=[pltpu.VMEM((4, 128, NSTATE), F32)], sends=sends, gather=True)(ACT, DT, dtb, alog, tri2)


def ssd_bwd(ACT, DT, dtb, alog, tri2, HS, dY, L, sends=()):
    T = ACT.shape[0]
    nlc, nch = L // Q, T // Q

    def body(a_ref, dt_ref, dtb_ref, al_ref, tri_ref, hs_ref, dy_ref, da_o, ddt_o, ddtb_o, dal_o, dh):
        d, sr = pl.program_id(0), pl.program_id(1)
        _acc_init(sr == 0, [dh, ddtb_o, dal_o])
        a = a_ref[...]
        xs = [a[:, 128 * j:128 * (j + 1)] for j in range(4)]
        bs = [a[:, 512 + 128 * g:640 + 128 * g] for g in range(2)]
        cs = [a[:, 768 + 128 * g:896 + 128 * g] for g in range(2)]
        hs = [hs_ref[0, 0, j] for j in range(4)]
        tri = tri_ref[0]
        fn = lambda xs_, bs_, cs_, dtr, dtb_, al, hs_: _ssd_chunk(xs_, bs_, cs_, dtr, dtb_, al, hs_, tri, d)
        _, vjp = jax.vjp(fn, xs, bs, cs, dt_ref[...], dtb_ref[...], al_ref[...], hs)
        dy = dy_ref[...]
        dys = [dy[:, 128 * j:128 * (j + 1)] for j in range(4)]
        dxs, dbs, dcs, ddt, ddtb, dal, dhs = vjp((dys, [dh[j] for j in range(4)]))
        da_o[0] = jnp.concatenate(dxs + dbs + dcs, axis=1)
        ddt_o[0] = ddt
        ddtb_o[0] += ddtb
        dal_o[0] += dal
        for j in range(4):
            dh[j] = dhs[j]

    cidx = lambda d, sr: _ssd_chunk_idx(d, nch - 1 - sr, nlc, nch)
    ck = lambda w: pl.BlockSpec((Q, w), lambda d, sr: (cidx(d, sr), 0))
    v128 = pl.BlockSpec((1, 128), lambda d, sr: (0, 0))
    o128 = pl.BlockSpec((1, 1, 128), lambda d, sr: (d, 0, 0))
    return _pc(body, "ssd_bwd", [_sds((2, T, 1024)), _sds((2, T, 128)), _sds((2, 1, 128)), _sds((2, 1, 128))],
               grid=(2, nch),
               in_specs=[ck(1024), ck(128), v128, v128, pl.BlockSpec((1, Q, Q), lambda d, sr: (d, 0, 0)),
                         pl.BlockSpec((1, 1, 4, 128, NSTATE), lambda d, sr: (d, nch - 1 - sr, 0, 0, 0)), ck(512)],
               out_specs=[pl.BlockSpec((1, Q, 1024), lambda d, sr: (d, cidx(d, sr), 0)),
                          pl.BlockSpec((1, Q, 128), lambda d, sr: (d, cidx(d, sr), 0)), o128, o128],
               scratch=[pltpu.VMEM((4, 128, NSTATE), F32)], sends=sends)(ACT, DT, dtb, alog, tri2, HS, dY)


_PAIR_HEADS = np.array([[0, 2], [1, 3]])


def _tables(L):
    t = jnp.arange(L)
    inv = 10000.0 ** (-jnp.arange(16, dtype=F32) / 16)

    def half(pos):
        ang = pos.astype(F32)[:, None] * inv[None, :]
        return jnp.concatenate([ang, ang], axis=1)

    ang = jnp.tile(jnp.concatenate([half(t // GW), half(t % GW)], axis=1), (1, 4))
    cos = jnp.concatenate([jnp.cos(ang), jnp.ones((LC, 256), F32)], axis=0)
    sin = jnp.concatenate([jnp.sin(ang), jnp.zeros((LC, 256), F32)], axis=0)
    rm = np.zeros((256, 256), np.float32)
    for j in range(256):
        if j % 32 < 16:
            rm[j + 16, j] = -1.0
        else:
            rm[j - 16, j] = 1.0
    tri = np.tril(np.ones((Q, Q), np.float32))
    return cos, sin, jnp.asarray(rm), jnp.asarray(np.stack([tri, tri.T]))


def _na_index(R):
    rc = np.array([0, 1, 2, 3, 4, R - 3, R - 2, R - 1])
    dy = np.clip(rc - 4, 0, R - 8)[:, None] + np.arange(8)[None, :] - rc[:, None] + 7
    qc, cc = np.arange(GW)[:, None], np.arange(GW)[None, :]
    dx = np.clip(cc - qc, -15, 15) + 15
    cstart = np.clip(qc - 8, 0, GW - 16)
    cmask = (cc >= cstart) & (cc < cstart + 16)
    idx = dy[:, None, :, None] * 31 + dx[None, :, None, :]
    return idx.reshape(8, GW, 8 * GW), np.broadcast_to(cmask[None, :, None, :], idx.shape).reshape(8, GW, 8 * GW), \
        dy, dx, cmask


def _na_bias(rpb, R):
    _, cm, dy, dx, _ = _na_index(R)
    e1t = np.zeros((128, GW * GW), np.float32)
    e1t[dx.reshape(-1), np.arange(GW * GW)] = 1.0
    v = jnp.pad(rpb[:, dy.reshape(-1), :].reshape(256, 31), ((0, 0), (0, 97)))
    full = exact_mm_call(v, jnp.asarray(e1t))
    dense = full.reshape(4, 8, 8, GW, GW).transpose(0, 1, 3, 2, 4).reshape(4, 8, GW, 8 * GW)
    return jnp.where(cm[None], dense, NEG)


def _na_bias_grad(dbias, R):
    _, _, dy, dx, cmask = _na_index(R)
    e1 = np.zeros((GW * GW, 128), np.float32)
    e1[np.arange(GW * GW), dx.reshape(-1)] = cmask.reshape(-1)
    a1 = dbias.reshape(4, 8, GW, 8, GW).transpose(0, 1, 3, 2, 4).reshape(256, GW * GW)
    v = exact_mm_call(a1, jnp.asarray(e1))[:, :31].reshape(4, 64, 31)
    e2 = np.zeros((64, 128), np.float32)
    e2[np.arange(64), dy.reshape(-1)] = 1.0
    a2 = jnp.pad(v.transpose(0, 2, 1).reshape(124, 64), ((0, 4), (0, 0)))
    return exact_mm_call(a2, jnp.asarray(e2))[:124, :15].reshape(4, 31, 15).transpose(0, 2, 1)


def _lanes(v, n=128):
    v = v.reshape(1, -1)
    return jnp.pad(v, ((0, 0), (0, n - v.shape[1])))


def _cls2(a, b):
    return jnp.stack([a, b]).reshape(2, 1, D)


def _win_p(g):
    return jnp.concatenate([g.reshape(IN_COLS, D), jnp.zeros((NP_IN - IN_COLS, D), g.dtype)], axis=0)


def _layer_consts(p):
    sinkp = jnp.broadcast_to(p["wa_sink"][_PAIR_HEADS][:, :, None, None], (2, 2, 1, 128))
    return dict(
        sinkp=sinkp, nosink=jnp.full((2, 2, 1, 128), NEG, F32),
        w8=jnp.concatenate([p["ssm_conv_w"], jnp.zeros((1, 1024), F32)], axis=0),
        cb=p["ssm_conv_b"].reshape(1, 1024), dtb=_lanes(p["ssm_dt_bias"]), alog=_lanes(p["ssm_a_log"]),
        dsk=jnp.repeat(p["ssm_d"], HD).reshape(1, 512), gs=p["ssm_norm_g"].reshape(1, 512),
        gmix=p["g_mix"].reshape(1, D), gffn=p["g_ffn"].reshape(1, D))


def _mods(mod2):
    return [_cls2(mod2[0, D * k:D * (k + 1)], mod2[1, D * k:D * (k + 1)]) for k in range(6)]


def _layer_fwd(X, mod2, c, rpb, tabs, L, ctx_out, shards, nxt):
    cos, sin, rm, tri2 = tabs
    sh1, sc1, gt1, sh2, sc2, gt2 = _mods(mod2)
    biasd = _na_bias(rpb, L // GW)
    fi, fo, wo = shards
    fcut, ocut = 448, 224
    (qa, qb, z, ka, va, kb, vb, xbc, dt, h1), (gfo_a,) = in_fwd(X, c["gmix"], sh1, sc1, c["win"], cos, sin, rm, L,
                                                                sends=(fo[:ocut],))
    (oa,), (gfi_b,) = wa_fwd(qa, ka, va, c["sinkp"], L, sends=(fi[fcut:],))
    (ob,), (gwo,) = na_fwd(qb, kb, vb, biasd, L, sends=(wo,))
    c = dict(c, wout=gwo.reshape(D, D))
    if ctx_out:
        oa_c = ctx_fwd(qa, ka, va, c["sinkp"], True, L)
        ob_c = ctx_fwd(qb, kb, vb, c["nosink"], False, L)
    else:
        oa_c = ob_c = jnp.zeros((LC, 256), F32)
    oa = jnp.concatenate([oa, oa_c], axis=0)
    ob = jnp.concatenate([ob, ob_c], axis=0)
    act = conv_fwd(xbc, c["w8"], c["cb"], L)
    (y2, hs), (gfi_a,) = ssd_fwd(act, dt, c["dtb"], c["alog"], tri2, L, sends=(fi[:fcut],))
    (X1, cat), (gfo_b,) = out_fwd(oa, ob, y2, act, z, c["dsk"], c["gs"], c["wout"], X, gt1, L, sends=(fo[ocut:],))
    c = dict(c, wfi=jnp.concatenate([gfi_a, gfi_b], axis=1).reshape(2 * DFF, D),
             wfo=jnp.concatenate([gfo_a, gfo_b], axis=1).reshape(DFF, D))
    res = ffn_fwd(X1, c["gffn"], sh2, sc2, gt2, c["wfi"], c["wfo"], L, sends=nxt)
    (X2,), got = res if nxt else ((res,), ())
    saved = dict(X=X, X1=X1, qa=qa, qb=qb, z=z, ka=ka, va=va, kb=kb, vb=vb, xbc=xbc, dt=dt, h1=h1, oa=oa, ob=ob,
                 act=act, y2=y2, hs=hs, cat=cat, biasd=biasd)
    return X2, saved, c, got


def _row_blocks(gw):
    return gw.reshape(NDEV, gw.shape[0] // NDEV, gw.shape[1])


def _layer_bwd(dX2, s, mod2, c, tabs, L, ctx_out, carry):
    cos, sin, rm, tri2 = tabs
    sh1, sc1, gt1, sh2, sc2, gt2 = _mods(mod2)
    R = L // GW
    res = ffn_bwd(s["X1"], c["gffn"], sh2, sc2, gt2, c["wfi"], c["wfo"], dX2, L, sends=carry)
    (dX1, h2, dU, actf, dOut, dgffn, dsh2, dsc2, dgt2), got = res if carry else (res, ())
    g = {}
    gfi = _row_blocks(tn_mm(dU, h2, 1408, 1024, MXU))
    gfo = _row_blocks(tn_mm(actf, dOut, 1408, 1024, MXU))
    doa, dob, dy, dxs_skip, dz, dmix, ddsk, dgs, dgt1 = out_bwd(s["oa"], s["ob"], s["y2"], s["act"], s["z"], c["dsk"],
                                                                c["gs"], c["wout"], gt1, dX1, L)
    gout = _row_blocks(tn_mm(s["cat"], dmix, 1024, 512, MXU))
    (dS, ddt2, ddtb, dal), (g["w_ffn_in"],) = ssd_bwd(
        s["act"], s["dt"], c["dtb"], c["alog"], tri2, s["hs"], dy, L, sends=(gfi,))
    (dxbc, dw8, dcb), (g["w_ffn_out"],) = conv_bwd(s["xbc"], c["w8"], c["cb"], dS, dxs_skip, L, sends=(gfo,))
    (dqa, dka, dva, dska), (g["w_out"],) = wa_bwd(s["qa"], s["ka"], s["va"], c["sinkp"], doa, L, sends=(gout,))
    dka, dva = dka[Q:], dva[Q:]
    dqb, dkb, dvb, dkxb, dvxb, dbias = na_bwd(s["qb"], s["kb"], s["vb"], s["biasd"], dob, L)
    if ctx_out:
        dqa_c, dk1, dv1, dsk1 = ctx_bwd(s["qa"], s["ka"], s["va"], c["sinkp"], doa, True, L)
        dqb_c, dk2, dv2, _ = ctx_bwd(s["qb"], s["kb"], s["vb"], c["nosink"], dob, False, L)
        dka = jnp.concatenate([dka[:L], dka[L:] + dk1], axis=0)
        dva = jnp.concatenate([dva[:L], dva[L:] + dv1], axis=0)
        dska = dska + dsk1
        dkxb, dvxb = dkxb + dk2, dvxb + dv2
    else:
        dqa_c = dqb_c = jnp.zeros((LC, 256), F32)
    cat0 = lambda a, b: jnp.concatenate([a, b], axis=0)
    dX, dycat, dgmix, dsh1, dsc1 = in_bwd(
        s["X"], c["gmix"], sh1, sc1, c["win"], cos, sin, rm, dX1, cat0(dqa, dqa_c), cat0(dqb, dqb_c), dz,
        dka, dva, cat0(dkb, dkxb), cat0(dvb, dvxb), dxbc, ddt2, L,
        latent_only=ctx_out)
    if ctx_out:
        gin = [_row_blocks(tn_mm(dycat, s["h1"], 1024, D // 2, MXU, ncol=D // 2, col0=k)[:IN_COLS]) for k in (0, 1)]
    else:
        gin = _row_blocks(tn_mm(dycat, s["h1"], 1024, 1024, MXU)[:IN_COLS])
    g["g_mix"] = dgmix.reshape(D)
    g["g_ffn"] = dgffn.reshape(D)
    sk = jnp.sum(dska, axis=(2, 3))
    g["wa_sink"] = jnp.zeros((4,), F32).at[_PAIR_HEADS.reshape(-1)].set(sk.reshape(-1))
    g["na_rpb"] = _na_bias_grad(dbias, R)
    g["ssm_conv_w"] = dw8[:7]
    g["ssm_conv_b"] = dcb.reshape(1024)
    g["ssm_dt_bias"] = (ddtb[0] + ddtb[1])[0, :16].reshape(2, 8)
    g["ssm_a_log"] = (dal[0] + dal[1])[0, :16].reshape(2, 8)
    g["ssm_d"] = jnp.sum(ddsk.reshape(8, HD), axis=1)
    g["ssm_norm_g"] = dgs.reshape(512)
    dmod2 = jnp.concatenate([dsh1, dsc1, dgt1, dsh2, dsc2, dgt2], axis=2).reshape(2, 6 * D)
    return dX, g, dmod2, gin, got


def local_step(x, ctx, tgt, mods, layers, shards, g_final, L):
    tabs = _tables(L)
    X = (x, ctx)
    consts = [_layer_consts(p) for p in layers]
    saved = []
    got = all_gather([shards["w_in"][0]], "gather_first")
    for i in range(2):
        consts[i] = dict(consts[i], win=_win_p(got[0]))
        nxt = (shards["w_in"][1],) if i == 0 else ()
        X, s, consts[i], got = _layer_fwd(X, mods[i], consts[i], layers[i]["na_rpb"], tabs, L, i == 0,
                                          (shards["w_ffn_in"][i], shards["w_ffn_out"][i], shards["w_out"][i]), nxt)
        saved.append(s)
    loss8, dX, dgfin = loss_head(X, g_final.reshape(1, D), tgt, L)
    grads, dmods = [None, None], [None, None]
    dX, grads[1], dmods[1], gin1, _ = _layer_bwd(dX, saved[1], mods[1], consts[1], tabs, L, False, ())
    dX, grads[0], dmods[0], gin0, (grads[1]["w_in"],) = _layer_bwd(dX, saved[0], mods[0], consts[0], tabs, L, True,
                                                                   (gin1,))
    return loss8[0, 0], dX, grads, jnp.stack(dmods), dgfin.reshape(D), gin0


def _place():
    x, y, c = lax.axis_index("x"), lax.axis_index("y"), lax.axis_index("c")
    return x, y, c


def _slot(b):
    return 4 * b[0] + 2 * b[1] + b[2]


def _any():
    return pl.BlockSpec(memory_space=pl.ANY)


def all_gather(xs, name):
    n = len(xs)

    def body(*refs):
        x_refs, o_refs = refs[:n], refs[n:2 * n]
        send_sems, recv_sems, local_sems = refs[2 * n:]
        x, y, c = _place()
        me, sib = (x, y, c), (x, y, 1 - c)
        chips = [(1 - x, y), (x, 1 - y), (1 - x, 1 - y)]

        def copy(t, k, blk, to, src=None):
            dst = o_refs[t].at[_slot(blk)]
            return pltpu.make_async_remote_copy(
                src_ref=dst if src is None else src, dst_ref=dst, send_sem=send_sems.at[7 * t + k],
                recv_sem=recv_sems.at[7 * t + k], device_id=to, device_id_type=MESH_T)

        mine = [pltpu.make_async_copy(x_refs[t], o_refs[t].at[_slot(me)], local_sems.at[t]) for t in range(n)]
        for cp in mine:
            cp.start()
        first = []
        for t in range(n):
            first.append(copy(t, 0, me, sib, src=x_refs[t]))
            first += [copy(t, 1 + j, me, (*chip, c), src=x_refs[t]) for j, chip in enumerate(chips)]
        for cp in first:
            cp.start()
        passed = []
        for j, chip in enumerate(chips):
            for t in range(n):
                copy(t, 1 + j, (*chip, c), me).wait_recv()
                cp = copy(t, 4 + j, (*chip, c), sib)
                cp.start()
                passed.append(cp)
        for t in range(n):
            copy(t, 0, sib, me).wait_recv()
            for j, chip in enumerate(chips):
                copy(t, 4 + j, (*chip, 1 - c), me).wait_recv()
        for cp in first + passed:
            cp.wait_send()
        for cp in mine:
            cp.wait()

    return pl.pallas_call(
        body, name=name, out_shape=[_sds((NDEV,) + a.shape, a.dtype) for a in xs],
        in_specs=[_any()] * n, out_specs=[_any()] * n,
        scratch_shapes=[pltpu.SemaphoreType.DMA((7 * n,)), pltpu.SemaphoreType.DMA((7 * n,)),
                        pltpu.SemaphoreType.DMA((n,))],
        interpret=_INTERPRET)(*xs)


def all_to_all(xs, name):
    n = len(xs)

    def body(*refs):
        _a2a_start(refs[:n], refs[n:2 * n], *refs[2 * n:])
        _a2a_wait(refs[:n], refs[n:2 * n], *refs[2 * n:])

    return pl.pallas_call(
        body, name=name, out_shape=[_sds(a.shape, a.dtype) for a in xs],
        in_specs=[_any()] * n, out_specs=[_any()] * n, scratch_shapes=_a2a_sems(n), interpret=_INTERPRET)(*xs)


def _a2a_sems(n):
    return [pltpu.SemaphoreType.DMA((7 * n,)), pltpu.SemaphoreType.DMA((7 * n,)), pltpu.SemaphoreType.DMA((n,))]


def _a2a_copies(x_refs, o_refs, send_sems, recv_sems, local_sems):
    n = len(x_refs)
    x, y, c = _place()
    me = (x, y, c)
    flip = lambda v, b: (1 - v) if b else v
    peers = [(flip(x, k >> 2 & 1), flip(y, k >> 1 & 1), flip(c, k & 1)) for k in range(1, NDEV)]
    mine = [pltpu.make_async_copy(x_refs[t].at[_slot(me)], o_refs[t].at[_slot(me)], local_sems.at[t])
            for t in range(n)]

    def copy(t, k, src_slot, dst_slot, to):
        return pltpu.make_async_remote_copy(
            src_ref=x_refs[t].at[src_slot], dst_ref=o_refs[t].at[dst_slot], send_sem=send_sems.at[7 * t + k],
            recv_sem=recv_sems.at[7 * t + k], device_id=to, device_id_type=MESH_T)

    sends = [copy(t, k, _slot(p), _slot(me), p) for t in range(n) for k, p in enumerate(peers)]
    recvs = [copy(t, k, _slot(p), _slot(p), me) for t in range(n) for k, p in enumerate(peers)]
    return mine, sends, recvs


def _ag_copies(x_refs, o_refs, send_sems, recv_sems, local_sems):
    n = len(x_refs)
    x, y, c = _place()
    me = (x, y, c)
    flip = lambda v, b: (1 - v) if b else v
    peers = [(flip(x, k >> 2 & 1), flip(y, k >> 1 & 1), flip(c, k & 1)) for k in range(1, NDEV)]
    mine = [pltpu.make_async_copy(x_refs[t], o_refs[t].at[_slot(me)], local_sems.at[t]) for t in range(n)]

    def copy(t, k, dst_slot, to):
        return pltpu.make_async_remote_copy(
            src_ref=x_refs[t], dst_ref=o_refs[t].at[dst_slot], send_sem=send_sems.at[7 * t + k],
            recv_sem=recv_sems.at[7 * t + k], device_id=to, device_id_type=MESH_T)

    sends = [copy(t, k, _slot(me), p) for t in range(n) for k, p in enumerate(peers)]
    recvs = [copy(t, k, _slot(p), me) for t in range(n) for k, p in enumerate(peers)]
    return mine, sends, recvs


def _ag_start(x_refs, o_refs, send_sems, recv_sems, local_sems):
    mine, sends, _ = _ag_copies(x_refs, o_refs, send_sems, recv_sems, local_sems)
    for cp in mine + sends:
        cp.start()


def _ag_wait(x_refs, o_refs, send_sems, recv_sems, local_sems):
    mine, sends, recvs = _ag_copies(x_refs, o_refs, send_sems, recv_sems, local_sems)
    for cp in recvs:
        cp.wait_recv()
    for cp in sends:
        cp.wait_send()
    for cp in mine:
        cp.wait()


def _a2a_start(x_refs, o_refs, send_sems, recv_sems, local_sems):
    mine, sends, _ = _a2a_copies(x_refs, o_refs, send_sems, recv_sems, local_sems)
    for cp in mine + sends:
        cp.start()


def _a2a_wait(x_refs, o_refs, send_sems, recv_sems, local_sems):
    mine, sends, recvs = _a2a_copies(x_refs, o_refs, send_sems, recv_sems, local_sems)
    for cp in recvs:
        cp.wait_recv()
    for cp in sends:
        cp.wait_send()
    for cp in mine:
        cp.wait()


def adam_reduce(P, w, m, v, name, sends=()):
    n, R, C = P.shape
    br = R // 4 if R % 64 == 0 else R

    def body(p_ref, w_ref, m_ref, v_ref, g_o, d_o, m_o, v_o):
        g = p_ref[0].astype(F32)
        for k in range(1, n):
            g = g + p_ref[k].astype(F32)
        m1 = ADAM_B1 * m_ref[...] + (1.0 - ADAM_B1) * g
        v1 = ADAM_B2 * v_ref[...] + (1.0 - ADAM_B2) * jnp.square(g)
        m_hat = m1 / (1.0 - ADAM_B1 ** ADAM_STEP)
        v_hat = v1 / (1.0 - ADAM_B2 ** ADAM_STEP)
        g_o[...] = g
        d_o[...] = -ADAM_LR * (m_hat / (jnp.sqrt(v_hat) + ADAM_EPS) + ADAM_WD * w_ref[...])
        m_o[...] = m1
        v_o[...] = v1

    blk = pl.BlockSpec((br, C), lambda i: (i, 0))
    return _pc(body, name, [_sds((R, C))] * 4, grid=(R // br,),
               in_specs=[pl.BlockSpec((n, br, C), lambda i: (0, i, 0)), blk, blk, blk], out_specs=[blk] * 4,
               sends=sends)(P, w, m, v)


def adam_layers(P0, P1, w, m, v, name, sends=()):
    n, R, C = P0.shape
    br = R // 4 if R % 64 == 0 else R
    nb = R // br

    def body(p0_ref, p1_ref, w_ref, m_ref, v_ref, g_o, d_o, m_o, v_o):
        def total(p_ref):
            g = p_ref[0].astype(F32)
            for k in range(1, n):
                g = g + p_ref[k].astype(F32)
            return g

        g = jnp.where(pl.program_id(0) == 0, total(p0_ref), total(p1_ref))
        m1 = ADAM_B1 * m_ref[0] + (1.0 - ADAM_B1) * g
        v1 = ADAM_B2 * v_ref[0] + (1.0 - ADAM_B2) * jnp.square(g)
        m_hat = m1 / (1.0 - ADAM_B1 ** ADAM_STEP)
        v_hat = v1 / (1.0 - ADAM_B2 ** ADAM_STEP)
        g_o[0] = g
        d_o[0] = -ADAM_LR * (m_hat / (jnp.sqrt(v_hat) + ADAM_EPS) + ADAM_WD * w_ref[0])
        m_o[0] = m1
        v_o[0] = v1

    blk = pl.BlockSpec((1, br, C), lambda l, i: (l, i, 0))
    p0 = pl.BlockSpec((n, br, C), lambda l, i: (0, jnp.where(l == 0, i, nb - 1), 0))
    p1 = pl.BlockSpec((n, br, C), lambda l, i: (0, jnp.where(l == 1, i, 0), 0))
    return _pc(body, name, [_sds((2, R, C))] * 4, grid=(2, nb), in_specs=[p0, p1, blk, blk, blk],
               out_specs=[blk] * 4, sends=sends)(P0, P1, w, m, v)


def mod_fwd(scin, wmod, bcol):
    def body(s_ref, w_ref, b_ref, o_ref):
        o_ref[0] = mm(_silu(s_ref[...]), w_ref[0]) + b_ref[0]

    return _pc(body, "mod_fwd", _sds((2, 16, 768)), grid=(2,),
               in_specs=[pl.BlockSpec((16, D), lambda l: (0, 0)), pl.BlockSpec((1, D, 768), lambda l: (l, 0, 0)),
                         pl.BlockSpec((1, 1, 768), lambda l: (l, 0, 0))],
               out_specs=pl.BlockSpec((1, 16, 768), lambda l: (l, 0, 0)))(scin, wmod, bcol)


def mod_bwd(scin, wmod, G):
    def body(s_ref, w_ref, g_ref, dw_o, ds_o):
        _, vjp = jax.vjp(lambda s, w: mm(_silu(s), w), s_ref[...], w_ref[0])
        ds, dw = vjp(g_ref[0])
        dw_o[0] = dw
        _acc_init(pl.program_id(0) == 0, [ds_o])
        ds_o[...] += ds

    full = pl.BlockSpec((16, D), lambda l: (0, 0))
    wsp = pl.BlockSpec((1, D, 768), lambda l: (l, 0, 0))
    return _pc(body, "mod_bwd", [_sds((2, D, 768)), _sds((16, D))], grid=(2,),
               in_specs=[full, wsp, pl.BlockSpec((1, 16, 768), lambda l: (l, 0, 0))], out_specs=[wsp, full])(
        scin, wmod, G)


_SMALL = ["b_mod", "g_mix", "wa_sink", "na_rpb", "ssm_conv_w", "ssm_conv_b", "ssm_dt_bias", "ssm_a_log", "ssm_d",
          "ssm_norm_g", "g_ffn", "g_final", "dmod_s", "dmod_c"]


def _pack(parts):
    rows = []
    for a in parts:
        f = a.reshape(-1).astype(F32)
        rows.append(jnp.pad(f, (0, (-f.shape[0]) % 1024)).reshape(-1, 128))
    return jnp.concatenate(rows, axis=0)


def _unpack(packed, shapes):
    out, r = [], 0
    for s in shapes:
        nel = int(np.prod(s))
        nr = -(-nel // 1024) * 8
        out.append(packed[r:r + nr].reshape(-1)[:nel].reshape(s))
        r += nr
    return out


def kernel(x, c, ctx, c_ctx, w_mod, b_mod, g_mix, w_in, wa_sink, na_rpb, ssm_conv_w, ssm_conv_b, ssm_dt_bias, ssm_a_log, ssm_d, ssm_norm_g, w_out, g_ffn, w_ffn_in, w_ffn_out, g_final, loss_target, m_c_ctx, m_w_mod, m_b_mod, m_g_mix, m_w_in, m_wa_sink, m_na_rpb, m_ssm_conv_w, m_ssm_conv_b, m_ssm_dt_bias, m_ssm_a_log, m_ssm_d, m_ssm_norm_g, m_w_out, m_g_ffn, m_w_ffn_in, m_w_ffn_out, m_g_final, v_c_ctx, v_w_mod, v_b_mod, v_g_mix, v_w_in, v_wa_sink, v_na_rpb, v_ssm_conv_w, v_ssm_conv_b, v_ssm_dt_bias, v_ssm_a_log, v_ssm_d, v_ssm_norm_g, v_w_out, v_g_ffn, v_w_ffn_in, v_w_ffn_out, v_g_final):
    L = x.shape[1]
    px, py, pc = _place()
    me = 4 * px + 2 * py + pc
    W = dict(c_ctx=c_ctx, w_mod=w_mod, b_mod=b_mod, g_mix=g_mix, w_in=w_in, wa_sink=wa_sink, na_rpb=na_rpb,
             ssm_conv_w=ssm_conv_w, ssm_conv_b=ssm_conv_b, ssm_dt_bias=ssm_dt_bias, ssm_a_log=ssm_a_log, ssm_d=ssm_d,
             ssm_norm_g=ssm_norm_g, w_out=w_out, g_ffn=g_ffn, w_ffn_in=w_ffn_in, w_ffn_out=w_ffn_out, g_final=g_final)
    M = dict(c_ctx=m_c_ctx, w_mod=m_w_mod, b_mod=m_b_mod, g_mix=m_g_mix, w_in=m_w_in, wa_sink=m_wa_sink,
             na_rpb=m_na_rpb, ssm_conv_w=m_ssm_conv_w, ssm_conv_b=m_ssm_conv_b, ssm_dt_bias=m_ssm_dt_bias,
             ssm_a_log=m_ssm_a_log, ssm_d=m_ssm_d, ssm_norm_g=m_ssm_norm_g, w_out=m_w_out, g_ffn=m_g_ffn,
             w_ffn_in=m_w_ffn_in, w_ffn_out=m_w_ffn_out, g_final=m_g_final)
    V = dict(c_ctx=v_c_ctx, w_mod=v_w_mod, b_mod=v_b_mod, g_mix=v_g_mix, w_in=v_w_in, wa_sink=v_wa_sink,
             na_rpb=v_na_rpb, ssm_conv_w=v_ssm_conv_w, ssm_conv_b=v_ssm_conv_b, ssm_dt_bias=v_ssm_dt_bias,
             ssm_a_log=v_ssm_a_log, ssm_d=v_ssm_d, ssm_norm_g=v_ssm_norm_g, w_out=v_w_out, g_ffn=v_g_ffn,
             w_ffn_in=v_w_ffn_in, w_ffn_out=v_w_ffn_out, g_final=v_g_final)

    c_all, conv_all = all_gather([c, ssm_conv_w], "gather_small")
    tr = lambda a: a.transpose(0, 2, 1)
    shards = dict(w_in=tr(w_in).astype(MXU), w_out=w_out.astype(MXU), w_ffn_in=tr(w_ffn_in).astype(MXU),
                  w_ffn_out=w_ffn_out.astype(MXU))
    conv_f = conv_all.transpose(1, 2, 0, 3).reshape(2, 7, 1024)

    scin = jnp.concatenate([c_all.reshape(NDEV, D), c_ctx.reshape(1, D), jnp.zeros((7, D), F32)], axis=0)
    bcol = lax.dynamic_slice_in_dim(b_mod, me * 768, 768, axis=1).reshape(2, 1, 768)
    mod_all, = all_gather([mod_fwd(scin, w_mod, bcol)], "gather_mod")
    mod_rows = mod_all.transpose(1, 2, 0, 3).reshape(2, 16, 6 * D)
    mods = jnp.stack([lax.dynamic_index_in_dim(mod_rows, me, axis=1, keepdims=False), mod_rows[:, 8]], axis=1)

    layers = [dict(g_mix=g_mix[i], wa_sink=wa_sink[i], na_rpb=na_rpb[i], ssm_conv_w=conv_f[i],
                   ssm_conv_b=ssm_conv_b[i], ssm_dt_bias=ssm_dt_bias[i], ssm_a_log=ssm_a_log[i], ssm_d=ssm_d[i],
                   ssm_norm_g=ssm_norm_g[i], g_ffn=g_ffn[i]) for i in range(2)]
    loss, dx, grads, dmods, dgfin, gin0 = local_step(x[0], ctx[0], loss_target[0], mods, layers, shards, g_final, L)
    loss = lax.psum(loss, ("x", "y", "c"))

    stk = lambda n: jnp.stack([grads[0][n], grads[1][n]])
    small = dict(b_mod=dmods[:, 0] + dmods[:, 1], g_final=dgfin, dmod_s=dmods[:, 0], dmod_c=dmods[:, 1])
    for nme in _SMALL:
        if nme not in small:
            small[nme] = stk(nme)
    shapes = [small[nme].shape for nme in _SMALL]
    zero_like = lambda nme: jnp.zeros(small[nme].shape, F32)
    own = lambda S, nme: S[nme] if (nme in S and S[nme].shape == small[nme].shape) else zero_like(nme)
    gath, = all_gather([_pack([small[nme] for nme in _SMALL])], "gather_grads")
    sm = adam_reduce(gath, _pack([own(W, nme) for nme in _SMALL]), _pack([own(M, nme) for nme in _SMALL]),
                     _pack([own(V, nme) for nme in _SMALL]), "adam_small")
    res = {nme: vals for nme, vals in zip(_SMALL, zip(*[_unpack(a, shapes) for a in sm]))}

    cols = lambda a: lax.dynamic_slice_in_dim(a, me * 768, 768, axis=-1)
    gparts = [_unpack(gath[d], shapes) for d in range(NDEV)]
    dmod_s_all = jnp.stack([gparts[d][_SMALL.index("dmod_s")] for d in range(NDEV)], axis=1)
    G = jnp.concatenate([cols(dmod_s_all), cols(res["dmod_c"][0])[:, None, :], jnp.zeros((2, 7, 768), F32)], axis=1)
    dwmod, dscin = mod_bwd(scin, w_mod, G)
    cc_g, = all_gather([dscin[8].reshape(8, 128)], "gather_cctx")
    out = {}
    out["c_ctx"] = [a.reshape(D) for a in adam_reduce(cc_g, c_ctx.reshape(8, 128), m_c_ctx.reshape(8, 128),
                                                      v_c_ctx.reshape(8, 128), "adam_cctx")]
    res_wmod, (got1,) = adam_reduce(dwmod.reshape(1, 2 * D, 768), w_mod.reshape(2 * D, 768),
                                    m_w_mod.reshape(2 * D, 768), v_w_mod.reshape(2 * D, 768), "adam_wmod",
                                    sends=(gin0[1],))
    out["w_mod"] = [a.reshape(2, D, 768) for a in res_wmod]
    gconv = lax.dynamic_slice_in_dim(res["ssm_conv_w"][0], me * 128, 128, axis=2)
    out["ssm_conv_w"] = [a.reshape(2, 7, 128) for a in adam_reduce(
        gconv.reshape(1, 14, 128), ssm_conv_w.reshape(14, 128), m_ssm_conv_w.reshape(14, 128),
        v_ssm_conv_w.reshape(14, 128), "adam_conv")]
    for nme in _SMALL:
        if nme not in ("ssm_conv_w", "dmod_s", "dmod_c"):
            out[nme] = list(res[nme])

    adam_big = lambda nme, t, **kw: adam_layers(grads[0][nme], grads[1][nme], t(W[nme]), t(M[nme]), t(V[nme]),
                                                "adam_" + nme, **kw)
    same = lambda a: a
    res_fi, (got0,) = adam_big("w_ffn_in", tr, sends=(gin0[0],))
    grads[0]["w_in"] = jnp.concatenate([got0, got1], axis=2)
    out["w_ffn_in"] = [tr(a) for a in res_fi]
    out["w_ffn_out"] = list(adam_big("w_ffn_out", same))
    out["w_out"] = list(adam_big("w_out", same))
    out["w_in"] = [tr(a) for a in adam_big("w_in", tr)]
    order = ["c_ctx", "w_mod", "b_mod", "g_mix", "w_in", "wa_sink", "na_rpb", "ssm_conv_w", "ssm_conv_b",
             "ssm_dt_bias", "ssm_a_log", "ssm_d", "ssm_norm_g", "w_out", "g_ffn", "w_ffn_in", "w_ffn_out", "g_final"]
    return (loss, dx.reshape(1, L, D), *[out[nme][0] for nme in order], *[out[nme][1] for nme in order],
            *[out[nme][2] for nme in order], *[out[nme][3] for nme in order])
```

```python
import functools
import math

import numpy as np
import jax
import jax.numpy as jnp
from jax import lax
from jax.experimental import pallas as pl
from jax.experimental.pallas import tpu as pltpu

F32 = jnp.float32
MXU = jnp.bfloat16
_INTERPRET = False
VMEM_LIMIT = 60 * 1024 * 1024

D = 1024
LC = 256
GW = 64
HD = 64
EPS = 1e-6
NEG = -1e30
NDEV = 8
Q = 128
NSTATE = 128
DFF = 2816
IN_COLS = 2832
NP_IN = 3072
C_QA, C_QB, C_Z, C_KA, C_VA, C_KB, C_VB, C_XBC, C_DT = 0, 256, 512, 1024, 1152, 1280, 1536, 1792, 2816
ADAM_LR, ADAM_B1, ADAM_B2, ADAM_EPS, ADAM_WD, ADAM_STEP = 0.001, 0.9, 0.999, 1e-08, 0.01, 10
MESH_T = pl.DeviceIdType.MESH


def _dg(a, b, ca, cb):
    return lax.dot_general(a.astype(MXU), b.astype(MXU), (((ca,), (cb,)), ((), ())), preferred_element_type=F32)


@jax.custom_vjp
def mm(a, b):
    return _dg(a, b, 1, 0)


def _mm_f(a, b):
    return _dg(a, b, 1, 0), (a, b)


def _mm_b(res, g):
    a, b = res
    return _dg(g, b, 1, 1).astype(a.dtype), _dg(a, g, 0, 0).astype(b.dtype)


mm.defvjp(_mm_f, _mm_b)


@jax.custom_vjp
def mm_nt(a, b):
    return _dg(a, b, 1, 1)


def _mmnt_f(a, b):
    return _dg(a, b, 1, 1), (a, b)


def _mmnt_b(res, g):
    a, b = res
    return _dg(g, b, 1, 0).astype(a.dtype), _dg(g, a, 0, 0).astype(b.dtype)


mm_nt.defvjp(_mmnt_f, _mmnt_b)


@jax.custom_vjp
def mm_tn(a, b):
    return _dg(a, b, 0, 0)


def _mmtn_f(a, b):
    return _dg(a, b, 0, 0), (a, b)


def _mmtn_b(res, g):
    a, b = res
    return _dg(b, g, 1, 1).astype(a.dtype), _dg(a, g, 1, 0).astype(b.dtype)


mm_tn.defvjp(_mmtn_f, _mmtn_b)


@jax.custom_vjp
def mmw(a, w):
    return _dg(a, w, 1, 0)


mmw.defvjp(lambda a, w: (_dg(a, w, 1, 0), w), lambda w, g: (_dg(g, w, 1, 1), None))


@jax.custom_vjp
def mmw_nt(a, w):
    return _dg(a, w, 1, 1)


mmw_nt.defvjp(lambda a, w: (_dg(a, w, 1, 1), w), lambda w, g: (_dg(g, w, 1, 0), None))


def _exact(a, b):
    return lax.dot_general(a, b, (((1,), (0,)), ((), ())), precision=lax.Precision.HIGHEST,
                           preferred_element_type=F32)


def _pc(body, name, out_shape, grid=None, in_specs=None, out_specs=None, scratch=(), sends=(), gather=False):
    params = pltpu.CompilerParams(vmem_limit_bytes=VMEM_LIMIT)
    if sends and not isinstance(out_shape, (list, tuple)):
        out_shape, out_specs = [out_shape], [out_specs]
    start, wait = (_ag_start, _ag_wait) if gather else (_a2a_start, _a2a_wait)
    if not sends:
        kw = {}
        if grid is not None:
            kw = dict(grid=grid, in_specs=in_specs, out_specs=out_specs)
        elif in_specs is not None:
            kw = dict(in_specs=in_specs, out_specs=out_specs)
        return pl.pallas_call(body, name=name, out_shape=out_shape, scratch_shapes=list(scratch),
                              compiler_params=params, interpret=_INTERPRET, **kw)
    n, nin, nout, nscr = len(sends), len(in_specs), len(out_shape), len(scratch)

    def body2(*refs):
        cin, xs = refs[:nin], refs[nin:nin + n]
        couts, os_ = refs[nin + n:nin + n + nout], refs[nin + n + nout:nin + 2 * n + nout]
        cscr, sems = refs[nin + 2 * n + nout:nin + 2 * n + nout + nscr], refs[nin + 2 * n + nout + nscr:]
        ids = [pl.program_id(a) for a in range(len(grid))]
        first = functools.reduce(lambda a, b: a & b, [i == 0 for i in ids])
        last = functools.reduce(lambda a, b: a & b, [i == g - 1 for i, g in zip(ids, grid)])

        @pl.when(first)
        def _():
            start(xs, os_, *sems)

        body(*cin, *couts, *cscr)

        @pl.when(last)
        def _():
            wait(xs, os_, *sems)

    call = pl.pallas_call(
        body2, name=name,
        out_shape=list(out_shape) + [_sds(((NDEV,) if gather else ()) + a.shape, a.dtype) for a in sends],
        grid=grid, in_specs=list(in_specs) + [_any()] * n, out_specs=list(out_specs) + [_any()] * n,
        scratch_shapes=list(scratch) + _a2a_sems(n), compiler_params=params, interpret=_INTERPRET)

    def run(*args):
        res = call(*args, *sends)
        return res[:nout], res[nout:]

    return run


def _vm():
    return pl.BlockSpec(memory_space=pltpu.VMEM)


def _sds(shape, dt=F32):
    return jax.ShapeDtypeStruct(shape, dt)


def _iota(shape, dim):
    return lax.broadcasted_iota(jnp.int32, shape, dim)


def _silu(x):
    return x * jax.nn.sigmoid(x)


def _softplus(x):
    return jnp.maximum(x, 0.0) + jnp.log1p(jnp.exp(-jnp.abs(x)))


def _normmod(x, g, sh, sc):
    r = lax.rsqrt(jnp.mean(x * x, axis=-1, keepdims=True) + EPS)
    return (x * r * g) * (1.0 + sc) + sh


def _rope(x, cos, sin, rm):
    return x * cos + _exact(x, rm) * sin


def _swap12(x):
    lane = _iota(x.shape, 1)
    up, down = pltpu.roll(x, 192, 1), pltpu.roll(x, 64, 1)
    return jnp.where((lane >= 64) & (lane < 128), up, jnp.where((lane >= 128) & (lane < 192), down, x))


def _acc_init(first, refs):
    @pl.when(first)
    def _():
        for r in refs:
            r[...] = jnp.zeros_like(r)


def _stream(X, TR, nlt):
    if not isinstance(X, tuple):
        return (X,), [pl.BlockSpec((TR, D), lambda i: (i, 0))], lambda refs: refs[0][...]
    specs = [pl.BlockSpec((TR, D), lambda i: (jnp.minimum(i, nlt - 1), 0)), pl.BlockSpec((TR, D), lambda i: (0, 0))]
    return X, specs, lambda refs: jnp.where(pl.program_id(0) < nlt, refs[0][...], refs[1][...])


def in_fwd(X, g, sh, sc, W, cos, sin, rm, L, sends=()):
    T = L + LC
    TR = 256
    nlt = L // TR
    xs, xspecs, xread = _stream(X, TR, nlt)

    def body(*refs):
        (g_ref, sh_ref, sc_ref, w_ref, cos_ref, sin_ref, rm_ref,
         qa, qb, z, ka, va, kb, vb, xbc, dt, hout) = refs[len(xs):]
        h = _normmod(xread(refs), g_ref[...], sh_ref[0], sc_ref[0]).astype(MXU)
        hout[...] = h
        y = lax.dot_general(h, w_ref[...], (((1,), (1,)), ((), ())), preferred_element_type=F32)
        cs, sn, r = cos_ref[...], sin_ref[...], rm_ref[...]
        qa[...] = _rope(_swap12(y[:, C_QA:C_QB]), cs, sn, r).astype(MXU)
        qb[...] = y[:, C_QB:C_Z].astype(MXU)
        z[...] = y[:, C_Z:C_KA]
        ka[...] = _rope(y[:, C_KA:C_VA], cs[:, :128], sn[:, :128], r[:128, :128]).astype(MXU)
        va[...] = y[:, C_VA:C_KB].astype(MXU)
        kb[...] = y[:, C_KB:C_VB].astype(MXU)
        vb[...] = y[:, C_VB:C_XBC].astype(MXU)
        xbc[...] = y[:, C_XBC:C_DT]
        dt[...] = y[:, C_DT:C_DT + 128]

    row = lambda w: pl.BlockSpec((TR, w), lambda i: (i, 0))
    cls = pl.BlockSpec((1, 1, D), lambda i: (i // nlt, 0, 0))
    widths = [(256, MXU), (256, MXU), (512, F32), (128, MXU), (128, MXU), (256, MXU), (256, MXU), (1024, F32),
              (128, F32), (D, MXU)]
    return _pc(body, "in_fwd", [_sds((T, w), d) for w, d in widths], grid=(T // TR,),
               in_specs=xspecs + [pl.BlockSpec((1, D), lambda i: (0, 0)), cls, cls, _vm(), row(256), row(256), _vm()],
               out_specs=[row(w) for w, _ in widths], sends=sends, gather=True)(*xs, g, sh, sc, W, cos, sin, rm)


def in_bwd(X, g, sh, sc, W, cos, sin, rm, dxres, dqa, dqb, dz, dka, dva, dkb, dvb, dxbc, ddt2, L, latent_only):
    T = L + LC
    TR = 256
    nlt = L // TR
    xs, xspecs, xread = _stream(X, TR, nlt)

    def body(*refs):
        (g_ref, sh_ref, sc_ref, w_ref, cos_ref, sin_ref, rm_ref, dxres_ref, dqa_r, dqb_r, dz_r, dka_r,
         dva_r, dkb_r, dvb_r, dxbc_r, ddt0_r, ddt1_r, dx_o, dy_o, dg_o, dsh_o, dsc_o) = refs[len(xs):]
        i = pl.program_id(0)
        cs, sn, r = cos_ref[...], sin_ref[...], rm_ref[...]
        _, vq = jax.vjp(lambda t: _rope(t, cs, sn, r), dqa_r[...])
        _, vk = jax.vjp(lambda t: _rope(t, cs[:, :128], sn[:, :128], r[:128, :128]), dka_r[...])
        dyqa = _swap12(vq(dqa_r[...])[0])
        dyka, = vk(dka_r[...])
        ddt = ddt0_r[0] + ddt1_r[0]
        dy = jnp.concatenate([dyqa, dqb_r[...], dz_r[...], dyka, dva_r[...], dkb_r[...], dvb_r[...], dxbc_r[...],
                              ddt, jnp.zeros((TR, NP_IN - C_DT - 128), F32)], axis=1).astype(MXU)
        dy_o[...] = dy
        dh = jnp.dot(dy, w_ref[...], preferred_element_type=F32)
        _, vp = jax.vjp(_normmod, xread(refs), g_ref[...], sh_ref[0], sc_ref[0])
        dx, dg, dsh, dsc = vp(dh)
        if latent_only:
            @pl.when(i < nlt)
            def _():
                dx_o[...] = dx + dxres_ref[...]
        else:
            dx_o[...] = dx + dxres_ref[...]
        _acc_init(i == 0, [dg_o])
        _acc_init((i == 0) | (i == nlt), [dsh_o, dsc_o])
        dg_o[...] += dg
        dsh_o[0] += dsh
        dsc_o[0] += dsc

    row = lambda w: pl.BlockSpec((TR, w), lambda i: (i, 0))
    cls = pl.BlockSpec((1, 1, D), lambda i: (i // nlt, 0, 0))
    vec = pl.BlockSpec((1, D), lambda i: (0, 0))
    dts = lambda d: pl.BlockSpec((1, TR, 128), lambda i: (d, i, 0))
    dxs = pl.BlockSpec((TR, D), lambda i: (jnp.minimum(i, nlt - 1), 0)) if latent_only else row(D)
    return _pc(body, "in_bwd",
               [_sds((L if latent_only else T, D)), _sds((T, NP_IN), MXU), _sds((1, D)), _sds((2, 1, D)),
                _sds((2, 1, D))],
               grid=(T // TR,),
               in_specs=xspecs + [vec, cls, cls, _vm(), row(256), row(256), _vm(), row(D), row(256), row(256),
                                  row(512), row(128), row(128), row(256), row(256), row(1024), dts(0), dts(1)],
               out_specs=[dxs, row(NP_IN), vec, cls, cls])(
        *xs, g, sh, sc, W, cos, sin, rm, dxres, dqa, dqb, dz, dka, dva, dkb, dvb, dxbc, ddt2, ddt2)


def tn_mm(A, G, bk, bn, out_dtype, ncol=None, col0=0):
    T, K = A.shape
    N = G.shape[1] if ncol is None else ncol
    first = col0 * (N // bn)
    bt = T
    nt = T // bt

    def body(a_ref, g_ref, o_ref, acc):
        t = pl.program_id(2)
        _acc_init(t == 0, [acc])
        acc[...] += lax.dot_general(a_ref[...], g_ref[...], (((0,), (0,)), ((), ())), preferred_element_type=F32)

        @pl.when(t == nt - 1)
        def _():
            o_ref[...] = acc[...].astype(out_dtype)

    return _pc(body, "tn_mm", _sds((K, N), out_dtype), grid=(K // bk, N // bn, nt),
               in_specs=[pl.BlockSpec((bt, bk), lambda k, n, t: (t, k)),
                         pl.BlockSpec((bt, bn), lambda k, n, t: (t, first + n))],
               out_specs=pl.BlockSpec((bk, bn), lambda k, n, t: (k, n)),
               scratch=[pltpu.VMEM((bk, bn), F32)])(A, G)


def _ssm_out(yf, yb, xs, z, dsk, gs):
    y = (yf + yb + dsk * xs) * _silu(z)
    r = lax.rsqrt(jnp.mean(y * y, axis=-1, keepdims=True) + EPS)
    return y * r * gs


def out_fwd(oa, ob, y2, act, z, dsk, gs, W, X, gate, L, sends=()):
    T = L + LC
    TR = 256
    nlt = L // TR
    xs, xspecs, xread = _stream(X, TR, nlt)

    def body(*refs):
        oa_r, ob_r, yf_r, yb_r, xs_r, z_r, dsk_r, gs_r, w_ref, gt_ref, x1_o, cat_o = refs[len(xs):]
        oc = _ssm_out(yf_r[0], yb_r[0], xs_r[...], z_r[...], dsk_r[...], gs_r[...])
        cat = jnp.concatenate([_swap12(oa_r[...]), ob_r[...], oc], axis=1).astype(MXU)
        cat_o[...] = cat
        x1_o[...] = xread(refs) + gt_ref[0] * jnp.dot(cat, w_ref[...], preferred_element_type=F32)

    row = lambda w: pl.BlockSpec((TR, w), lambda i: (i, 0))
    ys = lambda d: pl.BlockSpec((1, TR, 512), lambda i: (d, i, 0))
    cls = pl.BlockSpec((1, 1, D), lambda i: (i // nlt, 0, 0))
    v512 = pl.BlockSpec((1, 512), lambda i: (0, 0))
    return _pc(body, "out_fwd", [_sds((T, D)), _sds((T, D), MXU)], grid=(T // TR,),
               in_specs=xspecs + [row(256), row(256), ys(0), ys(1), row(512), row(512), v512, v512, _vm(), cls],
               out_specs=[row(D), row(D)], sends=sends, gather=True)(*xs, oa, ob, y2, y2, act, z, dsk, gs, W, gate)


def out_bwd(oa, ob, y2, act, z, dsk, gs, W, gate, dX1, L):
    T = dX1.shape[0]
    TR = 256
    nlt = L // TR

    def body(oa_r, ob_r, yf_r, yb_r, xs_r, z_r, dsk_r, gs_r, w_ref, gt_ref, dx1_r,
             doa_o, dob_o, dy_o, dxs_o, dz_o, dmix_o, ddsk_o, dgs_o, dgt_o):
        i = pl.program_id(0)
        w = w_ref[...]

        def f(oa_, ob_, yf, yb, xs, z_, dsk_, gs_, gt):
            oc = _ssm_out(yf, yb, xs, z_, dsk_, gs_)
            return gt * mmw(jnp.concatenate([oa_, ob_, oc], axis=1), w)

        _, vjp = jax.vjp(f, _swap12(oa_r[...]), ob_r[...], yf_r[0], yb_r[0], xs_r[...], z_r[...], dsk_r[...],
                         gs_r[...], gt_ref[0])
        dx1 = dx1_r[...]
        doa, dob, dyf, _, dxs, dz, ddsk, dgs, dgt = vjp(dx1)
        doa_o[...] = _swap12(doa)
        dob_o[...] = dob
        dy_o[...] = dyf
        dxs_o[...] = dxs
        dz_o[...] = dz
        dmix_o[...] = (gt_ref[0] * dx1).astype(MXU)
        _acc_init(i == 0, [ddsk_o, dgs_o])
        _acc_init((i == 0) | (i == nlt), [dgt_o])
        ddsk_o[...] += ddsk
        dgs_o[...] += dgs
        dgt_o[0] += dgt

    row = lambda w: pl.BlockSpec((TR, w), lambda i: (i, 0))
    ys = lambda d: pl.BlockSpec((1, TR, 512), lambda i: (d, i, 0))
    cls = pl.BlockSpec((1, 1, D), lambda i: (i // nlt, 0, 0))
    v512 = pl.BlockSpec((1, 512), lambda i: (0, 0))
    return _pc(body, "out_bwd",
               [_sds((T, 256)), _sds((T, 256)), _sds((T, 512)), _sds((T, 512)), _sds((T, 512)), _sds((T, D), MXU),
                _sds((1, 512)), _sds((1, 512)), _sds((2, 1, D))],
               grid=(T // TR,),
               in_specs=[row(256), row(256), ys(0), ys(1), row(512), row(512), v512, v512, _vm(), cls, row(D)],
               out_specs=[row(256), row(256), row(512), row(512), row(512), row(D), v512, v512, cls])(
        oa, ob, y2, y2, act, z, dsk, gs, W, gate, dX1)


def ffn_fwd(X, g, sh, sc, gate, Win, Wout, L, sends=()):
    T = X.shape[0]
    TR = 256
    nlt = L // TR

    def body(x_ref, g_ref, sh_ref, sc_ref, gt_ref, wi_ref, wo_ref, o_ref, f_ref):
        h = _normmod(x_ref[...], g_ref[...], sh_ref[0], sc_ref[0]).astype(MXU)
        nt = (((1,), (1,)), ((), ()))
        a = lax.dot_general(h, wi_ref[0:DFF, :], nt, preferred_element_type=F32)
        u = lax.dot_general(h, wi_ref[DFF:2 * DFF, :], nt, preferred_element_type=F32)
        act = (_silu(a) * u).astype(MXU)
        ff = jnp.dot(act, wo_ref[...], preferred_element_type=F32)
        f_ref[...] = ff
        o_ref[...] = x_ref[...] + gt_ref[0] * ff

    row = lambda w: pl.BlockSpec((TR, w), lambda i: (i, 0))
    cls = pl.BlockSpec((1, 1, D), lambda i: (i // nlt, 0, 0))
    vec = pl.BlockSpec((1, D), lambda i: (0, 0))
    return _pc(body, "ffn_fwd", [_sds((T, D)), _sds((T, D))], grid=(T // TR,),
               in_specs=[row(D), vec, cls, cls, cls, _vm(), _vm()], out_specs=[row(D), row(D)], sends=sends,
               gather=True)(X, g, sh, sc, gate, Win, Wout)


def ffn_bwd(X, g, sh, sc, gate, Win, Wout, FF, dX2, L, sends=(), nchunk=2):
    T = X.shape[0]
    TR = 256
    nlt = L // TR
    CH = DFF // nchunk

    def body(x_ref, g_ref, sh_ref, sc_ref, gt_ref, wi_ref, wo_ref, ff_r, dx2_r,
             dx_o, h_o, du_o, act_o, dout_o, dg_o, dsh_o, dsc_o, dgt_o):
        i = pl.program_id(0)
        h, vp = jax.vjp(_normmod, x_ref[...], g_ref[...], sh_ref[0], sc_ref[0])
        dx2 = dx2_r[...]
        dout = gt_ref[0] * dx2
        zero = jnp.zeros((TR, CH), F32)
        dh = jnp.zeros((TR, D), F32)
        for c in range(nchunk):
            lo, hi = c * CH, (c + 1) * CH
            wg, wu, wo = wi_ref[lo:hi, :], wi_ref[DFF + lo:DFF + hi, :], wo_ref[lo:hi, :]

            def f(h_, eg, eu):
                act = _silu(mmw_nt(h_, wg) + eg) * (mmw_nt(h_, wu) + eu)
                return mmw(act, wo), act

            _, vjp_c, act = jax.vjp(f, h, zero, zero, has_aux=True)
            dh_c, da, du = vjp_c(dout)
            dh = dh + dh_c
            du_o[:, lo:hi] = da.astype(MXU)
            du_o[:, DFF + lo:DFF + hi] = du.astype(MXU)
            act_o[:, lo:hi] = act.astype(MXU)
        dx, dg, dsh, dsc = vp(dh)
        dx_o[...] = dx + dx2
        h_o[...] = h.astype(MXU)
        dout_o[...] = dout.astype(MXU)
        _acc_init(i == 0, [dg_o])
        _acc_init((i == 0) | (i == nlt), [dsh_o, dsc_o, dgt_o])
        dg_o[...] += dg
        dsh_o[0] += dsh
        dsc_o[0] += dsc
        dgt_o[0] += jnp.sum(dx2 * ff_r[...], axis=0, keepdims=True)

    row = lambda w: pl.BlockSpec((TR, w), lambda i: (i, 0))
    cls = pl.BlockSpec((1, 1, D), lambda i: (i // nlt, 0, 0))
    vec = pl.BlockSpec((1, D), lambda i: (0, 0))
    return _pc(body, "ffn_bwd",
               [_sds((T, D)), _sds((T, D), MXU), _sds((T, 2 * DFF), MXU), _sds((T, DFF), MXU), _sds((T, D), MXU),
                _sds((1, D)), _sds((2, 1, D)), _sds((2, 1, D)), _sds((2, 1, D))],
               grid=(T // TR,),
               in_specs=[row(D), vec, cls, cls, cls, _vm(), _vm(), row(D), row(D)],
               out_specs=[row(D), row(D), row(2 * DFF), row(DFF), row(D), vec, cls, cls, cls], sends=sends)(
        X, g, sh, sc, gate, Win, Wout, FF, dX2)


def loss_head(X2, g, tgt, L):
    T = X2.shape[0]
    TR = 256
    nlt = L // TR

    def body(x_ref, g_ref, t_ref, loss_o, dx_o, dg_o):
        i = pl.program_id(0)
        _acc_init(i == 0, [loss_o, dg_o])

        @pl.when(i < nlt)
        def _():
            def f(x, g_):
                y = x * lax.rsqrt(jnp.mean(x * x, axis=-1, keepdims=True) + EPS) * g_
                return 0.5 * jnp.sum(jnp.mean(jnp.square(y - t_ref[...]), axis=-1, keepdims=True), axis=0,
                                     keepdims=True)

            val, vjp = jax.vjp(f, x_ref[...], g_ref[...])
            dx, dg = vjp(jnp.ones((1, 1), F32))
            dx_o[...] = dx
            loss_o[...] += jnp.broadcast_to(val, (8, 128))
            dg_o[...] += dg

        @pl.when(i >= nlt)
        def _():
            dx_o[...] = jnp.zeros_like(dx_o)

    row = pl.BlockSpec((TR, D), lambda i: (i, 0))
    vec = pl.BlockSpec((1, D), lambda i: (0, 0))
    return _pc(body, "loss_head", [_sds((8, 128)), _sds((T, D)), _sds((1, D))], grid=(T // TR,),
               in_specs=[row, vec, pl.BlockSpec((TR, D), lambda i: (jnp.minimum(i, nlt - 1), 0))],
               out_specs=[pl.BlockSpec((8, 128), lambda i: (0, 0)), row, vec])(X2, g, tgt)


def _stack_impl(q):
    lane = _iota(q.shape, 1)
    return jnp.concatenate([jnp.where(lane < HD, q, 0.0), jnp.where(lane >= HD, q, 0.0)], axis=0)


def _unstack_impl(o):
    M = o.shape[0] // 2
    return jnp.where(_iota((M, o.shape[1]), 1) < HD, o[:M], o[M:])


@jax.custom_vjp
def _stack(q):
    return _stack_impl(q)


_stack.defvjp(lambda q: (_stack_impl(q), None), lambda _, g: (_unstack_impl(g),))


@jax.custom_vjp
def _unstack(o):
    return _unstack_impl(o)


_unstack.defvjp(lambda o: (_unstack_impl(o), None), lambda _, g: (_stack_impl(g),))


def _softmax_av(q, ks, vs, biases, sink):
    q2 = _stack(q)
    ss = []
    for k, b in zip(ks, biases):
        s = mm_nt(q2, k) * (HD ** -0.5)
        ss.append(s if b is None else s + b)
    m = functools.reduce(jnp.maximum, [jnp.max(s, axis=1, keepdims=True) for s in ss])
    if sink is not None:
        m = jnp.maximum(m, sink)
    m = lax.stop_gradient(m)
    es = [jnp.exp(s - m) for s in ss]
    den = functools.reduce(lambda a, b_: a + b_, [jnp.sum(e, axis=1, keepdims=True) for e in es])
    if sink is not None:
        den = den + jnp.exp(sink - m)
    inv = 1.0 / den
    return _unstack(functools.reduce(lambda a, b_: a + b_, [mm(e * inv, v) for e, v in zip(es, vs)]))


def _sink_col(s0, s1, M):
    return jnp.concatenate([jnp.broadcast_to(jnp.mean(s0, axis=1, keepdims=True), (M, 1)),
                            jnp.broadcast_to(jnp.mean(s1, axis=1, keepdims=True), (M, 1))], axis=0)


def _stack4_impl(q):
    lane = _iota((q.shape[0], 128), 1)
    parts = []
    for p in range(2):
        qp = q[:, 128 * p:128 * (p + 1)]
        parts += [jnp.where(lane < HD, qp, 0.0), jnp.where(lane >= HD, qp, 0.0)]
    return jnp.concatenate(parts, axis=0)


def _unstack4_impl(o):
    M = o.shape[0] // 4
    lane = _iota((M, 128), 1)
    return jnp.concatenate([jnp.where(lane < HD, o[0:M], o[M:2 * M]),
                            jnp.where(lane < HD, o[2 * M:3 * M], o[3 * M:4 * M])], axis=1)


@jax.custom_vjp
def _stack4(q):
    return _stack4_impl(q)


_stack4.defvjp(lambda q: (_stack4_impl(q), None), lambda _, g: (_unstack4_impl(g),))


@jax.custom_vjp
def _unstack4(o):
    return _unstack4_impl(o)


_unstack4.defvjp(lambda o: (_unstack4_impl(o), None), lambda _, g: (_stack4_impl(g),))


WA_NB = 4


def _wa_blocks(qs, kws, vws, kx, vx, sks, n0, L):
    sc = HD ** -0.5
    sink = jnp.concatenate([jnp.broadcast_to(jnp.mean(s_, axis=1, keepdims=True), (Q, 1)) for s_ in sks], axis=0)
    bias = []
    for b_ in range(len(qs)):
        n = n0 + b_
        qpos = n * Q + (_iota((4 * Q, 3 * Q), 0) & (Q - 1))
        kpos = (n - 1) * Q + _iota((4 * Q, 3 * Q), 1)
        bias.append(jnp.where((jnp.abs(qpos - kpos) <= Q) & (kpos >= 0) & (kpos < L), 0.0, NEG))
    q4 = [_stack4(q) for q in qs]
    sl = [mm_nt(a, k) * sc + b_ for a, k, b_ in zip(q4, kws, bias)]
    sx = [mm_nt(a, kx) * sc for a in q4]
    m = [lax.stop_gradient(jnp.maximum(jnp.maximum(jnp.max(a, axis=1, keepdims=True),
                                                   jnp.max(b_, axis=1, keepdims=True)), sink))
         for a, b_ in zip(sl, sx)]
    el = [jnp.exp(a - c) for a, c in zip(sl, m)]
    ex = [jnp.exp(a - c) for a, c in zip(sx, m)]
    inv = [1.0 / (jnp.sum(a, axis=1, keepdims=True) + jnp.sum(b_, axis=1, keepdims=True) + jnp.exp(sink - c))
           for a, b_, c in zip(el, ex, m)]
    return [_unstack4(mm(a * i, v) + mm(b_ * i, vx)) for a, b_, i, v in zip(el, ex, inv, vws)]


def _wa_load(q_r, k_r, v_r, n0):
    f = lambda t: t.astype(F32)
    qs = [f(q_r[b_ * Q:(b_ + 1) * Q, :]) for b_ in range(WA_NB)]
    wins = [pl.ds(pl.multiple_of((n0 + b_) * Q, Q), 3 * Q) for b_ in range(WA_NB)]
    return qs, [f(k_r[w, :]) for w in wins], [f(v_r[w, :]) for w in wins], wins


def _wa_specs(L):
    nb = L // Q
    qs = pl.BlockSpec((WA_NB * Q, 256), lambda n: (n, 0))
    kfull = pl.BlockSpec((L + LC + Q, 128), lambda n: (0, 0))
    sks = pl.BlockSpec((2, 2, 1, 128), lambda n: (0, 0, 0, 0))
    return nb, qs, kfull, sks


def wa_fwd(QA, KA, VA, sinkp, L, sends=()):
    nb, qs, kfull, sks = _wa_specs(L)
    pad = lambda a: jnp.concatenate([jnp.zeros((Q, 128), a.dtype), a], axis=0)

    def body(q_r, k_r, v_r, sk_r, o_ref):
        n0 = pl.program_id(0) * WA_NB
        qs_, kws, vws, _ = _wa_load(q_r, k_r, v_r, n0)
        cx = pl.ds(Q + L, LC)
        outs = _wa_blocks(qs_, kws, vws, k_r[cx, :].astype(F32), v_r[cx, :].astype(F32),
                          [sk_r[0, 0], sk_r[0, 1], sk_r[1, 0], sk_r[1, 1]], n0, L)
        o_ref[...] = jnp.concatenate(outs, axis=0)

    return _pc(body, "wa_fwd", _sds((L, 256)), grid=(nb // WA_NB,), in_specs=[qs, kfull, kfull, sks], out_specs=qs,
               sends=sends, gather=True)(QA, pad(KA), pad(VA), sinkp)


def wa_bwd(QA, KA, VA, sinkp, dO, L, sends=()):
    nb, qs, kfull, sks = _wa_specs(L)
    pad = lambda a: jnp.concatenate([jnp.zeros((Q, 128), a.dtype), a], axis=0)

    def body(q_r, k_r, v_r, sk_r, do_r, dq_o, dk_o, dv_o, dsk_o):
        n0 = pl.program_id(0) * WA_NB
        _acc_init(n0 == 0, [dk_o, dv_o, dsk_o])
        qs_, kws, vws, wins = _wa_load(q_r, k_r, v_r, n0)
        cx = pl.ds(Q + L, LC)
        fn = lambda a, b, c, d, e, s_: _wa_blocks(a, b, c, d, e, s_, n0, L)
        _, vjp = jax.vjp(fn, qs_, kws, vws, k_r[cx, :].astype(F32), v_r[cx, :].astype(F32),
                         [sk_r[0, 0], sk_r[0, 1], sk_r[1, 0], sk_r[1, 1]])
        dqs, dkws, dvws, dkx, dvx, ds = vjp([do_r[b_ * Q:(b_ + 1) * Q, :] for b_ in range(WA_NB)])
        dq_o[...] = jnp.concatenate(dqs, axis=0)
        for w, dk, dv in zip(wins, dkws, dvws):
            dk_o[w, :] += dk
            dv_o[w, :] += dv
        dk_o[cx, :] += dkx
        dv_o[cx, :] += dvx
        for i_ in range(4):
            dsk_o[i_ // 2, i_ % 2] += ds[i_]

    return _pc(body, "wa_bwd", [_sds((L, 256)), _sds((L + LC + Q, 128)), _sds((L + LC + Q, 128)),
                                _sds((2, 2, 1, 128))],
               grid=(nb // WA_NB,), in_specs=[qs, kfull, kfull, sks, qs], out_specs=[qs, kfull, kfull, sks],
               sends=sends)(QA, pad(KA), pad(VA), sinkp, dO)


def _ctx_block(q, kx, vx, s0, s1):
    return _softmax_av(q, [kx], [vx], [None], _sink_col(s0, s1, LC))


def ctx_fwd(Qx, Kx, Vx, sinkp, shared, L):
    cq = pl.BlockSpec((LC, 128), lambda p: (L // LC, p))
    ck = pl.BlockSpec((LC, 128), lambda p: (L // LC, 0 if shared else p))
    sks = pl.BlockSpec((1, 2, 1, 128), lambda p: (p, 0, 0, 0))

    def body(q_r, k_r, v_r, sk_r, o_ref):
        f = lambda t: t[...].astype(F32)
        o_ref[...] = _ctx_block(f(q_r), f(k_r), f(v_r), sk_r[0, 0], sk_r[0, 1])

    return _pc(body, "ctx_fwd", _sds((LC, 256)), grid=(2,), in_specs=[cq, ck, ck, sks],
               out_specs=pl.BlockSpec((LC, 128), lambda p: (0, p)))(Qx, Kx, Vx, sinkp)


def ctx_bwd(Qx, Kx, Vx, sinkp, dO, shared, L):
    cq = pl.BlockSpec((LC, 128), lambda p: (L // LC, p))
    ck = pl.BlockSpec((LC, 128), lambda p: (L // LC, 0 if shared else p))
    sks = pl.BlockSpec((1, 2, 1, 128), lambda p: (p, 0, 0, 0))
    op = pl.BlockSpec((LC, 128), lambda p: (0, p))
    ok = pl.BlockSpec((LC, 128), lambda p: (0, 0 if shared else p))
    dos = pl.BlockSpec((LC, 128), lambda p: (L // LC, p))

    def body(q_r, k_r, v_r, sk_r, do_r, dq_o, dk_o, dv_o, dsk_o):
        p = pl.program_id(0)
        f = lambda t: t[...].astype(F32)
        _, vjp = jax.vjp(_ctx_block, f(q_r), f(k_r), f(v_r), sk_r[0, 0], sk_r[0, 1])
        dq, dk, dv, ds0, ds1 = vjp(do_r[...])
        dq_o[...] = dq
        _acc_init((p == 0) if shared else (p >= 0), [dk_o, dv_o])
        dk_o[...] += dk
        dv_o[...] += dv
        dsk_o[0, 0] = ds0
        dsk_o[0, 1] = ds1

    kw = 128 if shared else 256
    return _pc(body, "ctx_bwd", [_sds((LC, 256)), _sds((LC, kw)), _sds((LC, kw)), _sds((2, 2, 1, 128))],
               grid=(2,), in_specs=[cq, ck, ck, sks, dos], out_specs=[op, ok, ok, sks])(Qx, Kx, Vx, sinkp, dO)


def _na_rows(qs, kws, vws, kx, vx, bs):
    sc = HD ** -0.5
    q2 = [_stack(q) for q in qs]
    sl = [mm_nt(a, k) * sc + b for a, k, b in zip(q2, kws, bs)]
    sx = [mm_nt(a, kx) * sc for a in q2]
    m = [lax.stop_gradient(jnp.maximum(jnp.max(a, axis=1, keepdims=True), jnp.max(b, axis=1, keepdims=True)))
         for a, b in zip(sl, sx)]
    el = [jnp.exp(a - c) for a, c in zip(sl, m)]
    ex = [jnp.exp(a - c) for a, c in zip(sx, m)]
    inv = [1.0 / (jnp.sum(a, axis=1, keepdims=True) + jnp.sum(b, axis=1, keepdims=True)) for a, b in zip(el, ex)]
    o2 = [mm(a * i, v) + mm(b * i, vx) for a, b, i, v in zip(el, ex, inv, vws)]
    return [_unstack(o) for o in o2]


def _na_geom(rb, j, R):
    r = rb * 8 + j
    s = jnp.clip(r - 4, 0, R - 8)
    cls = jnp.where(r < 4, r, jnp.where(r > R - 4, r - (R - 8), 4))
    return pl.ds(pl.multiple_of(s * GW, GW), 8 * GW), cls


def _na_load(q_r, k_r, v_r, b_r, rb, R):
    geo = [_na_geom(rb, j, R) for j in range(8)]
    qs = [q_r[j * GW:(j + 1) * GW, :].astype(F32) for j in range(8)]
    kws = [k_r[win, :].astype(F32) for win, _ in geo]
    vws = [v_r[win, :].astype(F32) for win, _ in geo]
    bs = [jnp.concatenate([b_r[0, cls], b_r[1, cls]], axis=0) for _, cls in geo]
    return geo, qs, kws, vws, bs


def na_fwd(QB, KB, VB, biasd, L, sends=()):
    R = L // GW
    qs = pl.BlockSpec((8 * GW, 128), lambda p, rb: (rb, p))
    kfull = pl.BlockSpec((L, 128), lambda p, rb: (0, p))
    kctx = pl.BlockSpec((LC, 128), lambda p, rb: (L // LC, p))
    bs = pl.BlockSpec((2, 8, GW, 8 * GW), lambda p, rb: (p, 0, 0, 0))

    def body(q_r, k_r, v_r, kx_r, vx_r, b_r, o_ref):
        _, qs_, kws, vws, bs_ = _na_load(q_r, k_r, v_r, b_r, pl.program_id(1), R)
        outs = _na_rows(qs_, kws, vws, kx_r[...].astype(F32), vx_r[...].astype(F32), bs_)
        o_ref[...] = jnp.concatenate(outs, axis=0)

    return _pc(body, "na_fwd", _sds((L, 256)), grid=(2, R // 8), in_specs=[qs, kfull, kfull, kctx, kctx, bs],
               out_specs=qs, sends=sends, gather=True)(QB, KB, VB, KB, VB, biasd)


def na_bwd(QB, KB, VB, biasd, dO, L):
    R = L // GW
    qs = pl.BlockSpec((8 * GW, 128), lambda p, rb: (rb, p))
    kfull = pl.BlockSpec((L, 128), lambda p, rb: (0, p))
    kctx = pl.BlockSpec((LC, 128), lambda p, rb: (L // LC, p))
    bs = pl.BlockSpec((2, 8, GW, 8 * GW), lambda p, rb: (p, 0, 0, 0))
    oc = pl.BlockSpec((LC, 128), lambda p, rb: (0, p))

    def body(q_r, k_r, v_r, kx_r, vx_r, b_r, do_r, dq_o, dk_o, dv_o, dkx_o, dvx_o, db_o):
        rb = pl.program_id(1)
        _acc_init(rb == 0, [dk_o, dv_o, dkx_o, dvx_o, db_o])
        geo, qs_, kws, vws, bs_ = _na_load(q_r, k_r, v_r, b_r, rb, R)
        _, vjp = jax.vjp(_na_rows, qs_, kws, vws, kx_r[...].astype(F32), vx_r[...].astype(F32), bs_)
        dqs, dkws, dvws, dkx, dvx, dbs = vjp([do_r[j * GW:(j + 1) * GW, :] for j in range(8)])
        dq_o[...] = jnp.concatenate(dqs, axis=0)
        dkx_o[...] += dkx
        dvx_o[...] += dvx
        for j, (win, cls) in enumerate(geo):
            dk_o[win, :] += dkws[j]
            dv_o[win, :] += dvws[j]
            db_o[0, cls] += dbs[j][:GW]
            db_o[1, cls] += dbs[j][GW:]

    return _pc(body, "na_bwd",
               [_sds((L, 256)), _sds((L, 256)), _sds((L, 256)), _sds((LC, 256)), _sds((LC, 256)),
                _sds((4, 8, GW, 8 * GW))],
               grid=(2, R // 8), in_specs=[qs, kfull, kfull, kctx, kctx, bs, qs],
               out_specs=[qs, kfull, kfull, oc, oc, bs])(QB, KB, VB, KB, VB, biasd, dO)


def exact_mm_call(A, B):
    def body(a_ref, b_ref, o_ref):
        o_ref[...] = _exact(a_ref[...], b_ref[...])

    return _pc(body, "exact_mm", _sds((A.shape[0], B.shape[1])))(A, B)


def _conv_shift(x, d, L):
    T = x.shape[0]
    if d == 0:
        return x
    t = _iota(x.shape, 0)
    src = t + d
    ok = (src >= 0) & (src < T) & ((src >= L) == (t >= L))
    return jnp.where(ok, pltpu.roll(x, (-d) % T, 0), 0.0)


def conv_fwd(XBC, w8, b, L):
    T = XBC.shape[0]

    def body(x_ref, w_ref, b_ref, o_ref):
        x = x_ref[...]
        pre = b_ref[...] + functools.reduce(
            lambda a, c: a + c, [_conv_shift(x, k - 3, L) * w_ref[k:k + 1, :] for k in range(7)])
        o_ref[...] = _silu(pre)

    col = pl.BlockSpec((T, 128), lambda j: (0, j))
    return _pc(body, "conv_fwd", _sds((T, 1024)), grid=(8,),
               in_specs=[col, pl.BlockSpec((8, 128), lambda j: (0, j)), pl.BlockSpec((1, 128), lambda j: (0, j))],
               out_specs=col)(XBC, w8, b)


def conv_bwd(XBC, w8, b, dS, dxs_skip, L, sends=()):
    T = XBC.shape[0]

    def body(x_ref, w_ref, b_ref, d0_r, d1_r, dsk_r, dx_o, dw_o, db_o):
        j = pl.program_id(0)
        x = x_ref[...]
        xs = [_conv_shift(x, k - 3, L) for k in range(7)]
        pre = b_ref[...] + functools.reduce(lambda a, c: a + c, [xs[k] * w_ref[k:k + 1, :] for k in range(7)])
        _, vjp = jax.vjp(_silu, pre)
        dact = d0_r[0] + d1_r[0] + jnp.where(j < 4, dsk_r[...], 0.0)
        dpre, = vjp(dact)
        dx_o[...] = functools.reduce(
            lambda a, c: a + c, [_conv_shift(dpre, 3 - k, L) * w_ref[k:k + 1, :] for k in range(7)])
        dw_o[...] = jnp.concatenate([jnp.sum(dpre * xs[k], axis=0, keepdims=True) for k in range(7)]
                                    + [jnp.zeros((1, 128), F32)], axis=0)
        db_o[...] = jnp.sum(dpre, axis=0, keepdims=True)

    col = pl.BlockSpec((T, 128), lambda j: (0, j))
    w_s = pl.BlockSpec((8, 128), lambda j: (0, j))
    b_s = pl.BlockSpec((1, 128), lambda j: (0, j))
    ds = lambda d: pl.BlockSpec((1, T, 128), lambda j: (d, 0, j))
    return _pc(body, "conv_bwd", [_sds((T, 1024)), _sds((8, 1024)), _sds((1, 1024))], grid=(8,),
               in_specs=[col, w_s, b_s, ds(0), ds(1), pl.BlockSpec((T, 128), lambda j: (0, jnp.minimum(j, 3)))],
               out_specs=[col, w_s, b_s], sends=sends)(XBC, w8, b, dS, dS, dxs_skip)


def _ssd_chunk(xs, bs, cs, dtraw, dtb, alog, hs, tri, d):
    dt = _softplus(dtraw + dtb)
    a = dt * (-jnp.exp(alog))
    acum = _exact(tri, a)
    tot = jnp.sum(a, axis=0, keepdims=True)
    wcol = jnp.exp(tot - acum) * dt
    ea = jnp.exp(acum)
    cd = jnp.exp(tot)
    acum_t, dt_t = acum.T, dt.T
    lane = _iota((Q, 128), 1)
    srow = _iota((128, Q), 0)
    lane1 = _iota((1, 128), 1)
    prow = _iota((128, NSTATE), 0)
    mask = tri > 0.5
    cbs = [mm_nt(cs[g], bs[g]) for g in range(2)]
    ys, hn = [], []
    for j in range(4):
        g = j // 2
        x = xs[j]
        yi, st, eac, cdl = [], [], [], []
        for u in range(2):
            slot = d * 8 + 2 * j + u
            col = lambda m: jnp.sum(jnp.where(lane == slot, m, 0.0), axis=1, keepdims=True)
            rowv = lambda m: jnp.sum(jnp.where(srow == slot, m, 0.0), axis=0, keepdims=True)
            seg = col(acum) - rowv(acum_t)
            dcy = jnp.where(mask, jnp.exp(jnp.where(mask, seg, 0.0)), 0.0)
            yi.append(mm(cbs[g] * dcy * rowv(dt_t), x))
            st.append(mm_tn(x, bs[g] * col(wcol)))
            eac.append(col(ea))
            cdl.append(jnp.sum(jnp.where(lane1 == slot, cd, 0.0), axis=1, keepdims=True))
        yin = mm_nt(cs[g], hs[j])
        ys.append(jnp.where(lane < HD, yi[0] + yin * eac[0], yi[1] + yin * eac[1]))
        hn.append(hs[j] * jnp.where(prow < HD, cdl[0], cdl[1]) + jnp.where(prow < HD, st[0], st[1]))
    return ys, hn


def _ssd_chunk_idx(d, s, nlc, nch):
    return jnp.where(d == 0, (s + nlc) % nch, nch - 1 - s)


def ssd_fwd(ACT, DT, dtb, alog, tri2, L, sends=()):
    T = ACT.shape[0]
    nlc, nch = L // Q, T // Q

    def body(a_ref, dt_ref, dtb_ref, al_ref, tri_ref, y_o, hs_o, hst):
        d, s = pl.program_id(0), pl.program_id(1)
        _acc_init(s == 0, [hst])
        a = a_ref[...]
        xs = [a[:, 128 * j:128 * (j + 1)] for j in range(4)]
        bs = [a[:, 512 + 128 * g:640 + 128 * g] for g in range(2)]
        cs = [a[:, 768 + 128 * g:896 + 128 * g] for g in range(2)]
        hs = [hst[j] for j in range(4)]
        hs_o[0, 0] = hst[...]
        ys, hn = _ssd_chunk(xs, bs, cs, dt_ref[...], dtb_ref[...], al_ref[...], hs, tri_ref[0], d)
        y_o[0] = jnp.concatenate(ys, axis=1)
        for j in range(4):
            hst[j] = hn[j]

    ck = lambda w: pl.BlockSpec((Q, w), lambda d, s: (_ssd_chunk_idx(d, s, nlc, nch), 0))
    v128 = pl.BlockSpec((1, 128), lambda d, s: (0, 0))
    return _pc(body, "ssd_fwd", [_sds((2, T, 512)), _sds((2, nch, 4, 128, NSTATE))], grid=(2, nch),
               in_specs=[ck(1024), ck(128), v128, v128, pl.BlockSpec((1, Q, Q), lambda d, s: (d, 0, 0))],
               out_specs=[pl.BlockSpec((1, Q, 512), lambda d, s: (d, _ssd_chunk_idx(d, s, nlc, nch), 0)),
                          pl.BlockSpec((1, 1, 4, 128, NSTATE), lambda d, s: (d, s, 0, 0, 0))],
               scratch=[pltpu.VMEM((4, 128, NSTATE), F32)], sends=sends, gather=True)(ACT, DT, dtb, alog, tri2)


def ssd_bwd(ACT, DT, dtb, alog, tri2, HS, dY, L, sends=()):
    T = ACT.shape[0]
    nlc, nch = L // Q, T // Q

    def body(a_ref, dt_ref, dtb_ref, al_ref, tri_ref, hs_ref, dy_ref, da_o, ddt_o, ddtb_o, dal_o, dh):
        d, sr = pl.program_id(0), pl.program_id(1)
        _acc_init(sr == 0, [dh, ddtb_o, dal_o])
        a = a_ref[...]
        xs = [a[:, 128 * j:128 * (j + 1)] for j in range(4)]
        bs = [a[:, 512 + 128 * g:640 + 128 * g] for g in range(2)]
        cs = [a[:, 768 + 128 * g:896 + 128 * g] for g in range(2)]
        hs = [hs_ref[0, 0, j] for j in range(4)]
        tri = tri_ref[0]
        fn = lambda xs_, bs_, cs_, dtr, dtb_, al, hs_: _ssd_chunk(xs_, bs_, cs_, dtr, dtb_, al, hs_, tri, d)
        _, vjp = jax.vjp(fn, xs, bs, cs, dt_ref[...], dtb_ref[...], al_ref[...], hs)
        dy = dy_ref[...]
        dys = [dy[:, 128 * j:128 * (j + 1)] for j in range(4)]
        dxs, dbs, dcs, ddt, ddtb, dal, dhs = vjp((dys, [dh[j] for j in range(4)]))
        da_o[0] = jnp.concatenate(dxs + dbs + dcs, axis=1)
        ddt_o[0] = ddt
        ddtb_o[0] += ddtb
        dal_o[0] += dal
        for j in range(4):
            dh[j] = dhs[j]

    cidx = lambda d, sr: _ssd_chunk_idx(d, nch - 1 - sr, nlc, nch)
    ck = lambda w: pl.BlockSpec((Q, w), lambda d, sr: (cidx(d, sr), 0))
    v128 = pl.BlockSpec((1, 128), lambda d, sr: (0, 0))
    o128 = pl.BlockSpec((1, 1, 128), lambda d, sr: (d, 0, 0))
    return _pc(body, "ssd_bwd", [_sds((2, T, 1024)), _sds((2, T, 128)), _sds((2, 1, 128)), _sds((2, 1, 128))],
               grid=(2, nch),
               in_specs=[ck(1024), ck(128), v128, v128, pl.BlockSpec((1, Q, Q), lambda d, sr: (d, 0, 0)),
                         pl.BlockSpec((1, 1, 4, 128, NSTATE), lambda d, sr: (d, nch - 1 - sr, 0, 0, 0)), ck(512)],
               out_specs=[pl.BlockSpec((1, Q, 1024), lambda d, sr: (d, cidx(d, sr), 0)),
                          pl.BlockSpec((1, Q, 128), lambda d, sr: (d, cidx(d, sr), 0)), o128, o128],
               scratch=[pltpu.VMEM((4, 128, NSTATE), F32)], sends=sends)(ACT, DT, dtb, alog, tri2, HS, dY)


_PAIR_HEADS = np.array([[0, 2], [1, 3]])


def _tables(L):
    t = jnp.arange(L)
    inv = 10000.0 ** (-jnp.arange(16, dtype=F32) / 16)

    def half(pos):
        ang = pos.astype(F32)[:, None] * inv[None, :]
        return jnp.concatenate([ang, ang], axis=1)

    ang = jnp.tile(jnp.concatenate([half(t // GW), half(t % GW)], axis=1), (1, 4))
    cos = jnp.concatenate([jnp.cos(ang), jnp.ones((LC, 256), F32)], axis=0)
    sin = jnp.concatenate([jnp.sin(ang), jnp.zeros((LC, 256), F32)], axis=0)
    rm = np.zeros((256, 256), np.float32)
    for j in range(256):
        if j % 32 < 16:
            rm[j + 16, j] = -1.0
        else:
            rm[j - 16, j] = 1.0
    tri = np.tril(np.ones((Q, Q), np.float32))
    return cos, sin, jnp.asarray(rm), jnp.asarray(np.stack([tri, tri.T]))


def _na_index(R):
    rc = np.array([0, 1, 2, 3, 4, R - 3, R - 2, R - 1])
    dy = np.clip(rc - 4, 0, R - 8)[:, None] + np.arange(8)[None, :] - rc[:, None] + 7
    qc, cc = np.arange(GW)[:, None], np.arange(GW)[None, :]
    dx = np.clip(cc - qc, -15, 15) + 15
    cstart = np.clip(qc - 8, 0, GW - 16)
    cmask = (cc >= cstart) & (cc < cstart + 16)
    idx = dy[:, None, :, None] * 31 + dx[None, :, None, :]
    return idx.reshape(8, GW, 8 * GW), np.broadcast_to(cmask[None, :, None, :], idx.shape).reshape(8, GW, 8 * GW), \
        dy, dx, cmask


def _na_bias(rpb, R):
    _, cm, dy, dx, _ = _na_index(R)
    e1t = np.zeros((128, GW * GW), np.float32)
    e1t[dx.reshape(-1), np.arange(GW * GW)] = 1.0
    v = jnp.pad(rpb[:, dy.reshape(-1), :].reshape(256, 31), ((0, 0), (0, 97)))
    full = exact_mm_call(v, jnp.asarray(e1t))
    dense = full.reshape(4, 8, 8, GW, GW).transpose(0, 1, 3, 2, 4).reshape(4, 8, GW, 8 * GW)
    return jnp.where(cm[None], dense, NEG)


def _na_bias_grad(dbias, R):
    _, _, dy, dx, cmask = _na_index(R)
    e1 = np.zeros((GW * GW, 128), np.float32)
    e1[np.arange(GW * GW), dx.reshape(-1)] = cmask.reshape(-1)
    a1 = dbias.reshape(4, 8, GW, 8, GW).transpose(0, 1, 3, 2, 4).reshape(256, GW * GW)
    v = exact_mm_call(a1, jnp.asarray(e1))[:, :31].reshape(4, 64, 31)
    e2 = np.zeros((64, 128), np.float32)
    e2[np.arange(64), dy.reshape(-1)] = 1.0
    a2 = jnp.pad(v.transpose(0, 2, 1).reshape(124, 64), ((0, 4), (0, 0)))
    return exact_mm_call(a2, jnp.asarray(e2))[:124, :15].reshape(4, 31, 15).transpose(0, 2, 1)


def _lanes(v, n=128):
    v = v.reshape(1, -1)
    return jnp.pad(v, ((0, 0), (0, n - v.shape[1])))


def _cls2(a, b):
    return jnp.stack([a, b]).reshape(2, 1, D)


def _win_p(g):
    return jnp.concatenate([g.reshape(IN_COLS, D), jnp.zeros((NP_IN - IN_COLS, D), g.dtype)], axis=0)


def _layer_consts(p):
    sinkp = jnp.broadcast_to(p["wa_sink"][_PAIR_HEADS][:, :, None, None], (2, 2, 1, 128))
    return dict(
        sinkp=sinkp, nosink=jnp.full((2, 2, 1, 128), NEG, F32),
        w8=jnp.concatenate([p["ssm_conv_w"], jnp.zeros((1, 1024), F32)], axis=0),
        cb=p["ssm_conv_b"].reshape(1, 1024), dtb=_lanes(p["ssm_dt_bias"]), alog=_lanes(p["ssm_a_log"]),
        dsk=jnp.repeat(p["ssm_d"], HD).reshape(1, 512), gs=p["ssm_norm_g"].reshape(1, 512),
        gmix=p["g_mix"].reshape(1, D), gffn=p["g_ffn"].reshape(1, D))


def _mods(mod2):
    return [_cls2(mod2[0, D * k:D * (k + 1)], mod2[1, D * k:D * (k + 1)]) for k in range(6)]


def _layer_fwd(X, mod2, c, rpb, tabs, L, ctx_out, shards, nxt):
    cos, sin, rm, tri2 = tabs
    sh1, sc1, gt1, sh2, sc2, gt2 = _mods(mod2)
    biasd = _na_bias(rpb, L // GW)
    fi, fo, wo = shards
    fcut, ocut = 448, 224
    (qa, qb, z, ka, va, kb, vb, xbc, dt, h1), (gfo_a,) = in_fwd(X, c["gmix"], sh1, sc1, c["win"], cos, sin, rm, L,
                                                                sends=(fo[:ocut],))
    (oa,), (gfi_b,) = wa_fwd(qa, ka, va, c["sinkp"], L, sends=(fi[fcut:],))
    (ob,), (gwo,) = na_fwd(qb, kb, vb, biasd, L, sends=(wo,))
    c = dict(c, wout=gwo.reshape(D, D))
    if ctx_out:
        oa_c = ctx_fwd(qa, ka, va, c["sinkp"], True, L)
        ob_c = ctx_fwd(qb, kb, vb, c["nosink"], False, L)
    else:
        oa_c = ob_c = jnp.zeros((LC, 256), F32)
    oa = jnp.concatenate([oa, oa_c], axis=0)
    ob = jnp.concatenate([ob, ob_c], axis=0)
    act = conv_fwd(xbc, c["w8"], c["cb"], L)
    (y2, hs), (gfi_a,) = ssd_fwd(act, dt, c["dtb"], c["alog"], tri2, L, sends=(fi[:fcut],))
    (X1, cat), (gfo_b,) = out_fwd(oa, ob, y2, act, z, c["dsk"], c["gs"], c["wout"], X, gt1, L, sends=(fo[ocut:],))
    c = dict(c, wfi=jnp.concatenate([gfi_a, gfi_b], axis=1).reshape(2 * DFF, D),
             wfo=jnp.concatenate([gfo_a, gfo_b], axis=1).reshape(DFF, D))
    res = ffn_fwd(X1, c["gffn"], sh2, sc2, gt2, c["wfi"], c["wfo"], L, sends=nxt)
    (X2, ff), got = res if nxt else (res, ())
    saved = dict(X=X, X1=X1, ff=ff, qa=qa, qb=qb, z=z, ka=ka, va=va, kb=kb, vb=vb, xbc=xbc, dt=dt, h1=h1, oa=oa, ob=ob,
                 act=act, y2=y2, hs=hs, cat=cat, biasd=biasd)
    return X2, saved, c, got


def _row_blocks(gw):
    return gw.reshape(NDEV, gw.shape[0] // NDEV, gw.shape[1])


def _layer_bwd(dX2, s, mod2, c, tabs, L, ctx_out, carry):
    cos, sin, rm, tri2 = tabs
    sh1, sc1, gt1, sh2, sc2, gt2 = _mods(mod2)
    R = L // GW
    res = ffn_bwd(s["X1"], c["gffn"], sh2, sc2, gt2, c["wfi"], c["wfo"], s["ff"], dX2, L, sends=carry)
    (dX1, h2, dU, actf, dOut, dgffn, dsh2, dsc2, dgt2), got = res if carry else (res, ())
    g = {}
    gfi = _row_blocks(tn_mm(dU, h2, 512, 1024, MXU))
    gfo = _row_blocks(tn_mm(actf, dOut, 256, 1024, MXU))
    doa, dob, dy, dxs_skip, dz, dmix, ddsk, dgs, dgt1 = out_bwd(s["oa"], s["ob"], s["y2"], s["act"], s["z"], c["dsk"],
                                                                c["gs"], c["wout"], gt1, dX1, L)
    gout = _row_blocks(tn_mm(s["cat"], dmix, 512, 1024, MXU))
    (dS, ddt2, ddtb, dal), (g["w_ffn_in"],) = ssd_bwd(
        s["act"], s["dt"], c["dtb"], c["alog"], tri2, s["hs"], dy, L, sends=(gfi,))
    (dxbc, dw8, dcb), (g["w_ffn_out"],) = conv_bwd(s["xbc"], c["w8"], c["cb"], dS, dxs_skip, L, sends=(gfo,))
    (dqa, dka, dva, dska), (g["w_out"],) = wa_bwd(s["qa"], s["ka"], s["va"], c["sinkp"], doa, L, sends=(gout,))
    dka, dva = dka[Q:], dva[Q:]
    dqb, dkb, dvb, dkxb, dvxb, dbias = na_bwd(s["qb"], s["kb"], s["vb"], s["biasd"], dob, L)
    if ctx_out:
        dqa_c, dk1, dv1, dsk1 = ctx_bwd(s["qa"], s["ka"], s["va"], c["sinkp"], doa, True, L)
        dqb_c, dk2, dv2, _ = ctx_bwd(s["qb"], s["kb"], s["vb"], c["nosink"], dob, False, L)
        dka = jnp.concatenate([dka[:L], dka[L:] + dk1], axis=0)
        dva = jnp.concatenate([dva[:L], dva[L:] + dv1], axis=0)
        dska = dska + dsk1
        dkxb, dvxb = dkxb + dk2, dvxb + dv2
    else:
        dqa_c = dqb_c = jnp.zeros((LC, 256), F32)
    cat0 = lambda a, b: jnp.concatenate([a, b], axis=0)
    dX, dycat, dgmix, dsh1, dsc1 = in_bwd(
        s["X"], c["gmix"], sh1, sc1, c["win"], cos, sin, rm, dX1, cat0(dqa, dqa_c), cat0(dqb, dqb_c), dz,
        dka, dva, cat0(dkb, dkxb), cat0(dvb, dvxb), dxbc, ddt2, L,
        latent_only=ctx_out)
    if ctx_out:
        gin = [_row_blocks(tn_mm(dycat, s["h1"], 512, D // 2, MXU, ncol=D // 2, col0=k)[:IN_COLS]) for k in (0, 1)]
    else:
        gin = _row_blocks(tn_mm(dycat, s["h1"], 512, 1024, MXU)[:IN_COLS])
    g["g_mix"] = dgmix.reshape(D)
    g["g_ffn"] = dgffn.reshape(D)
    sk = jnp.sum(dska, axis=(2, 3))
    g["wa_sink"] = jnp.zeros((4,), F32).at[_PAIR_HEADS.reshape(-1)].set(sk.reshape(-1))
    g["na_rpb"] = _na_bias_grad(dbias, R)
    g["ssm_conv_w"] = dw8[:7]
    g["ssm_conv_b"] = dcb.reshape(1024)
    g["ssm_dt_bias"] = (ddtb[0] + ddtb[1])[0, :16].reshape(2, 8)
    g["ssm_a_log"] = (dal[0] + dal[1])[0, :16].reshape(2, 8)
    g["ssm_d"] = jnp.sum(ddsk.reshape(8, HD), axis=1)
    g["ssm_norm_g"] = dgs.reshape(512)
    dmod2 = jnp.concatenate([dsh1, dsc1, dgt1, dsh2, dsc2, dgt2], axis=2).reshape(2, 6 * D)
    return dX, g, dmod2, gin, got


def local_step(x, ctx, tgt, mods, layers, shards, g_final, L):
    tabs = _tables(L)
    X = (x, ctx)
    consts = [_layer_consts(p) for p in layers]
    saved = []
    got = all_gather([shards["w_in"][0]], "gather_first")
    for i in range(2):
        consts[i] = dict(consts[i], win=_win_p(got[0]))
        nxt = (shards["w_in"][1],) if i == 0 else ()
        X, s, consts[i], got = _layer_fwd(X, mods[i], consts[i], layers[i]["na_rpb"], tabs, L, i == 0,
                                          (shards["w_ffn_in"][i], shards["w_ffn_out"][i], shards["w_out"][i]), nxt)
        saved.append(s)
    loss8, dX, dgfin = loss_head(X, g_final.reshape(1, D), tgt, L)
    grads, dmods = [None, None], [None, None]
    dX, grads[1], dmods[1], gin1, _ = _layer_bwd(dX, saved[1], mods[1], consts[1], tabs, L, False, ())
    dX, grads[0], dmods[0], gin0, (grads[1]["w_in"],) = _layer_bwd(dX, saved[0], mods[0], consts[0], tabs, L, True,
                                                                   (gin1,))
    return loss8[0, 0], dX, grads, jnp.stack(dmods), dgfin.reshape(D), gin0


def _place():
    x, y, c = lax.axis_index("x"), lax.axis_index("y"), lax.axis_index("c")
    return x, y, c


def _slot(b):
    return 4 * b[0] + 2 * b[1] + b[2]


def _any():
    return pl.BlockSpec(memory_space=pl.ANY)


def all_gather(xs, name):
    n = len(xs)

    def body(*refs):
        x_refs, o_refs = refs[:n], refs[n:2 * n]
        send_sems, recv_sems, local_sems = refs[2 * n:]
        x, y, c = _place()
        me, sib = (x, y, c), (x, y, 1 - c)
        chips = [(1 - x, y), (x, 1 - y), (1 - x, 1 - y)]

        def copy(t, k, blk, to, src=None):
            dst = o_refs[t].at[_slot(blk)]
            return pltpu.make_async_remote_copy(
                src_ref=dst if src is None else src, dst_ref=dst, send_sem=send_sems.at[7 * t + k],
                recv_sem=recv_sems.at[7 * t + k], device_id=to, device_id_type=MESH_T)

        mine = [pltpu.make_async_copy(x_refs[t], o_refs[t].at[_slot(me)], local_sems.at[t]) for t in range(n)]
        for cp in mine:
            cp.start()
        first = []
        for t in range(n):
            first.append(copy(t, 0, me, sib, src=x_refs[t]))
            first += [copy(t, 1 + j, me, (*chip, c), src=x_refs[t]) for j, chip in enumerate(chips)]
        for cp in first:
            cp.start()
        passed = []
        for j, chip in enumerate(chips):
            for t in range(n):
                copy(t, 1 + j, (*chip, c), me).wait_recv()
                cp = copy(t, 4 + j, (*chip, c), sib)
                cp.start()
                passed.append(cp)
        for t in range(n):
            copy(t, 0, sib, me).wait_recv()
            for j, chip in enumerate(chips):
                copy(t, 4 + j, (*chip, 1 - c), me).wait_recv()
        for cp in first + passed:
            cp.wait_send()
        for cp in mine:
            cp.wait()

    return pl.pallas_call(
        body, name=name, out_shape=[_sds((NDEV,) + a.shape, a.dtype) for a in xs],
        in_specs=[_any()] * n, out_specs=[_any()] * n,
        scratch_shapes=[pltpu.SemaphoreType.DMA((7 * n,)), pltpu.SemaphoreType.DMA((7 * n,)),
                        pltpu.SemaphoreType.DMA((n,))],
        interpret=_INTERPRET)(*xs)


def all_to_all(xs, name):
    n = len(xs)

    def body(*refs):
        _a2a_start(refs[:n], refs[n:2 * n], *refs[2 * n:])
        _a2a_wait(refs[:n], refs[n:2 * n], *refs[2 * n:])

    return pl.pallas_call(
        body, name=name, out_shape=[_sds(a.shape, a.dtype) for a in xs],
        in_specs=[_any()] * n, out_specs=[_any()] * n, scratch_shapes=_a2a_sems(n), interpret=_INTERPRET)(*xs)


def _a2a_sems(n):
    return [pltpu.SemaphoreType.DMA((7 * n,)), pltpu.SemaphoreType.DMA((7 * n,)), pltpu.SemaphoreType.DMA((n,))]


def _a2a_copies(x_refs, o_refs, send_sems, recv_sems, local_sems):
    n = len(x_refs)
    x, y, c = _place()
    me = (x, y, c)
    flip = lambda v, b: (1 - v) if b else v
    peers = [(flip(x, k >> 2 & 1), flip(y, k >> 1 & 1), flip(c, k & 1)) for k in range(1, NDEV)]
    mine = [pltpu.make_async_copy(x_refs[t].at[_slot(me)], o_refs[t].at[_slot(me)], local_sems.at[t])
            for t in range(n)]

    def copy(t, k, src_slot, dst_slot, to):
        return pltpu.make_async_remote_copy(
            src_ref=x_refs[t].at[src_slot], dst_ref=o_refs[t].at[dst_slot], send_sem=send_sems.at[7 * t + k],
            recv_sem=recv_sems.at[7 * t + k], device_id=to, device_id_type=MESH_T)

    sends = [copy(t, k, _slot(p), _slot(me), p) for t in range(n) for k, p in enumerate(peers)]
    recvs = [copy(t, k, _slot(p), _slot(p), me) for t in range(n) for k, p in enumerate(peers)]
    return mine, sends, recvs


def _ag_copies(x_refs, o_refs, send_sems, recv_sems, local_sems):
    n = len(x_refs)
    x, y, c = _place()
    me = (x, y, c)
    flip = lambda v, b: (1 - v) if b else v
    peers = [(flip(x, k >> 2 & 1), flip(y, k >> 1 & 1), flip(c, k & 1)) for k in range(1, NDEV)]
    mine = [pltpu.make_async_copy(x_refs[t], o_refs[t].at[_slot(me)], local_sems.at[t]) for t in range(n)]

    def copy(t, k, dst_slot, to):
        return pltpu.make_async_remote_copy(
            src_ref=x_refs[t], dst_ref=o_refs[t].at[dst_slot], send_sem=send_sems.at[7 * t + k],
            recv_sem=recv_sems.at[7 * t + k], device_id=to, device_id_type=MESH_T)

    sends = [copy(t, k, _slot(me), p) for t in range(n) for k, p in enumerate(peers)]
    recvs = [copy(t, k, _slot(p), me) for t in range(n) for k, p in enumerate(peers)]
    return mine, sends, recvs


def _ag_start(x_refs, o_refs, send_sems, recv_sems, local_sems):
    mine, sends, _ = _ag_copies(x_refs, o_refs, send_sems, recv_sems, local_sems)
    for cp in mine + sends:
        cp.start()


def _ag_wait(x_refs, o_refs, send_sems, recv_sems, local_sems):
    mine, sends, recvs = _ag_copies(x_refs, o_refs, send_sems, recv_sems, local_sems)
    for cp in recvs:
        cp.wait_recv()
    for cp in sends:
        cp.wait_send()
    for cp in mine:
        cp.wait()


def _a2a_start(x_refs, o_refs, send_sems, recv_sems, local_sems):
    mine, sends, _ = _a2a_copies(x_refs, o_refs, send_sems, recv_sems, local_sems)
    for cp in mine + sends:
        cp.start()


def _a2a_wait(x_refs, o_refs, send_sems, recv_sems, local_sems):
    mine, sends, recvs = _a2a_copies(x_refs, o_refs, send_sems, recv_sems, local_sems)
    for cp in recvs:
        cp.wait_recv()
    for cp in sends:
        cp.wait_send()
    for cp in mine:
        cp.wait()


def adam_reduce(P, w, m, v, name, sends=()):
    n, R, C = P.shape
    br = R // 4 if R % 64 == 0 else R

    def body(p_ref, w_ref, m_ref, v_ref, g_o, d_o, m_o, v_o):
        g = p_ref[0].astype(F32)
        for k in range(1, n):
            g = g + p_ref[k].astype(F32)
        m1 = ADAM_B1 * m_ref[...] + (1.0 - ADAM_B1) * g
        v1 = ADAM_B2 * v_ref[...] + (1.0 - ADAM_B2) * jnp.square(g)
        m_hat = m1 / (1.0 - ADAM_B1 ** ADAM_STEP)
        v_hat = v1 / (1.0 - ADAM_B2 ** ADAM_STEP)
        g_o[...] = g
        d_o[...] = -ADAM_LR * (m_hat / (jnp.sqrt(v_hat) + ADAM_EPS) + ADAM_WD * w_ref[...])
        m_o[...] = m1
        v_o[...] = v1

    blk = pl.BlockSpec((br, C), lambda i: (i, 0))
    return _pc(body, name, [_sds((R, C))] * 4, grid=(R // br,),
               in_specs=[pl.BlockSpec((n, br, C), lambda i: (0, i, 0)), blk, blk, blk], out_specs=[blk] * 4,
               sends=sends)(P, w, m, v)


def adam_layers(P0, P1, w, m, v, name, sends=()):
    n, R, C = P0.shape
    br = R // 4 if R % 64 == 0 else R
    nb = R // br

    def body(p0_ref, p1_ref, w_ref, m_ref, v_ref, g_o, d_o, m_o, v_o):
        def total(p_ref):
            g = p_ref[0].astype(F32)
            for k in range(1, n):
                g = g + p_ref[k].astype(F32)
            return g

        g = jnp.where(pl.program_id(0) == 0, total(p0_ref), total(p1_ref))
        m1 = ADAM_B1 * m_ref[0] + (1.0 - ADAM_B1) * g
        v1 = ADAM_B2 * v_ref[0] + (1.0 - ADAM_B2) * jnp.square(g)
        m_hat = m1 / (1.0 - ADAM_B1 ** ADAM_STEP)
        v_hat = v1 / (1.0 - ADAM_B2 ** ADAM_STEP)
        g_o[0] = g
        d_o[0] = -ADAM_LR * (m_hat / (jnp.sqrt(v_hat) + ADAM_EPS) + ADAM_WD * w_ref[0])
        m_o[0] = m1
        v_o[0] = v1

    blk = pl.BlockSpec((1, br, C), lambda l, i: (l, i, 0))
    p0 = pl.BlockSpec((n, br, C), lambda l, i: (0, jnp.where(l == 0, i, nb - 1), 0))
    p1 = pl.BlockSpec((n, br, C), lambda l, i: (0, jnp.where(l == 1, i, 0), 0))
    return _pc(body, name, [_sds((2, R, C))] * 4, grid=(2, nb), in_specs=[p0, p1, blk, blk, blk],
               out_specs=[blk] * 4, sends=sends)(P0, P1, w, m, v)


def mod_fwd(scin, wmod, bcol):
    def body(s_ref, w_ref, b_ref, o_ref):
        o_ref[0] = mm(_silu(s_ref[...]), w_ref[0]) + b_ref[0]

    return _pc(body, "mod_fwd", _sds((2, 16, 768)), grid=(2,),
               in_specs=[pl.BlockSpec((16, D), lambda l: (0, 0)), pl.BlockSpec((1, D, 768), lambda l: (l, 0, 0)),
                         pl.BlockSpec((1, 1, 768), lambda l: (l, 0, 0))],
               out_specs=pl.BlockSpec((1, 16, 768), lambda l: (l, 0, 0)))(scin, wmod, bcol)


def mod_bwd(scin, wmod, G):
    def body(s_ref, w_ref, g_ref, dw_o, ds_o):
        _, vjp = jax.vjp(lambda s, w: mm(_silu(s), w), s_ref[...], w_ref[0])
        ds, dw = vjp(g_ref[0])
        dw_o[0] = dw
        _acc_init(pl.program_id(0) == 0, [ds_o])
        ds_o[...] += ds

    full = pl.BlockSpec((16, D), lambda l: (0, 0))
    wsp = pl.BlockSpec((1, D, 768), lambda l: (l, 0, 0))
    return _pc(body, "mod_bwd", [_sds((2, D, 768)), _sds((16, D))], grid=(2,),
               in_specs=[full, wsp, pl.BlockSpec((1, 16, 768), lambda l: (l, 0, 0))], out_specs=[wsp, full])(
        scin, wmod, G)


_SMALL = ["b_mod", "g_mix", "wa_sink", "na_rpb", "ssm_conv_w", "ssm_conv_b", "ssm_dt_bias", "ssm_a_log", "ssm_d",
          "ssm_norm_g", "g_ffn", "g_final", "dmod_s", "dmod_c"]


def _pack(parts):
    rows = []
    for a in parts:
        f = a.reshape(-1).astype(F32)
        rows.append(jnp.pad(f, (0, (-f.shape[0]) % 1024)).reshape(-1, 128))
    return jnp.concatenate(rows, axis=0)


def _unpack(packed, shapes):
    out, r = [], 0
    for s in shapes:
        nel = int(np.prod(s))
        nr = -(-nel // 1024) * 8
        out.append(packed[r:r + nr].reshape(-1)[:nel].reshape(s))
        r += nr
    return out


def kernel(x, c, ctx, c_ctx, w_mod, b_mod, g_mix, w_in, wa_sink, na_rpb, ssm_conv_w, ssm_conv_b, ssm_dt_bias, ssm_a_log, ssm_d, ssm_norm_g, w_out, g_ffn, w_ffn_in, w_ffn_out, g_final, loss_target, m_c_ctx, m_w_mod, m_b_mod, m_g_mix, m_w_in, m_wa_sink, m_na_rpb, m_ssm_conv_w, m_ssm_conv_b, m_ssm_dt_bias, m_ssm_a_log, m_ssm_d, m_ssm_norm_g, m_w_out, m_g_ffn, m_w_ffn_in, m_w_ffn_out, m_g_final, v_c_ctx, v_w_mod, v_b_mod, v_g_mix, v_w_in, v_wa_sink, v_na_rpb, v_ssm_conv_w, v_ssm_conv_b, v_ssm_dt_bias, v_ssm_a_log, v_ssm_d, v_ssm_norm_g, v_w_out, v_g_ffn, v_w_ffn_in, v_w_ffn_out, v_g_final):
    L = x.shape[1]
    px, py, pc = _place()
    me = 4 * px + 2 * py + pc
    W = dict(c_ctx=c_ctx, w_mod=w_mod, b_mod=b_mod, g_mix=g_mix, w_in=w_in, wa_sink=wa_sink, na_rpb=na_rpb,
             ssm_conv_w=ssm_conv_w, ssm_conv_b=ssm_conv_b, ssm_dt_bias=ssm_dt_bias, ssm_a_log=ssm_a_log, ssm_d=ssm_d,
             ssm_norm_g=ssm_norm_g, w_out=w_out, g_ffn=g_ffn, w_ffn_in=w_ffn_in, w_ffn_out=w_ffn_out, g_final=g_final)
    M = dict(c_ctx=m_c_ctx, w_mod=m_w_mod, b_mod=m_b_mod, g_mix=m_g_mix, w_in=m_w_in, wa_sink=m_wa_sink,
             na_rpb=m_na_rpb, ssm_conv_w=m_ssm_conv_w, ssm_conv_b=m_ssm_conv_b, ssm_dt_bias=m_ssm_dt_bias,
             ssm_a_log=m_ssm_a_log, ssm_d=m_ssm_d, ssm_norm_g=m_ssm_norm_g, w_out=m_w_out, g_ffn=m_g_ffn,
             w_ffn_in=m_w_ffn_in, w_ffn_out=m_w_ffn_out, g_final=m_g_final)
    V = dict(c_ctx=v_c_ctx, w_mod=v_w_mod, b_mod=v_b_mod, g_mix=v_g_mix, w_in=v_w_in, wa_sink=v_wa_sink,
             na_rpb=v_na_rpb, ssm_conv_w=v_ssm_conv_w, ssm_conv_b=v_ssm_conv_b, ssm_dt_bias=v_ssm_dt_bias,
             ssm_a_log=v_ssm_a_log, ssm_d=v_ssm_d, ssm_norm_g=v_ssm_norm_g, w_out=v_w_out, g_ffn=v_g_ffn,
             w_ffn_in=v_w_ffn_in, w_ffn_out=v_w_ffn_out, g_final=v_g_final)

    c_all, conv_all = all_gather([c, ssm_conv_w], "gather_small")
    tr = lambda a: a.transpose(0, 2, 1)
    shards = dict(w_in=tr(w_in).astype(MXU), w_out=w_out.astype(MXU), w_ffn_in=tr(w_ffn_in).astype(MXU),
                  w_ffn_out=w_ffn_out.astype(MXU))
    conv_f = conv_all.transpose(1, 2, 0, 3).reshape(2, 7, 1024)

    scin = jnp.concatenate([c_all.reshape(NDEV, D), c_ctx.reshape(1, D), jnp.zeros((7, D), F32)], axis=0)
    bcol = lax.dynamic_slice_in_dim(b_mod, me * 768, 768, axis=1).reshape(2, 1, 768)
    mod_all, = all_gather([mod_fwd(scin, w_mod, bcol)], "gather_mod")
    mod_rows = mod_all.transpose(1, 2, 0, 3).reshape(2, 16, 6 * D)
    mods = jnp.stack([lax.dynamic_index_in_dim(mod_rows, me, axis=1, keepdims=False), mod_rows[:, 8]], axis=1)

    layers = [dict(g_mix=g_mix[i], wa_sink=wa_sink[i], na_rpb=na_rpb[i], ssm_conv_w=conv_f[i],
                   ssm_conv_b=ssm_conv_b[i], ssm_dt_bias=ssm_dt_bias[i], ssm_a_log=ssm_a_log[i], ssm_d=ssm_d[i],
                   ssm_norm_g=ssm_norm_g[i], g_ffn=g_ffn[i]) for i in range(2)]
    loss, dx, grads, dmods, dgfin, gin0 = local_step(x[0], ctx[0], loss_target[0], mods, layers, shards, g_final, L)
    loss = lax.psum(loss, ("x", "y", "c"))

    stk = lambda n: jnp.stack([grads[0][n], grads[1][n]])
    small = dict(b_mod=dmods[:, 0] + dmods[:, 1], g_final=dgfin, dmod_s=dmods[:, 0], dmod_c=dmods[:, 1])
    for nme in _SMALL:
        if nme not in small:
            small[nme] = stk(nme)
    shapes = [small[nme].shape for nme in _SMALL]
    zero_like = lambda nme: jnp.zeros(small[nme].shape, F32)
    own = lambda S, nme: S[nme] if (nme in S and S[nme].shape == small[nme].shape) else zero_like(nme)
    gath, = all_gather([_pack([small[nme] for nme in _SMALL])], "gather_grads")
    sm = adam_reduce(gath, _pack([own(W, nme) for nme in _SMALL]), _pack([own(M, nme) for nme in _SMALL]),
                     _pack([own(V, nme) for nme in _SMALL]), "adam_small")
    res = {nme: vals for nme, vals in zip(_SMALL, zip(*[_unpack(a, shapes) for a in sm]))}

    cols = lambda a: lax.dynamic_slice_in_dim(a, me * 768, 768, axis=-1)
    gparts = [_unpack(gath[d], shapes) for d in range(NDEV)]
    dmod_s_all = jnp.stack([gparts[d][_SMALL.index("dmod_s")] for d in range(NDEV)], axis=1)
    G = jnp.concatenate([cols(dmod_s_all), cols(res["dmod_c"][0])[:, None, :], jnp.zeros((2, 7, 768), F32)], axis=1)
    dwmod, dscin = mod_bwd(scin, w_mod, G)
    cc_g, = all_gather([dscin[8].reshape(8, 128)], "gather_cctx")
    out = {}
    out["c_ctx"] = [a.reshape(D) for a in adam_reduce(cc_g, c_ctx.reshape(8, 128), m_c_ctx.reshape(8, 128),
                                                      v_c_ctx.reshape(8, 128), "adam_cctx")]
    res_wmod, (got1,) = adam_reduce(dwmod.reshape(1, 2 * D, 768), w_mod.reshape(2 * D, 768),
                                    m_w_mod.reshape(2 * D, 768), v_w_mod.reshape(2 * D, 768), "adam_wmod",
                                    sends=(gin0[1],))
    out["w_mod"] = [a.reshape(2, D, 768) for a in res_wmod]
    gconv = lax.dynamic_slice_in_dim(res["ssm_conv_w"][0], me * 128, 128, axis=2)
    out["ssm_conv_w"] = [a.reshape(2, 7, 128) for a in adam_reduce(
        gconv.reshape(1, 14, 128), ssm_conv_w.reshape(14, 128), m_ssm_conv_w.reshape(14, 128),
        v_ssm_conv_w.reshape(14, 128), "adam_conv")]
    for nme in _SMALL:
        if nme not in ("ssm_conv_w", "dmod_s", "dmod_c"):
            out[nme] = list(res[nme])

    adam_big = lambda nme, t, **kw: adam_layers(grads[0][nme], grads[1][nme], t(W[nme]), t(M[nme]), t(V[nme]),
                                                "adam_" + nme, **kw)
    same = lambda a: a
    res_fi, (got0,) = adam_big("w_ffn_in", tr, sends=(gin0[0],))
    grads[0]["w_in"] = jnp.concatenate([got0, got1], axis=2)
    out["w_ffn_in"] = [tr(a) for a in res_fi]
    out["w_ffn_out"] = list(adam_big("w_ffn_out", same))
    out["w_out"] = list(adam_big("w_out", same))
    out["w_in"] = [tr(a) for a in adam_big("w_in", tr)]
    order = ["c_ctx", "w_mod", "b_mod", "g_mix", "w_in", "wa_sink", "na_rpb", "ssm_conv_w", "ssm_conv_b",
             "ssm_dt_bias", "ssm_a_log", "ssm_d", "ssm_norm_g", "w_out", "g_ffn", "w_ffn_in", "w_ffn_out", "g_final"]
    return (loss, dx.reshape(1, L, D), *[out[nme][0] for nme in order], *[out[nme][1] for nme in order],
            *[out[nme][2] for nme in order], *[out[nme][3] for nme in order])
```

```python
import functools
import math

import numpy as np
import jax
import jax.numpy as jnp
from jax import lax
from jax.experimental import pallas as pl
from jax.experimental.pallas import tpu as pltpu

F32 = jnp.float32
MXU = jnp.bfloat16
_INTERPRET = False
VMEM_LIMIT = 60 * 1024 * 1024

D = 1024
LC = 256
GW = 64
HD = 64
EPS = 1e-6
NEG = -1e30
NDEV = 8
Q = 128
NSTATE = 128
DFF = 2816
IN_COLS = 2832
NP_IN = 3072
C_QA, C_QB, C_Z, C_KA, C_VA, C_KB, C_VB, C_XBC, C_DT = 0, 256, 512, 1024, 1152, 1280, 1536, 1792, 2816
ADAM_LR, ADAM_B1, ADAM_B2, ADAM_EPS, ADAM_WD, ADAM_STEP = 0.001, 0.9, 0.999, 1e-08, 0.01, 10
MESH_T = pl.DeviceIdType.MESH


def _dg(a, b, ca, cb):
    return lax.dot_general(a.astype(MXU), b.astype(MXU), (((ca,), (cb,)), ((), ())), preferred_element_type=F32)


@jax.custom_vjp
def mm(a, b):
    return _dg(a, b, 1, 0)


def _mm_f(a, b):
    return _dg(a, b, 1, 0), (a, b)


def _mm_b(res, g):
    a, b = res
    return _dg(g, b, 1, 1).astype(a.dtype), _dg(a, g, 0, 0).astype(b.dtype)


mm.defvjp(_mm_f, _mm_b)


@jax.custom_vjp
def mm_nt(a, b):
    return _dg(a, b, 1, 1)


def _mmnt_f(a, b):
    return _dg(a, b, 1, 1), (a, b)


def _mmnt_b(res, g):
    a, b = res
    return _dg(g, b, 1, 0).astype(a.dtype), _dg(g, a, 0, 0).astype(b.dtype)


mm_nt.defvjp(_mmnt_f, _mmnt_b)


@jax.custom_vjp
def mm_tn(a, b):
    return _dg(a, b, 0, 0)


def _mmtn_f(a, b):
    return _dg(a, b, 0, 0), (a, b)


def _mmtn_b(res, g):
    a, b = res
    return _dg(b, g, 1, 1).astype(a.dtype), _dg(a, g, 1, 0).astype(b.dtype)


mm_tn.defvjp(_mmtn_f, _mmtn_b)


@jax.custom_vjp
def mmw(a, w):
    return _dg(a, w, 1, 0)


mmw.defvjp(lambda a, w: (_dg(a, w, 1, 0), w), lambda w, g: (_dg(g, w, 1, 1), None))


@jax.custom_vjp
def mmw_nt(a, w):
    return _dg(a, w, 1, 1)


mmw_nt.defvjp(lambda a, w: (_dg(a, w, 1, 1), w), lambda w, g: (_dg(g, w, 1, 0), None))


def _exact(a, b):
    return lax.dot_general(a, b, (((1,), (0,)), ((), ())), precision=lax.Precision.HIGHEST,
                           preferred_element_type=F32)


def _pc(body, name, out_shape, grid=None, in_specs=None, out_specs=None, scratch=(), sends=(), gather=False):
    params = pltpu.CompilerParams(vmem_limit_bytes=VMEM_LIMIT)
    if sends and not isinstance(out_shape, (list, tuple)):
        out_shape, out_specs = [out_shape], [out_specs]
    start, wait = (_ag_start, _ag_wait) if gather else (_a2a_start, _a2a_wait)
    if not sends:
        kw = {}
        if grid is not None:
            kw = dict(grid=grid, in_specs=in_specs, out_specs=out_specs)
        elif in_specs is not None:
            kw = dict(in_specs=in_specs, out_specs=out_specs)
        return pl.pallas_call(body, name=name, out_shape=out_shape, scratch_shapes=list(scratch),
                              compiler_params=params, interpret=_INTERPRET, **kw)
    n, nin, nout, nscr = len(sends), len(in_specs), len(out_shape), len(scratch)

    def body2(*refs):
        cin, xs = refs[:nin], refs[nin:nin + n]
        couts, os_ = refs[nin + n:nin + n + nout], refs[nin + n + nout:nin + 2 * n + nout]
        cscr, sems = refs[nin + 2 * n + nout:nin + 2 * n + nout + nscr], refs[nin + 2 * n + nout + nscr:]
        ids = [pl.program_id(a) for a in range(len(grid))]
        first = functools.reduce(lambda a, b: a & b, [i == 0 for i in ids])
        last = functools.reduce(lambda a, b: a & b, [i == g - 1 for i, g in zip(ids, grid)])

        @pl.when(first)
        def _():
            start(xs, os_, *sems)

        body(*cin, *couts, *cscr)

        @pl.when(last)
        def _():
            wait(xs, os_, *sems)

    call = pl.pallas_call(
        body2, name=name,
        out_shape=list(out_shape) + [_sds(((NDEV,) if gather else ()) + a.shape, a.dtype) for a in sends],
        grid=grid, in_specs=list(in_specs) + [_any()] * n, out_specs=list(out_specs) + [_any()] * n,
        scratch_shapes=list(scratch) + _a2a_sems(n), compiler_params=params, interpret=_INTERPRET)

    def run(*args):
        res = call(*args, *sends)
        return res[:nout], res[nout:]

    return run


def _vm():
    return pl.BlockSpec(memory_space=pltpu.VMEM)


def _sds(shape, dt=F32):
    return jax.ShapeDtypeStruct(shape, dt)


def _iota(shape, dim):
    return lax.broadcasted_iota(jnp.int32, shape, dim)


def _silu(x):
    return x * jax.nn.sigmoid(x)


def _softplus(x):
    return jnp.maximum(x, 0.0) + jnp.log1p(jnp.exp(-jnp.abs(x)))


def _normmod(x, g, sh, sc):
    r = lax.rsqrt(jnp.mean(x * x, axis=-1, keepdims=True) + EPS)
    return (x * r * g) * (1.0 + sc) + sh


def _rope(x, cos, sin, rm):
    return x * cos + _exact(x, rm) * sin


def _swap12(x):
    lane = _iota(x.shape, 1)
    up, down = pltpu.roll(x, 192, 1), pltpu.roll(x, 64, 1)
    return jnp.where((lane >= 64) & (lane < 128), up, jnp.where((lane >= 128) & (lane < 192), down, x))


def _acc_init(first, refs):
    @pl.when(first)
    def _():
        for r in refs:
            r[...] = jnp.zeros_like(r)


def _stream(X, TR, nlt):
    if not isinstance(X, tuple):
        return (X,), [pl.BlockSpec((TR, D), lambda i: (i, 0))], lambda refs: refs[0][...]
    specs = [pl.BlockSpec((TR, D), lambda i: (jnp.minimum(i, nlt - 1), 0)), pl.BlockSpec((TR, D), lambda i: (0, 0))]
    return X, specs, lambda refs: jnp.where(pl.program_id(0) < nlt, refs[0][...], refs[1][...])


def in_fwd(X, g, sh, sc, W, cos, sin, rm, L, sends=()):
    T = L + LC
    TR = 256
    nlt = L // TR
    xs, xspecs, xread = _stream(X, TR, nlt)

    def body(*refs):
        (g_ref, sh_ref, sc_ref, w_ref, cos_ref, sin_ref, rm_ref,
         qa, qb, z, ka, va, kb, vb, xbc, dt, hout) = refs[len(xs):]
        h = _normmod(xread(refs), g_ref[...], sh_ref[0], sc_ref[0]).astype(MXU)
        hout[...] = h
        y = lax.dot_general(h, w_ref[...], (((1,), (1,)), ((), ())), preferred_element_type=F32)
        cs, sn, r = cos_ref[...], sin_ref[...], rm_ref[...]
        qa[...] = _rope(_swap12(y[:, C_QA:C_QB]), cs, sn, r).astype(MXU)
        qb[...] = y[:, C_QB:C_Z].astype(MXU)
        z[...] = y[:, C_Z:C_KA]
        ka[...] = _rope(y[:, C_KA:C_VA], cs[:, :128], sn[:, :128], r[:128, :128]).astype(MXU)
        va[...] = y[:, C_VA:C_KB].astype(MXU)
        kb[...] = y[:, C_KB:C_VB].astype(MXU)
        vb[...] = y[:, C_VB:C_XBC].astype(MXU)
        xbc[...] = y[:, C_XBC:C_DT]
        dt[...] = y[:, C_DT:C_DT + 128]

    row = lambda w: pl.BlockSpec((TR, w), lambda i: (i, 0))
    cls = pl.BlockSpec((1, 1, D), lambda i: (i // nlt, 0, 0))
    widths = [(256, MXU), (256, MXU), (512, F32), (128, MXU), (128, MXU), (256, MXU), (256, MXU), (1024, F32),
              (128, F32), (D, MXU)]
    return _pc(body, "in_fwd", [_sds((T, w), d) for w, d in widths], grid=(T // TR,),
               in_specs=xspecs + [pl.BlockSpec((1, D), lambda i: (0, 0)), cls, cls, _vm(), row(256), row(256), _vm()],
               out_specs=[row(w) for w, _ in widths], sends=sends, gather=True)(*xs, g, sh, sc, W, cos, sin, rm)


def in_bwd(X, g, sh, sc, W, cos, sin, rm, dxres, dqa, dqb, dz, dka, dva, dkb, dvb, dxbc, ddt2, L, latent_only):
    T = L + LC
    TR = 256
    nlt = L // TR
    xs, xspecs, xread = _stream(X, TR, nlt)

    def body(*refs):
        (g_ref, sh_ref, sc_ref, w_ref, cos_ref, sin_ref, rm_ref, dxres_ref, dqa_r, dqb_r, dz_r, dka_r,
         dva_r, dkb_r, dvb_r, dxbc_r, ddt0_r, ddt1_r, dx_o, dy_o, dg_o, dsh_o, dsc_o) = refs[len(xs):]
        i = pl.program_id(0)
        cs, sn, r = cos_ref[...], sin_ref[...], rm_ref[...]
        _, vq = jax.vjp(lambda t: _rope(t, cs, sn, r), dqa_r[...])
        _, vk = jax.vjp(lambda t: _rope(t, cs[:, :128], sn[:, :128], r[:128, :128]), dka_r[...])
        dyqa = _swap12(vq(dqa_r[...])[0])
        dyka, = vk(dka_r[...])
        ddt = ddt0_r[0] + ddt1_r[0]
        dy = jnp.concatenate([dyqa, dqb_r[...], dz_r[...], dyka, dva_r[...], dkb_r[...], dvb_r[...], dxbc_r[...],
                              ddt, jnp.zeros((TR, NP_IN - C_DT - 128), F32)], axis=1).astype(MXU)
        dy_o[...] = dy
        dh = jnp.dot(dy, w_ref[...], preferred_element_type=F32)
        _, vp = jax.vjp(_normmod, xread(refs), g_ref[...], sh_ref[0], sc_ref[0])
        dx, dg, dsh, dsc = vp(dh)
        if latent_only:
            @pl.when(i < nlt)
            def _():
                dx_o[...] = dx + dxres_ref[...]
        else:
            dx_o[...] = dx + dxres_ref[...]
        _acc_init(i == 0, [dg_o])
        _acc_init((i == 0) | (i == nlt), [dsh_o, dsc_o])
        dg_o[...] += dg
        dsh_o[0] += dsh
        dsc_o[0] += dsc

    row = lambda w: pl.BlockSpec((TR, w), lambda i: (i, 0))
    cls = pl.BlockSpec((1, 1, D), lambda i: (i // nlt, 0, 0))
    vec = pl.BlockSpec((1, D), lambda i: (0, 0))
    dts = lambda d: pl.BlockSpec((1, TR, 128), lambda i: (d, i, 0))
    dxs = pl.BlockSpec((TR, D), lambda i: (jnp.minimum(i, nlt - 1), 0)) if latent_only else row(D)
    return _pc(body, "in_bwd",
               [_sds((L if latent_only else T, D)), _sds((T, NP_IN), MXU), _sds((1, D)), _sds((2, 1, D)),
                _sds((2, 1, D))],
               grid=(T // TR,),
               in_specs=xspecs + [vec, cls, cls, _vm(), row(256), row(256), _vm(), row(D), row(256), row(256),
                                  row(512), row(128), row(128), row(256), row(256), row(1024), dts(0), dts(1)],
               out_specs=[dxs, row(NP_IN), vec, cls, cls])(
        *xs, g, sh, sc, W, cos, sin, rm, dxres, dqa, dqb, dz, dka, dva, dkb, dvb, dxbc, ddt2, ddt2)


def tn_mm(A, G, bk, bn, out_dtype, ncol=None, col0=0):
    T, K = A.shape
    N = G.shape[1] if ncol is None else ncol
    first = col0 * (N // bn)
    bt = T
    nt = T // bt

    def body(a_ref, g_ref, o_ref, acc):
        t = pl.program_id(2)
        _acc_init(t == 0, [acc])
        acc[...] += lax.dot_general(a_ref[...], g_ref[...], (((0,), (0,)), ((), ())), preferred_element_type=F32)

        @pl.when(t == nt - 1)
        def _():
            o_ref[...] = acc[...].astype(out_dtype)

    return _pc(body, "tn_mm", _sds((K, N), out_dtype), grid=(K // bk, N // bn, nt),
               in_specs=[pl.BlockSpec((bt, bk), lambda k, n, t: (t, k)),
                         pl.BlockSpec((bt, bn), lambda k, n, t: (t, first + n))],
               out_specs=pl.BlockSpec((bk, bn), lambda k, n, t: (k, n)),
               scratch=[pltpu.VMEM((bk, bn), F32)])(A, G)


def _ssm_out(yf, yb, xs, z, dsk, gs):
    y = (yf + yb + dsk * xs) * _silu(z)
    r = lax.rsqrt(jnp.mean(y * y, axis=-1, keepdims=True) + EPS)
    return y * r * gs


def out_fwd(oa, ob, y2, act, z, dsk, gs, W, X, gate, L, sends=()):
    T = L + LC
    TR = 256
    nlt = L // TR
    xs, xspecs, xread = _stream(X, TR, nlt)

    def body(*refs):
        oa_r, ob_r, yf_r, yb_r, xs_r, z_r, dsk_r, gs_r, w_ref, gt_ref, x1_o, cat_o = refs[len(xs):]
        oc = _ssm_out(yf_r[0], yb_r[0], xs_r[...], z_r[...], dsk_r[...], gs_r[...])
        cat = jnp.concatenate([_swap12(oa_r[...]), ob_r[...], oc], axis=1).astype(MXU)
        cat_o[...] = cat
        x1_o[...] = xread(refs) + gt_ref[0] * jnp.dot(cat, w_ref[...], preferred_element_type=F32)

    row = lambda w: pl.BlockSpec((TR, w), lambda i: (i, 0))
    ys = lambda d: pl.BlockSpec((1, TR, 512), lambda i: (d, i, 0))
    cls = pl.BlockSpec((1, 1, D), lambda i: (i // nlt, 0, 0))
    v512 = pl.BlockSpec((1, 512), lambda i: (0, 0))
    return _pc(body, "out_fwd", [_sds((T, D)), _sds((T, D), MXU)], grid=(T // TR,),
               in_specs=xspecs + [row(256), row(256), ys(0), ys(1), row(512), row(512), v512, v512, _vm(), cls],
               out_specs=[row(D), row(D)], sends=sends, gather=True)(*xs, oa, ob, y2, y2, act, z, dsk, gs, W, gate)


def out_bwd(oa, ob, y2, act, z, dsk, gs, W, gate, dX1, L):
    T = dX1.shape[0]
    TR = 256
    nlt = L // TR

    def body(oa_r, ob_r, yf_r, yb_r, xs_r, z_r, dsk_r, gs_r, w_ref, gt_ref, dx1_r,
             doa_o, dob_o, dy_o, dxs_o, dz_o, dmix_o, ddsk_o, dgs_o, dgt_o):
        i = pl.program_id(0)
        w = w_ref[...]

        def f(oa_, ob_, yf, yb, xs, z_, dsk_, gs_, gt):
            oc = _ssm_out(yf, yb, xs, z_, dsk_, gs_)
            return gt * mmw(jnp.concatenate([oa_, ob_, oc], axis=1), w)

        _, vjp = jax.vjp(f, _swap12(oa_r[...]), ob_r[...], yf_r[0], yb_r[0], xs_r[...], z_r[...], dsk_r[...],
                         gs_r[...], gt_ref[0])
        dx1 = dx1_r[...]
        doa, dob, dyf, _, dxs, dz, ddsk, dgs, dgt = vjp(dx1)
        doa_o[...] = _swap12(doa)
        dob_o[...] = dob
        dy_o[...] = dyf
        dxs_o[...] = dxs
        dz_o[...] = dz
        dmix_o[...] = (gt_ref[0] * dx1).astype(MXU)
        _acc_init(i == 0, [ddsk_o, dgs_o])
        _acc_init((i == 0) | (i == nlt), [dgt_o])
        ddsk_o[...] += ddsk
        dgs_o[...] += dgs
        dgt_o[0] += dgt

    row = lambda w: pl.BlockSpec((TR, w), lambda i: (i, 0))
    ys = lambda d: pl.BlockSpec((1, TR, 512), lambda i: (d, i, 0))
    cls = pl.BlockSpec((1, 1, D), lambda i: (i // nlt, 0, 0))
    v512 = pl.BlockSpec((1, 512), lambda i: (0, 0))
    return _pc(body, "out_bwd",
               [_sds((T, 256)), _sds((T, 256)), _sds((T, 512)), _sds((T, 512)), _sds((T, 512)), _sds((T, D), MXU),
                _sds((1, 512)), _sds((1, 512)), _sds((2, 1, D))],
               grid=(T // TR,),
               in_specs=[row(256), row(256), ys(0), ys(1), row(512), row(512), v512, v512, _vm(), cls, row(D)],
               out_specs=[row(256), row(256), row(512), row(512), row(512), row(D), v512, v512, cls])(
        oa, ob, y2, y2, act, z, dsk, gs, W, gate, dX1)


def ffn_fwd(X, g, sh, sc, gate, Win, Wout, L, sends=()):
    T = X.shape[0]
    TR = 256
    nlt = L // TR

    def body(x_ref, g_ref, sh_ref, sc_ref, gt_ref, wi_ref, wo_ref, o_ref, f_ref):
        h = _normmod(x_ref[...], g_ref[...], sh_ref[0], sc_ref[0]).astype(MXU)
        nt = (((1,), (1,)), ((), ()))
        a = lax.dot_general(h, wi_ref[0:DFF, :], nt, preferred_element_type=F32)
        u = lax.dot_general(h, wi_ref[DFF:2 * DFF, :], nt, preferred_element_type=F32)
        act = (_silu(a) * u).astype(MXU)
        ff = jnp.dot(act, wo_ref[...], preferred_element_type=F32)
        f_ref[...] = ff
        o_ref[...] = x_ref[...] + gt_ref[0] * ff

    row = lambda w: pl.BlockSpec((TR, w), lambda i: (i, 0))
    cls = pl.BlockSpec((1, 1, D), lambda i: (i // nlt, 0, 0))
    vec = pl.BlockSpec((1, D), lambda i: (0, 0))
    return _pc(body, "ffn_fwd", [_sds((T, D)), _sds((T, D))], grid=(T // TR,),
               in_specs=[row(D), vec, cls, cls, cls, _vm(), _vm()], out_specs=[row(D), row(D)], sends=sends,
               gather=True)(X, g, sh, sc, gate, Win, Wout)


def ffn_bwd(X, g, sh, sc, gate, Win, Wout, FF, dX2, L, sends=(), nchunk=2):
    T = X.shape[0]
    TR = 256
    nlt = L // TR
    CH = DFF // nchunk

    def body(x_ref, g_ref, sh_ref, sc_ref, gt_ref, wi_ref, wo_ref, ff_r, dx2_r,
             dx_o, h_o, du_o, act_o, dout_o, dg_o, dsh_o, dsc_o, dgt_o):
        i = pl.program_id(0)
        h, vp = jax.vjp(_normmod, x_ref[...], g_ref[...], sh_ref[0], sc_ref[0])
        dx2 = dx2_r[...]
        dout = gt_ref[0] * dx2
        zero = jnp.zeros((TR, CH), F32)
        dh = jnp.zeros((TR, D), F32)
        for c in range(nchunk):
            lo, hi = c * CH, (c + 1) * CH
            wg, wu, wo = wi_ref[lo:hi, :], wi_ref[DFF + lo:DFF + hi, :], wo_ref[lo:hi, :]

            def f(h_, eg, eu):
                act = _silu(mmw_nt(h_, wg) + eg) * (mmw_nt(h_, wu) + eu)
                return mmw(act, wo), act

            _, vjp_c, act = jax.vjp(f, h, zero, zero, has_aux=True)
            dh_c, da, du = vjp_c(dout)
            dh = dh + dh_c
            du_o[:, lo:hi] = da.astype(MXU)
            du_o[:, DFF + lo:DFF + hi] = du.astype(MXU)
            act_o[:, lo:hi] = act.astype(MXU)
        dx, dg, dsh, dsc = vp(dh)
        dx_o[...] = dx + dx2
        h_o[...] = h.astype(MXU)
        dout_o[...] = dout.astype(MXU)
        _acc_init(i == 0, [dg_o])
        _acc_init((i == 0) | (i == nlt), [dsh_o, dsc_o, dgt_o])
        dg_o[...] += dg
        dsh_o[0] += dsh
        dsc_o[0] += dsc
        dgt_o[0] += jnp.sum(dx2 * ff_r[...], axis=0, keepdims=True)

    row = lambda w: pl.BlockSpec((TR, w), lambda i: (i, 0))
    cls = pl.BlockSpec((1, 1, D), lambda i: (i // nlt, 0, 0))
    vec = pl.BlockSpec((1, D), lambda i: (0, 0))
    return _pc(body, "ffn_bwd",
               [_sds((T, D)), _sds((T, D), MXU), _sds((T, 2 * DFF), MXU), _sds((T, DFF), MXU), _sds((T, D), MXU),
                _sds((1, D)), _sds((2, 1, D)), _sds((2, 1, D)), _sds((2, 1, D))],
               grid=(T // TR,),
               in_specs=[row(D), vec, cls, cls, cls, _vm(), _vm(), row(D), row(D)],
               out_specs=[row(D), row(D), row(2 * DFF), row(DFF), row(D), vec, cls, cls, cls], sends=sends)(
        X, g, sh, sc, gate, Win, Wout, FF, dX2)


def loss_head(X2, g, tgt, L):
    T = X2.shape[0]
    TR = 256
    nlt = L // TR

    def body(x_ref, g_ref, t_ref, loss_o, dx_o, dg_o):
        i = pl.program_id(0)
        _acc_init(i == 0, [loss_o, dg_o])

        @pl.when(i < nlt)
        def _():
            def f(x, g_):
                y = x * lax.rsqrt(jnp.mean(x * x, axis=-1, keepdims=True) + EPS) * g_
                return 0.5 * jnp.sum(jnp.mean(jnp.square(y - t_ref[...]), axis=-1, keepdims=True), axis=0,
                                     keepdims=True)

            val, vjp = jax.vjp(f, x_ref[...], g_ref[...])
            dx, dg = vjp(jnp.ones((1, 1), F32))
            dx_o[...] = dx
            loss_o[...] += jnp.broadcast_to(val, (8, 128))
            dg_o[...] += dg

        @pl.when(i >= nlt)
        def _():
            dx_o[...] = jnp.zeros_like(dx_o)

    row = pl.BlockSpec((TR, D), lambda i: (i, 0))
    vec = pl.BlockSpec((1, D), lambda i: (0, 0))
    return _pc(body, "loss_head", [_sds((8, 128)), _sds((T, D)), _sds((1, D))], grid=(T // TR,),
               in_specs=[row, vec, pl.BlockSpec((TR, D), lambda i: (jnp.minimum(i, nlt - 1), 0))],
               out_specs=[pl.BlockSpec((8, 128), lambda i: (0, 0)), row, vec])(X2, g, tgt)


def _stack_impl(q):
    lane = _iota(q.shape, 1)
    return jnp.concatenate([jnp.where(lane < HD, q, 0.0), jnp.where(lane >= HD, q, 0.0)], axis=0)


def _unstack_impl(o):
    M = o.shape[0] // 2
    return jnp.where(_iota((M, o.shape[1]), 1) < HD, o[:M], o[M:])


@jax.custom_vjp
def _stack(q):
    return _stack_impl(q)


_stack.defvjp(lambda q: (_stack_impl(q), None), lambda _, g: (_unstack_impl(g),))


@jax.custom_vjp
def _unstack(o):
    return _unstack_impl(o)


_unstack.defvjp(lambda o: (_unstack_impl(o), None), lambda _, g: (_stack_impl(g),))


def _softmax_av(q, ks, vs, biases, sink):
    q2 = _stack(q)
    ss = []
    for k, b in zip(ks, biases):
        s = mm_nt(q2, k) * (HD ** -0.5)
        ss.append(s if b is None else s + b)
    m = functools.reduce(jnp.maximum, [jnp.max(s, axis=1, keepdims=True) for s in ss])
    if sink is not None:
        m = jnp.maximum(m, sink)
    m = lax.stop_gradient(m)
    es = [jnp.exp(s - m) for s in ss]
    den = functools.reduce(lambda a, b_: a + b_, [jnp.sum(e, axis=1, keepdims=True) for e in es])
    if sink is not None:
        den = den + jnp.exp(sink - m)
    inv = 1.0 / den
    return _unstack(functools.reduce(lambda a, b_: a + b_, [mm(e * inv, v) for e, v in zip(es, vs)]))


def _sink_col(s0, s1, M):
    return jnp.concatenate([jnp.broadcast_to(jnp.mean(s0, axis=1, keepdims=True), (M, 1)),
                            jnp.broadcast_to(jnp.mean(s1, axis=1, keepdims=True), (M, 1))], axis=0)


def _stack4_impl(q):
    lane = _iota((q.shape[0], 128), 1)
    parts = []
    for p in range(2):
        qp = q[:, 128 * p:128 * (p + 1)]
        parts += [jnp.where(lane < HD, qp, 0.0), jnp.where(lane >= HD, qp, 0.0)]
    return jnp.concatenate(parts, axis=0)


def _unstack4_impl(o):
    M = o.shape[0] // 4
    lane = _iota((M, 128), 1)
    return jnp.concatenate([jnp.where(lane < HD, o[0:M], o[M:2 * M]),
                            jnp.where(lane < HD, o[2 * M:3 * M], o[3 * M:4 * M])], axis=1)


@jax.custom_vjp
def _stack4(q):
    return _stack4_impl(q)


_stack4.defvjp(lambda q: (_stack4_impl(q), None), lambda _, g: (_unstack4_impl(g),))


@jax.custom_vjp
def _unstack4(o):
    return _unstack4_impl(o)


_unstack4.defvjp(lambda o: (_unstack4_impl(o), None), lambda _, g: (_stack4_impl(g),))


WA_NB = 4


def _wa_blocks(qs, kws, vws, kx, vx, sks, n0, L):
    sc = HD ** -0.5
    sink = jnp.concatenate([jnp.broadcast_to(jnp.mean(s_, axis=1, keepdims=True), (Q, 1)) for s_ in sks], axis=0)
    bias = []
    for b_ in range(len(qs)):
        n = n0 + b_
        qpos = n * Q + (_iota((4 * Q, 3 * Q), 0) & (Q - 1))
        kpos = (n - 1) * Q + _iota((4 * Q, 3 * Q), 1)
        bias.append(jnp.where((jnp.abs(qpos - kpos) <= Q) & (kpos >= 0) & (kpos < L), 0.0, NEG))
    q4 = [_stack4(q) for q in qs]
    sl = [mm_nt(a, k) * sc + b_ for a, k, b_ in zip(q4, kws, bias)]
    sx = [mm_nt(a, kx) * sc for a in q4]
    m = [lax.stop_gradient(jnp.maximum(jnp.maximum(jnp.max(a, axis=1, keepdims=True),
                                                   jnp.max(b_, axis=1, keepdims=True)), sink))
         for a, b_ in zip(sl, sx)]
    el = [jnp.exp(a - c) for a, c in zip(sl, m)]
    ex = [jnp.exp(a - c) for a, c in zip(sx, m)]
    inv = [1.0 / (jnp.sum(a, axis=1, keepdims=True) + jnp.sum(b_, axis=1, keepdims=True) + jnp.exp(sink - c))
           for a, b_, c in zip(el, ex, m)]
    return [_unstack4(mm(a * i, v) + mm(b_ * i, vx)) for a, b_, i, v in zip(el, ex, inv, vws)]


def _wa_load(q_r, k_r, v_r, n0):
    f = lambda t: t.astype(F32)
    qs = [f(q_r[b_ * Q:(b_ + 1) * Q, :]) for b_ in range(WA_NB)]
    wins = [pl.ds(pl.multiple_of((n0 + b_) * Q, Q), 3 * Q) for b_ in range(WA_NB)]
    return qs, [f(k_r[w, :]) for w in wins], [f(v_r[w, :]) for w in wins], wins


def _wa_specs(L):
    nb = L // Q
    qs = pl.BlockSpec((WA_NB * Q, 256), lambda n: (n, 0))
    kfull = pl.BlockSpec((L + LC + Q, 128), lambda n: (0, 0))
    sks = pl.BlockSpec((2, 2, 1, 128), lambda n: (0, 0, 0, 0))
    return nb, qs, kfull, sks


def wa_fwd(QA, KA, VA, sinkp, L, sends=()):
    nb, qs, kfull, sks = _wa_specs(L)
    pad = lambda a: jnp.concatenate([jnp.zeros((Q, 128), a.dtype), a], axis=0)

    def body(q_r, k_r, v_r, sk_r, o_ref):
        n0 = pl.program_id(0) * WA_NB
        qs_, kws, vws, _ = _wa_load(q_r, k_r, v_r, n0)
        cx = pl.ds(Q + L, LC)
        outs = _wa_blocks(qs_, kws, vws, k_r[cx, :].astype(F32), v_r[cx, :].astype(F32),
                          [sk_r[0, 0], sk_r[0, 1], sk_r[1, 0], sk_r[1, 1]], n0, L)
        o_ref[...] = jnp.concatenate(outs, axis=0)

    return _pc(body, "wa_fwd", _sds((L, 256)), grid=(nb // WA_NB,), in_specs=[qs, kfull, kfull, sks], out_specs=qs,
               sends=sends, gather=True)(QA, pad(KA), pad(VA), sinkp)


def wa_bwd(QA, KA, VA, sinkp, dO, L, sends=()):
    nb, qs, kfull, sks = _wa_specs(L)
    pad = lambda a: jnp.concatenate([jnp.zeros((Q, 128), a.dtype), a], axis=0)

    def body(q_r, k_r, v_r, sk_r, do_r, dq_o, dk_o, dv_o, dsk_o):
        n0 = pl.program_id(0) * WA_NB
        _acc_init(n0 == 0, [dk_o, dv_o, dsk_o])
        qs_, kws, vws, wins = _wa_load(q_r, k_r, v_r, n0)
        cx = pl.ds(Q + L, LC)
        fn = lambda a, b, c, d, e, s_: _wa_blocks(a, b, c, d, e, s_, n0, L)
        _, vjp = jax.vjp(fn, qs_, kws, vws, k_r[cx, :].astype(F32), v_r[cx, :].astype(F32),
                         [sk_r[0, 0], sk_r[0, 1], sk_r[1, 0], sk_r[1, 1]])
        dqs, dkws, dvws, dkx, dvx, ds = vjp([do_r[b_ * Q:(b_ + 1) * Q, :] for b_ in range(WA_NB)])
        dq_o[...] = jnp.concatenate(dqs, axis=0)
        for w, dk, dv in zip(wins, dkws, dvws):
            dk_o[w, :] += dk
            dv_o[w, :] += dv
        dk_o[cx, :] += dkx
        dv_o[cx, :] += dvx
        for i_ in range(4):
            dsk_o[i_ // 2, i_ % 2] += ds[i_]

    return _pc(body, "wa_bwd", [_sds((L, 256)), _sds((L + LC + Q, 128)), _sds((L + LC + Q, 128)),
                                _sds((2, 2, 1, 128))],
               grid=(nb // WA_NB,), in_specs=[qs, kfull, kfull, sks, qs], out_specs=[qs, kfull, kfull, sks],
               sends=sends)(QA, pad(KA), pad(VA), sinkp, dO)


def _ctx_block(q, kx, vx, s0, s1):
    return _softmax_av(q, [kx], [vx], [None], _sink_col(s0, s1, LC))


def ctx_fwd(Qx, Kx, Vx, sinkp, shared, L):
    cq = pl.BlockSpec((LC, 128), lambda p: (L // LC, p))
    ck = pl.BlockSpec((LC, 128), lambda p: (L // LC, 0 if shared else p))
    sks = pl.BlockSpec((1, 2, 1, 128), lambda p: (p, 0, 0, 0))

    def body(q_r, k_r, v_r, sk_r, o_ref):
        f = lambda t: t[...].astype(F32)
        o_ref[...] = _ctx_block(f(q_r), f(k_r), f(v_r), sk_r[0, 0], sk_r[0, 1])

    return _pc(body, "ctx_fwd", _sds((LC, 256)), grid=(2,), in_specs=[cq, ck, ck, sks],
               out_specs=pl.BlockSpec((LC, 128), lambda p: (0, p)))(Qx, Kx, Vx, sinkp)


def ctx_bwd(Qx, Kx, Vx, sinkp, dO, shared, L):
    cq = pl.BlockSpec((LC, 128), lambda p: (L // LC, p))
    ck = pl.BlockSpec((LC, 128), lambda p: (L // LC, 0 if shared else p))
    sks = pl.BlockSpec((1, 2, 1, 128), lambda p: (p, 0, 0, 0))
    op = pl.BlockSpec((LC, 128), lambda p: (0, p))
    ok = pl.BlockSpec((LC, 128), lambda p: (0, 0 if shared else p))
    dos = pl.BlockSpec((LC, 128), lambda p: (L // LC, p))

    def body(q_r, k_r, v_r, sk_r, do_r, dq_o, dk_o, dv_o, dsk_o):
        p = pl.program_id(0)
        f = lambda t: t[...].astype(F32)
        _, vjp = jax.vjp(_ctx_block, f(q_r), f(k_r), f(v_r), sk_r[0, 0], sk_r[0, 1])
        dq, dk, dv, ds0, ds1 = vjp(do_r[...])
        dq_o[...] = dq
        _acc_init((p == 0) if shared else (p >= 0), [dk_o, dv_o])
        dk_o[...] += dk
        dv_o[...] += dv
        dsk_o[0, 0] = ds0
        dsk_o[0, 1] = ds1

    kw = 128 if shared else 256
    return _pc(body, "ctx_bwd", [_sds((LC, 256)), _sds((LC, kw)), _sds((LC, kw)), _sds((2, 2, 1, 128))],
               grid=(2,), in_specs=[cq, ck, ck, sks, dos], out_specs=[op, ok, ok, sks])(Qx, Kx, Vx, sinkp, dO)


def _na_rows(qs, kws, vws, kx, vx, bs):
    sc = HD ** -0.5
    q2 = [_stack(q) for q in qs]
    sl = [mm_nt(a, k) * sc + b for a, k, b in zip(q2, kws, bs)]
    sx = [mm_nt(a, kx) * sc for a in q2]
    m = [lax.stop_gradient(jnp.maximum(jnp.max(a, axis=1, keepdims=True), jnp.max(b, axis=1, keepdims=True)))
         for a, b in zip(sl, sx)]
    el = [jnp.exp(a - c) for a, c in zip(sl, m)]
    ex = [jnp.exp(a - c) for a, c in zip(sx, m)]
    inv = [1.0 / (jnp.sum(a, axis=1, keepdims=True) + jnp.sum(b, axis=1, keepdims=True)) for a, b in zip(el, ex)]
    o2 = [mm(a * i, v) + mm(b * i, vx) for a, b, i, v in zip(el, ex, inv, vws)]
    return [_unstack(o) for o in o2]


def _na_geom(rb, j, R):
    r = rb * 8 + j
    s = jnp.clip(r - 4, 0, R - 8)
    cls = jnp.where(r < 4, r, jnp.where(r > R - 4, r - (R - 8), 4))
    return pl.ds(pl.multiple_of(s * GW, GW), 8 * GW), cls


def _na_load(q_r, k_r, v_r, b_r, rb, R):
    geo = [_na_geom(rb, j, R) for j in range(8)]
    qs = [q_r[j * GW:(j + 1) * GW, :].astype(F32) for j in range(8)]
    kws = [k_r[win, :].astype(F32) for win, _ in geo]
    vws = [v_r[win, :].astype(F32) for win, _ in geo]
    bs = [jnp.concatenate([b_r[0, cls], b_r[1, cls]], axis=0) for _, cls in geo]
    return geo, qs, kws, vws, bs


def na_fwd(QB, KB, VB, biasd, L, sends=()):
    R = L // GW
    qs = pl.BlockSpec((8 * GW, 128), lambda p, rb: (rb, p))
    kfull = pl.BlockSpec((L, 128), lambda p, rb: (0, p))
    kctx = pl.BlockSpec((LC, 128), lambda p, rb: (L // LC, p))
    bs = pl.BlockSpec((2, 8, GW, 8 * GW), lambda p, rb: (p, 0, 0, 0))

    def body(q_r, k_r, v_r, kx_r, vx_r, b_r, o_ref):
        _, qs_, kws, vws, bs_ = _na_load(q_r, k_r, v_r, b_r, pl.program_id(1), R)
        outs = _na_rows(qs_, kws, vws, kx_r[...].astype(F32), vx_r[...].astype(F32), bs_)
        o_ref[...] = jnp.concatenate(outs, axis=0)

    return _pc(body, "na_fwd", _sds((L, 256)), grid=(2, R // 8), in_specs=[qs, kfull, kfull, kctx, kctx, bs],
               out_specs=qs, sends=sends, gather=True)(QB, KB, VB, KB, VB, biasd)


def na_bwd(QB, KB, VB, biasd, dO, L):
    R = L // GW
    qs = pl.BlockSpec((8 * GW, 128), lambda p, rb: (rb, p))
    kfull = pl.BlockSpec((L, 128), lambda p, rb: (0, p))
    kctx = pl.BlockSpec((LC, 128), lambda p, rb: (L // LC, p))
    bs = pl.BlockSpec((2, 8, GW, 8 * GW), lambda p, rb: (p, 0, 0, 0))
    oc = pl.BlockSpec((LC, 128), lambda p, rb: (0, p))

    def body(q_r, k_r, v_r, kx_r, vx_r, b_r, do_r, dq_o, dk_o, dv_o, dkx_o, dvx_o, db_o):
        rb = pl.program_id(1)
        _acc_init(rb == 0, [dk_o, dv_o, dkx_o, dvx_o, db_o])
        geo, qs_, kws, vws, bs_ = _na_load(q_r, k_r, v_r, b_r, rb, R)
        _, vjp = jax.vjp(_na_rows, qs_, kws, vws, kx_r[...].astype(F32), vx_r[...].astype(F32), bs_)
        dqs, dkws, dvws, dkx, dvx, dbs = vjp([do_r[j * GW:(j + 1) * GW, :] for j in range(8)])
        dq_o[...] = jnp.concatenate(dqs, axis=0)
        dkx_o[...] += dkx
        dvx_o[...] += dvx
        for j, (win, cls) in enumerate(geo):
            dk_o[win, :] += dkws[j]
            dv_o[win, :] += dvws[j]
            db_o[0, cls] += dbs[j][:GW]
            db_o[1, cls] += dbs[j][GW:]

    return _pc(body, "na_bwd",
               [_sds((L, 256)), _sds((L, 256)), _sds((L, 256)), _sds((LC, 256)), _sds((LC, 256)),
                _sds((4, 8, GW, 8 * GW))],
               grid=(2, R // 8), in_specs=[qs, kfull, kfull, kctx, kctx, bs, qs],
               out_specs=[qs, kfull, kfull, oc, oc, bs])(QB, KB, VB, KB, VB, biasd, dO)


def exact_mm_call(A, B):
    def body(a_ref, b_ref, o_ref):
        o_ref[...] = _exact(a_ref[...], b_ref[...])

    return _pc(body, "exact_mm", _sds((A.shape[0], B.shape[1])))(A, B)


def _conv_shift(x, d, L):
    T = x.shape[0]
    if d == 0:
        return x
    t = _iota(x.shape, 0)
    src = t + d
    ok = (src >= 0) & (src < T) & ((src >= L) == (t >= L))
    return jnp.where(ok, pltpu.roll(x, (-d) % T, 0), 0.0)


def conv_fwd(XBC, w8, b, L):
    T = XBC.shape[0]

    def body(x_ref, w_ref, b_ref, o_ref):
        x = x_ref[...]
        pre = b_ref[...] + functools.reduce(
            lambda a, c: a + c, [_conv_shift(x, k - 3, L) * w_ref[k:k + 1, :] for k in range(7)])
        o_ref[...] = _silu(pre)

    col = pl.BlockSpec((T, 128), lambda j: (0, j))
    return _pc(body, "conv_fwd", _sds((T, 1024)), grid=(8,),
               in_specs=[col, pl.BlockSpec((8, 128), lambda j: (0, j)), pl.BlockSpec((1, 128), lambda j: (0, j))],
               out_specs=col)(XBC, w8, b)


def conv_bwd(XBC, w8, b, dS, dxs_skip, L, sends=()):
    T = XBC.shape[0]

    def body(x_ref, w_ref, b_ref, d0_r, d1_r, dsk_r, dx_o, dw_o, db_o):
        j = pl.program_id(0)
        x = x_ref[...]
        xs = [_conv_shift(x, k - 3, L) for k in range(7)]
        pre = b_ref[...] + functools.reduce(lambda a, c: a + c, [xs[k] * w_ref[k:k + 1, :] for k in range(7)])
        _, vjp = jax.vjp(_silu, pre)
        dact = d0_r[0] + d1_r[0] + jnp.where(j < 4, dsk_r[...], 0.0)
        dpre, = vjp(dact)
        dx_o[...] = functools.reduce(
            lambda a, c: a + c, [_conv_shift(dpre, 3 - k, L) * w_ref[k:k + 1, :] for k in range(7)])
        dw_o[...] = jnp.concatenate([jnp.sum(dpre * xs[k], axis=0, keepdims=True) for k in range(7)]
                                    + [jnp.zeros((1, 128), F32)], axis=0)
        db_o[...] = jnp.sum(dpre, axis=0, keepdims=True)

    col = pl.BlockSpec((T, 128), lambda j: (0, j))
    w_s = pl.BlockSpec((8, 128), lambda j: (0, j))
    b_s = pl.BlockSpec((1, 128), lambda j: (0, j))
    ds = lambda d: pl.BlockSpec((1, T, 128), lambda j: (d, 0, j))
    return _pc(body, "conv_bwd", [_sds((T, 1024)), _sds((8, 1024)), _sds((1, 1024))], grid=(8,),
               in_specs=[col, w_s, b_s, ds(0), ds(1), pl.BlockSpec((T, 128), lambda j: (0, jnp.minimum(j, 3)))],
               out_specs=[col, w_s, b_s], sends=sends)(XBC, w8, b, dS, dS, dxs_skip)


def _ssd_chunk(xs, bs, cs, dtraw, dtb, alog, hs, tri, d):
    dt = _softplus(dtraw + dtb)
    a = dt * (-jnp.exp(alog))
    acum = _exact(tri, a)
    tot = jnp.sum(a, axis=0, keepdims=True)
    wcol = jnp.exp(tot - acum) * dt
    ea = jnp.exp(acum)
    cd = jnp.exp(tot)
    acum_t, dt_t = acum.T, dt.T
    lane = _iota((Q, 128), 1)
    srow = _iota((128, Q), 0)
    lane1 = _iota((1, 128), 1)
    prow = _iota((128, NSTATE), 0)
    mask = tri > 0.5
    cbs = [mm_nt(cs[g], bs[g]) for g in range(2)]
    ys, hn = [], []
    for j in range(4):
        g = j // 2
        x = xs[j]
        yi, st, eac, cdl = [], [], [], []
        for u in range(2):
            slot = d * 8 + 2 * j + u
            col = lambda m: jnp.sum(jnp.where(lane == slot, m, 0.0), axis=1, keepdims=True)
            rowv = lambda m: jnp.sum(jnp.where(srow == slot, m, 0.0), axis=0, keepdims=True)
            seg = col(acum) - rowv(acum_t)
            dcy = jnp.where(mask, jnp.exp(jnp.where(mask, seg, 0.0)), 0.0)
            yi.append(mm(cbs[g] * dcy * rowv(dt_t), x))
            st.append(mm_tn(x, bs[g] * col(wcol)))
            eac.append(col(ea))
            cdl.append(jnp.sum(jnp.where(lane1 == slot, cd, 0.0), axis=1, keepdims=True))
        yin = mm_nt(cs[g], hs[j])
        ys.append(jnp.where(lane < HD, yi[0] + yin * eac[0], yi[1] + yin * eac[1]))
        hn.append(hs[j] * jnp.where(prow < HD, cdl[0], cdl[1]) + jnp.where(prow < HD, st[0], st[1]))
    return ys, hn


def _ssd_chunk_idx(d, s, nlc, nch):
    return jnp.where(d == 0, (s + nlc) % nch, nch - 1 - s)


def ssd_fwd(ACT, DT, dtb, alog, tri2, L, sends=()):
    T = ACT.shape[0]
    nlc, nch = L // Q, T // Q

    def body(a_ref, dt_ref, dtb_ref, al_ref, tri_ref, y_o, hs_o, hst):
        d, s = pl.program_id(0), pl.program_id(1)
        _acc_init(s == 0, [hst])
        a = a_ref[...]
        xs = [a[:, 128 * j:128 * (j + 1)] for j in range(4)]
        bs = [a[:, 512 + 128 * g:640 + 128 * g] for g in range(2)]
        cs = [a[:, 768 + 128 * g:896 + 128 * g] for g in range(2)]
        hs = [hst[j] for j in range(4)]
        hs_o[0, 0] = hst[...]
        ys, hn = _ssd_chunk(xs, bs, cs, dt_ref[...], dtb_ref[...], al_ref[...], hs, tri_ref[0], d)
        y_o[0] = jnp.concatenate(ys, axis=1)
        for j in range(4):
            hst[j] = hn[j]

    ck = lambda w: pl.BlockSpec((Q, w), lambda d, s: (_ssd_chunk_idx(d, s, nlc, nch), 0))
    v128 = pl.BlockSpec((1, 128), lambda d, s: (0, 0))
    return _pc(body, "ssd_fwd", [_sds((2, T, 512)), _sds((2, nch, 4, 128, NSTATE))], grid=(2, nch),
               in_specs=[ck(1024), ck(128), v128, v128, pl.BlockSpec((1, Q, Q), lambda d, s: (d, 0, 0))],
               out_specs=[pl.BlockSpec((1, Q, 512), lambda d, s: (d, _ssd_chunk_idx(d, s, nlc, nch), 0)),
                          pl.BlockSpec((1, 1, 4, 128, NSTATE), lambda d, s: (d, s, 0, 0, 0))],
               scratch=[pltpu.VMEM((4, 128, NSTATE), F32)], sends=sends, gather=True)(ACT, DT, dtb, alog, tri2)


def ssd_bwd(ACT, DT, dtb, alog, tri2, HS, dY, L, sends=()):
    T = ACT.shape[0]
    nlc, nch = L // Q, T // Q

    def body(a_ref, dt_ref, dtb_ref, al_ref, tri_ref, hs_ref, dy_ref, da_o, ddt_o, ddtb_o, dal_o, dh):
        d, sr = pl.program_id(0), pl.program_id(1)
        _acc_init(sr == 0, [dh, ddtb_o, dal_o])
        a = a_ref[...]
        xs = [a[:, 128 * j:128 * (j + 1)] for j in range(4)]
        bs = [a[:, 512 + 128 * g:640 + 128 * g] for g in range(2)]
        cs = [a[:, 768 + 128 * g:896 + 128 * g] for g in range(2)]
        hs = [hs_ref[0, 0, j] for j in range(4)]
        tri = tri_ref[0]
        fn = lambda xs_, bs_, cs_, dtr, dtb_, al, hs_: _ssd_chunk(xs_, bs_, cs_, dtr, dtb_, al, hs_, tri, d)
        _, vjp = jax.vjp(fn, xs, bs, cs, dt_ref[...], dtb_ref[...], al_ref[...], hs)
        dy = dy_ref[...]
        dys = [dy[:, 128 * j:128 * (j + 1)] for j in range(4)]
        dxs, dbs, dcs, ddt, ddtb, dal, dhs = vjp((dys, [dh[j] for j in range(4)]))
        da_o[0] = jnp.concatenate(dxs + dbs + dcs, axis=1)
        ddt_o[0] = ddt
        ddtb_o[0] += ddtb
        dal_o[0] += dal
        for j in range(4):
            dh[j] = dhs[j]

    cidx = lambda d, sr: _ssd_chunk_idx(d, nch - 1 - sr, nlc, nch)
    ck = lambda w: pl.BlockSpec((Q, w), lambda d, sr: (cidx(d, sr), 0))
    v128 = pl.BlockSpec((1, 128), lambda d, sr: (0, 0))
    o128 = pl.BlockSpec((1, 1, 128), lambda d, sr: (d, 0, 0))
    return _pc(body, "ssd_bwd", [_sds((2, T, 1024)), _sds((2, T, 128)), _sds((2, 1, 128)), _sds((2, 1, 128))],
               grid=(2, nch),
               in_specs=[ck(1024), ck(128), v128, v128, pl.BlockSpec((1, Q, Q), lambda d, sr: (d, 0, 0)),
                         pl.BlockSpec((1, 1, 4, 128, NSTATE), lambda d, sr: (d, nch - 1 - sr, 0, 0, 0)), ck(512)],
               out_specs=[pl.BlockSpec((1, Q, 1024), lambda d, sr: (d, cidx(d, sr), 0)),
                          pl.BlockSpec((1, Q, 128), lambda d, sr: (d, cidx(d, sr), 0)), o128, o128],
               scratch=[pltpu.VMEM((4, 128, NSTATE), F32)], sends=sends)(ACT, DT, dtb, alog, tri2, HS, dY)


_PAIR_HEADS = np.array([[0, 2], [1, 3]])


def _tables(L):
    t = jnp.arange(L)
    inv = 10000.0 ** (-jnp.arange(16, dtype=F32) / 16)

    def half(pos):
        ang = pos.astype(F32)[:, None] * inv[None, :]
        return jnp.concatenate([ang, ang], axis=1)

    ang = jnp.tile(jnp.concatenate([half(t // GW), half(t % GW)], axis=1), (1, 4))
    cos = jnp.concatenate([jnp.cos(ang), jnp.ones((LC, 256), F32)], axis=0)
    sin = jnp.concatenate([jnp.sin(ang), jnp.zeros((LC, 256), F32)], axis=0)
    rm = np.zeros((256, 256), np.float32)
    for j in range(256):
        if j % 32 < 16:
            rm[j + 16, j] = -1.0
        else:
            rm[j - 16, j] = 1.0
    tri = np.tril(np.ones((Q, Q), np.float32))
    return cos, sin, jnp.asarray(rm), jnp.asarray(np.stack([tri, tri.T]))


def _na_index(R):
    rc = np.array([0, 1, 2, 3, 4, R - 3, R - 2, R - 1])
    dy = np.clip(rc - 4, 0, R - 8)[:, None] + np.arange(8)[None, :] - rc[:, None] + 7
    qc, cc = np.arange(GW)[:, None], np.arange(GW)[None, :]
    dx = np.clip(cc - qc, -15, 15) + 15
    cstart = np.clip(qc - 8, 0, GW - 16)
    cmask = (cc >= cstart) & (cc < cstart + 16)
    idx = dy[:, None, :, None] * 31 + dx[None, :, None, :]
    return idx.reshape(8, GW, 8 * GW), np.broadcast_to(cmask[None, :, None, :], idx.shape).reshape(8, GW, 8 * GW), \
        dy, dx, cmask


def _na_bias(rpb, R):
    _, cm, dy, _, _ = _na_index(R)
    rows = rpb[:, dy.reshape(-1), :].reshape(4, 8, 4, 2, 31)
    p2 = jnp.pad(jnp.pad(rows, ((0, 0),) * 4 + ((0, 33),)).reshape(4, 8, 4, 128), ((0, 0), (0, 0), (0, 4), (0, 0)))
    negmask = jnp.asarray(np.where(cm[0], 0.0, NEG).astype(np.float32))

    def body(p_ref, m_ref, o_ref):
        for c in range(8):
            tiles = [pltpu.roll(jnp.broadcast_to(p_ref[0, c, jp:jp + 1, :], (GW, 128)), 113, 1, stride=1,
                                stride_axis=0) for jp in range(4)]
            o_ref[0, c] = jnp.where(m_ref[...] < 0.0, NEG, jnp.concatenate(tiles, axis=1))

    return _pc(body, "na_bias", _sds((4, 8, GW, 8 * GW)), grid=(4,),
               in_specs=[pl.BlockSpec((1, 8, 8, 128), lambda h: (h, 0, 0, 0)),
                         pl.BlockSpec((GW, 8 * GW), lambda h: (0, 0))],
               out_specs=pl.BlockSpec((1, 8, GW, 8 * GW), lambda h: (h, 0, 0, 0)))(p2, negmask)


def _na_bias_grad(dbias, R):
    _, _, dy, dx, cmask = _na_index(R)
    e1 = np.zeros((GW * GW, 128), np.float32)
    e1[np.arange(GW * GW), dx.reshape(-1)] = cmask.reshape(-1)
    a1 = dbias.reshape(4, 8, GW, 8, GW).transpose(0, 1, 3, 2, 4).reshape(256, GW * GW)
    v = exact_mm_call(a1, jnp.asarray(e1))[:, :31].reshape(4, 64, 31)
    e2 = np.zeros((64, 128), np.float32)
    e2[np.arange(64), dy.reshape(-1)] = 1.0
    a2 = jnp.pad(v.transpose(0, 2, 1).reshape(124, 64), ((0, 4), (0, 0)))
    return exact_mm_call(a2, jnp.asarray(e2))[:124, :15].reshape(4, 31, 15).transpose(0, 2, 1)


def _lanes(v, n=128):
    v = v.reshape(1, -1)
    return jnp.pad(v, ((0, 0), (0, n - v.shape[1])))


def _cls2(a, b):
    return jnp.stack([a, b]).reshape(2, 1, D)


def _win_p(g):
    return jnp.concatenate([g.reshape(IN_COLS, D), jnp.zeros((NP_IN - IN_COLS, D), g.dtype)], axis=0)


def _layer_consts(p):
    sinkp = jnp.broadcast_to(p["wa_sink"][_PAIR_HEADS][:, :, None, None], (2, 2, 1, 128))
    return dict(
        sinkp=sinkp, nosink=jnp.full((2, 2, 1, 128), NEG, F32),
        w8=jnp.concatenate([p["ssm_conv_w"], jnp.zeros((1, 1024), F32)], axis=0),
        cb=p["ssm_conv_b"].reshape(1, 1024), dtb=_lanes(p["ssm_dt_bias"]), alog=_lanes(p["ssm_a_log"]),
        dsk=jnp.repeat(p["ssm_d"], HD).reshape(1, 512), gs=p["ssm_norm_g"].reshape(1, 512),
        gmix=p["g_mix"].reshape(1, D), gffn=p["g_ffn"].reshape(1, D))


def _mods(mod2):
    return [_cls2(mod2[0, D * k:D * (k + 1)], mod2[1, D * k:D * (k + 1)]) for k in range(6)]


def _layer_fwd(X, mod2, c, rpb, tabs, L, ctx_out, shards, nxt):
    cos, sin, rm, tri2 = tabs
    sh1, sc1, gt1, sh2, sc2, gt2 = _mods(mod2)
    biasd = _na_bias(rpb, L // GW)
    fi, fo, wo = shards
    fcut, ocut = 448, 224
    (qa, qb, z, ka, va, kb, vb, xbc, dt, h1), (gfo_a,) = in_fwd(X, c["gmix"], sh1, sc1, c["win"], cos, sin, rm, L,
                                                                sends=(fo[:ocut],))
    (oa,), (gfi_b,) = wa_fwd(qa, ka, va, c["sinkp"], L, sends=(fi[fcut:],))
    (ob,), (gwo,) = na_fwd(qb, kb, vb, biasd, L, sends=(wo,))
    c = dict(c, wout=gwo.reshape(D, D))
    if ctx_out:
        oa_c = ctx_fwd(qa, ka, va, c["sinkp"], True, L)
        ob_c = ctx_fwd(qb, kb, vb, c["nosink"], False, L)
    else:
        oa_c = ob_c = jnp.zeros((LC, 256), F32)
    oa = jnp.concatenate([oa, oa_c], axis=0)
    ob = jnp.concatenate([ob, ob_c], axis=0)
    act = conv_fwd(xbc, c["w8"], c["cb"], L)
    (y2, hs), (gfi_a,) = ssd_fwd(act, dt, c["dtb"], c["alog"], tri2, L, sends=(fi[:fcut],))
    (X1, cat), (gfo_b,) = out_fwd(oa, ob, y2, act, z, c["dsk"], c["gs"], c["wout"], X, gt1, L, sends=(fo[ocut:],))
    c = dict(c, wfi=jnp.concatenate([gfi_a, gfi_b], axis=1).reshape(2 * DFF, D),
             wfo=jnp.concatenate([gfo_a, gfo_b], axis=1).reshape(DFF, D))
    res = ffn_fwd(X1, c["gffn"], sh2, sc2, gt2, c["wfi"], c["wfo"], L, sends=nxt)
    (X2, ff), got = res if nxt else (res, ())
    saved = dict(X=X, X1=X1, ff=ff, qa=qa, qb=qb, z=z, ka=ka, va=va, kb=kb, vb=vb, xbc=xbc, dt=dt, h1=h1, oa=oa, ob=ob,
                 act=act, y2=y2, hs=hs, cat=cat, biasd=biasd)
    return X2, saved, c, got


def _row_blocks(gw):
    return gw.reshape(NDEV, gw.shape[0] // NDEV, gw.shape[1])


def _layer_bwd(dX2, s, mod2, c, tabs, L, ctx_out, carry):
    cos, sin, rm, tri2 = tabs
    sh1, sc1, gt1, sh2, sc2, gt2 = _mods(mod2)
    R = L // GW
    res = ffn_bwd(s["X1"], c["gffn"], sh2, sc2, gt2, c["wfi"], c["wfo"], s["ff"], dX2, L, sends=carry)
    (dX1, h2, dU, actf, dOut, dgffn, dsh2, dsc2, dgt2), got = res if carry else (res, ())
    g = {}
    gfi = _row_blocks(tn_mm(dU, h2, 512, 1024, MXU))
    gfo = _row_blocks(tn_mm(actf, dOut, 256, 1024, MXU))
    doa, dob, dy, dxs_skip, dz, dmix, ddsk, dgs, dgt1 = out_bwd(s["oa"], s["ob"], s["y2"], s["act"], s["z"], c["dsk"],
                                                                c["gs"], c["wout"], gt1, dX1, L)
    gout = _row_blocks(tn_mm(s["cat"], dmix, 512, 1024, MXU))
    (dS, ddt2, ddtb, dal), (g["w_ffn_in"],) = ssd_bwd(
        s["act"], s["dt"], c["dtb"], c["alog"], tri2, s["hs"], dy, L, sends=(gfi,))
    (dxbc, dw8, dcb), (g["w_ffn_out"],) = conv_bwd(s["xbc"], c["w8"], c["cb"], dS, dxs_skip, L, sends=(gfo,))
    (dqa, dka, dva, dska), (g["w_out"],) = wa_bwd(s["qa"], s["ka"], s["va"], c["sinkp"], doa, L, sends=(gout,))
    dka, dva = dka[Q:], dva[Q:]
    dqb, dkb, dvb, dkxb, dvxb, dbias = na_bwd(s["qb"], s["kb"], s["vb"], s["biasd"], dob, L)
    if ctx_out:
        dqa_c, dk1, dv1, dsk1 = ctx_bwd(s["qa"], s["ka"], s["va"], c["sinkp"], doa, True, L)
        dqb_c, dk2, dv2, _ = ctx_bwd(s["qb"], s["kb"], s["vb"], c["nosink"], dob, False, L)
        dka = jnp.concatenate([dka[:L], dka[L:] + dk1], axis=0)
        dva = jnp.concatenate([dva[:L], dva[L:] + dv1], axis=0)
        dska = dska + dsk1
        dkxb, dvxb = dkxb + dk2, dvxb + dv2
    else:
        dqa_c = dqb_c = jnp.zeros((LC, 256), F32)
    cat0 = lambda a, b: jnp.concatenate([a, b], axis=0)
    dX, dycat, dgmix, dsh1, dsc1 = in_bwd(
        s["X"], c["gmix"], sh1, sc1, c["win"], cos, sin, rm, dX1, cat0(dqa, dqa_c), cat0(dqb, dqb_c), dz,
        dka, dva, cat0(dkb, dkxb), cat0(dvb, dvxb), dxbc, ddt2, L,
        latent_only=ctx_out)
    if ctx_out:
        gin = [_row_blocks(tn_mm(dycat, s["h1"], 512, D // 2, MXU, ncol=D // 2, col0=k)[:IN_COLS]) for k in (0, 1)]
    else:
        gin = _row_blocks(tn_mm(dycat, s["h1"], 512, 1024, MXU)[:IN_COLS])
    g["g_mix"] = dgmix.reshape(D)
    g["g_ffn"] = dgffn.reshape(D)
    sk = jnp.sum(dska, axis=(2, 3))
    g["wa_sink"] = jnp.zeros((4,), F32).at[_PAIR_HEADS.reshape(-1)].set(sk.reshape(-1))
    g["na_rpb"] = _na_bias_grad(dbias, R)
    g["ssm_conv_w"] = dw8[:7]
    g["ssm_conv_b"] = dcb.reshape(1024)
    g["ssm_dt_bias"] = (ddtb[0] + ddtb[1])[0, :16].reshape(2, 8)
    g["ssm_a_log"] = (dal[0] + dal[1])[0, :16].reshape(2, 8)
    g["ssm_d"] = jnp.sum(ddsk.reshape(8, HD), axis=1)
    g["ssm_norm_g"] = dgs.reshape(512)
    dmod2 = jnp.concatenate([dsh1, dsc1, dgt1, dsh2, dsc2, dgt2], axis=2).reshape(2, 6 * D)
    return dX, g, dmod2, gin, got


def local_step(x, ctx, tgt, mods, layers, shards, g_final, L):
    tabs = _tables(L)
    X = (x, ctx)
    consts = [_layer_consts(p) for p in layers]
    saved = []
    got = (shards["w_in_first"],)
    for i in range(2):
        consts[i] = dict(consts[i], win=_win_p(got[0]))
        nxt = (shards["w_in"][1],) if i == 0 else ()
        X, s, consts[i], got = _layer_fwd(X, mods[i], consts[i], layers[i]["na_rpb"], tabs, L, i == 0,
                                          (shards["w_ffn_in"][i], shards["w_ffn_out"][i], shards["w_out"][i]), nxt)
        saved.append(s)
    loss8, dX, dgfin = loss_head(X, g_final.reshape(1, D), tgt, L)
    grads, dmods = [None, None], [None, None]
    dX, grads[1], dmods[1], gin1, _ = _layer_bwd(dX, saved[1], mods[1], consts[1], tabs, L, False, ())
    dX, grads[0], dmods[0], gin0, (grads[1]["w_in"],) = _layer_bwd(dX, saved[0], mods[0], consts[0], tabs, L, True,
                                                                   (gin1,))
    return loss8[0, 0], dX, grads, jnp.stack(dmods), dgfin.reshape(D), gin0


def _place():
    x, y, c = lax.axis_index("x"), lax.axis_index("y"), lax.axis_index("c")
    return x, y, c


def _slot(b):
    return 4 * b[0] + 2 * b[1] + b[2]


def _any():
    return pl.BlockSpec(memory_space=pl.ANY)


def all_gather(xs, name):
    n = len(xs)

    def body(*refs):
        x_refs, o_refs = refs[:n], refs[n:2 * n]
        send_sems, recv_sems, local_sems = refs[2 * n:]
        x, y, c = _place()
        me, sib = (x, y, c), (x, y, 1 - c)
        chips = [(1 - x, y), (x, 1 - y), (1 - x, 1 - y)]

        def copy(t, k, blk, to, src=None):
            dst = o_refs[t].at[_slot(blk)]
            return pltpu.make_async_remote_copy(
                src_ref=dst if src is None else src, dst_ref=dst, send_sem=send_sems.at[7 * t + k],
                recv_sem=recv_sems.at[7 * t + k], device_id=to, device_id_type=MESH_T)

        mine = [pltpu.make_async_copy(x_refs[t], o_refs[t].at[_slot(me)], local_sems.at[t]) for t in range(n)]
        for cp in mine:
            cp.start()
        first = []
        for t in range(n):
            first.append(copy(t, 0, me, sib, src=x_refs[t]))
            first += [copy(t, 1 + j, me, (*chip, c), src=x_refs[t]) for j, chip in enumerate(chips)]
        for cp in first:
            cp.start()
        passed = []
        for j, chip in enumerate(chips):
            for t in range(n):
                copy(t, 1 + j, (*chip, c), me).wait_recv()
                cp = copy(t, 4 + j, (*chip, c), sib)
                cp.start()
                passed.append(cp)
        for t in range(n):
            copy(t, 0, sib, me).wait_recv()
            for j, chip in enumerate(chips):
                copy(t, 4 + j, (*chip, 1 - c), me).wait_recv()
        for cp in first + passed:
            cp.wait_send()
        for cp in mine:
            cp.wait()

    return pl.pallas_call(
        body, name=name, out_shape=[_sds((NDEV,) + a.shape, a.dtype) for a in xs],
        in_specs=[_any()] * n, out_specs=[_any()] * n,
        scratch_shapes=[pltpu.SemaphoreType.DMA((7 * n,)), pltpu.SemaphoreType.DMA((7 * n,)),
                        pltpu.SemaphoreType.DMA((n,))],
        interpret=_INTERPRET)(*xs)


def all_to_all(xs, name):
    n = len(xs)

    def body(*refs):
        _a2a_start(refs[:n], refs[n:2 * n], *refs[2 * n:])
        _a2a_wait(refs[:n], refs[n:2 * n], *refs[2 * n:])

    return pl.pallas_call(
        body, name=name, out_shape=[_sds(a.shape, a.dtype) for a in xs],
        in_specs=[_any()] * n, out_specs=[_any()] * n, scratch_shapes=_a2a_sems(n), interpret=_INTERPRET)(*xs)


def _a2a_sems(n):
    return [pltpu.SemaphoreType.DMA((7 * n,)), pltpu.SemaphoreType.DMA((7 * n,)), pltpu.SemaphoreType.DMA((n,))]


def _a2a_copies(x_refs, o_refs, send_sems, recv_sems, local_sems):
    n = len(x_refs)
    x, y, c = _place()
    me = (x, y, c)
    flip = lambda v, b: (1 - v) if b else v
    peers = [(flip(x, k >> 2 & 1), flip(y, k >> 1 & 1), flip(c, k & 1)) for k in range(1, NDEV)]
    mine = [pltpu.make_async_copy(x_refs[t].at[_slot(me)], o_refs[t].at[_slot(me)], local_sems.at[t])
            for t in range(n)]

    def copy(t, k, src_slot, dst_slot, to):
        return pltpu.make_async_remote_copy(
            src_ref=x_refs[t].at[src_slot], dst_ref=o_refs[t].at[dst_slot], send_sem=send_sems.at[7 * t + k],
            recv_sem=recv_sems.at[7 * t + k], device_id=to, device_id_type=MESH_T)

    sends = [copy(t, k, _slot(p), _slot(me), p) for t in range(n) for k, p in enumerate(peers)]
    recvs = [copy(t, k, _slot(p), _slot(p), me) for t in range(n) for k, p in enumerate(peers)]
    return mine, sends, recvs


def _ag_copies(x_refs, o_refs, send_sems, recv_sems, local_sems):
    n = len(x_refs)
    x, y, c = _place()
    me = (x, y, c)
    flip = lambda v, b: (1 - v) if b else v
    peers = [(flip(x, k >> 2 & 1), flip(y, k >> 1 & 1), flip(c, k & 1)) for k in range(1, NDEV)]
    mine = [pltpu.make_async_copy(x_refs[t], o_refs[t].at[_slot(me)], local_sems.at[t]) for t in range(n)]

    def copy(t, k, dst_slot, to):
        return pltpu.make_async_remote_copy(
            src_ref=x_refs[t], dst_ref=o_refs[t].at[dst_slot], send_sem=send_sems.at[7 * t + k],
            recv_sem=recv_sems.at[7 * t + k], device_id=to, device_id_type=MESH_T)

    sends = [copy(t, k, _slot(me), p) for t in range(n) for k, p in enumerate(peers)]
    recvs = [copy(t, k, _slot(p), me) for t in range(n) for k, p in enumerate(peers)]
    return mine, sends, recvs


def _ag_start(x_refs, o_refs, send_sems, recv_sems, local_sems):
    mine, sends, _ = _ag_copies(x_refs, o_refs, send_sems, recv_sems, local_sems)
    for cp in mine + sends:
        cp.start()


def _ag_wait(x_refs, o_refs, send_sems, recv_sems, local_sems):
    mine, sends, recvs = _ag_copies(x_refs, o_refs, send_sems, recv_sems, local_sems)
    for cp in recvs:
        cp.wait_recv()
    for cp in sends:
        cp.wait_send()
    for cp in mine:
        cp.wait()


def _a2a_start(x_refs, o_refs, send_sems, recv_sems, local_sems):
    mine, sends, _ = _a2a_copies(x_refs, o_refs, send_sems, recv_sems, local_sems)
    for cp in mine + sends:
        cp.start()


def _a2a_wait(x_refs, o_refs, send_sems, recv_sems, local_sems):
    mine, sends, recvs = _a2a_copies(x_refs, o_refs, send_sems, recv_sems, local_sems)
    for cp in recvs:
        cp.wait_recv()
    for cp in sends:
        cp.wait_send()
    for cp in mine:
        cp.wait()


def adam_reduce(P, w, m, v, name, sends=()):
    n, R, C = P.shape
    br = R // 4 if R % 64 == 0 else R

    def body(p_ref, w_ref, m_ref, v_ref, g_o, d_o, m_o, v_o):
        g = p_ref[0].astype(F32)
        for k in range(1, n):
            g = g + p_ref[k].astype(F32)
        m1 = ADAM_B1 * m_ref[...] + (1.0 - ADAM_B1) * g
        v1 = ADAM_B2 * v_ref[...] + (1.0 - ADAM_B2) * jnp.square(g)
        m_hat = m1 / (1.0 - ADAM_B1 ** ADAM_STEP)
        v_hat = v1 / (1.0 - ADAM_B2 ** ADAM_STEP)
        g_o[...] = g
        d_o[...] = -ADAM_LR * (m_hat / (jnp.sqrt(v_hat) + ADAM_EPS) + ADAM_WD * w_ref[...])
        m_o[...] = m1
        v_o[...] = v1

    blk = pl.BlockSpec((br, C), lambda i: (i, 0))
    return _pc(body, name, [_sds((R, C))] * 4, grid=(R // br,),
               in_specs=[pl.BlockSpec((n, br, C), lambda i: (0, i, 0)), blk, blk, blk], out_specs=[blk] * 4,
               sends=sends)(P, w, m, v)


def adam_layers(P0, P1, w, m, v, name, sends=()):
    n, R, C = P0.shape
    br = R // 4 if R % 64 == 0 else R
    nb = R // br

    def body(p0_ref, p1_ref, w_ref, m_ref, v_ref, g_o, d_o, m_o, v_o):
        def total(p_ref):
            g = p_ref[0].astype(F32)
            for k in range(1, n):
                g = g + p_ref[k].astype(F32)
            return g

        g = jnp.where(pl.program_id(0) == 0, total(p0_ref), total(p1_ref))
        m1 = ADAM_B1 * m_ref[0] + (1.0 - ADAM_B1) * g
        v1 = ADAM_B2 * v_ref[0] + (1.0 - ADAM_B2) * jnp.square(g)
        m_hat = m1 / (1.0 - ADAM_B1 ** ADAM_STEP)
        v_hat = v1 / (1.0 - ADAM_B2 ** ADAM_STEP)
        g_o[0] = g
        d_o[0] = -ADAM_LR * (m_hat / (jnp.sqrt(v_hat) + ADAM_EPS) + ADAM_WD * w_ref[0])
        m_o[0] = m1
        v_o[0] = v1

    blk = pl.BlockSpec((1, br, C), lambda l, i: (l, i, 0))
    p0 = pl.BlockSpec((n, br, C), lambda l, i: (0, jnp.where(l == 0, i, nb - 1), 0))
    p1 = pl.BlockSpec((n, br, C), lambda l, i: (0, jnp.where(l == 1, i, 0), 0))
    return _pc(body, name, [_sds((2, R, C))] * 4, grid=(2, nb), in_specs=[p0, p1, blk, blk, blk],
               out_specs=[blk] * 4, sends=sends)(P0, P1, w, m, v)


def mod_fwd(scin, wmod, bcol):
    def body(s_ref, w_ref, b_ref, o_ref):
        o_ref[0] = mm(_silu(s_ref[...]), w_ref[0]) + b_ref[0]

    return _pc(body, "mod_fwd", _sds((2, 16, 768)), grid=(2,),
               in_specs=[pl.BlockSpec((16, D), lambda l: (0, 0)), pl.BlockSpec((1, D, 768), lambda l: (l, 0, 0)),
                         pl.BlockSpec((1, 1, 768), lambda l: (l, 0, 0))],
               out_specs=pl.BlockSpec((1, 16, 768), lambda l: (l, 0, 0)))(scin, wmod, bcol)


def mod_bwd(scin, wmod, G):
    def body(s_ref, w_ref, g_ref, dw_o, ds_o):
        _, vjp = jax.vjp(lambda s, w: mm(_silu(s), w), s_ref[...], w_ref[0])
        ds, dw = vjp(g_ref[0])
        dw_o[0] = dw
        _acc_init(pl.program_id(0) == 0, [ds_o])
        ds_o[...] += ds

    full = pl.BlockSpec((16, D), lambda l: (0, 0))
    wsp = pl.BlockSpec((1, D, 768), lambda l: (l, 0, 0))
    return _pc(body, "mod_bwd", [_sds((2, D, 768)), _sds((16, D))], grid=(2,),
               in_specs=[full, wsp, pl.BlockSpec((1, 16, 768), lambda l: (l, 0, 0))], out_specs=[wsp, full])(
        scin, wmod, G)


_SMALL = ["b_mod", "g_mix", "wa_sink", "na_rpb", "ssm_conv_w", "ssm_conv_b", "ssm_dt_bias", "ssm_a_log", "ssm_d",
          "ssm_norm_g", "g_ffn", "g_final", "dmod_s", "dmod_c"]


def _pack(parts):
    rows = []
    for a in parts:
        f = a.reshape(-1).astype(F32)
        rows.append(jnp.pad(f, (0, (-f.shape[0]) % 1024)).reshape(-1, 128))
    return jnp.concatenate(rows, axis=0)


def _unpack(packed, shapes):
    out, r = [], 0
    for s in shapes:
        nel = int(np.prod(s))
        nr = -(-nel // 1024) * 8
        out.append(packed[r:r + nr].reshape(-1)[:nel].reshape(s))
        r += nr
    return out


def kernel(x, c, ctx, c_ctx, w_mod, b_mod, g_mix, w_in, wa_sink, na_rpb, ssm_conv_w, ssm_conv_b, ssm_dt_bias, ssm_a_log, ssm_d, ssm_norm_g, w_out, g_ffn, w_ffn_in, w_ffn_out, g_final, loss_target, m_c_ctx, m_w_mod, m_b_mod, m_g_mix, m_w_in, m_wa_sink, m_na_rpb, m_ssm_conv_w, m_ssm_conv_b, m_ssm_dt_bias, m_ssm_a_log, m_ssm_d, m_ssm_norm_g, m_w_out, m_g_ffn, m_w_ffn_in, m_w_ffn_out, m_g_final, v_c_ctx, v_w_mod, v_b_mod, v_g_mix, v_w_in, v_wa_sink, v_na_rpb, v_ssm_conv_w, v_ssm_conv_b, v_ssm_dt_bias, v_ssm_a_log, v_ssm_d, v_ssm_norm_g, v_w_out, v_g_ffn, v_w_ffn_in, v_w_ffn_out, v_g_final):
    L = x.shape[1]
    px, py, pc = _place()
    me = 4 * px + 2 * py + pc
    W = dict(c_ctx=c_ctx, w_mod=w_mod, b_mod=b_mod, g_mix=g_mix, w_in=w_in, wa_sink=wa_sink, na_rpb=na_rpb,
             ssm_conv_w=ssm_conv_w, ssm_conv_b=ssm_conv_b, ssm_dt_bias=ssm_dt_bias, ssm_a_log=ssm_a_log, ssm_d=ssm_d,
             ssm_norm_g=ssm_norm_g, w_out=w_out, g_ffn=g_ffn, w_ffn_in=w_ffn_in, w_ffn_out=w_ffn_out, g_final=g_final)
    M = dict(c_ctx=m_c_ctx, w_mod=m_w_mod, b_mod=m_b_mod, g_mix=m_g_mix, w_in=m_w_in, wa_sink=m_wa_sink,
             na_rpb=m_na_rpb, ssm_conv_w=m_ssm_conv_w, ssm_conv_b=m_ssm_conv_b, ssm_dt_bias=m_ssm_dt_bias,
             ssm_a_log=m_ssm_a_log, ssm_d=m_ssm_d, ssm_norm_g=m_ssm_norm_g, w_out=m_w_out, g_ffn=m_g_ffn,
             w_ffn_in=m_w_ffn_in, w_ffn_out=m_w_ffn_out, g_final=m_g_final)
    V = dict(c_ctx=v_c_ctx, w_mod=v_w_mod, b_mod=v_b_mod, g_mix=v_g_mix, w_in=v_w_in, wa_sink=v_wa_sink,
             na_rpb=v_na_rpb, ssm_conv_w=v_ssm_conv_w, ssm_conv_b=v_ssm_conv_b, ssm_dt_bias=v_ssm_dt_bias,
             ssm_a_log=v_ssm_a_log, ssm_d=v_ssm_d, ssm_norm_g=v_ssm_norm_g, w_out=v_w_out, g_ffn=v_g_ffn,
             w_ffn_in=v_w_ffn_in, w_ffn_out=v_w_ffn_out, g_final=v_g_final)

    tr = lambda a: a.transpose(0, 2, 1)
    shards = dict(w_in=tr(w_in).astype(MXU), w_out=w_out.astype(MXU), w_ffn_in=tr(w_ffn_in).astype(MXU),
                  w_ffn_out=w_ffn_out.astype(MXU))
    c_all, conv_all, shards["w_in_first"] = all_gather([c, ssm_conv_w, shards["w_in"][0]], "gather_first")
    conv_f = conv_all.transpose(1, 2, 0, 3).reshape(2, 7, 1024)

    scin = jnp.concatenate([c_all.reshape(NDEV, D), c_ctx.reshape(1, D), jnp.zeros((7, D), F32)], axis=0)
    bcol = lax.dynamic_slice_in_dim(b_mod, me * 768, 768, axis=1).reshape(2, 1, 768)
    mod_all, = all_gather([mod_fwd(scin, w_mod, bcol)], "gather_mod")
    mod_rows = mod_all.transpose(1, 2, 0, 3).reshape(2, 16, 6 * D)
    mods = jnp.stack([lax.dynamic_index_in_dim(mod_rows, me, axis=1, keepdims=False), mod_rows[:, 8]], axis=1)

    layers = [dict(g_mix=g_mix[i], wa_sink=wa_sink[i], na_rpb=na_rpb[i], ssm_conv_w=conv_f[i],
                   ssm_conv_b=ssm_conv_b[i], ssm_dt_bias=ssm_dt_bias[i], ssm_a_log=ssm_a_log[i], ssm_d=ssm_d[i],
                   ssm_norm_g=ssm_norm_g[i], g_ffn=g_ffn[i]) for i in range(2)]
    loss, dx, grads, dmods, dgfin, gin0 = local_step(x[0], ctx[0], loss_target[0], mods, layers, shards, g_final, L)
    loss = lax.psum(loss, ("x", "y", "c"))

    stk = lambda n: jnp.stack([grads[0][n], grads[1][n]])
    small = dict(b_mod=dmods[:, 0] + dmods[:, 1], g_final=dgfin, dmod_s=dmods[:, 0], dmod_c=dmods[:, 1])
    for nme in _SMALL:
        if nme not in small:
            small[nme] = stk(nme)
    shapes = [small[nme].shape for nme in _SMALL]
    zero_like = lambda nme: jnp.zeros(small[nme].shape, F32)
    own = lambda S, nme: S[nme] if (nme in S and S[nme].shape == small[nme].shape) else zero_like(nme)
    gath, = all_gather([_pack([small[nme] for nme in _SMALL])], "gather_grads")
    sm = adam_reduce(gath, _pack([own(W, nme) for nme in _SMALL]), _pack([own(M, nme) for nme in _SMALL]),
                     _pack([own(V, nme) for nme in _SMALL]), "adam_small")
    res = {nme: vals for nme, vals in zip(_SMALL, zip(*[_unpack(a, shapes) for a in sm]))}

    cols = lambda a: lax.dynamic_slice_in_dim(a, me * 768, 768, axis=-1)
    gparts = [_unpack(gath[d], shapes) for d in range(NDEV)]
    dmod_s_all = jnp.stack([gparts[d][_SMALL.index("dmod_s")] for d in range(NDEV)], axis=1)
    G = jnp.concatenate([cols(dmod_s_all), cols(res["dmod_c"][0])[:, None, :], jnp.zeros((2, 7, 768), F32)], axis=1)
    dwmod, dscin = mod_bwd(scin, w_mod, G)
    cc_g, = all_gather([dscin[8].reshape(8, 128)], "gather_cctx")
    out = {}
    out["c_ctx"] = [a.reshape(D) for a in adam_reduce(cc_g, c_ctx.reshape(8, 128), m_c_ctx.reshape(8, 128),
                                                      v_c_ctx.reshape(8, 128), "adam_cctx")]
    res_wmod, (got1,) = adam_reduce(dwmod.reshape(1, 2 * D, 768), w_mod.reshape(2 * D, 768),
                                    m_w_mod.reshape(2 * D, 768), v_w_mod.reshape(2 * D, 768), "adam_wmod",
                                    sends=(gin0[1],))
    out["w_mod"] = [a.reshape(2, D, 768) for a in res_wmod]
    gconv = lax.dynamic_slice_in_dim(res["ssm_conv_w"][0], me * 128, 128, axis=2)
    out["ssm_conv_w"] = [a.reshape(2, 7, 128) for a in adam_reduce(
        gconv.reshape(1, 14, 128), ssm_conv_w.reshape(14, 128), m_ssm_conv_w.reshape(14, 128),
        v_ssm_conv_w.reshape(14, 128), "adam_conv")]
    for nme in _SMALL:
        if nme not in ("ssm_conv_w", "dmod_s", "dmod_c"):
            out[nme] = list(res[nme])

    adam_big = lambda nme, t, **kw: adam_layers(grads[0][nme], grads[1][nme], t(W[nme]), t(M[nme]), t(V[nme]),
                                                "adam_" + nme, **kw)
    same = lambda a: a
    res_fi, (got0,) = adam_big("w_ffn_in", tr, sends=(gin0[0],))
    grads[0]["w_in"] = jnp.concatenate([got0, got1], axis=2)
    out["w_ffn_in"] = [tr(a) for a in res_fi]
    out["w_ffn_out"] = list(adam_big("w_ffn_out", same))
    out["w_out"] = list(adam_big("w_out", same))
    out["w_in"] = [tr(a) for a in adam_big("w_in", tr)]
    order = ["c_ctx", "w_mod", "b_mod", "g_mix", "w_in", "wa_sink", "na_rpb", "ssm_conv_w", "ssm_conv_b",
             "ssm_dt_bias", "ssm_a_log", "ssm_d", "ssm_norm_g", "w_out", "g_ffn", "w_ffn_in", "w_ffn_out", "g_final"]
    return (loss, dx.reshape(1, L, D), *[out[nme][0] for nme in order], *[out[nme][1] for nme in order],
            *[out[nme][2] for nme in order], *[out[nme][3] for nme in order])
```

```python
import functools
import math

import numpy as np
import jax
import jax.numpy as jnp
from jax import lax
from jax.experimental import pallas as pl
from jax.experimental.pallas import tpu as pltpu

F32 = jnp.float32
MXU = jnp.bfloat16
_INTERPRET = False
VMEM_LIMIT = 60 * 1024 * 1024

D = 1024
LC = 256
GW = 64
HD = 64
EPS = 1e-6
NEG = -1e30
NDEV = 8
Q = 128
NSTATE = 128
DFF = 2816
IN_COLS = 2832
NP_IN = 3072
C_QA, C_QB, C_Z, C_KA, C_VA, C_KB, C_VB, C_XBC, C_DT = 0, 256, 512, 1024, 1152, 1280, 1536, 1792, 2816
ADAM_LR, ADAM_B1, ADAM_B2, ADAM_EPS, ADAM_WD, ADAM_STEP = 0.001, 0.9, 0.999, 1e-08, 0.01, 10
MESH_T = pl.DeviceIdType.MESH


def _dg(a, b, ca, cb):
    return lax.dot_general(a.astype(MXU), b.astype(MXU), (((ca,), (cb,)), ((), ())), preferred_element_type=F32)


@jax.custom_vjp
def mm(a, b):
    return _dg(a, b, 1, 0)


def _mm_f(a, b):
    return _dg(a, b, 1, 0), (a, b)


def _mm_b(res, g):
    a, b = res
    return _dg(g, b, 1, 1).astype(a.dtype), _dg(a, g, 0, 0).astype(b.dtype)


mm.defvjp(_mm_f, _mm_b)


@jax.custom_vjp
def mm_nt(a, b):
    return _dg(a, b, 1, 1)


def _mmnt_f(a, b):
    return _dg(a, b, 1, 1), (a, b)


def _mmnt_b(res, g):
    a, b = res
    return _dg(g, b, 1, 0).astype(a.dtype), _dg(g, a, 0, 0).astype(b.dtype)


mm_nt.defvjp(_mmnt_f, _mmnt_b)


@jax.custom_vjp
def mm_tn(a, b):
    return _dg(a, b, 0, 0)


def _mmtn_f(a, b):
    return _dg(a, b, 0, 0), (a, b)


def _mmtn_b(res, g):
    a, b = res
    return _dg(b, g, 1, 1).astype(a.dtype), _dg(a, g, 1, 0).astype(b.dtype)


mm_tn.defvjp(_mmtn_f, _mmtn_b)


@jax.custom_vjp
def mmw(a, w):
    return _dg(a, w, 1, 0)


mmw.defvjp(lambda a, w: (_dg(a, w, 1, 0), w), lambda w, g: (_dg(g, w, 1, 1), None))


@jax.custom_vjp
def mmw_nt(a, w):
    return _dg(a, w, 1, 1)


mmw_nt.defvjp(lambda a, w: (_dg(a, w, 1, 1), w), lambda w, g: (_dg(g, w, 1, 0), None))


def _exact(a, b):
    return lax.dot_general(a, b, (((1,), (0,)), ((), ())), precision=lax.Precision.HIGHEST,
                           preferred_element_type=F32)


def _pc(body, name, out_shape, grid=None, in_specs=None, out_specs=None, scratch=(), sends=(), gather=False):
    params = pltpu.CompilerParams(vmem_limit_bytes=VMEM_LIMIT)
    if sends and not isinstance(out_shape, (list, tuple)):
        out_shape, out_specs = [out_shape], [out_specs]
    start, wait = (_ag_start, _ag_wait) if gather else (_a2a_start, _a2a_wait)
    if not sends:
        kw = {}
        if grid is not None:
            kw = dict(grid=grid, in_specs=in_specs, out_specs=out_specs)
        elif in_specs is not None:
            kw = dict(in_specs=in_specs, out_specs=out_specs)
        return pl.pallas_call(body, name=name, out_shape=out_shape, scratch_shapes=list(scratch),
                              compiler_params=params, interpret=_INTERPRET, **kw)
    n, nin, nout, nscr = len(sends), len(in_specs), len(out_shape), len(scratch)

    def body2(*refs):
        cin, xs = refs[:nin], refs[nin:nin + n]
        couts, os_ = refs[nin + n:nin + n + nout], refs[nin + n + nout:nin + 2 * n + nout]
        cscr, sems = refs[nin + 2 * n + nout:nin + 2 * n + nout + nscr], refs[nin + 2 * n + nout + nscr:]
        ids = [pl.program_id(a) for a in range(len(grid))]
        first = functools.reduce(lambda a, b: a & b, [i == 0 for i in ids])
        last = functools.reduce(lambda a, b: a & b, [i == g - 1 for i, g in zip(ids, grid)])

        @pl.when(first)
        def _():
            start(xs, os_, *sems)

        body(*cin, *couts, *cscr)

        @pl.when(last)
        def _():
            wait(xs, os_, *sems)

    call = pl.pallas_call(
        body2, name=name,
        out_shape=list(out_shape) + [_sds(((NDEV,) if gather else ()) + a.shape, a.dtype) for a in sends],
        grid=grid, in_specs=list(in_specs) + [_any()] * n, out_specs=list(out_specs) + [_any()] * n,
        scratch_shapes=list(scratch) + _a2a_sems(n), compiler_params=params, interpret=_INTERPRET)

    def run(*args):
        res = call(*args, *sends)
        return res[:nout], res[nout:]

    return run


def _vm():
    return pl.BlockSpec(memory_space=pltpu.VMEM)


def _sds(shape, dt=F32):
    return jax.ShapeDtypeStruct(shape, dt)


def _iota(shape, dim):
    return lax.broadcasted_iota(jnp.int32, shape, dim)


def _silu(x):
    return x * jax.nn.sigmoid(x)


def _softplus(x):
    return jnp.maximum(x, 0.0) + jnp.log1p(jnp.exp(-jnp.abs(x)))


def _normmod(x, g, sh, sc):
    r = lax.rsqrt(jnp.mean(x * x, axis=-1, keepdims=True) + EPS)
    return (x * r * g) * (1.0 + sc) + sh


def _rope(x, cos, sin, rm):
    return x * cos + _exact(x, rm) * sin


def _swap12(x):
    lane = _iota(x.shape, 1)
    up, down = pltpu.roll(x, 192, 1), pltpu.roll(x, 64, 1)
    return jnp.where((lane >= 64) & (lane < 128), up, jnp.where((lane >= 128) & (lane < 192), down, x))


def _acc_init(first, refs):
    @pl.when(first)
    def _():
        for r in refs:
            r[...] = jnp.zeros_like(r)


def _stream(X, TR, nlt):
    if not isinstance(X, tuple):
        return (X,), [pl.BlockSpec((TR, D), lambda i: (i, 0))], lambda refs: refs[0][...]
    specs = [pl.BlockSpec((TR, D), lambda i: (jnp.minimum(i, nlt - 1), 0)), pl.BlockSpec((TR, D), lambda i: (0, 0))]
    return X, specs, lambda refs: jnp.where(pl.program_id(0) < nlt, refs[0][...], refs[1][...])


def in_fwd(X, g, sh, sc, W, cos, sin, rm, L, sends=()):
    T = L + LC
    TR = 256
    nlt = L // TR
    xs, xspecs, xread = _stream(X, TR, nlt)

    def body(*refs):
        (g_ref, sh_ref, sc_ref, w_ref, cos_ref, sin_ref, rm_ref,
         qa, qb, z, ka, va, kb, vb, xbc, dt, hout) = refs[len(xs):]
        h = _normmod(xread(refs), g_ref[...], sh_ref[0], sc_ref[0]).astype(MXU)
        hout[...] = h
        y = lax.dot_general(h, w_ref[...], (((1,), (1,)), ((), ())), preferred_element_type=F32)
        cs, sn, r = cos_ref[...], sin_ref[...], rm_ref[...]
        qa[...] = _rope(_swap12(y[:, C_QA:C_QB]), cs, sn, r).astype(MXU)
        qb[...] = y[:, C_QB:C_Z].astype(MXU)
        z[...] = y[:, C_Z:C_KA]
        ka[...] = _rope(y[:, C_KA:C_VA], cs[:, :128], sn[:, :128], r[:128, :128]).astype(MXU)
        va[...] = y[:, C_VA:C_KB].astype(MXU)
        kb[...] = y[:, C_KB:C_VB].astype(MXU)
        vb[...] = y[:, C_VB:C_XBC].astype(MXU)
        xbc[...] = y[:, C_XBC:C_DT]
        dt[...] = y[:, C_DT:C_DT + 128]

    row = lambda w: pl.BlockSpec((TR, w), lambda i: (i, 0))
    cls = pl.BlockSpec((1, 1, D), lambda i: (i // nlt, 0, 0))
    widths = [(256, MXU), (256, MXU), (512, F32), (128, MXU), (128, MXU), (256, MXU), (256, MXU), (1024, F32),
              (128, F32), (D, MXU)]
    return _pc(body, "in_fwd", [_sds((T, w), d) for w, d in widths], grid=(T // TR,),
               in_specs=xspecs + [pl.BlockSpec((1, D), lambda i: (0, 0)), cls, cls, _vm(), row(256), row(256), _vm()],
               out_specs=[row(w) for w, _ in widths], sends=sends, gather=True)(*xs, g, sh, sc, W, cos, sin, rm)


def in_bwd(X, g, sh, sc, W, cos, sin, rm, dxres, dqa, dqb, dz, dka, dva, dkb, dvb, dxbc, ddt2, L, latent_only):
    T = L + LC
    TR = 256
    nlt = L // TR
    xs, xspecs, xread = _stream(X, TR, nlt)

    def body(*refs):
        (g_ref, sh_ref, sc_ref, w_ref, cos_ref, sin_ref, rm_ref, dxres_ref, dqa_r, dqb_r, dz_r, dka_r,
         dva_r, dkb_r, dvb_r, dxbc_r, ddt0_r, ddt1_r, dx_o, dy_o, dg_o, dsh_o, dsc_o) = refs[len(xs):]
        i = pl.program_id(0)
        cs, sn, r = cos_ref[...], sin_ref[...], rm_ref[...]
        _, vq = jax.vjp(lambda t: _rope(t, cs, sn, r), dqa_r[...])
        _, vk = jax.vjp(lambda t: _rope(t, cs[:, :128], sn[:, :128], r[:128, :128]), dka_r[...])
        dyqa = _swap12(vq(dqa_r[...])[0])
        dyka, = vk(dka_r[...])
        ddt = ddt0_r[0] + ddt1_r[0]
        dy = jnp.concatenate([dyqa, dqb_r[...], dz_r[...], dyka, dva_r[...], dkb_r[...], dvb_r[...], dxbc_r[...],
                              ddt, jnp.zeros((TR, NP_IN - C_DT - 128), F32)], axis=1).astype(MXU)
        dy_o[...] = dy
        dh = jnp.dot(dy, w_ref[...], preferred_element_type=F32)
        _, vp = jax.vjp(_normmod, xread(refs), g_ref[...], sh_ref[0], sc_ref[0])
        dx, dg, dsh, dsc = vp(dh)
        if latent_only:
            @pl.when(i < nlt)
            def _():
                dx_o[...] = dx + dxres_ref[...]
        else:
            dx_o[...] = dx + dxres_ref[...]
        _acc_init(i == 0, [dg_o])
        _acc_init((i == 0) | (i == nlt), [dsh_o, dsc_o])
        dg_o[...] += dg
        dsh_o[0] += dsh
        dsc_o[0] += dsc

    row = lambda w: pl.BlockSpec((TR, w), lambda i: (i, 0))
    cls = pl.BlockSpec((1, 1, D), lambda i: (i // nlt, 0, 0))
    vec = pl.BlockSpec((1, D), lambda i: (0, 0))
    dts = lambda d: pl.BlockSpec((1, TR, 128), lambda i: (d, i, 0))
    dxs = pl.BlockSpec((TR, D), lambda i: (jnp.minimum(i, nlt - 1), 0)) if latent_only else row(D)
    return _pc(body, "in_bwd",
               [_sds((L if latent_only else T, D)), _sds((T, NP_IN), MXU), _sds((1, D)), _sds((2, 1, D)),
                _sds((2, 1, D))],
               grid=(T // TR,),
               in_specs=xspecs + [vec, cls, cls, _vm(), row(256), row(256), _vm(), row(D), row(256), row(256),
                                  row(512), row(128), row(128), row(256), row(256), row(1024), dts(0), dts(1)],
               out_specs=[dxs, row(NP_IN), vec, cls, cls])(
        *xs, g, sh, sc, W, cos, sin, rm, dxres, dqa, dqb, dz, dka, dva, dkb, dvb, dxbc, ddt2, ddt2)


def tn_mm(A, G, bk, bn, out_dtype, ncol=None, col0=0):
    T, K = A.shape
    N = G.shape[1] if ncol is None else ncol
    first = col0 * (N // bn)
    bt = T
    nt = T // bt

    def body(a_ref, g_ref, o_ref, acc):
        t = pl.program_id(2)
        _acc_init(t == 0, [acc])
        acc[...] += lax.dot_general(a_ref[...], g_ref[...], (((0,), (0,)), ((), ())), preferred_element_type=F32)

        @pl.when(t == nt - 1)
        def _():
            o_ref[...] = acc[...].astype(out_dtype)

    return _pc(body, "tn_mm", _sds((K, N), out_dtype), grid=(K // bk, N // bn, nt),
               in_specs=[pl.BlockSpec((bt, bk), lambda k, n, t: (t, k)),
                         pl.BlockSpec((bt, bn), lambda k, n, t: (t, first + n))],
               out_specs=pl.BlockSpec((bk, bn), lambda k, n, t: (k, n)),
               scratch=[pltpu.VMEM((bk, bn), F32)])(A, G)


def _ssm_out(yf, yb, xs, z, dsk, gs):
    y = (yf + yb + dsk * xs) * _silu(z)
    r = lax.rsqrt(jnp.mean(y * y, axis=-1, keepdims=True) + EPS)
    return y * r * gs


def out_fwd(oa, ob, y2, act, z, dsk, gs, W, X, gate, L, sends=()):
    T = L + LC
    TR = 256
    nlt = L // TR
    xs, xspecs, xread = _stream(X, TR, nlt)

    def body(*refs):
        oa_r, ob_r, yf_r, yb_r, xs_r, z_r, dsk_r, gs_r, w_ref, gt_ref, x1_o, cat_o = refs[len(xs):]
        oc = _ssm_out(yf_r[0], yb_r[0], xs_r[...], z_r[...], dsk_r[...], gs_r[...])
        cat = jnp.concatenate([_swap12(oa_r[...]), ob_r[...], oc], axis=1).astype(MXU)
        cat_o[...] = cat
        x1_o[...] = xread(refs) + gt_ref[0] * jnp.dot(cat, w_ref[...], preferred_element_type=F32)

    row = lambda w: pl.BlockSpec((TR, w), lambda i: (i, 0))
    ys = lambda d: pl.BlockSpec((1, TR, 512), lambda i: (d, i, 0))
    cls = pl.BlockSpec((1, 1, D), lambda i: (i // nlt, 0, 0))
    v512 = pl.BlockSpec((1, 512), lambda i: (0, 0))
    return _pc(body, "out_fwd", [_sds((T, D)), _sds((T, D), MXU)], grid=(T // TR,),
               in_specs=xspecs + [row(256), row(256), ys(0), ys(1), row(512), row(512), v512, v512, _vm(), cls],
               out_specs=[row(D), row(D)], sends=sends, gather=True)(*xs, oa, ob, y2, y2, act, z, dsk, gs, W, gate)


def out_bwd(oa, ob, y2, act, z, dsk, gs, W, gate, dX1, L):
    T = dX1.shape[0]
    TR = 256
    nlt = L // TR

    def body(oa_r, ob_r, yf_r, yb_r, xs_r, z_r, dsk_r, gs_r, w_ref, gt_ref, dx1_r,
             doa_o, dob_o, dy_o, dxs_o, dz_o, dmix_o, ddsk_o, dgs_o, dgt_o):
        i = pl.program_id(0)
        w = w_ref[...]

        def f(oa_, ob_, yf, yb, xs, z_, dsk_, gs_, gt):
            oc = _ssm_out(yf, yb, xs, z_, dsk_, gs_)
            return gt * mmw(jnp.concatenate([oa_, ob_, oc], axis=1), w)

        _, vjp = jax.vjp(f, _swap12(oa_r[...]), ob_r[...], yf_r[0], yb_r[0], xs_r[...], z_r[...], dsk_r[...],
                         gs_r[...], gt_ref[0])
        dx1 = dx1_r[...]
        doa, dob, dyf, _, dxs, dz, ddsk, dgs, dgt = vjp(dx1)
        doa_o[...] = _swap12(doa)
        dob_o[...] = dob
        dy_o[...] = dyf
        dxs_o[...] = dxs
        dz_o[...] = dz
        dmix_o[...] = (gt_ref[0] * dx1).astype(MXU)
        _acc_init(i == 0, [ddsk_o, dgs_o])
        _acc_init((i == 0) | (i == nlt), [dgt_o])
        ddsk_o[...] += ddsk
        dgs_o[...] += dgs
        dgt_o[0] += dgt

    row = lambda w: pl.BlockSpec((TR, w), lambda i: (i, 0))
    ys = lambda d: pl.BlockSpec((1, TR, 512), lambda i: (d, i, 0))
    cls = pl.BlockSpec((1, 1, D), lambda i: (i // nlt, 0, 0))
    v512 = pl.BlockSpec((1, 512), lambda i: (0, 0))
    return _pc(body, "out_bwd",
               [_sds((T, 256)), _sds((T, 256)), _sds((T, 512)), _sds((T, 512)), _sds((T, 512)), _sds((T, D), MXU),
                _sds((1, 512)), _sds((1, 512)), _sds((2, 1, D))],
               grid=(T // TR,),
               in_specs=[row(256), row(256), ys(0), ys(1), row(512), row(512), v512, v512, _vm(), cls, row(D)],
               out_specs=[row(256), row(256), row(512), row(512), row(512), row(D), v512, v512, cls])(
        oa, ob, y2, y2, act, z, dsk, gs, W, gate, dX1)


def ffn_fwd(X, g, sh, sc, gate, Win, Wout, L, sends=()):
    T = X.shape[0]
    TR = 256
    nlt = L // TR

    def body(x_ref, g_ref, sh_ref, sc_ref, gt_ref, wi_ref, wo_ref, o_ref, f_ref):
        h = _normmod(x_ref[...], g_ref[...], sh_ref[0], sc_ref[0]).astype(MXU)
        nt = (((1,), (1,)), ((), ()))
        a = lax.dot_general(h, wi_ref[0:DFF, :], nt, preferred_element_type=F32)
        u = lax.dot_general(h, wi_ref[DFF:2 * DFF, :], nt, preferred_element_type=F32)
        act = (_silu(a) * u).astype(MXU)
        ff = jnp.dot(act, wo_ref[...], preferred_element_type=F32)
        f_ref[...] = ff
        o_ref[...] = x_ref[...] + gt_ref[0] * ff

    row = lambda w: pl.BlockSpec((TR, w), lambda i: (i, 0))
    cls = pl.BlockSpec((1, 1, D), lambda i: (i // nlt, 0, 0))
    vec = pl.BlockSpec((1, D), lambda i: (0, 0))
    return _pc(body, "ffn_fwd", [_sds((T, D)), _sds((T, D))], grid=(T // TR,),
               in_specs=[row(D), vec, cls, cls, cls, _vm(), _vm()], out_specs=[row(D), row(D)], sends=sends,
               gather=True)(X, g, sh, sc, gate, Win, Wout)


def ffn_bwd(X, g, sh, sc, gate, Win, Wout, FF, dX2, L, sends=(), nchunk=2):
    T = X.shape[0]
    TR = 256
    nlt = L // TR
    CH = DFF // nchunk

    def body(x_ref, g_ref, sh_ref, sc_ref, gt_ref, wi_ref, wo_ref, ff_r, dx2_r,
             dx_o, h_o, du_o, act_o, dout_o, dg_o, dsh_o, dsc_o, dgt_o):
        i = pl.program_id(0)
        h, vp = jax.vjp(_normmod, x_ref[...], g_ref[...], sh_ref[0], sc_ref[0])
        dx2 = dx2_r[...]
        dout = gt_ref[0] * dx2
        zero = jnp.zeros((TR, CH), F32)
        dh = jnp.zeros((TR, D), F32)
        for c in range(nchunk):
            lo, hi = c * CH, (c + 1) * CH
            wg, wu, wo = wi_ref[lo:hi, :], wi_ref[DFF + lo:DFF + hi, :], wo_ref[lo:hi, :]

            def f(h_, eg, eu):
                act = _silu(mmw_nt(h_, wg) + eg) * (mmw_nt(h_, wu) + eu)
                return mmw(act, wo), act

            _, vjp_c, act = jax.vjp(f, h, zero, zero, has_aux=True)
            dh_c, da, du = vjp_c(dout)
            dh = dh + dh_c
            du_o[:, lo:hi] = da.astype(MXU)
            du_o[:, DFF + lo:DFF + hi] = du.astype(MXU)
            act_o[:, lo:hi] = act.astype(MXU)
        dx, dg, dsh, dsc = vp(dh)
        dx_o[...] = dx + dx2
        h_o[...] = h.astype(MXU)
        dout_o[...] = dout.astype(MXU)
        _acc_init(i == 0, [dg_o])
        _acc_init((i == 0) | (i == nlt), [dsh_o, dsc_o, dgt_o])
        dg_o[...] += dg
        dsh_o[0] += dsh
        dsc_o[0] += dsc
        dgt_o[0] += jnp.sum(dx2 * ff_r[...], axis=0, keepdims=True)

    row = lambda w: pl.BlockSpec((TR, w), lambda i: (i, 0))
    cls = pl.BlockSpec((1, 1, D), lambda i: (i // nlt, 0, 0))
    vec = pl.BlockSpec((1, D), lambda i: (0, 0))
    return _pc(body, "ffn_bwd",
               [_sds((T, D)), _sds((T, D), MXU), _sds((T, 2 * DFF), MXU), _sds((T, DFF), MXU), _sds((T, D), MXU),
                _sds((1, D)), _sds((2, 1, D)), _sds((2, 1, D)), _sds((2, 1, D))],
               grid=(T // TR,),
               in_specs=[row(D), vec, cls, cls, cls, _vm(), _vm(), row(D), row(D)],
               out_specs=[row(D), row(D), row(2 * DFF), row(DFF), row(D), vec, cls, cls, cls], sends=sends)(
        X, g, sh, sc, gate, Win, Wout, FF, dX2)


def loss_head(X2, g, tgt, L):
    T = X2.shape[0]
    TR = 256
    nlt = L // TR

    def body(x_ref, g_ref, t_ref, loss_o, dx_o, dg_o):
        i = pl.program_id(0)
        _acc_init(i == 0, [loss_o, dg_o])

        @pl.when(i < nlt)
        def _():
            def f(x, g_):
                y = x * lax.rsqrt(jnp.mean(x * x, axis=-1, keepdims=True) + EPS) * g_
                return 0.5 * jnp.sum(jnp.mean(jnp.square(y - t_ref[...]), axis=-1, keepdims=True), axis=0,
                                     keepdims=True)

            val, vjp = jax.vjp(f, x_ref[...], g_ref[...])
            dx, dg = vjp(jnp.ones((1, 1), F32))
            dx_o[...] = dx
            loss_o[...] += jnp.broadcast_to(val, (8, 128))
            dg_o[...] += dg

        @pl.when(i >= nlt)
        def _():
            dx_o[...] = jnp.zeros_like(dx_o)

    row = pl.BlockSpec((TR, D), lambda i: (i, 0))
    vec = pl.BlockSpec((1, D), lambda i: (0, 0))
    return _pc(body, "loss_head", [_sds((8, 128)), _sds((T, D)), _sds((1, D))], grid=(T // TR,),
               in_specs=[row, vec, pl.BlockSpec((TR, D), lambda i: (jnp.minimum(i, nlt - 1), 0))],
               out_specs=[pl.BlockSpec((8, 128), lambda i: (0, 0)), row, vec])(X2, g, tgt)


def _stack_impl(q):
    lane = _iota(q.shape, 1)
    return jnp.concatenate([jnp.where(lane < HD, q, 0.0), jnp.where(lane >= HD, q, 0.0)], axis=0)


def _unstack_impl(o):
    M = o.shape[0] // 2
    return jnp.where(_iota((M, o.shape[1]), 1) < HD, o[:M], o[M:])


@jax.custom_vjp
def _stack(q):
    return _stack_impl(q)


_stack.defvjp(lambda q: (_stack_impl(q), None), lambda _, g: (_unstack_impl(g),))


@jax.custom_vjp
def _unstack(o):
    return _unstack_impl(o)


_unstack.defvjp(lambda o: (_unstack_impl(o), None), lambda _, g: (_stack_impl(g),))


def _softmax_av(q, ks, vs, biases, sink):
    q2 = _stack(q)
    ss = []
    for k, b in zip(ks, biases):
        s = mm_nt(q2, k) * (HD ** -0.5)
        ss.append(s if b is None else s + b)
    m = functools.reduce(jnp.maximum, [jnp.max(s, axis=1, keepdims=True) for s in ss])
    if sink is not None:
        m = jnp.maximum(m, sink)
    m = lax.stop_gradient(m)
    es = [jnp.exp(s - m) for s in ss]
    den = functools.reduce(lambda a, b_: a + b_, [jnp.sum(e, axis=1, keepdims=True) for e in es])
    if sink is not None:
        den = den + jnp.exp(sink - m)
    inv = 1.0 / den
    return _unstack(functools.reduce(lambda a, b_: a + b_, [mm(e * inv, v) for e, v in zip(es, vs)]))


def _sink_col(s0, s1, M):
    return jnp.concatenate([jnp.broadcast_to(jnp.mean(s0, axis=1, keepdims=True), (M, 1)),
                            jnp.broadcast_to(jnp.mean(s1, axis=1, keepdims=True), (M, 1))], axis=0)


def _stack4_impl(q):
    lane = _iota((q.shape[0], 128), 1)
    parts = []
    for p in range(2):
        qp = q[:, 128 * p:128 * (p + 1)]
        parts += [jnp.where(lane < HD, qp, 0.0), jnp.where(lane >= HD, qp, 0.0)]
    return jnp.concatenate(parts, axis=0)


def _unstack4_impl(o):
    M = o.shape[0] // 4
    lane = _iota((M, 128), 1)
    return jnp.concatenate([jnp.where(lane < HD, o[0:M], o[M:2 * M]),
                            jnp.where(lane < HD, o[2 * M:3 * M], o[3 * M:4 * M])], axis=1)


@jax.custom_vjp
def _stack4(q):
    return _stack4_impl(q)


_stack4.defvjp(lambda q: (_stack4_impl(q), None), lambda _, g: (_unstack4_impl(g),))


@jax.custom_vjp
def _unstack4(o):
    return _unstack4_impl(o)


_unstack4.defvjp(lambda o: (_unstack4_impl(o), None), lambda _, g: (_stack4_impl(g),))


WA_NB = 4


def _wa_blocks(qs, kws, vws, kx, vx, sks, n0, L):
    sc = HD ** -0.5
    sink = jnp.concatenate([jnp.broadcast_to(jnp.mean(s_, axis=1, keepdims=True), (Q, 1)) for s_ in sks], axis=0)
    bias = []
    for b_ in range(len(qs)):
        n = n0 + b_
        qpos = n * Q + (_iota((4 * Q, 3 * Q), 0) & (Q - 1))
        kpos = (n - 1) * Q + _iota((4 * Q, 3 * Q), 1)
        bias.append(jnp.where((jnp.abs(qpos - kpos) <= Q) & (kpos >= 0) & (kpos < L), 0.0, NEG))
    q4 = [_stack4(q) for q in qs]
    sl = [mm_nt(a, k) * sc + b_ for a, k, b_ in zip(q4, kws, bias)]
    sx = [mm_nt(a, kx) * sc for a in q4]
    m = [lax.stop_gradient(jnp.maximum(jnp.maximum(jnp.max(a, axis=1, keepdims=True),
                                                   jnp.max(b_, axis=1, keepdims=True)), sink))
         for a, b_ in zip(sl, sx)]
    el = [jnp.exp(a - c) for a, c in zip(sl, m)]
    ex = [jnp.exp(a - c) for a, c in zip(sx, m)]
    inv = [1.0 / (jnp.sum(a, axis=1, keepdims=True) + jnp.sum(b_, axis=1, keepdims=True) + jnp.exp(sink - c))
           for a, b_, c in zip(el, ex, m)]
    return [_unstack4(mm(a * i, v) + mm(b_ * i, vx)) for a, b_, i, v in zip(el, ex, inv, vws)]


def _wa_load(q_r, k_r, v_r, n0):
    f = lambda t: t.astype(F32)
    qs = [f(q_r[b_ * Q:(b_ + 1) * Q, :]) for b_ in range(WA_NB)]
    wins = [pl.ds(pl.multiple_of((n0 + b_) * Q, Q), 3 * Q) for b_ in range(WA_NB)]
    return qs, [f(k_r[w, :]) for w in wins], [f(v_r[w, :]) for w in wins], wins


def _wa_specs(L):
    nb = L // Q
    qs = pl.BlockSpec((WA_NB * Q, 256), lambda n: (n, 0))
    kfull = pl.BlockSpec((L + LC + Q, 128), lambda n: (0, 0))
    sks = pl.BlockSpec((2, 2, 1, 128), lambda n: (0, 0, 0, 0))
    return nb, qs, kfull, sks


def wa_fwd(QA, KA, VA, sinkp, L, sends=()):
    nb, qs, kfull, sks = _wa_specs(L)
    pad = lambda a: jnp.concatenate([jnp.zeros((Q, 128), a.dtype), a], axis=0)

    def body(q_r, k_r, v_r, sk_r, o_ref):
        n0 = pl.program_id(0) * WA_NB
        qs_, kws, vws, _ = _wa_load(q_r, k_r, v_r, n0)
        cx = pl.ds(Q + L, LC)
        outs = _wa_blocks(qs_, kws, vws, k_r[cx, :].astype(F32), v_r[cx, :].astype(F32),
                          [sk_r[0, 0], sk_r[0, 1], sk_r[1, 0], sk_r[1, 1]], n0, L)
        o_ref[...] = jnp.concatenate(outs, axis=0)

    return _pc(body, "wa_fwd", _sds((L, 256)), grid=(nb // WA_NB,), in_specs=[qs, kfull, kfull, sks], out_specs=qs,
               sends=sends, gather=True)(QA, pad(KA), pad(VA), sinkp)


def wa_bwd(QA, KA, VA, sinkp, dO, L, sends=()):
    nb, qs, kfull, sks = _wa_specs(L)
    pad = lambda a: jnp.concatenate([jnp.zeros((Q, 128), a.dtype), a], axis=0)

    def body(q_r, k_r, v_r, sk_r, do_r, dq_o, dk_o, dv_o, dsk_o):
        n0 = pl.program_id(0) * WA_NB
        _acc_init(n0 == 0, [dk_o, dv_o, dsk_o])
        qs_, kws, vws, wins = _wa_load(q_r, k_r, v_r, n0)
        cx = pl.ds(Q + L, LC)
        fn = lambda a, b, c, d, e, s_: _wa_blocks(a, b, c, d, e, s_, n0, L)
        _, vjp = jax.vjp(fn, qs_, kws, vws, k_r[cx, :].astype(F32), v_r[cx, :].astype(F32),
                         [sk_r[0, 0], sk_r[0, 1], sk_r[1, 0], sk_r[1, 1]])
        dqs, dkws, dvws, dkx, dvx, ds = vjp([do_r[b_ * Q:(b_ + 1) * Q, :] for b_ in range(WA_NB)])
        dq_o[...] = jnp.concatenate(dqs, axis=0)
        for w, dk, dv in zip(wins, dkws, dvws):
            dk_o[w, :] += dk
            dv_o[w, :] += dv
        dk_o[cx, :] += dkx
        dv_o[cx, :] += dvx
        for i_ in range(4):
            dsk_o[i_ // 2, i_ % 2] += ds[i_]

    return _pc(body, "wa_bwd", [_sds((L, 256)), _sds((L + LC + Q, 128)), _sds((L + LC + Q, 128)),
                                _sds((2, 2, 1, 128))],
               grid=(nb // WA_NB,), in_specs=[qs, kfull, kfull, sks, qs], out_specs=[qs, kfull, kfull, sks],
               sends=sends)(QA, pad(KA), pad(VA), sinkp, dO)


def _ctx_block(q, kx, vx, s0, s1):
    return _softmax_av(q, [kx], [vx], [None], _sink_col(s0, s1, LC))


def ctx_fwd(Qx, Kx, Vx, sinkp, shared, L):
    cq = pl.BlockSpec((LC, 128), lambda p: (L // LC, p))
    ck = pl.BlockSpec((LC, 128), lambda p: (L // LC, 0 if shared else p))
    sks = pl.BlockSpec((1, 2, 1, 128), lambda p: (p, 0, 0, 0))

    def body(q_r, k_r, v_r, sk_r, o_ref):
        f = lambda t: t[...].astype(F32)
        o_ref[...] = _ctx_block(f(q_r), f(k_r), f(v_r), sk_r[0, 0], sk_r[0, 1])

    return _pc(body, "ctx_fwd", _sds((LC, 256)), grid=(2,), in_specs=[cq, ck, ck, sks],
               out_specs=pl.BlockSpec((LC, 128), lambda p: (0, p)))(Qx, Kx, Vx, sinkp)


def ctx_bwd(Qx, Kx, Vx, sinkp, dO, shared, L):
    cq = pl.BlockSpec((LC, 128), lambda p: (L // LC, p))
    ck = pl.BlockSpec((LC, 128), lambda p: (L // LC, 0 if shared else p))
    sks = pl.BlockSpec((1, 2, 1, 128), lambda p: (p, 0, 0, 0))
    op = pl.BlockSpec((LC, 128), lambda p: (0, p))
    ok = pl.BlockSpec((LC, 128), lambda p: (0, 0 if shared else p))
    dos = pl.BlockSpec((LC, 128), lambda p: (L // LC, p))

    def body(q_r, k_r, v_r, sk_r, do_r, dq_o, dk_o, dv_o, dsk_o):
        p = pl.program_id(0)
        f = lambda t: t[...].astype(F32)
        _, vjp = jax.vjp(_ctx_block, f(q_r), f(k_r), f(v_r), sk_r[0, 0], sk_r[0, 1])
        dq, dk, dv, ds0, ds1 = vjp(do_r[...])
        dq_o[...] = dq
        _acc_init((p == 0) if shared else (p >= 0), [dk_o, dv_o])
        dk_o[...] += dk
        dv_o[...] += dv
        dsk_o[0, 0] = ds0
        dsk_o[0, 1] = ds1

    kw = 128 if shared else 256
    return _pc(body, "ctx_bwd", [_sds((LC, 256)), _sds((LC, kw)), _sds((LC, kw)), _sds((2, 2, 1, 128))],
               grid=(2,), in_specs=[cq, ck, ck, sks, dos], out_specs=[op, ok, ok, sks])(Qx, Kx, Vx, sinkp, dO)


def _na_rows(qs, kws, vws, kx, vx, bs):
    sc = HD ** -0.5
    q2 = [_stack(q) for q in qs]
    sl = [mm_nt(a, k) * sc + b for a, k, b in zip(q2, kws, bs)]
    sx = [mm_nt(a, kx) * sc for a in q2]
    m = [lax.stop_gradient(jnp.maximum(jnp.max(a, axis=1, keepdims=True), jnp.max(b, axis=1, keepdims=True)))
         for a, b in zip(sl, sx)]
    el = [jnp.exp(a - c) for a, c in zip(sl, m)]
    ex = [jnp.exp(a - c) for a, c in zip(sx, m)]
    inv = [1.0 / (jnp.sum(a, axis=1, keepdims=True) + jnp.sum(b, axis=1, keepdims=True)) for a, b in zip(el, ex)]
    o2 = [mm(a * i, v) + mm(b * i, vx) for a, b, i, v in zip(el, ex, inv, vws)]
    return [_unstack(o) for o in o2]


def _na_geom(rb, j, R):
    r = rb * 8 + j
    s = jnp.clip(r - 4, 0, R - 8)
    cls = jnp.where(r < 4, r, jnp.where(r > R - 4, r - (R - 8), 4))
    return pl.ds(pl.multiple_of(s * GW, GW), 8 * GW), cls


def _na_load(q_r, k_r, v_r, b_r, rb, R):
    geo = [_na_geom(rb, j, R) for j in range(8)]
    qs = [q_r[j * GW:(j + 1) * GW, :].astype(F32) for j in range(8)]
    kws = [k_r[win, :].astype(F32) for win, _ in geo]
    vws = [v_r[win, :].astype(F32) for win, _ in geo]
    bs = [jnp.concatenate([b_r[0, cls], b_r[1, cls]], axis=0) for _, cls in geo]
    return geo, qs, kws, vws, bs


def na_fwd(QB, KB, VB, biasd, L, sends=()):
    R = L // GW
    qs = pl.BlockSpec((8 * GW, 128), lambda p, rb: (rb, p))
    kfull = pl.BlockSpec((L, 128), lambda p, rb: (0, p))
    kctx = pl.BlockSpec((LC, 128), lambda p, rb: (L // LC, p))
    bs = pl.BlockSpec((2, 8, GW, 8 * GW), lambda p, rb: (p, 0, 0, 0))

    def body(q_r, k_r, v_r, kx_r, vx_r, b_r, o_ref):
        _, qs_, kws, vws, bs_ = _na_load(q_r, k_r, v_r, b_r, pl.program_id(1), R)
        outs = _na_rows(qs_, kws, vws, kx_r[...].astype(F32), vx_r[...].astype(F32), bs_)
        o_ref[...] = jnp.concatenate(outs, axis=0)

    return _pc(body, "na_fwd", _sds((L, 256)), grid=(2, R // 8), in_specs=[qs, kfull, kfull, kctx, kctx, bs],
               out_specs=qs, sends=sends, gather=True)(QB, KB, VB, KB, VB, biasd)


def na_bwd(QB, KB, VB, biasd, dO, L):
    R = L // GW
    qs = pl.BlockSpec((8 * GW, 128), lambda p, rb: (rb, p))
    kfull = pl.BlockSpec((L, 128), lambda p, rb: (0, p))
    kctx = pl.BlockSpec((LC, 128), lambda p, rb: (L // LC, p))
    bs = pl.BlockSpec((2, 8, GW, 8 * GW), lambda p, rb: (p, 0, 0, 0))
    oc = pl.BlockSpec((LC, 128), lambda p, rb: (0, p))

    def body(q_r, k_r, v_r, kx_r, vx_r, b_r, do_r, dq_o, dk_o, dv_o, dkx_o, dvx_o, db_o):
        rb = pl.program_id(1)
        _acc_init(rb == 0, [dk_o, dv_o, dkx_o, dvx_o, db_o])
        geo, qs_, kws, vws, bs_ = _na_load(q_r, k_r, v_r, b_r, rb, R)
        _, vjp = jax.vjp(_na_rows, qs_, kws, vws, kx_r[...].astype(F32), vx_r[...].astype(F32), bs_)
        dqs, dkws, dvws, dkx, dvx, dbs = vjp([do_r[j * GW:(j + 1) * GW, :] for j in range(8)])
        dq_o[...] = jnp.concatenate(dqs, axis=0)
        dkx_o[...] += dkx
        dvx_o[...] += dvx
        for j, (win, cls) in enumerate(geo):
            dk_o[win, :] += dkws[j]
            dv_o[win, :] += dvws[j]
            db_o[0, cls] += dbs[j][:GW]
            db_o[1, cls] += dbs[j][GW:]

    return _pc(body, "na_bwd",
               [_sds((L, 256)), _sds((L, 256)), _sds((L, 256)), _sds((LC, 256)), _sds((LC, 256)),
                _sds((4, 8, GW, 8 * GW))],
               grid=(2, R // 8), in_specs=[qs, kfull, kfull, kctx, kctx, bs, qs],
               out_specs=[qs, kfull, kfull, oc, oc, bs])(QB, KB, VB, KB, VB, biasd, dO)


def exact_mm_call(A, B):
    def body(a_ref, b_ref, o_ref):
        o_ref[...] = _exact(a_ref[...], b_ref[...])

    return _pc(body, "exact_mm", _sds((A.shape[0], B.shape[1])))(A, B)


def _conv_shift(x, d, L):
    T = x.shape[0]
    if d == 0:
        return x
    t = _iota(x.shape, 0)
    src = t + d
    ok = (src >= 0) & (src < T) & ((src >= L) == (t >= L))
    return jnp.where(ok, pltpu.roll(x, (-d) % T, 0), 0.0)


def conv_fwd(XBC, w8, b, L):
    T = XBC.shape[0]

    def body(x_ref, w_ref, b_ref, o_ref):
        x = x_ref[...]
        pre = b_ref[...] + functools.reduce(
            lambda a, c: a + c, [_conv_shift(x, k - 3, L) * w_ref[k:k + 1, :] for k in range(7)])
        o_ref[...] = _silu(pre)

    col = pl.BlockSpec((T, 128), lambda j: (0, j))
    return _pc(body, "conv_fwd", _sds((T, 1024)), grid=(8,),
               in_specs=[col, pl.BlockSpec((8, 128), lambda j: (0, j)), pl.BlockSpec((1, 128), lambda j: (0, j))],
               out_specs=col)(XBC, w8, b)


def conv_bwd(XBC, w8, b, dS, dxs_skip, L, sends=()):
    T = XBC.shape[0]

    def body(x_ref, w_ref, b_ref, d0_r, d1_r, dsk_r, dx_o, dw_o, db_o):
        j = pl.program_id(0)
        x = x_ref[...]
        xs = [_conv_shift(x, k - 3, L) for k in range(7)]
        pre = b_ref[...] + functools.reduce(lambda a, c: a + c, [xs[k] * w_ref[k:k + 1, :] for k in range(7)])
        _, vjp = jax.vjp(_silu, pre)
        dact = d0_r[0] + d1_r[0] + jnp.where(j < 4, dsk_r[...], 0.0)
        dpre, = vjp(dact)
        dx_o[...] = functools.reduce(
            lambda a, c: a + c, [_conv_shift(dpre, 3 - k, L) * w_ref[k:k + 1, :] for k in range(7)])
        dw_o[...] = jnp.concatenate([jnp.sum(dpre * xs[k], axis=0, keepdims=True) for k in range(7)]
                                    + [jnp.zeros((1, 128), F32)], axis=0)
        db_o[...] = jnp.sum(dpre, axis=0, keepdims=True)

    col = pl.BlockSpec((T, 128), lambda j: (0, j))
    w_s = pl.BlockSpec((8, 128), lambda j: (0, j))
    b_s = pl.BlockSpec((1, 128), lambda j: (0, j))
    ds = lambda d: pl.BlockSpec((1, T, 128), lambda j: (d, 0, j))
    return _pc(body, "conv_bwd", [_sds((T, 1024)), _sds((8, 1024)), _sds((1, 1024))], grid=(8,),
               in_specs=[col, w_s, b_s, ds(0), ds(1), pl.BlockSpec((T, 128), lambda j: (0, jnp.minimum(j, 3)))],
               out_specs=[col, w_s, b_s], sends=sends)(XBC, w8, b, dS, dS, dxs_skip)


def _ssd_chunk(xs, bs, cs, dtraw, dtb, alog, hs, tri, d):
    dt = _softplus(dtraw + dtb)
    a = dt * (-jnp.exp(alog))
    acum = _exact(tri, a)
    tot = jnp.sum(a, axis=0, keepdims=True)
    wcol = jnp.exp(tot - acum) * dt
    ea = jnp.exp(acum)
    cd = jnp.exp(tot)
    acum_t, dt_t = acum.T, dt.T
    lane = _iota((Q, 128), 1)
    srow = _iota((128, Q), 0)
    lane1 = _iota((1, 128), 1)
    prow = _iota((128, NSTATE), 0)
    mask = tri > 0.5
    cbs = [mm_nt(cs[g], bs[g]) for g in range(2)]
    ys, hn = [], []
    for j in range(4):
        g = j // 2
        x = xs[j]
        yi, st, eac, cdl = [], [], [], []
        for u in range(2):
            slot = d * 8 + 2 * j + u
            col = lambda m: jnp.sum(jnp.where(lane == slot, m, 0.0), axis=1, keepdims=True)
            rowv = lambda m: jnp.sum(jnp.where(srow == slot, m, 0.0), axis=0, keepdims=True)
            seg = col(acum) - rowv(acum_t)
            dcy = jnp.where(mask, jnp.exp(jnp.where(mask, seg, 0.0)), 0.0)
            yi.append(mm(cbs[g] * dcy * rowv(dt_t), x))
            st.append(mm_tn(x, bs[g] * col(wcol)))
            eac.append(col(ea))
            cdl.append(jnp.sum(jnp.where(lane1 == slot, cd, 0.0), axis=1, keepdims=True))
        yin = mm_nt(cs[g], hs[j])
        ys.append(jnp.where(lane < HD, yi[0] + yin * eac[0], yi[1] + yin * eac[1]))
        hn.append(hs[j] * jnp.where(prow < HD, cdl[0], cdl[1]) + jnp.where(prow < HD, st[0], st[1]))
    return ys, hn


def _ssd_chunk_idx(d, s, nlc, nch):
    return jnp.where(d == 0, (s + nlc) % nch, nch - 1 - s)


def ssd_fwd(ACT, DT, dtb, alog, tri2, L, sends=()):
    T = ACT.shape[0]
    nlc, nch = L // Q, T // Q

    def body(a_ref, dt_ref, dtb_ref, al_ref, tri_ref, y_o, hs_o, hst):
        d, s = pl.program_id(0), pl.program_id(1)
        _acc_init(s == 0, [hst])
        a = a_ref[...]
        xs = [a[:, 128 * j:128 * (j + 1)] for j in range(4)]
        bs = [a[:, 512 + 128 * g:640 + 128 * g] for g in range(2)]
        cs = [a[:, 768 + 128 * g:896 + 128 * g] for g in range(2)]
        hs = [hst[j] for j in range(4)]
        hs_o[0, 0] = hst[...]
        ys, hn = _ssd_chunk(xs, bs, cs, dt_ref[...], dtb_ref[...], al_ref[...], hs, tri_ref[0], d)
        y_o[0] = jnp.concatenate(ys, axis=1)
        for j in range(4):
            hst[j] = hn[j]

    ck = lambda w: pl.BlockSpec((Q, w), lambda d, s: (_ssd_chunk_idx(d, s, nlc, nch), 0))
    v128 = pl.BlockSpec((1, 128), lambda d, s: (0, 0))
    return _pc(body, "ssd_fwd", [_sds((2, T, 512)), _sds((2, nch, 4, 128, NSTATE))], grid=(2, nch),
               in_specs=[ck(1024), ck(128), v128, v128, pl.BlockSpec((1, Q, Q), lambda d, s: (d, 0, 0))],
               out_specs=[pl.BlockSpec((1, Q, 512), lambda d, s: (d, _ssd_chunk_idx(d, s, nlc, nch), 0)),
                          pl.BlockSpec((1, 1, 4, 128, NSTATE), lambda d, s: (d, s, 0, 0, 0))],
               scratch=[pltpu.VMEM((4, 128, NSTATE), F32)], sends=sends, gather=True)(ACT, DT, dtb, alog, tri2)


def ssd_bwd(ACT, DT, dtb, alog, tri2, HS, dY, L, sends=()):
    T = ACT.shape[0]
    nlc, nch = L // Q, T // Q

    def body(a_ref, dt_ref, dtb_ref, al_ref, tri_ref, hs_ref, dy_ref, da_o, ddt_o, ddtb_o, dal_o, dh):
        d, sr = pl.program_id(0), pl.program_id(1)
        _acc_init(sr == 0, [dh, ddtb_o, dal_o])
        a = a_ref[...]
        xs = [a[:, 128 * j:128 * (j + 1)] for j in range(4)]
        bs = [a[:, 512 + 128 * g:640 + 128 * g] for g in range(2)]
        cs = [a[:, 768 + 128 * g:896 + 128 * g] for g in range(2)]
        hs = [hs_ref[0, 0, j] for j in range(4)]
        tri = tri_ref[0]
        fn = lambda xs_, bs_, cs_, dtr, dtb_, al, hs_: _ssd_chunk(xs_, bs_, cs_, dtr, dtb_, al, hs_, tri, d)
        _, vjp = jax.vjp(fn, xs, bs, cs, dt_ref[...], dtb_ref[...], al_ref[...], hs)
        dy = dy_ref[...]
        dys = [dy[:, 128 * j:128 * (j + 1)] for j in range(4)]
        dxs, dbs, dcs, ddt, ddtb, dal, dhs = vjp((dys, [dh[j] for j in range(4)]))
        da_o[0] = jnp.concatenate(dxs + dbs + dcs, axis=1)
        ddt_o[0] = ddt
        ddtb_o[0] += ddtb
        dal_o[0] += dal
        for j in range(4):
            dh[j] = dhs[j]

    cidx = lambda d, sr: _ssd_chunk_idx(d, nch - 1 - sr, nlc, nch)
    ck = lambda w: pl.BlockSpec((Q, w), lambda d, sr: (cidx(d, sr), 0))
    v128 = pl.BlockSpec((1, 128), lambda d, sr: (0, 0))
    o128 = pl.BlockSpec((1, 1, 128), lambda d, sr: (d, 0, 0))
    return _pc(body, "ssd_bwd", [_sds((2, T, 1024)), _sds((2, T, 128)), _sds((2, 1, 128)), _sds((2, 1, 128))],
               grid=(2, nch),
               in_specs=[ck(1024), ck(128), v128, v128, pl.BlockSpec((1, Q, Q), lambda d, sr: (d, 0, 0)),
                         pl.BlockSpec((1, 1, 4, 128, NSTATE), lambda d, sr: (d, nch - 1 - sr, 0, 0, 0)), ck(512)],
               out_specs=[pl.BlockSpec((1, Q, 1024), lambda d, sr: (d, cidx(d, sr), 0)),
                          pl.BlockSpec((1, Q, 128), lambda d, sr: (d, cidx(d, sr), 0)), o128, o128],
               scratch=[pltpu.VMEM((4, 128, NSTATE), F32)], sends=sends)(ACT, DT, dtb, alog, tri2, HS, dY)


_PAIR_HEADS = np.array([[0, 2], [1, 3]])


def _tables(L):
    t = jnp.arange(L)
    inv = 10000.0 ** (-jnp.arange(16, dtype=F32) / 16)

    def half(pos):
        ang = pos.astype(F32)[:, None] * inv[None, :]
        return jnp.concatenate([ang, ang], axis=1)

    ang = jnp.tile(jnp.concatenate([half(t // GW), half(t % GW)], axis=1), (1, 4))
    cos = jnp.concatenate([jnp.cos(ang), jnp.ones((LC, 256), F32)], axis=0)
    sin = jnp.concatenate([jnp.sin(ang), jnp.zeros((LC, 256), F32)], axis=0)
    rm = np.zeros((256, 256), np.float32)
    for j in range(256):
        if j % 32 < 16:
            rm[j + 16, j] = -1.0
        else:
            rm[j - 16, j] = 1.0
    tri = np.tril(np.ones((Q, Q), np.float32))
    return cos, sin, jnp.asarray(rm), jnp.asarray(np.stack([tri, tri.T]))


def _na_index(R):
    rc = np.array([0, 1, 2, 3, 4, R - 3, R - 2, R - 1])
    dy = np.clip(rc - 4, 0, R - 8)[:, None] + np.arange(8)[None, :] - rc[:, None] + 7
    qc, cc = np.arange(GW)[:, None], np.arange(GW)[None, :]
    dx = np.clip(cc - qc, -15, 15) + 15
    cstart = np.clip(qc - 8, 0, GW - 16)
    cmask = (cc >= cstart) & (cc < cstart + 16)
    idx = dy[:, None, :, None] * 31 + dx[None, :, None, :]
    return idx.reshape(8, GW, 8 * GW), np.broadcast_to(cmask[None, :, None, :], idx.shape).reshape(8, GW, 8 * GW), \
        dy, dx, cmask


def _na_bias(rpb, R):
    _, cm, dy, _, _ = _na_index(R)
    rows = rpb[:, dy.reshape(-1), :].reshape(4, 8, 4, 2, 31)
    p2 = jnp.pad(jnp.pad(rows, ((0, 0),) * 4 + ((0, 33),)).reshape(4, 8, 4, 128), ((0, 0), (0, 0), (0, 4), (0, 0)))
    negmask = jnp.asarray(np.where(cm[0], 0.0, NEG).astype(np.float32))

    def body(p_ref, m_ref, o_ref):
        for c in range(8):
            tiles = [pltpu.roll(jnp.broadcast_to(p_ref[0, c, jp:jp + 1, :], (GW, 128)), 113, 1, stride=1,
                                stride_axis=0) for jp in range(4)]
            o_ref[0, c] = jnp.where(m_ref[...] < 0.0, NEG, jnp.concatenate(tiles, axis=1))

    return _pc(body, "na_bias", _sds((4, 8, GW, 8 * GW)), grid=(4,),
               in_specs=[pl.BlockSpec((1, 8, 8, 128), lambda h: (h, 0, 0, 0)),
                         pl.BlockSpec((GW, 8 * GW), lambda h: (0, 0))],
               out_specs=pl.BlockSpec((1, 8, GW, 8 * GW), lambda h: (h, 0, 0, 0)))(p2, negmask)


def _na_bias_grad(dbias, R):
    _, _, dy, dx, cmask = _na_index(R)
    e1 = np.zeros((GW * GW, 128), np.float32)
    e1[np.arange(GW * GW), dx.reshape(-1)] = cmask.reshape(-1)
    a1 = dbias.reshape(4, 8, GW, 8, GW).transpose(0, 1, 3, 2, 4).reshape(256, GW * GW)
    v = exact_mm_call(a1, jnp.asarray(e1))[:, :31].reshape(4, 64, 31)
    e2 = np.zeros((64, 128), np.float32)
    e2[np.arange(64), dy.reshape(-1)] = 1.0
    a2 = jnp.pad(v.transpose(0, 2, 1).reshape(124, 64), ((0, 4), (0, 0)))
    return exact_mm_call(a2, jnp.asarray(e2))[:124, :15].reshape(4, 31, 15).transpose(0, 2, 1)


def _lanes(v, n=128):
    v = v.reshape(1, -1)
    return jnp.pad(v, ((0, 0), (0, n - v.shape[1])))


def _cls2(a, b):
    return jnp.stack([a, b]).reshape(2, 1, D)


def _win_p(g):
    return jnp.concatenate([g.reshape(IN_COLS, D), jnp.zeros((NP_IN - IN_COLS, D), g.dtype)], axis=0)


def _layer_consts(p):
    sinkp = jnp.broadcast_to(p["wa_sink"][_PAIR_HEADS][:, :, None, None], (2, 2, 1, 128))
    return dict(
        sinkp=sinkp, nosink=jnp.full((2, 2, 1, 128), NEG, F32),
        w8=jnp.concatenate([p["ssm_conv_w"], jnp.zeros((1, 1024), F32)], axis=0),
        cb=p["ssm_conv_b"].reshape(1, 1024), dtb=_lanes(p["ssm_dt_bias"]), alog=_lanes(p["ssm_a_log"]),
        dsk=jnp.repeat(p["ssm_d"], HD).reshape(1, 512), gs=p["ssm_norm_g"].reshape(1, 512),
        gmix=p["g_mix"].reshape(1, D), gffn=p["g_ffn"].reshape(1, D))


def _mods(mod2):
    return [_cls2(mod2[0, D * k:D * (k + 1)], mod2[1, D * k:D * (k + 1)]) for k in range(6)]


def _layer_fwd(X, mod2, c, rpb, tabs, L, ctx_out, shards, nxt):
    cos, sin, rm, tri2 = tabs
    sh1, sc1, gt1, sh2, sc2, gt2 = _mods(mod2)
    biasd = _na_bias(rpb, L // GW)
    fi, fo, wo = shards
    fcut, ocut = 448, 224
    (qa, qb, z, ka, va, kb, vb, xbc, dt, h1), (gfo_a,) = in_fwd(X, c["gmix"], sh1, sc1, c["win"], cos, sin, rm, L,
                                                                sends=(fo[:ocut],))
    (oa,), (gfi_b,) = wa_fwd(qa, ka, va, c["sinkp"], L, sends=(fi[fcut:],))
    (ob,), (gwo,) = na_fwd(qb, kb, vb, biasd, L, sends=(wo,))
    c = dict(c, wout=gwo.reshape(D, D))
    if ctx_out:
        oa_c = ctx_fwd(qa, ka, va, c["sinkp"], True, L)
        ob_c = ctx_fwd(qb, kb, vb, c["nosink"], False, L)
    else:
        oa_c = ob_c = jnp.zeros((LC, 256), F32)
    oa = jnp.concatenate([oa, oa_c], axis=0)
    ob = jnp.concatenate([ob, ob_c], axis=0)
    act = conv_fwd(xbc, c["w8"], c["cb"], L)
    (y2, hs), (gfi_a,) = ssd_fwd(act, dt, c["dtb"], c["alog"], tri2, L, sends=(fi[:fcut],))
    (X1, cat), (gfo_b,) = out_fwd(oa, ob, y2, act, z, c["dsk"], c["gs"], c["wout"], X, gt1, L, sends=(fo[ocut:],))
    c = dict(c, wfi=jnp.concatenate([gfi_a, gfi_b], axis=1).reshape(2 * DFF, D),
             wfo=jnp.concatenate([gfo_a, gfo_b], axis=1).reshape(DFF, D))
    res = ffn_fwd(X1, c["gffn"], sh2, sc2, gt2, c["wfi"], c["wfo"], L, sends=nxt)
    (X2, ff), got = res if nxt else (res, ())
    saved = dict(X=X, X1=X1, ff=ff, qa=qa, qb=qb, z=z, ka=ka, va=va, kb=kb, vb=vb, xbc=xbc, dt=dt, h1=h1, oa=oa, ob=ob,
                 act=act, y2=y2, hs=hs, cat=cat, biasd=biasd)
    return X2, saved, c, got


def _row_blocks(gw):
    return gw.reshape(NDEV, gw.shape[0] // NDEV, gw.shape[1])


def _layer_bwd(dX2, s, mod2, c, tabs, L, ctx_out, carry):
    cos, sin, rm, tri2 = tabs
    sh1, sc1, gt1, sh2, sc2, gt2 = _mods(mod2)
    R = L // GW
    res = ffn_bwd(s["X1"], c["gffn"], sh2, sc2, gt2, c["wfi"], c["wfo"], s["ff"], dX2, L, sends=carry)
    (dX1, h2, dU, actf, dOut, dgffn, dsh2, dsc2, dgt2), got = res if carry else (res, ())
    g = {}
    gfi = _row_blocks(tn_mm(dU, h2, 512, 1024, MXU))
    gfo = _row_blocks(tn_mm(actf, dOut, 256, 1024, MXU))
    doa, dob, dy, dxs_skip, dz, dmix, ddsk, dgs, dgt1 = out_bwd(s["oa"], s["ob"], s["y2"], s["act"], s["z"], c["dsk"],
                                                                c["gs"], c["wout"], gt1, dX1, L)
    gout = _row_blocks(tn_mm(s["cat"], dmix, 512, 1024, MXU))
    (dS, ddt2, ddtb, dal), (g["w_ffn_in"],) = ssd_bwd(
        s["act"], s["dt"], c["dtb"], c["alog"], tri2, s["hs"], dy, L, sends=(gfi,))
    (dxbc, dw8, dcb), (g["w_ffn_out"],) = conv_bwd(s["xbc"], c["w8"], c["cb"], dS, dxs_skip, L, sends=(gfo,))
    (dqa, dka, dva, dska), (g["w_out"],) = wa_bwd(s["qa"], s["ka"], s["va"], c["sinkp"], doa, L, sends=(gout,))
    dka, dva = dka[Q:], dva[Q:]
    dqb, dkb, dvb, dkxb, dvxb, dbias = na_bwd(s["qb"], s["kb"], s["vb"], s["biasd"], dob, L)
    if ctx_out:
        dqa_c, dk1, dv1, dsk1 = ctx_bwd(s["qa"], s["ka"], s["va"], c["sinkp"], doa, True, L)
        dqb_c, dk2, dv2, _ = ctx_bwd(s["qb"], s["kb"], s["vb"], c["nosink"], dob, False, L)
        dka = jnp.concatenate([dka[:L], dka[L:] + dk1], axis=0)
        dva = jnp.concatenate([dva[:L], dva[L:] + dv1], axis=0)
        dska = dska + dsk1
        dkxb, dvxb = dkxb + dk2, dvxb + dv2
    else:
        dqa_c = dqb_c = jnp.zeros((LC, 256), F32)
    cat0 = lambda a, b: jnp.concatenate([a, b], axis=0)
    dX, dycat, dgmix, dsh1, dsc1 = in_bwd(
        s["X"], c["gmix"], sh1, sc1, c["win"], cos, sin, rm, dX1, cat0(dqa, dqa_c), cat0(dqb, dqb_c), dz,
        dka, dva, cat0(dkb, dkxb), cat0(dvb, dvxb), dxbc, ddt2, L,
        latent_only=ctx_out)
    if ctx_out:
        gin = [_row_blocks(tn_mm(dycat, s["h1"], 512, D // 2, MXU, ncol=D // 2, col0=k)[:IN_COLS]) for k in (0, 1)]
    else:
        gin = _row_blocks(tn_mm(dycat, s["h1"], 512, 1024, MXU)[:IN_COLS])
    g["g_mix"] = dgmix.reshape(D)
    g["g_ffn"] = dgffn.reshape(D)
    sk = jnp.sum(dska, axis=(2, 3))
    g["wa_sink"] = jnp.zeros((4,), F32).at[_PAIR_HEADS.reshape(-1)].set(sk.reshape(-1))
    g["na_rpb"] = _na_bias_grad(dbias, R)
    g["ssm_conv_w"] = dw8[:7]
    g["ssm_conv_b"] = dcb.reshape(1024)
    g["ssm_dt_bias"] = (ddtb[0] + ddtb[1])[0, :16].reshape(2, 8)
    g["ssm_a_log"] = (dal[0] + dal[1])[0, :16].reshape(2, 8)
    g["ssm_d"] = jnp.sum(ddsk.reshape(8, HD), axis=1)
    g["ssm_norm_g"] = dgs.reshape(512)
    dmod2 = jnp.concatenate([dsh1, dsc1, dgt1, dsh2, dsc2, dgt2], axis=2).reshape(2, 6 * D)
    return dX, g, dmod2, gin, got


def local_step(x, ctx, tgt, mods, layers, shards, g_final, L):
    tabs = _tables(L)
    X = (x, ctx)
    consts = [_layer_consts(p) for p in layers]
    saved = []
    got = (shards["w_in_first"],)
    for i in range(2):
        consts[i] = dict(consts[i], win=_win_p(got[0]))
        nxt = (shards["w_in"][1],) if i == 0 else ()
        X, s, consts[i], got = _layer_fwd(X, mods[i], consts[i], layers[i]["na_rpb"], tabs, L, i == 0,
                                          (shards["w_ffn_in"][i], shards["w_ffn_out"][i], shards["w_out"][i]), nxt)
        saved.append(s)
    loss8, dX, dgfin = loss_head(X, g_final.reshape(1, D), tgt, L)
    grads, dmods = [None, None], [None, None]
    dX, grads[1], dmods[1], gin1, _ = _layer_bwd(dX, saved[1], mods[1], consts[1], tabs, L, False, ())
    dX, grads[0], dmods[0], gin0, (grads[1]["w_in"],) = _layer_bwd(dX, saved[0], mods[0], consts[0], tabs, L, True,
                                                                   (gin1,))
    return loss8[0, 0], dX, grads, jnp.stack(dmods), dgfin.reshape(D), gin0


def _place():
    x, y, c = lax.axis_index("x"), lax.axis_index("y"), lax.axis_index("c")
    return x, y, c


def _slot(b):
    return 4 * b[0] + 2 * b[1] + b[2]


def _any():
    return pl.BlockSpec(memory_space=pl.ANY)


def all_gather(xs, name):
    n = len(xs)

    def body(*refs):
        x_refs, o_refs = refs[:n], refs[n:2 * n]
        send_sems, recv_sems, local_sems = refs[2 * n:]
        x, y, c = _place()
        me, sib = (x, y, c), (x, y, 1 - c)
        chips = [(1 - x, y), (x, 1 - y), (1 - x, 1 - y)]

        def copy(t, k, blk, to, src=None):
            dst = o_refs[t].at[_slot(blk)]
            return pltpu.make_async_remote_copy(
                src_ref=dst if src is None else src, dst_ref=dst, send_sem=send_sems.at[7 * t + k],
                recv_sem=recv_sems.at[7 * t + k], device_id=to, device_id_type=MESH_T)

        mine = [pltpu.make_async_copy(x_refs[t], o_refs[t].at[_slot(me)], local_sems.at[t]) for t in range(n)]
        for cp in mine:
            cp.start()
        first = []
        for t in range(n):
            first.append(copy(t, 0, me, sib, src=x_refs[t]))
            first += [copy(t, 1 + j, me, (*chip, c), src=x_refs[t]) for j, chip in enumerate(chips)]
        for cp in first:
            cp.start()
        passed = []
        for j, chip in enumerate(chips):
            for t in range(n):
                copy(t, 1 + j, (*chip, c), me).wait_recv()
                cp = copy(t, 4 + j, (*chip, c), sib)
                cp.start()
                passed.append(cp)
        for t in range(n):
            copy(t, 0, sib, me).wait_recv()
            for j, chip in enumerate(chips):
                copy(t, 4 + j, (*chip, 1 - c), me).wait_recv()
        for cp in first + passed:
            cp.wait_send()
        for cp in mine:
            cp.wait()

    return pl.pallas_call(
        body, name=name, out_shape=[_sds((NDEV,) + a.shape, a.dtype) for a in xs],
        in_specs=[_any()] * n, out_specs=[_any()] * n,
        scratch_shapes=[pltpu.SemaphoreType.DMA((7 * n,)), pltpu.SemaphoreType.DMA((7 * n,)),
                        pltpu.SemaphoreType.DMA((n,))],
        interpret=_INTERPRET)(*xs)


def _a2a_sems(n):
    return [pltpu.SemaphoreType.DMA((7 * n,)), pltpu.SemaphoreType.DMA((7 * n,)), pltpu.SemaphoreType.DMA((n,))]


def _a2a_copies(x_refs, o_refs, send_sems, recv_sems, local_sems):
    n = len(x_refs)
    x, y, c = _place()
    me = (x, y, c)
    flip = lambda v, b: (1 - v) if b else v
    peers = [(flip(x, k >> 2 & 1), flip(y, k >> 1 & 1), flip(c, k & 1)) for k in range(1, NDEV)]
    mine = [pltpu.make_async_copy(x_refs[t].at[_slot(me)], o_refs[t].at[_slot(me)], local_sems.at[t])
            for t in range(n)]

    def copy(t, k, src_slot, dst_slot, to):
        return pltpu.make_async_remote_copy(
            src_ref=x_refs[t].at[src_slot], dst_ref=o_refs[t].at[dst_slot], send_sem=send_sems.at[7 * t + k],
            recv_sem=recv_sems.at[7 * t + k], device_id=to, device_id_type=MESH_T)

    sends = [copy(t, k, _slot(p), _slot(me), p) for t in range(n) for k, p in enumerate(peers)]
    recvs = [copy(t, k, _slot(p), _slot(p), me) for t in range(n) for k, p in enumerate(peers)]
    return mine, sends, recvs


def _ag_copies(x_refs, o_refs, send_sems, recv_sems, local_sems):
    n = len(x_refs)
    x, y, c = _place()
    me = (x, y, c)
    flip = lambda v, b: (1 - v) if b else v
    peers = [(flip(x, k >> 2 & 1), flip(y, k >> 1 & 1), flip(c, k & 1)) for k in range(1, NDEV)]
    mine = [pltpu.make_async_copy(x_refs[t], o_refs[t].at[_slot(me)], local_sems.at[t]) for t in range(n)]

    def copy(t, k, dst_slot, to):
        return pltpu.make_async_remote_copy(
            src_ref=x_refs[t], dst_ref=o_refs[t].at[dst_slot], send_sem=send_sems.at[7 * t + k],
            recv_sem=recv_sems.at[7 * t + k], device_id=to, device_id_type=MESH_T)

    sends = [copy(t, k, _slot(me), p) for t in range(n) for k, p in enumerate(peers)]
    recvs = [copy(t, k, _slot(p), me) for t in range(n) for k, p in enumerate(peers)]
    return mine, sends, recvs


def _ag_start(x_refs, o_refs, send_sems, recv_sems, local_sems):
    mine, sends, _ = _ag_copies(x_refs, o_refs, send_sems, recv_sems, local_sems)
    for cp in mine + sends:
        cp.start()


def _ag_wait(x_refs, o_refs, send_sems, recv_sems, local_sems):
    mine, sends, recvs = _ag_copies(x_refs, o_refs, send_sems, recv_sems, local_sems)
    for cp in recvs:
        cp.wait_recv()
    for cp in sends:
        cp.wait_send()
    for cp in mine:
        cp.wait()


def _a2a_start(x_refs, o_refs, send_sems, recv_sems, local_sems):
    mine, sends, _ = _a2a_copies(x_refs, o_refs, send_sems, recv_sems, local_sems)
    for cp in mine + sends:
        cp.start()


def _a2a_wait(x_refs, o_refs, send_sems, recv_sems, local_sems):
    mine, sends, recvs = _a2a_copies(x_refs, o_refs, send_sems, recv_sems, local_sems)
    for cp in recvs:
        cp.wait_recv()
    for cp in sends:
        cp.wait_send()
    for cp in mine:
        cp.wait()


def adam_reduce(P, w, m, v, name, sends=()):
    n, R, C = P.shape
    br = R // 4 if R % 64 == 0 else R

    def body(p_ref, w_ref, m_ref, v_ref, g_o, d_o, m_o, v_o):
        g = p_ref[0].astype(F32)
        for k in range(1, n):
            g = g + p_ref[k].astype(F32)
        m1 = ADAM_B1 * m_ref[...] + (1.0 - ADAM_B1) * g
        v1 = ADAM_B2 * v_ref[...] + (1.0 - ADAM_B2) * jnp.square(g)
        m_hat = m1 / (1.0 - ADAM_B1 ** ADAM_STEP)
        v_hat = v1 / (1.0 - ADAM_B2 ** ADAM_STEP)
        g_o[...] = g
        d_o[...] = -ADAM_LR * (m_hat / (jnp.sqrt(v_hat) + ADAM_EPS) + ADAM_WD * w_ref[...])
        m_o[...] = m1
        v_o[...] = v1

    blk = pl.BlockSpec((br, C), lambda i: (i, 0))
    return _pc(body, name, [_sds((R, C))] * 4, grid=(R // br,),
               in_specs=[pl.BlockSpec((n, br, C), lambda i: (0, i, 0)), blk, blk, blk], out_specs=[blk] * 4,
               sends=sends)(P, w, m, v)


def adam_layers(P0, P1, w, m, v, name, sends=()):
    n, R, C = P0.shape
    br = R // 4 if R % 64 == 0 else R
    nb = R // br

    def body(p0_ref, p1_ref, w_ref, m_ref, v_ref, g_o, d_o, m_o, v_o):
        def total(p_ref):
            g = p_ref[0].astype(F32)
            for k in range(1, n):
                g = g + p_ref[k].astype(F32)
            return g

        g = jnp.where(pl.program_id(0) == 0, total(p0_ref), total(p1_ref))
        m1 = ADAM_B1 * m_ref[0] + (1.0 - ADAM_B1) * g
        v1 = ADAM_B2 * v_ref[0] + (1.0 - ADAM_B2) * jnp.square(g)
        m_hat = m1 / (1.0 - ADAM_B1 ** ADAM_STEP)
        v_hat = v1 / (1.0 - ADAM_B2 ** ADAM_STEP)
        g_o[0] = g
        d_o[0] = -ADAM_LR * (m_hat / (jnp.sqrt(v_hat) + ADAM_EPS) + ADAM_WD * w_ref[0])
        m_o[0] = m1
        v_o[0] = v1

    blk = pl.BlockSpec((1, br, C), lambda l, i: (l, i, 0))
    p0 = pl.BlockSpec((n, br, C), lambda l, i: (0, jnp.where(l == 0, i, nb - 1), 0))
    p1 = pl.BlockSpec((n, br, C), lambda l, i: (0, jnp.where(l == 1, i, 0), 0))
    return _pc(body, name, [_sds((2, R, C))] * 4, grid=(2, nb), in_specs=[p0, p1, blk, blk, blk],
               out_specs=[blk] * 4, sends=sends)(P0, P1, w, m, v)


def mod_fwd(scin, wmod, bcol):
    def body(s_ref, w_ref, b_ref, o_ref):
        o_ref[0] = mm(_silu(s_ref[...]), w_ref[0]) + b_ref[0]

    return _pc(body, "mod_fwd", _sds((2, 16, 768)), grid=(2,),
               in_specs=[pl.BlockSpec((16, D), lambda l: (0, 0)), pl.BlockSpec((1, D, 768), lambda l: (l, 0, 0)),
                         pl.BlockSpec((1, 1, 768), lambda l: (l, 0, 0))],
               out_specs=pl.BlockSpec((1, 16, 768), lambda l: (l, 0, 0)))(scin, wmod, bcol)


def mod_bwd(scin, wmod, G):
    def body(s_ref, w_ref, g_ref, dw_o, ds_o):
        _, vjp = jax.vjp(lambda s, w: mm(_silu(s), w), s_ref[...], w_ref[0])
        ds, dw = vjp(g_ref[0])
        dw_o[0] = dw
        _acc_init(pl.program_id(0) == 0, [ds_o])
        ds_o[...] += ds

    full = pl.BlockSpec((16, D), lambda l: (0, 0))
    wsp = pl.BlockSpec((1, D, 768), lambda l: (l, 0, 0))
    return _pc(body, "mod_bwd", [_sds((2, D, 768)), _sds((16, D))], grid=(2,),
               in_specs=[full, wsp, pl.BlockSpec((1, 16, 768), lambda l: (l, 0, 0))], out_specs=[wsp, full])(
        scin, wmod, G)


_SMALL = ["b_mod", "g_mix", "wa_sink", "na_rpb", "ssm_conv_w", "ssm_conv_b", "ssm_dt_bias", "ssm_a_log", "ssm_d",
          "ssm_norm_g", "g_ffn", "g_final", "dmod_s", "dmod_c", "loss"]


def _pack(parts):
    rows = []
    for a in parts:
        f = a.reshape(-1).astype(F32)
        rows.append(jnp.pad(f, (0, (-f.shape[0]) % 1024)).reshape(-1, 128))
    return jnp.concatenate(rows, axis=0)


def _unpack(packed, shapes):
    out, r = [], 0
    for s in shapes:
        nel = int(np.prod(s))
        nr = -(-nel // 1024) * 8
        out.append(packed[r:r + nr].reshape(-1)[:nel].reshape(s))
        r += nr
    return out


def kernel(x, c, ctx, c_ctx, w_mod, b_mod, g_mix, w_in, wa_sink, na_rpb, ssm_conv_w, ssm_conv_b, ssm_dt_bias, ssm_a_log, ssm_d, ssm_norm_g, w_out, g_ffn, w_ffn_in, w_ffn_out, g_final, loss_target, m_c_ctx, m_w_mod, m_b_mod, m_g_mix, m_w_in, m_wa_sink, m_na_rpb, m_ssm_conv_w, m_ssm_conv_b, m_ssm_dt_bias, m_ssm_a_log, m_ssm_d, m_ssm_norm_g, m_w_out, m_g_ffn, m_w_ffn_in, m_w_ffn_out, m_g_final, v_c_ctx, v_w_mod, v_b_mod, v_g_mix, v_w_in, v_wa_sink, v_na_rpb, v_ssm_conv_w, v_ssm_conv_b, v_ssm_dt_bias, v_ssm_a_log, v_ssm_d, v_ssm_norm_g, v_w_out, v_g_ffn, v_w_ffn_in, v_w_ffn_out, v_g_final):
    L = x.shape[1]
    px, py, pc = _place()
    me = 4 * px + 2 * py + pc
    W = dict(c_ctx=c_ctx, w_mod=w_mod, b_mod=b_mod, g_mix=g_mix, w_in=w_in, wa_sink=wa_sink, na_rpb=na_rpb,
             ssm_conv_w=ssm_conv_w, ssm_conv_b=ssm_conv_b, ssm_dt_bias=ssm_dt_bias, ssm_a_log=ssm_a_log, ssm_d=ssm_d,
             ssm_norm_g=ssm_norm_g, w_out=w_out, g_ffn=g_ffn, w_ffn_in=w_ffn_in, w_ffn_out=w_ffn_out, g_final=g_final)
    M = dict(c_ctx=m_c_ctx, w_mod=m_w_mod, b_mod=m_b_mod, g_mix=m_g_mix, w_in=m_w_in, wa_sink=m_wa_sink,
             na_rpb=m_na_rpb, ssm_conv_w=m_ssm_conv_w, ssm_conv_b=m_ssm_conv_b, ssm_dt_bias=m_ssm_dt_bias,
             ssm_a_log=m_ssm_a_log, ssm_d=m_ssm_d, ssm_norm_g=m_ssm_norm_g, w_out=m_w_out, g_ffn=m_g_ffn,
             w_ffn_in=m_w_ffn_in, w_ffn_out=m_w_ffn_out, g_final=m_g_final)
    V = dict(c_ctx=v_c_ctx, w_mod=v_w_mod, b_mod=v_b_mod, g_mix=v_g_mix, w_in=v_w_in, wa_sink=v_wa_sink,
             na_rpb=v_na_rpb, ssm_conv_w=v_ssm_conv_w, ssm_conv_b=v_ssm_conv_b, ssm_dt_bias=v_ssm_dt_bias,
             ssm_a_log=v_ssm_a_log, ssm_d=v_ssm_d, ssm_norm_g=v_ssm_norm_g, w_out=v_w_out, g_ffn=v_g_ffn,
             w_ffn_in=v_w_ffn_in, w_ffn_out=v_w_ffn_out, g_final=v_g_final)

    tr = lambda a: a.transpose(0, 2, 1)
    shards = dict(w_in=tr(w_in).astype(MXU), w_out=w_out.astype(MXU), w_ffn_in=tr(w_ffn_in).astype(MXU),
                  w_ffn_out=w_ffn_out.astype(MXU))
    c_all, conv_all, shards["w_in_first"] = all_gather([c, ssm_conv_w, shards["w_in"][0]], "gather_first")
    conv_f = conv_all.transpose(1, 2, 0, 3).reshape(2, 7, 1024)

    scin = jnp.concatenate([c_all.reshape(NDEV, D), c_ctx.reshape(1, D), jnp.zeros((7, D), F32)], axis=0)
    bcol = lax.dynamic_slice_in_dim(b_mod, me * 768, 768, axis=1).reshape(2, 1, 768)
    mod_all, = all_gather([mod_fwd(scin, w_mod, bcol)], "gather_mod")
    mod_rows = mod_all.transpose(1, 2, 0, 3).reshape(2, 16, 6 * D)
    mods = jnp.stack([lax.dynamic_index_in_dim(mod_rows, me, axis=1, keepdims=False), mod_rows[:, 8]], axis=1)

    layers = [dict(g_mix=g_mix[i], wa_sink=wa_sink[i], na_rpb=na_rpb[i], ssm_conv_w=conv_f[i],
                   ssm_conv_b=ssm_conv_b[i], ssm_dt_bias=ssm_dt_bias[i], ssm_a_log=ssm_a_log[i], ssm_d=ssm_d[i],
                   ssm_norm_g=ssm_norm_g[i], g_ffn=g_ffn[i]) for i in range(2)]
    loss, dx, grads, dmods, dgfin, gin0 = local_step(x[0], ctx[0], loss_target[0], mods, layers, shards, g_final, L)

    stk = lambda n: jnp.stack([grads[0][n], grads[1][n]])
    small = dict(b_mod=dmods[:, 0] + dmods[:, 1], g_final=dgfin, dmod_s=dmods[:, 0], dmod_c=dmods[:, 1],
                 loss=loss.reshape(1))
    for nme in _SMALL:
        if nme not in small:
            small[nme] = stk(nme)
    shapes = [small[nme].shape for nme in _SMALL]
    zero_like = lambda nme: jnp.zeros(small[nme].shape, F32)
    own = lambda S, nme: S[nme] if (nme in S and S[nme].shape == small[nme].shape) else zero_like(nme)
    gath, = all_gather([_pack([small[nme] for nme in _SMALL])], "gather_grads")
    sm = adam_reduce(gath, _pack([own(W, nme) for nme in _SMALL]), _pack([own(M, nme) for nme in _SMALL]),
                     _pack([own(V, nme) for nme in _SMALL]), "adam_small")
    res = {nme: vals for nme, vals in zip(_SMALL, zip(*[_unpack(a, shapes) for a in sm]))}
    loss = res["loss"][0][0]

    cols = lambda a: lax.dynamic_slice_in_dim(a, me * 768, 768, axis=-1)
    rows_of = lambda s: -(-int(np.prod(s)) // 1024) * 8
    r0 = sum(rows_of(s) for s in shapes[:_SMALL.index("dmod_s")])
    dmod_s_all = gath[:, r0:r0 + rows_of(small["dmod_s"].shape)].reshape(NDEV, 2, 6 * D).transpose(1, 0, 2)
    G = jnp.concatenate([cols(dmod_s_all), cols(res["dmod_c"][0])[:, None, :], jnp.zeros((2, 7, 768), F32)], axis=1)
    dwmod, dscin = mod_bwd(scin, w_mod, G)
    cc_g, = all_gather([dscin[8].reshape(8, 128)], "gather_cctx")
    out = {}
    out["c_ctx"] = [a.reshape(D) for a in adam_reduce(cc_g, c_ctx.reshape(8, 128), m_c_ctx.reshape(8, 128),
                                                      v_c_ctx.reshape(8, 128), "adam_cctx")]
    res_wmod, (got1,) = adam_reduce(dwmod.reshape(1, 2 * D, 768), w_mod.reshape(2 * D, 768),
                                    m_w_mod.reshape(2 * D, 768), v_w_mod.reshape(2 * D, 768), "adam_wmod",
                                    sends=(gin0[1],))
    out["w_mod"] = [a.reshape(2, D, 768) for a in res_wmod]
    gconv = lax.dynamic_slice_in_dim(res["ssm_conv_w"][0], me * 128, 128, axis=2)
    out["ssm_conv_w"] = [a.reshape(2, 7, 128) for a in adam_reduce(
        gconv.reshape(1, 14, 128), ssm_conv_w.reshape(14, 128), m_ssm_conv_w.reshape(14, 128),
        v_ssm_conv_w.reshape(14, 128), "adam_conv")]
    for nme in _SMALL:
        if nme not in ("ssm_conv_w", "dmod_s", "dmod_c", "loss"):
            out[nme] = list(res[nme])

    adam_big = lambda nme, t, **kw: adam_layers(grads[0][nme], grads[1][nme], t(W[nme]), t(M[nme]), t(V[nme]),
                                                "adam_" + nme, **kw)
    same = lambda a: a
    res_fi, (got0,) = adam_big("w_ffn_in", tr, sends=(gin0[0],))
    grads[0]["w_in"] = jnp.concatenate([got0, got1], axis=2)
    out["w_ffn_in"] = [tr(a) for a in res_fi]
    out["w_ffn_out"] = list(adam_big("w_ffn_out", same))
    out["w_out"] = list(adam_big("w_out", same))
    out["w_in"] = [tr(a) for a in adam_big("w_in", tr)]
    order = ["c_ctx", "w_mod", "b_mod", "g_mix", "w_in", "wa_sink", "na_rpb", "ssm_conv_w", "ssm_conv_b",
             "ssm_dt_bias", "ssm_a_log", "ssm_d", "ssm_norm_g", "w_out", "g_ffn", "w_ffn_in", "w_ffn_out", "g_final"]
    return (loss, dx.reshape(1, L, D), *[out[nme][0] for nme in order], *[out[nme][1] for nme in order],
            *[out[nme][2] for nme in order], *[out[nme][3] for nme in order])
```

```python
import functools

import numpy as np
import jax
import jax.numpy as jnp
from jax import lax
from jax.experimental import pallas as pl
from jax.experimental.pallas import tpu as pltpu

F32 = jnp.float32
MXU = jnp.bfloat16
_INTERPRET = False
VMEM_LIMIT = 60 * 1024 * 1024

D = 1024
LC = 256
GW = 64
HD = 64
EPS = 1e-6
NEG = -1e30
NDEV = 8
Q = 128
NSTATE = 128
DFF = 2816
IN_COLS = 2832
NP_IN = 3072
C_QA, C_QB, C_Z, C_KA, C_VA, C_KB, C_VB, C_XBC, C_DT = 0, 256, 512, 1024, 1152, 1280, 1536, 1792, 2816
ADAM_LR, ADAM_B1, ADAM_B2, ADAM_EPS, ADAM_WD, ADAM_STEP = 0.001, 0.9, 0.999, 1e-08, 0.01, 10
MESH_T = pl.DeviceIdType.MESH


def _dg(a, b, ca, cb):
    return lax.dot_general(a.astype(MXU), b.astype(MXU), (((ca,), (cb,)), ((), ())), preferred_element_type=F32)


@jax.custom_vjp
def mm(a, b):
    return _dg(a, b, 1, 0)


def _mm_f(a, b):
    return _dg(a, b, 1, 0), (a, b)


def _mm_b(res, g):
    a, b = res
    return _dg(g, b, 1, 1).astype(a.dtype), _dg(a, g, 0, 0).astype(b.dtype)


mm.defvjp(_mm_f, _mm_b)


@jax.custom_vjp
def mm_nt(a, b):
    return _dg(a, b, 1, 1)


def _mmnt_f(a, b):
    return _dg(a, b, 1, 1), (a, b)


def _mmnt_b(res, g):
    a, b = res
    return _dg(g, b, 1, 0).astype(a.dtype), _dg(g, a, 0, 0).astype(b.dtype)


mm_nt.defvjp(_mmnt_f, _mmnt_b)


@jax.custom_vjp
def mm_tn(a, b):
    return _dg(a, b, 0, 0)


def _mmtn_f(a, b):
    return _dg(a, b, 0, 0), (a, b)


def _mmtn_b(res, g):
    a, b = res
    return _dg(b, g, 1, 1).astype(a.dtype), _dg(a, g, 1, 0).astype(b.dtype)


mm_tn.defvjp(_mmtn_f, _mmtn_b)


@jax.custom_vjp
def mmw(a, w):
    return _dg(a, w, 1, 0)


mmw.defvjp(lambda a, w: (_dg(a, w, 1, 0), w), lambda w, g: (_dg(g, w, 1, 1), None))


@jax.custom_vjp
def mmw_nt(a, w):
    return _dg(a, w, 1, 1)


mmw_nt.defvjp(lambda a, w: (_dg(a, w, 1, 1), w), lambda w, g: (_dg(g, w, 1, 0), None))


def _exact(a, b):
    return lax.dot_general(a, b, (((1,), (0,)), ((), ())), precision=lax.Precision.HIGHEST,
                           preferred_element_type=F32)


def _pc(body, name, out_shape, grid=None, in_specs=None, out_specs=None, scratch=(), sends=(), gather=False):
    params = pltpu.CompilerParams(vmem_limit_bytes=VMEM_LIMIT)
    if sends and not isinstance(out_shape, (list, tuple)):
        out_shape, out_specs = [out_shape], [out_specs]
    start, wait = (_ag_start, _ag_wait) if gather else (_a2a_start, _a2a_wait)
    if not sends:
        kw = {}
        if grid is not None:
            kw = dict(grid=grid, in_specs=in_specs, out_specs=out_specs)
        elif in_specs is not None:
            kw = dict(in_specs=in_specs, out_specs=out_specs)
        return pl.pallas_call(body, name=name, out_shape=out_shape, scratch_shapes=list(scratch),
                              compiler_params=params, interpret=_INTERPRET, **kw)
    n, nin, nout, nscr = len(sends), len(in_specs), len(out_shape), len(scratch)

    def body2(*refs):
        cin, xs = refs[:nin], refs[nin:nin + n]
        couts, os_ = refs[nin + n:nin + n + nout], refs[nin + n + nout:nin + 2 * n + nout]
        cscr, sems = refs[nin + 2 * n + nout:nin + 2 * n + nout + nscr], refs[nin + 2 * n + nout + nscr:]
        ids = [pl.program_id(a) for a in range(len(grid))]
        first = functools.reduce(lambda a, b: a & b, [i == 0 for i in ids])
        last = functools.reduce(lambda a, b: a & b, [i == g - 1 for i, g in zip(ids, grid)])

        @pl.when(first)
        def _():
            start(xs, os_, *sems)

        body(*cin, *couts, *cscr)

        @pl.when(last)
        def _():
            wait(xs, os_, *sems)

    call = pl.pallas_call(
        body2, name=name,
        out_shape=list(out_shape) + [_sds(((NDEV,) if gather else ()) + a.shape, a.dtype) for a in sends],
        grid=grid, in_specs=list(in_specs) + [_any()] * n, out_specs=list(out_specs) + [_any()] * n,
        scratch_shapes=list(scratch) + _a2a_sems(n), compiler_params=params, interpret=_INTERPRET)

    def run(*args):
        res = call(*args, *sends)
        return res[:nout], res[nout:]

    return run


def _vm():
    return pl.BlockSpec(memory_space=pltpu.VMEM)


def _sds(shape, dt=F32):
    return jax.ShapeDtypeStruct(shape, dt)


def _iota(shape, dim):
    return lax.broadcasted_iota(jnp.int32, shape, dim)


def _silu(x):
    return x * jax.nn.sigmoid(x)


def _softplus(x):
    return jnp.maximum(x, 0.0) + jnp.log1p(jnp.exp(-jnp.abs(x)))


def _normmod(x, g, sh, sc):
    r = lax.rsqrt(jnp.mean(x * x, axis=-1, keepdims=True) + EPS)
    return (x * r * g) * (1.0 + sc) + sh


def _rope(x, cos, sin, rm):
    return x * cos + _exact(x, rm) * sin


def _swap12(x):
    lane = _iota(x.shape, 1)
    up, down = pltpu.roll(x, 192, 1), pltpu.roll(x, 64, 1)
    return jnp.where((lane >= 64) & (lane < 128), up, jnp.where((lane >= 128) & (lane < 192), down, x))


def _acc_init(first, refs):
    @pl.when(first)
    def _():
        for r in refs:
            r[...] = jnp.zeros_like(r)


def _stream(X, TR, nlt):
    if not isinstance(X, tuple):
        return (X,), [pl.BlockSpec((TR, D), lambda i: (i, 0))], lambda refs: refs[0][...]
    specs = [pl.BlockSpec((TR, D), lambda i: (jnp.minimum(i, nlt - 1), 0)), pl.BlockSpec((TR, D), lambda i: (0, 0))]
    return X, specs, lambda refs: jnp.where(pl.program_id(0) < nlt, refs[0][...], refs[1][...])


def in_fwd(X, g, sh, sc, W, cos, sin, rm, L, sends=()):
    T = L + LC
    TR = 256
    nlt = L // TR
    xs, xspecs, xread = _stream(X, TR, nlt)

    def body(*refs):
        (g_ref, sh_ref, sc_ref, w_ref, cos_ref, sin_ref, rm_ref,
         qa, qb, z, ka, va, kb, vb, xbc, dt, hout) = refs[len(xs):]
        h = _normmod(xread(refs), g_ref[...], sh_ref[0], sc_ref[0]).astype(MXU)
        hout[...] = h
        y = lax.dot_general(h, w_ref[...], (((1,), (1,)), ((), ())), preferred_element_type=F32)
        cs, sn, r = cos_ref[...], sin_ref[...], rm_ref[...]
        qa[...] = _rope(_swap12(y[:, C_QA:C_QB]), cs, sn, r).astype(MXU)
        qb[...] = y[:, C_QB:C_Z].astype(MXU)
        z[...] = y[:, C_Z:C_KA]
        ka[...] = _rope(y[:, C_KA:C_VA], cs[:, :128], sn[:, :128], r[:128, :128]).astype(MXU)
        va[...] = y[:, C_VA:C_KB].astype(MXU)
        kb[...] = y[:, C_KB:C_VB].astype(MXU)
        vb[...] = y[:, C_VB:C_XBC].astype(MXU)
        xbc[...] = y[:, C_XBC:C_DT]
        dt[...] = y[:, C_DT:C_DT + 128]

    row = lambda w: pl.BlockSpec((TR, w), lambda i: (i, 0))
    cls = pl.BlockSpec((1, 1, D), lambda i: (i // nlt, 0, 0))
    widths = [(256, MXU), (256, MXU), (512, F32), (128, MXU), (128, MXU), (256, MXU), (256, MXU), (1024, F32),
              (128, F32), (D, MXU)]
    return _pc(body, "in_fwd", [_sds((T, w), d) for w, d in widths], grid=(T // TR,),
               in_specs=xspecs + [pl.BlockSpec((1, D), lambda i: (0, 0)), cls, cls, _vm(), row(256), row(256), _vm()],
               out_specs=[row(w) for w, _ in widths], sends=sends, gather=True)(*xs, g, sh, sc, W, cos, sin, rm)


def in_bwd(X, g, sh, sc, W, cos, sin, rm, dxres, dqa, dqb, dz, dka, dva, dkb, dvb, dxbc, ddt2, L, latent_only):
    T = L + LC
    TR = 256
    nlt = L // TR
    xs, xspecs, xread = _stream(X, TR, nlt)

    def body(*refs):
        (g_ref, sh_ref, sc_ref, w_ref, cos_ref, sin_ref, rm_ref, dxres_ref, dqa_r, dqb_r, dz_r, dka_r,
         dva_r, dkb_r, dvb_r, dxbc_r, ddt0_r, ddt1_r, dx_o, dy_o, dg_o, dsh_o, dsc_o) = refs[len(xs):]
        i = pl.program_id(0)
        cs, sn, r = cos_ref[...], sin_ref[...], rm_ref[...]
        _, vq = jax.vjp(lambda t: _rope(t, cs, sn, r), dqa_r[...])
        _, vk = jax.vjp(lambda t: _rope(t, cs[:, :128], sn[:, :128], r[:128, :128]), dka_r[...])
        dyqa = _swap12(vq(dqa_r[...])[0])
        dyka, = vk(dka_r[...])
        ddt = ddt0_r[0] + ddt1_r[0]
        dy = jnp.concatenate([dyqa, dqb_r[...], dz_r[...], dyka, dva_r[...], dkb_r[...], dvb_r[...], dxbc_r[...],
                              ddt, jnp.zeros((TR, NP_IN - C_DT - 128), F32)], axis=1).astype(MXU)
        dy_o[...] = dy
        dh = jnp.dot(dy, w_ref[...], preferred_element_type=F32)
        _, vp = jax.vjp(_normmod, xread(refs), g_ref[...], sh_ref[0], sc_ref[0])
        dx, dg, dsh, dsc = vp(dh)
        if latent_only:
            @pl.when(i < nlt)
            def _():
                dx_o[...] = dx + dxres_ref[...]
        else:
            dx_o[...] = dx + dxres_ref[...]
        _acc_init(i == 0, [dg_o])
        _acc_init((i == 0) | (i == nlt), [dsh_o, dsc_o])
        dg_o[...] += dg
        dsh_o[0] += dsh
        dsc_o[0] += dsc

    row = lambda w: pl.BlockSpec((TR, w), lambda i: (i, 0))
    cls = pl.BlockSpec((1, 1, D), lambda i: (i // nlt, 0, 0))
    vec = pl.BlockSpec((1, D), lambda i: (0, 0))
    dts = lambda d: pl.BlockSpec((1, TR, 128), lambda i: (d, i, 0))
    dxs = pl.BlockSpec((TR, D), lambda i: (jnp.minimum(i, nlt - 1), 0)) if latent_only else row(D)
    return _pc(body, "in_bwd",
               [_sds((L if latent_only else T, D)), _sds((T, NP_IN), MXU), _sds((1, D)), _sds((2, 1, D)),
                _sds((2, 1, D))],
               grid=(T // TR,),
               in_specs=xspecs + [vec, cls, cls, _vm(), row(256), row(256), _vm(), row(D), row(256), row(256),
                                  row(512), row(128), row(128), row(256), row(256), row(1024), dts(0), dts(1)],
               out_specs=[dxs, row(NP_IN), vec, cls, cls])(
        *xs, g, sh, sc, W, cos, sin, rm, dxres, dqa, dqb, dz, dka, dva, dkb, dvb, dxbc, ddt2, ddt2)


def tn_mm(A, G, bk, bn, out_dtype, ncol=None, col0=0):
    T, K = A.shape
    N = G.shape[1] if ncol is None else ncol
    first = col0 * (N // bn)
    bt = T
    nt = T // bt

    def body(a_ref, g_ref, o_ref, acc):
        t = pl.program_id(2)
        _acc_init(t == 0, [acc])
        acc[...] += lax.dot_general(a_ref[...], g_ref[...], (((0,), (0,)), ((), ())), preferred_element_type=F32)

        @pl.when(t == nt - 1)
        def _():
            o_ref[...] = acc[...].astype(out_dtype)

    return _pc(body, "tn_mm", _sds((K, N), out_dtype), grid=(K // bk, N // bn, nt),
               in_specs=[pl.BlockSpec((bt, bk), lambda k, n, t: (t, k)),
                         pl.BlockSpec((bt, bn), lambda k, n, t: (t, first + n))],
               out_specs=pl.BlockSpec((bk, bn), lambda k, n, t: (k, n)),
               scratch=[pltpu.VMEM((bk, bn), F32)])(A, G)


def _ssm_out(yf, yb, xs, z, dsk, gs):
    y = (yf + yb + dsk * xs) * _silu(z)
    r = lax.rsqrt(jnp.mean(y * y, axis=-1, keepdims=True) + EPS)
    return y * r * gs


def out_fwd(oa, ob, y2, act, z, dsk, gs, W, X, gate, L, sends=()):
    T = L + LC
    TR = 256
    nlt = L // TR
    xs, xspecs, xread = _stream(X, TR, nlt)

    def body(*refs):
        oa_r, ob_r, yf_r, yb_r, xs_r, z_r, dsk_r, gs_r, w_ref, gt_ref, x1_o, cat_o = refs[len(xs):]
        oc = _ssm_out(yf_r[0], yb_r[0], xs_r[...], z_r[...], dsk_r[...], gs_r[...])
        cat = jnp.concatenate([_swap12(oa_r[...]), ob_r[...], oc], axis=1).astype(MXU)
        cat_o[...] = cat
        x1_o[...] = xread(refs) + gt_ref[0] * jnp.dot(cat, w_ref[...], preferred_element_type=F32)

    row = lambda w: pl.BlockSpec((TR, w), lambda i: (i, 0))
    ys = lambda d: pl.BlockSpec((1, TR, 512), lambda i: (d, i, 0))
    cls = pl.BlockSpec((1, 1, D), lambda i: (i // nlt, 0, 0))
    v512 = pl.BlockSpec((1, 512), lambda i: (0, 0))
    return _pc(body, "out_fwd", [_sds((T, D)), _sds((T, D), MXU)], grid=(T // TR,),
               in_specs=xspecs + [row(256), row(256), ys(0), ys(1), row(512), row(512), v512, v512, _vm(), cls],
               out_specs=[row(D), row(D)], sends=sends, gather=True)(*xs, oa, ob, y2, y2, act, z, dsk, gs, W, gate)


def out_bwd(oa, ob, y2, act, z, dsk, gs, W, gate, dX1, L):
    T = dX1.shape[0]
    TR = 256
    nlt = L // TR

    def body(oa_r, ob_r, yf_r, yb_r, xs_r, z_r, dsk_r, gs_r, w_ref, gt_ref, dx1_r,
             doa_o, dob_o, dy_o, dxs_o, dz_o, dmix_o, ddsk_o, dgs_o, dgt_o):
        i = pl.program_id(0)
        w = w_ref[...]

        def f(oa_, ob_, yf, yb, xs, z_, dsk_, gs_, gt):
            oc = _ssm_out(yf, yb, xs, z_, dsk_, gs_)
            return gt * mmw(jnp.concatenate([oa_, ob_, oc], axis=1), w)

        _, vjp = jax.vjp(f, _swap12(oa_r[...]), ob_r[...], yf_r[0], yb_r[0], xs_r[...], z_r[...], dsk_r[...],
                         gs_r[...], gt_ref[0])
        dx1 = dx1_r[...]
        doa, dob, dyf, _, dxs, dz, ddsk, dgs, dgt = vjp(dx1)
        doa_o[...] = _swap12(doa)
        dob_o[...] = dob
        dy_o[...] = dyf
        dxs_o[...] = dxs
        dz_o[...] = dz
        dmix_o[...] = (gt_ref[0] * dx1).astype(MXU)
        _acc_init(i == 0, [ddsk_o, dgs_o])
        _acc_init((i == 0) | (i == nlt), [dgt_o])
        ddsk_o[...] += ddsk
        dgs_o[...] += dgs
        dgt_o[0] += dgt

    row = lambda w: pl.BlockSpec((TR, w), lambda i: (i, 0))
    ys = lambda d: pl.BlockSpec((1, TR, 512), lambda i: (d, i, 0))
    cls = pl.BlockSpec((1, 1, D), lambda i: (i // nlt, 0, 0))
    v512 = pl.BlockSpec((1, 512), lambda i: (0, 0))
    return _pc(body, "out_bwd",
               [_sds((T, 256)), _sds((T, 256)), _sds((T, 512)), _sds((T, 512)), _sds((T, 512)), _sds((T, D), MXU),
                _sds((1, 512)), _sds((1, 512)), _sds((2, 1, D))],
               grid=(T // TR,),
               in_specs=[row(256), row(256), ys(0), ys(1), row(512), row(512), v512, v512, _vm(), cls, row(D)],
               out_specs=[row(256), row(256), row(512), row(512), row(512), row(D), v512, v512, cls])(
        oa, ob, y2, y2, act, z, dsk, gs, W, gate, dX1)


def ffn_fwd(X, g, sh, sc, gate, Win, Wout, L, sends=()):
    T = X.shape[0]
    TR = 256
    nlt = L // TR

    def body(x_ref, g_ref, sh_ref, sc_ref, gt_ref, wi_ref, wo_ref, o_ref, f_ref):
        h = _normmod(x_ref[...], g_ref[...], sh_ref[0], sc_ref[0]).astype(MXU)
        nt = (((1,), (1,)), ((), ()))
        a = lax.dot_general(h, wi_ref[0:DFF, :], nt, preferred_element_type=F32)
        u = lax.dot_general(h, wi_ref[DFF:2 * DFF, :], nt, preferred_element_type=F32)
        act = (_silu(a) * u).astype(MXU)
        ff = jnp.dot(act, wo_ref[...], preferred_element_type=F32)
        f_ref[...] = ff
        o_ref[...] = x_ref[...] + gt_ref[0] * ff

    row = lambda w: pl.BlockSpec((TR, w), lambda i: (i, 0))
    cls = pl.BlockSpec((1, 1, D), lambda i: (i // nlt, 0, 0))
    vec = pl.BlockSpec((1, D), lambda i: (0, 0))
    return _pc(body, "ffn_fwd", [_sds((T, D)), _sds((T, D))], grid=(T // TR,),
               in_specs=[row(D), vec, cls, cls, cls, _vm(), _vm()], out_specs=[row(D), row(D)], sends=sends,
               gather=True)(X, g, sh, sc, gate, Win, Wout)


def ffn_bwd(X, g, sh, sc, gate, Win, Wout, FF, dX2, L, sends=(), nchunk=2):
    T = X.shape[0]
    TR = 256
    nlt = L // TR
    CH = DFF // nchunk

    def body(x_ref, g_ref, sh_ref, sc_ref, gt_ref, wi_ref, wo_ref, ff_r, dx2_r,
             dx_o, h_o, du_o, act_o, dout_o, dg_o, dsh_o, dsc_o, dgt_o):
        i = pl.program_id(0)
        h, vp = jax.vjp(_normmod, x_ref[...], g_ref[...], sh_ref[0], sc_ref[0])
        dx2 = dx2_r[...]
        dout = gt_ref[0] * dx2
        zero = jnp.zeros((TR, CH), F32)
        dh = jnp.zeros((TR, D), F32)
        for c in range(nchunk):
            lo, hi = c * CH, (c + 1) * CH
            wg, wu, wo = wi_ref[lo:hi, :], wi_ref[DFF + lo:DFF + hi, :], wo_ref[lo:hi, :]

            def f(h_, eg, eu):
                act = _silu(mmw_nt(h_, wg) + eg) * (mmw_nt(h_, wu) + eu)
                return mmw(act, wo), act

            _, vjp_c, act = jax.vjp(f, h, zero, zero, has_aux=True)
            dh_c, da, du = vjp_c(dout)
            dh = dh + dh_c
            du_o[:, lo:hi] = da.astype(MXU)
            du_o[:, DFF + lo:DFF + hi] = du.astype(MXU)
            act_o[:, lo:hi] = act.astype(MXU)
        dx, dg, dsh, dsc = vp(dh)
        dx_o[...] = dx + dx2
        h_o[...] = h.astype(MXU)
        dout_o[...] = dout.astype(MXU)
        _acc_init(i == 0, [dg_o])
        _acc_init((i == 0) | (i == nlt), [dsh_o, dsc_o, dgt_o])
        dg_o[...] += dg
        dsh_o[0] += dsh
        dsc_o[0] += dsc
        dgt_o[0] += jnp.sum(dx2 * ff_r[...], axis=0, keepdims=True)

    row = lambda w: pl.BlockSpec((TR, w), lambda i: (i, 0))
    cls = pl.BlockSpec((1, 1, D), lambda i: (i // nlt, 0, 0))
    vec = pl.BlockSpec((1, D), lambda i: (0, 0))
    return _pc(body, "ffn_bwd",
               [_sds((T, D)), _sds((T, D), MXU), _sds((T, 2 * DFF), MXU), _sds((T, DFF), MXU), _sds((T, D), MXU),
                _sds((1, D)), _sds((2, 1, D)), _sds((2, 1, D)), _sds((2, 1, D))],
               grid=(T // TR,),
               in_specs=[row(D), vec, cls, cls, cls, _vm(), _vm(), row(D), row(D)],
               out_specs=[row(D), row(D), row(2 * DFF), row(DFF), row(D), vec, cls, cls, cls], sends=sends)(
        X, g, sh, sc, gate, Win, Wout, FF, dX2)


def loss_head(X2, g, tgt, L):
    T = X2.shape[0]
    TR = 256
    nlt = L // TR

    def body(x_ref, g_ref, t_ref, loss_o, dx_o, dg_o):
        i = pl.program_id(0)
        _acc_init(i == 0, [loss_o, dg_o])

        @pl.when(i < nlt)
        def _():
            def f(x, g_):
                y = x * lax.rsqrt(jnp.mean(x * x, axis=-1, keepdims=True) + EPS) * g_
                return 0.5 * jnp.sum(jnp.mean(jnp.square(y - t_ref[...]), axis=-1, keepdims=True), axis=0,
                                     keepdims=True)

            val, vjp = jax.vjp(f, x_ref[...], g_ref[...])
            dx, dg = vjp(jnp.ones((1, 1), F32))
            dx_o[...] = dx
            loss_o[...] += jnp.broadcast_to(val, (8, 128))
            dg_o[...] += dg

        @pl.when(i >= nlt)
        def _():
            dx_o[...] = jnp.zeros_like(dx_o)

    row = pl.BlockSpec((TR, D), lambda i: (i, 0))
    vec = pl.BlockSpec((1, D), lambda i: (0, 0))
    return _pc(body, "loss_head", [_sds((8, 128)), _sds((T, D)), _sds((1, D))], grid=(T // TR,),
               in_specs=[row, vec, pl.BlockSpec((TR, D), lambda i: (jnp.minimum(i, nlt - 1), 0))],
               out_specs=[pl.BlockSpec((8, 128), lambda i: (0, 0)), row, vec])(X2, g, tgt)


def _stack_impl(q):
    lane = _iota(q.shape, 1)
    return jnp.concatenate([jnp.where(lane < HD, q, 0.0), jnp.where(lane >= HD, q, 0.0)], axis=0)


def _unstack_impl(o):
    M = o.shape[0] // 2
    return jnp.where(_iota((M, o.shape[1]), 1) < HD, o[:M], o[M:])


@jax.custom_vjp
def _stack(q):
    return _stack_impl(q)


_stack.defvjp(lambda q: (_stack_impl(q), None), lambda _, g: (_unstack_impl(g),))


@jax.custom_vjp
def _unstack(o):
    return _unstack_impl(o)


_unstack.defvjp(lambda o: (_unstack_impl(o), None), lambda _, g: (_stack_impl(g),))


def _softmax_av(q, ks, vs, biases, sink):
    q2 = _stack(q)
    ss = []
    for k, b in zip(ks, biases):
        s = mm_nt(q2, k) * (HD ** -0.5)
        ss.append(s if b is None else s + b)
    m = functools.reduce(jnp.maximum, [jnp.max(s, axis=1, keepdims=True) for s in ss])
    if sink is not None:
        m = jnp.maximum(m, sink)
    m = lax.stop_gradient(m)
    es = [jnp.exp(s - m) for s in ss]
    den = functools.reduce(lambda a, b_: a + b_, [jnp.sum(e, axis=1, keepdims=True) for e in es])
    if sink is not None:
        den = den + jnp.exp(sink - m)
    inv = 1.0 / den
    return _unstack(functools.reduce(lambda a, b_: a + b_, [mm(e * inv, v) for e, v in zip(es, vs)]))


def _sink_col(s0, s1, M):
    return jnp.concatenate([jnp.broadcast_to(jnp.mean(s0, axis=1, keepdims=True), (M, 1)),
                            jnp.broadcast_to(jnp.mean(s1, axis=1, keepdims=True), (M, 1))], axis=0)


def _stack4_impl(q):
    lane = _iota((q.shape[0], 128), 1)
    parts = []
    for p in range(2):
        qp = q[:, 128 * p:128 * (p + 1)]
        parts += [jnp.where(lane < HD, qp, 0.0), jnp.where(lane >= HD, qp, 0.0)]
    return jnp.concatenate(parts, axis=0)


def _unstack4_impl(o):
    M = o.shape[0] // 4
    lane = _iota((M, 128), 1)
    return jnp.concatenate([jnp.where(lane < HD, o[0:M], o[M:2 * M]),
                            jnp.where(lane < HD, o[2 * M:3 * M], o[3 * M:4 * M])], axis=1)


@jax.custom_vjp
def _stack4(q):
    return _stack4_impl(q)


_stack4.defvjp(lambda q: (_stack4_impl(q), None), lambda _, g: (_unstack4_impl(g),))


@jax.custom_vjp
def _unstack4(o):
    return _unstack4_impl(o)


_unstack4.defvjp(lambda o: (_unstack4_impl(o), None), lambda _, g: (_stack4_impl(g),))


WA_NB = 4


def _wa_blocks(qs, kws, vws, kx, vx, sks, n0, L):
    sc = HD ** -0.5
    sink = jnp.concatenate([jnp.broadcast_to(jnp.mean(s_, axis=1, keepdims=True), (Q, 1)) for s_ in sks], axis=0)
    bias = []
    for b_ in range(len(qs)):
        n = n0 + b_
        qpos = n * Q + (_iota((4 * Q, 3 * Q), 0) & (Q - 1))
        kpos = (n - 1) * Q + _iota((4 * Q, 3 * Q), 1)
        bias.append(jnp.where((jnp.abs(qpos - kpos) <= Q) & (kpos >= 0) & (kpos < L), 0.0, NEG))
    q4 = [_stack4(q) for q in qs]
    sl = [mm_nt(a, k) * sc + b_ for a, k, b_ in zip(q4, kws, bias)]
    sx = [mm_nt(a, kx) * sc for a in q4]
    m = [lax.stop_gradient(jnp.maximum(jnp.maximum(jnp.max(a, axis=1, keepdims=True),
                                                   jnp.max(b_, axis=1, keepdims=True)), sink))
         for a, b_ in zip(sl, sx)]
    el = [jnp.exp(a - c) for a, c in zip(sl, m)]
    ex = [jnp.exp(a - c) for a, c in zip(sx, m)]
    inv = [1.0 / (jnp.sum(a, axis=1, keepdims=True) + jnp.sum(b_, axis=1, keepdims=True) + jnp.exp(sink - c))
           for a, b_, c in zip(el, ex, m)]
    return [_unstack4(mm(a * i, v) + mm(b_ * i, vx)) for a, b_, i, v in zip(el, ex, inv, vws)]


def _wa_load(q_r, k_r, v_r, n0):
    f = lambda t: t.astype(F32)
    qs = [f(q_r[b_ * Q:(b_ + 1) * Q, :]) for b_ in range(WA_NB)]
    wins = [pl.ds(pl.multiple_of((n0 + b_) * Q, Q), 3 * Q) for b_ in range(WA_NB)]
    return qs, [f(k_r[w, :]) for w in wins], [f(v_r[w, :]) for w in wins], wins


def _wa_specs(L):
    nb = L // Q
    qs = pl.BlockSpec((WA_NB * Q, 256), lambda n: (n, 0))
    kfull = pl.BlockSpec((L + LC + Q, 128), lambda n: (0, 0))
    sks = pl.BlockSpec((2, 2, 1, 128), lambda n: (0, 0, 0, 0))
    return nb, qs, kfull, sks


def wa_fwd(QA, KA, VA, sinkp, L, sends=()):
    nb, qs, kfull, sks = _wa_specs(L)
    pad = lambda a: jnp.concatenate([jnp.zeros((Q, 128), a.dtype), a], axis=0)

    def body(q_r, k_r, v_r, sk_r, o_ref):
        n0 = pl.program_id(0) * WA_NB
        qs_, kws, vws, _ = _wa_load(q_r, k_r, v_r, n0)
        cx = pl.ds(Q + L, LC)
        outs = _wa_blocks(qs_, kws, vws, k_r[cx, :].astype(F32), v_r[cx, :].astype(F32),
                          [sk_r[0, 0], sk_r[0, 1], sk_r[1, 0], sk_r[1, 1]], n0, L)
        o_ref[...] = jnp.concatenate(outs, axis=0)

    return _pc(body, "wa_fwd", _sds((L, 256)), grid=(nb // WA_NB,), in_specs=[qs, kfull, kfull, sks], out_specs=qs,
               sends=sends, gather=True)(QA, pad(KA), pad(VA), sinkp)


def wa_bwd(QA, KA, VA, sinkp, dO, L, sends=()):
    nb, qs, kfull, sks = _wa_specs(L)
    pad = lambda a: jnp.concatenate([jnp.zeros((Q, 128), a.dtype), a], axis=0)

    def body(q_r, k_r, v_r, sk_r, do_r, dq_o, dk_o, dv_o, dsk_o):
        n0 = pl.program_id(0) * WA_NB
        _acc_init(n0 == 0, [dk_o, dv_o, dsk_o])
        qs_, kws, vws, wins = _wa_load(q_r, k_r, v_r, n0)
        cx = pl.ds(Q + L, LC)
        fn = lambda a, b, c, d, e, s_: _wa_blocks(a, b, c, d, e, s_, n0, L)
        _, vjp = jax.vjp(fn, qs_, kws, vws, k_r[cx, :].astype(F32), v_r[cx, :].astype(F32),
                         [sk_r[0, 0], sk_r[0, 1], sk_r[1, 0], sk_r[1, 1]])
        dqs, dkws, dvws, dkx, dvx, ds = vjp([do_r[b_ * Q:(b_ + 1) * Q, :] for b_ in range(WA_NB)])
        dq_o[...] = jnp.concatenate(dqs, axis=0)
        for w, dk, dv in zip(wins, dkws, dvws):
            dk_o[w, :] += dk
            dv_o[w, :] += dv
        dk_o[cx, :] += dkx
        dv_o[cx, :] += dvx
        for i_ in range(4):
            dsk_o[i_ // 2, i_ % 2] += ds[i_]

    return _pc(body, "wa_bwd", [_sds((L, 256)), _sds((L + LC + Q, 128)), _sds((L + LC + Q, 128)),
                                _sds((2, 2, 1, 128))],
               grid=(nb // WA_NB,), in_specs=[qs, kfull, kfull, sks, qs], out_specs=[qs, kfull, kfull, sks],
               sends=sends)(QA, pad(KA), pad(VA), sinkp, dO)


def _ctx_block(q, kx, vx, s0, s1):
    return _softmax_av(q, [kx], [vx], [None], _sink_col(s0, s1, LC))


def ctx_fwd(Qx, Kx, Vx, sinkp, shared, L):
    cq = pl.BlockSpec((LC, 128), lambda p: (L // LC, p))
    ck = pl.BlockSpec((LC, 128), lambda p: (L // LC, 0 if shared else p))
    sks = pl.BlockSpec((1, 2, 1, 128), lambda p: (p, 0, 0, 0))

    def body(q_r, k_r, v_r, sk_r, o_ref):
        f = lambda t: t[...].astype(F32)
        o_ref[...] = _ctx_block(f(q_r), f(k_r), f(v_r), sk_r[0, 0], sk_r[0, 1])

    return _pc(body, "ctx_fwd", _sds((LC, 256)), grid=(2,), in_specs=[cq, ck, ck, sks],
               out_specs=pl.BlockSpec((LC, 128), lambda p: (0, p)))(Qx, Kx, Vx, sinkp)


def ctx_bwd(Qx, Kx, Vx, sinkp, dO, shared, L):
    cq = pl.BlockSpec((LC, 128), lambda p: (L // LC, p))
    ck = pl.BlockSpec((LC, 128), lambda p: (L // LC, 0 if shared else p))
    sks = pl.BlockSpec((1, 2, 1, 128), lambda p: (p, 0, 0, 0))
    op = pl.BlockSpec((LC, 128), lambda p: (0, p))
    ok = pl.BlockSpec((LC, 128), lambda p: (0, 0 if shared else p))
    dos = pl.BlockSpec((LC, 128), lambda p: (L // LC, p))

    def body(q_r, k_r, v_r, sk_r, do_r, dq_o, dk_o, dv_o, dsk_o):
        p = pl.program_id(0)
        f = lambda t: t[...].astype(F32)
        _, vjp = jax.vjp(_ctx_block, f(q_r), f(k_r), f(v_r), sk_r[0, 0], sk_r[0, 1])
        dq, dk, dv, ds0, ds1 = vjp(do_r[...])
        dq_o[...] = dq
        _acc_init((p == 0) if shared else (p >= 0), [dk_o, dv_o])
        dk_o[...] += dk
        dv_o[...] += dv
        dsk_o[0, 0] = ds0
        dsk_o[0, 1] = ds1

    kw = 128 if shared else 256
    return _pc(body, "ctx_bwd", [_sds((LC, 256)), _sds((LC, kw)), _sds((LC, kw)), _sds((2, 2, 1, 128))],
               grid=(2,), in_specs=[cq, ck, ck, sks, dos], out_specs=[op, ok, ok, sks])(Qx, Kx, Vx, sinkp, dO)


def _na_rows(qs, kws, vws, kx, vx, bs):
    sc = HD ** -0.5
    q2 = [_stack(q) for q in qs]
    sl = [mm_nt(a, k) * sc + b for a, k, b in zip(q2, kws, bs)]
    sx = [mm_nt(a, kx) * sc for a in q2]
    m = [lax.stop_gradient(jnp.maximum(jnp.max(a, axis=1, keepdims=True), jnp.max(b, axis=1, keepdims=True)))
         for a, b in zip(sl, sx)]
    el = [jnp.exp(a - c) for a, c in zip(sl, m)]
    ex = [jnp.exp(a - c) for a, c in zip(sx, m)]
    inv = [1.0 / (jnp.sum(a, axis=1, keepdims=True) + jnp.sum(b, axis=1, keepdims=True)) for a, b in zip(el, ex)]
    o2 = [mm(a * i, v) + mm(b * i, vx) for a, b, i, v in zip(el, ex, inv, vws)]
    return [_unstack(o) for o in o2]


NA_ROWS = 16


def _na_geom(r, R):
    s = jnp.clip(r - 4, 0, R - 8)
    cls = jnp.where(r < 4, r, jnp.where(r > R - 4, r - (R - 8), 4))
    return pl.ds(pl.multiple_of(s * GW, GW), 8 * GW), cls


def _na_load(q_r, k_r, v_r, b_r, rb, R):
    nr = min(NA_ROWS, R)
    geo = [_na_geom(rb * nr + j, R) for j in range(nr)]
    qs = [q_r[j * GW:(j + 1) * GW, :].astype(F32) for j in range(nr)]
    kws = [k_r[win, :].astype(F32) for win, _ in geo]
    vws = [v_r[win, :].astype(F32) for win, _ in geo]
    bs = [jnp.concatenate([b_r[0, cls], b_r[1, cls]], axis=0) for _, cls in geo]
    return geo, qs, kws, vws, bs


def na_fwd(QB, KB, VB, biasd, L, sends=()):
    R = L // GW
    nr = min(NA_ROWS, R)
    qs = pl.BlockSpec((nr * GW, 128), lambda p, rb: (rb, p))
    kfull = pl.BlockSpec((L, 128), lambda p, rb: (0, p))
    kctx = pl.BlockSpec((LC, 128), lambda p, rb: (L // LC, p))
    bs = pl.BlockSpec((2, 8, GW, 8 * GW), lambda p, rb: (p, 0, 0, 0))

    def body(q_r, k_r, v_r, kx_r, vx_r, b_r, o_ref):
        _, qs_, kws, vws, bs_ = _na_load(q_r, k_r, v_r, b_r, pl.program_id(1), R)
        outs = _na_rows(qs_, kws, vws, kx_r[...].astype(F32), vx_r[...].astype(F32), bs_)
        o_ref[...] = jnp.concatenate(outs, axis=0)

    return _pc(body, "na_fwd", _sds((L, 256)), grid=(2, R // nr), in_specs=[qs, kfull, kfull, kctx, kctx, bs],
               out_specs=qs, sends=sends, gather=True)(QB, KB, VB, KB, VB, biasd)


def na_bwd(QB, KB, VB, biasd, dO, L):
    R = L // GW
    nr = min(NA_ROWS, R)
    qs = pl.BlockSpec((nr * GW, 128), lambda p, rb: (rb, p))
    kfull = pl.BlockSpec((L, 128), lambda p, rb: (0, p))
    kctx = pl.BlockSpec((LC, 128), lambda p, rb: (L // LC, p))
    bs = pl.BlockSpec((2, 8, GW, 8 * GW), lambda p, rb: (p, 0, 0, 0))
    oc = pl.BlockSpec((LC, 128), lambda p, rb: (0, p))

    def body(q_r, k_r, v_r, kx_r, vx_r, b_r, do_r, dq_o, dk_o, dv_o, dkx_o, dvx_o, db_o):
        rb = pl.program_id(1)
        _acc_init(rb == 0, [dk_o, dv_o, dkx_o, dvx_o, db_o])
        geo, qs_, kws, vws, bs_ = _na_load(q_r, k_r, v_r, b_r, rb, R)
        _, vjp = jax.vjp(_na_rows, qs_, kws, vws, kx_r[...].astype(F32), vx_r[...].astype(F32), bs_)
        dqs, dkws, dvws, dkx, dvx, dbs = vjp([do_r[j * GW:(j + 1) * GW, :] for j in range(nr)])
        dq_o[...] = jnp.concatenate(dqs, axis=0)
        dkx_o[...] += dkx
        dvx_o[...] += dvx
        for j, (win, cls) in enumerate(geo):
            dk_o[win, :] += dkws[j]
            dv_o[win, :] += dvws[j]
            db_o[0, cls] += dbs[j][:GW]
            db_o[1, cls] += dbs[j][GW:]

    return _pc(body, "na_bwd",
               [_sds((L, 256)), _sds((L, 256)), _sds((L, 256)), _sds((LC, 256)), _sds((LC, 256)),
                _sds((4, 8, GW, 8 * GW))],
               grid=(2, R // nr), in_specs=[qs, kfull, kfull, kctx, kctx, bs, qs],
               out_specs=[qs, kfull, kfull, oc, oc, bs])(QB, KB, VB, KB, VB, biasd, dO)


def exact_mm_call(A, B):
    def body(a_ref, b_ref, o_ref):
        o_ref[...] = _exact(a_ref[...], b_ref[...])

    return _pc(body, "exact_mm", _sds((A.shape[0], B.shape[1])))(A, B)


def _conv_shift(x, d, L):
    T = x.shape[0]
    if d == 0:
        return x
    t = _iota(x.shape, 0)
    src = t + d
    ok = (src >= 0) & (src < T) & ((src >= L) == (t >= L))
    return jnp.where(ok, pltpu.roll(x, (-d) % T, 0), 0.0)


def conv_fwd(XBC, w8, b, L, sends=()):
    T = XBC.shape[0]

    def body(x_ref, w_ref, b_ref, o_ref):
        x = x_ref[...]
        pre = b_ref[...] + functools.reduce(
            lambda a, c: a + c, [_conv_shift(x, k - 3, L) * w_ref[k:k + 1, :] for k in range(7)])
        o_ref[...] = _silu(pre)

    col = pl.BlockSpec((T, 128), lambda j: (0, j))
    return _pc(body, "conv_fwd", _sds((T, 1024)), grid=(8,),
               in_specs=[col, pl.BlockSpec((8, 128), lambda j: (0, j)), pl.BlockSpec((1, 128), lambda j: (0, j))],
               out_specs=col, sends=sends, gather=True)(XBC, w8, b)


def conv_bwd(XBC, w8, b, dS, dxs_skip, L, sends=()):
    T = XBC.shape[0]

    def body(x_ref, w_ref, b_ref, d0_r, d1_r, dsk_r, dx_o, dw_o, db_o):
        j = pl.program_id(0)
        x = x_ref[...]
        xs = [_conv_shift(x, k - 3, L) for k in range(7)]
        pre = b_ref[...] + functools.reduce(lambda a, c: a + c, [xs[k] * w_ref[k:k + 1, :] for k in range(7)])
        _, vjp = jax.vjp(_silu, pre)
        dact = d0_r[0] + d1_r[0] + jnp.where(j < 4, dsk_r[...], 0.0)
        dpre, = vjp(dact)
        dx_o[...] = functools.reduce(
            lambda a, c: a + c, [_conv_shift(dpre, 3 - k, L) * w_ref[k:k + 1, :] for k in range(7)])
        dw_o[...] = jnp.concatenate([jnp.sum(dpre * xs[k], axis=0, keepdims=True) for k in range(7)]
                                    + [jnp.zeros((1, 128), F32)], axis=0)
        db_o[...] = jnp.sum(dpre, axis=0, keepdims=True)

    col = pl.BlockSpec((T, 128), lambda j: (0, j))
    w_s = pl.BlockSpec((8, 128), lambda j: (0, j))
    b_s = pl.BlockSpec((1, 128), lambda j: (0, j))
    ds = lambda d: pl.BlockSpec((1, T, 128), lambda j: (d, 0, j))
    return _pc(body, "conv_bwd", [_sds((T, 1024)), _sds((8, 1024)), _sds((1, 1024))], grid=(8,),
               in_specs=[col, w_s, b_s, ds(0), ds(1), pl.BlockSpec((T, 128), lambda j: (0, jnp.minimum(j, 3)))],
               out_specs=[col, w_s, b_s], sends=sends)(XBC, w8, b, dS, dS, dxs_skip)


def _ssd_chunk(xs, bs, cs, dtraw, dtb, alog, hs, tri, d):
    dt = _softplus(dtraw + dtb)
    a = dt * (-jnp.exp(alog))
    acum = _exact(tri, a)
    tot = jnp.sum(a, axis=0, keepdims=True)
    wcol = jnp.exp(tot - acum) * dt
    ea = jnp.exp(acum)
    cd = jnp.exp(tot)
    acum_t, dt_t = acum.T, dt.T
    lane = _iota((Q, 128), 1)
    srow = _iota((128, Q), 0)
    lane1 = _iota((1, 128), 1)
    prow = _iota((128, NSTATE), 0)
    mask = tri > 0.5
    cbs = [mm_nt(cs[g], bs[g]) for g in range(2)]
    ys, hn = [], []
    for j in range(4):
        g = j // 2
        x = xs[j]
        yi, st, eac, cdl = [], [], [], []
        for u in range(2):
            slot = d * 8 + 2 * j + u
            col = lambda m: jnp.sum(jnp.where(lane == slot, m, 0.0), axis=1, keepdims=True)
            rowv = lambda m: jnp.sum(jnp.where(srow == slot, m, 0.0), axis=0, keepdims=True)
            seg = col(acum) - rowv(acum_t)
            dcy = jnp.where(mask, jnp.exp(jnp.where(mask, seg, 0.0)), 0.0)
            yi.append(mm(cbs[g] * dcy * rowv(dt_t), x))
            st.append(mm_tn(x, bs[g] * col(wcol)))
            eac.append(col(ea))
            cdl.append(jnp.sum(jnp.where(lane1 == slot, cd, 0.0), axis=1, keepdims=True))
        yin = mm_nt(cs[g], hs[j])
        ys.append(jnp.where(lane < HD, yi[0] + yin * eac[0], yi[1] + yin * eac[1]))
        hn.append(hs[j] * jnp.where(prow < HD, cdl[0], cdl[1]) + jnp.where(prow < HD, st[0], st[1]))
    return ys, hn


def _ssd_chunk_idx(d, s, nlc, nch):
    return jnp.where(d == 0, (s + nlc) % nch, nch - 1 - s)


def ssd_fwd(ACT, DT, dtb, alog, tri2, L, sends=()):
    T = ACT.shape[0]
    nlc, nch = L // Q, T // Q

    def body(a_ref, dt_ref, dtb_ref, al_ref, tri_ref, y_o, hs_o, hst):
        d, s = pl.program_id(0), pl.program_id(1)
        _acc_init(s == 0, [hst])
        a = a_ref[...]
        xs = [a[:, 128 * j:128 * (j + 1)] for j in range(4)]
        bs = [a[:, 512 + 128 * g:640 + 128 * g] for g in range(2)]
        cs = [a[:, 768 + 128 * g:896 + 128 * g] for g in range(2)]
        hs = [hst[j] for j in range(4)]
        hs_o[0, 0] = hst[...]
        ys, hn = _ssd_chunk(xs, bs, cs, dt_ref[...], dtb_ref[...], al_ref[...], hs, tri_ref[0], d)
        y_o[0] = jnp.concatenate(ys, axis=1)
        for j in range(4):
            hst[j] = hn[j]

    ck = lambda w: pl.BlockSpec((Q, w), lambda d, s: (_ssd_chunk_idx(d, s, nlc, nch), 0))
    v128 = pl.BlockSpec((1, 128), lambda d, s: (0, 0))
    return _pc(body, "ssd_fwd", [_sds((2, T, 512)), _sds((2, nch, 4, 128, NSTATE))], grid=(2, nch),
               in_specs=[ck(1024), ck(128), v128, v128, pl.BlockSpec((1, Q, Q), lambda d, s: (d, 0, 0))],
               out_specs=[pl.BlockSpec((1, Q, 512), lambda d, s: (d, _ssd_chunk_idx(d, s, nlc, nch), 0)),
                          pl.BlockSpec((1, 1, 4, 128, NSTATE), lambda d, s: (d, s, 0, 0, 0))],
               scratch=[pltpu.VMEM((4, 128, NSTATE), F32)], sends=sends, gather=True)(ACT, DT, dtb, alog, tri2)


def ssd_bwd(ACT, DT, dtb, alog, tri2, HS, dY, L, sends=()):
    T = ACT.shape[0]
    nlc, nch = L // Q, T // Q

    def body(a_ref, dt_ref, dtb_ref, al_ref, tri_ref, hs_ref, dy_ref, da_o, ddt_o, ddtb_o, dal_o, dh):
        d, sr = pl.program_id(0), pl.program_id(1)
        _acc_init(sr == 0, [dh, ddtb_o, dal_o])
        a = a_ref[...]
        xs = [a[:, 128 * j:128 * (j + 1)] for j in range(4)]
        bs = [a[:, 512 + 128 * g:640 + 128 * g] for g in range(2)]
        cs = [a[:, 768 + 128 * g:896 + 128 * g] for g in range(2)]
        hs = [hs_ref[0, 0, j] for j in range(4)]
        tri = tri_ref[0]
        fn = lambda xs_, bs_, cs_, dtr, dtb_, al, hs_: _ssd_chunk(xs_, bs_, cs_, dtr, dtb_, al, hs_, tri, d)
        _, vjp = jax.vjp(fn, xs, bs, cs, dt_ref[...], dtb_ref[...], al_ref[...], hs)
        dy = dy_ref[...]
        dys = [dy[:, 128 * j:128 * (j + 1)] for j in range(4)]
        dxs, dbs, dcs, ddt, ddtb, dal, dhs = vjp((dys, [dh[j] for j in range(4)]))
        da_o[0] = jnp.concatenate(dxs + dbs + dcs, axis=1)
        ddt_o[0] = ddt
        ddtb_o[0] += ddtb
        dal_o[0] += dal
        for j in range(4):
            dh[j] = dhs[j]

    cidx = lambda d, sr: _ssd_chunk_idx(d, nch - 1 - sr, nlc, nch)
    ck = lambda w: pl.BlockSpec((Q, w), lambda d, sr: (cidx(d, sr), 0))
    v128 = pl.BlockSpec((1, 128), lambda d, sr: (0, 0))
    o128 = pl.BlockSpec((1, 1, 128), lambda d, sr: (d, 0, 0))
    return _pc(body, "ssd_bwd", [_sds((2, T, 1024)), _sds((2, T, 128)), _sds((2, 1, 128)), _sds((2, 1, 128))],
               grid=(2, nch),
               in_specs=[ck(1024), ck(128), v128, v128, pl.BlockSpec((1, Q, Q), lambda d, sr: (d, 0, 0)),
                         pl.BlockSpec((1, 1, 4, 128, NSTATE), lambda d, sr: (d, nch - 1 - sr, 0, 0, 0)), ck(512)],
               out_specs=[pl.BlockSpec((1, Q, 1024), lambda d, sr: (d, cidx(d, sr), 0)),
                          pl.BlockSpec((1, Q, 128), lambda d, sr: (d, cidx(d, sr), 0)), o128, o128],
               scratch=[pltpu.VMEM((4, 128, NSTATE), F32)], sends=sends)(ACT, DT, dtb, alog, tri2, HS, dY)


_PAIR_HEADS = np.array([[0, 2], [1, 3]])


def _tables(L):
    t = jnp.arange(L)
    inv = 10000.0 ** (-jnp.arange(16, dtype=F32) / 16)

    def half(pos):
        ang = pos.astype(F32)[:, None] * inv[None, :]
        return jnp.concatenate([ang, ang], axis=1)

    ang = jnp.tile(jnp.concatenate([half(t // GW), half(t % GW)], axis=1), (1, 4))
    cos = jnp.concatenate([jnp.cos(ang), jnp.ones((LC, 256), F32)], axis=0)
    sin = jnp.concatenate([jnp.sin(ang), jnp.zeros((LC, 256), F32)], axis=0)
    rm = np.zeros((256, 256), np.float32)
    for j in range(256):
        if j % 32 < 16:
            rm[j + 16, j] = -1.0
        else:
            rm[j - 16, j] = 1.0
    tri = np.tril(np.ones((Q, Q), np.float32))
    return cos, sin, jnp.asarray(rm), jnp.asarray(np.stack([tri, tri.T]))


def _na_index(R):
    rc = np.array([0, 1, 2, 3, 4, R - 3, R - 2, R - 1])
    dy = np.clip(rc - 4, 0, R - 8)[:, None] + np.arange(8)[None, :] - rc[:, None] + 7
    qc, cc = np.arange(GW)[:, None], np.arange(GW)[None, :]
    dx = np.clip(cc - qc, -15, 15) + 15
    cstart = np.clip(qc - 8, 0, GW - 16)
    cmask = (cc >= cstart) & (cc < cstart + 16)
    idx = dy[:, None, :, None] * 31 + dx[None, :, None, :]
    return idx.reshape(8, GW, 8 * GW), np.broadcast_to(cmask[None, :, None, :], idx.shape).reshape(8, GW, 8 * GW), \
        dy, dx, cmask


def _na_bias(rpb, R):
    _, cm, dy, _, _ = _na_index(R)
    rows = rpb[:, dy.reshape(-1), :].reshape(4, 8, 4, 2, 31)
    p2 = jnp.pad(jnp.pad(rows, ((0, 0),) * 4 + ((0, 33),)).reshape(4, 8, 4, 128), ((0, 0), (0, 0), (0, 4), (0, 0)))
    negmask = jnp.asarray(np.where(cm[0], 0.0, NEG).astype(np.float32))

    def body(p_ref, m_ref, o_ref):
        for c in range(8):
            tiles = [pltpu.roll(jnp.broadcast_to(p_ref[0, c, jp:jp + 1, :], (GW, 128)), 113, 1, stride=1,
                                stride_axis=0) for jp in range(4)]
            o_ref[0, c] = jnp.where(m_ref[...] < 0.0, NEG, jnp.concatenate(tiles, axis=1))

    return _pc(body, "na_bias", _sds((4, 8, GW, 8 * GW)), grid=(4,),
               in_specs=[pl.BlockSpec((1, 8, 8, 128), lambda h: (h, 0, 0, 0)),
                         pl.BlockSpec((GW, 8 * GW), lambda h: (0, 0))],
               out_specs=pl.BlockSpec((1, 8, GW, 8 * GW), lambda h: (h, 0, 0, 0)))(p2, negmask)


def _na_bias_grad(dbias, R):
    _, _, dy, dx, cmask = _na_index(R)
    e1 = np.zeros((GW * GW, 128), np.float32)
    e1[np.arange(GW * GW), dx.reshape(-1)] = cmask.reshape(-1)
    a1 = dbias.reshape(4, 8, GW, 8, GW).transpose(0, 1, 3, 2, 4).reshape(256, GW * GW)
    v = exact_mm_call(a1, jnp.asarray(e1))[:, :31].reshape(4, 64, 31)
    e2 = np.zeros((64, 128), np.float32)
    e2[np.arange(64), dy.reshape(-1)] = 1.0
    a2 = jnp.pad(v.transpose(0, 2, 1).reshape(124, 64), ((0, 4), (0, 0)))
    return exact_mm_call(a2, jnp.asarray(e2))[:124, :15].reshape(4, 31, 15).transpose(0, 2, 1)


def _lanes(v, n=128):
    v = v.reshape(1, -1)
    return jnp.pad(v, ((0, 0), (0, n - v.shape[1])))


def _cls2(a, b):
    return jnp.stack([a, b]).reshape(2, 1, D)


def _win_p(g):
    return jnp.concatenate([g.reshape(IN_COLS, D), jnp.zeros((NP_IN - IN_COLS, D), g.dtype)], axis=0)


def _layer_consts(p):
    sinkp = jnp.broadcast_to(p["wa_sink"][_PAIR_HEADS][:, :, None, None], (2, 2, 1, 128))
    return dict(
        sinkp=sinkp, nosink=jnp.full((2, 2, 1, 128), NEG, F32),
        w8=jnp.concatenate([p["ssm_conv_w"], jnp.zeros((1, 1024), F32)], axis=0),
        cb=p["ssm_conv_b"].reshape(1, 1024), dtb=_lanes(p["ssm_dt_bias"]), alog=_lanes(p["ssm_a_log"]),
        dsk=jnp.repeat(p["ssm_d"], HD).reshape(1, 512), gs=p["ssm_norm_g"].reshape(1, 512),
        gmix=p["g_mix"].reshape(1, D), gffn=p["g_ffn"].reshape(1, D))


def _mods(mod2):
    return [_cls2(mod2[0, D * k:D * (k + 1)], mod2[1, D * k:D * (k + 1)]) for k in range(6)]


def _layer_fwd(X, mod2, c, rpb, tabs, L, ctx_out, shards, nxt):
    cos, sin, rm, tri2 = tabs
    sh1, sc1, gt1, sh2, sc2, gt2 = _mods(mod2)
    biasd = _na_bias(rpb, L // GW)
    fi, fo, wo = shards
    fcut, fcut2, ocut = 448, 640, 224
    (qa, qb, z, ka, va, kb, vb, xbc, dt, h1), (gfo_a,) = in_fwd(X, c["gmix"], sh1, sc1, c["win"], cos, sin, rm, L,
                                                                sends=(fo[:ocut],))
    (oa,), (gfi_b,) = wa_fwd(qa, ka, va, c["sinkp"], L, sends=(fi[fcut:fcut2],))
    (ob,), (gwo,) = na_fwd(qb, kb, vb, biasd, L, sends=(wo,))
    c = dict(c, wout=gwo.reshape(D, D))
    if ctx_out:
        oa_c = ctx_fwd(qa, ka, va, c["sinkp"], True, L)
        ob_c = ctx_fwd(qb, kb, vb, c["nosink"], False, L)
    else:
        oa_c = ob_c = jnp.zeros((LC, 256), F32)
    oa = jnp.concatenate([oa, oa_c], axis=0)
    ob = jnp.concatenate([ob, ob_c], axis=0)
    (act,), (gfi_c,) = conv_fwd(xbc, c["w8"], c["cb"], L, sends=(fi[fcut2:],))
    (y2, hs), (gfi_a,) = ssd_fwd(act, dt, c["dtb"], c["alog"], tri2, L, sends=(fi[:fcut],))
    (X1, cat), (gfo_b,) = out_fwd(oa, ob, y2, act, z, c["dsk"], c["gs"], c["wout"], X, gt1, L, sends=(fo[ocut:],))
    c = dict(c, wfi=jnp.concatenate([gfi_a, gfi_b, gfi_c], axis=1).reshape(2 * DFF, D),
             wfo=jnp.concatenate([gfo_a, gfo_b], axis=1).reshape(DFF, D))
    res = ffn_fwd(X1, c["gffn"], sh2, sc2, gt2, c["wfi"], c["wfo"], L, sends=nxt)
    (X2, ff), got = res if nxt else (res, ())
    saved = dict(X=X, X1=X1, ff=ff, qa=qa, qb=qb, z=z, ka=ka, va=va, kb=kb, vb=vb, xbc=xbc, dt=dt, h1=h1, oa=oa, ob=ob,
                 act=act, y2=y2, hs=hs, cat=cat, biasd=biasd)
    return X2, saved, c, got


def _row_blocks(gw):
    return gw.reshape(NDEV, gw.shape[0] // NDEV, gw.shape[1])


def _layer_bwd(dX2, s, mod2, c, tabs, L, ctx_out, carry):
    cos, sin, rm, tri2 = tabs
    sh1, sc1, gt1, sh2, sc2, gt2 = _mods(mod2)
    R = L // GW
    res = ffn_bwd(s["X1"], c["gffn"], sh2, sc2, gt2, c["wfi"], c["wfo"], s["ff"], dX2, L, sends=carry)
    (dX1, h2, dU, actf, dOut, dgffn, dsh2, dsc2, dgt2), got = res if carry else (res, ())
    g = {}
    gfi = _row_blocks(tn_mm(dU, h2, 512, 1024, MXU))
    gfo = _row_blocks(tn_mm(actf, dOut, 256, 1024, MXU))
    doa, dob, dy, dxs_skip, dz, dmix, ddsk, dgs, dgt1 = out_bwd(s["oa"], s["ob"], s["y2"], s["act"], s["z"], c["dsk"],
                                                                c["gs"], c["wout"], gt1, dX1, L)
    gout = _row_blocks(tn_mm(s["cat"], dmix, 512, 1024, MXU))
    (dS, ddt2, ddtb, dal), (g["w_ffn_in"],) = ssd_bwd(
        s["act"], s["dt"], c["dtb"], c["alog"], tri2, s["hs"], dy, L, sends=(gfi,))
    (dxbc, dw8, dcb), (g["w_ffn_out"],) = conv_bwd(s["xbc"], c["w8"], c["cb"], dS, dxs_skip, L, sends=(gfo,))
    (dqa, dka, dva, dska), (g["w_out"],) = wa_bwd(s["qa"], s["ka"], s["va"], c["sinkp"], doa, L, sends=(gout,))
    dka, dva = dka[Q:], dva[Q:]
    dqb, dkb, dvb, dkxb, dvxb, dbias = na_bwd(s["qb"], s["kb"], s["vb"], s["biasd"], dob, L)
    if ctx_out:
        dqa_c, dk1, dv1, dsk1 = ctx_bwd(s["qa"], s["ka"], s["va"], c["sinkp"], doa, True, L)
        dqb_c, dk2, dv2, _ = ctx_bwd(s["qb"], s["kb"], s["vb"], c["nosink"], dob, False, L)
        dka = jnp.concatenate([dka[:L], dka[L:] + dk1], axis=0)
        dva = jnp.concatenate([dva[:L], dva[L:] + dv1], axis=0)
        dska = dska + dsk1
        dkxb, dvxb = dkxb + dk2, dvxb + dv2
    else:
        dqa_c = dqb_c = jnp.zeros((LC, 256), F32)
    cat0 = lambda a, b: jnp.concatenate([a, b], axis=0)
    dX, dycat, dgmix, dsh1, dsc1 = in_bwd(
        s["X"], c["gmix"], sh1, sc1, c["win"], cos, sin, rm, dX1, cat0(dqa, dqa_c), cat0(dqb, dqb_c), dz,
        dka, dva, cat0(dkb, dkxb), cat0(dvb, dvxb), dxbc, ddt2, L,
        latent_only=ctx_out)
    if ctx_out:
        gin = [_row_blocks(tn_mm(dycat, s["h1"], 512, D // 2, MXU, ncol=D // 2, col0=k)[:IN_COLS]) for k in (0, 1)]
    else:
        gin = _row_blocks(tn_mm(dycat, s["h1"], 512, 1024, MXU)[:IN_COLS])
    g["g_mix"] = dgmix.reshape(D)
    g["g_ffn"] = dgffn.reshape(D)
    sk = jnp.sum(dska, axis=(2, 3))
    g["wa_sink"] = jnp.zeros((4,), F32).at[_PAIR_HEADS.reshape(-1)].set(sk.reshape(-1))
    g["na_rpb"] = _na_bias_grad(dbias, R)
    g["ssm_conv_w"] = dw8[:7]
    g["ssm_conv_b"] = dcb.reshape(1024)
    g["ssm_dt_bias"] = (ddtb[0] + ddtb[1])[0, :16].reshape(2, 8)
    g["ssm_a_log"] = (dal[0] + dal[1])[0, :16].reshape(2, 8)
    g["ssm_d"] = jnp.sum(ddsk.reshape(8, HD), axis=1)
    g["ssm_norm_g"] = dgs.reshape(512)
    dmod2 = jnp.concatenate([dsh1, dsc1, dgt1, dsh2, dsc2, dgt2], axis=2).reshape(2, 6 * D)
    return dX, g, dmod2, gin, got


def local_step(x, ctx, tgt, mods, layers, shards, g_final, L):
    tabs = _tables(L)
    X = (x, ctx)
    consts = [_layer_consts(p) for p in layers]
    saved = []
    got = (shards["w_in_first"],)
    for i in range(2):
        consts[i] = dict(consts[i], win=_win_p(got[0]))
        nxt = (shards["w_in"][1],) if i == 0 else ()
        X, s, consts[i], got = _layer_fwd(X, mods[i], consts[i], layers[i]["na_rpb"], tabs, L, i == 0,
                                          (shards["w_ffn_in"][i], shards["w_ffn_out"][i], shards["w_out"][i]), nxt)
        saved.append(s)
    loss8, dX, dgfin = loss_head(X, g_final.reshape(1, D), tgt, L)
    grads, dmods = [None, None], [None, None]
    dX, grads[1], dmods[1], gin1, _ = _layer_bwd(dX, saved[1], mods[1], consts[1], tabs, L, False, ())
    dX, grads[0], dmods[0], gin0, (grads[1]["w_in"],) = _layer_bwd(dX, saved[0], mods[0], consts[0], tabs, L, True,
                                                                   (gin1,))
    return loss8[0, 0], dX, grads, jnp.stack(dmods), dgfin.reshape(D), gin0


def _place():
    x, y, c = lax.axis_index("x"), lax.axis_index("y"), lax.axis_index("c")
    return x, y, c


def _slot(b):
    return 4 * b[0] + 2 * b[1] + b[2]


def _any():
    return pl.BlockSpec(memory_space=pl.ANY)


def all_gather(xs, name):
    n = len(xs)

    def body(*refs):
        x_refs, o_refs = refs[:n], refs[n:2 * n]
        send_sems, recv_sems, local_sems = refs[2 * n:]
        x, y, c = _place()
        me, sib = (x, y, c), (x, y, 1 - c)
        chips = [(1 - x, y), (x, 1 - y), (1 - x, 1 - y)]

        def copy(t, k, blk, to, src=None):
            dst = o_refs[t].at[_slot(blk)]
            return pltpu.make_async_remote_copy(
                src_ref=dst if src is None else src, dst_ref=dst, send_sem=send_sems.at[7 * t + k],
                recv_sem=recv_sems.at[7 * t + k], device_id=to, device_id_type=MESH_T)

        mine = [pltpu.make_async_copy(x_refs[t], o_refs[t].at[_slot(me)], local_sems.at[t]) for t in range(n)]
        for cp in mine:
            cp.start()
        first = []
        for t in range(n):
            first.append(copy(t, 0, me, sib, src=x_refs[t]))
            first += [copy(t, 1 + j, me, (*chip, c), src=x_refs[t]) for j, chip in enumerate(chips)]
        for cp in first:
            cp.start()
        passed = []
        for j, chip in enumerate(chips):
            for t in range(n):
                copy(t, 1 + j, (*chip, c), me).wait_recv()
                cp = copy(t, 4 + j, (*chip, c), sib)
                cp.start()
                passed.append(cp)
        for t in range(n):
            copy(t, 0, sib, me).wait_recv()
            for j, chip in enumerate(chips):
                copy(t, 4 + j, (*chip, 1 - c), me).wait_recv()
        for cp in first + passed:
            cp.wait_send()
        for cp in mine:
            cp.wait()

    return pl.pallas_call(
        body, name=name, out_shape=[_sds((NDEV,) + a.shape, a.dtype) for a in xs],
        in_specs=[_any()] * n, out_specs=[_any()] * n,
        scratch_shapes=[pltpu.SemaphoreType.DMA((7 * n,)), pltpu.SemaphoreType.DMA((7 * n,)),
                        pltpu.SemaphoreType.DMA((n,))],
        interpret=_INTERPRET)(*xs)


def _a2a_sems(n):
    return [pltpu.SemaphoreType.DMA((7 * n,)), pltpu.SemaphoreType.DMA((7 * n,)), pltpu.SemaphoreType.DMA((n,))]


def _a2a_copies(x_refs, o_refs, send_sems, recv_sems, local_sems):
    n = len(x_refs)
    x, y, c = _place()
    me = (x, y, c)
    flip = lambda v, b: (1 - v) if b else v
    peers = [(flip(x, k >> 2 & 1), flip(y, k >> 1 & 1), flip(c, k & 1)) for k in range(1, NDEV)]
    mine = [pltpu.make_async_copy(x_refs[t].at[_slot(me)], o_refs[t].at[_slot(me)], local_sems.at[t])
            for t in range(n)]

    def copy(t, k, src_slot, dst_slot, to):
        return pltpu.make_async_remote_copy(
            src_ref=x_refs[t].at[src_slot], dst_ref=o_refs[t].at[dst_slot], send_sem=send_sems.at[7 * t + k],
            recv_sem=recv_sems.at[7 * t + k], device_id=to, device_id_type=MESH_T)

    sends = [copy(t, k, _slot(p), _slot(me), p) for t in range(n) for k, p in enumerate(peers)]
    recvs = [copy(t, k, _slot(p), _slot(p), me) for t in range(n) for k, p in enumerate(peers)]
    return mine, sends, recvs


def _ag_copies(x_refs, o_refs, send_sems, recv_sems, local_sems):
    n = len(x_refs)
    x, y, c = _place()
    me = (x, y, c)
    flip = lambda v, b: (1 - v) if b else v
    peers = [(flip(x, k >> 2 & 1), flip(y, k >> 1 & 1), flip(c, k & 1)) for k in range(1, NDEV)]
    mine = [pltpu.make_async_copy(x_refs[t], o_refs[t].at[_slot(me)], local_sems.at[t]) for t in range(n)]

    def copy(t, k, dst_slot, to):
        return pltpu.make_async_remote_copy(
            src_ref=x_refs[t], dst_ref=o_refs[t].at[dst_slot], send_sem=send_sems.at[7 * t + k],
            recv_sem=recv_sems.at[7 * t + k], device_id=to, device_id_type=MESH_T)

    sends = [copy(t, k, _slot(me), p) for t in range(n) for k, p in enumerate(peers)]
    recvs = [copy(t, k, _slot(p), me) for t in range(n) for k, p in enumerate(peers)]
    return mine, sends, recvs


def _ag_start(x_refs, o_refs, send_sems, recv_sems, local_sems):
    mine, sends, _ = _ag_copies(x_refs, o_refs, send_sems, recv_sems, local_sems)
    for cp in mine + sends:
        cp.start()


def _ag_wait(x_refs, o_refs, send_sems, recv_sems, local_sems):
    mine, sends, recvs = _ag_copies(x_refs, o_refs, send_sems, recv_sems, local_sems)
    for cp in recvs:
        cp.wait_recv()
    for cp in sends:
        cp.wait_send()
    for cp in mine:
        cp.wait()


def _a2a_start(x_refs, o_refs, send_sems, recv_sems, local_sems):
    mine, sends, _ = _a2a_copies(x_refs, o_refs, send_sems, recv_sems, local_sems)
    for cp in mine + sends:
        cp.start()


def _a2a_wait(x_refs, o_refs, send_sems, recv_sems, local_sems):
    mine, sends, recvs = _a2a_copies(x_refs, o_refs, send_sems, recv_sems, local_sems)
    for cp in recvs:
        cp.wait_recv()
    for cp in sends:
        cp.wait_send()
    for cp in mine:
        cp.wait()


def adam_reduce(P, w, m, v, name, sends=()):
    n, R, C = P.shape
    br = R // 4 if R % 64 == 0 else R

    def body(p_ref, w_ref, m_ref, v_ref, g_o, d_o, m_o, v_o):
        g = p_ref[0].astype(F32)
        for k in range(1, n):
            g = g + p_ref[k].astype(F32)
        m1 = ADAM_B1 * m_ref[...] + (1.0 - ADAM_B1) * g
        v1 = ADAM_B2 * v_ref[...] + (1.0 - ADAM_B2) * jnp.square(g)
        m_hat = m1 / (1.0 - ADAM_B1 ** ADAM_STEP)
        v_hat = v1 / (1.0 - ADAM_B2 ** ADAM_STEP)
        g_o[...] = g
        d_o[...] = -ADAM_LR * (m_hat / (jnp.sqrt(v_hat) + ADAM_EPS) + ADAM_WD * w_ref[...])
        m_o[...] = m1
        v_o[...] = v1

    blk = pl.BlockSpec((br, C), lambda i: (i, 0))
    return _pc(body, name, [_sds((R, C))] * 4, grid=(R // br,),
               in_specs=[pl.BlockSpec((n, br, C), lambda i: (0, i, 0)), blk, blk, blk], out_specs=[blk] * 4,
               sends=sends)(P, w, m, v)


def adam_layers(P0, P1, w, m, v, name, sends=()):
    n, R, C = P0.shape
    br = R // 4 if R % 64 == 0 else R
    nb = R // br

    def body(p0_ref, p1_ref, w_ref, m_ref, v_ref, g_o, d_o, m_o, v_o):
        def total(p_ref):
            g = p_ref[0].astype(F32)
            for k in range(1, n):
                g = g + p_ref[k].astype(F32)
            return g

        g = jnp.where(pl.program_id(0) == 0, total(p0_ref), total(p1_ref))
        m1 = ADAM_B1 * m_ref[0] + (1.0 - ADAM_B1) * g
        v1 = ADAM_B2 * v_ref[0] + (1.0 - ADAM_B2) * jnp.square(g)
        m_hat = m1 / (1.0 - ADAM_B1 ** ADAM_STEP)
        v_hat = v1 / (1.0 - ADAM_B2 ** ADAM_STEP)
        g_o[0] = g
        d_o[0] = -ADAM_LR * (m_hat / (jnp.sqrt(v_hat) + ADAM_EPS) + ADAM_WD * w_ref[0])
        m_o[0] = m1
        v_o[0] = v1

    blk = pl.BlockSpec((1, br, C), lambda l, i: (l, i, 0))
    p0 = pl.BlockSpec((n, br, C), lambda l, i: (0, jnp.where(l == 0, i, nb - 1), 0))
    p1 = pl.BlockSpec((n, br, C), lambda l, i: (0, jnp.where(l == 1, i, 0), 0))
    return _pc(body, name, [_sds((2, R, C))] * 4, grid=(2, nb), in_specs=[p0, p1, blk, blk, blk],
               out_specs=[blk] * 4, sends=sends)(P0, P1, w, m, v)


def mod_fwd(scin, wmod, bcol):
    def body(s_ref, w_ref, b_ref, o_ref):
        o_ref[0] = mm(_silu(s_ref[...]), w_ref[0]) + b_ref[0]

    return _pc(body, "mod_fwd", _sds((2, 16, 768)), grid=(2,),
               in_specs=[pl.BlockSpec((16, D), lambda l: (0, 0)), pl.BlockSpec((1, D, 768), lambda l: (l, 0, 0)),
                         pl.BlockSpec((1, 1, 768), lambda l: (l, 0, 0))],
               out_specs=pl.BlockSpec((1, 16, 768), lambda l: (l, 0, 0)))(scin, wmod, bcol)


def mod_bwd(scin, wmod, G):
    def body(s_ref, w_ref, g_ref, dw_o, ds_o):
        _, vjp = jax.vjp(lambda s, w: mm(_silu(s), w), s_ref[...], w_ref[0])
        ds, dw = vjp(g_ref[0])
        dw_o[0] = dw
        _acc_init(pl.program_id(0) == 0, [ds_o])
        ds_o[...] += ds

    full = pl.BlockSpec((16, D), lambda l: (0, 0))
    wsp = pl.BlockSpec((1, D, 768), lambda l: (l, 0, 0))
    return _pc(body, "mod_bwd", [_sds((2, D, 768)), _sds((16, D))], grid=(2,),
               in_specs=[full, wsp, pl.BlockSpec((1, 16, 768), lambda l: (l, 0, 0))], out_specs=[wsp, full])(
        scin, wmod, G)


_SMALL = ["b_mod", "g_mix", "wa_sink", "na_rpb", "ssm_conv_w", "ssm_conv_b", "ssm_dt_bias", "ssm_a_log", "ssm_d",
          "ssm_norm_g", "g_ffn", "g_final", "dmod_s", "dmod_c", "loss"]


def _pack(parts):
    rows = []
    for a in parts:
        f = a.reshape(-1).astype(F32)
        rows.append(jnp.pad(f, (0, (-f.shape[0]) % 1024)).reshape(-1, 128))
    return jnp.concatenate(rows, axis=0)


def _unpack(packed, shapes):
    out, r = [], 0
    for s in shapes:
        nel = int(np.prod(s))
        nr = -(-nel // 1024) * 8
        out.append(packed[r:r + nr].reshape(-1)[:nel].reshape(s))
        r += nr
    return out


def kernel(x, c, ctx, c_ctx, w_mod, b_mod, g_mix, w_in, wa_sink, na_rpb, ssm_conv_w, ssm_conv_b, ssm_dt_bias, ssm_a_log, ssm_d, ssm_norm_g, w_out, g_ffn, w_ffn_in, w_ffn_out, g_final, loss_target, m_c_ctx, m_w_mod, m_b_mod, m_g_mix, m_w_in, m_wa_sink, m_na_rpb, m_ssm_conv_w, m_ssm_conv_b, m_ssm_dt_bias, m_ssm_a_log, m_ssm_d, m_ssm_norm_g, m_w_out, m_g_ffn, m_w_ffn_in, m_w_ffn_out, m_g_final, v_c_ctx, v_w_mod, v_b_mod, v_g_mix, v_w_in, v_wa_sink, v_na_rpb, v_ssm_conv_w, v_ssm_conv_b, v_ssm_dt_bias, v_ssm_a_log, v_ssm_d, v_ssm_norm_g, v_w_out, v_g_ffn, v_w_ffn_in, v_w_ffn_out, v_g_final):
    L = x.shape[1]
    px, py, pc = _place()
    me = 4 * px + 2 * py + pc
    W = dict(c_ctx=c_ctx, w_mod=w_mod, b_mod=b_mod, g_mix=g_mix, w_in=w_in, wa_sink=wa_sink, na_rpb=na_rpb,
             ssm_conv_w=ssm_conv_w, ssm_conv_b=ssm_conv_b, ssm_dt_bias=ssm_dt_bias, ssm_a_log=ssm_a_log, ssm_d=ssm_d,
             ssm_norm_g=ssm_norm_g, w_out=w_out, g_ffn=g_ffn, w_ffn_in=w_ffn_in, w_ffn_out=w_ffn_out, g_final=g_final)
    M = dict(c_ctx=m_c_ctx, w_mod=m_w_mod, b_mod=m_b_mod, g_mix=m_g_mix, w_in=m_w_in, wa_sink=m_wa_sink,
             na_rpb=m_na_rpb, ssm_conv_w=m_ssm_conv_w, ssm_conv_b=m_ssm_conv_b, ssm_dt_bias=m_ssm_dt_bias,
             ssm_a_log=m_ssm_a_log, ssm_d=m_ssm_d, ssm_norm_g=m_ssm_norm_g, w_out=m_w_out, g_ffn=m_g_ffn,
             w_ffn_in=m_w_ffn_in, w_ffn_out=m_w_ffn_out, g_final=m_g_final)
    V = dict(c_ctx=v_c_ctx, w_mod=v_w_mod, b_mod=v_b_mod, g_mix=v_g_mix, w_in=v_w_in, wa_sink=v_wa_sink,
             na_rpb=v_na_rpb, ssm_conv_w=v_ssm_conv_w, ssm_conv_b=v_ssm_conv_b, ssm_dt_bias=v_ssm_dt_bias,
             ssm_a_log=v_ssm_a_log, ssm_d=v_ssm_d, ssm_norm_g=v_ssm_norm_g, w_out=v_w_out, g_ffn=v_g_ffn,
             w_ffn_in=v_w_ffn_in, w_ffn_out=v_w_ffn_out, g_final=v_g_final)

    tr = lambda a: a.transpose(0, 2, 1)
    shards = dict(w_in=tr(w_in).astype(MXU), w_out=w_out.astype(MXU), w_ffn_in=tr(w_ffn_in).astype(MXU),
                  w_ffn_out=w_ffn_out.astype(MXU))
    c_all, conv_all, shards["w_in_first"] = all_gather([c, ssm_conv_w, shards["w_in"][0]], "gather_first")
    conv_f = conv_all.transpose(1, 2, 0, 3).reshape(2, 7, 1024)

    scin = jnp.concatenate([c_all.reshape(NDEV, D), c_ctx.reshape(1, D), jnp.zeros((7, D), F32)], axis=0)
    bcol = lax.dynamic_slice_in_dim(b_mod, me * 768, 768, axis=1).reshape(2, 1, 768)
    mod_all, = all_gather([mod_fwd(scin, w_mod, bcol)], "gather_mod")
    mod_rows = mod_all.transpose(1, 2, 0, 3).reshape(2, 16, 6 * D)
    mods = jnp.stack([lax.dynamic_index_in_dim(mod_rows, me, axis=1, keepdims=False), mod_rows[:, 8]], axis=1)

    layers = [dict(g_mix=g_mix[i], wa_sink=wa_sink[i], na_rpb=na_rpb[i], ssm_conv_w=conv_f[i],
                   ssm_conv_b=ssm_conv_b[i], ssm_dt_bias=ssm_dt_bias[i], ssm_a_log=ssm_a_log[i], ssm_d=ssm_d[i],
                   ssm_norm_g=ssm_norm_g[i], g_ffn=g_ffn[i]) for i in range(2)]
    loss, dx, grads, dmods, dgfin, gin0 = local_step(x[0], ctx[0], loss_target[0], mods, layers, shards, g_final, L)

    stk = lambda n: jnp.stack([grads[0][n], grads[1][n]])
    small = dict(b_mod=dmods[:, 0] + dmods[:, 1], g_final=dgfin, dmod_s=dmods[:, 0], dmod_c=dmods[:, 1],
                 loss=loss.reshape(1))
    for nme in _SMALL:
        if nme not in small:
            small[nme] = stk(nme)
    shapes = [small[nme].shape for nme in _SMALL]
    zero_like = lambda nme: jnp.zeros(small[nme].shape, F32)
    own = lambda S, nme: S[nme] if (nme in S and S[nme].shape == small[nme].shape) else zero_like(nme)
    gath, = all_gather([_pack([small[nme] for nme in _SMALL])], "gather_grads")
    sm = adam_reduce(gath, _pack([own(W, nme) for nme in _SMALL]), _pack([own(M, nme) for nme in _SMALL]),
                     _pack([own(V, nme) for nme in _SMALL]), "adam_small")
    res = {nme: vals for nme, vals in zip(_SMALL, zip(*[_unpack(a, shapes) for a in sm]))}
    loss = res["loss"][0][0]

    cols = lambda a: lax.dynamic_slice_in_dim(a, me * 768, 768, axis=-1)
    rows_of = lambda s: -(-int(np.prod(s)) // 1024) * 8
    r0 = sum(rows_of(s) for s in shapes[:_SMALL.index("dmod_s")])
    dmod_s_all = gath[:, r0:r0 + rows_of(small["dmod_s"].shape)].reshape(NDEV, 2, 6 * D).transpose(1, 0, 2)
    G = jnp.concatenate([cols(dmod_s_all), cols(res["dmod_c"][0])[:, None, :], jnp.zeros((2, 7, 768), F32)], axis=1)
    dwmod, dscin = mod_bwd(scin, w_mod, G)
    cc_g, = all_gather([dscin[8].reshape(8, 128)], "gather_cctx")
    out = {}
    out["c_ctx"] = [a.reshape(D) for a in adam_reduce(cc_g, c_ctx.reshape(8, 128), m_c_ctx.reshape(8, 128),
                                                      v_c_ctx.reshape(8, 128), "adam_cctx")]
    res_wmod, (got1,) = adam_reduce(dwmod.reshape(1, 2 * D, 768), w_mod.reshape(2 * D, 768),
                                    m_w_mod.reshape(2 * D, 768), v_w_mod.reshape(2 * D, 768), "adam_wmod",
                                    sends=(gin0[1],))
    out["w_mod"] = [a.reshape(2, D, 768) for a in res_wmod]
    gconv = lax.dynamic_slice_in_dim(res["ssm_conv_w"][0], me * 128, 128, axis=2)
    out["ssm_conv_w"] = [a.reshape(2, 7, 128) for a in adam_reduce(
        gconv.reshape(1, 14, 128), ssm_conv_w.reshape(14, 128), m_ssm_conv_w.reshape(14, 128),
        v_ssm_conv_w.reshape(14, 128), "adam_conv")]
    for nme in _SMALL:
        if nme not in ("ssm_conv_w", "dmod_s", "dmod_c", "loss"):
            out[nme] = list(res[nme])

    adam_big = lambda nme, t, **kw: adam_layers(grads[0][nme], grads[1][nme], t(W[nme]), t(M[nme]), t(V[nme]),
                                                "adam_" + nme, **kw)
    same = lambda a: a
    res_fi, (got0,) = adam_big("w_ffn_in", tr, sends=(gin0[0],))
    grads[0]["w_in"] = jnp.concatenate([got0, got1], axis=2)
    out["w_ffn_in"] = [tr(a) for a in res_fi]
    out["w_ffn_out"] = list(adam_big("w_ffn_out", same))
    out["w_out"] = list(adam_big("w_out", same))
    out["w_in"] = [tr(a) for a in adam_big("w_in", tr)]
    order = ["c_ctx", "w_mod", "b_mod", "g_mix", "w_in", "wa_sink", "na_rpb", "ssm_conv_w", "ssm_conv_b",
             "ssm_dt_bias", "ssm_a_log", "ssm_d", "ssm_norm_g", "w_out", "g_ffn", "w_ffn_in", "w_ffn_out", "g_final"]
    return (loss, dx.reshape(1, L, D), *[out[nme][0] for nme in order], *[out[nme][1] for nme in order],
            *[out[nme][2] for nme in order], *[out[nme][3] for nme in order])
```

```python
import functools

import numpy as np
import jax
import jax.numpy as jnp
from jax import lax
from jax.experimental import pallas as pl
from jax.experimental.pallas import tpu as pltpu

F32 = jnp.float32
MXU = jnp.bfloat16
_INTERPRET = False
VMEM_LIMIT = 60 * 1024 * 1024

D = 1024
LC = 256
GW = 64
HD = 64
EPS = 1e-6
NEG = -1e30
NDEV = 8
Q = 128
NSTATE = 128
DFF = 2816
IN_COLS = 2832
NP_IN = 3072
C_QA, C_QB, C_Z, C_KA, C_VA, C_KB, C_VB, C_XBC, C_DT = 0, 256, 512, 1024, 1152, 1280, 1536, 1792, 2816
ADAM_LR, ADAM_B1, ADAM_B2, ADAM_EPS, ADAM_WD, ADAM_STEP = 0.001, 0.9, 0.999, 1e-08, 0.01, 10
MESH_T = pl.DeviceIdType.MESH


def _dg(a, b, ca, cb):
    return lax.dot_general(a.astype(MXU), b.astype(MXU), (((ca,), (cb,)), ((), ())), preferred_element_type=F32)


@jax.custom_vjp
def mm(a, b):
    return _dg(a, b, 1, 0)


def _mm_f(a, b):
    return _dg(a, b, 1, 0), (a, b)


def _mm_b(res, g):
    a, b = res
    return _dg(g, b, 1, 1).astype(a.dtype), _dg(a, g, 0, 0).astype(b.dtype)


mm.defvjp(_mm_f, _mm_b)


@jax.custom_vjp
def mm_nt(a, b):
    return _dg(a, b, 1, 1)


def _mmnt_f(a, b):
    return _dg(a, b, 1, 1), (a, b)


def _mmnt_b(res, g):
    a, b = res
    return _dg(g, b, 1, 0).astype(a.dtype), _dg(g, a, 0, 0).astype(b.dtype)


mm_nt.defvjp(_mmnt_f, _mmnt_b)


@jax.custom_vjp
def mm_tn(a, b):
    return _dg(a, b, 0, 0)


def _mmtn_f(a, b):
    return _dg(a, b, 0, 0), (a, b)


def _mmtn_b(res, g):
    a, b = res
    return _dg(b, g, 1, 1).astype(a.dtype), _dg(a, g, 1, 0).astype(b.dtype)


mm_tn.defvjp(_mmtn_f, _mmtn_b)


@jax.custom_vjp
def mmw(a, w):
    return _dg(a, w, 1, 0)


mmw.defvjp(lambda a, w: (_dg(a, w, 1, 0), w), lambda w, g: (_dg(g, w, 1, 1), None))


@jax.custom_vjp
def mmw_nt(a, w):
    return _dg(a, w, 1, 1)


mmw_nt.defvjp(lambda a, w: (_dg(a, w, 1, 1), w), lambda w, g: (_dg(g, w, 1, 0), None))


def _exact(a, b):
    return lax.dot_general(a, b, (((1,), (0,)), ((), ())), precision=lax.Precision.HIGHEST,
                           preferred_element_type=F32)


def _pc(body, name, out_shape, grid=None, in_specs=None, out_specs=None, scratch=(), sends=(), gather=False):
    params = pltpu.CompilerParams(vmem_limit_bytes=VMEM_LIMIT)
    if sends and not isinstance(out_shape, (list, tuple)):
        out_shape, out_specs = [out_shape], [out_specs]
    start, wait = (_ag_start, _ag_wait) if gather else (_a2a_start, _a2a_wait)
    if not sends:
        kw = {}
        if grid is not None:
            kw = dict(grid=grid, in_specs=in_specs, out_specs=out_specs)
        elif in_specs is not None:
            kw = dict(in_specs=in_specs, out_specs=out_specs)
        return pl.pallas_call(body, name=name, out_shape=out_shape, scratch_shapes=list(scratch),
                              compiler_params=params, interpret=_INTERPRET, **kw)
    n, nin, nout, nscr = len(sends), len(in_specs), len(out_shape), len(scratch)

    def body2(*refs):
        cin, xs = refs[:nin], refs[nin:nin + n]
        couts, os_ = refs[nin + n:nin + n + nout], refs[nin + n + nout:nin + 2 * n + nout]
        cscr, sems = refs[nin + 2 * n + nout:nin + 2 * n + nout + nscr], refs[nin + 2 * n + nout + nscr:]
        ids = [pl.program_id(a) for a in range(len(grid))]
        first = functools.reduce(lambda a, b: a & b, [i == 0 for i in ids])
        last = functools.reduce(lambda a, b: a & b, [i == g - 1 for i, g in zip(ids, grid)])

        @pl.when(first)
        def _():
            start(xs, os_, *sems)

        body(*cin, *couts, *cscr)

        @pl.when(last)
        def _():
            wait(xs, os_, *sems)

    call = pl.pallas_call(
        body2, name=name,
        out_shape=list(out_shape) + [_sds(((NDEV,) if gather else ()) + a.shape, a.dtype) for a in sends],
        grid=grid, in_specs=list(in_specs) + [_any()] * n, out_specs=list(out_specs) + [_any()] * n,
        scratch_shapes=list(scratch) + _a2a_sems(n), compiler_params=params, interpret=_INTERPRET)

    def run(*args):
        res = call(*args, *sends)
        return res[:nout], res[nout:]

    return run


def _vm():
    return pl.BlockSpec(memory_space=pltpu.VMEM)


def _sds(shape, dt=F32):
    return jax.ShapeDtypeStruct(shape, dt)


def _iota(shape, dim):
    return lax.broadcasted_iota(jnp.int32, shape, dim)


def _silu(x):
    return x * jax.nn.sigmoid(x)


def _softplus(x):
    return jnp.maximum(x, 0.0) + jnp.log1p(jnp.exp(-jnp.abs(x)))


def _normmod(x, g, sh, sc):
    r = lax.rsqrt(jnp.mean(x * x, axis=-1, keepdims=True) + EPS)
    return (x * r * g) * (1.0 + sc) + sh


def _rope(x, cos, sin, rm):
    return x * cos + _exact(x, rm) * sin


def _swap12(x):
    lane = _iota(x.shape, 1)
    up, down = pltpu.roll(x, 192, 1), pltpu.roll(x, 64, 1)
    return jnp.where((lane >= 64) & (lane < 128), up, jnp.where((lane >= 128) & (lane < 192), down, x))


def _acc_init(first, refs):
    @pl.when(first)
    def _():
        for r in refs:
            r[...] = jnp.zeros_like(r)


def _stream(X, TR, nlt):
    if not isinstance(X, tuple):
        return (X,), [pl.BlockSpec((TR, D), lambda i: (i, 0))], lambda refs: refs[0][...]
    specs = [pl.BlockSpec((TR, D), lambda i: (jnp.minimum(i, nlt - 1), 0)), pl.BlockSpec((TR, D), lambda i: (0, 0))]
    return X, specs, lambda refs: jnp.where(pl.program_id(0) < nlt, refs[0][...], refs[1][...])


def in_fwd(X, g, sh, sc, W, cos, sin, rm, L, sends=()):
    T = L + LC
    TR = 256
    nlt = L // TR
    xs, xspecs, xread = _stream(X, TR, nlt)

    def body(*refs):
        (g_ref, sh_ref, sc_ref, w_ref, cos_ref, sin_ref, rm_ref,
         qa, qb, z, ka, va, kb, vb, xbc, dt, hout) = refs[len(xs):]
        h = _normmod(xread(refs), g_ref[...], sh_ref[0], sc_ref[0]).astype(MXU)
        hout[...] = h
        y = lax.dot_general(h, w_ref[...], (((1,), (1,)), ((), ())), preferred_element_type=F32)
        cs, sn, r = cos_ref[...], sin_ref[...], rm_ref[...]
        qa[...] = _rope(_swap12(y[:, C_QA:C_QB]), cs, sn, r).astype(MXU)
        qb[...] = y[:, C_QB:C_Z].astype(MXU)
        z[...] = y[:, C_Z:C_KA]
        ka[...] = _rope(y[:, C_KA:C_VA], cs[:, :128], sn[:, :128], r[:128, :128]).astype(MXU)
        va[...] = y[:, C_VA:C_KB].astype(MXU)
        kb[...] = y[:, C_KB:C_VB].astype(MXU)
        vb[...] = y[:, C_VB:C_XBC].astype(MXU)
        xbc[...] = y[:, C_XBC:C_DT]
        dt[...] = y[:, C_DT:C_DT + 128]

    row = lambda w: pl.BlockSpec((TR, w), lambda i: (i, 0))
    cls = pl.BlockSpec((1, 1, D), lambda i: (i // nlt, 0, 0))
    widths = [(256, MXU), (256, MXU), (512, F32), (128, MXU), (128, MXU), (256, MXU), (256, MXU), (1024, F32),
              (128, F32), (D, MXU)]
    return _pc(body, "in_fwd", [_sds((T, w), d) for w, d in widths], grid=(T // TR,),
               in_specs=xspecs + [pl.BlockSpec((1, D), lambda i: (0, 0)), cls, cls, _vm(), row(256), row(256), _vm()],
               out_specs=[row(w) for w, _ in widths], sends=sends, gather=True)(*xs, g, sh, sc, W, cos, sin, rm)


def in_bwd(X, g, sh, sc, W, cos, sin, rm, dxres, dqa, dqb, dz, dka, dva, dkb, dvb, dxbc, ddt2, L, latent_only):
    T = L + LC
    TR = 256
    nlt = L // TR
    xs, xspecs, xread = _stream(X, TR, nlt)

    def body(*refs):
        (g_ref, sh_ref, sc_ref, w_ref, cos_ref, sin_ref, rm_ref, dxres_ref, dqa_r, dqb_r, dz_r, dka_r,
         dva_r, dkb_r, dvb_r, dxbc_r, ddt0_r, ddt1_r, dx_o, dy_o, dg_o, dsh_o, dsc_o) = refs[len(xs):]
        i = pl.program_id(0)
        cs, sn, r = cos_ref[...], sin_ref[...], rm_ref[...]
        _, vq = jax.vjp(lambda t: _rope(t, cs, sn, r), dqa_r[...])
        _, vk = jax.vjp(lambda t: _rope(t, cs[:, :128], sn[:, :128], r[:128, :128]), dka_r[...])
        dyqa = _swap12(vq(dqa_r[...])[0])
        dyka, = vk(dka_r[...])
        ddt = ddt0_r[0] + ddt1_r[0]
        dy = jnp.concatenate([dyqa, dqb_r[...], dz_r[...], dyka, dva_r[...], dkb_r[...], dvb_r[...], dxbc_r[...],
                              ddt, jnp.zeros((TR, NP_IN - C_DT - 128), F32)], axis=1).astype(MXU)
        dy_o[...] = dy
        dh = jnp.dot(dy, w_ref[...], preferred_element_type=F32)
        _, vp = jax.vjp(_normmod, xread(refs), g_ref[...], sh_ref[0], sc_ref[0])
        dx, dg, dsh, dsc = vp(dh)
        if latent_only:
            @pl.when(i < nlt)
            def _():
                dx_o[...] = dx + dxres_ref[...]
        else:
            dx_o[...] = dx + dxres_ref[...]
        _acc_init(i == 0, [dg_o])
        _acc_init((i == 0) | (i == nlt), [dsh_o, dsc_o])
        dg_o[...] += dg
        dsh_o[0] += dsh
        dsc_o[0] += dsc

    row = lambda w: pl.BlockSpec((TR, w), lambda i: (i, 0))
    cls = pl.BlockSpec((1, 1, D), lambda i: (i // nlt, 0, 0))
    vec = pl.BlockSpec((1, D), lambda i: (0, 0))
    dts = lambda d: pl.BlockSpec((1, TR, 128), lambda i: (d, i, 0))
    dxs = pl.BlockSpec((TR, D), lambda i: (jnp.minimum(i, nlt - 1), 0)) if latent_only else row(D)
    return _pc(body, "in_bwd",
               [_sds((L if latent_only else T, D)), _sds((T, NP_IN), MXU), _sds((1, D)), _sds((2, 1, D)),
                _sds((2, 1, D))],
               grid=(T // TR,),
               in_specs=xspecs + [vec, cls, cls, _vm(), row(256), row(256), _vm(), row(D), row(256), row(256),
                                  row(512), row(128), row(128), row(256), row(256), row(1024), dts(0), dts(1)],
               out_specs=[dxs, row(NP_IN), vec, cls, cls])(
        *xs, g, sh, sc, W, cos, sin, rm, dxres, dqa, dqb, dz, dka, dva, dkb, dvb, dxbc, ddt2, ddt2)


def tn_mm(A, G, bk, bn, out_dtype, ncol=None, col0=0):
    T, K = A.shape
    N = G.shape[1] if ncol is None else ncol
    first = col0 * (N // bn)
    bt = T
    nt = T // bt

    def body(a_ref, g_ref, o_ref, acc):
        t = pl.program_id(2)
        _acc_init(t == 0, [acc])
        acc[...] += lax.dot_general(a_ref[...], g_ref[...], (((0,), (0,)), ((), ())), preferred_element_type=F32)

        @pl.when(t == nt - 1)
        def _():
            o_ref[...] = acc[...].astype(out_dtype)

    return _pc(body, "tn_mm", _sds((K, N), out_dtype), grid=(K // bk, N // bn, nt),
               in_specs=[pl.BlockSpec((bt, bk), lambda k, n, t: (t, k)),
                         pl.BlockSpec((bt, bn), lambda k, n, t: (t, first + n))],
               out_specs=pl.BlockSpec((bk, bn), lambda k, n, t: (k, n)),
               scratch=[pltpu.VMEM((bk, bn), F32)])(A, G)


def _ssm_out(yf, yb, xs, z, dsk, gs):
    y = (yf + yb + dsk * xs) * _silu(z)
    r = lax.rsqrt(jnp.mean(y * y, axis=-1, keepdims=True) + EPS)
    return y * r * gs


def out_fwd(oa, ob, y2, act, z, dsk, gs, W, X, gate, L, sends=()):
    T = L + LC
    TR = 256
    nlt = L // TR
    xs, xspecs, xread = _stream(X, TR, nlt)

    def body(*refs):
        oa_r, ob_r, yf_r, yb_r, xs_r, z_r, dsk_r, gs_r, w_ref, gt_ref, x1_o, cat_o = refs[len(xs):]
        oc = _ssm_out(yf_r[0], yb_r[0], xs_r[...], z_r[...], dsk_r[...], gs_r[...])
        cat = jnp.concatenate([_swap12(oa_r[...]), ob_r[...], oc], axis=1).astype(MXU)
        cat_o[...] = cat
        x1_o[...] = xread(refs) + gt_ref[0] * jnp.dot(cat, w_ref[...], preferred_element_type=F32)

    row = lambda w: pl.BlockSpec((TR, w), lambda i: (i, 0))
    ys = lambda d: pl.BlockSpec((1, TR, 512), lambda i: (d, i, 0))
    cls = pl.BlockSpec((1, 1, D), lambda i: (i // nlt, 0, 0))
    v512 = pl.BlockSpec((1, 512), lambda i: (0, 0))
    return _pc(body, "out_fwd", [_sds((T, D)), _sds((T, D), MXU)], grid=(T // TR,),
               in_specs=xspecs + [row(256), row(256), ys(0), ys(1), row(512), row(512), v512, v512, _vm(), cls],
               out_specs=[row(D), row(D)], sends=sends, gather=True)(*xs, oa, ob, y2, y2, act, z, dsk, gs, W, gate)


def out_bwd(oa, ob, y2, act, z, dsk, gs, W, gate, dX1, L):
    T = dX1.shape[0]
    TR = 256
    nlt = L // TR

    def body(oa_r, ob_r, yf_r, yb_r, xs_r, z_r, dsk_r, gs_r, w_ref, gt_ref, dx1_r,
             doa_o, dob_o, dy_o, dxs_o, dz_o, dmix_o, ddsk_o, dgs_o, dgt_o):
        i = pl.program_id(0)
        w = w_ref[...]

        def f(oa_, ob_, yf, yb, xs, z_, dsk_, gs_, gt):
            oc = _ssm_out(yf, yb, xs, z_, dsk_, gs_)
            return gt * mmw(jnp.concatenate([oa_, ob_, oc], axis=1), w)

        _, vjp = jax.vjp(f, _swap12(oa_r[...]), ob_r[...], yf_r[0], yb_r[0], xs_r[...], z_r[...], dsk_r[...],
                         gs_r[...], gt_ref[0])
        dx1 = dx1_r[...]
        doa, dob, dyf, _, dxs, dz, ddsk, dgs, dgt = vjp(dx1)
        doa_o[...] = _swap12(doa)
        dob_o[...] = dob
        dy_o[...] = dyf
        dxs_o[...] = dxs
        dz_o[...] = dz
        dmix_o[...] = (gt_ref[0] * dx1).astype(MXU)
        _acc_init(i == 0, [ddsk_o, dgs_o])
        _acc_init((i == 0) | (i == nlt), [dgt_o])
        ddsk_o[...] += ddsk
        dgs_o[...] += dgs
        dgt_o[0] += dgt

    row = lambda w: pl.BlockSpec((TR, w), lambda i: (i, 0))
    ys = lambda d: pl.BlockSpec((1, TR, 512), lambda i: (d, i, 0))
    cls = pl.BlockSpec((1, 1, D), lambda i: (i // nlt, 0, 0))
    v512 = pl.BlockSpec((1, 512), lambda i: (0, 0))
    return _pc(body, "out_bwd",
               [_sds((T, 256)), _sds((T, 256)), _sds((T, 512)), _sds((T, 512)), _sds((T, 512)), _sds((T, D), MXU),
                _sds((1, 512)), _sds((1, 512)), _sds((2, 1, D))],
               grid=(T // TR,),
               in_specs=[row(256), row(256), ys(0), ys(1), row(512), row(512), v512, v512, _vm(), cls, row(D)],
               out_specs=[row(256), row(256), row(512), row(512), row(512), row(D), v512, v512, cls])(
        oa, ob, y2, y2, act, z, dsk, gs, W, gate, dX1)


def ffn_fwd(X, g, sh, sc, gate, Win, Wout, L, sends=()):
    T = X.shape[0]
    TR = 256
    nlt = L // TR

    def body(x_ref, g_ref, sh_ref, sc_ref, gt_ref, wi_ref, wo_ref, o_ref, f_ref):
        h = _normmod(x_ref[...], g_ref[...], sh_ref[0], sc_ref[0]).astype(MXU)
        nt = (((1,), (1,)), ((), ()))
        a = lax.dot_general(h, wi_ref[0:DFF, :], nt, preferred_element_type=F32)
        u = lax.dot_general(h, wi_ref[DFF:2 * DFF, :], nt, preferred_element_type=F32)
        act = (_silu(a) * u).astype(MXU)
        ff = jnp.dot(act, wo_ref[...], preferred_element_type=F32)
        f_ref[...] = ff
        o_ref[...] = x_ref[...] + gt_ref[0] * ff

    row = lambda w: pl.BlockSpec((TR, w), lambda i: (i, 0))
    cls = pl.BlockSpec((1, 1, D), lambda i: (i // nlt, 0, 0))
    vec = pl.BlockSpec((1, D), lambda i: (0, 0))
    return _pc(body, "ffn_fwd", [_sds((T, D)), _sds((T, D))], grid=(T // TR,),
               in_specs=[row(D), vec, cls, cls, cls, _vm(), _vm()], out_specs=[row(D), row(D)], sends=sends,
               gather=True)(X, g, sh, sc, gate, Win, Wout)


def ffn_bwd(X, g, sh, sc, gate, Win, Wout, FF, dX2, L, sends=(), nchunk=2):
    T = X.shape[0]
    TR = 256
    nlt = L // TR
    CH = DFF // nchunk

    def body(x_ref, g_ref, sh_ref, sc_ref, gt_ref, wi_ref, wo_ref, ff_r, dx2_r,
             dx_o, h_o, du_o, act_o, dout_o, dg_o, dsh_o, dsc_o, dgt_o):
        i = pl.program_id(0)
        h, vp = jax.vjp(_normmod, x_ref[...], g_ref[...], sh_ref[0], sc_ref[0])
        dx2 = dx2_r[...]
        dout = gt_ref[0] * dx2
        zero = jnp.zeros((TR, CH), F32)
        dh = jnp.zeros((TR, D), F32)
        for c in range(nchunk):
            lo, hi = c * CH, (c + 1) * CH
            wg, wu, wo = wi_ref[lo:hi, :], wi_ref[DFF + lo:DFF + hi, :], wo_ref[lo:hi, :]

            def f(h_, eg, eu):
                act = _silu(mmw_nt(h_, wg) + eg) * (mmw_nt(h_, wu) + eu)
                return mmw(act, wo), act

            _, vjp_c, act = jax.vjp(f, h, zero, zero, has_aux=True)
            dh_c, da, du = vjp_c(dout)
            dh = dh + dh_c
            du_o[:, lo:hi] = da.astype(MXU)
            du_o[:, DFF + lo:DFF + hi] = du.astype(MXU)
            act_o[:, lo:hi] = act.astype(MXU)
        dx, dg, dsh, dsc = vp(dh)
        dx_o[...] = dx + dx2
        h_o[...] = h.astype(MXU)
        dout_o[...] = dout.astype(MXU)
        _acc_init(i == 0, [dg_o])
        _acc_init((i == 0) | (i == nlt), [dsh_o, dsc_o, dgt_o])
        dg_o[...] += dg
        dsh_o[0] += dsh
        dsc_o[0] += dsc
        dgt_o[0] += jnp.sum(dx2 * ff_r[...], axis=0, keepdims=True)

    row = lambda w: pl.BlockSpec((TR, w), lambda i: (i, 0))
    cls = pl.BlockSpec((1, 1, D), lambda i: (i // nlt, 0, 0))
    vec = pl.BlockSpec((1, D), lambda i: (0, 0))
    return _pc(body, "ffn_bwd",
               [_sds((T, D)), _sds((T, D), MXU), _sds((T, 2 * DFF), MXU), _sds((T, DFF), MXU), _sds((T, D), MXU),
                _sds((1, D)), _sds((2, 1, D)), _sds((2, 1, D)), _sds((2, 1, D))],
               grid=(T // TR,),
               in_specs=[row(D), vec, cls, cls, cls, _vm(), _vm(), row(D), row(D)],
               out_specs=[row(D), row(D), row(2 * DFF), row(DFF), row(D), vec, cls, cls, cls], sends=sends)(
        X, g, sh, sc, gate, Win, Wout, FF, dX2)


def loss_head(X2, g, tgt, L):
    T = X2.shape[0]
    TR = 256
    nlt = L // TR

    def body(x_ref, g_ref, t_ref, loss_o, dx_o, dg_o):
        i = pl.program_id(0)
        _acc_init(i == 0, [loss_o, dg_o])

        @pl.when(i < nlt)
        def _():
            def f(x, g_):
                y = x * lax.rsqrt(jnp.mean(x * x, axis=-1, keepdims=True) + EPS) * g_
                return 0.5 * jnp.sum(jnp.mean(jnp.square(y - t_ref[...]), axis=-1, keepdims=True), axis=0,
                                     keepdims=True)

            val, vjp = jax.vjp(f, x_ref[...], g_ref[...])
            dx, dg = vjp(jnp.ones((1, 1), F32))
            dx_o[...] = dx
            loss_o[...] += jnp.broadcast_to(val, (8, 128))
            dg_o[...] += dg

        @pl.when(i >= nlt)
        def _():
            dx_o[...] = jnp.zeros_like(dx_o)

    row = pl.BlockSpec((TR, D), lambda i: (i, 0))
    vec = pl.BlockSpec((1, D), lambda i: (0, 0))
    return _pc(body, "loss_head", [_sds((8, 128)), _sds((T, D)), _sds((1, D))], grid=(T // TR,),
               in_specs=[row, vec, pl.BlockSpec((TR, D), lambda i: (jnp.minimum(i, nlt - 1), 0))],
               out_specs=[pl.BlockSpec((8, 128), lambda i: (0, 0)), row, vec])(X2, g, tgt)


def _stack_impl(q):
    lane = _iota(q.shape, 1)
    return jnp.concatenate([jnp.where(lane < HD, q, 0.0), jnp.where(lane >= HD, q, 0.0)], axis=0)


def _unstack_impl(o):
    M = o.shape[0] // 2
    return jnp.where(_iota((M, o.shape[1]), 1) < HD, o[:M], o[M:])


@jax.custom_vjp
def _stack(q):
    return _stack_impl(q)


_stack.defvjp(lambda q: (_stack_impl(q), None), lambda _, g: (_unstack_impl(g),))


@jax.custom_vjp
def _unstack(o):
    return _unstack_impl(o)


_unstack.defvjp(lambda o: (_unstack_impl(o), None), lambda _, g: (_stack_impl(g),))


def _softmax_av(q, ks, vs, biases, sink):
    q2 = _stack(q)
    ss = []
    for k, b in zip(ks, biases):
        s = mm_nt(q2, k) * (HD ** -0.5)
        ss.append(s if b is None else s + b)
    m = functools.reduce(jnp.maximum, [jnp.max(s, axis=1, keepdims=True) for s in ss])
    if sink is not None:
        m = jnp.maximum(m, sink)
    m = lax.stop_gradient(m)
    es = [jnp.exp(s - m) for s in ss]
    den = functools.reduce(lambda a, b_: a + b_, [jnp.sum(e, axis=1, keepdims=True) for e in es])
    if sink is not None:
        den = den + jnp.exp(sink - m)
    inv = 1.0 / den
    return _unstack(functools.reduce(lambda a, b_: a + b_, [mm(e * inv, v) for e, v in zip(es, vs)]))


def _sink_col(s0, s1, M):
    return jnp.concatenate([jnp.broadcast_to(jnp.mean(s0, axis=1, keepdims=True), (M, 1)),
                            jnp.broadcast_to(jnp.mean(s1, axis=1, keepdims=True), (M, 1))], axis=0)


def _stack4_impl(q):
    lane = _iota((q.shape[0], 128), 1)
    parts = []
    for p in range(2):
        qp = q[:, 128 * p:128 * (p + 1)]
        parts += [jnp.where(lane < HD, qp, 0.0), jnp.where(lane >= HD, qp, 0.0)]
    return jnp.concatenate(parts, axis=0)


def _unstack4_impl(o):
    M = o.shape[0] // 4
    lane = _iota((M, 128), 1)
    return jnp.concatenate([jnp.where(lane < HD, o[0:M], o[M:2 * M]),
                            jnp.where(lane < HD, o[2 * M:3 * M], o[3 * M:4 * M])], axis=1)


@jax.custom_vjp
def _stack4(q):
    return _stack4_impl(q)


_stack4.defvjp(lambda q: (_stack4_impl(q), None), lambda _, g: (_unstack4_impl(g),))


@jax.custom_vjp
def _unstack4(o):
    return _unstack4_impl(o)


_unstack4.defvjp(lambda o: (_unstack4_impl(o), None), lambda _, g: (_stack4_impl(g),))


WA_NB = 4


def _wa_blocks(qs, kws, vws, kx, vx, sks, n0, L):
    sc = HD ** -0.5
    sink = jnp.concatenate([jnp.broadcast_to(jnp.mean(s_, axis=1, keepdims=True), (Q, 1)) for s_ in sks], axis=0)
    bias = []
    for b_ in range(len(qs)):
        n = n0 + b_
        qpos = n * Q + (_iota((4 * Q, 3 * Q), 0) & (Q - 1))
        kpos = (n - 1) * Q + _iota((4 * Q, 3 * Q), 1)
        bias.append(jnp.where((jnp.abs(qpos - kpos) <= Q) & (kpos >= 0) & (kpos < L), 0.0, NEG))
    q4 = [_stack4(q) for q in qs]
    sl = [mm_nt(a, k) * sc + b_ for a, k, b_ in zip(q4, kws, bias)]
    sx = [mm_nt(a, kx) * sc for a in q4]
    m = [lax.stop_gradient(jnp.maximum(jnp.maximum(jnp.max(a, axis=1, keepdims=True),
                                                   jnp.max(b_, axis=1, keepdims=True)), sink))
         for a, b_ in zip(sl, sx)]
    el = [jnp.exp(a - c) for a, c in zip(sl, m)]
    ex = [jnp.exp(a - c) for a, c in zip(sx, m)]
    inv = [1.0 / (jnp.sum(a, axis=1, keepdims=True) + jnp.sum(b_, axis=1, keepdims=True) + jnp.exp(sink - c))
           for a, b_, c in zip(el, ex, m)]
    return [_unstack4(mm(a * i, v) + mm(b_ * i, vx)) for a, b_, i, v in zip(el, ex, inv, vws)]


def _wa_load(q_r, k_r, v_r, n0):
    f = lambda t: t.astype(F32)
    qs = [f(q_r[b_ * Q:(b_ + 1) * Q, :]) for b_ in range(WA_NB)]
    wins = [pl.ds(pl.multiple_of((n0 + b_) * Q, Q), 3 * Q) for b_ in range(WA_NB)]
    return qs, [f(k_r[w, :]) for w in wins], [f(v_r[w, :]) for w in wins], wins


def _wa_specs(L):
    nb = L // Q
    qs = pl.BlockSpec((WA_NB * Q, 256), lambda n: (n, 0))
    kfull = pl.BlockSpec((L + LC + Q, 128), lambda n: (0, 0))
    sks = pl.BlockSpec((2, 2, 1, 128), lambda n: (0, 0, 0, 0))
    return nb, qs, kfull, sks


def wa_fwd(QA, KA, VA, sinkp, L, sends=()):
    nb, qs, kfull, sks = _wa_specs(L)
    pad = lambda a: jnp.concatenate([jnp.zeros((Q, 128), a.dtype), a], axis=0)

    def body(q_r, k_r, v_r, sk_r, o_ref):
        n0 = pl.program_id(0) * WA_NB
        qs_, kws, vws, _ = _wa_load(q_r, k_r, v_r, n0)
        cx = pl.ds(Q + L, LC)
        outs = _wa_blocks(qs_, kws, vws, k_r[cx, :].astype(F32), v_r[cx, :].astype(F32),
                          [sk_r[0, 0], sk_r[0, 1], sk_r[1, 0], sk_r[1, 1]], n0, L)
        o_ref[...] = jnp.concatenate(outs, axis=0)

    return _pc(body, "wa_fwd", _sds((L, 256)), grid=(nb // WA_NB,), in_specs=[qs, kfull, kfull, sks], out_specs=qs,
               sends=sends, gather=True)(QA, pad(KA), pad(VA), sinkp)


def wa_bwd(QA, KA, VA, sinkp, dO, L, sends=()):
    nb, qs, kfull, sks = _wa_specs(L)
    pad = lambda a: jnp.concatenate([jnp.zeros((Q, 128), a.dtype), a], axis=0)

    def body(q_r, k_r, v_r, sk_r, do_r, dq_o, dk_o, dv_o, dsk_o):
        n0 = pl.program_id(0) * WA_NB
        _acc_init(n0 == 0, [dk_o, dv_o, dsk_o])
        qs_, kws, vws, wins = _wa_load(q_r, k_r, v_r, n0)
        cx = pl.ds(Q + L, LC)
        fn = lambda a, b, c, d, e, s_: _wa_blocks(a, b, c, d, e, s_, n0, L)
        _, vjp = jax.vjp(fn, qs_, kws, vws, k_r[cx, :].astype(F32), v_r[cx, :].astype(F32),
                         [sk_r[0, 0], sk_r[0, 1], sk_r[1, 0], sk_r[1, 1]])
        dqs, dkws, dvws, dkx, dvx, ds = vjp([do_r[b_ * Q:(b_ + 1) * Q, :] for b_ in range(WA_NB)])
        dq_o[...] = jnp.concatenate(dqs, axis=0)
        for w, dk, dv in zip(wins, dkws, dvws):
            dk_o[w, :] += dk
            dv_o[w, :] += dv
        dk_o[cx, :] += dkx
        dv_o[cx, :] += dvx
        for i_ in range(4):
            dsk_o[i_ // 2, i_ % 2] += ds[i_]

    return _pc(body, "wa_bwd", [_sds((L, 256)), _sds((L + LC + Q, 128)), _sds((L + LC + Q, 128)),
                                _sds((2, 2, 1, 128))],
               grid=(nb // WA_NB,), in_specs=[qs, kfull, kfull, sks, qs], out_specs=[qs, kfull, kfull, sks],
               sends=sends)(QA, pad(KA), pad(VA), sinkp, dO)


def _ctx_block(q, kx, vx, s0, s1):
    return _softmax_av(q, [kx], [vx], [None], _sink_col(s0, s1, LC))


def ctx_fwd(Qx, Kx, Vx, sinkp, shared, L):
    cq = pl.BlockSpec((LC, 128), lambda p: (L // LC, p))
    ck = pl.BlockSpec((LC, 128), lambda p: (L // LC, 0 if shared else p))
    sks = pl.BlockSpec((1, 2, 1, 128), lambda p: (p, 0, 0, 0))

    def body(q_r, k_r, v_r, sk_r, o_ref):
        f = lambda t: t[...].astype(F32)
        o_ref[...] = _ctx_block(f(q_r), f(k_r), f(v_r), sk_r[0, 0], sk_r[0, 1])

    return _pc(body, "ctx_fwd", _sds((LC, 256)), grid=(2,), in_specs=[cq, ck, ck, sks],
               out_specs=pl.BlockSpec((LC, 128), lambda p: (0, p)))(Qx, Kx, Vx, sinkp)


def ctx_bwd(Qx, Kx, Vx, sinkp, dO, shared, L):
    cq = pl.BlockSpec((LC, 128), lambda p: (L // LC, p))
    ck = pl.BlockSpec((LC, 128), lambda p: (L // LC, 0 if shared else p))
    sks = pl.BlockSpec((1, 2, 1, 128), lambda p: (p, 0, 0, 0))
    op = pl.BlockSpec((LC, 128), lambda p: (0, p))
    ok = pl.BlockSpec((LC, 128), lambda p: (0, 0 if shared else p))
    dos = pl.BlockSpec((LC, 128), lambda p: (L // LC, p))

    def body(q_r, k_r, v_r, sk_r, do_r, dq_o, dk_o, dv_o, dsk_o):
        p = pl.program_id(0)
        f = lambda t: t[...].astype(F32)
        _, vjp = jax.vjp(_ctx_block, f(q_r), f(k_r), f(v_r), sk_r[0, 0], sk_r[0, 1])
        dq, dk, dv, ds0, ds1 = vjp(do_r[...])
        dq_o[...] = dq
        _acc_init((p == 0) if shared else (p >= 0), [dk_o, dv_o])
        dk_o[...] += dk
        dv_o[...] += dv
        dsk_o[0, 0] = ds0
        dsk_o[0, 1] = ds1

    kw = 128 if shared else 256
    return _pc(body, "ctx_bwd", [_sds((LC, 256)), _sds((LC, kw)), _sds((LC, kw)), _sds((2, 2, 1, 128))],
               grid=(2,), in_specs=[cq, ck, ck, sks, dos], out_specs=[op, ok, ok, sks])(Qx, Kx, Vx, sinkp, dO)


def _na_rows(qs, kws, vws, kx, vx, bs):
    sc = HD ** -0.5
    q2 = [_stack(q) for q in qs]
    sl = [mm_nt(a, k) * sc + b for a, k, b in zip(q2, kws, bs)]
    sx = [mm_nt(a, kx) * sc for a in q2]
    m = [lax.stop_gradient(jnp.maximum(jnp.max(a, axis=1, keepdims=True), jnp.max(b, axis=1, keepdims=True)))
         for a, b in zip(sl, sx)]
    el = [jnp.exp(a - c) for a, c in zip(sl, m)]
    ex = [jnp.exp(a - c) for a, c in zip(sx, m)]
    inv = [1.0 / (jnp.sum(a, axis=1, keepdims=True) + jnp.sum(b, axis=1, keepdims=True)) for a, b in zip(el, ex)]
    o2 = [mm(a * i, v) + mm(b * i, vx) for a, b, i, v in zip(el, ex, inv, vws)]
    return [_unstack(o) for o in o2]


NA_ROWS = 16


def _na_geom(r, R):
    s = jnp.clip(r - 4, 0, R - 8)
    cls = jnp.where(r < 4, r, jnp.where(r > R - 4, r - (R - 8), 4))
    return pl.ds(pl.multiple_of(s * GW, GW), 8 * GW), cls


def _na_load(q_r, k_r, v_r, b_r, rb, R):
    nr = min(NA_ROWS, R)
    geo = [_na_geom(rb * nr + j, R) for j in range(nr)]
    qs = [q_r[j * GW:(j + 1) * GW, :].astype(F32) for j in range(nr)]
    kws = [k_r[win, :].astype(F32) for win, _ in geo]
    vws = [v_r[win, :].astype(F32) for win, _ in geo]
    bs = [jnp.concatenate([b_r[0, cls], b_r[1, cls]], axis=0) for _, cls in geo]
    return geo, qs, kws, vws, bs


def na_fwd(QB, KB, VB, biasd, L, sends=()):
    R = L // GW
    nr = min(NA_ROWS, R)
    qs = pl.BlockSpec((nr * GW, 128), lambda p, rb: (rb, p))
    kfull = pl.BlockSpec((L, 128), lambda p, rb: (0, p))
    kctx = pl.BlockSpec((LC, 128), lambda p, rb: (L // LC, p))
    bs = pl.BlockSpec((2, 8, GW, 8 * GW), lambda p, rb: (p, 0, 0, 0))

    def body(q_r, k_r, v_r, kx_r, vx_r, b_r, o_ref):
        _, qs_, kws, vws, bs_ = _na_load(q_r, k_r, v_r, b_r, pl.program_id(1), R)
        outs = _na_rows(qs_, kws, vws, kx_r[...].astype(F32), vx_r[...].astype(F32), bs_)
        o_ref[...] = jnp.concatenate(outs, axis=0)

    return _pc(body, "na_fwd", _sds((L, 256)), grid=(2, R // nr), in_specs=[qs, kfull, kfull, kctx, kctx, bs],
               out_specs=qs, sends=sends, gather=True)(QB, KB, VB, KB, VB, biasd)


def na_bwd(QB, KB, VB, biasd, dO, L):
    R = L // GW
    nr = min(NA_ROWS, R)
    qs = pl.BlockSpec((nr * GW, 128), lambda p, rb: (rb, p))
    kfull = pl.BlockSpec((L, 128), lambda p, rb: (0, p))
    kctx = pl.BlockSpec((LC, 128), lambda p, rb: (L // LC, p))
    bs = pl.BlockSpec((2, 8, GW, 8 * GW), lambda p, rb: (p, 0, 0, 0))
    oc = pl.BlockSpec((LC, 128), lambda p, rb: (0, p))

    def body(q_r, k_r, v_r, kx_r, vx_r, b_r, do_r, dq_o, dk_o, dv_o, dkx_o, dvx_o, db_o):
        rb = pl.program_id(1)
        _acc_init(rb == 0, [dk_o, dv_o, dkx_o, dvx_o, db_o])
        geo, qs_, kws, vws, bs_ = _na_load(q_r, k_r, v_r, b_r, rb, R)
        _, vjp = jax.vjp(_na_rows, qs_, kws, vws, kx_r[...].astype(F32), vx_r[...].astype(F32), bs_)
        dqs, dkws, dvws, dkx, dvx, dbs = vjp([do_r[j * GW:(j + 1) * GW, :] for j in range(nr)])
        dq_o[...] = jnp.concatenate(dqs, axis=0)
        dkx_o[...] += dkx
        dvx_o[...] += dvx
        for j, (win, cls) in enumerate(geo):
            dk_o[win, :] += dkws[j]
            dv_o[win, :] += dvws[j]
            db_o[0, cls] += dbs[j][:GW]
            db_o[1, cls] += dbs[j][GW:]

    return _pc(body, "na_bwd",
               [_sds((L, 256)), _sds((L, 256)), _sds((L, 256)), _sds((LC, 256)), _sds((LC, 256)),
                _sds((4, 8, GW, 8 * GW))],
               grid=(2, R // nr), in_specs=[qs, kfull, kfull, kctx, kctx, bs, qs],
               out_specs=[qs, kfull, kfull, oc, oc, bs])(QB, KB, VB, KB, VB, biasd, dO)


def exact_mm_call(A, B):
    def body(a_ref, b_ref, o_ref):
        o_ref[...] = _exact(a_ref[...], b_ref[...])

    return _pc(body, "exact_mm", _sds((A.shape[0], B.shape[1])))(A, B)


def _conv_shift(x, d, L):
    T = x.shape[0]
    if d == 0:
        return x
    t = _iota(x.shape, 0)
    src = t + d
    ok = (src >= 0) & (src < T) & ((src >= L) == (t >= L))
    return jnp.where(ok, pltpu.roll(x, (-d) % T, 0), 0.0)


def conv_fwd(XBC, w8, b, L, sends=()):
    T = XBC.shape[0]

    def body(x_ref, w_ref, b_ref, o_ref):
        x = x_ref[...]
        pre = b_ref[...] + functools.reduce(
            lambda a, c: a + c, [_conv_shift(x, k - 3, L) * w_ref[k:k + 1, :] for k in range(7)])
        o_ref[...] = _silu(pre)

    col = pl.BlockSpec((T, 128), lambda j: (0, j))
    return _pc(body, "conv_fwd", _sds((T, 1024)), grid=(8,),
               in_specs=[col, pl.BlockSpec((8, 128), lambda j: (0, j)), pl.BlockSpec((1, 128), lambda j: (0, j))],
               out_specs=col, sends=sends, gather=True)(XBC, w8, b)


def conv_bwd(XBC, w8, b, dS, dxs_skip, L, sends=()):
    T = XBC.shape[0]

    def body(x_ref, w_ref, b_ref, d0_r, d1_r, dsk_r, dx_o, dw_o, db_o):
        j = pl.program_id(0)
        x = x_ref[...]
        xs = [_conv_shift(x, k - 3, L) for k in range(7)]
        pre = b_ref[...] + functools.reduce(lambda a, c: a + c, [xs[k] * w_ref[k:k + 1, :] for k in range(7)])
        _, vjp = jax.vjp(_silu, pre)
        dact = d0_r[0] + d1_r[0] + jnp.where(j < 4, dsk_r[...], 0.0)
        dpre, = vjp(dact)
        dx_o[...] = functools.reduce(
            lambda a, c: a + c, [_conv_shift(dpre, 3 - k, L) * w_ref[k:k + 1, :] for k in range(7)])
        dw_o[...] = jnp.concatenate([jnp.sum(dpre * xs[k], axis=0, keepdims=True) for k in range(7)]
                                    + [jnp.zeros((1, 128), F32)], axis=0)
        db_o[...] = jnp.sum(dpre, axis=0, keepdims=True)

    col = pl.BlockSpec((T, 128), lambda j: (0, j))
    w_s = pl.BlockSpec((8, 128), lambda j: (0, j))
    b_s = pl.BlockSpec((1, 128), lambda j: (0, j))
    ds = lambda d: pl.BlockSpec((1, T, 128), lambda j: (d, 0, j))
    return _pc(body, "conv_bwd", [_sds((T, 1024)), _sds((8, 1024)), _sds((1, 1024))], grid=(8,),
               in_specs=[col, w_s, b_s, ds(0), ds(1), pl.BlockSpec((T, 128), lambda j: (0, jnp.minimum(j, 3)))],
               out_specs=[col, w_s, b_s], sends=sends)(XBC, w8, b, dS, dS, dxs_skip)


def _ssd_chunk(xs, bs, cs, dtraw, dtb, alog, hs, tri, d):
    dt = _softplus(dtraw + dtb)
    a = dt * (-jnp.exp(alog))
    acum = _exact(tri, a)
    tot = jnp.sum(a, axis=0, keepdims=True)
    wcol = jnp.exp(tot - acum) * dt
    ea = jnp.exp(acum)
    cd = jnp.exp(tot)
    acum_t, dt_t = acum.T, dt.T
    lane = _iota((Q, 128), 1)
    srow = _iota((128, Q), 0)
    lane1 = _iota((1, 128), 1)
    prow = _iota((128, NSTATE), 0)
    mask = tri > 0.5
    cbs = [mm_nt(cs[g], bs[g]) for g in range(2)]
    ys, hn = [], []
    for j in range(4):
        g = j // 2
        x = xs[j]
        yi, st, eac, cdl = [], [], [], []
        for u in range(2):
            slot = d * 8 + 2 * j + u
            col = lambda m: jnp.sum(jnp.where(lane == slot, m, 0.0), axis=1, keepdims=True)
            rowv = lambda m: jnp.sum(jnp.where(srow == slot, m, 0.0), axis=0, keepdims=True)
            seg = col(acum) - rowv(acum_t)
            dcy = jnp.where(mask, jnp.exp(jnp.where(mask, seg, 0.0)), 0.0)
            yi.append(mm(cbs[g] * dcy * rowv(dt_t), x))
            st.append(mm_tn(x, bs[g] * col(wcol)))
            eac.append(col(ea))
            cdl.append(jnp.sum(jnp.where(lane1 == slot, cd, 0.0), axis=1, keepdims=True))
        yin = mm_nt(cs[g], hs[j])
        ys.append(jnp.where(lane < HD, yi[0] + yin * eac[0], yi[1] + yin * eac[1]))
        hn.append(hs[j] * jnp.where(prow < HD, cdl[0], cdl[1]) + jnp.where(prow < HD, st[0], st[1]))
    return ys, hn


SSD_SUB = 2


def _ssd_block_idx(d, s, nlb, nbk):
    return jnp.where(d == 0, (s + nlb) % nbk, nbk - 1 - s)


def _ssd_rows(d, i):
    return pl.ds(pl.multiple_of(jnp.where(d == 0, i, SSD_SUB - 1 - i) * Q, Q), Q)


def _ssd_split(a):
    return ([a[:, 128 * j:128 * (j + 1)] for j in range(4)], [a[:, 512 + 128 * g:640 + 128 * g] for g in range(2)],
            [a[:, 768 + 128 * g:896 + 128 * g] for g in range(2)])


def ssd_fwd(ACT, DT, dtb, alog, tri2, L, sends=()):
    T = ACT.shape[0]
    RB = SSD_SUB * Q
    nlb, nbk = L // RB, T // RB

    def body(a_ref, dt_ref, dtb_ref, al_ref, tri_ref, y_o, hs_o, hst):
        d, s = pl.program_id(0), pl.program_id(1)
        _acc_init(s == 0, [hst])
        for i in range(SSD_SUB):
            rows = _ssd_rows(d, i)
            xs, bs, cs = _ssd_split(a_ref[rows, :])
            hs_o[0, i] = hst[...]
            ys, hn = _ssd_chunk(xs, bs, cs, dt_ref[rows, :], dtb_ref[...], al_ref[...], [hst[j] for j in range(4)],
                                tri_ref[0], d)
            y_o[0, rows, :] = jnp.concatenate(ys, axis=1)
            for j in range(4):
                hst[j] = hn[j]

    bk = lambda w: pl.BlockSpec((RB, w), lambda d, s: (_ssd_block_idx(d, s, nlb, nbk), 0))
    v128 = pl.BlockSpec((1, 128), lambda d, s: (0, 0))
    return _pc(body, "ssd_fwd", [_sds((2, T, 512)), _sds((2, T // Q, 4, 128, NSTATE))], grid=(2, nbk),
               in_specs=[bk(1024), bk(128), v128, v128, pl.BlockSpec((1, Q, Q), lambda d, s: (d, 0, 0))],
               out_specs=[pl.BlockSpec((1, RB, 512), lambda d, s: (d, _ssd_block_idx(d, s, nlb, nbk), 0)),
                          pl.BlockSpec((1, SSD_SUB, 4, 128, NSTATE), lambda d, s: (d, s, 0, 0, 0))],
               scratch=[pltpu.VMEM((4, 128, NSTATE), F32)], sends=sends, gather=True)(ACT, DT, dtb, alog, tri2)


def ssd_bwd(ACT, DT, dtb, alog, tri2, HS, dY, L, sends=()):
    T = ACT.shape[0]
    RB = SSD_SUB * Q
    nlb, nbk = L // RB, T // RB

    def body(a_ref, dt_ref, dtb_ref, al_ref, tri_ref, hs_ref, dy_ref, da_o, ddt_o, ddtb_o, dal_o, dh):
        d, sr = pl.program_id(0), pl.program_id(1)
        _acc_init(sr == 0, [dh, ddtb_o, dal_o])
        tri = tri_ref[0]
        fn = lambda xs_, bs_, cs_, dtr, dtb_, al, hs_: _ssd_chunk(xs_, bs_, cs_, dtr, dtb_, al, hs_, tri, d)
        for i in reversed(range(SSD_SUB)):
            rows = _ssd_rows(d, i)
            xs, bs, cs = _ssd_split(a_ref[rows, :])
            _, vjp = jax.vjp(fn, xs, bs, cs, dt_ref[rows, :], dtb_ref[...], al_ref[...],
                             [hs_ref[0, i, j] for j in range(4)])
            dy = dy_ref[rows, :]
            dxs, dbs, dcs, ddt, ddtb, dal, dhs = vjp(([dy[:, 128 * j:128 * (j + 1)] for j in range(4)],
                                                      [dh[j] for j in range(4)]))
            da_o[0, rows, :] = jnp.concatenate(dxs + dbs + dcs, axis=1)
            ddt_o[0, rows, :] = ddt
            ddtb_o[0] += ddtb
            dal_o[0] += dal
            for j in range(4):
                dh[j] = dhs[j]

    bidx = lambda d, sr: _ssd_block_idx(d, nbk - 1 - sr, nlb, nbk)
    bk = lambda w: pl.BlockSpec((RB, w), lambda d, sr: (bidx(d, sr), 0))
    v128 = pl.BlockSpec((1, 128), lambda d, sr: (0, 0))
    o128 = pl.BlockSpec((1, 1, 128), lambda d, sr: (d, 0, 0))
    return _pc(body, "ssd_bwd", [_sds((2, T, 1024)), _sds((2, T, 128)), _sds((2, 1, 128)), _sds((2, 1, 128))],
               grid=(2, nbk),
               in_specs=[bk(1024), bk(128), v128, v128, pl.BlockSpec((1, Q, Q), lambda d, sr: (d, 0, 0)),
                         pl.BlockSpec((1, SSD_SUB, 4, 128, NSTATE), lambda d, sr: (d, nbk - 1 - sr, 0, 0, 0)), bk(512)],
               out_specs=[pl.BlockSpec((1, RB, 1024), lambda d, sr: (d, bidx(d, sr), 0)),
                          pl.BlockSpec((1, RB, 128), lambda d, sr: (d, bidx(d, sr), 0)), o128, o128],
               scratch=[pltpu.VMEM((4, 128, NSTATE), F32)], sends=sends)(ACT, DT, dtb, alog, tri2, HS, dY)


_PAIR_HEADS = np.array([[0, 2], [1, 3]])


def _tables(L):
    t = jnp.arange(L)
    inv = 10000.0 ** (-jnp.arange(16, dtype=F32) / 16)

    def half(pos):
        ang = pos.astype(F32)[:, None] * inv[None, :]
        return jnp.concatenate([ang, ang], axis=1)

    ang = jnp.tile(jnp.concatenate([half(t // GW), half(t % GW)], axis=1), (1, 4))
    cos = jnp.concatenate([jnp.cos(ang), jnp.ones((LC, 256), F32)], axis=0)
    sin = jnp.concatenate([jnp.sin(ang), jnp.zeros((LC, 256), F32)], axis=0)
    rm = np.zeros((256, 256), np.float32)
    for j in range(256):
        if j % 32 < 16:
            rm[j + 16, j] = -1.0
        else:
            rm[j - 16, j] = 1.0
    tri = np.tril(np.ones((Q, Q), np.float32))
    return cos, sin, jnp.asarray(rm), jnp.asarray(np.stack([tri, tri.T]))


def _na_index(R):
    rc = np.array([0, 1, 2, 3, 4, R - 3, R - 2, R - 1])
    dy = np.clip(rc - 4, 0, R - 8)[:, None] + np.arange(8)[None, :] - rc[:, None] + 7
    qc, cc = np.arange(GW)[:, None], np.arange(GW)[None, :]
    dx = np.clip(cc - qc, -15, 15) + 15
    cstart = np.clip(qc - 8, 0, GW - 16)
    cmask = (cc >= cstart) & (cc < cstart + 16)
    idx = dy[:, None, :, None] * 31 + dx[None, :, None, :]
    return idx.reshape(8, GW, 8 * GW), np.broadcast_to(cmask[None, :, None, :], idx.shape).reshape(8, GW, 8 * GW), \
        dy, dx, cmask


def _na_bias(rpb, R):
    _, cm, dy, _, _ = _na_index(R)
    rows = rpb[:, dy.reshape(-1), :].reshape(4, 8, 4, 2, 31)
    p2 = jnp.pad(jnp.pad(rows, ((0, 0),) * 4 + ((0, 33),)).reshape(4, 8, 4, 128), ((0, 0), (0, 0), (0, 4), (0, 0)))
    negmask = jnp.asarray(np.where(cm[0], 0.0, NEG).astype(np.float32))

    def body(p_ref, m_ref, o_ref):
        for c in range(8):
            tiles = [pltpu.roll(jnp.broadcast_to(p_ref[0, c, jp:jp + 1, :], (GW, 128)), 113, 1, stride=1,
                                stride_axis=0) for jp in range(4)]
            o_ref[0, c] = jnp.where(m_ref[...] < 0.0, NEG, jnp.concatenate(tiles, axis=1))

    return _pc(body, "na_bias", _sds((4, 8, GW, 8 * GW)), grid=(4,),
               in_specs=[pl.BlockSpec((1, 8, 8, 128), lambda h: (h, 0, 0, 0)),
                         pl.BlockSpec((GW, 8 * GW), lambda h: (0, 0))],
               out_specs=pl.BlockSpec((1, 8, GW, 8 * GW), lambda h: (h, 0, 0, 0)))(p2, negmask)


def _na_bias_grad(dbias, R):
    _, _, dy, dx, cmask = _na_index(R)
    e1 = np.zeros((GW * GW, 128), np.float32)
    e1[np.arange(GW * GW), dx.reshape(-1)] = cmask.reshape(-1)
    a1 = dbias.reshape(4, 8, GW, 8, GW).transpose(0, 1, 3, 2, 4).reshape(256, GW * GW)
    v = exact_mm_call(a1, jnp.asarray(e1))[:, :31].reshape(4, 64, 31)
    e2 = np.zeros((64, 128), np.float32)
    e2[np.arange(64), dy.reshape(-1)] = 1.0
    a2 = jnp.pad(v.transpose(0, 2, 1).reshape(124, 64), ((0, 4), (0, 0)))
    return exact_mm_call(a2, jnp.asarray(e2))[:124, :15].reshape(4, 31, 15).transpose(0, 2, 1)


def _lanes(v, n=128):
    v = v.reshape(1, -1)
    return jnp.pad(v, ((0, 0), (0, n - v.shape[1])))


def _cls2(a, b):
    return jnp.stack([a, b]).reshape(2, 1, D)


def _win_p(g):
    return jnp.concatenate([g.reshape(IN_COLS, D), jnp.zeros((NP_IN - IN_COLS, D), g.dtype)], axis=0)


def _layer_consts(p):
    sinkp = jnp.broadcast_to(p["wa_sink"][_PAIR_HEADS][:, :, None, None], (2, 2, 1, 128))
    return dict(
        sinkp=sinkp, nosink=jnp.full((2, 2, 1, 128), NEG, F32),
        w8=jnp.concatenate([p["ssm_conv_w"], jnp.zeros((1, 1024), F32)], axis=0),
        cb=p["ssm_conv_b"].reshape(1, 1024), dtb=_lanes(p["ssm_dt_bias"]), alog=_lanes(p["ssm_a_log"]),
        dsk=jnp.repeat(p["ssm_d"], HD).reshape(1, 512), gs=p["ssm_norm_g"].reshape(1, 512),
        gmix=p["g_mix"].reshape(1, D), gffn=p["g_ffn"].reshape(1, D))


def _mods(mod2):
    return [_cls2(mod2[0, D * k:D * (k + 1)], mod2[1, D * k:D * (k + 1)]) for k in range(6)]


def _layer_fwd(X, mod2, c, rpb, tabs, L, ctx_out, shards, nxt):
    cos, sin, rm, tri2 = tabs
    sh1, sc1, gt1, sh2, sc2, gt2 = _mods(mod2)
    biasd = _na_bias(rpb, L // GW)
    fi, fo, wo = shards
    fcut, fcut2, ocut = 448, 640, 224
    (qa, qb, z, ka, va, kb, vb, xbc, dt, h1), (gfo_a,) = in_fwd(X, c["gmix"], sh1, sc1, c["win"], cos, sin, rm, L,
                                                                sends=(fo[:ocut],))
    (oa,), (gfi_b,) = wa_fwd(qa, ka, va, c["sinkp"], L, sends=(fi[fcut:fcut2],))
    (ob,), (gwo,) = na_fwd(qb, kb, vb, biasd, L, sends=(wo,))
    c = dict(c, wout=gwo.reshape(D, D))
    if ctx_out:
        oa_c = ctx_fwd(qa, ka, va, c["sinkp"], True, L)
        ob_c = ctx_fwd(qb, kb, vb, c["nosink"], False, L)
    else:
        oa_c = ob_c = jnp.zeros((LC, 256), F32)
    oa = jnp.concatenate([oa, oa_c], axis=0)
    ob = jnp.concatenate([ob, ob_c], axis=0)
    (act,), (gfi_c,) = conv_fwd(xbc, c["w8"], c["cb"], L, sends=(fi[fcut2:],))
    (y2, hs), (gfi_a,) = ssd_fwd(act, dt, c["dtb"], c["alog"], tri2, L, sends=(fi[:fcut],))
    (X1, cat), (gfo_b,) = out_fwd(oa, ob, y2, act, z, c["dsk"], c["gs"], c["wout"], X, gt1, L, sends=(fo[ocut:],))
    c = dict(c, wfi=jnp.concatenate([gfi_a, gfi_b, gfi_c], axis=1).reshape(2 * DFF, D),
             wfo=jnp.concatenate([gfo_a, gfo_b], axis=1).reshape(DFF, D))
    res = ffn_fwd(X1, c["gffn"], sh2, sc2, gt2, c["wfi"], c["wfo"], L, sends=nxt)
    (X2, ff), got = res if nxt else (res, ())
    saved = dict(X=X, X1=X1, ff=ff, qa=qa, qb=qb, z=z, ka=ka, va=va, kb=kb, vb=vb, xbc=xbc, dt=dt, h1=h1, oa=oa, ob=ob,
                 act=act, y2=y2, hs=hs, cat=cat, biasd=biasd)
    return X2, saved, c, got


def _row_blocks(gw):
    return gw.reshape(NDEV, gw.shape[0] // NDEV, gw.shape[1])


def _layer_bwd(dX2, s, mod2, c, tabs, L, ctx_out, carry):
    cos, sin, rm, tri2 = tabs
    sh1, sc1, gt1, sh2, sc2, gt2 = _mods(mod2)
    R = L // GW
    res = ffn_bwd(s["X1"], c["gffn"], sh2, sc2, gt2, c["wfi"], c["wfo"], s["ff"], dX2, L, sends=carry)
    (dX1, h2, dU, actf, dOut, dgffn, dsh2, dsc2, dgt2), got = res if carry else (res, ())
    g = {}
    gfi = _row_blocks(tn_mm(dU, h2, 512, 1024, MXU))
    gfo = _row_blocks(tn_mm(actf, dOut, 256, 1024, MXU))
    doa, dob, dy, dxs_skip, dz, dmix, ddsk, dgs, dgt1 = out_bwd(s["oa"], s["ob"], s["y2"], s["act"], s["z"], c["dsk"],
                                                                c["gs"], c["wout"], gt1, dX1, L)
    gout = _row_blocks(tn_mm(s["cat"], dmix, 512, 1024, MXU))
    (dS, ddt2, ddtb, dal), (g["w_ffn_in"],) = ssd_bwd(
        s["act"], s["dt"], c["dtb"], c["alog"], tri2, s["hs"], dy, L, sends=(gfi,))
    (dxbc, dw8, dcb), (g["w_ffn_out"],) = conv_bwd(s["xbc"], c["w8"], c["cb"], dS, dxs_skip, L, sends=(gfo,))
    (dqa, dka, dva, dska), (g["w_out"],) = wa_bwd(s["qa"], s["ka"], s["va"], c["sinkp"], doa, L, sends=(gout,))
    dka, dva = dka[Q:], dva[Q:]
    dqb, dkb, dvb, dkxb, dvxb, dbias = na_bwd(s["qb"], s["kb"], s["vb"], s["biasd"], dob, L)
    if ctx_out:
        dqa_c, dk1, dv1, dsk1 = ctx_bwd(s["qa"], s["ka"], s["va"], c["sinkp"], doa, True, L)
        dqb_c, dk2, dv2, _ = ctx_bwd(s["qb"], s["kb"], s["vb"], c["nosink"], dob, False, L)
        dka = jnp.concatenate([dka[:L], dka[L:] + dk1], axis=0)
        dva = jnp.concatenate([dva[:L], dva[L:] + dv1], axis=0)
        dska = dska + dsk1
        dkxb, dvxb = dkxb + dk2, dvxb + dv2
    else:
        dqa_c = dqb_c = jnp.zeros((LC, 256), F32)
    cat0 = lambda a, b: jnp.concatenate([a, b], axis=0)
    dX, dycat, dgmix, dsh1, dsc1 = in_bwd(
        s["X"], c["gmix"], sh1, sc1, c["win"], cos, sin, rm, dX1, cat0(dqa, dqa_c), cat0(dqb, dqb_c), dz,
        dka, dva, cat0(dkb, dkxb), cat0(dvb, dvxb), dxbc, ddt2, L,
        latent_only=ctx_out)
    if ctx_out:
        gin = [_row_blocks(tn_mm(dycat, s["h1"], 512, D // 2, MXU, ncol=D // 2, col0=k)[:IN_COLS]) for k in (0, 1)]
    else:
        gin = _row_blocks(tn_mm(dycat, s["h1"], 512, 1024, MXU)[:IN_COLS])
    g["g_mix"] = dgmix.reshape(D)
    g["g_ffn"] = dgffn.reshape(D)
    sk = jnp.sum(dska, axis=(2, 3))
    g["wa_sink"] = jnp.zeros((4,), F32).at[_PAIR_HEADS.reshape(-1)].set(sk.reshape(-1))
    g["na_rpb"] = _na_bias_grad(dbias, R)
    g["ssm_conv_w"] = dw8[:7]
    g["ssm_conv_b"] = dcb.reshape(1024)
    g["ssm_dt_bias"] = (ddtb[0] + ddtb[1])[0, :16].reshape(2, 8)
    g["ssm_a_log"] = (dal[0] + dal[1])[0, :16].reshape(2, 8)
    g["ssm_d"] = jnp.sum(ddsk.reshape(8, HD), axis=1)
    g["ssm_norm_g"] = dgs.reshape(512)
    dmod2 = jnp.concatenate([dsh1, dsc1, dgt1, dsh2, dsc2, dgt2], axis=2).reshape(2, 6 * D)
    return dX, g, dmod2, gin, got


def local_step(x, ctx, tgt, mods, layers, shards, g_final, L):
    tabs = _tables(L)
    X = (x, ctx)
    consts = [_layer_consts(p) for p in layers]
    saved = []
    got = (shards["w_in_first"],)
    for i in range(2):
        consts[i] = dict(consts[i], win=_win_p(got[0]))
        nxt = (shards["w_in"][1],) if i == 0 else ()
        X, s, consts[i], got = _layer_fwd(X, mods[i], consts[i], layers[i]["na_rpb"], tabs, L, i == 0,
                                          (shards["w_ffn_in"][i], shards["w_ffn_out"][i], shards["w_out"][i]), nxt)
        saved.append(s)
    loss8, dX, dgfin = loss_head(X, g_final.reshape(1, D), tgt, L)
    grads, dmods = [None, None], [None, None]
    dX, grads[1], dmods[1], gin1, _ = _layer_bwd(dX, saved[1], mods[1], consts[1], tabs, L, False, ())
    dX, grads[0], dmods[0], gin0, (grads[1]["w_in"],) = _layer_bwd(dX, saved[0], mods[0], consts[0], tabs, L, True,
                                                                   (gin1,))
    return loss8[0, 0], dX, grads, jnp.stack(dmods), dgfin.reshape(D), gin0


def _place():
    x, y, c = lax.axis_index("x"), lax.axis_index("y"), lax.axis_index("c")
    return x, y, c


def _slot(b):
    return 4 * b[0] + 2 * b[1] + b[2]


def _any():
    return pl.BlockSpec(memory_space=pl.ANY)


def all_gather(xs, name):
    n = len(xs)

    def body(*refs):
        x_refs, o_refs = refs[:n], refs[n:2 * n]
        send_sems, recv_sems, local_sems = refs[2 * n:]
        x, y, c = _place()
        me, sib = (x, y, c), (x, y, 1 - c)
        chips = [(1 - x, y), (x, 1 - y), (1 - x, 1 - y)]

        def copy(t, k, blk, to, src=None):
            dst = o_refs[t].at[_slot(blk)]
            return pltpu.make_async_remote_copy(
                src_ref=dst if src is None else src, dst_ref=dst, send_sem=send_sems.at[7 * t + k],
                recv_sem=recv_sems.at[7 * t + k], device_id=to, device_id_type=MESH_T)

        mine = [pltpu.make_async_copy(x_refs[t], o_refs[t].at[_slot(me)], local_sems.at[t]) for t in range(n)]
        for cp in mine:
            cp.start()
        first = []
        for t in range(n):
            first.append(copy(t, 0, me, sib, src=x_refs[t]))
            first += [copy(t, 1 + j, me, (*chip, c), src=x_refs[t]) for j, chip in enumerate(chips)]
        for cp in first:
            cp.start()
        passed = []
        for j, chip in enumerate(chips):
            for t in range(n):
                copy(t, 1 + j, (*chip, c), me).wait_recv()
                cp = copy(t, 4 + j, (*chip, c), sib)
                cp.start()
                passed.append(cp)
        for t in range(n):
            copy(t, 0, sib, me).wait_recv()
            for j, chip in enumerate(chips):
                copy(t, 4 + j, (*chip, 1 - c), me).wait_recv()
        for cp in first + passed:
            cp.wait_send()
        for cp in mine:
            cp.wait()

    return pl.pallas_call(
        body, name=name, out_shape=[_sds((NDEV,) + a.shape, a.dtype) for a in xs],
        in_specs=[_any()] * n, out_specs=[_any()] * n,
        scratch_shapes=[pltpu.SemaphoreType.DMA((7 * n,)), pltpu.SemaphoreType.DMA((7 * n,)),
                        pltpu.SemaphoreType.DMA((n,))],
        interpret=_INTERPRET)(*xs)


def _a2a_sems(n):
    return [pltpu.SemaphoreType.DMA((7 * n,)), pltpu.SemaphoreType.DMA((7 * n,)), pltpu.SemaphoreType.DMA((n,))]


def _a2a_copies(x_refs, o_refs, send_sems, recv_sems, local_sems):
    n = len(x_refs)
    x, y, c = _place()
    me = (x, y, c)
    flip = lambda v, b: (1 - v) if b else v
    peers = [(flip(x, k >> 2 & 1), flip(y, k >> 1 & 1), flip(c, k & 1)) for k in range(1, NDEV)]
    mine = [pltpu.make_async_copy(x_refs[t].at[_slot(me)], o_refs[t].at[_slot(me)], local_sems.at[t])
            for t in range(n)]

    def copy(t, k, src_slot, dst_slot, to):
        return pltpu.make_async_remote_copy(
            src_ref=x_refs[t].at[src_slot], dst_ref=o_refs[t].at[dst_slot], send_sem=send_sems.at[7 * t + k],
            recv_sem=recv_sems.at[7 * t + k], device_id=to, device_id_type=MESH_T)

    sends = [copy(t, k, _slot(p), _slot(me), p) for t in range(n) for k, p in enumerate(peers)]
    recvs = [copy(t, k, _slot(p), _slot(p), me) for t in range(n) for k, p in enumerate(peers)]
    return mine, sends, recvs


def _ag_copies(x_refs, o_refs, send_sems, recv_sems, local_sems):
    n = len(x_refs)
    x, y, c = _place()
    me = (x, y, c)
    flip = lambda v, b: (1 - v) if b else v
    peers = [(flip(x, k >> 2 & 1), flip(y, k >> 1 & 1), flip(c, k & 1)) for k in range(1, NDEV)]
    mine = [pltpu.make_async_copy(x_refs[t], o_refs[t].at[_slot(me)], local_sems.at[t]) for t in range(n)]

    def copy(t, k, dst_slot, to):
        return pltpu.make_async_remote_copy(
            src_ref=x_refs[t], dst_ref=o_refs[t].at[dst_slot], send_sem=send_sems.at[7 * t + k],
            recv_sem=recv_sems.at[7 * t + k], device_id=to, device_id_type=MESH_T)

    sends = [copy(t, k, _slot(me), p) for t in range(n) for k, p in enumerate(peers)]
    recvs = [copy(t, k, _slot(p), me) for t in range(n) for k, p in enumerate(peers)]
    return mine, sends, recvs


def _ag_start(x_refs, o_refs, send_sems, recv_sems, local_sems):
    mine, sends, _ = _ag_copies(x_refs, o_refs, send_sems, recv_sems, local_sems)
    for cp in mine + sends:
        cp.start()


def _ag_wait(x_refs, o_refs, send_sems, recv_sems, local_sems):
    mine, sends, recvs = _ag_copies(x_refs, o_refs, send_sems, recv_sems, local_sems)
    for cp in recvs:
        cp.wait_recv()
    for cp in sends:
        cp.wait_send()
    for cp in mine:
        cp.wait()


def _a2a_start(x_refs, o_refs, send_sems, recv_sems, local_sems):
    mine, sends, _ = _a2a_copies(x_refs, o_refs, send_sems, recv_sems, local_sems)
    for cp in mine + sends:
        cp.start()


def _a2a_wait(x_refs, o_refs, send_sems, recv_sems, local_sems):
    mine, sends, recvs = _a2a_copies(x_refs, o_refs, send_sems, recv_sems, local_sems)
    for cp in recvs:
        cp.wait_recv()
    for cp in sends:
        cp.wait_send()
    for cp in mine:
        cp.wait()


def adam_reduce(P, w, m, v, name, sends=()):
    n, R, C = P.shape
    br = R // 4 if R % 64 == 0 else R

    def body(p_ref, w_ref, m_ref, v_ref, g_o, d_o, m_o, v_o):
        g = p_ref[0].astype(F32)
        for k in range(1, n):
            g = g + p_ref[k].astype(F32)
        m1 = ADAM_B1 * m_ref[...] + (1.0 - ADAM_B1) * g
        v1 = ADAM_B2 * v_ref[...] + (1.0 - ADAM_B2) * jnp.square(g)
        m_hat = m1 / (1.0 - ADAM_B1 ** ADAM_STEP)
        v_hat = v1 / (1.0 - ADAM_B2 ** ADAM_STEP)
        g_o[...] = g
        d_o[...] = -ADAM_LR * (m_hat / (jnp.sqrt(v_hat) + ADAM_EPS) + ADAM_WD * w_ref[...])
        m_o[...] = m1
        v_o[...] = v1

    blk = pl.BlockSpec((br, C), lambda i: (i, 0))
    return _pc(body, name, [_sds((R, C))] * 4, grid=(R // br,),
               in_specs=[pl.BlockSpec((n, br, C), lambda i: (0, i, 0)), blk, blk, blk], out_specs=[blk] * 4,
               sends=sends)(P, w, m, v)


def adam_layers(P0, P1, w, m, v, name, sends=()):
    n, R, C = P0.shape
    br = R // 4 if R % 64 == 0 else R
    nb = R // br

    def body(p0_ref, p1_ref, w_ref, m_ref, v_ref, g_o, d_o, m_o, v_o):
        def total(p_ref):
            g = p_ref[0].astype(F32)
            for k in range(1, n):
                g = g + p_ref[k].astype(F32)
            return g

        g = jnp.where(pl.program_id(0) == 0, total(p0_ref), total(p1_ref))
        m1 = ADAM_B1 * m_ref[0] + (1.0 - ADAM_B1) * g
        v1 = ADAM_B2 * v_ref[0] + (1.0 - ADAM_B2) * jnp.square(g)
        m_hat = m1 / (1.0 - ADAM_B1 ** ADAM_STEP)
        v_hat = v1 / (1.0 - ADAM_B2 ** ADAM_STEP)
        g_o[0] = g
        d_o[0] = -ADAM_LR * (m_hat / (jnp.sqrt(v_hat) + ADAM_EPS) + ADAM_WD * w_ref[0])
        m_o[0] = m1
        v_o[0] = v1

    blk = pl.BlockSpec((1, br, C), lambda l, i: (l, i, 0))
    p0 = pl.BlockSpec((n, br, C), lambda l, i: (0, jnp.where(l == 0, i, nb - 1), 0))
    p1 = pl.BlockSpec((n, br, C), lambda l, i: (0, jnp.where(l == 1, i, 0), 0))
    return _pc(body, name, [_sds((2, R, C))] * 4, grid=(2, nb), in_specs=[p0, p1, blk, blk, blk],
               out_specs=[blk] * 4, sends=sends)(P0, P1, w, m, v)


def mod_fwd(scin, wmod, bcol):
    def body(s_ref, w_ref, b_ref, o_ref):
        o_ref[0] = mm(_silu(s_ref[...]), w_ref[0]) + b_ref[0]

    return _pc(body, "mod_fwd", _sds((2, 16, 768)), grid=(2,),
               in_specs=[pl.BlockSpec((16, D), lambda l: (0, 0)), pl.BlockSpec((1, D, 768), lambda l: (l, 0, 0)),
                         pl.BlockSpec((1, 1, 768), lambda l: (l, 0, 0))],
               out_specs=pl.BlockSpec((1, 16, 768), lambda l: (l, 0, 0)))(scin, wmod, bcol)


def mod_bwd(scin, wmod, G):
    def body(s_ref, w_ref, g_ref, dw_o, ds_o):
        _, vjp = jax.vjp(lambda s, w: mm(_silu(s), w), s_ref[...], w_ref[0])
        ds, dw = vjp(g_ref[0])
        dw_o[0] = dw
        _acc_init(pl.program_id(0) == 0, [ds_o])
        ds_o[...] += ds

    full = pl.BlockSpec((16, D), lambda l: (0, 0))
    wsp = pl.BlockSpec((1, D, 768), lambda l: (l, 0, 0))
    return _pc(body, "mod_bwd", [_sds((2, D, 768)), _sds((16, D))], grid=(2,),
               in_specs=[full, wsp, pl.BlockSpec((1, 16, 768), lambda l: (l, 0, 0))], out_specs=[wsp, full])(
        scin, wmod, G)


_SMALL = ["b_mod", "g_mix", "wa_sink", "na_rpb", "ssm_conv_w", "ssm_conv_b", "ssm_dt_bias", "ssm_a_log", "ssm_d",
          "ssm_norm_g", "g_ffn", "g_final", "dmod_s", "dmod_c", "loss"]


def _pack(parts):
    rows = []
    for a in parts:
        f = a.reshape(-1).astype(F32)
        rows.append(jnp.pad(f, (0, (-f.shape[0]) % 1024)).reshape(-1, 128))
    return jnp.concatenate(rows, axis=0)


def _unpack(packed, shapes):
    out, r = [], 0
    for s in shapes:
        nel = int(np.prod(s))
        nr = -(-nel // 1024) * 8
        out.append(packed[r:r + nr].reshape(-1)[:nel].reshape(s))
        r += nr
    return out


def kernel(x, c, ctx, c_ctx, w_mod, b_mod, g_mix, w_in, wa_sink, na_rpb, ssm_conv_w, ssm_conv_b, ssm_dt_bias, ssm_a_log, ssm_d, ssm_norm_g, w_out, g_ffn, w_ffn_in, w_ffn_out, g_final, loss_target, m_c_ctx, m_w_mod, m_b_mod, m_g_mix, m_w_in, m_wa_sink, m_na_rpb, m_ssm_conv_w, m_ssm_conv_b, m_ssm_dt_bias, m_ssm_a_log, m_ssm_d, m_ssm_norm_g, m_w_out, m_g_ffn, m_w_ffn_in, m_w_ffn_out, m_g_final, v_c_ctx, v_w_mod, v_b_mod, v_g_mix, v_w_in, v_wa_sink, v_na_rpb, v_ssm_conv_w, v_ssm_conv_b, v_ssm_dt_bias, v_ssm_a_log, v_ssm_d, v_ssm_norm_g, v_w_out, v_g_ffn, v_w_ffn_in, v_w_ffn_out, v_g_final):
    L = x.shape[1]
    px, py, pc = _place()
    me = 4 * px + 2 * py + pc
    W = dict(c_ctx=c_ctx, w_mod=w_mod, b_mod=b_mod, g_mix=g_mix, w_in=w_in, wa_sink=wa_sink, na_rpb=na_rpb,
             ssm_conv_w=ssm_conv_w, ssm_conv_b=ssm_conv_b, ssm_dt_bias=ssm_dt_bias, ssm_a_log=ssm_a_log, ssm_d=ssm_d,
             ssm_norm_g=ssm_norm_g, w_out=w_out, g_ffn=g_ffn, w_ffn_in=w_ffn_in, w_ffn_out=w_ffn_out, g_final=g_final)
    M = dict(c_ctx=m_c_ctx, w_mod=m_w_mod, b_mod=m_b_mod, g_mix=m_g_mix, w_in=m_w_in, wa_sink=m_wa_sink,
             na_rpb=m_na_rpb, ssm_conv_w=m_ssm_conv_w, ssm_conv_b=m_ssm_conv_b, ssm_dt_bias=m_ssm_dt_bias,
             ssm_a_log=m_ssm_a_log, ssm_d=m_ssm_d, ssm_norm_g=m_ssm_norm_g, w_out=m_w_out, g_ffn=m_g_ffn,
             w_ffn_in=m_w_ffn_in, w_ffn_out=m_w_ffn_out, g_final=m_g_final)
    V = dict(c_ctx=v_c_ctx, w_mod=v_w_mod, b_mod=v_b_mod, g_mix=v_g_mix, w_in=v_w_in, wa_sink=v_wa_sink,
             na_rpb=v_na_rpb, ssm_conv_w=v_ssm_conv_w, ssm_conv_b=v_ssm_conv_b, ssm_dt_bias=v_ssm_dt_bias,
             ssm_a_log=v_ssm_a_log, ssm_d=v_ssm_d, ssm_norm_g=v_ssm_norm_g, w_out=v_w_out, g_ffn=v_g_ffn,
             w_ffn_in=v_w_ffn_in, w_ffn_out=v_w_ffn_out, g_final=v_g_final)

    tr = lambda a: a.transpose(0, 2, 1)
    shards = dict(w_in=tr(w_in).astype(MXU), w_out=w_out.astype(MXU), w_ffn_in=tr(w_ffn_in).astype(MXU),
                  w_ffn_out=w_ffn_out.astype(MXU))
    c_all, conv_all, shards["w_in_first"] = all_gather([c, ssm_conv_w, shards["w_in"][0]], "gather_first")
    conv_f = conv_all.transpose(1, 2, 0, 3).reshape(2, 7, 1024)

    scin = jnp.concatenate([c_all.reshape(NDEV, D), c_ctx.reshape(1, D), jnp.zeros((7, D), F32)], axis=0)
    bcol = lax.dynamic_slice_in_dim(b_mod, me * 768, 768, axis=1).reshape(2, 1, 768)
    mod_all, = all_gather([mod_fwd(scin, w_mod, bcol)], "gather_mod")
    mod_rows = mod_all.transpose(1, 2, 0, 3).reshape(2, 16, 6 * D)
    mods = jnp.stack([lax.dynamic_index_in_dim(mod_rows, me, axis=1, keepdims=False), mod_rows[:, 8]], axis=1)

    layers = [dict(g_mix=g_mix[i], wa_sink=wa_sink[i], na_rpb=na_rpb[i], ssm_conv_w=conv_f[i],
                   ssm_conv_b=ssm_conv_b[i], ssm_dt_bias=ssm_dt_bias[i], ssm_a_log=ssm_a_log[i], ssm_d=ssm_d[i],
                   ssm_norm_g=ssm_norm_g[i], g_ffn=g_ffn[i]) for i in range(2)]
    loss, dx, grads, dmods, dgfin, gin0 = local_step(x[0], ctx[0], loss_target[0], mods, layers, shards, g_final, L)

    stk = lambda n: jnp.stack([grads[0][n], grads[1][n]])
    small = dict(b_mod=dmods[:, 0] + dmods[:, 1], g_final=dgfin, dmod_s=dmods[:, 0], dmod_c=dmods[:, 1],
                 loss=loss.reshape(1))
    for nme in _SMALL:
        if nme not in small:
            small[nme] = stk(nme)
    shapes = [small[nme].shape for nme in _SMALL]
    zero_like = lambda nme: jnp.zeros(small[nme].shape, F32)
    own = lambda S, nme: S[nme] if (nme in S and S[nme].shape == small[nme].shape) else zero_like(nme)
    gath, = all_gather([_pack([small[nme] for nme in _SMALL])], "gather_grads")
    sm = adam_reduce(gath, _pack([own(W, nme) for nme in _SMALL]), _pack([own(M, nme) for nme in _SMALL]),
                     _pack([own(V, nme) for nme in _SMALL]), "adam_small")
    res = {nme: vals for nme, vals in zip(_SMALL, zip(*[_unpack(a, shapes) for a in sm]))}
    loss = res["loss"][0][0]

    cols = lambda a: lax.dynamic_slice_in_dim(a, me * 768, 768, axis=-1)
    rows_of = lambda s: -(-int(np.prod(s)) // 1024) * 8
    r0 = sum(rows_of(s) for s in shapes[:_SMALL.index("dmod_s")])
    dmod_s_all = gath[:, r0:r0 + rows_of(small["dmod_s"].shape)].reshape(NDEV, 2, 6 * D).transpose(1, 0, 2)
    G = jnp.concatenate([cols(dmod_s_all), cols(res["dmod_c"][0])[:, None, :], jnp.zeros((2, 7, 768), F32)], axis=1)
    dwmod, dscin = mod_bwd(scin, w_mod, G)
    cc_g, = all_gather([dscin[8].reshape(8, 128)], "gather_cctx")
    out = {}
    out["c_ctx"] = [a.reshape(D) for a in adam_reduce(cc_g, c_ctx.reshape(8, 128), m_c_ctx.reshape(8, 128),
                                                      v_c_ctx.reshape(8, 128), "adam_cctx")]
    res_wmod, (got1,) = adam_reduce(dwmod.reshape(1, 2 * D, 768), w_mod.reshape(2 * D, 768),
                                    m_w_mod.reshape(2 * D, 768), v_w_mod.reshape(2 * D, 768), "adam_wmod",
                                    sends=(gin0[1],))
    out["w_mod"] = [a.reshape(2, D, 768) for a in res_wmod]
    gconv = lax.dynamic_slice_in_dim(res["ssm_conv_w"][0], me * 128, 128, axis=2)
    out["ssm_conv_w"] = [a.reshape(2, 7, 128) for a in adam_reduce(
        gconv.reshape(1, 14, 128), ssm_conv_w.reshape(14, 128), m_ssm_conv_w.reshape(14, 128),
        v_ssm_conv_w.reshape(14, 128), "adam_conv")]
    for nme in _SMALL:
        if nme not in ("ssm_conv_w", "dmod_s", "dmod_c", "loss"):
            out[nme] = list(res[nme])

    adam_big = lambda nme, t, **kw: adam_layers(grads[0][nme], grads[1][nme], t(W[nme]), t(M[nme]), t(V[nme]),
                                                "adam_" + nme, **kw)
    same = lambda a: a
    res_fi, (got0,) = adam_big("w_ffn_in", tr, sends=(gin0[0],))
    grads[0]["w_in"] = jnp.concatenate([got0, got1], axis=2)
    out["w_ffn_in"] = [tr(a) for a in res_fi]
    out["w_ffn_out"] = list(adam_big("w_ffn_out", same))
    out["w_out"] = list(adam_big("w_out", same))
    out["w_in"] = [tr(a) for a in adam_big("w_in", tr)]
    order = ["c_ctx", "w_mod", "b_mod", "g_mix", "w_in", "wa_sink", "na_rpb", "ssm_conv_w", "ssm_conv_b",
             "ssm_dt_bias", "ssm_a_log", "ssm_d", "ssm_norm_g", "w_out", "g_ffn", "w_ffn_in", "w_ffn_out", "g_final"]
    return (loss, dx.reshape(1, L, D), *[out[nme][0] for nme in order], *[out[nme][1] for nme in order],
            *[out[nme][2] for nme in order], *[out[nme][3] for nme in order])
```

```python
import functools

import numpy as np
import jax
import jax.numpy as jnp
from jax import lax
from jax.experimental import pallas as pl
from jax.experimental.pallas import tpu as pltpu

F32 = jnp.float32
MXU = jnp.bfloat16
_INTERPRET = False
VMEM_LIMIT = 60 * 1024 * 1024

D = 1024
LC = 256
GW = 64
HD = 64
EPS = 1e-6
NEG = -1e30
NDEV = 8
Q = 128
NSTATE = 128
DFF = 2816
IN_COLS = 2832
NP_IN = 3072
C_QA, C_QB, C_Z, C_KA, C_VA, C_KB, C_VB, C_XBC, C_DT = 0, 256, 512, 1024, 1152, 1280, 1536, 1792, 2816
ADAM_LR, ADAM_B1, ADAM_B2, ADAM_EPS, ADAM_WD, ADAM_STEP = 0.001, 0.9, 0.999, 1e-08, 0.01, 10
MESH_T = pl.DeviceIdType.MESH


def _dg(a, b, ca, cb):
    return lax.dot_general(a.astype(MXU), b.astype(MXU), (((ca,), (cb,)), ((), ())), preferred_element_type=F32)


@jax.custom_vjp
def mm(a, b):
    return _dg(a, b, 1, 0)


def _mm_f(a, b):
    return _dg(a, b, 1, 0), (a, b)


def _mm_b(res, g):
    a, b = res
    return _dg(g, b, 1, 1).astype(a.dtype), _dg(a, g, 0, 0).astype(b.dtype)


mm.defvjp(_mm_f, _mm_b)


@jax.custom_vjp
def mm_nt(a, b):
    return _dg(a, b, 1, 1)


def _mmnt_f(a, b):
    return _dg(a, b, 1, 1), (a, b)


def _mmnt_b(res, g):
    a, b = res
    return _dg(g, b, 1, 0).astype(a.dtype), _dg(g, a, 0, 0).astype(b.dtype)


mm_nt.defvjp(_mmnt_f, _mmnt_b)


@jax.custom_vjp
def mm_tn(a, b):
    return _dg(a, b, 0, 0)


def _mmtn_f(a, b):
    return _dg(a, b, 0, 0), (a, b)


def _mmtn_b(res, g):
    a, b = res
    return _dg(b, g, 1, 1).astype(a.dtype), _dg(a, g, 1, 0).astype(b.dtype)


mm_tn.defvjp(_mmtn_f, _mmtn_b)


@jax.custom_vjp
def mmw(a, w):
    return _dg(a, w, 1, 0)


mmw.defvjp(lambda a, w: (_dg(a, w, 1, 0), w), lambda w, g: (_dg(g, w, 1, 1), None))


@jax.custom_vjp
def mmw_nt(a, w):
    return _dg(a, w, 1, 1)


mmw_nt.defvjp(lambda a, w: (_dg(a, w, 1, 1), w), lambda w, g: (_dg(g, w, 1, 0), None))


def _exact(a, b):
    return lax.dot_general(a, b, (((1,), (0,)), ((), ())), precision=lax.Precision.HIGHEST,
                           preferred_element_type=F32)


def _pc(body, name, out_shape, grid=None, in_specs=None, out_specs=None, scratch=(), sends=(), gather=False):
    params = pltpu.CompilerParams(vmem_limit_bytes=VMEM_LIMIT)
    if sends and not isinstance(out_shape, (list, tuple)):
        out_shape, out_specs = [out_shape], [out_specs]
    start, wait = (_ag_start, _ag_wait) if gather else (_a2a_start, _a2a_wait)
    if not sends:
        kw = {}
        if grid is not None:
            kw = dict(grid=grid, in_specs=in_specs, out_specs=out_specs)
        elif in_specs is not None:
            kw = dict(in_specs=in_specs, out_specs=out_specs)
        return pl.pallas_call(body, name=name, out_shape=out_shape, scratch_shapes=list(scratch),
                              compiler_params=params, interpret=_INTERPRET, **kw)
    n, nin, nout, nscr = len(sends), len(in_specs), len(out_shape), len(scratch)

    def body2(*refs):
        cin, xs = refs[:nin], refs[nin:nin + n]
        couts, os_ = refs[nin + n:nin + n + nout], refs[nin + n + nout:nin + 2 * n + nout]
        cscr, sems = refs[nin + 2 * n + nout:nin + 2 * n + nout + nscr], refs[nin + 2 * n + nout + nscr:]
        ids = [pl.program_id(a) for a in range(len(grid))]
        first = functools.reduce(lambda a, b: a & b, [i == 0 for i in ids])
        last = functools.reduce(lambda a, b: a & b, [i == g - 1 for i, g in zip(ids, grid)])

        @pl.when(first)
        def _():
            start(xs, os_, *sems)

        body(*cin, *couts, *cscr)

        @pl.when(last)
        def _():
            wait(xs, os_, *sems)

    call = pl.pallas_call(
        body2, name=name,
        out_shape=list(out_shape) + [_sds(((NDEV,) if gather else ()) + a.shape, a.dtype) for a in sends],
        grid=grid, in_specs=list(in_specs) + [_any()] * n, out_specs=list(out_specs) + [_any()] * n,
        scratch_shapes=list(scratch) + _a2a_sems(n), compiler_params=params, interpret=_INTERPRET)

    def run(*args):
        res = call(*args, *sends)
        return res[:nout], res[nout:]

    return run


def _vm():
    return pl.BlockSpec(memory_space=pltpu.VMEM)


def _sds(shape, dt=F32):
    return jax.ShapeDtypeStruct(shape, dt)


def _iota(shape, dim):
    return lax.broadcasted_iota(jnp.int32, shape, dim)


def _silu(x):
    return x * jax.nn.sigmoid(x)


def _softplus(x):
    return jnp.maximum(x, 0.0) + jnp.log1p(jnp.exp(-jnp.abs(x)))


def _normmod(x, g, sh, sc):
    r = lax.rsqrt(jnp.mean(x * x, axis=-1, keepdims=True) + EPS)
    return (x * r * g) * (1.0 + sc) + sh


def _rope(x, cos, sin, rm):
    return x * cos + _exact(x, rm) * sin


def _swap12(x):
    lane = _iota(x.shape, 1)
    up, down = pltpu.roll(x, 192, 1), pltpu.roll(x, 64, 1)
    return jnp.where((lane >= 64) & (lane < 128), up, jnp.where((lane >= 128) & (lane < 192), down, x))


def _acc_init(first, refs):
    @pl.when(first)
    def _():
        for r in refs:
            r[...] = jnp.zeros_like(r)


def _stream(X, TR, nlt):
    if not isinstance(X, tuple):
        return (X,), [pl.BlockSpec((TR, D), lambda i: (i, 0))], lambda refs: refs[0][...]
    specs = [pl.BlockSpec((TR, D), lambda i: (jnp.minimum(i, nlt - 1), 0)), pl.BlockSpec((TR, D), lambda i: (0, 0))]
    return X, specs, lambda refs: jnp.where(pl.program_id(0) < nlt, refs[0][...], refs[1][...])


def in_fwd(X, g, sh, sc, W, cos, sin, rm, L, sends=()):
    T = L + LC
    TR = 256
    nlt = L // TR
    xs, xspecs, xread = _stream(X, TR, nlt)

    def body(*refs):
        (g_ref, sh_ref, sc_ref, w_ref, cos_ref, sin_ref, rm_ref,
         qa, qb, z, ka, va, kb, vb, xbc, dt, hout) = refs[len(xs):]
        h = _normmod(xread(refs), g_ref[...], sh_ref[0], sc_ref[0]).astype(MXU)
        hout[...] = h
        y = lax.dot_general(h, w_ref[...], (((1,), (1,)), ((), ())), preferred_element_type=F32)
        cs, sn, r = cos_ref[...], sin_ref[...], rm_ref[...]
        qa[...] = _rope(_swap12(y[:, C_QA:C_QB]), cs, sn, r).astype(MXU)
        qb[...] = y[:, C_QB:C_Z].astype(MXU)
        z[...] = y[:, C_Z:C_KA]
        ka[...] = _rope(y[:, C_KA:C_VA], cs[:, :128], sn[:, :128], r[:128, :128]).astype(MXU)
        va[...] = y[:, C_VA:C_KB].astype(MXU)
        kb[...] = y[:, C_KB:C_VB].astype(MXU)
        vb[...] = y[:, C_VB:C_XBC].astype(MXU)
        xbc[...] = y[:, C_XBC:C_DT]
        dt[...] = y[:, C_DT:C_DT + 128]

    row = lambda w: pl.BlockSpec((TR, w), lambda i: (i, 0))
    cls = pl.BlockSpec((1, 1, D), lambda i: (i // nlt, 0, 0))
    widths = [(256, MXU), (256, MXU), (512, F32), (128, MXU), (128, MXU), (256, MXU), (256, MXU), (1024, F32),
              (128, F32), (D, MXU)]
    return _pc(body, "in_fwd", [_sds((T, w), d) for w, d in widths], grid=(T // TR,),
               in_specs=xspecs + [pl.BlockSpec((1, D), lambda i: (0, 0)), cls, cls, _vm(), row(256), row(256), _vm()],
               out_specs=[row(w) for w, _ in widths], sends=sends, gather=True)(*xs, g, sh, sc, W, cos, sin, rm)


def in_bwd(X, g, sh, sc, W, cos, sin, rm, dxres, dqa, dqb, dz, dka, dva, dkb, dvb, dxbc, ddt2, L, latent_only):
    T = L + LC
    TR = 256
    nlt = L // TR
    xs, xspecs, xread = _stream(X, TR, nlt)

    def body(*refs):
        (g_ref, sh_ref, sc_ref, w_ref, cos_ref, sin_ref, rm_ref, dxres_ref, dqa_r, dqb_r, dz_r, dka_r,
         dva_r, dkb_r, dvb_r, dxbc_r, ddt0_r, ddt1_r, dx_o, dy_o, dg_o, dsh_o, dsc_o) = refs[len(xs):]
        i = pl.program_id(0)
        cs, sn, r = cos_ref[...], sin_ref[...], rm_ref[...]
        _, vq = jax.vjp(lambda t: _rope(t, cs, sn, r), dqa_r[...])
        _, vk = jax.vjp(lambda t: _rope(t, cs[:, :128], sn[:, :128], r[:128, :128]), dka_r[...])
        dyqa = _swap12(vq(dqa_r[...])[0])
        dyka, = vk(dka_r[...])
        ddt = ddt0_r[0] + ddt1_r[0]
        dy = jnp.concatenate([dyqa, dqb_r[...], dz_r[...], dyka, dva_r[...], dkb_r[...], dvb_r[...], dxbc_r[...],
                              ddt, jnp.zeros((TR, NP_IN - C_DT - 128), F32)], axis=1).astype(MXU)
        dy_o[...] = dy
        dh = jnp.dot(dy, w_ref[...], preferred_element_type=F32)
        _, vp = jax.vjp(_normmod, xread(refs), g_ref[...], sh_ref[0], sc_ref[0])
        dx, dg, dsh, dsc = vp(dh)
        if latent_only:
            @pl.when(i < nlt)
            def _():
                dx_o[...] = dx + dxres_ref[...]
        else:
            dx_o[...] = dx + dxres_ref[...]
        _acc_init(i == 0, [dg_o])
        _acc_init((i == 0) | (i == nlt), [dsh_o, dsc_o])
        dg_o[...] += dg
        dsh_o[0] += dsh
        dsc_o[0] += dsc

    row = lambda w: pl.BlockSpec((TR, w), lambda i: (i, 0))
    cls = pl.BlockSpec((1, 1, D), lambda i: (i // nlt, 0, 0))
    vec = pl.BlockSpec((1, D), lambda i: (0, 0))
    dts = lambda d: pl.BlockSpec((1, TR, 128), lambda i: (d, i, 0))
    dxs = pl.BlockSpec((TR, D), lambda i: (jnp.minimum(i, nlt - 1), 0)) if latent_only else row(D)
    return _pc(body, "in_bwd",
               [_sds((L if latent_only else T, D)), _sds((T, NP_IN), MXU), _sds((1, D)), _sds((2, 1, D)),
                _sds((2, 1, D))],
               grid=(T // TR,),
               in_specs=xspecs + [vec, cls, cls, _vm(), row(256), row(256), _vm(), row(D), row(256), row(256),
                                  row(512), row(128), row(128), row(256), row(256), row(1024), dts(0), dts(1)],
               out_specs=[dxs, row(NP_IN), vec, cls, cls])(
        *xs, g, sh, sc, W, cos, sin, rm, dxres, dqa, dqb, dz, dka, dva, dkb, dvb, dxbc, ddt2, ddt2)


def tn_mm(A, G, bk, bn, out_dtype, ncol=None, col0=0):
    T, K = A.shape
    N = G.shape[1] if ncol is None else ncol
    first = col0 * (N // bn)
    bt = T
    nt = T // bt

    def body(a_ref, g_ref, o_ref, acc):
        t = pl.program_id(2)
        _acc_init(t == 0, [acc])
        acc[...] += lax.dot_general(a_ref[...], g_ref[...], (((0,), (0,)), ((), ())), preferred_element_type=F32)

        @pl.when(t == nt - 1)
        def _():
            o_ref[...] = acc[...].astype(out_dtype)

    return _pc(body, "tn_mm", _sds((K, N), out_dtype), grid=(K // bk, N // bn, nt),
               in_specs=[pl.BlockSpec((bt, bk), lambda k, n, t: (t, k)),
                         pl.BlockSpec((bt, bn), lambda k, n, t: (t, first + n))],
               out_specs=pl.BlockSpec((bk, bn), lambda k, n, t: (k, n)),
               scratch=[pltpu.VMEM((bk, bn), F32)])(A, G)


def _ssm_out(yf, yb, xs, z, dsk, gs):
    y = (yf + yb + dsk * xs) * _silu(z)
    r = lax.rsqrt(jnp.mean(y * y, axis=-1, keepdims=True) + EPS)
    return y * r * gs


def out_fwd(oa, ob, y2, act, z, dsk, gs, W, X, gate, L, sends=()):
    T = L + LC
    TR = 256
    nlt = L // TR
    xs, xspecs, xread = _stream(X, TR, nlt)

    def body(*refs):
        oa_r, ob_r, yf_r, yb_r, xs_r, z_r, dsk_r, gs_r, w_ref, gt_ref, x1_o, cat_o = refs[len(xs):]
        oc = _ssm_out(yf_r[0], yb_r[0], xs_r[...], z_r[...], dsk_r[...], gs_r[...])
        cat = jnp.concatenate([_swap12(oa_r[...]), ob_r[...], oc], axis=1).astype(MXU)
        cat_o[...] = cat
        x1_o[...] = xread(refs) + gt_ref[0] * jnp.dot(cat, w_ref[...], preferred_element_type=F32)

    row = lambda w: pl.BlockSpec((TR, w), lambda i: (i, 0))
    ys = lambda d: pl.BlockSpec((1, TR, 512), lambda i: (d, i, 0))
    cls = pl.BlockSpec((1, 1, D), lambda i: (i // nlt, 0, 0))
    v512 = pl.BlockSpec((1, 512), lambda i: (0, 0))
    return _pc(body, "out_fwd", [_sds((T, D)), _sds((T, D), MXU)], grid=(T // TR,),
               in_specs=xspecs + [row(256), row(256), ys(0), ys(1), row(512), row(512), v512, v512, _vm(), cls],
               out_specs=[row(D), row(D)], sends=sends, gather=True)(*xs, oa, ob, y2, y2, act, z, dsk, gs, W, gate)


def out_bwd(oa, ob, y2, act, z, dsk, gs, W, gate, dX1, L):
    T = dX1.shape[0]
    TR = 256
    nlt = L // TR

    def body(oa_r, ob_r, yf_r, yb_r, xs_r, z_r, dsk_r, gs_r, w_ref, gt_ref, dx1_r,
             doa_o, dob_o, dy_o, dxs_o, dz_o, dmix_o, ddsk_o, dgs_o, dgt_o):
        i = pl.program_id(0)
        w = w_ref[...]

        def f(oa_, ob_, yf, yb, xs, z_, dsk_, gs_, gt):
            oc = _ssm_out(yf, yb, xs, z_, dsk_, gs_)
            return gt * mmw(jnp.concatenate([oa_, ob_, oc], axis=1), w)

        _, vjp = jax.vjp(f, _swap12(oa_r[...]), ob_r[...], yf_r[0], yb_r[0], xs_r[...], z_r[...], dsk_r[...],
                         gs_r[...], gt_ref[0])
        dx1 = dx1_r[...]
        doa, dob, dyf, _, dxs, dz, ddsk, dgs, dgt = vjp(dx1)
        doa_o[...] = _swap12(doa)
        dob_o[...] = dob
        dy_o[...] = dyf
        dxs_o[...] = dxs
        dz_o[...] = dz
        dmix_o[...] = (gt_ref[0] * dx1).astype(MXU)
        _acc_init(i == 0, [ddsk_o, dgs_o])
        _acc_init((i == 0) | (i == nlt), [dgt_o])
        ddsk_o[...] += ddsk
        dgs_o[...] += dgs
        dgt_o[0] += dgt

    row = lambda w: pl.BlockSpec((TR, w), lambda i: (i, 0))
    ys = lambda d: pl.BlockSpec((1, TR, 512), lambda i: (d, i, 0))
    cls = pl.BlockSpec((1, 1, D), lambda i: (i // nlt, 0, 0))
    v512 = pl.BlockSpec((1, 512), lambda i: (0, 0))
    return _pc(body, "out_bwd",
               [_sds((T, 256)), _sds((T, 256)), _sds((T, 512)), _sds((T, 512)), _sds((T, 512)), _sds((T, D), MXU),
                _sds((1, 512)), _sds((1, 512)), _sds((2, 1, D))],
               grid=(T // TR,),
               in_specs=[row(256), row(256), ys(0), ys(1), row(512), row(512), v512, v512, _vm(), cls, row(D)],
               out_specs=[row(256), row(256), row(512), row(512), row(512), row(D), v512, v512, cls])(
        oa, ob, y2, y2, act, z, dsk, gs, W, gate, dX1)


def ffn_fwd(X, g, sh, sc, gate, Win, Wout, L, sends=()):
    T = X.shape[0]
    TR = 256
    nlt = L // TR

    def body(x_ref, g_ref, sh_ref, sc_ref, gt_ref, wi_ref, wo_ref, o_ref, f_ref):
        h = _normmod(x_ref[...], g_ref[...], sh_ref[0], sc_ref[0]).astype(MXU)
        nt = (((1,), (1,)), ((), ()))
        a = lax.dot_general(h, wi_ref[0:DFF, :], nt, preferred_element_type=F32)
        u = lax.dot_general(h, wi_ref[DFF:2 * DFF, :], nt, preferred_element_type=F32)
        act = (_silu(a) * u).astype(MXU)
        ff = jnp.dot(act, wo_ref[...], preferred_element_type=F32)
        f_ref[...] = ff
        o_ref[...] = x_ref[...] + gt_ref[0] * ff

    row = lambda w: pl.BlockSpec((TR, w), lambda i: (i, 0))
    cls = pl.BlockSpec((1, 1, D), lambda i: (i // nlt, 0, 0))
    vec = pl.BlockSpec((1, D), lambda i: (0, 0))
    return _pc(body, "ffn_fwd", [_sds((T, D)), _sds((T, D))], grid=(T // TR,),
               in_specs=[row(D), vec, cls, cls, cls, _vm(), _vm()], out_specs=[row(D), row(D)], sends=sends,
               gather=True)(X, g, sh, sc, gate, Win, Wout)


def ffn_bwd(X, g, sh, sc, gate, Win, Wout, FF, dX2, L, sends=(), nchunk=2):
    T = X.shape[0]
    TR = 256
    nlt = L // TR
    CH = DFF // nchunk

    def body(x_ref, g_ref, sh_ref, sc_ref, gt_ref, wi_ref, wo_ref, ff_r, dx2_r,
             dx_o, h_o, du_o, act_o, dout_o, dg_o, dsh_o, dsc_o, dgt_o):
        i = pl.program_id(0)
        h, vp = jax.vjp(_normmod, x_ref[...], g_ref[...], sh_ref[0], sc_ref[0])
        dx2 = dx2_r[...]
        dout = gt_ref[0] * dx2
        zero = jnp.zeros((TR, CH), F32)
        dh = jnp.zeros((TR, D), F32)
        for c in range(nchunk):
            lo, hi = c * CH, (c + 1) * CH
            wg, wu, wo = wi_ref[lo:hi, :], wi_ref[DFF + lo:DFF + hi, :], wo_ref[lo:hi, :]

            def f(h_, eg, eu):
                act = _silu(mmw_nt(h_, wg) + eg) * (mmw_nt(h_, wu) + eu)
                return mmw(act, wo), act

            _, vjp_c, act = jax.vjp(f, h, zero, zero, has_aux=True)
            dh_c, da, du = vjp_c(dout)
            dh = dh + dh_c
            du_o[:, lo:hi] = da.astype(MXU)
            du_o[:, DFF + lo:DFF + hi] = du.astype(MXU)
            act_o[:, lo:hi] = act.astype(MXU)
        dx, dg, dsh, dsc = vp(dh)
        dx_o[...] = dx + dx2
        h_o[...] = h.astype(MXU)
        dout_o[...] = dout.astype(MXU)
        _acc_init(i == 0, [dg_o])
        _acc_init((i == 0) | (i == nlt), [dsh_o, dsc_o, dgt_o])
        dg_o[...] += dg
        dsh_o[0] += dsh
        dsc_o[0] += dsc
        dgt_o[0] += jnp.sum(dx2 * ff_r[...], axis=0, keepdims=True)

    row = lambda w: pl.BlockSpec((TR, w), lambda i: (i, 0))
    cls = pl.BlockSpec((1, 1, D), lambda i: (i // nlt, 0, 0))
    vec = pl.BlockSpec((1, D), lambda i: (0, 0))
    return _pc(body, "ffn_bwd",
               [_sds((T, D)), _sds((T, D), MXU), _sds((T, 2 * DFF), MXU), _sds((T, DFF), MXU), _sds((T, D), MXU),
                _sds((1, D)), _sds((2, 1, D)), _sds((2, 1, D)), _sds((2, 1, D))],
               grid=(T // TR,),
               in_specs=[row(D), vec, cls, cls, cls, _vm(), _vm(), row(D), row(D)],
               out_specs=[row(D), row(D), row(2 * DFF), row(DFF), row(D), vec, cls, cls, cls], sends=sends)(
        X, g, sh, sc, gate, Win, Wout, FF, dX2)


def loss_head(X2, g, tgt, L):
    T = X2.shape[0]
    TR = 256
    nlt = L // TR

    def body(x_ref, g_ref, t_ref, loss_o, dx_o, dg_o):
        i = pl.program_id(0)
        _acc_init(i == 0, [loss_o, dg_o])

        @pl.when(i < nlt)
        def _():
            def f(x, g_):
                y = x * lax.rsqrt(jnp.mean(x * x, axis=-1, keepdims=True) + EPS) * g_
                return 0.5 * jnp.sum(jnp.mean(jnp.square(y - t_ref[...]), axis=-1, keepdims=True), axis=0,
                                     keepdims=True)

            val, vjp = jax.vjp(f, x_ref[...], g_ref[...])
            dx, dg = vjp(jnp.ones((1, 1), F32))
            dx_o[...] = dx
            loss_o[...] += jnp.broadcast_to(val, (8, 128))
            dg_o[...] += dg

        @pl.when(i >= nlt)
        def _():
            dx_o[...] = jnp.zeros_like(dx_o)

    row = pl.BlockSpec((TR, D), lambda i: (i, 0))
    vec = pl.BlockSpec((1, D), lambda i: (0, 0))
    return _pc(body, "loss_head", [_sds((8, 128)), _sds((T, D)), _sds((1, D))], grid=(T // TR,),
               in_specs=[row, vec, pl.BlockSpec((TR, D), lambda i: (jnp.minimum(i, nlt - 1), 0))],
               out_specs=[pl.BlockSpec((8, 128), lambda i: (0, 0)), row, vec])(X2, g, tgt)


def _stack_impl(q):
    lane = _iota(q.shape, 1)
    return jnp.concatenate([jnp.where(lane < HD, q, 0.0), jnp.where(lane >= HD, q, 0.0)], axis=0)


def _unstack_impl(o):
    M = o.shape[0] // 2
    return jnp.where(_iota((M, o.shape[1]), 1) < HD, o[:M], o[M:])


@jax.custom_vjp
def _stack(q):
    return _stack_impl(q)


_stack.defvjp(lambda q: (_stack_impl(q), None), lambda _, g: (_unstack_impl(g),))


@jax.custom_vjp
def _unstack(o):
    return _unstack_impl(o)


_unstack.defvjp(lambda o: (_unstack_impl(o), None), lambda _, g: (_stack_impl(g),))


def _softmax_av(q, ks, vs, biases, sink):
    q2 = _stack(q)
    ss = []
    for k, b in zip(ks, biases):
        s = mm_nt(q2, k) * (HD ** -0.5)
        ss.append(s if b is None else s + b)
    m = functools.reduce(jnp.maximum, [jnp.max(s, axis=1, keepdims=True) for s in ss])
    if sink is not None:
        m = jnp.maximum(m, sink)
    m = lax.stop_gradient(m)
    es = [jnp.exp(s - m) for s in ss]
    den = functools.reduce(lambda a, b_: a + b_, [jnp.sum(e, axis=1, keepdims=True) for e in es])
    if sink is not None:
        den = den + jnp.exp(sink - m)
    inv = 1.0 / den
    return _unstack(functools.reduce(lambda a, b_: a + b_, [mm(e * inv, v) for e, v in zip(es, vs)]))


def _sink_col(s0, s1, M):
    return jnp.concatenate([jnp.broadcast_to(jnp.mean(s0, axis=1, keepdims=True), (M, 1)),
                            jnp.broadcast_to(jnp.mean(s1, axis=1, keepdims=True), (M, 1))], axis=0)


def _stack4_impl(q):
    lane = _iota((q.shape[0], 128), 1)
    parts = []
    for p in range(2):
        qp = q[:, 128 * p:128 * (p + 1)]
        parts += [jnp.where(lane < HD, qp, 0.0), jnp.where(lane >= HD, qp, 0.0)]
    return jnp.concatenate(parts, axis=0)


def _unstack4_impl(o):
    M = o.shape[0] // 4
    lane = _iota((M, 128), 1)
    return jnp.concatenate([jnp.where(lane < HD, o[0:M], o[M:2 * M]),
                            jnp.where(lane < HD, o[2 * M:3 * M], o[3 * M:4 * M])], axis=1)


@jax.custom_vjp
def _stack4(q):
    return _stack4_impl(q)


_stack4.defvjp(lambda q: (_stack4_impl(q), None), lambda _, g: (_unstack4_impl(g),))


@jax.custom_vjp
def _unstack4(o):
    return _unstack4_impl(o)


_unstack4.defvjp(lambda o: (_unstack4_impl(o), None), lambda _, g: (_stack4_impl(g),))


WA_NB = 4


def _wa_blocks(qs, kws, vws, kx, vx, sks, n0, L):
    sc = HD ** -0.5
    sink = jnp.concatenate([jnp.broadcast_to(jnp.mean(s_, axis=1, keepdims=True), (Q, 1)) for s_ in sks], axis=0)
    bias = []
    for b_ in range(len(qs)):
        n = n0 + b_
        qpos = n * Q + (_iota((4 * Q, 3 * Q), 0) & (Q - 1))
        kpos = (n - 1) * Q + _iota((4 * Q, 3 * Q), 1)
        bias.append(jnp.where((jnp.abs(qpos - kpos) <= Q) & (kpos >= 0) & (kpos < L), 0.0, NEG))
    q4 = [_stack4(q) for q in qs]
    sl = [mm_nt(a, k) * sc + b_ for a, k, b_ in zip(q4, kws, bias)]
    sx = [mm_nt(a, kx) * sc for a in q4]
    m = [lax.stop_gradient(jnp.maximum(jnp.maximum(jnp.max(a, axis=1, keepdims=True),
                                                   jnp.max(b_, axis=1, keepdims=True)), sink))
         for a, b_ in zip(sl, sx)]
    el = [jnp.exp(a - c) for a, c in zip(sl, m)]
    ex = [jnp.exp(a - c) for a, c in zip(sx, m)]
    inv = [1.0 / (jnp.sum(a, axis=1, keepdims=True) + jnp.sum(b_, axis=1, keepdims=True) + jnp.exp(sink - c))
           for a, b_, c in zip(el, ex, m)]
    return [_unstack4(mm(a * i, v) + mm(b_ * i, vx)) for a, b_, i, v in zip(el, ex, inv, vws)]


def _wa_load(q_r, k_r, v_r, n0):
    f = lambda t: t.astype(F32)
    qs = [f(q_r[b_ * Q:(b_ + 1) * Q, :]) for b_ in range(WA_NB)]
    wins = [pl.ds(pl.multiple_of((n0 + b_) * Q, Q), 3 * Q) for b_ in range(WA_NB)]
    return qs, [f(k_r[w, :]) for w in wins], [f(v_r[w, :]) for w in wins], wins


def _wa_specs(L):
    nb = L // Q
    qs = pl.BlockSpec((WA_NB * Q, 256), lambda n: (n, 0))
    kfull = pl.BlockSpec((L + LC + Q, 128), lambda n: (0, 0))
    sks = pl.BlockSpec((2, 2, 1, 128), lambda n: (0, 0, 0, 0))
    return nb, qs, kfull, sks


def wa_fwd(QA, KA, VA, sinkp, L, sends=()):
    nb, qs, kfull, sks = _wa_specs(L)
    pad = lambda a: jnp.concatenate([jnp.zeros((Q, 128), a.dtype), a], axis=0)

    def body(q_r, k_r, v_r, sk_r, o_ref):
        n0 = pl.program_id(0) * WA_NB
        qs_, kws, vws, _ = _wa_load(q_r, k_r, v_r, n0)
        cx = pl.ds(Q + L, LC)
        outs = _wa_blocks(qs_, kws, vws, k_r[cx, :].astype(F32), v_r[cx, :].astype(F32),
                          [sk_r[0, 0], sk_r[0, 1], sk_r[1, 0], sk_r[1, 1]], n0, L)
        o_ref[...] = jnp.concatenate(outs, axis=0)

    return _pc(body, "wa_fwd", _sds((L, 256)), grid=(nb // WA_NB,), in_specs=[qs, kfull, kfull, sks], out_specs=qs,
               sends=sends, gather=True)(QA, pad(KA), pad(VA), sinkp)


def wa_bwd(QA, KA, VA, sinkp, dO, L, sends=()):
    nb, qs, kfull, sks = _wa_specs(L)
    pad = lambda a: jnp.concatenate([jnp.zeros((Q, 128), a.dtype), a], axis=0)

    def body(q_r, k_r, v_r, sk_r, do_r, dq_o, dk_o, dv_o, dsk_o):
        n0 = pl.program_id(0) * WA_NB
        _acc_init(n0 == 0, [dk_o, dv_o, dsk_o])
        qs_, kws, vws, wins = _wa_load(q_r, k_r, v_r, n0)
        cx = pl.ds(Q + L, LC)
        fn = lambda a, b, c, d, e, s_: _wa_blocks(a, b, c, d, e, s_, n0, L)
        _, vjp = jax.vjp(fn, qs_, kws, vws, k_r[cx, :].astype(F32), v_r[cx, :].astype(F32),
                         [sk_r[0, 0], sk_r[0, 1], sk_r[1, 0], sk_r[1, 1]])
        dqs, dkws, dvws, dkx, dvx, ds = vjp([do_r[b_ * Q:(b_ + 1) * Q, :] for b_ in range(WA_NB)])
        dq_o[...] = jnp.concatenate(dqs, axis=0)
        for w, dk, dv in zip(wins, dkws, dvws):
            dk_o[w, :] += dk
            dv_o[w, :] += dv
        dk_o[cx, :] += dkx
        dv_o[cx, :] += dvx
        for i_ in range(4):
            dsk_o[i_ // 2, i_ % 2] += ds[i_]

    return _pc(body, "wa_bwd", [_sds((L, 256)), _sds((L + LC + Q, 128)), _sds((L + LC + Q, 128)),
                                _sds((2, 2, 1, 128))],
               grid=(nb // WA_NB,), in_specs=[qs, kfull, kfull, sks, qs], out_specs=[qs, kfull, kfull, sks],
               sends=sends)(QA, pad(KA), pad(VA), sinkp, dO)


def _ctx_block(q, kx, vx, s0, s1):
    return _softmax_av(q, [kx], [vx], [None], _sink_col(s0, s1, LC))


def ctx_fwd(Qx, Kx, Vx, sinkp, shared, L):
    cq = pl.BlockSpec((LC, 128), lambda p: (L // LC, p))
    ck = pl.BlockSpec((LC, 128), lambda p: (L // LC, 0 if shared else p))
    sks = pl.BlockSpec((1, 2, 1, 128), lambda p: (p, 0, 0, 0))

    def body(q_r, k_r, v_r, sk_r, o_ref):
        f = lambda t: t[...].astype(F32)
        o_ref[...] = _ctx_block(f(q_r), f(k_r), f(v_r), sk_r[0, 0], sk_r[0, 1])

    return _pc(body, "ctx_fwd", _sds((LC, 256)), grid=(2,), in_specs=[cq, ck, ck, sks],
               out_specs=pl.BlockSpec((LC, 128), lambda p: (0, p)))(Qx, Kx, Vx, sinkp)


def ctx_bwd(Qx, Kx, Vx, sinkp, dO, shared, L):
    cq = pl.BlockSpec((LC, 128), lambda p: (L // LC, p))
    ck = pl.BlockSpec((LC, 128), lambda p: (L // LC, 0 if shared else p))
    sks = pl.BlockSpec((1, 2, 1, 128), lambda p: (p, 0, 0, 0))
    op = pl.BlockSpec((LC, 128), lambda p: (0, p))
    ok = pl.BlockSpec((LC, 128), lambda p: (0, 0 if shared else p))
    dos = pl.BlockSpec((LC, 128), lambda p: (L // LC, p))

    def body(q_r, k_r, v_r, sk_r, do_r, dq_o, dk_o, dv_o, dsk_o):
        p = pl.program_id(0)
        f = lambda t: t[...].astype(F32)
        _, vjp = jax.vjp(_ctx_block, f(q_r), f(k_r), f(v_r), sk_r[0, 0], sk_r[0, 1])
        dq, dk, dv, ds0, ds1 = vjp(do_r[...])
        dq_o[...] = dq
        _acc_init((p == 0) if shared else (p >= 0), [dk_o, dv_o])
        dk_o[...] += dk
        dv_o[...] += dv
        dsk_o[0, 0] = ds0
        dsk_o[0, 1] = ds1

    kw = 128 if shared else 256
    return _pc(body, "ctx_bwd", [_sds((LC, 256)), _sds((LC, kw)), _sds((LC, kw)), _sds((2, 2, 1, 128))],
               grid=(2,), in_specs=[cq, ck, ck, sks, dos], out_specs=[op, ok, ok, sks])(Qx, Kx, Vx, sinkp, dO)


def _na_rows(qs, kws, vws, kx, vx, bs):
    sc = HD ** -0.5
    q2 = [_stack(q) for q in qs]
    sl = [mm_nt(a, k) * sc + b for a, k, b in zip(q2, kws, bs)]
    sx = [mm_nt(a, kx) * sc for a in q2]
    m = [lax.stop_gradient(jnp.maximum(jnp.max(a, axis=1, keepdims=True), jnp.max(b, axis=1, keepdims=True)))
         for a, b in zip(sl, sx)]
    el = [jnp.exp(a - c) for a, c in zip(sl, m)]
    ex = [jnp.exp(a - c) for a, c in zip(sx, m)]
    inv = [1.0 / (jnp.sum(a, axis=1, keepdims=True) + jnp.sum(b, axis=1, keepdims=True)) for a, b in zip(el, ex)]
    o2 = [mm(a * i, v) + mm(b * i, vx) for a, b, i, v in zip(el, ex, inv, vws)]
    return [_unstack(o) for o in o2]


NA_ROWS = 16


def _na_geom(r, R):
    s = jnp.clip(r - 4, 0, R - 8)
    cls = jnp.where(r < 4, r, jnp.where(r > R - 4, r - (R - 8), 4))
    return pl.ds(pl.multiple_of(s * GW, GW), 8 * GW), cls


def _na_load(q_r, k_r, v_r, b_r, rb, R):
    nr = min(NA_ROWS, R)
    geo = [_na_geom(rb * nr + j, R) for j in range(nr)]
    qs = [q_r[j * GW:(j + 1) * GW, :].astype(F32) for j in range(nr)]
    kws = [k_r[win, :].astype(F32) for win, _ in geo]
    vws = [v_r[win, :].astype(F32) for win, _ in geo]
    bs = [jnp.concatenate([b_r[0, cls], b_r[1, cls]], axis=0) for _, cls in geo]
    return geo, qs, kws, vws, bs


def na_fwd(QB, KB, VB, biasd, L, sends=()):
    R = L // GW
    nr = min(NA_ROWS, R)
    qs = pl.BlockSpec((nr * GW, 128), lambda p, rb: (rb, p))
    kfull = pl.BlockSpec((L, 128), lambda p, rb: (0, p))
    kctx = pl.BlockSpec((LC, 128), lambda p, rb: (L // LC, p))
    bs = pl.BlockSpec((2, 8, GW, 8 * GW), lambda p, rb: (p, 0, 0, 0))

    def body(q_r, k_r, v_r, kx_r, vx_r, b_r, o_ref):
        _, qs_, kws, vws, bs_ = _na_load(q_r, k_r, v_r, b_r, pl.program_id(1), R)
        outs = _na_rows(qs_, kws, vws, kx_r[...].astype(F32), vx_r[...].astype(F32), bs_)
        o_ref[...] = jnp.concatenate(outs, axis=0)

    return _pc(body, "na_fwd", _sds((L, 256)), grid=(2, R // nr), in_specs=[qs, kfull, kfull, kctx, kctx, bs],
               out_specs=qs, sends=sends, gather=True)(QB, KB, VB, KB, VB, biasd)


def na_bwd(QB, KB, VB, biasd, dO, L):
    R = L // GW
    nr = min(NA_ROWS, R)
    qs = pl.BlockSpec((nr * GW, 128), lambda p, rb: (rb, p))
    kfull = pl.BlockSpec((L, 128), lambda p, rb: (0, p))
    kctx = pl.BlockSpec((LC, 128), lambda p, rb: (L // LC, p))
    bs = pl.BlockSpec((2, 8, GW, 8 * GW), lambda p, rb: (p, 0, 0, 0))
    oc = pl.BlockSpec((LC, 128), lambda p, rb: (0, p))

    def body(q_r, k_r, v_r, kx_r, vx_r, b_r, do_r, dq_o, dk_o, dv_o, dkx_o, dvx_o, db_o):
        rb = pl.program_id(1)
        _acc_init(rb == 0, [dk_o, dv_o, dkx_o, dvx_o, db_o])
        geo, qs_, kws, vws, bs_ = _na_load(q_r, k_r, v_r, b_r, rb, R)
        _, vjp = jax.vjp(_na_rows, qs_, kws, vws, kx_r[...].astype(F32), vx_r[...].astype(F32), bs_)
        dqs, dkws, dvws, dkx, dvx, dbs = vjp([do_r[j * GW:(j + 1) * GW, :] for j in range(nr)])
        dq_o[...] = jnp.concatenate(dqs, axis=0)
        dkx_o[...] += dkx
        dvx_o[...] += dvx
        for j, (win, cls) in enumerate(geo):
            dk_o[win, :] += dkws[j]
            dv_o[win, :] += dvws[j]
            db_o[0, cls] += dbs[j][:GW]
            db_o[1, cls] += dbs[j][GW:]

    return _pc(body, "na_bwd",
               [_sds((L, 256)), _sds((L, 256)), _sds((L, 256)), _sds((LC, 256)), _sds((LC, 256)),
                _sds((4, 8, GW, 8 * GW))],
               grid=(2, R // nr), in_specs=[qs, kfull, kfull, kctx, kctx, bs, qs],
               out_specs=[qs, kfull, kfull, oc, oc, bs])(QB, KB, VB, KB, VB, biasd, dO)


def exact_mm_call(A, B):
    def body(a_ref, b_ref, o_ref):
        o_ref[...] = _exact(a_ref[...], b_ref[...])

    return _pc(body, "exact_mm", _sds((A.shape[0], B.shape[1])))(A, B)


def _conv_shift(x, d, L):
    T = x.shape[0]
    if d == 0:
        return x
    t = _iota(x.shape, 0)
    src = t + d
    ok = (src >= 0) & (src < T) & ((src >= L) == (t >= L))
    return jnp.where(ok, pltpu.roll(x, (-d) % T, 0), 0.0)


def conv_fwd(XBC, w8, b, L, sends=()):
    T = XBC.shape[0]

    def body(x_ref, w_ref, b_ref, o_ref):
        x = x_ref[...]
        pre = b_ref[...] + functools.reduce(
            lambda a, c: a + c, [_conv_shift(x, k - 3, L) * w_ref[k:k + 1, :] for k in range(7)])
        o_ref[...] = _silu(pre)

    col = pl.BlockSpec((T, 128), lambda j: (0, j))
    return _pc(body, "conv_fwd", _sds((T, 1024)), grid=(8,),
               in_specs=[col, pl.BlockSpec((8, 128), lambda j: (0, j)), pl.BlockSpec((1, 128), lambda j: (0, j))],
               out_specs=col, sends=sends, gather=True)(XBC, w8, b)


def conv_bwd(XBC, w8, b, dS, dxs_skip, L, sends=()):
    T = XBC.shape[0]

    def body(x_ref, w_ref, b_ref, d0_r, d1_r, dsk_r, dx_o, dw_o, db_o):
        j = pl.program_id(0)
        x = x_ref[...]
        xs = [_conv_shift(x, k - 3, L) for k in range(7)]
        pre = b_ref[...] + functools.reduce(lambda a, c: a + c, [xs[k] * w_ref[k:k + 1, :] for k in range(7)])
        _, vjp = jax.vjp(_silu, pre)
        dact = d0_r[0] + d1_r[0] + jnp.where(j < 4, dsk_r[...], 0.0)
        dpre, = vjp(dact)
        dx_o[...] = functools.reduce(
            lambda a, c: a + c, [_conv_shift(dpre, 3 - k, L) * w_ref[k:k + 1, :] for k in range(7)])
        dw_o[...] = jnp.concatenate([jnp.sum(dpre * xs[k], axis=0, keepdims=True) for k in range(7)]
                                    + [jnp.zeros((1, 128), F32)], axis=0)
        db_o[...] = jnp.sum(dpre, axis=0, keepdims=True)

    col = pl.BlockSpec((T, 128), lambda j: (0, j))
    w_s = pl.BlockSpec((8, 128), lambda j: (0, j))
    b_s = pl.BlockSpec((1, 128), lambda j: (0, j))
    ds = lambda d: pl.BlockSpec((1, T, 128), lambda j: (d, 0, j))
    return _pc(body, "conv_bwd", [_sds((T, 1024)), _sds((8, 1024)), _sds((1, 1024))], grid=(8,),
               in_specs=[col, w_s, b_s, ds(0), ds(1), pl.BlockSpec((T, 128), lambda j: (0, jnp.minimum(j, 3)))],
               out_specs=[col, w_s, b_s], sends=sends)(XBC, w8, b, dS, dS, dxs_skip)


def _ssd_chunk(xs, bs, cs, dtraw, dtb, alog, hs, tri, d):
    dt = _softplus(dtraw + dtb)
    a = dt * (-jnp.exp(alog))
    acum = _exact(tri, a)
    tot = jnp.sum(a, axis=0, keepdims=True)
    wcol = jnp.exp(tot - acum) * dt
    ea = jnp.exp(acum)
    cd = jnp.exp(tot)
    acum_t, dt_t = acum.T, dt.T
    lane = _iota((Q, 128), 1)
    srow = _iota((128, Q), 0)
    lane1 = _iota((1, 128), 1)
    prow = _iota((128, NSTATE), 0)
    mask = tri > 0.5
    cbs = [mm_nt(cs[g], bs[g]) for g in range(2)]
    ys, hn = [], []
    for j in range(4):
        g = j // 2
        x = xs[j]
        yi, st, eac, cdl = [], [], [], []
        for u in range(2):
            slot = d * 8 + 2 * j + u
            col = lambda m: jnp.sum(jnp.where(lane == slot, m, 0.0), axis=1, keepdims=True)
            rowv = lambda m: jnp.sum(jnp.where(srow == slot, m, 0.0), axis=0, keepdims=True)
            seg = col(acum) - rowv(acum_t)
            dcy = jnp.where(mask, jnp.exp(jnp.where(mask, seg, 0.0)), 0.0)
            yi.append(mm(cbs[g] * dcy * rowv(dt_t), x))
            st.append(mm_tn(x, bs[g] * col(wcol)))
            eac.append(col(ea))
            cdl.append(jnp.sum(jnp.where(lane1 == slot, cd, 0.0), axis=1, keepdims=True))
        yin = mm_nt(cs[g], hs[j])
        ys.append(jnp.where(lane < HD, yi[0] + yin * eac[0], yi[1] + yin * eac[1]))
        hn.append(hs[j] * jnp.where(prow < HD, cdl[0], cdl[1]) + jnp.where(prow < HD, st[0], st[1]))
    return ys, hn


SSD_SUB = 2


def _ssd_block_idx(d, s, nlb, nbk):
    return jnp.where(d == 0, (s + nlb) % nbk, nbk - 1 - s)


def _ssd_rows(d, i):
    return pl.ds(pl.multiple_of(jnp.where(d == 0, i, SSD_SUB - 1 - i) * Q, Q), Q)


def _ssd_split(a):
    return ([a[:, 128 * j:128 * (j + 1)] for j in range(4)], [a[:, 512 + 128 * g:640 + 128 * g] for g in range(2)],
            [a[:, 768 + 128 * g:896 + 128 * g] for g in range(2)])


def ssd_fwd(ACT, DT, dtb, alog, tri2, L, sends=()):
    T = ACT.shape[0]
    RB = SSD_SUB * Q
    nlb, nbk = L // RB, T // RB

    def body(a_ref, dt_ref, dtb_ref, al_ref, tri_ref, y_o, hs_o, hst):
        d, s = pl.program_id(0), pl.program_id(1)
        _acc_init(s == 0, [hst])
        for i in range(SSD_SUB):
            rows = _ssd_rows(d, i)
            xs, bs, cs = _ssd_split(a_ref[rows, :])
            hs_o[0, i] = hst[...]
            ys, hn = _ssd_chunk(xs, bs, cs, dt_ref[rows, :], dtb_ref[...], al_ref[...], [hst[j] for j in range(4)],
                                tri_ref[0], d)
            y_o[0, rows, :] = jnp.concatenate(ys, axis=1)
            for j in range(4):
                hst[j] = hn[j]

    bk = lambda w: pl.BlockSpec((RB, w), lambda d, s: (_ssd_block_idx(d, s, nlb, nbk), 0))
    v128 = pl.BlockSpec((1, 128), lambda d, s: (0, 0))
    return _pc(body, "ssd_fwd", [_sds((2, T, 512)), _sds((2, T // Q, 4, 128, NSTATE))], grid=(2, nbk),
               in_specs=[bk(1024), bk(128), v128, v128, pl.BlockSpec((1, Q, Q), lambda d, s: (d, 0, 0))],
               out_specs=[pl.BlockSpec((1, RB, 512), lambda d, s: (d, _ssd_block_idx(d, s, nlb, nbk), 0)),
                          pl.BlockSpec((1, SSD_SUB, 4, 128, NSTATE), lambda d, s: (d, s, 0, 0, 0))],
               scratch=[pltpu.VMEM((4, 128, NSTATE), F32)], sends=sends, gather=True)(ACT, DT, dtb, alog, tri2)


def ssd_bwd(ACT, DT, dtb, alog, tri2, HS, dY, L, sends=()):
    T = ACT.shape[0]
    RB = SSD_SUB * Q
    nlb, nbk = L // RB, T // RB

    def body(a_ref, dt_ref, dtb_ref, al_ref, tri_ref, hs_ref, dy_ref, da_o, ddt_o, ddtb_o, dal_o, dh):
        d, sr = pl.program_id(0), pl.program_id(1)
        _acc_init(sr == 0, [dh, ddtb_o, dal_o])
        tri = tri_ref[0]
        fn = lambda xs_, bs_, cs_, dtr, dtb_, al, hs_: _ssd_chunk(xs_, bs_, cs_, dtr, dtb_, al, hs_, tri, d)
        for i in reversed(range(SSD_SUB)):
            rows = _ssd_rows(d, i)
            xs, bs, cs = _ssd_split(a_ref[rows, :])
            _, vjp = jax.vjp(fn, xs, bs, cs, dt_ref[rows, :], dtb_ref[...], al_ref[...],
                             [hs_ref[0, i, j] for j in range(4)])
            dy = dy_ref[rows, :]
            dxs, dbs, dcs, ddt, ddtb, dal, dhs = vjp(([dy[:, 128 * j:128 * (j + 1)] for j in range(4)],
                                                      [dh[j] for j in range(4)]))
            da_o[0, rows, :] = jnp.concatenate(dxs + dbs + dcs, axis=1)
            ddt_o[0, rows, :] = ddt
            ddtb_o[0] += ddtb
            dal_o[0] += dal
            for j in range(4):
                dh[j] = dhs[j]

    bidx = lambda d, sr: _ssd_block_idx(d, nbk - 1 - sr, nlb, nbk)
    bk = lambda w: pl.BlockSpec((RB, w), lambda d, sr: (bidx(d, sr), 0))
    v128 = pl.BlockSpec((1, 128), lambda d, sr: (0, 0))
    o128 = pl.BlockSpec((1, 1, 128), lambda d, sr: (d, 0, 0))
    return _pc(body, "ssd_bwd", [_sds((2, T, 1024)), _sds((2, T, 128)), _sds((2, 1, 128)), _sds((2, 1, 128))],
               grid=(2, nbk),
               in_specs=[bk(1024), bk(128), v128, v128, pl.BlockSpec((1, Q, Q), lambda d, sr: (d, 0, 0)),
                         pl.BlockSpec((1, SSD_SUB, 4, 128, NSTATE), lambda d, sr: (d, nbk - 1 - sr, 0, 0, 0)), bk(512)],
               out_specs=[pl.BlockSpec((1, RB, 1024), lambda d, sr: (d, bidx(d, sr), 0)),
                          pl.BlockSpec((1, RB, 128), lambda d, sr: (d, bidx(d, sr), 0)), o128, o128],
               scratch=[pltpu.VMEM((4, 128, NSTATE), F32)], sends=sends)(ACT, DT, dtb, alog, tri2, HS, dY)


_PAIR_HEADS = np.array([[0, 2], [1, 3]])


def _tables(L):
    t = jnp.arange(L)
    inv = 10000.0 ** (-jnp.arange(16, dtype=F32) / 16)

    def half(pos):
        ang = pos.astype(F32)[:, None] * inv[None, :]
        return jnp.concatenate([ang, ang], axis=1)

    ang = jnp.tile(jnp.concatenate([half(t // GW), half(t % GW)], axis=1), (1, 4))
    cos = jnp.concatenate([jnp.cos(ang), jnp.ones((LC, 256), F32)], axis=0)
    sin = jnp.concatenate([jnp.sin(ang), jnp.zeros((LC, 256), F32)], axis=0)
    rm = np.zeros((256, 256), np.float32)
    for j in range(256):
        if j % 32 < 16:
            rm[j + 16, j] = -1.0
        else:
            rm[j - 16, j] = 1.0
    tri = np.tril(np.ones((Q, Q), np.float32))
    return cos, sin, jnp.asarray(rm), jnp.asarray(np.stack([tri, tri.T]))


def _na_index(R):
    rc = np.array([0, 1, 2, 3, 4, R - 3, R - 2, R - 1])
    dy = np.clip(rc - 4, 0, R - 8)[:, None] + np.arange(8)[None, :] - rc[:, None] + 7
    qc, cc = np.arange(GW)[:, None], np.arange(GW)[None, :]
    dx = np.clip(cc - qc, -15, 15) + 15
    cstart = np.clip(qc - 8, 0, GW - 16)
    cmask = (cc >= cstart) & (cc < cstart + 16)
    idx = dy[:, None, :, None] * 31 + dx[None, :, None, :]
    return idx.reshape(8, GW, 8 * GW), np.broadcast_to(cmask[None, :, None, :], idx.shape).reshape(8, GW, 8 * GW), \
        dy, dx, cmask


def _na_bias(rpb, R):
    _, cm, dy, _, _ = _na_index(R)
    rows = rpb[:, dy.reshape(-1), :].reshape(4, 8, 4, 2, 31)
    p2 = jnp.pad(jnp.pad(rows, ((0, 0),) * 4 + ((0, 33),)).reshape(4, 8, 4, 128), ((0, 0), (0, 0), (0, 4), (0, 0)))
    negmask = jnp.asarray(np.where(cm[0], 0.0, NEG).astype(np.float32))

    def body(p_ref, m_ref, o_ref):
        for c in range(8):
            tiles = [pltpu.roll(jnp.broadcast_to(p_ref[0, c, jp:jp + 1, :], (GW, 128)), 113, 1, stride=1,
                                stride_axis=0) for jp in range(4)]
            o_ref[0, c] = jnp.where(m_ref[...] < 0.0, NEG, jnp.concatenate(tiles, axis=1))

    return _pc(body, "na_bias", _sds((4, 8, GW, 8 * GW)), grid=(4,),
               in_specs=[pl.BlockSpec((1, 8, 8, 128), lambda h: (h, 0, 0, 0)),
                         pl.BlockSpec((GW, 8 * GW), lambda h: (0, 0))],
               out_specs=pl.BlockSpec((1, 8, GW, 8 * GW), lambda h: (h, 0, 0, 0)))(p2, negmask)


def _na_bias_grad(dbias, R):
    _, _, dy, dx, cmask = _na_index(R)
    e1 = np.zeros((GW * GW, 128), np.float32)
    e1[np.arange(GW * GW), dx.reshape(-1)] = cmask.reshape(-1)
    a1 = dbias.reshape(4, 8, GW, 8, GW).transpose(0, 1, 3, 2, 4).reshape(256, GW * GW)
    v = exact_mm_call(a1, jnp.asarray(e1))[:, :31].reshape(4, 64, 31)
    e2 = np.zeros((64, 128), np.float32)
    e2[np.arange(64), dy.reshape(-1)] = 1.0
    a2 = jnp.pad(v.transpose(0, 2, 1).reshape(124, 64), ((0, 4), (0, 0)))
    return exact_mm_call(a2, jnp.asarray(e2))[:124, :15].reshape(4, 31, 15).transpose(0, 2, 1)


def _lanes(v, n=128):
    v = v.reshape(1, -1)
    return jnp.pad(v, ((0, 0), (0, n - v.shape[1])))


def _cls2(a, b):
    return jnp.stack([a, b]).reshape(2, 1, D)


def _win_p(g):
    return jnp.concatenate([g.reshape(IN_COLS, D), jnp.zeros((NP_IN - IN_COLS, D), g.dtype)], axis=0)


def _layer_consts(p):
    sinkp = jnp.broadcast_to(p["wa_sink"][_PAIR_HEADS][:, :, None, None], (2, 2, 1, 128))
    return dict(
        sinkp=sinkp, nosink=jnp.full((2, 2, 1, 128), NEG, F32),
        w8=jnp.concatenate([p["ssm_conv_w"], jnp.zeros((1, 1024), F32)], axis=0),
        cb=p["ssm_conv_b"].reshape(1, 1024), dtb=_lanes(p["ssm_dt_bias"]), alog=_lanes(p["ssm_a_log"]),
        dsk=jnp.repeat(p["ssm_d"], HD).reshape(1, 512), gs=p["ssm_norm_g"].reshape(1, 512),
        gmix=p["g_mix"].reshape(1, D), gffn=p["g_ffn"].reshape(1, D))


def _mods(mod2):
    return [_cls2(mod2[0, D * k:D * (k + 1)], mod2[1, D * k:D * (k + 1)]) for k in range(6)]


def _layer_fwd(X, mod2, c, rpb, tabs, L, ctx_out, shards, nxt):
    cos, sin, rm, tri2 = tabs
    sh1, sc1, gt1, sh2, sc2, gt2 = _mods(mod2)
    biasd = _na_bias(rpb, L // GW)
    fi, fo, wo = shards
    fcut, fcut2, ocut = 384, 576, 224
    (qa, qb, z, ka, va, kb, vb, xbc, dt, h1), (gfo_a,) = in_fwd(X, c["gmix"], sh1, sc1, c["win"], cos, sin, rm, L,
                                                                sends=(fo[:ocut],))
    (oa,), (gfi_b,) = wa_fwd(qa, ka, va, c["sinkp"], L, sends=(fi[fcut:fcut2],))
    (ob,), (gwo,) = na_fwd(qb, kb, vb, biasd, L, sends=(wo,))
    c = dict(c, wout=gwo.reshape(D, D))
    if ctx_out:
        oa_c = ctx_fwd(qa, ka, va, c["sinkp"], True, L)
        ob_c = ctx_fwd(qb, kb, vb, c["nosink"], False, L)
    else:
        oa_c = ob_c = jnp.zeros((LC, 256), F32)
    oa = jnp.concatenate([oa, oa_c], axis=0)
    ob = jnp.concatenate([ob, ob_c], axis=0)
    (act,), (gfi_c,) = conv_fwd(xbc, c["w8"], c["cb"], L, sends=(fi[fcut2:],))
    (y2, hs), (gfi_a,) = ssd_fwd(act, dt, c["dtb"], c["alog"], tri2, L, sends=(fi[:fcut],))
    (X1, cat), (gfo_b,) = out_fwd(oa, ob, y2, act, z, c["dsk"], c["gs"], c["wout"], X, gt1, L, sends=(fo[ocut:],))
    c = dict(c, wfi=jnp.concatenate([gfi_a, gfi_b, gfi_c], axis=1).reshape(2 * DFF, D),
             wfo=jnp.concatenate([gfo_a, gfo_b], axis=1).reshape(DFF, D))
    res = ffn_fwd(X1, c["gffn"], sh2, sc2, gt2, c["wfi"], c["wfo"], L, sends=nxt)
    (X2, ff), got = res if nxt else (res, ())
    saved = dict(X=X, X1=X1, ff=ff, qa=qa, qb=qb, z=z, ka=ka, va=va, kb=kb, vb=vb, xbc=xbc, dt=dt, h1=h1, oa=oa, ob=ob,
                 act=act, y2=y2, hs=hs, cat=cat, biasd=biasd)
    return X2, saved, c, got


def _row_blocks(gw):
    return gw.reshape(NDEV, gw.shape[0] // NDEV, gw.shape[1])


def _layer_bwd(dX2, s, mod2, c, tabs, L, ctx_out, carry):
    cos, sin, rm, tri2 = tabs
    sh1, sc1, gt1, sh2, sc2, gt2 = _mods(mod2)
    R = L // GW
    res = ffn_bwd(s["X1"], c["gffn"], sh2, sc2, gt2, c["wfi"], c["wfo"], s["ff"], dX2, L, sends=carry)
    (dX1, h2, dU, actf, dOut, dgffn, dsh2, dsc2, dgt2), got = res if carry else (res, ())
    g = {}
    gfi = _row_blocks(tn_mm(dU, h2, 512, 1024, MXU))
    gfo = _row_blocks(tn_mm(actf, dOut, 256, 1024, MXU))
    doa, dob, dy, dxs_skip, dz, dmix, ddsk, dgs, dgt1 = out_bwd(s["oa"], s["ob"], s["y2"], s["act"], s["z"], c["dsk"],
                                                                c["gs"], c["wout"], gt1, dX1, L)
    gout = _row_blocks(tn_mm(s["cat"], dmix, 512, 1024, MXU))
    (dS, ddt2, ddtb, dal), (g["w_ffn_in"],) = ssd_bwd(
        s["act"], s["dt"], c["dtb"], c["alog"], tri2, s["hs"], dy, L, sends=(gfi,))
    (dxbc, dw8, dcb), (g["w_ffn_out"],) = conv_bwd(s["xbc"], c["w8"], c["cb"], dS, dxs_skip, L, sends=(gfo,))
    (dqa, dka, dva, dska), (g["w_out"],) = wa_bwd(s["qa"], s["ka"], s["va"], c["sinkp"], doa, L, sends=(gout,))
    dka, dva = dka[Q:], dva[Q:]
    dqb, dkb, dvb, dkxb, dvxb, dbias = na_bwd(s["qb"], s["kb"], s["vb"], s["biasd"], dob, L)
    if ctx_out:
        dqa_c, dk1, dv1, dsk1 = ctx_bwd(s["qa"], s["ka"], s["va"], c["sinkp"], doa, True, L)
        dqb_c, dk2, dv2, _ = ctx_bwd(s["qb"], s["kb"], s["vb"], c["nosink"], dob, False, L)
        dka = jnp.concatenate([dka[:L], dka[L:] + dk1], axis=0)
        dva = jnp.concatenate([dva[:L], dva[L:] + dv1], axis=0)
        dska = dska + dsk1
        dkxb, dvxb = dkxb + dk2, dvxb + dv2
    else:
        dqa_c = dqb_c = jnp.zeros((LC, 256), F32)
    cat0 = lambda a, b: jnp.concatenate([a, b], axis=0)
    dX, dycat, dgmix, dsh1, dsc1 = in_bwd(
        s["X"], c["gmix"], sh1, sc1, c["win"], cos, sin, rm, dX1, cat0(dqa, dqa_c), cat0(dqb, dqb_c), dz,
        dka, dva, cat0(dkb, dkxb), cat0(dvb, dvxb), dxbc, ddt2, L,
        latent_only=ctx_out)
    if ctx_out:
        gin = [_row_blocks(tn_mm(dycat, s["h1"], 512, D // 2, MXU, ncol=D // 2, col0=k)[:IN_COLS]) for k in (0, 1)]
    else:
        gin = _row_blocks(tn_mm(dycat, s["h1"], 512, 1024, MXU)[:IN_COLS])
    g["g_mix"] = dgmix.reshape(D)
    g["g_ffn"] = dgffn.reshape(D)
    sk = jnp.sum(dska, axis=(2, 3))
    g["wa_sink"] = jnp.zeros((4,), F32).at[_PAIR_HEADS.reshape(-1)].set(sk.reshape(-1))
    g["na_rpb"] = _na_bias_grad(dbias, R)
    g["ssm_conv_w"] = dw8[:7]
    g["ssm_conv_b"] = dcb.reshape(1024)
    g["ssm_dt_bias"] = (ddtb[0] + ddtb[1])[0, :16].reshape(2, 8)
    g["ssm_a_log"] = (dal[0] + dal[1])[0, :16].reshape(2, 8)
    g["ssm_d"] = jnp.sum(ddsk.reshape(8, HD), axis=1)
    g["ssm_norm_g"] = dgs.reshape(512)
    dmod2 = jnp.concatenate([dsh1, dsc1, dgt1, dsh2, dsc2, dgt2], axis=2).reshape(2, 6 * D)
    return dX, g, dmod2, gin, got


def local_step(x, ctx, tgt, mods, layers, shards, g_final, L):
    tabs = _tables(L)
    X = (x, ctx)
    consts = [_layer_consts(p) for p in layers]
    saved = []
    got = (shards["w_in_first"],)
    for i in range(2):
        consts[i] = dict(consts[i], win=_win_p(got[0]))
        nxt = (shards["w_in"][1],) if i == 0 else ()
        X, s, consts[i], got = _layer_fwd(X, mods[i], consts[i], layers[i]["na_rpb"], tabs, L, i == 0,
                                          (shards["w_ffn_in"][i], shards["w_ffn_out"][i], shards["w_out"][i]), nxt)
        saved.append(s)
    loss8, dX, dgfin = loss_head(X, g_final.reshape(1, D), tgt, L)
    grads, dmods = [None, None], [None, None]
    dX, grads[1], dmods[1], gin1, _ = _layer_bwd(dX, saved[1], mods[1], consts[1], tabs, L, False, ())
    dX, grads[0], dmods[0], gin0, (grads[1]["w_in"],) = _layer_bwd(dX, saved[0], mods[0], consts[0], tabs, L, True,
                                                                   (gin1,))
    return loss8[0, 0], dX, grads, jnp.stack(dmods), dgfin.reshape(D), gin0


def _place():
    x, y, c = lax.axis_index("x"), lax.axis_index("y"), lax.axis_index("c")
    return x, y, c


def _slot(b):
    return 4 * b[0] + 2 * b[1] + b[2]


def _any():
    return pl.BlockSpec(memory_space=pl.ANY)


def all_gather(xs, name):
    n = len(xs)

    def body(*refs):
        x_refs, o_refs = refs[:n], refs[n:2 * n]
        send_sems, recv_sems, local_sems = refs[2 * n:]
        x, y, c = _place()
        me, sib = (x, y, c), (x, y, 1 - c)
        chips = [(1 - x, y), (x, 1 - y), (1 - x, 1 - y)]

        def copy(t, k, blk, to, src=None):
            dst = o_refs[t].at[_slot(blk)]
            return pltpu.make_async_remote_copy(
                src_ref=dst if src is None else src, dst_ref=dst, send_sem=send_sems.at[7 * t + k],
                recv_sem=recv_sems.at[7 * t + k], device_id=to, device_id_type=MESH_T)

        mine = [pltpu.make_async_copy(x_refs[t], o_refs[t].at[_slot(me)], local_sems.at[t]) for t in range(n)]
        for cp in mine:
            cp.start()
        first = []
        for t in range(n):
            first.append(copy(t, 0, me, sib, src=x_refs[t]))
            first += [copy(t, 1 + j, me, (*chip, c), src=x_refs[t]) for j, chip in enumerate(chips)]
        for cp in first:
            cp.start()
        passed = []
        for j, chip in enumerate(chips):
            for t in range(n):
                copy(t, 1 + j, (*chip, c), me).wait_recv()
                cp = copy(t, 4 + j, (*chip, c), sib)
                cp.start()
                passed.append(cp)
        for t in range(n):
            copy(t, 0, sib, me).wait_recv()
            for j, chip in enumerate(chips):
                copy(t, 4 + j, (*chip, 1 - c), me).wait_recv()
        for cp in first + passed:
            cp.wait_send()
        for cp in mine:
            cp.wait()

    return pl.pallas_call(
        body, name=name, out_shape=[_sds((NDEV,) + a.shape, a.dtype) for a in xs],
        in_specs=[_any()] * n, out_specs=[_any()] * n,
        scratch_shapes=[pltpu.SemaphoreType.DMA((7 * n,)), pltpu.SemaphoreType.DMA((7 * n,)),
                        pltpu.SemaphoreType.DMA((n,))],
        interpret=_INTERPRET)(*xs)


def _a2a_sems(n):
    return [pltpu.SemaphoreType.DMA((7 * n,)), pltpu.SemaphoreType.DMA((7 * n,)), pltpu.SemaphoreType.DMA((n,))]


def _a2a_copies(x_refs, o_refs, send_sems, recv_sems, local_sems):
    n = len(x_refs)
    x, y, c = _place()
    me = (x, y, c)
    flip = lambda v, b: (1 - v) if b else v
    peers = [(flip(x, k >> 2 & 1), flip(y, k >> 1 & 1), flip(c, k & 1)) for k in range(1, NDEV)]
    mine = [pltpu.make_async_copy(x_refs[t].at[_slot(me)], o_refs[t].at[_slot(me)], local_sems.at[t])
            for t in range(n)]

    def copy(t, k, src_slot, dst_slot, to):
        return pltpu.make_async_remote_copy(
            src_ref=x_refs[t].at[src_slot], dst_ref=o_refs[t].at[dst_slot], send_sem=send_sems.at[7 * t + k],
            recv_sem=recv_sems.at[7 * t + k], device_id=to, device_id_type=MESH_T)

    sends = [copy(t, k, _slot(p), _slot(me), p) for t in range(n) for k, p in enumerate(peers)]
    recvs = [copy(t, k, _slot(p), _slot(p), me) for t in range(n) for k, p in enumerate(peers)]
    return mine, sends, recvs


def _ag_copies(x_refs, o_refs, send_sems, recv_sems, local_sems):
    n = len(x_refs)
    x, y, c = _place()
    me = (x, y, c)
    flip = lambda v, b: (1 - v) if b else v
    peers = [(flip(x, k >> 2 & 1), flip(y, k >> 1 & 1), flip(c, k & 1)) for k in range(1, NDEV)]
    mine = [pltpu.make_async_copy(x_refs[t], o_refs[t].at[_slot(me)], local_sems.at[t]) for t in range(n)]

    def copy(t, k, dst_slot, to):
        return pltpu.make_async_remote_copy(
            src_ref=x_refs[t], dst_ref=o_refs[t].at[dst_slot], send_sem=send_sems.at[7 * t + k],
            recv_sem=recv_sems.at[7 * t + k], device_id=to, device_id_type=MESH_T)

    sends = [copy(t, k, _slot(me), p) for t in range(n) for k, p in enumerate(peers)]
    recvs = [copy(t, k, _slot(p), me) for t in range(n) for k, p in enumerate(peers)]
    return mine, sends, recvs


def _ag_start(x_refs, o_refs, send_sems, recv_sems, local_sems):
    mine, sends, _ = _ag_copies(x_refs, o_refs, send_sems, recv_sems, local_sems)
    for cp in mine + sends:
        cp.start()


def _ag_wait(x_refs, o_refs, send_sems, recv_sems, local_sems):
    mine, sends, recvs = _ag_copies(x_refs, o_refs, send_sems, recv_sems, local_sems)
    for cp in recvs:
        cp.wait_recv()
    for cp in sends:
        cp.wait_send()
    for cp in mine:
        cp.wait()


def _a2a_start(x_refs, o_refs, send_sems, recv_sems, local_sems):
    mine, sends, _ = _a2a_copies(x_refs, o_refs, send_sems, recv_sems, local_sems)
    for cp in mine + sends:
        cp.start()


def _a2a_wait(x_refs, o_refs, send_sems, recv_sems, local_sems):
    mine, sends, recvs = _a2a_copies(x_refs, o_refs, send_sems, recv_sems, local_sems)
    for cp in recvs:
        cp.wait_recv()
    for cp in sends:
        cp.wait_send()
    for cp in mine:
        cp.wait()


def adam_reduce(P, w, m, v, name, sends=()):
    n, R, C = P.shape
    br = R // 4 if R % 64 == 0 else R

    def body(p_ref, w_ref, m_ref, v_ref, g_o, d_o, m_o, v_o):
        g = p_ref[0].astype(F32)
        for k in range(1, n):
            g = g + p_ref[k].astype(F32)
        m1 = ADAM_B1 * m_ref[...] + (1.0 - ADAM_B1) * g
        v1 = ADAM_B2 * v_ref[...] + (1.0 - ADAM_B2) * jnp.square(g)
        m_hat = m1 / (1.0 - ADAM_B1 ** ADAM_STEP)
        v_hat = v1 / (1.0 - ADAM_B2 ** ADAM_STEP)
        g_o[...] = g
        d_o[...] = -ADAM_LR * (m_hat / (jnp.sqrt(v_hat) + ADAM_EPS) + ADAM_WD * w_ref[...])
        m_o[...] = m1
        v_o[...] = v1

    blk = pl.BlockSpec((br, C), lambda i: (i, 0))
    return _pc(body, name, [_sds((R, C))] * 4, grid=(R // br,),
               in_specs=[pl.BlockSpec((n, br, C), lambda i: (0, i, 0)), blk, blk, blk], out_specs=[blk] * 4,
               sends=sends)(P, w, m, v)


def adam_layers(P0, P1, w, m, v, name, sends=()):
    n, R, C = P0.shape
    br = R // 4 if R % 64 == 0 else R
    nb = R // br

    def body(p0_ref, p1_ref, w_ref, m_ref, v_ref, g_o, d_o, m_o, v_o):
        def total(p_ref):
            g = p_ref[0].astype(F32)
            for k in range(1, n):
                g = g + p_ref[k].astype(F32)
            return g

        g = jnp.where(pl.program_id(0) == 0, total(p0_ref), total(p1_ref))
        m1 = ADAM_B1 * m_ref[0] + (1.0 - ADAM_B1) * g
        v1 = ADAM_B2 * v_ref[0] + (1.0 - ADAM_B2) * jnp.square(g)
        m_hat = m1 / (1.0 - ADAM_B1 ** ADAM_STEP)
        v_hat = v1 / (1.0 - ADAM_B2 ** ADAM_STEP)
        g_o[0] = g
        d_o[0] = -ADAM_LR * (m_hat / (jnp.sqrt(v_hat) + ADAM_EPS) + ADAM_WD * w_ref[0])
        m_o[0] = m1
        v_o[0] = v1

    blk = pl.BlockSpec((1, br, C), lambda l, i: (l, i, 0))
    p0 = pl.BlockSpec((n, br, C), lambda l, i: (0, jnp.where(l == 0, i, nb - 1), 0))
    p1 = pl.BlockSpec((n, br, C), lambda l, i: (0, jnp.where(l == 1, i, 0), 0))
    return _pc(body, name, [_sds((2, R, C))] * 4, grid=(2, nb), in_specs=[p0, p1, blk, blk, blk],
               out_specs=[blk] * 4, sends=sends)(P0, P1, w, m, v)


def mod_fwd(scin, wmod, bcol):
    def body(s_ref, w_ref, b_ref, o_ref):
        o_ref[0] = mm(_silu(s_ref[...]), w_ref[0]) + b_ref[0]

    return _pc(body, "mod_fwd", _sds((2, 16, 768)), grid=(2,),
               in_specs=[pl.BlockSpec((16, D), lambda l: (0, 0)), pl.BlockSpec((1, D, 768), lambda l: (l, 0, 0)),
                         pl.BlockSpec((1, 1, 768), lambda l: (l, 0, 0))],
               out_specs=pl.BlockSpec((1, 16, 768), lambda l: (l, 0, 0)))(scin, wmod, bcol)


def mod_bwd(scin, wmod, G):
    def body(s_ref, w_ref, g_ref, dw_o, ds_o):
        _, vjp = jax.vjp(lambda s, w: mm(_silu(s), w), s_ref[...], w_ref[0])
        ds, dw = vjp(g_ref[0])
        dw_o[0] = dw
        _acc_init(pl.program_id(0) == 0, [ds_o])
        ds_o[...] += ds

    full = pl.BlockSpec((16, D), lambda l: (0, 0))
    wsp = pl.BlockSpec((1, D, 768), lambda l: (l, 0, 0))
    return _pc(body, "mod_bwd", [_sds((2, D, 768)), _sds((16, D))], grid=(2,),
               in_specs=[full, wsp, pl.BlockSpec((1, 16, 768), lambda l: (l, 0, 0))], out_specs=[wsp, full])(
        scin, wmod, G)


_SMALL = ["b_mod", "g_mix", "wa_sink", "na_rpb", "ssm_conv_w", "ssm_conv_b", "ssm_dt_bias", "ssm_a_log", "ssm_d",
          "ssm_norm_g", "g_ffn", "g_final", "dmod_s", "dmod_c", "loss"]


def _pack(parts):
    rows = []
    for a in parts:
        f = a.reshape(-1).astype(F32)
        rows.append(jnp.pad(f, (0, (-f.shape[0]) % 1024)).reshape(-1, 128))
    return jnp.concatenate(rows, axis=0)


def _unpack(packed, shapes):
    out, r = [], 0
    for s in shapes:
        nel = int(np.prod(s))
        nr = -(-nel // 1024) * 8
        out.append(packed[r:r + nr].reshape(-1)[:nel].reshape(s))
        r += nr
    return out


def kernel(x, c, ctx, c_ctx, w_mod, b_mod, g_mix, w_in, wa_sink, na_rpb, ssm_conv_w, ssm_conv_b, ssm_dt_bias, ssm_a_log, ssm_d, ssm_norm_g, w_out, g_ffn, w_ffn_in, w_ffn_out, g_final, loss_target, m_c_ctx, m_w_mod, m_b_mod, m_g_mix, m_w_in, m_wa_sink, m_na_rpb, m_ssm_conv_w, m_ssm_conv_b, m_ssm_dt_bias, m_ssm_a_log, m_ssm_d, m_ssm_norm_g, m_w_out, m_g_ffn, m_w_ffn_in, m_w_ffn_out, m_g_final, v_c_ctx, v_w_mod, v_b_mod, v_g_mix, v_w_in, v_wa_sink, v_na_rpb, v_ssm_conv_w, v_ssm_conv_b, v_ssm_dt_bias, v_ssm_a_log, v_ssm_d, v_ssm_norm_g, v_w_out, v_g_ffn, v_w_ffn_in, v_w_ffn_out, v_g_final):
    L = x.shape[1]
    px, py, pc = _place()
    me = 4 * px + 2 * py + pc
    W = dict(c_ctx=c_ctx, w_mod=w_mod, b_mod=b_mod, g_mix=g_mix, w_in=w_in, wa_sink=wa_sink, na_rpb=na_rpb,
             ssm_conv_w=ssm_conv_w, ssm_conv_b=ssm_conv_b, ssm_dt_bias=ssm_dt_bias, ssm_a_log=ssm_a_log, ssm_d=ssm_d,
             ssm_norm_g=ssm_norm_g, w_out=w_out, g_ffn=g_ffn, w_ffn_in=w_ffn_in, w_ffn_out=w_ffn_out, g_final=g_final)
    M = dict(c_ctx=m_c_ctx, w_mod=m_w_mod, b_mod=m_b_mod, g_mix=m_g_mix, w_in=m_w_in, wa_sink=m_wa_sink,
             na_rpb=m_na_rpb, ssm_conv_w=m_ssm_conv_w, ssm_conv_b=m_ssm_conv_b, ssm_dt_bias=m_ssm_dt_bias,
             ssm_a_log=m_ssm_a_log, ssm_d=m_ssm_d, ssm_norm_g=m_ssm_norm_g, w_out=m_w_out, g_ffn=m_g_ffn,
             w_ffn_in=m_w_ffn_in, w_ffn_out=m_w_ffn_out, g_final=m_g_final)
    V = dict(c_ctx=v_c_ctx, w_mod=v_w_mod, b_mod=v_b_mod, g_mix=v_g_mix, w_in=v_w_in, wa_sink=v_wa_sink,
             na_rpb=v_na_rpb, ssm_conv_w=v_ssm_conv_w, ssm_conv_b=v_ssm_conv_b, ssm_dt_bias=v_ssm_dt_bias,
             ssm_a_log=v_ssm_a_log, ssm_d=v_ssm_d, ssm_norm_g=v_ssm_norm_g, w_out=v_w_out, g_ffn=v_g_ffn,
             w_ffn_in=v_w_ffn_in, w_ffn_out=v_w_ffn_out, g_final=v_g_final)

    tr = lambda a: a.transpose(0, 2, 1)
    shards = dict(w_in=tr(w_in).astype(MXU), w_out=w_out.astype(MXU), w_ffn_in=tr(w_ffn_in).astype(MXU),
                  w_ffn_out=w_ffn_out.astype(MXU))
    c_all, conv_all, shards["w_in_first"] = all_gather([c, ssm_conv_w, shards["w_in"][0]], "gather_first")
    conv_f = conv_all.transpose(1, 2, 0, 3).reshape(2, 7, 1024)

    scin = jnp.concatenate([c_all.reshape(NDEV, D), c_ctx.reshape(1, D), jnp.zeros((7, D), F32)], axis=0)
    bcol = lax.dynamic_slice_in_dim(b_mod, me * 768, 768, axis=1).reshape(2, 1, 768)
    mod_all, = all_gather([mod_fwd(scin, w_mod, bcol)], "gather_mod")
    mod_rows = mod_all.transpose(1, 2, 0, 3).reshape(2, 16, 6 * D)
    mods = jnp.stack([lax.dynamic_index_in_dim(mod_rows, me, axis=1, keepdims=False), mod_rows[:, 8]], axis=1)

    layers = [dict(g_mix=g_mix[i], wa_sink=wa_sink[i], na_rpb=na_rpb[i], ssm_conv_w=conv_f[i],
                   ssm_conv_b=ssm_conv_b[i], ssm_dt_bias=ssm_dt_bias[i], ssm_a_log=ssm_a_log[i], ssm_d=ssm_d[i],
                   ssm_norm_g=ssm_norm_g[i], g_ffn=g_ffn[i]) for i in range(2)]
    loss, dx, grads, dmods, dgfin, gin0 = local_step(x[0], ctx[0], loss_target[0], mods, layers, shards, g_final, L)

    stk = lambda n: jnp.stack([grads[0][n], grads[1][n]])
    small = dict(b_mod=dmods[:, 0] + dmods[:, 1], g_final=dgfin, dmod_s=dmods[:, 0], dmod_c=dmods[:, 1],
                 loss=loss.reshape(1))
    for nme in _SMALL:
        if nme not in small:
            small[nme] = stk(nme)
    shapes = [small[nme].shape for nme in _SMALL]
    zero_like = lambda nme: jnp.zeros(small[nme].shape, F32)
    own = lambda S, nme: S[nme] if (nme in S and S[nme].shape == small[nme].shape) else zero_like(nme)
    gath, = all_gather([_pack([small[nme] for nme in _SMALL])], "gather_grads")
    sm = adam_reduce(gath, _pack([own(W, nme) for nme in _SMALL]), _pack([own(M, nme) for nme in _SMALL]),
                     _pack([own(V, nme) for nme in _SMALL]), "adam_small")
    res = {nme: vals for nme, vals in zip(_SMALL, zip(*[_unpack(a, shapes) for a in sm]))}
    loss = res["loss"][0][0]

    cols = lambda a: lax.dynamic_slice_in_dim(a, me * 768, 768, axis=-1)
    rows_of = lambda s: -(-int(np.prod(s)) // 1024) * 8
    r0 = sum(rows_of(s) for s in shapes[:_SMALL.index("dmod_s")])
    dmod_s_all = gath[:, r0:r0 + rows_of(small["dmod_s"].shape)].reshape(NDEV, 2, 6 * D).transpose(1, 0, 2)
    G = jnp.concatenate([cols(dmod_s_all), cols(res["dmod_c"][0])[:, None, :], jnp.zeros((2, 7, 768), F32)], axis=1)
    dwmod, dscin = mod_bwd(scin, w_mod, G)
    cc_g, = all_gather([dscin[8].reshape(8, 128)], "gather_cctx")
    out = {}
    out["c_ctx"] = [a.reshape(D) for a in adam_reduce(cc_g, c_ctx.reshape(8, 128), m_c_ctx.reshape(8, 128),
                                                      v_c_ctx.reshape(8, 128), "adam_cctx")]
    res_wmod, (got1,) = adam_reduce(dwmod.reshape(1, 2 * D, 768), w_mod.reshape(2 * D, 768),
                                    m_w_mod.reshape(2 * D, 768), v_w_mod.reshape(2 * D, 768), "adam_wmod",
                                    sends=(gin0[1],))
    out["w_mod"] = [a.reshape(2, D, 768) for a in res_wmod]
    gconv = lax.dynamic_slice_in_dim(res["ssm_conv_w"][0], me * 128, 128, axis=2)
    out["ssm_conv_w"] = [a.reshape(2, 7, 128) for a in adam_reduce(
        gconv.reshape(1, 14, 128), ssm_conv_w.reshape(14, 128), m_ssm_conv_w.reshape(14, 128),
        v_ssm_conv_w.reshape(14, 128), "adam_conv")]
    for nme in _SMALL:
        if nme not in ("ssm_conv_w", "dmod_s", "dmod_c", "loss"):
            out[nme] = list(res[nme])

    adam_big = lambda nme, t, **kw: adam_layers(grads[0][nme], grads[1][nme], t(W[nme]), t(M[nme]), t(V[nme]),
                                                "adam_" + nme, **kw)
    same = lambda a: a
    res_fi, (got0,) = adam_big("w_ffn_in", tr, sends=(gin0[0],))
    grads[0]["w_in"] = jnp.concatenate([got0, got1], axis=2)
    out["w_ffn_in"] = [tr(a) for a in res_fi]
    out["w_ffn_out"] = list(adam_big("w_ffn_out", same))
    out["w_out"] = list(adam_big("w_out", same))
    out["w_in"] = [tr(a) for a in adam_big("w_in", tr)]
    order = ["c_ctx", "w_mod", "b_mod", "g_mix", "w_in", "wa_sink", "na_rpb", "ssm_conv_w", "ssm_conv_b",
             "ssm_dt_bias", "ssm_a_log", "ssm_d", "ssm_norm_g", "w_out", "g_ffn", "w_ffn_in", "w_ffn_out", "g_final"]
    return (loss, dx.reshape(1, L, D), *[out[nme][0] for nme in order], *[out[nme][1] for nme in order],
            *[out[nme][2] for nme in order], *[out[nme][3] for nme in order])
```

```python
import functools

import numpy as np
import jax
import jax.numpy as jnp
from jax import lax
from jax.experimental import pallas as pl
from jax.experimental.pallas import tpu as pltpu

F32 = jnp.float32
MXU = jnp.bfloat16
_INTERPRET = False
VMEM_LIMIT = 60 * 1024 * 1024

D = 1024
LC = 256
GW = 64
HD = 64
EPS = 1e-6
NEG = -1e30
NDEV = 8
Q = 128
NSTATE = 128
DFF = 2816
IN_COLS = 2832
NP_IN = 3072
C_QA, C_QB, C_Z, C_KA, C_VA, C_KB, C_VB, C_XBC, C_DT = 0, 256, 512, 1024, 1152, 1280, 1536, 1792, 2816
ADAM_LR, ADAM_B1, ADAM_B2, ADAM_EPS, ADAM_WD, ADAM_STEP = 0.001, 0.9, 0.999, 1e-08, 0.01, 10
MESH_T = pl.DeviceIdType.MESH


def _dg(a, b, ca, cb):
    return lax.dot_general(a.astype(MXU), b.astype(MXU), (((ca,), (cb,)), ((), ())), preferred_element_type=F32)


@jax.custom_vjp
def mm(a, b):
    return _dg(a, b, 1, 0)


def _mm_f(a, b):
    return _dg(a, b, 1, 0), (a, b)


def _mm_b(res, g):
    a, b = res
    return _dg(g, b, 1, 1).astype(a.dtype), _dg(a, g, 0, 0).astype(b.dtype)


mm.defvjp(_mm_f, _mm_b)


@jax.custom_vjp
def mm_nt(a, b):
    return _dg(a, b, 1, 1)


def _mmnt_f(a, b):
    return _dg(a, b, 1, 1), (a, b)


def _mmnt_b(res, g):
    a, b = res
    return _dg(g, b, 1, 0).astype(a.dtype), _dg(g, a, 0, 0).astype(b.dtype)


mm_nt.defvjp(_mmnt_f, _mmnt_b)


@jax.custom_vjp
def mm_tn(a, b):
    return _dg(a, b, 0, 0)


def _mmtn_f(a, b):
    return _dg(a, b, 0, 0), (a, b)


def _mmtn_b(res, g):
    a, b = res
    return _dg(b, g, 1, 1).astype(a.dtype), _dg(a, g, 1, 0).astype(b.dtype)


mm_tn.defvjp(_mmtn_f, _mmtn_b)


@jax.custom_vjp
def mmw(a, w):
    return _dg(a, w, 1, 0)


mmw.defvjp(lambda a, w: (_dg(a, w, 1, 0), w), lambda w, g: (_dg(g, w, 1, 1), None))


@jax.custom_vjp
def mmw_nt(a, w):
    return _dg(a, w, 1, 1)


mmw_nt.defvjp(lambda a, w: (_dg(a, w, 1, 1), w), lambda w, g: (_dg(g, w, 1, 0), None))


def _exact(a, b):
    return lax.dot_general(a, b, (((1,), (0,)), ((), ())), precision=lax.Precision.HIGHEST,
                           preferred_element_type=F32)


def _pc(body, name, out_shape, grid=None, in_specs=None, out_specs=None, scratch=(), sends=(), gather=False):
    params = pltpu.CompilerParams(vmem_limit_bytes=VMEM_LIMIT)
    if sends and not isinstance(out_shape, (list, tuple)):
        out_shape, out_specs = [out_shape], [out_specs]
    start, wait = (_ag_start, _ag_wait) if gather else (_a2a_start, _a2a_wait)
    if not sends:
        kw = {}
        if grid is not None:
            kw = dict(grid=grid, in_specs=in_specs, out_specs=out_specs)
        elif in_specs is not None:
            kw = dict(in_specs=in_specs, out_specs=out_specs)
        return pl.pallas_call(body, name=name, out_shape=out_shape, scratch_shapes=list(scratch),
                              compiler_params=params, interpret=_INTERPRET, **kw)
    n, nin, nout, nscr = len(sends), len(in_specs), len(out_shape), len(scratch)

    def body2(*refs):
        cin, xs = refs[:nin], refs[nin:nin + n]
        couts, os_ = refs[nin + n:nin + n + nout], refs[nin + n + nout:nin + 2 * n + nout]
        cscr, sems = refs[nin + 2 * n + nout:nin + 2 * n + nout + nscr], refs[nin + 2 * n + nout + nscr:]
        ids = [pl.program_id(a) for a in range(len(grid))]
        first = functools.reduce(lambda a, b: a & b, [i == 0 for i in ids])
        last = functools.reduce(lambda a, b: a & b, [i == g - 1 for i, g in zip(ids, grid)])

        @pl.when(first)
        def _():
            start(xs, os_, *sems)

        body(*cin, *couts, *cscr)

        @pl.when(last)
        def _():
            wait(xs, os_, *sems)

    call = pl.pallas_call(
        body2, name=name,
        out_shape=list(out_shape) + [_sds(((NDEV,) if gather else ()) + a.shape, a.dtype) for a in sends],
        grid=grid, in_specs=list(in_specs) + [_any()] * n, out_specs=list(out_specs) + [_any()] * n,
        scratch_shapes=list(scratch) + _a2a_sems(n), compiler_params=params, interpret=_INTERPRET)

    def run(*args):
        res = call(*args, *sends)
        return res[:nout], res[nout:]

    return run


def _vm():
    return pl.BlockSpec(memory_space=pltpu.VMEM)


def _sds(shape, dt=F32):
    return jax.ShapeDtypeStruct(shape, dt)


def _iota(shape, dim):
    return lax.broadcasted_iota(jnp.int32, shape, dim)


def _silu(x):
    return x * jax.nn.sigmoid(x)


def _softplus(x):
    return jnp.maximum(x, 0.0) + jnp.log1p(jnp.exp(-jnp.abs(x)))


def _normmod(x, g, sh, sc):
    r = lax.rsqrt(jnp.mean(x * x, axis=-1, keepdims=True) + EPS)
    return (x * r * g) * (1.0 + sc) + sh


def _rope(x, cos, sin, rm):
    return x * cos + _exact(x, rm) * sin


def _swap12(x):
    lane = _iota(x.shape, 1)
    up, down = pltpu.roll(x, 192, 1), pltpu.roll(x, 64, 1)
    return jnp.where((lane >= 64) & (lane < 128), up, jnp.where((lane >= 128) & (lane < 192), down, x))


def _skip_ctx_tile(skip, nlt, compute, outs):
    if not skip:
        compute()
        return
    i = pl.program_id(0)
    pl.when(i < nlt)(compute)

    @pl.when(i >= nlt)
    def _():
        for r in outs:
            r[...] = jnp.zeros_like(r)


def _acc_init(first, refs):
    @pl.when(first)
    def _():
        for r in refs:
            r[...] = jnp.zeros_like(r)


def _stream(X, TR, nlt):
    if not isinstance(X, tuple):
        return (X,), [pl.BlockSpec((TR, D), lambda i: (i, 0))], lambda refs: refs[0][...]
    specs = [pl.BlockSpec((TR, D), lambda i: (jnp.minimum(i, nlt - 1), 0)), pl.BlockSpec((TR, D), lambda i: (0, 0))]
    return X, specs, lambda refs: jnp.where(pl.program_id(0) < nlt, refs[0][...], refs[1][...])


def in_fwd(X, g, sh, sc, W, cos, sin, rm, L, sends=()):
    T = L + LC
    TR = 256
    nlt = L // TR
    xs, xspecs, xread = _stream(X, TR, nlt)

    def body(*refs):
        (g_ref, sh_ref, sc_ref, w_ref, cos_ref, sin_ref, rm_ref,
         qa, qb, z, ka, va, kb, vb, xbc, dt, hout) = refs[len(xs):]
        h = _normmod(xread(refs), g_ref[...], sh_ref[0], sc_ref[0]).astype(MXU)
        hout[...] = h
        y = lax.dot_general(h, w_ref[...], (((1,), (1,)), ((), ())), preferred_element_type=F32)
        cs, sn, r = cos_ref[...], sin_ref[...], rm_ref[...]
        qa[...] = _rope(_swap12(y[:, C_QA:C_QB]), cs, sn, r).astype(MXU)
        qb[...] = y[:, C_QB:C_Z].astype(MXU)
        z[...] = y[:, C_Z:C_KA]
        ka[...] = _rope(y[:, C_KA:C_VA], cs[:, :128], sn[:, :128], r[:128, :128]).astype(MXU)
        va[...] = y[:, C_VA:C_KB].astype(MXU)
        kb[...] = y[:, C_KB:C_VB].astype(MXU)
        vb[...] = y[:, C_VB:C_XBC].astype(MXU)
        xbc[...] = y[:, C_XBC:C_DT]
        dt[...] = y[:, C_DT:C_DT + 128]

    row = lambda w: pl.BlockSpec((TR, w), lambda i: (i, 0))
    cls = pl.BlockSpec((1, 1, D), lambda i: (i // nlt, 0, 0))
    widths = [(256, MXU), (256, MXU), (512, F32), (128, MXU), (128, MXU), (256, MXU), (256, MXU), (1024, F32),
              (128, F32), (D, MXU)]
    return _pc(body, "in_fwd", [_sds((T, w), d) for w, d in widths], grid=(T // TR,),
               in_specs=xspecs + [pl.BlockSpec((1, D), lambda i: (0, 0)), cls, cls, _vm(), row(256), row(256), _vm()],
               out_specs=[row(w) for w, _ in widths], sends=sends, gather=True)(*xs, g, sh, sc, W, cos, sin, rm)


def in_bwd(X, g, sh, sc, W, cos, sin, rm, dxres, dqa, dqb, dz, dka, dva, dkb, dvb, dxbc, ddt2, L, latent_only):
    T = L + LC
    TR = 256
    nlt = L // TR
    xs, xspecs, xread = _stream(X, TR, nlt)

    def body(*refs):
        (g_ref, sh_ref, sc_ref, w_ref, cos_ref, sin_ref, rm_ref, dxres_ref, dqa_r, dqb_r, dz_r, dka_r,
         dva_r, dkb_r, dvb_r, dxbc_r, ddt0_r, ddt1_r, dx_o, dy_o, dg_o, dsh_o, dsc_o) = refs[len(xs):]
        i = pl.program_id(0)
        cs, sn, r = cos_ref[...], sin_ref[...], rm_ref[...]
        _, vq = jax.vjp(lambda t: _rope(t, cs, sn, r), dqa_r[...])
        _, vk = jax.vjp(lambda t: _rope(t, cs[:, :128], sn[:, :128], r[:128, :128]), dka_r[...])
        dyqa = _swap12(vq(dqa_r[...])[0])
        dyka, = vk(dka_r[...])
        ddt = ddt0_r[0] + ddt1_r[0]
        dy = jnp.concatenate([dyqa, dqb_r[...], dz_r[...], dyka, dva_r[...], dkb_r[...], dvb_r[...], dxbc_r[...],
                              ddt, jnp.zeros((TR, NP_IN - C_DT - 128), F32)], axis=1).astype(MXU)
        dy_o[...] = dy
        dh = jnp.dot(dy, w_ref[...], preferred_element_type=F32)
        _, vp = jax.vjp(_normmod, xread(refs), g_ref[...], sh_ref[0], sc_ref[0])
        dx, dg, dsh, dsc = vp(dh)
        if latent_only:
            @pl.when(i < nlt)
            def _():
                dx_o[...] = dx + dxres_ref[...]
        else:
            dx_o[...] = dx + dxres_ref[...]
        _acc_init(i == 0, [dg_o])
        _acc_init((i == 0) | (i == nlt), [dsh_o, dsc_o])
        dg_o[...] += dg
        dsh_o[0] += dsh
        dsc_o[0] += dsc

    row = lambda w: pl.BlockSpec((TR, w), lambda i: (i, 0))
    cls = pl.BlockSpec((1, 1, D), lambda i: (i // nlt, 0, 0))
    vec = pl.BlockSpec((1, D), lambda i: (0, 0))
    dts = lambda d: pl.BlockSpec((1, TR, 128), lambda i: (d, i, 0))
    dxs = pl.BlockSpec((TR, D), lambda i: (jnp.minimum(i, nlt - 1), 0)) if latent_only else row(D)
    return _pc(body, "in_bwd",
               [_sds((L if latent_only else T, D)), _sds((T, NP_IN), MXU), _sds((1, D)), _sds((2, 1, D)),
                _sds((2, 1, D))],
               grid=(T // TR,),
               in_specs=xspecs + [vec, cls, cls, _vm(), row(256), row(256), _vm(), row(D), row(256), row(256),
                                  row(512), row(128), row(128), row(256), row(256), row(1024), dts(0), dts(1)],
               out_specs=[dxs, row(NP_IN), vec, cls, cls])(
        *xs, g, sh, sc, W, cos, sin, rm, dxres, dqa, dqb, dz, dka, dva, dkb, dvb, dxbc, ddt2, ddt2)


def tn_mm(A, G, bk, bn, out_dtype, ncol=None, col0=0):
    T, K = A.shape
    N = G.shape[1] if ncol is None else ncol
    first = col0 * (N // bn)
    bt = T
    nt = T // bt

    def body(a_ref, g_ref, o_ref, acc):
        t = pl.program_id(2)
        _acc_init(t == 0, [acc])
        acc[...] += lax.dot_general(a_ref[...], g_ref[...], (((0,), (0,)), ((), ())), preferred_element_type=F32)

        @pl.when(t == nt - 1)
        def _():
            o_ref[...] = acc[...].astype(out_dtype)

    return _pc(body, "tn_mm", _sds((K, N), out_dtype), grid=(K // bk, N // bn, nt),
               in_specs=[pl.BlockSpec((bt, bk), lambda k, n, t: (t, k)),
                         pl.BlockSpec((bt, bn), lambda k, n, t: (t, first + n))],
               out_specs=pl.BlockSpec((bk, bn), lambda k, n, t: (k, n)),
               scratch=[pltpu.VMEM((bk, bn), F32)])(A, G)


def _ssm_out(yf, yb, xs, z, dsk, gs):
    y = (yf + yb + dsk * xs) * _silu(z)
    r = lax.rsqrt(jnp.mean(y * y, axis=-1, keepdims=True) + EPS)
    return y * r * gs


def out_fwd(oa, ob, y2, act, z, dsk, gs, W, X, gate, L, sends=()):
    T = L + LC
    TR = 256
    nlt = L // TR
    xs, xspecs, xread = _stream(X, TR, nlt)

    def body(*refs):
        oa_r, ob_r, yf_r, yb_r, xs_r, z_r, dsk_r, gs_r, w_ref, gt_ref, x1_o, cat_o = refs[len(xs):]
        oc = _ssm_out(yf_r[0], yb_r[0], xs_r[...], z_r[...], dsk_r[...], gs_r[...])
        cat = jnp.concatenate([_swap12(oa_r[...]), ob_r[...], oc], axis=1).astype(MXU)
        cat_o[...] = cat
        x1_o[...] = xread(refs) + gt_ref[0] * jnp.dot(cat, w_ref[...], preferred_element_type=F32)

    row = lambda w: pl.BlockSpec((TR, w), lambda i: (i, 0))
    ys = lambda d: pl.BlockSpec((1, TR, 512), lambda i: (d, i, 0))
    cls = pl.BlockSpec((1, 1, D), lambda i: (i // nlt, 0, 0))
    v512 = pl.BlockSpec((1, 512), lambda i: (0, 0))
    return _pc(body, "out_fwd", [_sds((T, D)), _sds((T, D), MXU)], grid=(T // TR,),
               in_specs=xspecs + [row(256), row(256), ys(0), ys(1), row(512), row(512), v512, v512, _vm(), cls],
               out_specs=[row(D), row(D)], sends=sends, gather=True)(*xs, oa, ob, y2, y2, act, z, dsk, gs, W, gate)


def out_bwd(oa, ob, y2, act, z, dsk, gs, W, gate, dX1, L):
    T = dX1.shape[0]
    TR = 256
    nlt = L // TR

    def body(oa_r, ob_r, yf_r, yb_r, xs_r, z_r, dsk_r, gs_r, w_ref, gt_ref, dx1_r,
             doa_o, dob_o, dy_o, dxs_o, dz_o, dmix_o, ddsk_o, dgs_o, dgt_o):
        i = pl.program_id(0)
        w = w_ref[...]

        def f(oa_, ob_, yf, yb, xs, z_, dsk_, gs_, gt):
            oc = _ssm_out(yf, yb, xs, z_, dsk_, gs_)
            return gt * mmw(jnp.concatenate([oa_, ob_, oc], axis=1), w)

        _, vjp = jax.vjp(f, _swap12(oa_r[...]), ob_r[...], yf_r[0], yb_r[0], xs_r[...], z_r[...], dsk_r[...],
                         gs_r[...], gt_ref[0])
        dx1 = dx1_r[...]
        doa, dob, dyf, _, dxs, dz, ddsk, dgs, dgt = vjp(dx1)
        doa_o[...] = _swap12(doa)
        dob_o[...] = dob
        dy_o[...] = dyf
        dxs_o[...] = dxs
        dz_o[...] = dz
        dmix_o[...] = (gt_ref[0] * dx1).astype(MXU)
        _acc_init(i == 0, [ddsk_o, dgs_o])
        _acc_init((i == 0) | (i == nlt), [dgt_o])
        ddsk_o[...] += ddsk
        dgs_o[...] += dgs
        dgt_o[0] += dgt

    row = lambda w: pl.BlockSpec((TR, w), lambda i: (i, 0))
    ys = lambda d: pl.BlockSpec((1, TR, 512), lambda i: (d, i, 0))
    cls = pl.BlockSpec((1, 1, D), lambda i: (i // nlt, 0, 0))
    v512 = pl.BlockSpec((1, 512), lambda i: (0, 0))
    return _pc(body, "out_bwd",
               [_sds((T, 256)), _sds((T, 256)), _sds((T, 512)), _sds((T, 512)), _sds((T, 512)), _sds((T, D), MXU),
                _sds((1, 512)), _sds((1, 512)), _sds((2, 1, D))],
               grid=(T // TR,),
               in_specs=[row(256), row(256), ys(0), ys(1), row(512), row(512), v512, v512, _vm(), cls, row(D)],
               out_specs=[row(256), row(256), row(512), row(512), row(512), row(D), v512, v512, cls])(
        oa, ob, y2, y2, act, z, dsk, gs, W, gate, dX1)


def ffn_fwd(X, g, sh, sc, gate, Win, Wout, L, sends=(), skip_ctx=False):
    T = X.shape[0]
    TR = 256
    nlt = L // TR

    def body(x_ref, g_ref, sh_ref, sc_ref, gt_ref, wi_ref, wo_ref, o_ref, f_ref):
        def compute():
            h = _normmod(x_ref[...], g_ref[...], sh_ref[0], sc_ref[0]).astype(MXU)
            nt = (((1,), (1,)), ((), ()))
            a = lax.dot_general(h, wi_ref[0:DFF, :], nt, preferred_element_type=F32)
            u = lax.dot_general(h, wi_ref[DFF:2 * DFF, :], nt, preferred_element_type=F32)
            act = (_silu(a) * u).astype(MXU)
            ff = jnp.dot(act, wo_ref[...], preferred_element_type=F32)
            f_ref[...] = ff
            o_ref[...] = x_ref[...] + gt_ref[0] * ff

        _skip_ctx_tile(skip_ctx, nlt, compute, [o_ref, f_ref])

    row = lambda w: pl.BlockSpec((TR, w), lambda i: (i, 0))
    cls = pl.BlockSpec((1, 1, D), lambda i: (i // nlt, 0, 0))
    vec = pl.BlockSpec((1, D), lambda i: (0, 0))
    return _pc(body, "ffn_fwd", [_sds((T, D)), _sds((T, D))], grid=(T // TR,),
               in_specs=[row(D), vec, cls, cls, cls, _vm(), _vm()], out_specs=[row(D), row(D)], sends=sends,
               gather=True)(X, g, sh, sc, gate, Win, Wout)


def ffn_bwd(X, g, sh, sc, gate, Win, Wout, FF, dX2, L, sends=(), nchunk=2, skip_ctx=False):
    T = X.shape[0]
    TR = 256
    nlt = L // TR
    CH = DFF // nchunk

    def body(x_ref, g_ref, sh_ref, sc_ref, gt_ref, wi_ref, wo_ref, ff_r, dx2_r,
             dx_o, h_o, du_o, act_o, dout_o, dg_o, dsh_o, dsc_o, dgt_o):
        i = pl.program_id(0)
        _acc_init(i == 0, [dg_o])
        _acc_init((i == 0) | (i == nlt), [dsh_o, dsc_o, dgt_o])

        def compute():
            h, vp = jax.vjp(_normmod, x_ref[...], g_ref[...], sh_ref[0], sc_ref[0])
            dx2 = dx2_r[...]
            dout = gt_ref[0] * dx2
            zero = jnp.zeros((TR, CH), F32)
            dh = jnp.zeros((TR, D), F32)
            for c in range(nchunk):
                lo, hi = c * CH, (c + 1) * CH
                wg, wu, wo = wi_ref[lo:hi, :], wi_ref[DFF + lo:DFF + hi, :], wo_ref[lo:hi, :]

                def f(h_, eg, eu):
                    act = _silu(mmw_nt(h_, wg) + eg) * (mmw_nt(h_, wu) + eu)
                    return mmw(act, wo), act

                _, vjp_c, act = jax.vjp(f, h, zero, zero, has_aux=True)
                dh_c, da, du = vjp_c(dout)
                dh = dh + dh_c
                du_o[:, lo:hi] = da.astype(MXU)
                du_o[:, DFF + lo:DFF + hi] = du.astype(MXU)
                act_o[:, lo:hi] = act.astype(MXU)
            dx, dg, dsh, dsc = vp(dh)
            dx_o[...] = dx + dx2
            h_o[...] = h.astype(MXU)
            dout_o[...] = dout.astype(MXU)
            dg_o[...] += dg
            dsh_o[0] += dsh
            dsc_o[0] += dsc
            dgt_o[0] += jnp.sum(dx2 * ff_r[...], axis=0, keepdims=True)

        _skip_ctx_tile(skip_ctx, nlt, compute, [dx_o, h_o, du_o, act_o, dout_o])

    row = lambda w: pl.BlockSpec((TR, w), lambda i: (i, 0))
    cls = pl.BlockSpec((1, 1, D), lambda i: (i // nlt, 0, 0))
    vec = pl.BlockSpec((1, D), lambda i: (0, 0))
    return _pc(body, "ffn_bwd",
               [_sds((T, D)), _sds((T, D), MXU), _sds((T, 2 * DFF), MXU), _sds((T, DFF), MXU), _sds((T, D), MXU),
                _sds((1, D)), _sds((2, 1, D)), _sds((2, 1, D)), _sds((2, 1, D))],
               grid=(T // TR,),
               in_specs=[row(D), vec, cls, cls, cls, _vm(), _vm(), row(D), row(D)],
               out_specs=[row(D), row(D), row(2 * DFF), row(DFF), row(D), vec, cls, cls, cls], sends=sends)(
        X, g, sh, sc, gate, Win, Wout, FF, dX2)


def loss_head(X2, g, tgt, L):
    T = X2.shape[0]
    TR = 256
    nlt = L // TR

    def body(x_ref, g_ref, t_ref, loss_o, dx_o, dg_o):
        i = pl.program_id(0)
        _acc_init(i == 0, [loss_o, dg_o])

        @pl.when(i < nlt)
        def _():
            def f(x, g_):
                y = x * lax.rsqrt(jnp.mean(x * x, axis=-1, keepdims=True) + EPS) * g_
                return 0.5 * jnp.sum(jnp.mean(jnp.square(y - t_ref[...]), axis=-1, keepdims=True), axis=0,
                                     keepdims=True)

            val, vjp = jax.vjp(f, x_ref[...], g_ref[...])
            dx, dg = vjp(jnp.ones((1, 1), F32))
            dx_o[...] = dx
            loss_o[...] += jnp.broadcast_to(val, (8, 128))
            dg_o[...] += dg

        @pl.when(i >= nlt)
        def _():
            dx_o[...] = jnp.zeros_like(dx_o)

    row = pl.BlockSpec((TR, D), lambda i: (i, 0))
    vec = pl.BlockSpec((1, D), lambda i: (0, 0))
    return _pc(body, "loss_head", [_sds((8, 128)), _sds((T, D)), _sds((1, D))], grid=(T // TR,),
               in_specs=[row, vec, pl.BlockSpec((TR, D), lambda i: (jnp.minimum(i, nlt - 1), 0))],
               out_specs=[pl.BlockSpec((8, 128), lambda i: (0, 0)), row, vec])(X2, g, tgt)


def _stack_impl(q):
    lane = _iota(q.shape, 1)
    return jnp.concatenate([jnp.where(lane < HD, q, 0.0), jnp.where(lane >= HD, q, 0.0)], axis=0)


def _unstack_impl(o):
    M = o.shape[0] // 2
    return jnp.where(_iota((M, o.shape[1]), 1) < HD, o[:M], o[M:])


@jax.custom_vjp
def _stack(q):
    return _stack_impl(q)


_stack.defvjp(lambda q: (_stack_impl(q), None), lambda _, g: (_unstack_impl(g),))


@jax.custom_vjp
def _unstack(o):
    return _unstack_impl(o)


_unstack.defvjp(lambda o: (_unstack_impl(o), None), lambda _, g: (_stack_impl(g),))


def _softmax_av(q, ks, vs, biases, sink):
    q2 = _stack(q)
    ss = []
    for k, b in zip(ks, biases):
        s = mm_nt(q2, k) * (HD ** -0.5)
        ss.append(s if b is None else s + b)
    m = functools.reduce(jnp.maximum, [jnp.max(s, axis=1, keepdims=True) for s in ss])
    if sink is not None:
        m = jnp.maximum(m, sink)
    m = lax.stop_gradient(m)
    es = [jnp.exp(s - m) for s in ss]
    den = functools.reduce(lambda a, b_: a + b_, [jnp.sum(e, axis=1, keepdims=True) for e in es])
    if sink is not None:
        den = den + jnp.exp(sink - m)
    inv = 1.0 / den
    return _unstack(functools.reduce(lambda a, b_: a + b_, [mm(e * inv, v) for e, v in zip(es, vs)]))


def _sink_col(s0, s1, M):
    return jnp.concatenate([jnp.broadcast_to(jnp.mean(s0, axis=1, keepdims=True), (M, 1)),
                            jnp.broadcast_to(jnp.mean(s1, axis=1, keepdims=True), (M, 1))], axis=0)


def _stack4_impl(q):
    lane = _iota((q.shape[0], 128), 1)
    parts = []
    for p in range(2):
        qp = q[:, 128 * p:128 * (p + 1)]
        parts += [jnp.where(lane < HD, qp, 0.0), jnp.where(lane >= HD, qp, 0.0)]
    return jnp.concatenate(parts, axis=0)


def _unstack4_impl(o):
    M = o.shape[0] // 4
    lane = _iota((M, 128), 1)
    return jnp.concatenate([jnp.where(lane < HD, o[0:M], o[M:2 * M]),
                            jnp.where(lane < HD, o[2 * M:3 * M], o[3 * M:4 * M])], axis=1)


@jax.custom_vjp
def _stack4(q):
    return _stack4_impl(q)


_stack4.defvjp(lambda q: (_stack4_impl(q), None), lambda _, g: (_unstack4_impl(g),))


@jax.custom_vjp
def _unstack4(o):
    return _unstack4_impl(o)


_unstack4.defvjp(lambda o: (_unstack4_impl(o), None), lambda _, g: (_stack4_impl(g),))


WA_NB = 4


def _wa_blocks(qs, kws, vws, kx, vx, sks, n0, L):
    sc = HD ** -0.5
    sink = jnp.concatenate([jnp.broadcast_to(jnp.mean(s_, axis=1, keepdims=True), (Q, 1)) for s_ in sks], axis=0)
    bias = []
    for b_ in range(len(qs)):
        n = n0 + b_
        qpos = n * Q + (_iota((4 * Q, 3 * Q), 0) & (Q - 1))
        kpos = (n - 1) * Q + _iota((4 * Q, 3 * Q), 1)
        bias.append(jnp.where((jnp.abs(qpos - kpos) <= Q) & (kpos >= 0) & (kpos < L), 0.0, NEG))
    q4 = [_stack4(q) for q in qs]
    sl = [mm_nt(a, k) * sc + b_ for a, k, b_ in zip(q4, kws, bias)]
    sx = [mm_nt(a, kx) * sc for a in q4]
    m = [lax.stop_gradient(jnp.maximum(jnp.maximum(jnp.max(a, axis=1, keepdims=True),
                                                   jnp.max(b_, axis=1, keepdims=True)), sink))
         for a, b_ in zip(sl, sx)]
    el = [jnp.exp(a - c) for a, c in zip(sl, m)]
    ex = [jnp.exp(a - c) for a, c in zip(sx, m)]
    inv = [1.0 / (jnp.sum(a, axis=1, keepdims=True) + jnp.sum(b_, axis=1, keepdims=True) + jnp.exp(sink - c))
           for a, b_, c in zip(el, ex, m)]
    return [_unstack4(mm(a * i, v) + mm(b_ * i, vx)) for a, b_, i, v in zip(el, ex, inv, vws)]


def _wa_load(q_r, k_r, v_r, n0):
    f = lambda t: t.astype(F32)
    qs = [f(q_r[b_ * Q:(b_ + 1) * Q, :]) for b_ in range(WA_NB)]
    wins = [pl.ds(pl.multiple_of((n0 + b_) * Q, Q), 3 * Q) for b_ in range(WA_NB)]
    return qs, [f(k_r[w, :]) for w in wins], [f(v_r[w, :]) for w in wins], wins


def _wa_specs(L):
    nb = L // Q
    qs = pl.BlockSpec((WA_NB * Q, 256), lambda n: (n, 0))
    kfull = pl.BlockSpec((L + LC + Q, 128), lambda n: (0, 0))
    sks = pl.BlockSpec((2, 2, 1, 128), lambda n: (0, 0, 0, 0))
    return nb, qs, kfull, sks


def wa_fwd(QA, KA, VA, sinkp, L, sends=()):
    nb, qs, kfull, sks = _wa_specs(L)
    pad = lambda a: jnp.concatenate([jnp.zeros((Q, 128), a.dtype), a], axis=0)

    def body(q_r, k_r, v_r, sk_r, o_ref):
        n0 = pl.program_id(0) * WA_NB
        qs_, kws, vws, _ = _wa_load(q_r, k_r, v_r, n0)
        cx = pl.ds(Q + L, LC)
        outs = _wa_blocks(qs_, kws, vws, k_r[cx, :].astype(F32), v_r[cx, :].astype(F32),
                          [sk_r[0, 0], sk_r[0, 1], sk_r[1, 0], sk_r[1, 1]], n0, L)
        o_ref[...] = jnp.concatenate(outs, axis=0)

    return _pc(body, "wa_fwd", _sds((L, 256)), grid=(nb // WA_NB,), in_specs=[qs, kfull, kfull, sks], out_specs=qs,
               sends=sends, gather=True)(QA, pad(KA), pad(VA), sinkp)


def wa_bwd(QA, KA, VA, sinkp, dO, L, sends=()):
    nb, qs, kfull, sks = _wa_specs(L)
    pad = lambda a: jnp.concatenate([jnp.zeros((Q, 128), a.dtype), a], axis=0)

    def body(q_r, k_r, v_r, sk_r, do_r, dq_o, dk_o, dv_o, dsk_o):
        n0 = pl.program_id(0) * WA_NB
        _acc_init(n0 == 0, [dk_o, dv_o, dsk_o])
        qs_, kws, vws, wins = _wa_load(q_r, k_r, v_r, n0)
        cx = pl.ds(Q + L, LC)
        fn = lambda a, b, c, d, e, s_: _wa_blocks(a, b, c, d, e, s_, n0, L)
        _, vjp = jax.vjp(fn, qs_, kws, vws, k_r[cx, :].astype(F32), v_r[cx, :].astype(F32),
                         [sk_r[0, 0], sk_r[0, 1], sk_r[1, 0], sk_r[1, 1]])
        dqs, dkws, dvws, dkx, dvx, ds = vjp([do_r[b_ * Q:(b_ + 1) * Q, :] for b_ in range(WA_NB)])
        dq_o[...] = jnp.concatenate(dqs, axis=0)
        for w, dk, dv in zip(wins, dkws, dvws):
            dk_o[w, :] += dk
            dv_o[w, :] += dv
        dk_o[cx, :] += dkx
        dv_o[cx, :] += dvx
        for i_ in range(4):
            dsk_o[i_ // 2, i_ % 2] += ds[i_]

    return _pc(body, "wa_bwd", [_sds((L, 256)), _sds((L + LC + Q, 128)), _sds((L + LC + Q, 128)),
                                _sds((2, 2, 1, 128))],
               grid=(nb // WA_NB,), in_specs=[qs, kfull, kfull, sks, qs], out_specs=[qs, kfull, kfull, sks],
               sends=sends)(QA, pad(KA), pad(VA), sinkp, dO)


def _ctx_block(q, kx, vx, s0, s1):
    return _softmax_av(q, [kx], [vx], [None], _sink_col(s0, s1, LC))


def ctx_fwd(Qx, Kx, Vx, sinkp, shared, L):
    cq = pl.BlockSpec((LC, 128), lambda p: (L // LC, p))
    ck = pl.BlockSpec((LC, 128), lambda p: (L // LC, 0 if shared else p))
    sks = pl.BlockSpec((1, 2, 1, 128), lambda p: (p, 0, 0, 0))

    def body(q_r, k_r, v_r, sk_r, o_ref):
        f = lambda t: t[...].astype(F32)
        o_ref[...] = _ctx_block(f(q_r), f(k_r), f(v_r), sk_r[0, 0], sk_r[0, 1])

    return _pc(body, "ctx_fwd", _sds((LC, 256)), grid=(2,), in_specs=[cq, ck, ck, sks],
               out_specs=pl.BlockSpec((LC, 128), lambda p: (0, p)))(Qx, Kx, Vx, sinkp)


def ctx_bwd(Qx, Kx, Vx, sinkp, dO, shared, L):
    cq = pl.BlockSpec((LC, 128), lambda p: (L // LC, p))
    ck = pl.BlockSpec((LC, 128), lambda p: (L // LC, 0 if shared else p))
    sks = pl.BlockSpec((1, 2, 1, 128), lambda p: (p, 0, 0, 0))
    op = pl.BlockSpec((LC, 128), lambda p: (0, p))
    ok = pl.BlockSpec((LC, 128), lambda p: (0, 0 if shared else p))
    dos = pl.BlockSpec((LC, 128), lambda p: (L // LC, p))

    def body(q_r, k_r, v_r, sk_r, do_r, dq_o, dk_o, dv_o, dsk_o):
        p = pl.program_id(0)
        f = lambda t: t[...].astype(F32)
        _, vjp = jax.vjp(_ctx_block, f(q_r), f(k_r), f(v_r), sk_r[0, 0], sk_r[0, 1])
        dq, dk, dv, ds0, ds1 = vjp(do_r[...])
        dq_o[...] = dq
        _acc_init((p == 0) if shared else (p >= 0), [dk_o, dv_o])
        dk_o[...] += dk
        dv_o[...] += dv
        dsk_o[0, 0] = ds0
        dsk_o[0, 1] = ds1

    kw = 128 if shared else 256
    return _pc(body, "ctx_bwd", [_sds((LC, 256)), _sds((LC, kw)), _sds((LC, kw)), _sds((2, 2, 1, 128))],
               grid=(2,), in_specs=[cq, ck, ck, sks, dos], out_specs=[op, ok, ok, sks])(Qx, Kx, Vx, sinkp, dO)


def _na_rows(qs, kws, vws, kx, vx, bs):
    sc = HD ** -0.5
    q2 = [_stack(q) for q in qs]
    sl = [mm_nt(a, k) * sc + b for a, k, b in zip(q2, kws, bs)]
    sx = [mm_nt(a, kx) * sc for a in q2]
    m = [lax.stop_gradient(jnp.maximum(jnp.max(a, axis=1, keepdims=True), jnp.max(b, axis=1, keepdims=True)))
         for a, b in zip(sl, sx)]
    el = [jnp.exp(a - c) for a, c in zip(sl, m)]
    ex = [jnp.exp(a - c) for a, c in zip(sx, m)]
    inv = [1.0 / (jnp.sum(a, axis=1, keepdims=True) + jnp.sum(b, axis=1, keepdims=True)) for a, b in zip(el, ex)]
    o2 = [mm(a * i, v) + mm(b * i, vx) for a, b, i, v in zip(el, ex, inv, vws)]
    return [_unstack(o) for o in o2]


NA_ROWS = 16


def _na_geom(r, R):
    s = jnp.clip(r - 4, 0, R - 8)
    cls = jnp.where(r < 4, r, jnp.where(r > R - 4, r - (R - 8), 4))
    return pl.ds(pl.multiple_of(s * GW, GW), 8 * GW), cls


def _na_load(q_r, k_r, v_r, b_r, rb, R):
    nr = min(NA_ROWS, R)
    geo = [_na_geom(rb * nr + j, R) for j in range(nr)]
    qs = [q_r[j * GW:(j + 1) * GW, :].astype(F32) for j in range(nr)]
    kws = [k_r[win, :].astype(F32) for win, _ in geo]
    vws = [v_r[win, :].astype(F32) for win, _ in geo]
    bs = [jnp.concatenate([b_r[0, cls], b_r[1, cls]], axis=0) for _, cls in geo]
    return geo, qs, kws, vws, bs


def na_fwd(QB, KB, VB, biasd, L, sends=()):
    R = L // GW
    nr = min(NA_ROWS, R)
    qs = pl.BlockSpec((nr * GW, 128), lambda p, rb: (rb, p))
    kfull = pl.BlockSpec((L, 128), lambda p, rb: (0, p))
    kctx = pl.BlockSpec((LC, 128), lambda p, rb: (L // LC, p))
    bs = pl.BlockSpec((2, 8, GW, 8 * GW), lambda p, rb: (p, 0, 0, 0))

    def body(q_r, k_r, v_r, kx_r, vx_r, b_r, o_ref):
        _, qs_, kws, vws, bs_ = _na_load(q_r, k_r, v_r, b_r, pl.program_id(1), R)
        outs = _na_rows(qs_, kws, vws, kx_r[...].astype(F32), vx_r[...].astype(F32), bs_)
        o_ref[...] = jnp.concatenate(outs, axis=0)

    return _pc(body, "na_fwd", _sds((L, 256)), grid=(2, R // nr), in_specs=[qs, kfull, kfull, kctx, kctx, bs],
               out_specs=qs, sends=sends, gather=True)(QB, KB, VB, KB, VB, biasd)


def na_bwd(QB, KB, VB, biasd, dO, L):
    R = L // GW
    nr = min(NA_ROWS, R)
    qs = pl.BlockSpec((nr * GW, 128), lambda p, rb: (rb, p))
    kfull = pl.BlockSpec((L, 128), lambda p, rb: (0, p))
    kctx = pl.BlockSpec((LC, 128), lambda p, rb: (L // LC, p))
    bs = pl.BlockSpec((2, 8, GW, 8 * GW), lambda p, rb: (p, 0, 0, 0))
    oc = pl.BlockSpec((LC, 128), lambda p, rb: (0, p))

    def body(q_r, k_r, v_r, kx_r, vx_r, b_r, do_r, dq_o, dk_o, dv_o, dkx_o, dvx_o, db_o):
        rb = pl.program_id(1)
        _acc_init(rb == 0, [dk_o, dv_o, dkx_o, dvx_o, db_o])
        geo, qs_, kws, vws, bs_ = _na_load(q_r, k_r, v_r, b_r, rb, R)
        _, vjp = jax.vjp(_na_rows, qs_, kws, vws, kx_r[...].astype(F32), vx_r[...].astype(F32), bs_)
        dqs, dkws, dvws, dkx, dvx, dbs = vjp([do_r[j * GW:(j + 1) * GW, :] for j in range(nr)])
        dq_o[...] = jnp.concatenate(dqs, axis=0)
        dkx_o[...] += dkx
        dvx_o[...] += dvx
        for j, (win, cls) in enumerate(geo):
            dk_o[win, :] += dkws[j]
            dv_o[win, :] += dvws[j]
            db_o[0, cls] += dbs[j][:GW]
            db_o[1, cls] += dbs[j][GW:]

    return _pc(body, "na_bwd",
               [_sds((L, 256)), _sds((L, 256)), _sds((L, 256)), _sds((LC, 256)), _sds((LC, 256)),
                _sds((4, 8, GW, 8 * GW))],
               grid=(2, R // nr), in_specs=[qs, kfull, kfull, kctx, kctx, bs, qs],
               out_specs=[qs, kfull, kfull, oc, oc, bs])(QB, KB, VB, KB, VB, biasd, dO)


def exact_mm_call(A, B):
    def body(a_ref, b_ref, o_ref):
        o_ref[...] = _exact(a_ref[...], b_ref[...])

    return _pc(body, "exact_mm", _sds((A.shape[0], B.shape[1])))(A, B)


def _conv_shift(x, d, L):
    T = x.shape[0]
    if d == 0:
        return x
    t = _iota(x.shape, 0)
    src = t + d
    ok = (src >= 0) & (src < T) & ((src >= L) == (t >= L))
    return jnp.where(ok, pltpu.roll(x, (-d) % T, 0), 0.0)


def conv_fwd(XBC, w8, b, L, sends=()):
    T = XBC.shape[0]

    def body(x_ref, w_ref, b_ref, o_ref):
        x = x_ref[...]
        pre = b_ref[...] + functools.reduce(
            lambda a, c: a + c, [_conv_shift(x, k - 3, L) * w_ref[k:k + 1, :] for k in range(7)])
        o_ref[...] = _silu(pre)

    col = pl.BlockSpec((T, 128), lambda j: (0, j))
    return _pc(body, "conv_fwd", _sds((T, 1024)), grid=(8,),
               in_specs=[col, pl.BlockSpec((8, 128), lambda j: (0, j)), pl.BlockSpec((1, 128), lambda j: (0, j))],
               out_specs=col, sends=sends, gather=True)(XBC, w8, b)


def conv_bwd(XBC, w8, b, dS, dxs_skip, L, sends=()):
    T = XBC.shape[0]

    def body(x_ref, w_ref, b_ref, d0_r, d1_r, dsk_r, dx_o, dw_o, db_o):
        j = pl.program_id(0)
        x = x_ref[...]
        xs = [_conv_shift(x, k - 3, L) for k in range(7)]
        pre = b_ref[...] + functools.reduce(lambda a, c: a + c, [xs[k] * w_ref[k:k + 1, :] for k in range(7)])
        _, vjp = jax.vjp(_silu, pre)
        dact = d0_r[0] + d1_r[0] + jnp.where(j < 4, dsk_r[...], 0.0)
        dpre, = vjp(dact)
        dx_o[...] = functools.reduce(
            lambda a, c: a + c, [_conv_shift(dpre, 3 - k, L) * w_ref[k:k + 1, :] for k in range(7)])
        dw_o[...] = jnp.concatenate([jnp.sum(dpre * xs[k], axis=0, keepdims=True) for k in range(7)]
                                    + [jnp.zeros((1, 128), F32)], axis=0)
        db_o[...] = jnp.sum(dpre, axis=0, keepdims=True)

    col = pl.BlockSpec((T, 128), lambda j: (0, j))
    w_s = pl.BlockSpec((8, 128), lambda j: (0, j))
    b_s = pl.BlockSpec((1, 128), lambda j: (0, j))
    ds = lambda d: pl.BlockSpec((1, T, 128), lambda j: (d, 0, j))
    return _pc(body, "conv_bwd", [_sds((T, 1024)), _sds((8, 1024)), _sds((1, 1024))], grid=(8,),
               in_specs=[col, w_s, b_s, ds(0), ds(1), pl.BlockSpec((T, 128), lambda j: (0, jnp.minimum(j, 3)))],
               out_specs=[col, w_s, b_s], sends=sends)(XBC, w8, b, dS, dS, dxs_skip)


def _ssd_chunk(xs, bs, cs, dtraw, dtb, alog, hs, tri, d):
    dt = _softplus(dtraw + dtb)
    a = dt * (-jnp.exp(alog))
    acum = _exact(tri, a)
    tot = jnp.sum(a, axis=0, keepdims=True)
    wcol = jnp.exp(tot - acum) * dt
    ea = jnp.exp(acum)
    cd = jnp.exp(tot)
    acum_t, dt_t = acum.T, dt.T
    lane = _iota((Q, 128), 1)
    srow = _iota((128, Q), 0)
    lane1 = _iota((1, 128), 1)
    prow = _iota((128, NSTATE), 0)
    mask = tri > 0.5
    cbs = [mm_nt(cs[g], bs[g]) for g in range(2)]
    ys, hn = [], []
    for j in range(4):
        g = j // 2
        x = xs[j]
        yi, st, eac, cdl = [], [], [], []
        for u in range(2):
            slot = d * 8 + 2 * j + u
            col = lambda m: jnp.sum(jnp.where(lane == slot, m, 0.0), axis=1, keepdims=True)
            rowv = lambda m: jnp.sum(jnp.where(srow == slot, m, 0.0), axis=0, keepdims=True)
            seg = col(acum) - rowv(acum_t)
            dcy = jnp.where(mask, jnp.exp(jnp.where(mask, seg, 0.0)), 0.0)
            yi.append(mm(cbs[g] * dcy * rowv(dt_t), x))
            st.append(mm_tn(x, bs[g] * col(wcol)))
            eac.append(col(ea))
            cdl.append(jnp.sum(jnp.where(lane1 == slot, cd, 0.0), axis=1, keepdims=True))
        yin = mm_nt(cs[g], hs[j])
        ys.append(jnp.where(lane < HD, yi[0] + yin * eac[0], yi[1] + yin * eac[1]))
        hn.append(hs[j] * jnp.where(prow < HD, cdl[0], cdl[1]) + jnp.where(prow < HD, st[0], st[1]))
    return ys, hn


SSD_SUB = 2


def _ssd_block_idx(d, s, nlb, nbk):
    return jnp.where(d == 0, (s + nlb) % nbk, nbk - 1 - s)


def _ssd_rows(d, i):
    return pl.ds(pl.multiple_of(jnp.where(d == 0, i, SSD_SUB - 1 - i) * Q, Q), Q)


def _ssd_split(a):
    return ([a[:, 128 * j:128 * (j + 1)] for j in range(4)], [a[:, 512 + 128 * g:640 + 128 * g] for g in range(2)],
            [a[:, 768 + 128 * g:896 + 128 * g] for g in range(2)])


def ssd_fwd(ACT, DT, dtb, alog, tri2, L, sends=()):
    T = ACT.shape[0]
    RB = SSD_SUB * Q
    nlb, nbk = L // RB, T // RB

    def body(a_ref, dt_ref, dtb_ref, al_ref, tri_ref, y_o, hs_o, hst):
        d, s = pl.program_id(0), pl.program_id(1)
        _acc_init(s == 0, [hst])
        for i in range(SSD_SUB):
            rows = _ssd_rows(d, i)
            xs, bs, cs = _ssd_split(a_ref[rows, :])
            hs_o[0, i] = hst[...]
            ys, hn = _ssd_chunk(xs, bs, cs, dt_ref[rows, :], dtb_ref[...], al_ref[...], [hst[j] for j in range(4)],
                                tri_ref[0], d)
            y_o[0, rows, :] = jnp.concatenate(ys, axis=1)
            for j in range(4):
                hst[j] = hn[j]

    bk = lambda w: pl.BlockSpec((RB, w), lambda d, s: (_ssd_block_idx(d, s, nlb, nbk), 0))
    v128 = pl.BlockSpec((1, 128), lambda d, s: (0, 0))
    return _pc(body, "ssd_fwd", [_sds((2, T, 512)), _sds((2, T // Q, 4, 128, NSTATE))], grid=(2, nbk),
               in_specs=[bk(1024), bk(128), v128, v128, pl.BlockSpec((1, Q, Q), lambda d, s: (d, 0, 0))],
               out_specs=[pl.BlockSpec((1, RB, 512), lambda d, s: (d, _ssd_block_idx(d, s, nlb, nbk), 0)),
                          pl.BlockSpec((1, SSD_SUB, 4, 128, NSTATE), lambda d, s: (d, s, 0, 0, 0))],
               scratch=[pltpu.VMEM((4, 128, NSTATE), F32)], sends=sends, gather=True)(ACT, DT, dtb, alog, tri2)


def ssd_bwd(ACT, DT, dtb, alog, tri2, HS, dY, L, sends=()):
    T = ACT.shape[0]
    RB = SSD_SUB * Q
    nlb, nbk = L // RB, T // RB

    def body(a_ref, dt_ref, dtb_ref, al_ref, tri_ref, hs_ref, dy_ref, da_o, ddt_o, ddtb_o, dal_o, dh):
        d, sr = pl.program_id(0), pl.program_id(1)
        _acc_init(sr == 0, [dh, ddtb_o, dal_o])
        tri = tri_ref[0]
        fn = lambda xs_, bs_, cs_, dtr, dtb_, al, hs_: _ssd_chunk(xs_, bs_, cs_, dtr, dtb_, al, hs_, tri, d)
        for i in reversed(range(SSD_SUB)):
            rows = _ssd_rows(d, i)
            xs, bs, cs = _ssd_split(a_ref[rows, :])
            _, vjp = jax.vjp(fn, xs, bs, cs, dt_ref[rows, :], dtb_ref[...], al_ref[...],
                             [hs_ref[0, i, j] for j in range(4)])
            dy = dy_ref[rows, :]
            dxs, dbs, dcs, ddt, ddtb, dal, dhs = vjp(([dy[:, 128 * j:128 * (j + 1)] for j in range(4)],
                                                      [dh[j] for j in range(4)]))
            da_o[0, rows, :] = jnp.concatenate(dxs + dbs + dcs, axis=1)
            ddt_o[0, rows, :] = ddt
            ddtb_o[0] += ddtb
            dal_o[0] += dal
            for j in range(4):
                dh[j] = dhs[j]

    bidx = lambda d, sr: _ssd_block_idx(d, nbk - 1 - sr, nlb, nbk)
    bk = lambda w: pl.BlockSpec((RB, w), lambda d, sr: (bidx(d, sr), 0))
    v128 = pl.BlockSpec((1, 128), lambda d, sr: (0, 0))
    o128 = pl.BlockSpec((1, 1, 128), lambda d, sr: (d, 0, 0))
    return _pc(body, "ssd_bwd", [_sds((2, T, 1024)), _sds((2, T, 128)), _sds((2, 1, 128)), _sds((2, 1, 128))],
               grid=(2, nbk),
               in_specs=[bk(1024), bk(128), v128, v128, pl.BlockSpec((1, Q, Q), lambda d, sr: (d, 0, 0)),
                         pl.BlockSpec((1, SSD_SUB, 4, 128, NSTATE), lambda d, sr: (d, nbk - 1 - sr, 0, 0, 0)), bk(512)],
               out_specs=[pl.BlockSpec((1, RB, 1024), lambda d, sr: (d, bidx(d, sr), 0)),
                          pl.BlockSpec((1, RB, 128), lambda d, sr: (d, bidx(d, sr), 0)), o128, o128],
               scratch=[pltpu.VMEM((4, 128, NSTATE), F32)], sends=sends)(ACT, DT, dtb, alog, tri2, HS, dY)


_PAIR_HEADS = np.array([[0, 2], [1, 3]])


def _tables(L):
    t = jnp.arange(L)
    inv = 10000.0 ** (-jnp.arange(16, dtype=F32) / 16)

    def half(pos):
        ang = pos.astype(F32)[:, None] * inv[None, :]
        return jnp.concatenate([ang, ang], axis=1)

    ang = jnp.tile(jnp.concatenate([half(t // GW), half(t % GW)], axis=1), (1, 4))
    cos = jnp.concatenate([jnp.cos(ang), jnp.ones((LC, 256), F32)], axis=0)
    sin = jnp.concatenate([jnp.sin(ang), jnp.zeros((LC, 256), F32)], axis=0)
    rm = np.zeros((256, 256), np.float32)
    for j in range(256):
        if j % 32 < 16:
            rm[j + 16, j] = -1.0
        else:
            rm[j - 16, j] = 1.0
    tri = np.tril(np.ones((Q, Q), np.float32))
    return cos, sin, jnp.asarray(rm), jnp.asarray(np.stack([tri, tri.T]))


def _na_index(R):
    rc = np.array([0, 1, 2, 3, 4, R - 3, R - 2, R - 1])
    dy = np.clip(rc - 4, 0, R - 8)[:, None] + np.arange(8)[None, :] - rc[:, None] + 7
    qc, cc = np.arange(GW)[:, None], np.arange(GW)[None, :]
    dx = np.clip(cc - qc, -15, 15) + 15
    cstart = np.clip(qc - 8, 0, GW - 16)
    cmask = (cc >= cstart) & (cc < cstart + 16)
    idx = dy[:, None, :, None] * 31 + dx[None, :, None, :]
    return idx.reshape(8, GW, 8 * GW), np.broadcast_to(cmask[None, :, None, :], idx.shape).reshape(8, GW, 8 * GW), \
        dy, dx, cmask


def _na_bias(rpb, R):
    _, cm, dy, _, _ = _na_index(R)
    rows = rpb[:, dy.reshape(-1), :].reshape(4, 8, 4, 2, 31)
    p2 = jnp.pad(jnp.pad(rows, ((0, 0),) * 4 + ((0, 33),)).reshape(4, 8, 4, 128), ((0, 0), (0, 0), (0, 4), (0, 0)))
    negmask = jnp.asarray(np.where(cm[0], 0.0, NEG).astype(np.float32))

    def body(p_ref, m_ref, o_ref):
        for c in range(8):
            tiles = [pltpu.roll(jnp.broadcast_to(p_ref[0, c, jp:jp + 1, :], (GW, 128)), 113, 1, stride=1,
                                stride_axis=0) for jp in range(4)]
            o_ref[0, c] = jnp.where(m_ref[...] < 0.0, NEG, jnp.concatenate(tiles, axis=1))

    return _pc(body, "na_bias", _sds((4, 8, GW, 8 * GW)), grid=(4,),
               in_specs=[pl.BlockSpec((1, 8, 8, 128), lambda h: (h, 0, 0, 0)),
                         pl.BlockSpec((GW, 8 * GW), lambda h: (0, 0))],
               out_specs=pl.BlockSpec((1, 8, GW, 8 * GW), lambda h: (h, 0, 0, 0)))(p2, negmask)


def _na_bias_grad(dbias, R):
    _, _, dy, dx, cmask = _na_index(R)
    e1 = np.zeros((GW * GW, 128), np.float32)
    e1[np.arange(GW * GW), dx.reshape(-1)] = cmask.reshape(-1)
    a1 = dbias.reshape(4, 8, GW, 8, GW).transpose(0, 1, 3, 2, 4).reshape(256, GW * GW)
    v = exact_mm_call(a1, jnp.asarray(e1))[:, :31].reshape(4, 64, 31)
    e2 = np.zeros((64, 128), np.float32)
    e2[np.arange(64), dy.reshape(-1)] = 1.0
    a2 = jnp.pad(v.transpose(0, 2, 1).reshape(124, 64), ((0, 4), (0, 0)))
    return exact_mm_call(a2, jnp.asarray(e2))[:124, :15].reshape(4, 31, 15).transpose(0, 2, 1)


def _lanes(v, n=128):
    v = v.reshape(1, -1)
    return jnp.pad(v, ((0, 0), (0, n - v.shape[1])))


def _cls2(a, b):
    return jnp.stack([a, b]).reshape(2, 1, D)


def _win_p(g):
    return jnp.concatenate([g.reshape(IN_COLS, D), jnp.zeros((NP_IN - IN_COLS, D), g.dtype)], axis=0)


def _layer_consts(p):
    sinkp = jnp.broadcast_to(p["wa_sink"][_PAIR_HEADS][:, :, None, None], (2, 2, 1, 128))
    return dict(
        sinkp=sinkp, nosink=jnp.full((2, 2, 1, 128), NEG, F32),
        w8=jnp.concatenate([p["ssm_conv_w"], jnp.zeros((1, 1024), F32)], axis=0),
        cb=p["ssm_conv_b"].reshape(1, 1024), dtb=_lanes(p["ssm_dt_bias"]), alog=_lanes(p["ssm_a_log"]),
        dsk=jnp.repeat(p["ssm_d"], HD).reshape(1, 512), gs=p["ssm_norm_g"].reshape(1, 512),
        gmix=p["g_mix"].reshape(1, D), gffn=p["g_ffn"].reshape(1, D))


def _mods(mod2):
    return [_cls2(mod2[0, D * k:D * (k + 1)], mod2[1, D * k:D * (k + 1)]) for k in range(6)]


def _layer_fwd(X, mod2, c, rpb, tabs, L, ctx_out, shards, nxt):
    cos, sin, rm, tri2 = tabs
    sh1, sc1, gt1, sh2, sc2, gt2 = _mods(mod2)
    biasd = _na_bias(rpb, L // GW)
    fi, fo, wo = shards
    fcut, fcut2, ocut = 384, 576, 224
    (qa, qb, z, ka, va, kb, vb, xbc, dt, h1), (gfo_a,) = in_fwd(X, c["gmix"], sh1, sc1, c["win"], cos, sin, rm, L,
                                                                sends=(fo[:ocut],))
    (oa,), (gfi_b,) = wa_fwd(qa, ka, va, c["sinkp"], L, sends=(fi[fcut:fcut2],))
    (ob,), (gwo,) = na_fwd(qb, kb, vb, biasd, L, sends=(wo,))
    c = dict(c, wout=gwo.reshape(D, D))
    if ctx_out:
        oa_c = ctx_fwd(qa, ka, va, c["sinkp"], True, L)
        ob_c = ctx_fwd(qb, kb, vb, c["nosink"], False, L)
    else:
        oa_c = ob_c = jnp.zeros((LC, 256), F32)
    oa = jnp.concatenate([oa, oa_c], axis=0)
    ob = jnp.concatenate([ob, ob_c], axis=0)
    (act,), (gfi_c,) = conv_fwd(xbc, c["w8"], c["cb"], L, sends=(fi[fcut2:],))
    (y2, hs), (gfi_a,) = ssd_fwd(act, dt, c["dtb"], c["alog"], tri2, L, sends=(fi[:fcut],))
    (X1, cat), (gfo_b,) = out_fwd(oa, ob, y2, act, z, c["dsk"], c["gs"], c["wout"], X, gt1, L, sends=(fo[ocut:],))
    c = dict(c, wfi=jnp.concatenate([gfi_a, gfi_b, gfi_c], axis=1).reshape(2 * DFF, D),
             wfo=jnp.concatenate([gfo_a, gfo_b], axis=1).reshape(DFF, D))
    res = ffn_fwd(X1, c["gffn"], sh2, sc2, gt2, c["wfi"], c["wfo"], L, sends=nxt, skip_ctx=not ctx_out)
    (X2, ff), got = res if nxt else (res, ())
    saved = dict(X=X, X1=X1, ff=ff, qa=qa, qb=qb, z=z, ka=ka, va=va, kb=kb, vb=vb, xbc=xbc, dt=dt, h1=h1, oa=oa, ob=ob,
                 act=act, y2=y2, hs=hs, cat=cat, biasd=biasd)
    return X2, saved, c, got


def _row_blocks(gw):
    return gw.reshape(NDEV, gw.shape[0] // NDEV, gw.shape[1])


def _layer_bwd(dX2, s, mod2, c, tabs, L, ctx_out, carry):
    cos, sin, rm, tri2 = tabs
    sh1, sc1, gt1, sh2, sc2, gt2 = _mods(mod2)
    R = L // GW
    res = ffn_bwd(s["X1"], c["gffn"], sh2, sc2, gt2, c["wfi"], c["wfo"], s["ff"], dX2, L, sends=carry,
                  skip_ctx=not ctx_out)
    (dX1, h2, dU, actf, dOut, dgffn, dsh2, dsc2, dgt2), got = res if carry else (res, ())
    g = {}
    gfi = _row_blocks(tn_mm(dU, h2, 512, 1024, MXU))
    gfo = _row_blocks(tn_mm(actf, dOut, 256, 1024, MXU))
    doa, dob, dy, dxs_skip, dz, dmix, ddsk, dgs, dgt1 = out_bwd(s["oa"], s["ob"], s["y2"], s["act"], s["z"], c["dsk"],
                                                                c["gs"], c["wout"], gt1, dX1, L)
    gout = _row_blocks(tn_mm(s["cat"], dmix, 512, 1024, MXU))
    (dS, ddt2, ddtb, dal), (g["w_ffn_in"],) = ssd_bwd(
        s["act"], s["dt"], c["dtb"], c["alog"], tri2, s["hs"], dy, L, sends=(gfi,))
    (dxbc, dw8, dcb), (g["w_ffn_out"],) = conv_bwd(s["xbc"], c["w8"], c["cb"], dS, dxs_skip, L, sends=(gfo,))
    (dqa, dka, dva, dska), (g["w_out"],) = wa_bwd(s["qa"], s["ka"], s["va"], c["sinkp"], doa, L, sends=(gout,))
    dka, dva = dka[Q:], dva[Q:]
    dqb, dkb, dvb, dkxb, dvxb, dbias = na_bwd(s["qb"], s["kb"], s["vb"], s["biasd"], dob, L)
    if ctx_out:
        dqa_c, dk1, dv1, dsk1 = ctx_bwd(s["qa"], s["ka"], s["va"], c["sinkp"], doa, True, L)
        dqb_c, dk2, dv2, _ = ctx_bwd(s["qb"], s["kb"], s["vb"], c["nosink"], dob, False, L)
        dka = jnp.concatenate([dka[:L], dka[L:] + dk1], axis=0)
        dva = jnp.concatenate([dva[:L], dva[L:] + dv1], axis=0)
        dska = dska + dsk1
        dkxb, dvxb = dkxb + dk2, dvxb + dv2
    else:
        dqa_c = dqb_c = jnp.zeros((LC, 256), F32)
    cat0 = lambda a, b: jnp.concatenate([a, b], axis=0)
    dX, dycat, dgmix, dsh1, dsc1 = in_bwd(
        s["X"], c["gmix"], sh1, sc1, c["win"], cos, sin, rm, dX1, cat0(dqa, dqa_c), cat0(dqb, dqb_c), dz,
        dka, dva, cat0(dkb, dkxb), cat0(dvb, dvxb), dxbc, ddt2, L,
        latent_only=ctx_out)
    if ctx_out:
        gin = [_row_blocks(tn_mm(dycat, s["h1"], 512, D // 2, MXU, ncol=D // 2, col0=k)[:IN_COLS]) for k in (0, 1)]
    else:
        gin = _row_blocks(tn_mm(dycat, s["h1"], 512, 1024, MXU)[:IN_COLS])
    g["g_mix"] = dgmix.reshape(D)
    g["g_ffn"] = dgffn.reshape(D)
    sk = jnp.sum(dska, axis=(2, 3))
    g["wa_sink"] = jnp.zeros((4,), F32).at[_PAIR_HEADS.reshape(-1)].set(sk.reshape(-1))
    g["na_rpb"] = _na_bias_grad(dbias, R)
    g["ssm_conv_w"] = dw8[:7]
    g["ssm_conv_b"] = dcb.reshape(1024)
    g["ssm_dt_bias"] = (ddtb[0] + ddtb[1])[0, :16].reshape(2, 8)
    g["ssm_a_log"] = (dal[0] + dal[1])[0, :16].reshape(2, 8)
    g["ssm_d"] = jnp.sum(ddsk.reshape(8, HD), axis=1)
    g["ssm_norm_g"] = dgs.reshape(512)
    dmod2 = jnp.concatenate([dsh1, dsc1, dgt1, dsh2, dsc2, dgt2], axis=2).reshape(2, 6 * D)
    return dX, g, dmod2, gin, got


def local_step(x, ctx, tgt, mods, layers, shards, g_final, L):
    tabs = _tables(L)
    X = (x, ctx)
    consts = [_layer_consts(p) for p in layers]
    saved = []
    got = (shards["w_in_first"],)
    for i in range(2):
        consts[i] = dict(consts[i], win=_win_p(got[0]))
        nxt = (shards["w_in"][1],) if i == 0 else ()
        X, s, consts[i], got = _layer_fwd(X, mods[i], consts[i], layers[i]["na_rpb"], tabs, L, i == 0,
                                          (shards["w_ffn_in"][i], shards["w_ffn_out"][i], shards["w_out"][i]), nxt)
        saved.append(s)
    loss8, dX, dgfin = loss_head(X, g_final.reshape(1, D), tgt, L)
    grads, dmods = [None, None], [None, None]
    dX, grads[1], dmods[1], gin1, _ = _layer_bwd(dX, saved[1], mods[1], consts[1], tabs, L, False, ())
    dX, grads[0], dmods[0], gin0, (grads[1]["w_in"],) = _layer_bwd(dX, saved[0], mods[0], consts[0], tabs, L, True,
                                                                   (gin1,))
    return loss8[0, 0], dX, grads, jnp.stack(dmods), dgfin.reshape(D), gin0


def _place():
    x, y, c = lax.axis_index("x"), lax.axis_index("y"), lax.axis_index("c")
    return x, y, c


def _slot(b):
    return 4 * b[0] + 2 * b[1] + b[2]


def _any():
    return pl.BlockSpec(memory_space=pl.ANY)


def all_gather(xs, name):
    n = len(xs)

    def body(*refs):
        x_refs, o_refs = refs[:n], refs[n:2 * n]
        send_sems, recv_sems, local_sems = refs[2 * n:]
        x, y, c = _place()
        me, sib = (x, y, c), (x, y, 1 - c)
        chips = [(1 - x, y), (x, 1 - y), (1 - x, 1 - y)]

        def copy(t, k, blk, to, src=None):
            dst = o_refs[t].at[_slot(blk)]
            return pltpu.make_async_remote_copy(
                src_ref=dst if src is None else src, dst_ref=dst, send_sem=send_sems.at[7 * t + k],
                recv_sem=recv_sems.at[7 * t + k], device_id=to, device_id_type=MESH_T)

        mine = [pltpu.make_async_copy(x_refs[t], o_refs[t].at[_slot(me)], local_sems.at[t]) for t in range(n)]
        for cp in mine:
            cp.start()
        first = []
        for t in range(n):
            first.append(copy(t, 0, me, sib, src=x_refs[t]))
            first += [copy(t, 1 + j, me, (*chip, c), src=x_refs[t]) for j, chip in enumerate(chips)]
        for cp in first:
            cp.start()
        passed = []
        for j, chip in enumerate(chips):
            for t in range(n):
                copy(t, 1 + j, (*chip, c), me).wait_recv()
                cp = copy(t, 4 + j, (*chip, c), sib)
                cp.start()
                passed.append(cp)
        for t in range(n):
            copy(t, 0, sib, me).wait_recv()
            for j, chip in enumerate(chips):
                copy(t, 4 + j, (*chip, 1 - c), me).wait_recv()
        for cp in first + passed:
            cp.wait_send()
        for cp in mine:
            cp.wait()

    return pl.pallas_call(
        body, name=name, out_shape=[_sds((NDEV,) + a.shape, a.dtype) for a in xs],
        in_specs=[_any()] * n, out_specs=[_any()] * n,
        scratch_shapes=[pltpu.SemaphoreType.DMA((7 * n,)), pltpu.SemaphoreType.DMA((7 * n,)),
                        pltpu.SemaphoreType.DMA((n,))],
        interpret=_INTERPRET)(*xs)


def _a2a_sems(n):
    return [pltpu.SemaphoreType.DMA((7 * n,)), pltpu.SemaphoreType.DMA((7 * n,)), pltpu.SemaphoreType.DMA((n,))]


def _a2a_copies(x_refs, o_refs, send_sems, recv_sems, local_sems):
    n = len(x_refs)
    x, y, c = _place()
    me = (x, y, c)
    flip = lambda v, b: (1 - v) if b else v
    peers = [(flip(x, k >> 2 & 1), flip(y, k >> 1 & 1), flip(c, k & 1)) for k in range(1, NDEV)]
    mine = [pltpu.make_async_copy(x_refs[t].at[_slot(me)], o_refs[t].at[_slot(me)], local_sems.at[t])
            for t in range(n)]

    def copy(t, k, src_slot, dst_slot, to):
        return pltpu.make_async_remote_copy(
            src_ref=x_refs[t].at[src_slot], dst_ref=o_refs[t].at[dst_slot], send_sem=send_sems.at[7 * t + k],
            recv_sem=recv_sems.at[7 * t + k], device_id=to, device_id_type=MESH_T)

    sends = [copy(t, k, _slot(p), _slot(me), p) for t in range(n) for k, p in enumerate(peers)]
    recvs = [copy(t, k, _slot(p), _slot(p), me) for t in range(n) for k, p in enumerate(peers)]
    return mine, sends, recvs


def _ag_copies(x_refs, o_refs, send_sems, recv_sems, local_sems):
    n = len(x_refs)
    x, y, c = _place()
    me = (x, y, c)
    flip = lambda v, b: (1 - v) if b else v
    peers = [(flip(x, k >> 2 & 1), flip(y, k >> 1 & 1), flip(c, k & 1)) for k in range(1, NDEV)]
    mine = [pltpu.make_async_copy(x_refs[t], o_refs[t].at[_slot(me)], local_sems.at[t]) for t in range(n)]

    def copy(t, k, dst_slot, to):
        return pltpu.make_async_remote_copy(
            src_ref=x_refs[t], dst_ref=o_refs[t].at[dst_slot], send_sem=send_sems.at[7 * t + k],
            recv_sem=recv_sems.at[7 * t + k], device_id=to, device_id_type=MESH_T)

    sends = [copy(t, k, _slot(me), p) for t in range(n) for k, p in enumerate(peers)]
    recvs = [copy(t, k, _slot(p), me) for t in range(n) for k, p in enumerate(peers)]
    return mine, sends, recvs


def _ag_start(x_refs, o_refs, send_sems, recv_sems, local_sems):
    mine, sends, _ = _ag_copies(x_refs, o_refs, send_sems, recv_sems, local_sems)
    for cp in mine + sends:
        cp.start()


def _ag_wait(x_refs, o_refs, send_sems, recv_sems, local_sems):
    mine, sends, recvs = _ag_copies(x_refs, o_refs, send_sems, recv_sems, local_sems)
    for cp in recvs:
        cp.wait_recv()
    for cp in sends:
        cp.wait_send()
    for cp in mine:
        cp.wait()


def _a2a_start(x_refs, o_refs, send_sems, recv_sems, local_sems):
    mine, sends, _ = _a2a_copies(x_refs, o_refs, send_sems, recv_sems, local_sems)
    for cp in mine + sends:
        cp.start()


def _a2a_wait(x_refs, o_refs, send_sems, recv_sems, local_sems):
    mine, sends, recvs = _a2a_copies(x_refs, o_refs, send_sems, recv_sems, local_sems)
    for cp in recvs:
        cp.wait_recv()
    for cp in sends:
        cp.wait_send()
    for cp in mine:
        cp.wait()


def adam_reduce(P, w, m, v, name, sends=()):
    n, R, C = P.shape
    br = R // 4 if R % 64 == 0 else R

    def body(p_ref, w_ref, m_ref, v_ref, g_o, d_o, m_o, v_o):
        g = p_ref[0].astype(F32)
        for k in range(1, n):
            g = g + p_ref[k].astype(F32)
        m1 = ADAM_B1 * m_ref[...] + (1.0 - ADAM_B1) * g
        v1 = ADAM_B2 * v_ref[...] + (1.0 - ADAM_B2) * jnp.square(g)
        m_hat = m1 / (1.0 - ADAM_B1 ** ADAM_STEP)
        v_hat = v1 / (1.0 - ADAM_B2 ** ADAM_STEP)
        g_o[...] = g
        d_o[...] = -ADAM_LR * (m_hat / (jnp.sqrt(v_hat) + ADAM_EPS) + ADAM_WD * w_ref[...])
        m_o[...] = m1
        v_o[...] = v1

    blk = pl.BlockSpec((br, C), lambda i: (i, 0))
    return _pc(body, name, [_sds((R, C))] * 4, grid=(R // br,),
               in_specs=[pl.BlockSpec((n, br, C), lambda i: (0, i, 0)), blk, blk, blk], out_specs=[blk] * 4,
               sends=sends)(P, w, m, v)


def adam_layers(P0, P1, w, m, v, name, sends=()):
    n, R, C = P0.shape
    br = R // 4 if R % 64 == 0 else R
    nb = R // br

    def body(p0_ref, p1_ref, w_ref, m_ref, v_ref, g_o, d_o, m_o, v_o):
        def total(p_ref):
            g = p_ref[0].astype(F32)
            for k in range(1, n):
                g = g + p_ref[k].astype(F32)
            return g

        g = jnp.where(pl.program_id(0) == 0, total(p0_ref), total(p1_ref))
        m1 = ADAM_B1 * m_ref[0] + (1.0 - ADAM_B1) * g
        v1 = ADAM_B2 * v_ref[0] + (1.0 - ADAM_B2) * jnp.square(g)
        m_hat = m1 / (1.0 - ADAM_B1 ** ADAM_STEP)
        v_hat = v1 / (1.0 - ADAM_B2 ** ADAM_STEP)
        g_o[0] = g
        d_o[0] = -ADAM_LR * (m_hat / (jnp.sqrt(v_hat) + ADAM_EPS) + ADAM_WD * w_ref[0])
        m_o[0] = m1
        v_o[0] = v1

    blk = pl.BlockSpec((1, br, C), lambda l, i: (l, i, 0))
    p0 = pl.BlockSpec((n, br, C), lambda l, i: (0, jnp.where(l == 0, i, nb - 1), 0))
    p1 = pl.BlockSpec((n, br, C), lambda l, i: (0, jnp.where(l == 1, i, 0), 0))
    return _pc(body, name, [_sds((2, R, C))] * 4, grid=(2, nb), in_specs=[p0, p1, blk, blk, blk],
               out_specs=[blk] * 4, sends=sends)(P0, P1, w, m, v)


def mod_fwd(scin, wmod, bcol):
    def body(s_ref, w_ref, b_ref, o_ref):
        o_ref[0] = mm(_silu(s_ref[...]), w_ref[0]) + b_ref[0]

    return _pc(body, "mod_fwd", _sds((2, 16, 768)), grid=(2,),
               in_specs=[pl.BlockSpec((16, D), lambda l: (0, 0)), pl.BlockSpec((1, D, 768), lambda l: (l, 0, 0)),
                         pl.BlockSpec((1, 1, 768), lambda l: (l, 0, 0))],
               out_specs=pl.BlockSpec((1, 16, 768), lambda l: (l, 0, 0)))(scin, wmod, bcol)


def mod_bwd(scin, wmod, G):
    def body(s_ref, w_ref, g_ref, dw_o, ds_o):
        _, vjp = jax.vjp(lambda s, w: mm(_silu(s), w), s_ref[...], w_ref[0])
        ds, dw = vjp(g_ref[0])
        dw_o[0] = dw
        _acc_init(pl.program_id(0) == 0, [ds_o])
        ds_o[...] += ds

    full = pl.BlockSpec((16, D), lambda l: (0, 0))
    wsp = pl.BlockSpec((1, D, 768), lambda l: (l, 0, 0))
    return _pc(body, "mod_bwd", [_sds((2, D, 768)), _sds((16, D))], grid=(2,),
               in_specs=[full, wsp, pl.BlockSpec((1, 16, 768), lambda l: (l, 0, 0))], out_specs=[wsp, full])(
        scin, wmod, G)


_SMALL = ["b_mod", "g_mix", "wa_sink", "na_rpb", "ssm_conv_w", "ssm_conv_b", "ssm_dt_bias", "ssm_a_log", "ssm_d",
          "ssm_norm_g", "g_ffn", "g_final", "dmod_s", "dmod_c", "loss"]


def _pack(parts):
    rows = []
    for a in parts:
        f = a.reshape(-1).astype(F32)
        rows.append(jnp.pad(f, (0, (-f.shape[0]) % 1024)).reshape(-1, 128))
    return jnp.concatenate(rows, axis=0)


def _unpack(packed, shapes):
    out, r = [], 0
    for s in shapes:
        nel = int(np.prod(s))
        nr = -(-nel // 1024) * 8
        out.append(packed[r:r + nr].reshape(-1)[:nel].reshape(s))
        r += nr
    return out


def kernel(x, c, ctx, c_ctx, w_mod, b_mod, g_mix, w_in, wa_sink, na_rpb, ssm_conv_w, ssm_conv_b, ssm_dt_bias, ssm_a_log, ssm_d, ssm_norm_g, w_out, g_ffn, w_ffn_in, w_ffn_out, g_final, loss_target, m_c_ctx, m_w_mod, m_b_mod, m_g_mix, m_w_in, m_wa_sink, m_na_rpb, m_ssm_conv_w, m_ssm_conv_b, m_ssm_dt_bias, m_ssm_a_log, m_ssm_d, m_ssm_norm_g, m_w_out, m_g_ffn, m_w_ffn_in, m_w_ffn_out, m_g_final, v_c_ctx, v_w_mod, v_b_mod, v_g_mix, v_w_in, v_wa_sink, v_na_rpb, v_ssm_conv_w, v_ssm_conv_b, v_ssm_dt_bias, v_ssm_a_log, v_ssm_d, v_ssm_norm_g, v_w_out, v_g_ffn, v_w_ffn_in, v_w_ffn_out, v_g_final):
    L = x.shape[1]
    px, py, pc = _place()
    me = 4 * px + 2 * py + pc
    W = dict(c_ctx=c_ctx, w_mod=w_mod, b_mod=b_mod, g_mix=g_mix, w_in=w_in, wa_sink=wa_sink, na_rpb=na_rpb,
             ssm_conv_w=ssm_conv_w, ssm_conv_b=ssm_conv_b, ssm_dt_bias=ssm_dt_bias, ssm_a_log=ssm_a_log, ssm_d=ssm_d,
             ssm_norm_g=ssm_norm_g, w_out=w_out, g_ffn=g_ffn, w_ffn_in=w_ffn_in, w_ffn_out=w_ffn_out, g_final=g_final)
    M = dict(c_ctx=m_c_ctx, w_mod=m_w_mod, b_mod=m_b_mod, g_mix=m_g_mix, w_in=m_w_in, wa_sink=m_wa_sink,
             na_rpb=m_na_rpb, ssm_conv_w=m_ssm_conv_w, ssm_conv_b=m_ssm_conv_b, ssm_dt_bias=m_ssm_dt_bias,
             ssm_a_log=m_ssm_a_log, ssm_d=m_ssm_d, ssm_norm_g=m_ssm_norm_g, w_out=m_w_out, g_ffn=m_g_ffn,
             w_ffn_in=m_w_ffn_in, w_ffn_out=m_w_ffn_out, g_final=m_g_final)
    V = dict(c_ctx=v_c_ctx, w_mod=v_w_mod, b_mod=v_b_mod, g_mix=v_g_mix, w_in=v_w_in, wa_sink=v_wa_sink,
             na_rpb=v_na_rpb, ssm_conv_w=v_ssm_conv_w, ssm_conv_b=v_ssm_conv_b, ssm_dt_bias=v_ssm_dt_bias,
             ssm_a_log=v_ssm_a_log, ssm_d=v_ssm_d, ssm_norm_g=v_ssm_norm_g, w_out=v_w_out, g_ffn=v_g_ffn,
             w_ffn_in=v_w_ffn_in, w_ffn_out=v_w_ffn_out, g_final=v_g_final)

    tr = lambda a: a.transpose(0, 2, 1)
    shards = dict(w_in=tr(w_in).astype(MXU), w_out=w_out.astype(MXU), w_ffn_in=tr(w_ffn_in).astype(MXU),
                  w_ffn_out=w_ffn_out.astype(MXU))
    c_all, conv_all, shards["w_in_first"] = all_gather([c, ssm_conv_w, shards["w_in"][0]], "gather_first")
    conv_f = conv_all.transpose(1, 2, 0, 3).reshape(2, 7, 1024)

    scin = jnp.concatenate([c_all.reshape(NDEV, D), c_ctx.reshape(1, D), jnp.zeros((7, D), F32)], axis=0)
    bcol = lax.dynamic_slice_in_dim(b_mod, me * 768, 768, axis=1).reshape(2, 1, 768)
    mod_all, = all_gather([mod_fwd(scin, w_mod, bcol)], "gather_mod")
    mod_rows = mod_all.transpose(1, 2, 0, 3).reshape(2, 16, 6 * D)
    mods = jnp.stack([lax.dynamic_index_in_dim(mod_rows, me, axis=1, keepdims=False), mod_rows[:, 8]], axis=1)

    layers = [dict(g_mix=g_mix[i], wa_sink=wa_sink[i], na_rpb=na_rpb[i], ssm_conv_w=conv_f[i],
                   ssm_conv_b=ssm_conv_b[i], ssm_dt_bias=ssm_dt_bias[i], ssm_a_log=ssm_a_log[i], ssm_d=ssm_d[i],
                   ssm_norm_g=ssm_norm_g[i], g_ffn=g_ffn[i]) for i in range(2)]
    loss, dx, grads, dmods, dgfin, gin0 = local_step(x[0], ctx[0], loss_target[0], mods, layers, shards, g_final, L)

    stk = lambda n: jnp.stack([grads[0][n], grads[1][n]])
    small = dict(b_mod=dmods[:, 0] + dmods[:, 1], g_final=dgfin, dmod_s=dmods[:, 0], dmod_c=dmods[:, 1],
                 loss=loss.reshape(1))
    for nme in _SMALL:
        if nme not in small:
            small[nme] = stk(nme)
    shapes = [small[nme].shape for nme in _SMALL]
    zero_like = lambda nme: jnp.zeros(small[nme].shape, F32)
    own = lambda S, nme: S[nme] if (nme in S and S[nme].shape == small[nme].shape) else zero_like(nme)
    gath, = all_gather([_pack([small[nme] for nme in _SMALL])], "gather_grads")
    sm = adam_reduce(gath, _pack([own(W, nme) for nme in _SMALL]), _pack([own(M, nme) for nme in _SMALL]),
                     _pack([own(V, nme) for nme in _SMALL]), "adam_small")
    res = {nme: vals for nme, vals in zip(_SMALL, zip(*[_unpack(a, shapes) for a in sm]))}
    loss = res["loss"][0][0]

    cols = lambda a: lax.dynamic_slice_in_dim(a, me * 768, 768, axis=-1)
    rows_of = lambda s: -(-int(np.prod(s)) // 1024) * 8
    r0 = sum(rows_of(s) for s in shapes[:_SMALL.index("dmod_s")])
    dmod_s_all = gath[:, r0:r0 + rows_of(small["dmod_s"].shape)].reshape(NDEV, 2, 6 * D).transpose(1, 0, 2)
    G = jnp.concatenate([cols(dmod_s_all), cols(res["dmod_c"][0])[:, None, :], jnp.zeros((2, 7, 768), F32)], axis=1)
    dwmod, dscin = mod_bwd(scin, w_mod, G)
    cc_g, = all_gather([dscin[8].reshape(8, 128)], "gather_cctx")
    out = {}
    out["c_ctx"] = [a.reshape(D) for a in adam_reduce(cc_g, c_ctx.reshape(8, 128), m_c_ctx.reshape(8, 128),
                                                      v_c_ctx.reshape(8, 128), "adam_cctx")]
    res_wmod, (got1,) = adam_reduce(dwmod.reshape(1, 2 * D, 768), w_mod.reshape(2 * D, 768),
                                    m_w_mod.reshape(2 * D, 768), v_w_mod.reshape(2 * D, 768), "adam_wmod",
                                    sends=(gin0[1],))
    out["w_mod"] = [a.reshape(2, D, 768) for a in res_wmod]
    gconv = lax.dynamic_slice_in_dim(res["ssm_conv_w"][0], me * 128, 128, axis=2)
    out["ssm_conv_w"] = [a.reshape(2, 7, 128) for a in adam_reduce(
        gconv.reshape(1, 14, 128), ssm_conv_w.reshape(14, 128), m_ssm_conv_w.reshape(14, 128),
        v_ssm_conv_w.reshape(14, 128), "adam_conv")]
    for nme in _SMALL:
        if nme not in ("ssm_conv_w", "dmod_s", "dmod_c", "loss"):
            out[nme] = list(res[nme])

    adam_big = lambda nme, t, **kw: adam_layers(grads[0][nme], grads[1][nme], t(W[nme]), t(M[nme]), t(V[nme]),
                                                "adam_" + nme, **kw)
    same = lambda a: a
    res_fi, (got0,) = adam_big("w_ffn_in", tr, sends=(gin0[0],))
    grads[0]["w_in"] = jnp.concatenate([got0, got1], axis=2)
    out["w_ffn_in"] = [tr(a) for a in res_fi]
    out["w_ffn_out"] = list(adam_big("w_ffn_out", same))
    out["w_out"] = list(adam_big("w_out", same))
    out["w_in"] = [tr(a) for a in adam_big("w_in", tr)]
    order = ["c_ctx", "w_mod", "b_mod", "g_mix", "w_in", "wa_sink", "na_rpb", "ssm_conv_w", "ssm_conv_b",
             "ssm_dt_bias", "ssm_a_log", "ssm_d", "ssm_norm_g", "w_out", "g_ffn", "w_ffn_in", "w_ffn_out", "g_final"]
    return (loss, dx.reshape(1, L, D), *[out[nme][0] for nme in order], *[out[nme][1] for nme in order],
            *[out[nme][2] for nme in order], *[out[nme][3] for nme in order])
```

```python
import functools

import numpy as np
import jax
import jax.numpy as jnp
from jax import lax
from jax.experimental import pallas as pl
from jax.experimental.pallas import tpu as pltpu

F32 = jnp.float32
MXU = jnp.bfloat16
_INTERPRET = False
VMEM_LIMIT = 60 * 1024 * 1024

D = 1024
LC = 256
GW = 64
HD = 64
EPS = 1e-6
NEG = -1e30
NDEV = 8
Q = 128
NSTATE = 128
DFF = 2816
IN_COLS = 2832
NP_IN = 3072
C_QA, C_QB, C_Z, C_KA, C_VA, C_KB, C_VB, C_XBC, C_DT = 0, 256, 512, 1024, 1152, 1280, 1536, 1792, 2816
ADAM_LR, ADAM_B1, ADAM_B2, ADAM_EPS, ADAM_WD, ADAM_STEP = 0.001, 0.9, 0.999, 1e-08, 0.01, 10
MESH_T = pl.DeviceIdType.MESH


def _dg(a, b, ca, cb):
    return lax.dot_general(a.astype(MXU), b.astype(MXU), (((ca,), (cb,)), ((), ())), preferred_element_type=F32)


@jax.custom_vjp
def mm(a, b):
    return _dg(a, b, 1, 0)


def _mm_f(a, b):
    return _dg(a, b, 1, 0), (a, b)


def _mm_b(res, g):
    a, b = res
    return _dg(g, b, 1, 1).astype(a.dtype), _dg(a, g, 0, 0).astype(b.dtype)


mm.defvjp(_mm_f, _mm_b)


@jax.custom_vjp
def mm_nt(a, b):
    return _dg(a, b, 1, 1)


def _mmnt_f(a, b):
    return _dg(a, b, 1, 1), (a, b)


def _mmnt_b(res, g):
    a, b = res
    return _dg(g, b, 1, 0).astype(a.dtype), _dg(g, a, 0, 0).astype(b.dtype)


mm_nt.defvjp(_mmnt_f, _mmnt_b)


@jax.custom_vjp
def mm_tn(a, b):
    return _dg(a, b, 0, 0)


def _mmtn_f(a, b):
    return _dg(a, b, 0, 0), (a, b)


def _mmtn_b(res, g):
    a, b = res
    return _dg(b, g, 1, 1).astype(a.dtype), _dg(a, g, 1, 0).astype(b.dtype)


mm_tn.defvjp(_mmtn_f, _mmtn_b)


@jax.custom_vjp
def mmw(a, w):
    return _dg(a, w, 1, 0)


mmw.defvjp(lambda a, w: (_dg(a, w, 1, 0), w), lambda w, g: (_dg(g, w, 1, 1), None))


@jax.custom_vjp
def mmw_nt(a, w):
    return _dg(a, w, 1, 1)


mmw_nt.defvjp(lambda a, w: (_dg(a, w, 1, 1), w), lambda w, g: (_dg(g, w, 1, 0), None))


def _exact(a, b):
    return lax.dot_general(a, b, (((1,), (0,)), ((), ())), precision=lax.Precision.HIGHEST,
                           preferred_element_type=F32)


def _pc(body, name, out_shape, grid=None, in_specs=None, out_specs=None, scratch=(), sends=(), gather=False):
    params = pltpu.CompilerParams(vmem_limit_bytes=VMEM_LIMIT)
    if sends and not isinstance(out_shape, (list, tuple)):
        out_shape, out_specs = [out_shape], [out_specs]
    start, wait = (_ag_start, _ag_wait) if gather else (_a2a_start, _a2a_wait)
    if not sends:
        kw = {}
        if grid is not None:
            kw = dict(grid=grid, in_specs=in_specs, out_specs=out_specs)
        elif in_specs is not None:
            kw = dict(in_specs=in_specs, out_specs=out_specs)
        return pl.pallas_call(body, name=name, out_shape=out_shape, scratch_shapes=list(scratch),
                              compiler_params=params, interpret=_INTERPRET, **kw)
    n, nin, nout, nscr = len(sends), len(in_specs), len(out_shape), len(scratch)

    def body2(*refs):
        cin, xs = refs[:nin], refs[nin:nin + n]
        couts, os_ = refs[nin + n:nin + n + nout], refs[nin + n + nout:nin + 2 * n + nout]
        cscr, sems = refs[nin + 2 * n + nout:nin + 2 * n + nout + nscr], refs[nin + 2 * n + nout + nscr:]
        ids = [pl.program_id(a) for a in range(len(grid))]
        first = functools.reduce(lambda a, b: a & b, [i == 0 for i in ids])
        last = functools.reduce(lambda a, b: a & b, [i == g - 1 for i, g in zip(ids, grid)])

        @pl.when(first)
        def _():
            start(xs, os_, *sems)

        body(*cin, *couts, *cscr)

        @pl.when(last)
        def _():
            wait(xs, os_, *sems)

    call = pl.pallas_call(
        body2, name=name,
        out_shape=list(out_shape) + [_sds(((NDEV,) if gather else ()) + a.shape, a.dtype) for a in sends],
        grid=grid, in_specs=list(in_specs) + [_any()] * n, out_specs=list(out_specs) + [_any()] * n,
        scratch_shapes=list(scratch) + _a2a_sems(n), compiler_params=params, interpret=_INTERPRET)

    def run(*args):
        res = call(*args, *sends)
        return res[:nout], res[nout:]

    return run


def _vm():
    return pl.BlockSpec(memory_space=pltpu.VMEM)


def _sds(shape, dt=F32):
    return jax.ShapeDtypeStruct(shape, dt)


def _iota(shape, dim):
    return lax.broadcasted_iota(jnp.int32, shape, dim)


def _silu(x):
    return x * jax.nn.sigmoid(x)


def _softplus(x):
    return jnp.maximum(x, 0.0) + jnp.log1p(jnp.exp(-jnp.abs(x)))


def _normmod(x, g, sh, sc):
    r = lax.rsqrt(jnp.mean(x * x, axis=-1, keepdims=True) + EPS)
    return (x * r * g) * (1.0 + sc) + sh


def _rope(x, cos, sin, rm):
    return x * cos + _exact(x, rm) * sin


def _swap12(x):
    lane = _iota(x.shape, 1)
    up, down = pltpu.roll(x, 192, 1), pltpu.roll(x, 64, 1)
    return jnp.where((lane >= 64) & (lane < 128), up, jnp.where((lane >= 128) & (lane < 192), down, x))


def _skip_ctx_tile(skip, nlt, compute, outs):
    if not skip:
        compute()
        return
    i = pl.program_id(0)
    pl.when(i < nlt)(compute)

    @pl.when(i >= nlt)
    def _():
        for r in outs:
            r[...] = jnp.zeros_like(r)


def _acc_init(first, refs):
    @pl.when(first)
    def _():
        for r in refs:
            r[...] = jnp.zeros_like(r)


def _stream(X, TR, nlt):
    if not isinstance(X, tuple):
        return (X,), [pl.BlockSpec((TR, D), lambda i: (i, 0))], lambda refs: refs[0][...]
    specs = [pl.BlockSpec((TR, D), lambda i: (jnp.minimum(i, nlt - 1), 0)), pl.BlockSpec((TR, D), lambda i: (0, 0))]
    return X, specs, lambda refs: jnp.where(pl.program_id(0) < nlt, refs[0][...], refs[1][...])


def in_fwd(X, g, sh, sc, W, cos, sin, rm, L, sends=()):
    T = L + LC
    TR = 256
    nlt = L // TR
    xs, xspecs, xread = _stream(X, TR, nlt)

    def body(*refs):
        (g_ref, sh_ref, sc_ref, w_ref, cos_ref, sin_ref, rm_ref,
         qa, qb, z, ka, va, kb, vb, xbc, dt, hout) = refs[len(xs):]
        h = _normmod(xread(refs), g_ref[...], sh_ref[0], sc_ref[0]).astype(MXU)
        hout[...] = h
        y = lax.dot_general(h, w_ref[...], (((1,), (1,)), ((), ())), preferred_element_type=F32)
        cs, sn, r = cos_ref[...], sin_ref[...], rm_ref[...]
        qa[...] = _rope(_swap12(y[:, C_QA:C_QB]), cs, sn, r).astype(MXU)
        qb[...] = y[:, C_QB:C_Z].astype(MXU)
        z[...] = y[:, C_Z:C_KA]
        ka[...] = _rope(y[:, C_KA:C_VA], cs[:, :128], sn[:, :128], r[:128, :128]).astype(MXU)
        va[...] = y[:, C_VA:C_KB].astype(MXU)
        kb[...] = y[:, C_KB:C_VB].astype(MXU)
        vb[...] = y[:, C_VB:C_XBC].astype(MXU)
        xbc[...] = y[:, C_XBC:C_DT]
        dt[...] = y[:, C_DT:C_DT + 128]

    row = lambda w: pl.BlockSpec((TR, w), lambda i: (i, 0))
    cls = pl.BlockSpec((1, 1, D), lambda i: (i // nlt, 0, 0))
    widths = [(256, MXU), (256, MXU), (512, F32), (128, MXU), (128, MXU), (256, MXU), (256, MXU), (1024, F32),
              (128, F32), (D, MXU)]
    return _pc(body, "in_fwd", [_sds((T, w), d) for w, d in widths], grid=(T // TR,),
               in_specs=xspecs + [pl.BlockSpec((1, D), lambda i: (0, 0)), cls, cls, _vm(), row(256), row(256), _vm()],
               out_specs=[row(w) for w, _ in widths], sends=sends, gather=True)(*xs, g, sh, sc, W, cos, sin, rm)


def in_bwd(X, g, sh, sc, W, cos, sin, rm, dxres, dqa, dqb, dz, dka, dva, dkb, dvb, dxbc, ddt2, L, latent_only):
    T = L + LC
    TR = 256
    nlt = L // TR
    xs, xspecs, xread = _stream(X, TR, nlt)

    def body(*refs):
        (g_ref, sh_ref, sc_ref, w_ref, cos_ref, sin_ref, rm_ref, dxres_ref, dqa_r, dqb_r, dz_r, dka_r,
         dva_r, dkb_r, dvb_r, dxbc_r, ddt0_r, ddt1_r, dx_o, dy_o, dg_o, dsh_o, dsc_o) = refs[len(xs):]
        i = pl.program_id(0)
        cs, sn, r = cos_ref[...], sin_ref[...], rm_ref[...]
        _, vq = jax.vjp(lambda t: _rope(t, cs, sn, r), dqa_r[...])
        _, vk = jax.vjp(lambda t: _rope(t, cs[:, :128], sn[:, :128], r[:128, :128]), dka_r[...])
        dyqa = _swap12(vq(dqa_r[...])[0])
        dyka, = vk(dka_r[...])
        ddt = ddt0_r[0] + ddt1_r[0]
        dy = jnp.concatenate([dyqa, dqb_r[...], dz_r[...], dyka, dva_r[...], dkb_r[...], dvb_r[...], dxbc_r[...],
                              ddt, jnp.zeros((TR, NP_IN - C_DT - 128), F32)], axis=1).astype(MXU)
        dy_o[...] = dy
        dh = jnp.dot(dy, w_ref[...], preferred_element_type=F32)
        _, vp = jax.vjp(_normmod, xread(refs), g_ref[...], sh_ref[0], sc_ref[0])
        dx, dg, dsh, dsc = vp(dh)
        if latent_only:
            @pl.when(i < nlt)
            def _():
                dx_o[...] = dx + dxres_ref[...]
        else:
            dx_o[...] = dx + dxres_ref[...]
        _acc_init(i == 0, [dg_o])
        _acc_init((i == 0) | (i == nlt), [dsh_o, dsc_o])
        dg_o[...] += dg
        dsh_o[0] += dsh
        dsc_o[0] += dsc

    row = lambda w: pl.BlockSpec((TR, w), lambda i: (i, 0))
    cls = pl.BlockSpec((1, 1, D), lambda i: (i // nlt, 0, 0))
    vec = pl.BlockSpec((1, D), lambda i: (0, 0))
    dts = lambda d: pl.BlockSpec((1, TR, 128), lambda i: (d, i, 0))
    dxs = pl.BlockSpec((TR, D), lambda i: (jnp.minimum(i, nlt - 1), 0)) if latent_only else row(D)
    return _pc(body, "in_bwd",
               [_sds((L if latent_only else T, D)), _sds((T, NP_IN), MXU), _sds((1, D)), _sds((2, 1, D)),
                _sds((2, 1, D))],
               grid=(T // TR,),
               in_specs=xspecs + [vec, cls, cls, _vm(), row(256), row(256), _vm(), row(D), row(256), row(256),
                                  row(512), row(128), row(128), row(256), row(256), row(1024), dts(0), dts(1)],
               out_specs=[dxs, row(NP_IN), vec, cls, cls])(
        *xs, g, sh, sc, W, cos, sin, rm, dxres, dqa, dqb, dz, dka, dva, dkb, dvb, dxbc, ddt2, ddt2)


def tn_mm(A, G, bk, bn, out_dtype, ncol=None, col0=0, rows=None):
    T, K = A.shape
    T = T if rows is None else rows
    N = G.shape[1] if ncol is None else ncol
    first = col0 * (N // bn)
    bt = T
    nt = T // bt

    def body(a_ref, g_ref, o_ref, acc):
        t = pl.program_id(2)
        _acc_init(t == 0, [acc])
        acc[...] += lax.dot_general(a_ref[...], g_ref[...], (((0,), (0,)), ((), ())), preferred_element_type=F32)

        @pl.when(t == nt - 1)
        def _():
            o_ref[...] = acc[...].astype(out_dtype)

    return _pc(body, "tn_mm", _sds((K, N), out_dtype), grid=(K // bk, N // bn, nt),
               in_specs=[pl.BlockSpec((bt, bk), lambda k, n, t: (t, k)),
                         pl.BlockSpec((bt, bn), lambda k, n, t: (t, first + n))],
               out_specs=pl.BlockSpec((bk, bn), lambda k, n, t: (k, n)),
               scratch=[pltpu.VMEM((bk, bn), F32)])(A, G)


def _ssm_out(yf, yb, xs, z, dsk, gs):
    y = (yf + yb + dsk * xs) * _silu(z)
    r = lax.rsqrt(jnp.mean(y * y, axis=-1, keepdims=True) + EPS)
    return y * r * gs


def out_fwd(oa, ob, y2, act, z, dsk, gs, W, X, gate, L, sends=()):
    T = L + LC
    TR = 256
    nlt = L // TR
    xs, xspecs, xread = _stream(X, TR, nlt)

    def body(*refs):
        oa_r, ob_r, yf_r, yb_r, xs_r, z_r, dsk_r, gs_r, w_ref, gt_ref, x1_o, cat_o = refs[len(xs):]
        oc = _ssm_out(yf_r[0], yb_r[0], xs_r[...], z_r[...], dsk_r[...], gs_r[...])
        cat = jnp.concatenate([_swap12(oa_r[...]), ob_r[...], oc], axis=1).astype(MXU)
        cat_o[...] = cat
        x1_o[...] = xread(refs) + gt_ref[0] * jnp.dot(cat, w_ref[...], preferred_element_type=F32)

    row = lambda w: pl.BlockSpec((TR, w), lambda i: (i, 0))
    ys = lambda d: pl.BlockSpec((1, TR, 512), lambda i: (d, i, 0))
    cls = pl.BlockSpec((1, 1, D), lambda i: (i // nlt, 0, 0))
    v512 = pl.BlockSpec((1, 512), lambda i: (0, 0))
    return _pc(body, "out_fwd", [_sds((T, D)), _sds((T, D), MXU)], grid=(T // TR,),
               in_specs=xspecs + [row(256), row(256), ys(0), ys(1), row(512), row(512), v512, v512, _vm(), cls],
               out_specs=[row(D), row(D)], sends=sends, gather=True)(*xs, oa, ob, y2, y2, act, z, dsk, gs, W, gate)


def out_bwd(oa, ob, y2, act, z, dsk, gs, W, gate, dX1, L):
    T = dX1.shape[0]
    TR = 256
    nlt = L // TR

    def body(oa_r, ob_r, yf_r, yb_r, xs_r, z_r, dsk_r, gs_r, w_ref, gt_ref, dx1_r,
             doa_o, dob_o, dy_o, dxs_o, dz_o, dmix_o, ddsk_o, dgs_o, dgt_o):
        i = pl.program_id(0)
        w = w_ref[...]

        def f(oa_, ob_, yf, yb, xs, z_, dsk_, gs_, gt):
            oc = _ssm_out(yf, yb, xs, z_, dsk_, gs_)
            return gt * mmw(jnp.concatenate([oa_, ob_, oc], axis=1), w)

        _, vjp = jax.vjp(f, _swap12(oa_r[...]), ob_r[...], yf_r[0], yb_r[0], xs_r[...], z_r[...], dsk_r[...],
                         gs_r[...], gt_ref[0])
        dx1 = dx1_r[...]
        doa, dob, dyf, _, dxs, dz, ddsk, dgs, dgt = vjp(dx1)
        doa_o[...] = _swap12(doa)
        dob_o[...] = dob
        dy_o[...] = dyf
        dxs_o[...] = dxs
        dz_o[...] = dz
        dmix_o[...] = (gt_ref[0] * dx1).astype(MXU)
        _acc_init(i == 0, [ddsk_o, dgs_o])
        _acc_init((i == 0) | (i == nlt), [dgt_o])
        ddsk_o[...] += ddsk
        dgs_o[...] += dgs
        dgt_o[0] += dgt

    row = lambda w: pl.BlockSpec((TR, w), lambda i: (i, 0))
    ys = lambda d: pl.BlockSpec((1, TR, 512), lambda i: (d, i, 0))
    cls = pl.BlockSpec((1, 1, D), lambda i: (i // nlt, 0, 0))
    v512 = pl.BlockSpec((1, 512), lambda i: (0, 0))
    return _pc(body, "out_bwd",
               [_sds((T, 256)), _sds((T, 256)), _sds((T, 512)), _sds((T, 512)), _sds((T, 512)), _sds((T, D), MXU),
                _sds((1, 512)), _sds((1, 512)), _sds((2, 1, D))],
               grid=(T // TR,),
               in_specs=[row(256), row(256), ys(0), ys(1), row(512), row(512), v512, v512, _vm(), cls, row(D)],
               out_specs=[row(256), row(256), row(512), row(512), row(512), row(D), v512, v512, cls])(
        oa, ob, y2, y2, act, z, dsk, gs, W, gate, dX1)


def ffn_fwd(X, g, sh, sc, gate, Win, Wout, L, sends=(), skip_ctx=False):
    T = X.shape[0]
    TR = 256
    nlt = L // TR

    def body(x_ref, g_ref, sh_ref, sc_ref, gt_ref, wi_ref, wo_ref, o_ref, f_ref):
        def compute():
            h = _normmod(x_ref[...], g_ref[...], sh_ref[0], sc_ref[0]).astype(MXU)
            nt = (((1,), (1,)), ((), ()))
            a = lax.dot_general(h, wi_ref[0:DFF, :], nt, preferred_element_type=F32)
            u = lax.dot_general(h, wi_ref[DFF:2 * DFF, :], nt, preferred_element_type=F32)
            act = (_silu(a) * u).astype(MXU)
            ff = jnp.dot(act, wo_ref[...], preferred_element_type=F32)
            f_ref[...] = ff
            o_ref[...] = x_ref[...] + gt_ref[0] * ff

        _skip_ctx_tile(skip_ctx, nlt, compute, [o_ref, f_ref])

    row = lambda w: pl.BlockSpec((TR, w), lambda i: (i, 0))
    cls = pl.BlockSpec((1, 1, D), lambda i: (i // nlt, 0, 0))
    vec = pl.BlockSpec((1, D), lambda i: (0, 0))
    return _pc(body, "ffn_fwd", [_sds((T, D)), _sds((T, D))], grid=(T // TR,),
               in_specs=[row(D), vec, cls, cls, cls, _vm(), _vm()], out_specs=[row(D), row(D)], sends=sends,
               gather=True)(X, g, sh, sc, gate, Win, Wout)


def ffn_bwd(X, g, sh, sc, gate, Win, Wout, FF, dX2, L, sends=(), nchunk=2, skip_ctx=False):
    T = X.shape[0]
    TR = 256
    nlt = L // TR
    CH = DFF // nchunk

    def body(x_ref, g_ref, sh_ref, sc_ref, gt_ref, wi_ref, wo_ref, ff_r, dx2_r,
             dx_o, h_o, du_o, act_o, dout_o, dg_o, dsh_o, dsc_o, dgt_o):
        i = pl.program_id(0)
        _acc_init(i == 0, [dg_o])
        _acc_init((i == 0) | (i == nlt), [dsh_o, dsc_o, dgt_o])

        def compute():
            h, vp = jax.vjp(_normmod, x_ref[...], g_ref[...], sh_ref[0], sc_ref[0])
            dx2 = dx2_r[...]
            dout = gt_ref[0] * dx2
            zero = jnp.zeros((TR, CH), F32)
            dh = jnp.zeros((TR, D), F32)
            for c in range(nchunk):
                lo, hi = c * CH, (c + 1) * CH
                wg, wu, wo = wi_ref[lo:hi, :], wi_ref[DFF + lo:DFF + hi, :], wo_ref[lo:hi, :]

                def f(h_, eg, eu):
                    act = _silu(mmw_nt(h_, wg) + eg) * (mmw_nt(h_, wu) + eu)
                    return mmw(act, wo), act

                _, vjp_c, act = jax.vjp(f, h, zero, zero, has_aux=True)
                dh_c, da, du = vjp_c(dout)
                dh = dh + dh_c
                du_o[:, lo:hi] = da.astype(MXU)
                du_o[:, DFF + lo:DFF + hi] = du.astype(MXU)
                act_o[:, lo:hi] = act.astype(MXU)
            dx, dg, dsh, dsc = vp(dh)
            dx_o[...] = dx + dx2
            h_o[...] = h.astype(MXU)
            dout_o[...] = dout.astype(MXU)
            dg_o[...] += dg
            dsh_o[0] += dsh
            dsc_o[0] += dsc
            dgt_o[0] += jnp.sum(dx2 * ff_r[...], axis=0, keepdims=True)

        _skip_ctx_tile(skip_ctx, nlt, compute, [dx_o, h_o, du_o, act_o, dout_o])

    row = lambda w: pl.BlockSpec((TR, w), lambda i: (i, 0))
    cls = pl.BlockSpec((1, 1, D), lambda i: (i // nlt, 0, 0))
    vec = pl.BlockSpec((1, D), lambda i: (0, 0))
    return _pc(body, "ffn_bwd",
               [_sds((T, D)), _sds((T, D), MXU), _sds((T, 2 * DFF), MXU), _sds((T, DFF), MXU), _sds((T, D), MXU),
                _sds((1, D)), _sds((2, 1, D)), _sds((2, 1, D)), _sds((2, 1, D))],
               grid=(T // TR,),
               in_specs=[row(D), vec, cls, cls, cls, _vm(), _vm(), row(D), row(D)],
               out_specs=[row(D), row(D), row(2 * DFF), row(DFF), row(D), vec, cls, cls, cls], sends=sends)(
        X, g, sh, sc, gate, Win, Wout, FF, dX2)


def loss_head(X2, g, tgt, L):
    T = X2.shape[0]
    TR = 256
    nlt = L // TR

    def body(x_ref, g_ref, t_ref, loss_o, dx_o, dg_o):
        i = pl.program_id(0)
        _acc_init(i == 0, [loss_o, dg_o])

        @pl.when(i < nlt)
        def _():
            def f(x, g_):
                y = x * lax.rsqrt(jnp.mean(x * x, axis=-1, keepdims=True) + EPS) * g_
                return 0.5 * jnp.sum(jnp.mean(jnp.square(y - t_ref[...]), axis=-1, keepdims=True), axis=0,
                                     keepdims=True)

            val, vjp = jax.vjp(f, x_ref[...], g_ref[...])
            dx, dg = vjp(jnp.ones((1, 1), F32))
            dx_o[...] = dx
            loss_o[...] += jnp.broadcast_to(val, (8, 128))
            dg_o[...] += dg

        @pl.when(i >= nlt)
        def _():
            dx_o[...] = jnp.zeros_like(dx_o)

    row = pl.BlockSpec((TR, D), lambda i: (i, 0))
    vec = pl.BlockSpec((1, D), lambda i: (0, 0))
    return _pc(body, "loss_head", [_sds((8, 128)), _sds((T, D)), _sds((1, D))], grid=(T // TR,),
               in_specs=[row, vec, pl.BlockSpec((TR, D), lambda i: (jnp.minimum(i, nlt - 1), 0))],
               out_specs=[pl.BlockSpec((8, 128), lambda i: (0, 0)), row, vec])(X2, g, tgt)


def _stack_impl(q):
    lane = _iota(q.shape, 1)
    return jnp.concatenate([jnp.where(lane < HD, q, 0.0), jnp.where(lane >= HD, q, 0.0)], axis=0)


def _unstack_impl(o):
    M = o.shape[0] // 2
    return jnp.where(_iota((M, o.shape[1]), 1) < HD, o[:M], o[M:])


@jax.custom_vjp
def _stack(q):
    return _stack_impl(q)


_stack.defvjp(lambda q: (_stack_impl(q), None), lambda _, g: (_unstack_impl(g),))


@jax.custom_vjp
def _unstack(o):
    return _unstack_impl(o)


_unstack.defvjp(lambda o: (_unstack_impl(o), None), lambda _, g: (_stack_impl(g),))


def _softmax_av(q, ks, vs, biases, sink):
    q2 = _stack(q)
    ss = []
    for k, b in zip(ks, biases):
        s = mm_nt(q2, k) * (HD ** -0.5)
        ss.append(s if b is None else s + b)
    m = functools.reduce(jnp.maximum, [jnp.max(s, axis=1, keepdims=True) for s in ss])
    if sink is not None:
        m = jnp.maximum(m, sink)
    m = lax.stop_gradient(m)
    es = [jnp.exp(s - m) for s in ss]
    den = functools.reduce(lambda a, b_: a + b_, [jnp.sum(e, axis=1, keepdims=True) for e in es])
    if sink is not None:
        den = den + jnp.exp(sink - m)
    inv = 1.0 / den
    return _unstack(functools.reduce(lambda a, b_: a + b_, [mm(e * inv, v) for e, v in zip(es, vs)]))


def _sink_col(s0, s1, M):
    return jnp.concatenate([jnp.broadcast_to(jnp.mean(s0, axis=1, keepdims=True), (M, 1)),
                            jnp.broadcast_to(jnp.mean(s1, axis=1, keepdims=True), (M, 1))], axis=0)


def _stack4_impl(q):
    lane = _iota((q.shape[0], 128), 1)
    parts = []
    for p in range(2):
        qp = q[:, 128 * p:128 * (p + 1)]
        parts += [jnp.where(lane < HD, qp, 0.0), jnp.where(lane >= HD, qp, 0.0)]
    return jnp.concatenate(parts, axis=0)


def _unstack4_impl(o):
    M = o.shape[0] // 4
    lane = _iota((M, 128), 1)
    return jnp.concatenate([jnp.where(lane < HD, o[0:M], o[M:2 * M]),
                            jnp.where(lane < HD, o[2 * M:3 * M], o[3 * M:4 * M])], axis=1)


@jax.custom_vjp
def _stack4(q):
    return _stack4_impl(q)


_stack4.defvjp(lambda q: (_stack4_impl(q), None), lambda _, g: (_unstack4_impl(g),))


@jax.custom_vjp
def _unstack4(o):
    return _unstack4_impl(o)


_unstack4.defvjp(lambda o: (_unstack4_impl(o), None), lambda _, g: (_stack4_impl(g),))


WA_NB = 4


def _wa_blocks(qs, kws, vws, kx, vx, sks, n0, L):
    sc = HD ** -0.5
    sink = jnp.concatenate([jnp.broadcast_to(jnp.mean(s_, axis=1, keepdims=True), (Q, 1)) for s_ in sks], axis=0)
    bias = []
    for b_ in range(len(qs)):
        n = n0 + b_
        qpos = n * Q + (_iota((4 * Q, 3 * Q), 0) & (Q - 1))
        kpos = (n - 1) * Q + _iota((4 * Q, 3 * Q), 1)
        bias.append(jnp.where((jnp.abs(qpos - kpos) <= Q) & (kpos >= 0) & (kpos < L), 0.0, NEG))
    q4 = [_stack4(q) for q in qs]
    sl = [mm_nt(a, k) * sc + b_ for a, k, b_ in zip(q4, kws, bias)]
    sx = [mm_nt(a, kx) * sc for a in q4]
    m = [lax.stop_gradient(jnp.maximum(jnp.maximum(jnp.max(a, axis=1, keepdims=True),
                                                   jnp.max(b_, axis=1, keepdims=True)), sink))
         for a, b_ in zip(sl, sx)]
    el = [jnp.exp(a - c) for a, c in zip(sl, m)]
    ex = [jnp.exp(a - c) for a, c in zip(sx, m)]
    inv = [1.0 / (jnp.sum(a, axis=1, keepdims=True) + jnp.sum(b_, axis=1, keepdims=True) + jnp.exp(sink - c))
           for a, b_, c in zip(el, ex, m)]
    return [_unstack4(mm(a * i, v) + mm(b_ * i, vx)) for a, b_, i, v in zip(el, ex, inv, vws)]


def _wa_load(q_r, k_r, v_r, n0):
    f = lambda t: t.astype(F32)
    qs = [f(q_r[b_ * Q:(b_ + 1) * Q, :]) for b_ in range(WA_NB)]
    wins = [pl.ds(pl.multiple_of((n0 + b_) * Q, Q), 3 * Q) for b_ in range(WA_NB)]
    return qs, [f(k_r[w, :]) for w in wins], [f(v_r[w, :]) for w in wins], wins


def _wa_specs(L):
    nb = L // Q
    qs = pl.BlockSpec((WA_NB * Q, 256), lambda n: (n, 0))
    kfull = pl.BlockSpec((L + LC + Q, 128), lambda n: (0, 0))
    sks = pl.BlockSpec((2, 2, 1, 128), lambda n: (0, 0, 0, 0))
    return nb, qs, kfull, sks


def wa_fwd(QA, KA, VA, sinkp, L, sends=()):
    nb, qs, kfull, sks = _wa_specs(L)
    pad = lambda a: jnp.concatenate([jnp.zeros((Q, 128), a.dtype), a], axis=0)

    def body(q_r, k_r, v_r, sk_r, o_ref):
        n0 = pl.program_id(0) * WA_NB
        qs_, kws, vws, _ = _wa_load(q_r, k_r, v_r, n0)
        cx = pl.ds(Q + L, LC)
        outs = _wa_blocks(qs_, kws, vws, k_r[cx, :].astype(F32), v_r[cx, :].astype(F32),
                          [sk_r[0, 0], sk_r[0, 1], sk_r[1, 0], sk_r[1, 1]], n0, L)
        o_ref[...] = jnp.concatenate(outs, axis=0)

    return _pc(body, "wa_fwd", _sds((L, 256)), grid=(nb // WA_NB,), in_specs=[qs, kfull, kfull, sks], out_specs=qs,
               sends=sends, gather=True)(QA, pad(KA), pad(VA), sinkp)


def wa_bwd(QA, KA, VA, sinkp, dO, L, sends=()):
    nb, qs, kfull, sks = _wa_specs(L)
    pad = lambda a: jnp.concatenate([jnp.zeros((Q, 128), a.dtype), a], axis=0)

    def body(q_r, k_r, v_r, sk_r, do_r, dq_o, dk_o, dv_o, dsk_o):
        n0 = pl.program_id(0) * WA_NB
        _acc_init(n0 == 0, [dk_o, dv_o, dsk_o])
        qs_, kws, vws, wins = _wa_load(q_r, k_r, v_r, n0)
        cx = pl.ds(Q + L, LC)
        fn = lambda a, b, c, d, e, s_: _wa_blocks(a, b, c, d, e, s_, n0, L)
        _, vjp = jax.vjp(fn, qs_, kws, vws, k_r[cx, :].astype(F32), v_r[cx, :].astype(F32),
                         [sk_r[0, 0], sk_r[0, 1], sk_r[1, 0], sk_r[1, 1]])
        dqs, dkws, dvws, dkx, dvx, ds = vjp([do_r[b_ * Q:(b_ + 1) * Q, :] for b_ in range(WA_NB)])
        dq_o[...] = jnp.concatenate(dqs, axis=0)
        for w, dk, dv in zip(wins, dkws, dvws):
            dk_o[w, :] += dk
            dv_o[w, :] += dv
        dk_o[cx, :] += dkx
        dv_o[cx, :] += dvx
        for i_ in range(4):
            dsk_o[i_ // 2, i_ % 2] += ds[i_]

    return _pc(body, "wa_bwd", [_sds((L, 256)), _sds((L + LC + Q, 128)), _sds((L + LC + Q, 128)),
                                _sds((2, 2, 1, 128))],
               grid=(nb // WA_NB,), in_specs=[qs, kfull, kfull, sks, qs], out_specs=[qs, kfull, kfull, sks],
               sends=sends)(QA, pad(KA), pad(VA), sinkp, dO)


def _ctx_block(q, kx, vx, s0, s1):
    return _softmax_av(q, [kx], [vx], [None], _sink_col(s0, s1, LC))


def ctx_fwd(Qx, Kx, Vx, sinkp, shared, L):
    cq = pl.BlockSpec((LC, 128), lambda p: (L // LC, p))
    ck = pl.BlockSpec((LC, 128), lambda p: (L // LC, 0 if shared else p))
    sks = pl.BlockSpec((1, 2, 1, 128), lambda p: (p, 0, 0, 0))

    def body(q_r, k_r, v_r, sk_r, o_ref):
        f = lambda t: t[...].astype(F32)
        o_ref[...] = _ctx_block(f(q_r), f(k_r), f(v_r), sk_r[0, 0], sk_r[0, 1])

    return _pc(body, "ctx_fwd", _sds((LC, 256)), grid=(2,), in_specs=[cq, ck, ck, sks],
               out_specs=pl.BlockSpec((LC, 128), lambda p: (0, p)))(Qx, Kx, Vx, sinkp)


def ctx_bwd(Qx, Kx, Vx, sinkp, dO, shared, L):
    cq = pl.BlockSpec((LC, 128), lambda p: (L // LC, p))
    ck = pl.BlockSpec((LC, 128), lambda p: (L // LC, 0 if shared else p))
    sks = pl.BlockSpec((1, 2, 1, 128), lambda p: (p, 0, 0, 0))
    op = pl.BlockSpec((LC, 128), lambda p: (0, p))
    ok = pl.BlockSpec((LC, 128), lambda p: (0, 0 if shared else p))
    dos = pl.BlockSpec((LC, 128), lambda p: (L // LC, p))

    def body(q_r, k_r, v_r, sk_r, do_r, dq_o, dk_o, dv_o, dsk_o):
        p = pl.program_id(0)
        f = lambda t: t[...].astype(F32)
        _, vjp = jax.vjp(_ctx_block, f(q_r), f(k_r), f(v_r), sk_r[0, 0], sk_r[0, 1])
        dq, dk, dv, ds0, ds1 = vjp(do_r[...])
        dq_o[...] = dq
        _acc_init((p == 0) if shared else (p >= 0), [dk_o, dv_o])
        dk_o[...] += dk
        dv_o[...] += dv
        dsk_o[0, 0] = ds0
        dsk_o[0, 1] = ds1

    kw = 128 if shared else 256
    return _pc(body, "ctx_bwd", [_sds((LC, 256)), _sds((LC, kw)), _sds((LC, kw)), _sds((2, 2, 1, 128))],
               grid=(2,), in_specs=[cq, ck, ck, sks, dos], out_specs=[op, ok, ok, sks])(Qx, Kx, Vx, sinkp, dO)


def _na_rows(qs, kws, vws, kx, vx, bs):
    sc = HD ** -0.5
    q2 = [_stack(q) for q in qs]
    sl = [mm_nt(a, k) * sc + b for a, k, b in zip(q2, kws, bs)]
    sx = [mm_nt(a, kx) * sc for a in q2]
    m = [lax.stop_gradient(jnp.maximum(jnp.max(a, axis=1, keepdims=True), jnp.max(b, axis=1, keepdims=True)))
         for a, b in zip(sl, sx)]
    el = [jnp.exp(a - c) for a, c in zip(sl, m)]
    ex = [jnp.exp(a - c) for a, c in zip(sx, m)]
    inv = [1.0 / (jnp.sum(a, axis=1, keepdims=True) + jnp.sum(b, axis=1, keepdims=True)) for a, b in zip(el, ex)]
    o2 = [mm(a * i, v) + mm(b * i, vx) for a, b, i, v in zip(el, ex, inv, vws)]
    return [_unstack(o) for o in o2]


NA_ROWS = 16


def _na_geom(r, R):
    s = jnp.clip(r - 4, 0, R - 8)
    cls = jnp.where(r < 4, r, jnp.where(r > R - 4, r - (R - 8), 4))
    return pl.ds(pl.multiple_of(s * GW, GW), 8 * GW), cls


def _na_load(q_r, k_r, v_r, b_r, rb, R):
    nr = min(NA_ROWS, R)
    geo = [_na_geom(rb * nr + j, R) for j in range(nr)]
    qs = [q_r[j * GW:(j + 1) * GW, :].astype(F32) for j in range(nr)]
    kws = [k_r[win, :].astype(F32) for win, _ in geo]
    vws = [v_r[win, :].astype(F32) for win, _ in geo]
    bs = [jnp.concatenate([b_r[0, cls], b_r[1, cls]], axis=0) for _, cls in geo]
    return geo, qs, kws, vws, bs


def na_fwd(QB, KB, VB, biasd, L, sends=()):
    R = L // GW
    nr = min(NA_ROWS, R)
    qs = pl.BlockSpec((nr * GW, 128), lambda p, rb: (rb, p))
    kfull = pl.BlockSpec((L, 128), lambda p, rb: (0, p))
    kctx = pl.BlockSpec((LC, 128), lambda p, rb: (L // LC, p))
    bs = pl.BlockSpec((2, 8, GW, 8 * GW), lambda p, rb: (p, 0, 0, 0))

    def body(q_r, k_r, v_r, kx_r, vx_r, b_r, o_ref):
        _, qs_, kws, vws, bs_ = _na_load(q_r, k_r, v_r, b_r, pl.program_id(1), R)
        outs = _na_rows(qs_, kws, vws, kx_r[...].astype(F32), vx_r[...].astype(F32), bs_)
        o_ref[...] = jnp.concatenate(outs, axis=0)

    return _pc(body, "na_fwd", _sds((L, 256)), grid=(2, R // nr), in_specs=[qs, kfull, kfull, kctx, kctx, bs],
               out_specs=qs, sends=sends, gather=True)(QB, KB, VB, KB, VB, biasd)


def na_bwd(QB, KB, VB, biasd, dO, L):
    R = L // GW
    nr = min(NA_ROWS, R)
    qs = pl.BlockSpec((nr * GW, 128), lambda p, rb: (rb, p))
    kfull = pl.BlockSpec((L, 128), lambda p, rb: (0, p))
    kctx = pl.BlockSpec((LC, 128), lambda p, rb: (L // LC, p))
    bs = pl.BlockSpec((2, 8, GW, 8 * GW), lambda p, rb: (p, 0, 0, 0))
    oc = pl.BlockSpec((LC, 128), lambda p, rb: (0, p))

    def body(q_r, k_r, v_r, kx_r, vx_r, b_r, do_r, dq_o, dk_o, dv_o, dkx_o, dvx_o, db_o):
        rb = pl.program_id(1)
        _acc_init(rb == 0, [dk_o, dv_o, dkx_o, dvx_o, db_o])
        geo, qs_, kws, vws, bs_ = _na_load(q_r, k_r, v_r, b_r, rb, R)
        _, vjp = jax.vjp(_na_rows, qs_, kws, vws, kx_r[...].astype(F32), vx_r[...].astype(F32), bs_)
        dqs, dkws, dvws, dkx, dvx, dbs = vjp([do_r[j * GW:(j + 1) * GW, :] for j in range(nr)])
        dq_o[...] = jnp.concatenate(dqs, axis=0)
        dkx_o[...] += dkx
        dvx_o[...] += dvx
        for j, (win, cls) in enumerate(geo):
            dk_o[win, :] += dkws[j]
            dv_o[win, :] += dvws[j]
            db_o[0, cls] += dbs[j][:GW]
            db_o[1, cls] += dbs[j][GW:]

    return _pc(body, "na_bwd",
               [_sds((L, 256)), _sds((L, 256)), _sds((L, 256)), _sds((LC, 256)), _sds((LC, 256)),
                _sds((4, 8, GW, 8 * GW))],
               grid=(2, R // nr), in_specs=[qs, kfull, kfull, kctx, kctx, bs, qs],
               out_specs=[qs, kfull, kfull, oc, oc, bs])(QB, KB, VB, KB, VB, biasd, dO)


def exact_mm_call(A, B):
    def body(a_ref, b_ref, o_ref):
        o_ref[...] = _exact(a_ref[...], b_ref[...])

    return _pc(body, "exact_mm", _sds((A.shape[0], B.shape[1])))(A, B)


def _conv_shift(x, d, L):
    T = x.shape[0]
    if d == 0:
        return x
    t = _iota(x.shape, 0)
    src = t + d
    ok = (src >= 0) & (src < T) & ((src >= L) == (t >= L))
    return jnp.where(ok, pltpu.roll(x, (-d) % T, 0), 0.0)


def conv_fwd(XBC, w8, b, L, sends=()):
    T = XBC.shape[0]

    def body(x_ref, w_ref, b_ref, o_ref):
        x = x_ref[...]
        pre = b_ref[...] + functools.reduce(
            lambda a, c: a + c, [_conv_shift(x, k - 3, L) * w_ref[k:k + 1, :] for k in range(7)])
        o_ref[...] = _silu(pre)

    col = pl.BlockSpec((T, 128), lambda j: (0, j))
    return _pc(body, "conv_fwd", _sds((T, 1024)), grid=(8,),
               in_specs=[col, pl.BlockSpec((8, 128), lambda j: (0, j)), pl.BlockSpec((1, 128), lambda j: (0, j))],
               out_specs=col, sends=sends, gather=True)(XBC, w8, b)


def conv_bwd(XBC, w8, b, dS, dxs_skip, L, sends=()):
    T = XBC.shape[0]

    def body(x_ref, w_ref, b_ref, d0_r, d1_r, dsk_r, dx_o, dw_o, db_o):
        j = pl.program_id(0)
        x = x_ref[...]
        xs = [_conv_shift(x, k - 3, L) for k in range(7)]
        pre = b_ref[...] + functools.reduce(lambda a, c: a + c, [xs[k] * w_ref[k:k + 1, :] for k in range(7)])
        _, vjp = jax.vjp(_silu, pre)
        dact = d0_r[0] + d1_r[0] + jnp.where(j < 4, dsk_r[...], 0.0)
        dpre, = vjp(dact)
        dx_o[...] = functools.reduce(
            lambda a, c: a + c, [_conv_shift(dpre, 3 - k, L) * w_ref[k:k + 1, :] for k in range(7)])
        dw_o[...] = jnp.concatenate([jnp.sum(dpre * xs[k], axis=0, keepdims=True) for k in range(7)]
                                    + [jnp.zeros((1, 128), F32)], axis=0)
        db_o[...] = jnp.sum(dpre, axis=0, keepdims=True)

    col = pl.BlockSpec((T, 128), lambda j: (0, j))
    w_s = pl.BlockSpec((8, 128), lambda j: (0, j))
    b_s = pl.BlockSpec((1, 128), lambda j: (0, j))
    ds = lambda d: pl.BlockSpec((1, T, 128), lambda j: (d, 0, j))
    return _pc(body, "conv_bwd", [_sds((T, 1024)), _sds((8, 1024)), _sds((1, 1024))], grid=(8,),
               in_specs=[col, w_s, b_s, ds(0), ds(1), pl.BlockSpec((T, 128), lambda j: (0, jnp.minimum(j, 3)))],
               out_specs=[col, w_s, b_s], sends=sends)(XBC, w8, b, dS, dS, dxs_skip)


def _ssd_chunk(xs, bs, cs, dtraw, dtb, alog, hs, tri, d):
    dt = _softplus(dtraw + dtb)
    a = dt * (-jnp.exp(alog))
    acum = _exact(tri, a)
    tot = jnp.sum(a, axis=0, keepdims=True)
    wcol = jnp.exp(tot - acum) * dt
    ea = jnp.exp(acum)
    cd = jnp.exp(tot)
    acum_t, dt_t = acum.T, dt.T
    lane = _iota((Q, 128), 1)
    srow = _iota((128, Q), 0)
    lane1 = _iota((1, 128), 1)
    prow = _iota((128, NSTATE), 0)
    mask = tri > 0.5
    cbs = [mm_nt(cs[g], bs[g]) for g in range(2)]
    ys, hn = [], []
    for j in range(4):
        g = j // 2
        x = xs[j]
        yi, st, eac, cdl = [], [], [], []
        for u in range(2):
            slot = d * 8 + 2 * j + u
            col = lambda m: jnp.sum(jnp.where(lane == slot, m, 0.0), axis=1, keepdims=True)
            rowv = lambda m: jnp.sum(jnp.where(srow == slot, m, 0.0), axis=0, keepdims=True)
            seg = col(acum) - rowv(acum_t)
            dcy = jnp.where(mask, jnp.exp(jnp.where(mask, seg, 0.0)), 0.0)
            yi.append(mm(cbs[g] * dcy * rowv(dt_t), x))
            st.append(mm_tn(x, bs[g] * col(wcol)))
            eac.append(col(ea))
            cdl.append(jnp.sum(jnp.where(lane1 == slot, cd, 0.0), axis=1, keepdims=True))
        yin = mm_nt(cs[g], hs[j])
        ys.append(jnp.where(lane < HD, yi[0] + yin * eac[0], yi[1] + yin * eac[1]))
        hn.append(hs[j] * jnp.where(prow < HD, cdl[0], cdl[1]) + jnp.where(prow < HD, st[0], st[1]))
    return ys, hn


SSD_SUB = 2


def _ssd_block_idx(d, s, nlb, nbk):
    return jnp.where(d == 0, (s + nlb) % nbk, nbk - 1 - s)


def _ssd_rows(d, i):
    return pl.ds(pl.multiple_of(jnp.where(d == 0, i, SSD_SUB - 1 - i) * Q, Q), Q)


def _ssd_split(a):
    return ([a[:, 128 * j:128 * (j + 1)] for j in range(4)], [a[:, 512 + 128 * g:640 + 128 * g] for g in range(2)],
            [a[:, 768 + 128 * g:896 + 128 * g] for g in range(2)])


def ssd_fwd(ACT, DT, dtb, alog, tri2, L, sends=()):
    T = ACT.shape[0]
    RB = SSD_SUB * Q
    nlb, nbk = L // RB, T // RB

    def body(a_ref, dt_ref, dtb_ref, al_ref, tri_ref, y_o, hs_o, hst):
        d, s = pl.program_id(0), pl.program_id(1)
        _acc_init(s == 0, [hst])
        for i in range(SSD_SUB):
            rows = _ssd_rows(d, i)
            xs, bs, cs = _ssd_split(a_ref[rows, :])
            hs_o[0, i] = hst[...]
            ys, hn = _ssd_chunk(xs, bs, cs, dt_ref[rows, :], dtb_ref[...], al_ref[...], [hst[j] for j in range(4)],
                                tri_ref[0], d)
            y_o[0, rows, :] = jnp.concatenate(ys, axis=1)
            for j in range(4):
                hst[j] = hn[j]

    bk = lambda w: pl.BlockSpec((RB, w), lambda d, s: (_ssd_block_idx(d, s, nlb, nbk), 0))
    v128 = pl.BlockSpec((1, 128), lambda d, s: (0, 0))
    return _pc(body, "ssd_fwd", [_sds((2, T, 512)), _sds((2, T // Q, 4, 128, NSTATE))], grid=(2, nbk),
               in_specs=[bk(1024), bk(128), v128, v128, pl.BlockSpec((1, Q, Q), lambda d, s: (d, 0, 0))],
               out_specs=[pl.BlockSpec((1, RB, 512), lambda d, s: (d, _ssd_block_idx(d, s, nlb, nbk), 0)),
                          pl.BlockSpec((1, SSD_SUB, 4, 128, NSTATE), lambda d, s: (d, s, 0, 0, 0))],
               scratch=[pltpu.VMEM((4, 128, NSTATE), F32)], sends=sends, gather=True)(ACT, DT, dtb, alog, tri2)


def ssd_bwd(ACT, DT, dtb, alog, tri2, HS, dY, L, sends=()):
    T = ACT.shape[0]
    RB = SSD_SUB * Q
    nlb, nbk = L // RB, T // RB

    def body(a_ref, dt_ref, dtb_ref, al_ref, tri_ref, hs_ref, dy_ref, da_o, ddt_o, ddtb_o, dal_o, dh):
        d, sr = pl.program_id(0), pl.program_id(1)
        _acc_init(sr == 0, [dh, ddtb_o, dal_o])
        tri = tri_ref[0]
        fn = lambda xs_, bs_, cs_, dtr, dtb_, al, hs_: _ssd_chunk(xs_, bs_, cs_, dtr, dtb_, al, hs_, tri, d)
        for i in reversed(range(SSD_SUB)):
            rows = _ssd_rows(d, i)
            xs, bs, cs = _ssd_split(a_ref[rows, :])
            _, vjp = jax.vjp(fn, xs, bs, cs, dt_ref[rows, :], dtb_ref[...], al_ref[...],
                             [hs_ref[0, i, j] for j in range(4)])
            dy = dy_ref[rows, :]
            dxs, dbs, dcs, ddt, ddtb, dal, dhs = vjp(([dy[:, 128 * j:128 * (j + 1)] for j in range(4)],
                                                      [dh[j] for j in range(4)]))
            da_o[0, rows, :] = jnp.concatenate(dxs + dbs + dcs, axis=1)
            ddt_o[0, rows, :] = ddt
            ddtb_o[0] += ddtb
            dal_o[0] += dal
            for j in range(4):
                dh[j] = dhs[j]

    bidx = lambda d, sr: _ssd_block_idx(d, nbk - 1 - sr, nlb, nbk)
    bk = lambda w: pl.BlockSpec((RB, w), lambda d, sr: (bidx(d, sr), 0))
    v128 = pl.BlockSpec((1, 128), lambda d, sr: (0, 0))
    o128 = pl.BlockSpec((1, 1, 128), lambda d, sr: (d, 0, 0))
    return _pc(body, "ssd_bwd", [_sds((2, T, 1024)), _sds((2, T, 128)), _sds((2, 1, 128)), _sds((2, 1, 128))],
               grid=(2, nbk),
               in_specs=[bk(1024), bk(128), v128, v128, pl.BlockSpec((1, Q, Q), lambda d, sr: (d, 0, 0)),
                         pl.BlockSpec((1, SSD_SUB, 4, 128, NSTATE), lambda d, sr: (d, nbk - 1 - sr, 0, 0, 0)), bk(512)],
               out_specs=[pl.BlockSpec((1, RB, 1024), lambda d, sr: (d, bidx(d, sr), 0)),
                          pl.BlockSpec((1, RB, 128), lambda d, sr: (d, bidx(d, sr), 0)), o128, o128],
               scratch=[pltpu.VMEM((4, 128, NSTATE), F32)], sends=sends)(ACT, DT, dtb, alog, tri2, HS, dY)


_PAIR_HEADS = np.array([[0, 2], [1, 3]])


def _tables(L):
    t = jnp.arange(L)
    inv = 10000.0 ** (-jnp.arange(16, dtype=F32) / 16)

    def half(pos):
        ang = pos.astype(F32)[:, None] * inv[None, :]
        return jnp.concatenate([ang, ang], axis=1)

    ang = jnp.tile(jnp.concatenate([half(t // GW), half(t % GW)], axis=1), (1, 4))
    cos = jnp.concatenate([jnp.cos(ang), jnp.ones((LC, 256), F32)], axis=0)
    sin = jnp.concatenate([jnp.sin(ang), jnp.zeros((LC, 256), F32)], axis=0)
    rm = np.zeros((256, 256), np.float32)
    for j in range(256):
        if j % 32 < 16:
            rm[j + 16, j] = -1.0
        else:
            rm[j - 16, j] = 1.0
    tri = np.tril(np.ones((Q, Q), np.float32))
    return cos, sin, jnp.asarray(rm), jnp.asarray(np.stack([tri, tri.T]))


def _na_index(R):
    rc = np.array([0, 1, 2, 3, 4, R - 3, R - 2, R - 1])
    dy = np.clip(rc - 4, 0, R - 8)[:, None] + np.arange(8)[None, :] - rc[:, None] + 7
    qc, cc = np.arange(GW)[:, None], np.arange(GW)[None, :]
    dx = np.clip(cc - qc, -15, 15) + 15
    cstart = np.clip(qc - 8, 0, GW - 16)
    cmask = (cc >= cstart) & (cc < cstart + 16)
    idx = dy[:, None, :, None] * 31 + dx[None, :, None, :]
    return idx.reshape(8, GW, 8 * GW), np.broadcast_to(cmask[None, :, None, :], idx.shape).reshape(8, GW, 8 * GW), \
        dy, dx, cmask


def _na_bias(rpb, R):
    _, cm, dy, _, _ = _na_index(R)
    rows = rpb[:, dy.reshape(-1), :].reshape(4, 8, 4, 2, 31)
    p2 = jnp.pad(jnp.pad(rows, ((0, 0),) * 4 + ((0, 33),)).reshape(4, 8, 4, 128), ((0, 0), (0, 0), (0, 4), (0, 0)))
    negmask = jnp.asarray(np.where(cm[0], 0.0, NEG).astype(np.float32))

    def body(p_ref, m_ref, o_ref):
        for c in range(8):
            tiles = [pltpu.roll(jnp.broadcast_to(p_ref[0, c, jp:jp + 1, :], (GW, 128)), 113, 1, stride=1,
                                stride_axis=0) for jp in range(4)]
            o_ref[0, c] = jnp.where(m_ref[...] < 0.0, NEG, jnp.concatenate(tiles, axis=1))

    return _pc(body, "na_bias", _sds((4, 8, GW, 8 * GW)), grid=(4,),
               in_specs=[pl.BlockSpec((1, 8, 8, 128), lambda h: (h, 0, 0, 0)),
                         pl.BlockSpec((GW, 8 * GW), lambda h: (0, 0))],
               out_specs=pl.BlockSpec((1, 8, GW, 8 * GW), lambda h: (h, 0, 0, 0)))(p2, negmask)


def _na_bias_grad(dbias, R):
    _, _, dy, dx, cmask = _na_index(R)
    e1 = np.zeros((GW * GW, 128), np.float32)
    e1[np.arange(GW * GW), dx.reshape(-1)] = cmask.reshape(-1)
    a1 = dbias.reshape(4, 8, GW, 8, GW).transpose(0, 1, 3, 2, 4).reshape(256, GW * GW)
    v = exact_mm_call(a1, jnp.asarray(e1))[:, :31].reshape(4, 64, 31)
    e2 = np.zeros((64, 128), np.float32)
    e2[np.arange(64), dy.reshape(-1)] = 1.0
    a2 = jnp.pad(v.transpose(0, 2, 1).reshape(124, 64), ((0, 4), (0, 0)))
    return exact_mm_call(a2, jnp.asarray(e2))[:124, :15].reshape(4, 31, 15).transpose(0, 2, 1)


def _lanes(v, n=128):
    v = v.reshape(1, -1)
    return jnp.pad(v, ((0, 0), (0, n - v.shape[1])))


def _cls2(a, b):
    return jnp.stack([a, b]).reshape(2, 1, D)


def _win_p(g):
    return jnp.concatenate([g.reshape(IN_COLS, D), jnp.zeros((NP_IN - IN_COLS, D), g.dtype)], axis=0)


def _layer_consts(p):
    sinkp = jnp.broadcast_to(p["wa_sink"][_PAIR_HEADS][:, :, None, None], (2, 2, 1, 128))
    return dict(
        sinkp=sinkp, nosink=jnp.full((2, 2, 1, 128), NEG, F32),
        w8=jnp.concatenate([p["ssm_conv_w"], jnp.zeros((1, 1024), F32)], axis=0),
        cb=p["ssm_conv_b"].reshape(1, 1024), dtb=_lanes(p["ssm_dt_bias"]), alog=_lanes(p["ssm_a_log"]),
        dsk=jnp.repeat(p["ssm_d"], HD).reshape(1, 512), gs=p["ssm_norm_g"].reshape(1, 512),
        gmix=p["g_mix"].reshape(1, D), gffn=p["g_ffn"].reshape(1, D))


def _mods(mod2):
    return [_cls2(mod2[0, D * k:D * (k + 1)], mod2[1, D * k:D * (k + 1)]) for k in range(6)]


def _layer_fwd(X, mod2, c, rpb, tabs, L, ctx_out, shards, nxt):
    cos, sin, rm, tri2 = tabs
    sh1, sc1, gt1, sh2, sc2, gt2 = _mods(mod2)
    biasd = _na_bias(rpb, L // GW)
    fi, fo, wo = shards
    fcut, fcut2, ocut = 384, 576, 224
    (qa, qb, z, ka, va, kb, vb, xbc, dt, h1), (gfo_a,) = in_fwd(X, c["gmix"], sh1, sc1, c["win"], cos, sin, rm, L,
                                                                sends=(fo[:ocut],))
    (oa,), (gfi_b,) = wa_fwd(qa, ka, va, c["sinkp"], L, sends=(fi[fcut:fcut2],))
    (ob,), (gwo,) = na_fwd(qb, kb, vb, biasd, L, sends=(wo,))
    c = dict(c, wout=gwo.reshape(D, D))
    if ctx_out:
        oa_c = ctx_fwd(qa, ka, va, c["sinkp"], True, L)
        ob_c = ctx_fwd(qb, kb, vb, c["nosink"], False, L)
    else:
        oa_c = ob_c = jnp.zeros((LC, 256), F32)
    oa = jnp.concatenate([oa, oa_c], axis=0)
    ob = jnp.concatenate([ob, ob_c], axis=0)
    (act,), (gfi_c,) = conv_fwd(xbc, c["w8"], c["cb"], L, sends=(fi[fcut2:],))
    (y2, hs), (gfi_a,) = ssd_fwd(act, dt, c["dtb"], c["alog"], tri2, L, sends=(fi[:fcut],))
    (X1, cat), (gfo_b,) = out_fwd(oa, ob, y2, act, z, c["dsk"], c["gs"], c["wout"], X, gt1, L, sends=(fo[ocut:],))
    c = dict(c, wfi=jnp.concatenate([gfi_a, gfi_b, gfi_c], axis=1).reshape(2 * DFF, D),
             wfo=jnp.concatenate([gfo_a, gfo_b], axis=1).reshape(DFF, D))
    res = ffn_fwd(X1, c["gffn"], sh2, sc2, gt2, c["wfi"], c["wfo"], L, sends=nxt, skip_ctx=not ctx_out)
    (X2, ff), got = res if nxt else (res, ())
    saved = dict(X=X, X1=X1, ff=ff, qa=qa, qb=qb, z=z, ka=ka, va=va, kb=kb, vb=vb, xbc=xbc, dt=dt, h1=h1, oa=oa, ob=ob,
                 act=act, y2=y2, hs=hs, cat=cat, biasd=biasd)
    return X2, saved, c, got


def _row_blocks(gw):
    return gw.reshape(NDEV, gw.shape[0] // NDEV, gw.shape[1])


def _layer_bwd(dX2, s, mod2, c, tabs, L, ctx_out, carry):
    cos, sin, rm, tri2 = tabs
    sh1, sc1, gt1, sh2, sc2, gt2 = _mods(mod2)
    R = L // GW
    res = ffn_bwd(s["X1"], c["gffn"], sh2, sc2, gt2, c["wfi"], c["wfo"], s["ff"], dX2, L, sends=carry,
                  skip_ctx=not ctx_out)
    (dX1, h2, dU, actf, dOut, dgffn, dsh2, dsc2, dgt2), got = res if carry else (res, ())
    g = {}
    lat = None if ctx_out else L
    gfi = _row_blocks(tn_mm(dU, h2, 512, 1024, MXU, rows=lat))
    gfo = _row_blocks(tn_mm(actf, dOut, 256, 1024, MXU, rows=lat))
    doa, dob, dy, dxs_skip, dz, dmix, ddsk, dgs, dgt1 = out_bwd(s["oa"], s["ob"], s["y2"], s["act"], s["z"], c["dsk"],
                                                                c["gs"], c["wout"], gt1, dX1, L)
    gout = _row_blocks(tn_mm(s["cat"], dmix, 512, 1024, MXU, rows=lat))
    (dS, ddt2, ddtb, dal), (g["w_ffn_in"],) = ssd_bwd(
        s["act"], s["dt"], c["dtb"], c["alog"], tri2, s["hs"], dy, L, sends=(gfi,))
    (dxbc, dw8, dcb), (g["w_ffn_out"],) = conv_bwd(s["xbc"], c["w8"], c["cb"], dS, dxs_skip, L, sends=(gfo,))
    (dqa, dka, dva, dska), (g["w_out"],) = wa_bwd(s["qa"], s["ka"], s["va"], c["sinkp"], doa, L, sends=(gout,))
    dka, dva = dka[Q:], dva[Q:]
    dqb, dkb, dvb, dkxb, dvxb, dbias = na_bwd(s["qb"], s["kb"], s["vb"], s["biasd"], dob, L)
    if ctx_out:
        dqa_c, dk1, dv1, dsk1 = ctx_bwd(s["qa"], s["ka"], s["va"], c["sinkp"], doa, True, L)
        dqb_c, dk2, dv2, _ = ctx_bwd(s["qb"], s["kb"], s["vb"], c["nosink"], dob, False, L)
        dka = jnp.concatenate([dka[:L], dka[L:] + dk1], axis=0)
        dva = jnp.concatenate([dva[:L], dva[L:] + dv1], axis=0)
        dska = dska + dsk1
        dkxb, dvxb = dkxb + dk2, dvxb + dv2
    else:
        dqa_c = dqb_c = jnp.zeros((LC, 256), F32)
    cat0 = lambda a, b: jnp.concatenate([a, b], axis=0)
    dX, dycat, dgmix, dsh1, dsc1 = in_bwd(
        s["X"], c["gmix"], sh1, sc1, c["win"], cos, sin, rm, dX1, cat0(dqa, dqa_c), cat0(dqb, dqb_c), dz,
        dka, dva, cat0(dkb, dkxb), cat0(dvb, dvxb), dxbc, ddt2, L,
        latent_only=ctx_out)
    if ctx_out:
        gin = [_row_blocks(tn_mm(dycat, s["h1"], 512, D // 2, MXU, ncol=D // 2, col0=k)[:IN_COLS]) for k in (0, 1)]
    else:
        gin = _row_blocks(tn_mm(dycat, s["h1"], 512, 1024, MXU)[:IN_COLS])
    g["g_mix"] = dgmix.reshape(D)
    g["g_ffn"] = dgffn.reshape(D)
    sk = jnp.sum(dska, axis=(2, 3))
    g["wa_sink"] = jnp.zeros((4,), F32).at[_PAIR_HEADS.reshape(-1)].set(sk.reshape(-1))
    g["na_rpb"] = _na_bias_grad(dbias, R)
    g["ssm_conv_w"] = dw8[:7]
    g["ssm_conv_b"] = dcb.reshape(1024)
    g["ssm_dt_bias"] = (ddtb[0] + ddtb[1])[0, :16].reshape(2, 8)
    g["ssm_a_log"] = (dal[0] + dal[1])[0, :16].reshape(2, 8)
    g["ssm_d"] = jnp.sum(ddsk.reshape(8, HD), axis=1)
    g["ssm_norm_g"] = dgs.reshape(512)
    dmod2 = jnp.concatenate([dsh1, dsc1, dgt1, dsh2, dsc2, dgt2], axis=2).reshape(2, 6 * D)
    return dX, g, dmod2, gin, got


def local_step(x, ctx, tgt, mods, layers, shards, g_final, L):
    tabs = _tables(L)
    X = (x, ctx)
    consts = [_layer_consts(p) for p in layers]
    saved = []
    got = (shards["w_in_first"],)
    for i in range(2):
        consts[i] = dict(consts[i], win=_win_p(got[0]))
        nxt = (shards["w_in"][1],) if i == 0 else ()
        X, s, consts[i], got = _layer_fwd(X, mods[i], consts[i], layers[i]["na_rpb"], tabs, L, i == 0,
                                          (shards["w_ffn_in"][i], shards["w_ffn_out"][i], shards["w_out"][i]), nxt)
        saved.append(s)
    loss8, dX, dgfin = loss_head(X, g_final.reshape(1, D), tgt, L)
    grads, dmods = [None, None], [None, None]
    dX, grads[1], dmods[1], gin1, _ = _layer_bwd(dX, saved[1], mods[1], consts[1], tabs, L, False, ())
    dX, grads[0], dmods[0], gin0, (grads[1]["w_in"],) = _layer_bwd(dX, saved[0], mods[0], consts[0], tabs, L, True,
                                                                   (gin1,))
    return loss8[0, 0], dX, grads, jnp.stack(dmods), dgfin.reshape(D), gin0


def _place():
    x, y, c = lax.axis_index("x"), lax.axis_index("y"), lax.axis_index("c")
    return x, y, c


def _slot(b):
    return 4 * b[0] + 2 * b[1] + b[2]


def _any():
    return pl.BlockSpec(memory_space=pl.ANY)


def all_gather(xs, name):
    n = len(xs)

    def body(*refs):
        x_refs, o_refs = refs[:n], refs[n:2 * n]
        send_sems, recv_sems, local_sems = refs[2 * n:]
        x, y, c = _place()
        me, sib = (x, y, c), (x, y, 1 - c)
        chips = [(1 - x, y), (x, 1 - y), (1 - x, 1 - y)]

        def copy(t, k, blk, to, src=None):
            dst = o_refs[t].at[_slot(blk)]
            return pltpu.make_async_remote_copy(
                src_ref=dst if src is None else src, dst_ref=dst, send_sem=send_sems.at[7 * t + k],
                recv_sem=recv_sems.at[7 * t + k], device_id=to, device_id_type=MESH_T)

        mine = [pltpu.make_async_copy(x_refs[t], o_refs[t].at[_slot(me)], local_sems.at[t]) for t in range(n)]
        for cp in mine:
            cp.start()
        first = []
        for t in range(n):
            first.append(copy(t, 0, me, sib, src=x_refs[t]))
            first += [copy(t, 1 + j, me, (*chip, c), src=x_refs[t]) for j, chip in enumerate(chips)]
        for cp in first:
            cp.start()
        passed = []
        for j, chip in enumerate(chips):
            for t in range(n):
                copy(t, 1 + j, (*chip, c), me).wait_recv()
                cp = copy(t, 4 + j, (*chip, c), sib)
                cp.start()
                passed.append(cp)
        for t in range(n):
            copy(t, 0, sib, me).wait_recv()
            for j, chip in enumerate(chips):
                copy(t, 4 + j, (*chip, 1 - c), me).wait_recv()
        for cp in first + passed:
            cp.wait_send()
        for cp in mine:
            cp.wait()

    return pl.pallas_call(
        body, name=name, out_shape=[_sds((NDEV,) + a.shape, a.dtype) for a in xs],
        in_specs=[_any()] * n, out_specs=[_any()] * n,
        scratch_shapes=[pltpu.SemaphoreType.DMA((7 * n,)), pltpu.SemaphoreType.DMA((7 * n,)),
                        pltpu.SemaphoreType.DMA((n,))],
        interpret=_INTERPRET)(*xs)


def _a2a_sems(n):
    return [pltpu.SemaphoreType.DMA((7 * n,)), pltpu.SemaphoreType.DMA((7 * n,)), pltpu.SemaphoreType.DMA((n,))]


def _a2a_copies(x_refs, o_refs, send_sems, recv_sems, local_sems):
    n = len(x_refs)
    x, y, c = _place()
    me = (x, y, c)
    flip = lambda v, b: (1 - v) if b else v
    peers = [(flip(x, k >> 2 & 1), flip(y, k >> 1 & 1), flip(c, k & 1)) for k in range(1, NDEV)]
    mine = [pltpu.make_async_copy(x_refs[t].at[_slot(me)], o_refs[t].at[_slot(me)], local_sems.at[t])
            for t in range(n)]

    def copy(t, k, src_slot, dst_slot, to):
        return pltpu.make_async_remote_copy(
            src_ref=x_refs[t].at[src_slot], dst_ref=o_refs[t].at[dst_slot], send_sem=send_sems.at[7 * t + k],
            recv_sem=recv_sems.at[7 * t + k], device_id=to, device_id_type=MESH_T)

    sends = [copy(t, k, _slot(p), _slot(me), p) for t in range(n) for k, p in enumerate(peers)]
    recvs = [copy(t, k, _slot(p), _slot(p), me) for t in range(n) for k, p in enumerate(peers)]
    return mine, sends, recvs


def _ag_copies(x_refs, o_refs, send_sems, recv_sems, local_sems):
    n = len(x_refs)
    x, y, c = _place()
    me = (x, y, c)
    flip = lambda v, b: (1 - v) if b else v
    peers = [(flip(x, k >> 2 & 1), flip(y, k >> 1 & 1), flip(c, k & 1)) for k in range(1, NDEV)]
    mine = [pltpu.make_async_copy(x_refs[t], o_refs[t].at[_slot(me)], local_sems.at[t]) for t in range(n)]

    def copy(t, k, dst_slot, to):
        return pltpu.make_async_remote_copy(
            src_ref=x_refs[t], dst_ref=o_refs[t].at[dst_slot], send_sem=send_sems.at[7 * t + k],
            recv_sem=recv_sems.at[7 * t + k], device_id=to, device_id_type=MESH_T)

    sends = [copy(t, k, _slot(me), p) for t in range(n) for k, p in enumerate(peers)]
    recvs = [copy(t, k, _slot(p), me) for t in range(n) for k, p in enumerate(peers)]
    return mine, sends, recvs


def _ag_start(x_refs, o_refs, send_sems, recv_sems, local_sems):
    mine, sends, _ = _ag_copies(x_refs, o_refs, send_sems, recv_sems, local_sems)
    for cp in mine + sends:
        cp.start()


def _ag_wait(x_refs, o_refs, send_sems, recv_sems, local_sems):
    mine, sends, recvs = _ag_copies(x_refs, o_refs, send_sems, recv_sems, local_sems)
    for cp in recvs:
        cp.wait_recv()
    for cp in sends:
        cp.wait_send()
    for cp in mine:
        cp.wait()


def _a2a_start(x_refs, o_refs, send_sems, recv_sems, local_sems):
    mine, sends, _ = _a2a_copies(x_refs, o_refs, send_sems, recv_sems, local_sems)
    for cp in mine + sends:
        cp.start()


def _a2a_wait(x_refs, o_refs, send_sems, recv_sems, local_sems):
    mine, sends, recvs = _a2a_copies(x_refs, o_refs, send_sems, recv_sems, local_sems)
    for cp in recvs:
        cp.wait_recv()
    for cp in sends:
        cp.wait_send()
    for cp in mine:
        cp.wait()


def adam_reduce(P, w, m, v, name, sends=()):
    n, R, C = P.shape
    br = R // 4 if R % 64 == 0 else R

    def body(p_ref, w_ref, m_ref, v_ref, g_o, d_o, m_o, v_o):
        g = p_ref[0].astype(F32)
        for k in range(1, n):
            g = g + p_ref[k].astype(F32)
        m1 = ADAM_B1 * m_ref[...] + (1.0 - ADAM_B1) * g
        v1 = ADAM_B2 * v_ref[...] + (1.0 - ADAM_B2) * jnp.square(g)
        m_hat = m1 / (1.0 - ADAM_B1 ** ADAM_STEP)
        v_hat = v1 / (1.0 - ADAM_B2 ** ADAM_STEP)
        g_o[...] = g
        d_o[...] = -ADAM_LR * (m_hat / (jnp.sqrt(v_hat) + ADAM_EPS) + ADAM_WD * w_ref[...])
        m_o[...] = m1
        v_o[...] = v1

    blk = pl.BlockSpec((br, C), lambda i: (i, 0))
    return _pc(body, name, [_sds((R, C))] * 4, grid=(R // br,),
               in_specs=[pl.BlockSpec((n, br, C), lambda i: (0, i, 0)), blk, blk, blk], out_specs=[blk] * 4,
               sends=sends)(P, w, m, v)


def adam_layers(P0, P1, w, m, v, name, sends=()):
    n, R, C = P0.shape
    br = R // 4 if R % 64 == 0 else R
    nb = R // br

    def body(p0_ref, p1_ref, w_ref, m_ref, v_ref, g_o, d_o, m_o, v_o):
        def total(p_ref):
            g = p_ref[0].astype(F32)
            for k in range(1, n):
                g = g + p_ref[k].astype(F32)
            return g

        g = jnp.where(pl.program_id(0) == 0, total(p0_ref), total(p1_ref))
        m1 = ADAM_B1 * m_ref[0] + (1.0 - ADAM_B1) * g
        v1 = ADAM_B2 * v_ref[0] + (1.0 - ADAM_B2) * jnp.square(g)
        m_hat = m1 / (1.0 - ADAM_B1 ** ADAM_STEP)
        v_hat = v1 / (1.0 - ADAM_B2 ** ADAM_STEP)
        g_o[0] = g
        d_o[0] = -ADAM_LR * (m_hat / (jnp.sqrt(v_hat) + ADAM_EPS) + ADAM_WD * w_ref[0])
        m_o[0] = m1
        v_o[0] = v1

    blk = pl.BlockSpec((1, br, C), lambda l, i: (l, i, 0))
    p0 = pl.BlockSpec((n, br, C), lambda l, i: (0, jnp.where(l == 0, i, nb - 1), 0))
    p1 = pl.BlockSpec((n, br, C), lambda l, i: (0, jnp.where(l == 1, i, 0), 0))
    return _pc(body, name, [_sds((2, R, C))] * 4, grid=(2, nb), in_specs=[p0, p1, blk, blk, blk],
               out_specs=[blk] * 4, sends=sends)(P0, P1, w, m, v)


def mod_fwd(scin, wmod, bcol):
    def body(s_ref, w_ref, b_ref, o_ref):
        o_ref[0] = mm(_silu(s_ref[...]), w_ref[0]) + b_ref[0]

    return _pc(body, "mod_fwd", _sds((2, 16, 768)), grid=(2,),
               in_specs=[pl.BlockSpec((16, D), lambda l: (0, 0)), pl.BlockSpec((1, D, 768), lambda l: (l, 0, 0)),
                         pl.BlockSpec((1, 1, 768), lambda l: (l, 0, 0))],
               out_specs=pl.BlockSpec((1, 16, 768), lambda l: (l, 0, 0)))(scin, wmod, bcol)


def mod_bwd(scin, wmod, G):
    def body(s_ref, w_ref, g_ref, dw_o, ds_o):
        _, vjp = jax.vjp(lambda s, w: mm(_silu(s), w), s_ref[...], w_ref[0])
        ds, dw = vjp(g_ref[0])
        dw_o[0] = dw
        _acc_init(pl.program_id(0) == 0, [ds_o])
        ds_o[...] += ds

    full = pl.BlockSpec((16, D), lambda l: (0, 0))
    wsp = pl.BlockSpec((1, D, 768), lambda l: (l, 0, 0))
    return _pc(body, "mod_bwd", [_sds((2, D, 768)), _sds((16, D))], grid=(2,),
               in_specs=[full, wsp, pl.BlockSpec((1, 16, 768), lambda l: (l, 0, 0))], out_specs=[wsp, full])(
        scin, wmod, G)


_SMALL = ["b_mod", "g_mix", "wa_sink", "na_rpb", "ssm_conv_w", "ssm_conv_b", "ssm_dt_bias", "ssm_a_log", "ssm_d",
          "ssm_norm_g", "g_ffn", "g_final", "dmod_s", "dmod_c", "loss"]


def _pack(parts):
    rows = []
    for a in parts:
        f = a.reshape(-1).astype(F32)
        rows.append(jnp.pad(f, (0, (-f.shape[0]) % 1024)).reshape(-1, 128))
    return jnp.concatenate(rows, axis=0)


def _unpack(packed, shapes):
    out, r = [], 0
    for s in shapes:
        nel = int(np.prod(s))
        nr = -(-nel // 1024) * 8
        out.append(packed[r:r + nr].reshape(-1)[:nel].reshape(s))
        r += nr
    return out


def kernel(x, c, ctx, c_ctx, w_mod, b_mod, g_mix, w_in, wa_sink, na_rpb, ssm_conv_w, ssm_conv_b, ssm_dt_bias, ssm_a_log, ssm_d, ssm_norm_g, w_out, g_ffn, w_ffn_in, w_ffn_out, g_final, loss_target, m_c_ctx, m_w_mod, m_b_mod, m_g_mix, m_w_in, m_wa_sink, m_na_rpb, m_ssm_conv_w, m_ssm_conv_b, m_ssm_dt_bias, m_ssm_a_log, m_ssm_d, m_ssm_norm_g, m_w_out, m_g_ffn, m_w_ffn_in, m_w_ffn_out, m_g_final, v_c_ctx, v_w_mod, v_b_mod, v_g_mix, v_w_in, v_wa_sink, v_na_rpb, v_ssm_conv_w, v_ssm_conv_b, v_ssm_dt_bias, v_ssm_a_log, v_ssm_d, v_ssm_norm_g, v_w_out, v_g_ffn, v_w_ffn_in, v_w_ffn_out, v_g_final):
    L = x.shape[1]
    px, py, pc = _place()
    me = 4 * px + 2 * py + pc
    W = dict(c_ctx=c_ctx, w_mod=w_mod, b_mod=b_mod, g_mix=g_mix, w_in=w_in, wa_sink=wa_sink, na_rpb=na_rpb,
             ssm_conv_w=ssm_conv_w, ssm_conv_b=ssm_conv_b, ssm_dt_bias=ssm_dt_bias, ssm_a_log=ssm_a_log, ssm_d=ssm_d,
             ssm_norm_g=ssm_norm_g, w_out=w_out, g_ffn=g_ffn, w_ffn_in=w_ffn_in, w_ffn_out=w_ffn_out, g_final=g_final)
    M = dict(c_ctx=m_c_ctx, w_mod=m_w_mod, b_mod=m_b_mod, g_mix=m_g_mix, w_in=m_w_in, wa_sink=m_wa_sink,
             na_rpb=m_na_rpb, ssm_conv_w=m_ssm_conv_w, ssm_conv_b=m_ssm_conv_b, ssm_dt_bias=m_ssm_dt_bias,
             ssm_a_log=m_ssm_a_log, ssm_d=m_ssm_d, ssm_norm_g=m_ssm_norm_g, w_out=m_w_out, g_ffn=m_g_ffn,
             w_ffn_in=m_w_ffn_in, w_ffn_out=m_w_ffn_out, g_final=m_g_final)
    V = dict(c_ctx=v_c_ctx, w_mod=v_w_mod, b_mod=v_b_mod, g_mix=v_g_mix, w_in=v_w_in, wa_sink=v_wa_sink,
             na_rpb=v_na_rpb, ssm_conv_w=v_ssm_conv_w, ssm_conv_b=v_ssm_conv_b, ssm_dt_bias=v_ssm_dt_bias,
             ssm_a_log=v_ssm_a_log, ssm_d=v_ssm_d, ssm_norm_g=v_ssm_norm_g, w_out=v_w_out, g_ffn=v_g_ffn,
             w_ffn_in=v_w_ffn_in, w_ffn_out=v_w_ffn_out, g_final=v_g_final)

    tr = lambda a: a.transpose(0, 2, 1)
    shards = dict(w_in=tr(w_in).astype(MXU), w_out=w_out.astype(MXU), w_ffn_in=tr(w_ffn_in).astype(MXU),
                  w_ffn_out=w_ffn_out.astype(MXU))
    c_all, conv_all, shards["w_in_first"] = all_gather([c, ssm_conv_w, shards["w_in"][0]], "gather_first")
    conv_f = conv_all.transpose(1, 2, 0, 3).reshape(2, 7, 1024)

    scin = jnp.concatenate([c_all.reshape(NDEV, D), c_ctx.reshape(1, D), jnp.zeros((7, D), F32)], axis=0)
    bcol = lax.dynamic_slice_in_dim(b_mod, me * 768, 768, axis=1).reshape(2, 1, 768)
    mod_all, = all_gather([mod_fwd(scin, w_mod, bcol)], "gather_mod")
    mod_rows = mod_all.transpose(1, 2, 0, 3).reshape(2, 16, 6 * D)
    mods = jnp.stack([lax.dynamic_index_in_dim(mod_rows, me, axis=1, keepdims=False), mod_rows[:, 8]], axis=1)

    layers = [dict(g_mix=g_mix[i], wa_sink=wa_sink[i], na_rpb=na_rpb[i], ssm_conv_w=conv_f[i],
                   ssm_conv_b=ssm_conv_b[i], ssm_dt_bias=ssm_dt_bias[i], ssm_a_log=ssm_a_log[i], ssm_d=ssm_d[i],
                   ssm_norm_g=ssm_norm_g[i], g_ffn=g_ffn[i]) for i in range(2)]
    loss, dx, grads, dmods, dgfin, gin0 = local_step(x[0], ctx[0], loss_target[0], mods, layers, shards, g_final, L)

    stk = lambda n: jnp.stack([grads[0][n], grads[1][n]])
    small = dict(b_mod=dmods[:, 0] + dmods[:, 1], g_final=dgfin, dmod_s=dmods[:, 0], dmod_c=dmods[:, 1],
                 loss=loss.reshape(1))
    for nme in _SMALL:
        if nme not in small:
            small[nme] = stk(nme)
    shapes = [small[nme].shape for nme in _SMALL]
    zero_like = lambda nme: jnp.zeros(small[nme].shape, F32)
    own = lambda S, nme: S[nme] if (nme in S and S[nme].shape == small[nme].shape) else zero_like(nme)
    gath, = all_gather([_pack([small[nme] for nme in _SMALL])], "gather_grads")
    sm = adam_reduce(gath, _pack([own(W, nme) for nme in _SMALL]), _pack([own(M, nme) for nme in _SMALL]),
                     _pack([own(V, nme) for nme in _SMALL]), "adam_small")
    res = {nme: vals for nme, vals in zip(_SMALL, zip(*[_unpack(a, shapes) for a in sm]))}
    loss = res["loss"][0][0]

    cols = lambda a: lax.dynamic_slice_in_dim(a, me * 768, 768, axis=-1)
    rows_of = lambda s: -(-int(np.prod(s)) // 1024) * 8
    r0 = sum(rows_of(s) for s in shapes[:_SMALL.index("dmod_s")])
    dmod_s_all = gath[:, r0:r0 + rows_of(small["dmod_s"].shape)].reshape(NDEV, 2, 6 * D).transpose(1, 0, 2)
    G = jnp.concatenate([cols(dmod_s_all), cols(res["dmod_c"][0])[:, None, :], jnp.zeros((2, 7, 768), F32)], axis=1)
    dwmod, dscin = mod_bwd(scin, w_mod, G)
    cc_g, = all_gather([dscin[8].reshape(8, 128)], "gather_cctx")
    out = {}
    out["c_ctx"] = [a.reshape(D) for a in adam_reduce(cc_g, c_ctx.reshape(8, 128), m_c_ctx.reshape(8, 128),
                                                      v_c_ctx.reshape(8, 128), "adam_cctx")]
    res_wmod, (got1,) = adam_reduce(dwmod.reshape(1, 2 * D, 768), w_mod.reshape(2 * D, 768),
                                    m_w_mod.reshape(2 * D, 768), v_w_mod.reshape(2 * D, 768), "adam_wmod",
                                    sends=(gin0[1],))
    out["w_mod"] = [a.reshape(2, D, 768) for a in res_wmod]
    gconv = lax.dynamic_slice_in_dim(res["ssm_conv_w"][0], me * 128, 128, axis=2)
    out["ssm_conv_w"] = [a.reshape(2, 7, 128) for a in adam_reduce(
        gconv.reshape(1, 14, 128), ssm_conv_w.reshape(14, 128), m_ssm_conv_w.reshape(14, 128),
        v_ssm_conv_w.reshape(14, 128), "adam_conv")]
    for nme in _SMALL:
        if nme not in ("ssm_conv_w", "dmod_s", "dmod_c", "loss"):
            out[nme] = list(res[nme])

    adam_big = lambda nme, t, **kw: adam_layers(grads[0][nme], grads[1][nme], t(W[nme]), t(M[nme]), t(V[nme]),
                                                "adam_" + nme, **kw)
    same = lambda a: a
    res_fi, (got0,) = adam_big("w_ffn_in", tr, sends=(gin0[0],))
    grads[0]["w_in"] = jnp.concatenate([got0, got1], axis=2)
    out["w_ffn_in"] = [tr(a) for a in res_fi]
    out["w_ffn_out"] = list(adam_big("w_ffn_out", same))
    out["w_out"] = list(adam_big("w_out", same))
    out["w_in"] = [tr(a) for a in adam_big("w_in", tr)]
    order = ["c_ctx", "w_mod", "b_mod", "g_mix", "w_in", "wa_sink", "na_rpb", "ssm_conv_w", "ssm_conv_b",
             "ssm_dt_bias", "ssm_a_log", "ssm_d", "ssm_norm_g", "w_out", "g_ffn", "w_ffn_in", "w_ffn_out", "g_final"]
    return (loss, dx.reshape(1, L, D), *[out[nme][0] for nme in order], *[out[nme][1] for nme in order],
            *[out[nme][2] for nme in order], *[out[nme][3] for nme in order])
```

```python
import functools

import numpy as np
import jax
import jax.numpy as jnp
from jax import lax
from jax.experimental import pallas as pl
from jax.experimental.pallas import tpu as pltpu

F32 = jnp.float32
MXU = jnp.bfloat16
_INTERPRET = False
VMEM_LIMIT = 60 * 1024 * 1024

D = 1024
LC = 256
GW = 64
HD = 64
EPS = 1e-6
NEG = -1e30
NDEV = 8
Q = 128
NSTATE = 128
DFF = 2816
IN_COLS = 2832
NP_IN = 3072
C_QA, C_QB, C_Z, C_KA, C_VA, C_KB, C_VB, C_XBC, C_DT = 0, 256, 512, 1024, 1152, 1280, 1536, 1792, 2816
ADAM_LR, ADAM_B1, ADAM_B2, ADAM_EPS, ADAM_WD, ADAM_STEP = 0.001, 0.9, 0.999, 1e-08, 0.01, 10
MESH_T = pl.DeviceIdType.MESH


def _dg(a, b, ca, cb):
    return lax.dot_general(a.astype(MXU), b.astype(MXU), (((ca,), (cb,)), ((), ())), preferred_element_type=F32)


@jax.custom_vjp
def mm(a, b):
    return _dg(a, b, 1, 0)


def _mm_f(a, b):
    return _dg(a, b, 1, 0), (a, b)


def _mm_b(res, g):
    a, b = res
    return _dg(g, b, 1, 1).astype(a.dtype), _dg(a, g, 0, 0).astype(b.dtype)


mm.defvjp(_mm_f, _mm_b)


@jax.custom_vjp
def mm_nt(a, b):
    return _dg(a, b, 1, 1)


def _mmnt_f(a, b):
    return _dg(a, b, 1, 1), (a, b)


def _mmnt_b(res, g):
    a, b = res
    return _dg(g, b, 1, 0).astype(a.dtype), _dg(g, a, 0, 0).astype(b.dtype)


mm_nt.defvjp(_mmnt_f, _mmnt_b)


@jax.custom_vjp
def mm_tn(a, b):
    return _dg(a, b, 0, 0)


def _mmtn_f(a, b):
    return _dg(a, b, 0, 0), (a, b)


def _mmtn_b(res, g):
    a, b = res
    return _dg(b, g, 1, 1).astype(a.dtype), _dg(a, g, 1, 0).astype(b.dtype)


mm_tn.defvjp(_mmtn_f, _mmtn_b)


@jax.custom_vjp
def mmw(a, w):
    return _dg(a, w, 1, 0)


mmw.defvjp(lambda a, w: (_dg(a, w, 1, 0), w), lambda w, g: (_dg(g, w, 1, 1), None))


@jax.custom_vjp
def mmw_nt(a, w):
    return _dg(a, w, 1, 1)


mmw_nt.defvjp(lambda a, w: (_dg(a, w, 1, 1), w), lambda w, g: (_dg(g, w, 1, 0), None))


def _exact(a, b):
    return lax.dot_general(a, b, (((1,), (0,)), ((), ())), precision=lax.Precision.HIGHEST,
                           preferred_element_type=F32)


def _pc(body, name, out_shape, grid=None, in_specs=None, out_specs=None, scratch=(), sends=(), gather=False):
    params = pltpu.CompilerParams(vmem_limit_bytes=VMEM_LIMIT)
    if sends and not isinstance(out_shape, (list, tuple)):
        out_shape, out_specs = [out_shape], [out_specs]
    start, wait = (_ag_start, _ag_wait) if gather else (_a2a_start, _a2a_wait)
    if not sends:
        kw = {}
        if grid is not None:
            kw = dict(grid=grid, in_specs=in_specs, out_specs=out_specs)
        elif in_specs is not None:
            kw = dict(in_specs=in_specs, out_specs=out_specs)
        return pl.pallas_call(body, name=name, out_shape=out_shape, scratch_shapes=list(scratch),
                              compiler_params=params, interpret=_INTERPRET, **kw)
    n, nin, nout, nscr = len(sends), len(in_specs), len(out_shape), len(scratch)

    def body2(*refs):
        cin, xs = refs[:nin], refs[nin:nin + n]
        couts, os_ = refs[nin + n:nin + n + nout], refs[nin + n + nout:nin + 2 * n + nout]
        cscr, sems = refs[nin + 2 * n + nout:nin + 2 * n + nout + nscr], refs[nin + 2 * n + nout + nscr:]
        ids = [pl.program_id(a) for a in range(len(grid))]
        first = functools.reduce(lambda a, b: a & b, [i == 0 for i in ids])
        last = functools.reduce(lambda a, b: a & b, [i == g - 1 for i, g in zip(ids, grid)])

        @pl.when(first)
        def _():
            start(xs, os_, *sems)

        body(*cin, *couts, *cscr)

        @pl.when(last)
        def _():
            wait(xs, os_, *sems)

    call = pl.pallas_call(
        body2, name=name,
        out_shape=list(out_shape) + [_sds(((NDEV,) if gather else ()) + a.shape, a.dtype) for a in sends],
        grid=grid, in_specs=list(in_specs) + [_any()] * n, out_specs=list(out_specs) + [_any()] * n,
        scratch_shapes=list(scratch) + _a2a_sems(n), compiler_params=params, interpret=_INTERPRET)

    def run(*args):
        res = call(*args, *sends)
        return res[:nout], res[nout:]

    return run


def _vm():
    return pl.BlockSpec(memory_space=pltpu.VMEM)


def _sds(shape, dt=F32):
    return jax.ShapeDtypeStruct(shape, dt)


def _iota(shape, dim):
    return lax.broadcasted_iota(jnp.int32, shape, dim)


def _silu(x):
    return x * jax.nn.sigmoid(x)


def _softplus(x):
    return jnp.maximum(x, 0.0) + jnp.log1p(jnp.exp(-jnp.abs(x)))


def _normmod(x, g, sh, sc):
    r = lax.rsqrt(jnp.mean(x * x, axis=-1, keepdims=True) + EPS)
    return (x * r * g) * (1.0 + sc) + sh


def _rope(x, cos, sin, rm):
    return x * cos + _exact(x, rm) * sin


def _swap12(x):
    lane = _iota(x.shape, 1)
    up, down = pltpu.roll(x, 192, 1), pltpu.roll(x, 64, 1)
    return jnp.where((lane >= 64) & (lane < 128), up, jnp.where((lane >= 128) & (lane < 192), down, x))


def _skip_ctx_tile(skip, nlt, compute, outs):
    if not skip:
        compute()
        return
    i = pl.program_id(0)
    pl.when(i < nlt)(compute)

    @pl.when(i >= nlt)
    def _():
        for r in outs:
            r[...] = jnp.zeros_like(r)


def _acc_init(first, refs):
    @pl.when(first)
    def _():
        for r in refs:
            r[...] = jnp.zeros_like(r)


def _stream(X, TR, nlt):
    if not isinstance(X, tuple):
        return (X,), [pl.BlockSpec((TR, D), lambda i: (i, 0))], lambda refs: refs[0][...]
    specs = [pl.BlockSpec((TR, D), lambda i: (jnp.minimum(i, nlt - 1), 0)), pl.BlockSpec((TR, D), lambda i: (0, 0))]
    return X, specs, lambda refs: jnp.where(pl.program_id(0) < nlt, refs[0][...], refs[1][...])


def in_fwd(X, g, sh, sc, W, cos, sin, rm, L, sends=()):
    T = L + LC
    TR = 256
    nlt = L // TR
    xs, xspecs, xread = _stream(X, TR, nlt)

    def body(*refs):
        (g_ref, sh_ref, sc_ref, w_ref, cos_ref, sin_ref, rm_ref,
         qa, qb, z, ka, va, kb, vb, xbc, dt, hout) = refs[len(xs):]
        h = _normmod(xread(refs), g_ref[...], sh_ref[0], sc_ref[0]).astype(MXU)
        hout[...] = h
        y = lax.dot_general(h, w_ref[...], (((1,), (1,)), ((), ())), preferred_element_type=F32)
        cs, sn, r = cos_ref[...], sin_ref[...], rm_ref[...]
        qa[...] = _rope(_swap12(y[:, C_QA:C_QB]), cs, sn, r).astype(MXU)
        qb[...] = y[:, C_QB:C_Z].astype(MXU)
        z[...] = y[:, C_Z:C_KA]
        ka[...] = _rope(y[:, C_KA:C_VA], cs[:, :128], sn[:, :128], r[:128, :128]).astype(MXU)
        va[...] = y[:, C_VA:C_KB].astype(MXU)
        kb[...] = y[:, C_KB:C_VB].astype(MXU)
        vb[...] = y[:, C_VB:C_XBC].astype(MXU)
        xbc[...] = y[:, C_XBC:C_DT]
        dt[...] = y[:, C_DT:C_DT + 128]

    row = lambda w: pl.BlockSpec((TR, w), lambda i: (i, 0))
    cls = pl.BlockSpec((1, 1, D), lambda i: (i // nlt, 0, 0))
    widths = [(256, MXU), (256, MXU), (512, F32), (128, MXU), (128, MXU), (256, MXU), (256, MXU), (1024, F32),
              (128, F32), (D, MXU)]
    return _pc(body, "in_fwd", [_sds((T, w), d) for w, d in widths], grid=(T // TR,),
               in_specs=xspecs + [pl.BlockSpec((1, D), lambda i: (0, 0)), cls, cls, _vm(), row(256), row(256), _vm()],
               out_specs=[row(w) for w, _ in widths], sends=sends, gather=True)(*xs, g, sh, sc, W, cos, sin, rm)


def in_bwd(X, g, sh, sc, W, cos, sin, rm, dxres, dqa, dqb, dz, dka, dva, dkb, dvb, dxbc, ddt2, L, latent_only):
    T = L + LC
    TR = 256
    nlt = L // TR
    xs, xspecs, xread = _stream(X, TR, nlt)

    def body(*refs):
        (g_ref, sh_ref, sc_ref, w_ref, cos_ref, sin_ref, rm_ref, dxres_ref, dqa_r, dqb_r, dz_r, dka_r,
         dva_r, dkb_r, dvb_r, dxbc_r, ddt0_r, ddt1_r, dx_o, dy_o, dg_o, dsh_o, dsc_o) = refs[len(xs):]
        i = pl.program_id(0)
        cs, sn, r = cos_ref[...], sin_ref[...], rm_ref[...]
        _, vq = jax.vjp(lambda t: _rope(t, cs, sn, r), dqa_r[...])
        _, vk = jax.vjp(lambda t: _rope(t, cs[:, :128], sn[:, :128], r[:128, :128]), dka_r[...])
        dyqa = _swap12(vq(dqa_r[...])[0])
        dyka, = vk(dka_r[...])
        ddt = ddt0_r[0] + ddt1_r[0]
        dy = jnp.concatenate([dyqa, dqb_r[...], dz_r[...], dyka, dva_r[...], dkb_r[...], dvb_r[...], dxbc_r[...],
                              ddt, jnp.zeros((TR, NP_IN - C_DT - 128), F32)], axis=1).astype(MXU)
        dy_o[...] = dy
        dh = jnp.dot(dy, w_ref[...], preferred_element_type=F32)
        _, vp = jax.vjp(_normmod, xread(refs), g_ref[...], sh_ref[0], sc_ref[0])
        dx, dg, dsh, dsc = vp(dh)
        if latent_only:
            @pl.when(i < nlt)
            def _():
                dx_o[...] = dx + dxres_ref[...]
        else:
            dx_o[...] = dx + dxres_ref[...]
        _acc_init(i == 0, [dg_o])
        _acc_init((i == 0) | (i == nlt), [dsh_o, dsc_o])
        dg_o[...] += dg
        dsh_o[0] += dsh
        dsc_o[0] += dsc

    row = lambda w: pl.BlockSpec((TR, w), lambda i: (i, 0))
    cls = pl.BlockSpec((1, 1, D), lambda i: (i // nlt, 0, 0))
    vec = pl.BlockSpec((1, D), lambda i: (0, 0))
    dts = lambda d: pl.BlockSpec((1, TR, 128), lambda i: (d, i, 0))
    dxs = pl.BlockSpec((TR, D), lambda i: (jnp.minimum(i, nlt - 1), 0)) if latent_only else row(D)
    return _pc(body, "in_bwd",
               [_sds((L if latent_only else T, D)), _sds((T, NP_IN), MXU), _sds((1, D)), _sds((2, 1, D)),
                _sds((2, 1, D))],
               grid=(T // TR,),
               in_specs=xspecs + [vec, cls, cls, _vm(), row(256), row(256), _vm(), row(D), row(256), row(256),
                                  row(512), row(128), row(128), row(256), row(256), row(1024), dts(0), dts(1)],
               out_specs=[dxs, row(NP_IN), vec, cls, cls])(
        *xs, g, sh, sc, W, cos, sin, rm, dxres, dqa, dqb, dz, dka, dva, dkb, dvb, dxbc, ddt2, ddt2)


def tn_mm(A, G, bk, bn, out_dtype, ncol=None, col0=0, rows=None, sends=()):
    T, K = A.shape
    T = T if rows is None else rows
    N = G.shape[1] if ncol is None else ncol
    first = col0 * (N // bn)
    bt = T
    nt = T // bt

    def body(a_ref, g_ref, o_ref, acc):
        t = pl.program_id(2)
        _acc_init(t == 0, [acc])
        acc[...] += lax.dot_general(a_ref[...], g_ref[...], (((0,), (0,)), ((), ())), preferred_element_type=F32)

        @pl.when(t == nt - 1)
        def _():
            o_ref[...] = acc[...].astype(out_dtype)

    return _pc(body, "tn_mm", _sds((K, N), out_dtype), grid=(K // bk, N // bn, nt),
               in_specs=[pl.BlockSpec((bt, bk), lambda k, n, t: (t, k)),
                         pl.BlockSpec((bt, bn), lambda k, n, t: (t, first + n))],
               out_specs=pl.BlockSpec((bk, bn), lambda k, n, t: (k, n)),
               scratch=[pltpu.VMEM((bk, bn), F32)], sends=sends)(A, G)


def _ssm_out(yf, yb, xs, z, dsk, gs):
    y = (yf + yb + dsk * xs) * _silu(z)
    r = lax.rsqrt(jnp.mean(y * y, axis=-1, keepdims=True) + EPS)
    return y * r * gs


def out_fwd(oa, ob, y2, act, z, dsk, gs, W, X, gate, L, sends=()):
    T = L + LC
    TR = 256
    nlt = L // TR
    xs, xspecs, xread = _stream(X, TR, nlt)

    def body(*refs):
        oa_r, ob_r, yf_r, yb_r, xs_r, z_r, dsk_r, gs_r, w_ref, gt_ref, x1_o, cat_o = refs[len(xs):]
        oc = _ssm_out(yf_r[0], yb_r[0], xs_r[...], z_r[...], dsk_r[...], gs_r[...])
        cat = jnp.concatenate([_swap12(oa_r[...]), ob_r[...], oc], axis=1).astype(MXU)
        cat_o[...] = cat
        x1_o[...] = xread(refs) + gt_ref[0] * jnp.dot(cat, w_ref[...], preferred_element_type=F32)

    row = lambda w: pl.BlockSpec((TR, w), lambda i: (i, 0))
    ys = lambda d: pl.BlockSpec((1, TR, 512), lambda i: (d, i, 0))
    cls = pl.BlockSpec((1, 1, D), lambda i: (i // nlt, 0, 0))
    v512 = pl.BlockSpec((1, 512), lambda i: (0, 0))
    return _pc(body, "out_fwd", [_sds((T, D)), _sds((T, D), MXU)], grid=(T // TR,),
               in_specs=xspecs + [row(256), row(256), ys(0), ys(1), row(512), row(512), v512, v512, _vm(), cls],
               out_specs=[row(D), row(D)], sends=sends, gather=True)(*xs, oa, ob, y2, y2, act, z, dsk, gs, W, gate)


def out_bwd(oa, ob, y2, act, z, dsk, gs, W, gate, dX1, L):
    T = dX1.shape[0]
    TR = 256
    nlt = L // TR

    def body(oa_r, ob_r, yf_r, yb_r, xs_r, z_r, dsk_r, gs_r, w_ref, gt_ref, dx1_r,
             doa_o, dob_o, dy_o, dxs_o, dz_o, dmix_o, ddsk_o, dgs_o, dgt_o):
        i = pl.program_id(0)
        w = w_ref[...]

        def f(oa_, ob_, yf, yb, xs, z_, dsk_, gs_, gt):
            oc = _ssm_out(yf, yb, xs, z_, dsk_, gs_)
            return gt * mmw(jnp.concatenate([oa_, ob_, oc], axis=1), w)

        _, vjp = jax.vjp(f, _swap12(oa_r[...]), ob_r[...], yf_r[0], yb_r[0], xs_r[...], z_r[...], dsk_r[...],
                         gs_r[...], gt_ref[0])
        dx1 = dx1_r[...]
        doa, dob, dyf, _, dxs, dz, ddsk, dgs, dgt = vjp(dx1)
        doa_o[...] = _swap12(doa)
        dob_o[...] = dob
        dy_o[...] = dyf
        dxs_o[...] = dxs
        dz_o[...] = dz
        dmix_o[...] = (gt_ref[0] * dx1).astype(MXU)
        _acc_init(i == 0, [ddsk_o, dgs_o])
        _acc_init((i == 0) | (i == nlt), [dgt_o])
        ddsk_o[...] += ddsk
        dgs_o[...] += dgs
        dgt_o[0] += dgt

    row = lambda w: pl.BlockSpec((TR, w), lambda i: (i, 0))
    ys = lambda d: pl.BlockSpec((1, TR, 512), lambda i: (d, i, 0))
    cls = pl.BlockSpec((1, 1, D), lambda i: (i // nlt, 0, 0))
    v512 = pl.BlockSpec((1, 512), lambda i: (0, 0))
    return _pc(body, "out_bwd",
               [_sds((T, 256)), _sds((T, 256)), _sds((T, 512)), _sds((T, 512)), _sds((T, 512)), _sds((T, D), MXU),
                _sds((1, 512)), _sds((1, 512)), _sds((2, 1, D))],
               grid=(T // TR,),
               in_specs=[row(256), row(256), ys(0), ys(1), row(512), row(512), v512, v512, _vm(), cls, row(D)],
               out_specs=[row(256), row(256), row(512), row(512), row(512), row(D), v512, v512, cls])(
        oa, ob, y2, y2, act, z, dsk, gs, W, gate, dX1)


def ffn_fwd(X, g, sh, sc, gate, Win, Wout, L, sends=(), skip_ctx=False):
    T = X.shape[0]
    TR = 256
    nlt = L // TR

    def body(x_ref, g_ref, sh_ref, sc_ref, gt_ref, wi_ref, wo_ref, o_ref, f_ref):
        def compute():
            h = _normmod(x_ref[...], g_ref[...], sh_ref[0], sc_ref[0]).astype(MXU)
            nt = (((1,), (1,)), ((), ()))
            a = lax.dot_general(h, wi_ref[0:DFF, :], nt, preferred_element_type=F32)
            u = lax.dot_general(h, wi_ref[DFF:2 * DFF, :], nt, preferred_element_type=F32)
            act = (_silu(a) * u).astype(MXU)
            ff = jnp.dot(act, wo_ref[...], preferred_element_type=F32)
            f_ref[...] = ff
            o_ref[...] = x_ref[...] + gt_ref[0] * ff

        _skip_ctx_tile(skip_ctx, nlt, compute, [o_ref, f_ref])

    row = lambda w: pl.BlockSpec((TR, w), lambda i: (i, 0))
    cls = pl.BlockSpec((1, 1, D), lambda i: (i // nlt, 0, 0))
    vec = pl.BlockSpec((1, D), lambda i: (0, 0))
    return _pc(body, "ffn_fwd", [_sds((T, D)), _sds((T, D))], grid=(T // TR,),
               in_specs=[row(D), vec, cls, cls, cls, _vm(), _vm()], out_specs=[row(D), row(D)], sends=sends,
               gather=True)(X, g, sh, sc, gate, Win, Wout)


def ffn_bwd(X, g, sh, sc, gate, Win, Wout, FF, dX2, L, sends=(), nchunk=2, skip_ctx=False):
    T = X.shape[0]
    TR = 256
    nlt = L // TR
    CH = DFF // nchunk

    def body(x_ref, g_ref, sh_ref, sc_ref, gt_ref, wi_ref, wo_ref, ff_r, dx2_r,
             dx_o, h_o, du_o, act_o, dout_o, dg_o, dsh_o, dsc_o, dgt_o):
        i = pl.program_id(0)
        _acc_init(i == 0, [dg_o])
        _acc_init((i == 0) | (i == nlt), [dsh_o, dsc_o, dgt_o])

        def compute():
            h, vp = jax.vjp(_normmod, x_ref[...], g_ref[...], sh_ref[0], sc_ref[0])
            dx2 = dx2_r[...]
            dout = gt_ref[0] * dx2
            zero = jnp.zeros((TR, CH), F32)
            dh = jnp.zeros((TR, D), F32)
            for c in range(nchunk):
                lo, hi = c * CH, (c + 1) * CH
                wg, wu, wo = wi_ref[lo:hi, :], wi_ref[DFF + lo:DFF + hi, :], wo_ref[lo:hi, :]

                def f(h_, eg, eu):
                    act = _silu(mmw_nt(h_, wg) + eg) * (mmw_nt(h_, wu) + eu)
                    return mmw(act, wo), act

                _, vjp_c, act = jax.vjp(f, h, zero, zero, has_aux=True)
                dh_c, da, du = vjp_c(dout)
                dh = dh + dh_c
                du_o[:, lo:hi] = da.astype(MXU)
                du_o[:, DFF + lo:DFF + hi] = du.astype(MXU)
                act_o[:, lo:hi] = act.astype(MXU)
            dx, dg, dsh, dsc = vp(dh)
            dx_o[...] = dx + dx2
            h_o[...] = h.astype(MXU)
            dout_o[...] = dout.astype(MXU)
            dg_o[...] += dg
            dsh_o[0] += dsh
            dsc_o[0] += dsc
            dgt_o[0] += jnp.sum(dx2 * ff_r[...], axis=0, keepdims=True)

        _skip_ctx_tile(skip_ctx, nlt, compute, [dx_o, h_o, du_o, act_o, dout_o])

    row = lambda w: pl.BlockSpec((TR, w), lambda i: (i, 0))
    cls = pl.BlockSpec((1, 1, D), lambda i: (i // nlt, 0, 0))
    vec = pl.BlockSpec((1, D), lambda i: (0, 0))
    return _pc(body, "ffn_bwd",
               [_sds((T, D)), _sds((T, D), MXU), _sds((T, 2 * DFF), MXU), _sds((T, DFF), MXU), _sds((T, D), MXU),
                _sds((1, D)), _sds((2, 1, D)), _sds((2, 1, D)), _sds((2, 1, D))],
               grid=(T // TR,),
               in_specs=[row(D), vec, cls, cls, cls, _vm(), _vm(), row(D), row(D)],
               out_specs=[row(D), row(D), row(2 * DFF), row(DFF), row(D), vec, cls, cls, cls], sends=sends)(
        X, g, sh, sc, gate, Win, Wout, FF, dX2)


def loss_head(X2, g, tgt, L):
    T = X2.shape[0]
    TR = 256
    nlt = L // TR

    def body(x_ref, g_ref, t_ref, loss_o, dx_o, dg_o):
        i = pl.program_id(0)
        _acc_init(i == 0, [loss_o, dg_o])

        @pl.when(i < nlt)
        def _():
            def f(x, g_):
                y = x * lax.rsqrt(jnp.mean(x * x, axis=-1, keepdims=True) + EPS) * g_
                return 0.5 * jnp.sum(jnp.mean(jnp.square(y - t_ref[...]), axis=-1, keepdims=True), axis=0,
                                     keepdims=True)

            val, vjp = jax.vjp(f, x_ref[...], g_ref[...])
            dx, dg = vjp(jnp.ones((1, 1), F32))
            dx_o[...] = dx
            loss_o[...] += jnp.broadcast_to(val, (8, 128))
            dg_o[...] += dg

        @pl.when(i >= nlt)
        def _():
            dx_o[...] = jnp.zeros_like(dx_o)

    row = pl.BlockSpec((TR, D), lambda i: (i, 0))
    vec = pl.BlockSpec((1, D), lambda i: (0, 0))
    return _pc(body, "loss_head", [_sds((8, 128)), _sds((T, D)), _sds((1, D))], grid=(T // TR,),
               in_specs=[row, vec, pl.BlockSpec((TR, D), lambda i: (jnp.minimum(i, nlt - 1), 0))],
               out_specs=[pl.BlockSpec((8, 128), lambda i: (0, 0)), row, vec])(X2, g, tgt)


def _stack_impl(q):
    lane = _iota(q.shape, 1)
    return jnp.concatenate([jnp.where(lane < HD, q, 0.0), jnp.where(lane >= HD, q, 0.0)], axis=0)


def _unstack_impl(o):
    M = o.shape[0] // 2
    return jnp.where(_iota((M, o.shape[1]), 1) < HD, o[:M], o[M:])


@jax.custom_vjp
def _stack(q):
    return _stack_impl(q)


_stack.defvjp(lambda q: (_stack_impl(q), None), lambda _, g: (_unstack_impl(g),))


@jax.custom_vjp
def _unstack(o):
    return _unstack_impl(o)


_unstack.defvjp(lambda o: (_unstack_impl(o), None), lambda _, g: (_stack_impl(g),))


def _softmax_av(q, ks, vs, biases, sink):
    q2 = _stack(q)
    ss = []
    for k, b in zip(ks, biases):
        s = mm_nt(q2, k) * (HD ** -0.5)
        ss.append(s if b is None else s + b)
    m = functools.reduce(jnp.maximum, [jnp.max(s, axis=1, keepdims=True) for s in ss])
    if sink is not None:
        m = jnp.maximum(m, sink)
    m = lax.stop_gradient(m)
    es = [jnp.exp(s - m) for s in ss]
    den = functools.reduce(lambda a, b_: a + b_, [jnp.sum(e, axis=1, keepdims=True) for e in es])
    if sink is not None:
        den = den + jnp.exp(sink - m)
    inv = 1.0 / den
    return _unstack(functools.reduce(lambda a, b_: a + b_, [mm(e * inv, v) for e, v in zip(es, vs)]))


def _sink_col(s0, s1, M):
    return jnp.concatenate([jnp.broadcast_to(jnp.mean(s0, axis=1, keepdims=True), (M, 1)),
                            jnp.broadcast_to(jnp.mean(s1, axis=1, keepdims=True), (M, 1))], axis=0)


def _stack4_impl(q):
    lane = _iota((q.shape[0], 128), 1)
    parts = []
    for p in range(2):
        qp = q[:, 128 * p:128 * (p + 1)]
        parts += [jnp.where(lane < HD, qp, 0.0), jnp.where(lane >= HD, qp, 0.0)]
    return jnp.concatenate(parts, axis=0)


def _unstack4_impl(o):
    M = o.shape[0] // 4
    lane = _iota((M, 128), 1)
    return jnp.concatenate([jnp.where(lane < HD, o[0:M], o[M:2 * M]),
                            jnp.where(lane < HD, o[2 * M:3 * M], o[3 * M:4 * M])], axis=1)


@jax.custom_vjp
def _stack4(q):
    return _stack4_impl(q)


_stack4.defvjp(lambda q: (_stack4_impl(q), None), lambda _, g: (_unstack4_impl(g),))


@jax.custom_vjp
def _unstack4(o):
    return _unstack4_impl(o)


_unstack4.defvjp(lambda o: (_unstack4_impl(o), None), lambda _, g: (_stack4_impl(g),))


WA_NB = 4


def _wa_blocks(qs, kws, vws, kx, vx, sks, n0, L):
    sc = HD ** -0.5
    sink = jnp.concatenate([jnp.broadcast_to(jnp.mean(s_, axis=1, keepdims=True), (Q, 1)) for s_ in sks], axis=0)
    bias = []
    for b_ in range(len(qs)):
        n = n0 + b_
        qpos = n * Q + (_iota((4 * Q, 3 * Q), 0) & (Q - 1))
        kpos = (n - 1) * Q + _iota((4 * Q, 3 * Q), 1)
        bias.append(jnp.where((jnp.abs(qpos - kpos) <= Q) & (kpos >= 0) & (kpos < L), 0.0, NEG))
    q4 = [_stack4(q) for q in qs]
    sl = [mm_nt(a, k) * sc + b_ for a, k, b_ in zip(q4, kws, bias)]
    sx = [mm_nt(a, kx) * sc for a in q4]
    m = [lax.stop_gradient(jnp.maximum(jnp.maximum(jnp.max(a, axis=1, keepdims=True),
                                                   jnp.max(b_, axis=1, keepdims=True)), sink))
         for a, b_ in zip(sl, sx)]
    el = [jnp.exp(a - c) for a, c in zip(sl, m)]
    ex = [jnp.exp(a - c) for a, c in zip(sx, m)]
    inv = [1.0 / (jnp.sum(a, axis=1, keepdims=True) + jnp.sum(b_, axis=1, keepdims=True) + jnp.exp(sink - c))
           for a, b_, c in zip(el, ex, m)]
    return [_unstack4(mm(a * i, v) + mm(b_ * i, vx)) for a, b_, i, v in zip(el, ex, inv, vws)]


def _wa_load(q_r, k_r, v_r, n0):
    f = lambda t: t.astype(F32)
    qs = [f(q_r[b_ * Q:(b_ + 1) * Q, :]) for b_ in range(WA_NB)]
    wins = [pl.ds(pl.multiple_of((n0 + b_) * Q, Q), 3 * Q) for b_ in range(WA_NB)]
    return qs, [f(k_r[w, :]) for w in wins], [f(v_r[w, :]) for w in wins], wins


def _wa_specs(L):
    nb = L // Q
    qs = pl.BlockSpec((WA_NB * Q, 256), lambda n: (n, 0))
    kfull = pl.BlockSpec((L + LC + Q, 128), lambda n: (0, 0))
    sks = pl.BlockSpec((2, 2, 1, 128), lambda n: (0, 0, 0, 0))
    return nb, qs, kfull, sks


def wa_fwd(QA, KA, VA, sinkp, L, sends=()):
    nb, qs, kfull, sks = _wa_specs(L)
    pad = lambda a: jnp.concatenate([jnp.zeros((Q, 128), a.dtype), a], axis=0)

    def body(q_r, k_r, v_r, sk_r, o_ref):
        n0 = pl.program_id(0) * WA_NB
        qs_, kws, vws, _ = _wa_load(q_r, k_r, v_r, n0)
        cx = pl.ds(Q + L, LC)
        outs = _wa_blocks(qs_, kws, vws, k_r[cx, :].astype(F32), v_r[cx, :].astype(F32),
                          [sk_r[0, 0], sk_r[0, 1], sk_r[1, 0], sk_r[1, 1]], n0, L)
        o_ref[...] = jnp.concatenate(outs, axis=0)

    return _pc(body, "wa_fwd", _sds((L, 256)), grid=(nb // WA_NB,), in_specs=[qs, kfull, kfull, sks], out_specs=qs,
               sends=sends, gather=True)(QA, pad(KA), pad(VA), sinkp)


def wa_bwd(QA, KA, VA, sinkp, dO, L, sends=()):
    nb, qs, kfull, sks = _wa_specs(L)
    pad = lambda a: jnp.concatenate([jnp.zeros((Q, 128), a.dtype), a], axis=0)

    def body(q_r, k_r, v_r, sk_r, do_r, dq_o, dk_o, dv_o, dsk_o):
        n0 = pl.program_id(0) * WA_NB
        _acc_init(n0 == 0, [dk_o, dv_o, dsk_o])
        qs_, kws, vws, wins = _wa_load(q_r, k_r, v_r, n0)
        cx = pl.ds(Q + L, LC)
        fn = lambda a, b, c, d, e, s_: _wa_blocks(a, b, c, d, e, s_, n0, L)
        _, vjp = jax.vjp(fn, qs_, kws, vws, k_r[cx, :].astype(F32), v_r[cx, :].astype(F32),
                         [sk_r[0, 0], sk_r[0, 1], sk_r[1, 0], sk_r[1, 1]])
        dqs, dkws, dvws, dkx, dvx, ds = vjp([do_r[b_ * Q:(b_ + 1) * Q, :] for b_ in range(WA_NB)])
        dq_o[...] = jnp.concatenate(dqs, axis=0)
        for w, dk, dv in zip(wins, dkws, dvws):
            dk_o[w, :] += dk
            dv_o[w, :] += dv
        dk_o[cx, :] += dkx
        dv_o[cx, :] += dvx
        for i_ in range(4):
            dsk_o[i_ // 2, i_ % 2] += ds[i_]

    return _pc(body, "wa_bwd", [_sds((L, 256)), _sds((L + LC + Q, 128)), _sds((L + LC + Q, 128)),
                                _sds((2, 2, 1, 128))],
               grid=(nb // WA_NB,), in_specs=[qs, kfull, kfull, sks, qs], out_specs=[qs, kfull, kfull, sks],
               sends=sends)(QA, pad(KA), pad(VA), sinkp, dO)


def _ctx_block(q, kx, vx, s0, s1):
    return _softmax_av(q, [kx], [vx], [None], _sink_col(s0, s1, LC))


def ctx_fwd(Qx, Kx, Vx, sinkp, shared, L):
    cq = pl.BlockSpec((LC, 128), lambda p: (L // LC, p))
    ck = pl.BlockSpec((LC, 128), lambda p: (L // LC, 0 if shared else p))
    sks = pl.BlockSpec((1, 2, 1, 128), lambda p: (p, 0, 0, 0))

    def body(q_r, k_r, v_r, sk_r, o_ref):
        f = lambda t: t[...].astype(F32)
        o_ref[...] = _ctx_block(f(q_r), f(k_r), f(v_r), sk_r[0, 0], sk_r[0, 1])

    return _pc(body, "ctx_fwd", _sds((LC, 256)), grid=(2,), in_specs=[cq, ck, ck, sks],
               out_specs=pl.BlockSpec((LC, 128), lambda p: (0, p)))(Qx, Kx, Vx, sinkp)


def ctx_bwd(Qx, Kx, Vx, sinkp, dO, shared, L):
    cq = pl.BlockSpec((LC, 128), lambda p: (L // LC, p))
    ck = pl.BlockSpec((LC, 128), lambda p: (L // LC, 0 if shared else p))
    sks = pl.BlockSpec((1, 2, 1, 128), lambda p: (p, 0, 0, 0))
    op = pl.BlockSpec((LC, 128), lambda p: (0, p))
    ok = pl.BlockSpec((LC, 128), lambda p: (0, 0 if shared else p))
    dos = pl.BlockSpec((LC, 128), lambda p: (L // LC, p))

    def body(q_r, k_r, v_r, sk_r, do_r, dq_o, dk_o, dv_o, dsk_o):
        p = pl.program_id(0)
        f = lambda t: t[...].astype(F32)
        _, vjp = jax.vjp(_ctx_block, f(q_r), f(k_r), f(v_r), sk_r[0, 0], sk_r[0, 1])
        dq, dk, dv, ds0, ds1 = vjp(do_r[...])
        dq_o[...] = dq
        _acc_init((p == 0) if shared else (p >= 0), [dk_o, dv_o])
        dk_o[...] += dk
        dv_o[...] += dv
        dsk_o[0, 0] = ds0
        dsk_o[0, 1] = ds1

    kw = 128 if shared else 256
    return _pc(body, "ctx_bwd", [_sds((LC, 256)), _sds((LC, kw)), _sds((LC, kw)), _sds((2, 2, 1, 128))],
               grid=(2,), in_specs=[cq, ck, ck, sks, dos], out_specs=[op, ok, ok, sks])(Qx, Kx, Vx, sinkp, dO)


def _na_rows(qs, kws, vws, kx, vx, bs):
    sc = HD ** -0.5
    q2 = [_stack(q) for q in qs]
    sl = [mm_nt(a, k) * sc + b for a, k, b in zip(q2, kws, bs)]
    sx = [mm_nt(a, kx) * sc for a in q2]
    m = [lax.stop_gradient(jnp.maximum(jnp.max(a, axis=1, keepdims=True), jnp.max(b, axis=1, keepdims=True)))
         for a, b in zip(sl, sx)]
    el = [jnp.exp(a - c) for a, c in zip(sl, m)]
    ex = [jnp.exp(a - c) for a, c in zip(sx, m)]
    inv = [1.0 / (jnp.sum(a, axis=1, keepdims=True) + jnp.sum(b, axis=1, keepdims=True)) for a, b in zip(el, ex)]
    o2 = [mm(a * i, v) + mm(b * i, vx) for a, b, i, v in zip(el, ex, inv, vws)]
    return [_unstack(o) for o in o2]


NA_ROWS = 16


def _na_geom(r, R):
    s = jnp.clip(r - 4, 0, R - 8)
    cls = jnp.where(r < 4, r, jnp.where(r > R - 4, r - (R - 8), 4))
    return pl.ds(pl.multiple_of(s * GW, GW), 8 * GW), cls


def _na_load(q_r, k_r, v_r, b_r, rb, R):
    nr = min(NA_ROWS, R)
    geo = [_na_geom(rb * nr + j, R) for j in range(nr)]
    qs = [q_r[j * GW:(j + 1) * GW, :].astype(F32) for j in range(nr)]
    kws = [k_r[win, :].astype(F32) for win, _ in geo]
    vws = [v_r[win, :].astype(F32) for win, _ in geo]
    bs = [jnp.concatenate([b_r[0, cls], b_r[1, cls]], axis=0) for _, cls in geo]
    return geo, qs, kws, vws, bs


def na_fwd(QB, KB, VB, biasd, L, sends=()):
    R = L // GW
    nr = min(NA_ROWS, R)
    qs = pl.BlockSpec((nr * GW, 128), lambda p, rb: (rb, p))
    kfull = pl.BlockSpec((L, 128), lambda p, rb: (0, p))
    kctx = pl.BlockSpec((LC, 128), lambda p, rb: (L // LC, p))
    bs = pl.BlockSpec((2, 8, GW, 8 * GW), lambda p, rb: (p, 0, 0, 0))

    def body(q_r, k_r, v_r, kx_r, vx_r, b_r, o_ref):
        _, qs_, kws, vws, bs_ = _na_load(q_r, k_r, v_r, b_r, pl.program_id(1), R)
        outs = _na_rows(qs_, kws, vws, kx_r[...].astype(F32), vx_r[...].astype(F32), bs_)
        o_ref[...] = jnp.concatenate(outs, axis=0)

    return _pc(body, "na_fwd", _sds((L, 256)), grid=(2, R // nr), in_specs=[qs, kfull, kfull, kctx, kctx, bs],
               out_specs=qs, sends=sends, gather=True)(QB, KB, VB, KB, VB, biasd)


def na_bwd(QB, KB, VB, biasd, dO, L):
    R = L // GW
    nr = min(NA_ROWS, R)
    qs = pl.BlockSpec((nr * GW, 128), lambda p, rb: (rb, p))
    kfull = pl.BlockSpec((L, 128), lambda p, rb: (0, p))
    kctx = pl.BlockSpec((LC, 128), lambda p, rb: (L // LC, p))
    bs = pl.BlockSpec((2, 8, GW, 8 * GW), lambda p, rb: (p, 0, 0, 0))
    oc = pl.BlockSpec((LC, 128), lambda p, rb: (0, p))

    def body(q_r, k_r, v_r, kx_r, vx_r, b_r, do_r, dq_o, dk_o, dv_o, dkx_o, dvx_o, db_o):
        rb = pl.program_id(1)
        _acc_init(rb == 0, [dk_o, dv_o, dkx_o, dvx_o, db_o])
        geo, qs_, kws, vws, bs_ = _na_load(q_r, k_r, v_r, b_r, rb, R)
        _, vjp = jax.vjp(_na_rows, qs_, kws, vws, kx_r[...].astype(F32), vx_r[...].astype(F32), bs_)
        dqs, dkws, dvws, dkx, dvx, dbs = vjp([do_r[j * GW:(j + 1) * GW, :] for j in range(nr)])
        dq_o[...] = jnp.concatenate(dqs, axis=0)
        dkx_o[...] += dkx
        dvx_o[...] += dvx
        for j, (win, cls) in enumerate(geo):
            dk_o[win, :] += dkws[j]
            dv_o[win, :] += dvws[j]
            db_o[0, cls] += dbs[j][:GW]
            db_o[1, cls] += dbs[j][GW:]

    return _pc(body, "na_bwd",
               [_sds((L, 256)), _sds((L, 256)), _sds((L, 256)), _sds((LC, 256)), _sds((LC, 256)),
                _sds((4, 8, GW, 8 * GW))],
               grid=(2, R // nr), in_specs=[qs, kfull, kfull, kctx, kctx, bs, qs],
               out_specs=[qs, kfull, kfull, oc, oc, bs])(QB, KB, VB, KB, VB, biasd, dO)


def exact_mm_call(A, B):
    def body(a_ref, b_ref, o_ref):
        o_ref[...] = _exact(a_ref[...], b_ref[...])

    return _pc(body, "exact_mm", _sds((A.shape[0], B.shape[1])))(A, B)


def _conv_shift(x, d, L):
    T = x.shape[0]
    if d == 0:
        return x
    t = _iota(x.shape, 0)
    src = t + d
    ok = (src >= 0) & (src < T) & ((src >= L) == (t >= L))
    return jnp.where(ok, pltpu.roll(x, (-d) % T, 0), 0.0)


def conv_fwd(XBC, w8, b, L, sends=()):
    T = XBC.shape[0]

    def body(x_ref, w_ref, b_ref, o_ref):
        x = x_ref[...]
        pre = b_ref[...] + functools.reduce(
            lambda a, c: a + c, [_conv_shift(x, k - 3, L) * w_ref[k:k + 1, :] for k in range(7)])
        o_ref[...] = _silu(pre)

    col = pl.BlockSpec((T, 128), lambda j: (0, j))
    return _pc(body, "conv_fwd", _sds((T, 1024)), grid=(8,),
               in_specs=[col, pl.BlockSpec((8, 128), lambda j: (0, j)), pl.BlockSpec((1, 128), lambda j: (0, j))],
               out_specs=col, sends=sends, gather=True)(XBC, w8, b)


def conv_bwd(XBC, w8, b, dS, dxs_skip, L, sends=()):
    T = XBC.shape[0]

    def body(x_ref, w_ref, b_ref, d0_r, d1_r, dsk_r, dx_o, dw_o, db_o):
        j = pl.program_id(0)
        x = x_ref[...]
        xs = [_conv_shift(x, k - 3, L) for k in range(7)]
        pre = b_ref[...] + functools.reduce(lambda a, c: a + c, [xs[k] * w_ref[k:k + 1, :] for k in range(7)])
        _, vjp = jax.vjp(_silu, pre)
        dact = d0_r[0] + d1_r[0] + jnp.where(j < 4, dsk_r[...], 0.0)
        dpre, = vjp(dact)
        dx_o[...] = functools.reduce(
            lambda a, c: a + c, [_conv_shift(dpre, 3 - k, L) * w_ref[k:k + 1, :] for k in range(7)])
        dw_o[...] = jnp.concatenate([jnp.sum(dpre * xs[k], axis=0, keepdims=True) for k in range(7)]
                                    + [jnp.zeros((1, 128), F32)], axis=0)
        db_o[...] = jnp.sum(dpre, axis=0, keepdims=True)

    col = pl.BlockSpec((T, 128), lambda j: (0, j))
    w_s = pl.BlockSpec((8, 128), lambda j: (0, j))
    b_s = pl.BlockSpec((1, 128), lambda j: (0, j))
    ds = lambda d: pl.BlockSpec((1, T, 128), lambda j: (d, 0, j))
    return _pc(body, "conv_bwd", [_sds((T, 1024)), _sds((8, 1024)), _sds((1, 1024))], grid=(8,),
               in_specs=[col, w_s, b_s, ds(0), ds(1), pl.BlockSpec((T, 128), lambda j: (0, jnp.minimum(j, 3)))],
               out_specs=[col, w_s, b_s], sends=sends)(XBC, w8, b, dS, dS, dxs_skip)


def _ssd_chunk(xs, bs, cs, dtraw, dtb, alog, hs, tri, d):
    dt = _softplus(dtraw + dtb)
    a = dt * (-jnp.exp(alog))
    acum = _exact(tri, a)
    tot = jnp.sum(a, axis=0, keepdims=True)
    wcol = jnp.exp(tot - acum) * dt
    ea = jnp.exp(acum)
    cd = jnp.exp(tot)
    acum_t, dt_t = acum.T, dt.T
    lane = _iota((Q, 128), 1)
    srow = _iota((128, Q), 0)
    lane1 = _iota((1, 128), 1)
    prow = _iota((128, NSTATE), 0)
    mask = tri > 0.5
    cbs = [mm_nt(cs[g], bs[g]) for g in range(2)]
    ys, hn = [], []
    for j in range(4):
        g = j // 2
        x = xs[j]
        yi, st, eac, cdl = [], [], [], []
        for u in range(2):
            slot = d * 8 + 2 * j + u
            col = lambda m: jnp.sum(jnp.where(lane == slot, m, 0.0), axis=1, keepdims=True)
            rowv = lambda m: jnp.sum(jnp.where(srow == slot, m, 0.0), axis=0, keepdims=True)
            seg = col(acum) - rowv(acum_t)
            dcy = jnp.where(mask, jnp.exp(jnp.where(mask, seg, 0.0)), 0.0)
            yi.append(mm(cbs[g] * dcy * rowv(dt_t), x))
            st.append(mm_tn(x, bs[g] * col(wcol)))
            eac.append(col(ea))
            cdl.append(jnp.sum(jnp.where(lane1 == slot, cd, 0.0), axis=1, keepdims=True))
        yin = mm_nt(cs[g], hs[j])
        ys.append(jnp.where(lane < HD, yi[0] + yin * eac[0], yi[1] + yin * eac[1]))
        hn.append(hs[j] * jnp.where(prow < HD, cdl[0], cdl[1]) + jnp.where(prow < HD, st[0], st[1]))
    return ys, hn


SSD_SUB = 2


def _ssd_block_idx(d, s, nlb, nbk):
    return jnp.where(d == 0, (s + nlb) % nbk, nbk - 1 - s)


def _ssd_rows(d, i):
    return pl.ds(pl.multiple_of(jnp.where(d == 0, i, SSD_SUB - 1 - i) * Q, Q), Q)


def _ssd_split(a):
    return ([a[:, 128 * j:128 * (j + 1)] for j in range(4)], [a[:, 512 + 128 * g:640 + 128 * g] for g in range(2)],
            [a[:, 768 + 128 * g:896 + 128 * g] for g in range(2)])


def ssd_fwd(ACT, DT, dtb, alog, tri2, L, sends=()):
    T = ACT.shape[0]
    RB = SSD_SUB * Q
    nlb, nbk = L // RB, T // RB

    def body(a_ref, dt_ref, dtb_ref, al_ref, tri_ref, y_o, hs_o, hst):
        d, s = pl.program_id(0), pl.program_id(1)
        _acc_init(s == 0, [hst])
        for i in range(SSD_SUB):
            rows = _ssd_rows(d, i)
            xs, bs, cs = _ssd_split(a_ref[rows, :])
            hs_o[0, i] = hst[...]
            ys, hn = _ssd_chunk(xs, bs, cs, dt_ref[rows, :], dtb_ref[...], al_ref[...], [hst[j] for j in range(4)],
                                tri_ref[0], d)
            y_o[0, rows, :] = jnp.concatenate(ys, axis=1)
            for j in range(4):
                hst[j] = hn[j]

    bk = lambda w: pl.BlockSpec((RB, w), lambda d, s: (_ssd_block_idx(d, s, nlb, nbk), 0))
    v128 = pl.BlockSpec((1, 128), lambda d, s: (0, 0))
    return _pc(body, "ssd_fwd", [_sds((2, T, 512)), _sds((2, T // Q, 4, 128, NSTATE))], grid=(2, nbk),
               in_specs=[bk(1024), bk(128), v128, v128, pl.BlockSpec((1, Q, Q), lambda d, s: (d, 0, 0))],
               out_specs=[pl.BlockSpec((1, RB, 512), lambda d, s: (d, _ssd_block_idx(d, s, nlb, nbk), 0)),
                          pl.BlockSpec((1, SSD_SUB, 4, 128, NSTATE), lambda d, s: (d, s, 0, 0, 0))],
               scratch=[pltpu.VMEM((4, 128, NSTATE), F32)], sends=sends, gather=True)(ACT, DT, dtb, alog, tri2)


def ssd_bwd(ACT, DT, dtb, alog, tri2, HS, dY, L, sends=()):
    T = ACT.shape[0]
    RB = SSD_SUB * Q
    nlb, nbk = L // RB, T // RB

    def body(a_ref, dt_ref, dtb_ref, al_ref, tri_ref, hs_ref, dy_ref, da_o, ddt_o, ddtb_o, dal_o, dh):
        d, sr = pl.program_id(0), pl.program_id(1)
        _acc_init(sr == 0, [dh, ddtb_o, dal_o])
        tri = tri_ref[0]
        fn = lambda xs_, bs_, cs_, dtr, dtb_, al, hs_: _ssd_chunk(xs_, bs_, cs_, dtr, dtb_, al, hs_, tri, d)
        for i in reversed(range(SSD_SUB)):
            rows = _ssd_rows(d, i)
            xs, bs, cs = _ssd_split(a_ref[rows, :])
            _, vjp = jax.vjp(fn, xs, bs, cs, dt_ref[rows, :], dtb_ref[...], al_ref[...],
                             [hs_ref[0, i, j] for j in range(4)])
            dy = dy_ref[rows, :]
            dxs, dbs, dcs, ddt, ddtb, dal, dhs = vjp(([dy[:, 128 * j:128 * (j + 1)] for j in range(4)],
                                                      [dh[j] for j in range(4)]))
            da_o[0, rows, :] = jnp.concatenate(dxs + dbs + dcs, axis=1)
            ddt_o[0, rows, :] = ddt
            ddtb_o[0] += ddtb
            dal_o[0] += dal
            for j in range(4):
                dh[j] = dhs[j]

    bidx = lambda d, sr: _ssd_block_idx(d, nbk - 1 - sr, nlb, nbk)
    bk = lambda w: pl.BlockSpec((RB, w), lambda d, sr: (bidx(d, sr), 0))
    v128 = pl.BlockSpec((1, 128), lambda d, sr: (0, 0))
    o128 = pl.BlockSpec((1, 1, 128), lambda d, sr: (d, 0, 0))
    return _pc(body, "ssd_bwd", [_sds((2, T, 1024)), _sds((2, T, 128)), _sds((2, 1, 128)), _sds((2, 1, 128))],
               grid=(2, nbk),
               in_specs=[bk(1024), bk(128), v128, v128, pl.BlockSpec((1, Q, Q), lambda d, sr: (d, 0, 0)),
                         pl.BlockSpec((1, SSD_SUB, 4, 128, NSTATE), lambda d, sr: (d, nbk - 1 - sr, 0, 0, 0)), bk(512)],
               out_specs=[pl.BlockSpec((1, RB, 1024), lambda d, sr: (d, bidx(d, sr), 0)),
                          pl.BlockSpec((1, RB, 128), lambda d, sr: (d, bidx(d, sr), 0)), o128, o128],
               scratch=[pltpu.VMEM((4, 128, NSTATE), F32)], sends=sends)(ACT, DT, dtb, alog, tri2, HS, dY)


_PAIR_HEADS = np.array([[0, 2], [1, 3]])


def _tables(L):
    t = jnp.arange(L)
    inv = 10000.0 ** (-jnp.arange(16, dtype=F32) / 16)

    def half(pos):
        ang = pos.astype(F32)[:, None] * inv[None, :]
        return jnp.concatenate([ang, ang], axis=1)

    ang = jnp.tile(jnp.concatenate([half(t // GW), half(t % GW)], axis=1), (1, 4))
    cos = jnp.concatenate([jnp.cos(ang), jnp.ones((LC, 256), F32)], axis=0)
    sin = jnp.concatenate([jnp.sin(ang), jnp.zeros((LC, 256), F32)], axis=0)
    rm = np.zeros((256, 256), np.float32)
    for j in range(256):
        if j % 32 < 16:
            rm[j + 16, j] = -1.0
        else:
            rm[j - 16, j] = 1.0
    tri = np.tril(np.ones((Q, Q), np.float32))
    return cos, sin, jnp.asarray(rm), jnp.asarray(np.stack([tri, tri.T]))


def _na_index(R):
    rc = np.array([0, 1, 2, 3, 4, R - 3, R - 2, R - 1])
    dy = np.clip(rc - 4, 0, R - 8)[:, None] + np.arange(8)[None, :] - rc[:, None] + 7
    qc, cc = np.arange(GW)[:, None], np.arange(GW)[None, :]
    dx = np.clip(cc - qc, -15, 15) + 15
    cstart = np.clip(qc - 8, 0, GW - 16)
    cmask = (cc >= cstart) & (cc < cstart + 16)
    idx = dy[:, None, :, None] * 31 + dx[None, :, None, :]
    return idx.reshape(8, GW, 8 * GW), np.broadcast_to(cmask[None, :, None, :], idx.shape).reshape(8, GW, 8 * GW), \
        dy, dx, cmask


def _na_bias(rpb, R):
    _, cm, dy, _, _ = _na_index(R)
    rows = rpb[:, dy.reshape(-1), :].reshape(4, 8, 4, 2, 31)
    p2 = jnp.pad(jnp.pad(rows, ((0, 0),) * 4 + ((0, 33),)).reshape(4, 8, 4, 128), ((0, 0), (0, 0), (0, 4), (0, 0)))
    negmask = jnp.asarray(np.where(cm[0], 0.0, NEG).astype(np.float32))

    def body(p_ref, m_ref, o_ref):
        for c in range(8):
            tiles = [pltpu.roll(jnp.broadcast_to(p_ref[0, c, jp:jp + 1, :], (GW, 128)), 113, 1, stride=1,
                                stride_axis=0) for jp in range(4)]
            o_ref[0, c] = jnp.where(m_ref[...] < 0.0, NEG, jnp.concatenate(tiles, axis=1))

    return _pc(body, "na_bias", _sds((4, 8, GW, 8 * GW)), grid=(4,),
               in_specs=[pl.BlockSpec((1, 8, 8, 128), lambda h: (h, 0, 0, 0)),
                         pl.BlockSpec((GW, 8 * GW), lambda h: (0, 0))],
               out_specs=pl.BlockSpec((1, 8, GW, 8 * GW), lambda h: (h, 0, 0, 0)))(p2, negmask)


def _na_bias_grad(dbias, R):
    _, _, dy, dx, cmask = _na_index(R)
    e1 = np.zeros((GW * GW, 128), np.float32)
    e1[np.arange(GW * GW), dx.reshape(-1)] = cmask.reshape(-1)
    a1 = dbias.reshape(4, 8, GW, 8, GW).transpose(0, 1, 3, 2, 4).reshape(256, GW * GW)
    v = exact_mm_call(a1, jnp.asarray(e1))[:, :31].reshape(4, 64, 31)
    e2 = np.zeros((64, 128), np.float32)
    e2[np.arange(64), dy.reshape(-1)] = 1.0
    a2 = jnp.pad(v.transpose(0, 2, 1).reshape(124, 64), ((0, 4), (0, 0)))
    return exact_mm_call(a2, jnp.asarray(e2))[:124, :15].reshape(4, 31, 15).transpose(0, 2, 1)


def _lanes(v, n=128):
    v = v.reshape(1, -1)
    return jnp.pad(v, ((0, 0), (0, n - v.shape[1])))


def _cls2(a, b):
    return jnp.stack([a, b]).reshape(2, 1, D)


def _win_p(g):
    return jnp.concatenate([g.reshape(IN_COLS, D), jnp.zeros((NP_IN - IN_COLS, D), g.dtype)], axis=0)


def _layer_consts(p):
    sinkp = jnp.broadcast_to(p["wa_sink"][_PAIR_HEADS][:, :, None, None], (2, 2, 1, 128))
    return dict(
        sinkp=sinkp, nosink=jnp.full((2, 2, 1, 128), NEG, F32),
        w8=jnp.concatenate([p["ssm_conv_w"], jnp.zeros((1, 1024), F32)], axis=0),
        cb=p["ssm_conv_b"].reshape(1, 1024), dtb=_lanes(p["ssm_dt_bias"]), alog=_lanes(p["ssm_a_log"]),
        dsk=jnp.repeat(p["ssm_d"], HD).reshape(1, 512), gs=p["ssm_norm_g"].reshape(1, 512),
        gmix=p["g_mix"].reshape(1, D), gffn=p["g_ffn"].reshape(1, D))


def _mods(mod2):
    return [_cls2(mod2[0, D * k:D * (k + 1)], mod2[1, D * k:D * (k + 1)]) for k in range(6)]


def _layer_fwd(X, mod2, c, rpb, tabs, L, ctx_out, shards, nxt):
    cos, sin, rm, tri2 = tabs
    sh1, sc1, gt1, sh2, sc2, gt2 = _mods(mod2)
    biasd = _na_bias(rpb, L // GW)
    fi, fo, wo = shards
    fcut, fcut2, ocut = 384, 576, 224
    (qa, qb, z, ka, va, kb, vb, xbc, dt, h1), (gfo_a,) = in_fwd(X, c["gmix"], sh1, sc1, c["win"], cos, sin, rm, L,
                                                                sends=(fo[:ocut],))
    (oa,), (gfi_b,) = wa_fwd(qa, ka, va, c["sinkp"], L, sends=(fi[fcut:fcut2],))
    (ob,), (gwo,) = na_fwd(qb, kb, vb, biasd, L, sends=(wo,))
    c = dict(c, wout=gwo.reshape(D, D))
    if ctx_out:
        oa_c = ctx_fwd(qa, ka, va, c["sinkp"], True, L)
        ob_c = ctx_fwd(qb, kb, vb, c["nosink"], False, L)
    else:
        oa_c = ob_c = jnp.zeros((LC, 256), F32)
    oa = jnp.concatenate([oa, oa_c], axis=0)
    ob = jnp.concatenate([ob, ob_c], axis=0)
    (act,), (gfi_c,) = conv_fwd(xbc, c["w8"], c["cb"], L, sends=(fi[fcut2:],))
    (y2, hs), (gfi_a,) = ssd_fwd(act, dt, c["dtb"], c["alog"], tri2, L, sends=(fi[:fcut],))
    (X1, cat), (gfo_b,) = out_fwd(oa, ob, y2, act, z, c["dsk"], c["gs"], c["wout"], X, gt1, L, sends=(fo[ocut:],))
    c = dict(c, wfi=jnp.concatenate([gfi_a, gfi_b, gfi_c], axis=1).reshape(2 * DFF, D),
             wfo=jnp.concatenate([gfo_a, gfo_b], axis=1).reshape(DFF, D))
    res = ffn_fwd(X1, c["gffn"], sh2, sc2, gt2, c["wfi"], c["wfo"], L, sends=nxt, skip_ctx=not ctx_out)
    (X2, ff), got = res if nxt else (res, ())
    saved = dict(X=X, X1=X1, ff=ff, qa=qa, qb=qb, z=z, ka=ka, va=va, kb=kb, vb=vb, xbc=xbc, dt=dt, h1=h1, oa=oa, ob=ob,
                 act=act, y2=y2, hs=hs, cat=cat, biasd=biasd)
    return X2, saved, c, got


def _row_blocks(gw):
    return gw.reshape(NDEV, gw.shape[0] // NDEV, gw.shape[1])


def _layer_bwd(dX2, s, mod2, c, tabs, L, ctx_out, carry):
    cos, sin, rm, tri2 = tabs
    sh1, sc1, gt1, sh2, sc2, gt2 = _mods(mod2)
    R = L // GW
    res = ffn_bwd(s["X1"], c["gffn"], sh2, sc2, gt2, c["wfi"], c["wfo"], s["ff"], dX2, L, sends=carry,
                  skip_ctx=not ctx_out)
    (dX1, h2, dU, actf, dOut, dgffn, dsh2, dsc2, dgt2), got = res if carry else (res, ())
    g = {}
    lat = None if ctx_out else L
    gfi = _row_blocks(tn_mm(dU, h2, 512, 1024, MXU, rows=lat))
    gfo = _row_blocks(tn_mm(actf, dOut, 256, 1024, MXU, rows=lat))
    doa, dob, dy, dxs_skip, dz, dmix, ddsk, dgs, dgt1 = out_bwd(s["oa"], s["ob"], s["y2"], s["act"], s["z"], c["dsk"],
                                                                c["gs"], c["wout"], gt1, dX1, L)
    gout = _row_blocks(tn_mm(s["cat"], dmix, 512, 1024, MXU, rows=lat))
    (dS, ddt2, ddtb, dal), (g["w_ffn_in"],) = ssd_bwd(
        s["act"], s["dt"], c["dtb"], c["alog"], tri2, s["hs"], dy, L, sends=(gfi,))
    (dxbc, dw8, dcb), (g["w_ffn_out"],) = conv_bwd(s["xbc"], c["w8"], c["cb"], dS, dxs_skip, L, sends=(gfo,))
    (dqa, dka, dva, dska), (g["w_out"],) = wa_bwd(s["qa"], s["ka"], s["va"], c["sinkp"], doa, L, sends=(gout,))
    dka, dva = dka[Q:], dva[Q:]
    dqb, dkb, dvb, dkxb, dvxb, dbias = na_bwd(s["qb"], s["kb"], s["vb"], s["biasd"], dob, L)
    if ctx_out:
        dqa_c, dk1, dv1, dsk1 = ctx_bwd(s["qa"], s["ka"], s["va"], c["sinkp"], doa, True, L)
        dqb_c, dk2, dv2, _ = ctx_bwd(s["qb"], s["kb"], s["vb"], c["nosink"], dob, False, L)
        dka = jnp.concatenate([dka[:L], dka[L:] + dk1], axis=0)
        dva = jnp.concatenate([dva[:L], dva[L:] + dv1], axis=0)
        dska = dska + dsk1
        dkxb, dvxb = dkxb + dk2, dvxb + dv2
    else:
        dqa_c = dqb_c = jnp.zeros((LC, 256), F32)
    cat0 = lambda a, b: jnp.concatenate([a, b], axis=0)
    dX, dycat, dgmix, dsh1, dsc1 = in_bwd(
        s["X"], c["gmix"], sh1, sc1, c["win"], cos, sin, rm, dX1, cat0(dqa, dqa_c), cat0(dqb, dqb_c), dz,
        dka, dva, cat0(dkb, dkxb), cat0(dvb, dvxb), dxbc, ddt2, L,
        latent_only=ctx_out)
    if ctx_out:
        half = lambda k, **kw: tn_mm(dycat, s["h1"], 512, D // 2, MXU, ncol=D // 2, col0=k, **kw)
        (g1,), (got0,) = half(1, sends=(_row_blocks(half(0)[:IN_COLS]),))
        gin = (got0, _row_blocks(g1[:IN_COLS]))
    else:
        gin = _row_blocks(tn_mm(dycat, s["h1"], 512, 1024, MXU)[:IN_COLS])
    g["g_mix"] = dgmix.reshape(D)
    g["g_ffn"] = dgffn.reshape(D)
    sk = jnp.sum(dska, axis=(2, 3))
    g["wa_sink"] = jnp.zeros((4,), F32).at[_PAIR_HEADS.reshape(-1)].set(sk.reshape(-1))
    g["na_rpb"] = _na_bias_grad(dbias, R)
    g["ssm_conv_w"] = dw8[:7]
    g["ssm_conv_b"] = dcb.reshape(1024)
    g["ssm_dt_bias"] = (ddtb[0] + ddtb[1])[0, :16].reshape(2, 8)
    g["ssm_a_log"] = (dal[0] + dal[1])[0, :16].reshape(2, 8)
    g["ssm_d"] = jnp.sum(ddsk.reshape(8, HD), axis=1)
    g["ssm_norm_g"] = dgs.reshape(512)
    dmod2 = jnp.concatenate([dsh1, dsc1, dgt1, dsh2, dsc2, dgt2], axis=2).reshape(2, 6 * D)
    return dX, g, dmod2, gin, got


def local_step(x, ctx, tgt, mods, layers, shards, g_final, L):
    tabs = _tables(L)
    X = (x, ctx)
    consts = [_layer_consts(p) for p in layers]
    saved = []
    got = (shards["w_in_first"],)
    for i in range(2):
        consts[i] = dict(consts[i], win=_win_p(got[0]))
        nxt = (shards["w_in"][1],) if i == 0 else ()
        X, s, consts[i], got = _layer_fwd(X, mods[i], consts[i], layers[i]["na_rpb"], tabs, L, i == 0,
                                          (shards["w_ffn_in"][i], shards["w_ffn_out"][i], shards["w_out"][i]), nxt)
        saved.append(s)
    loss8, dX, dgfin = loss_head(X, g_final.reshape(1, D), tgt, L)
    grads, dmods = [None, None], [None, None]
    dX, grads[1], dmods[1], gin1, _ = _layer_bwd(dX, saved[1], mods[1], consts[1], tabs, L, False, ())
    dX, grads[0], dmods[0], gin0, (grads[1]["w_in"],) = _layer_bwd(dX, saved[0], mods[0], consts[0], tabs, L, True,
                                                                   (gin1,))
    return loss8[0, 0], dX, grads, jnp.stack(dmods), dgfin.reshape(D), gin0


def _place():
    x, y, c = lax.axis_index("x"), lax.axis_index("y"), lax.axis_index("c")
    return x, y, c


def _slot(b):
    return 4 * b[0] + 2 * b[1] + b[2]


def _any():
    return pl.BlockSpec(memory_space=pl.ANY)


def all_gather(xs, name):
    n = len(xs)

    def body(*refs):
        x_refs, o_refs = refs[:n], refs[n:2 * n]
        send_sems, recv_sems, local_sems = refs[2 * n:]
        x, y, c = _place()
        me, sib = (x, y, c), (x, y, 1 - c)
        chips = [(1 - x, y), (x, 1 - y), (1 - x, 1 - y)]

        def copy(t, k, blk, to, src=None):
            dst = o_refs[t].at[_slot(blk)]
            return pltpu.make_async_remote_copy(
                src_ref=dst if src is None else src, dst_ref=dst, send_sem=send_sems.at[7 * t + k],
                recv_sem=recv_sems.at[7 * t + k], device_id=to, device_id_type=MESH_T)

        mine = [pltpu.make_async_copy(x_refs[t], o_refs[t].at[_slot(me)], local_sems.at[t]) for t in range(n)]
        for cp in mine:
            cp.start()
        first = []
        for t in range(n):
            first.append(copy(t, 0, me, sib, src=x_refs[t]))
            first += [copy(t, 1 + j, me, (*chip, c), src=x_refs[t]) for j, chip in enumerate(chips)]
        for cp in first:
            cp.start()
        passed = []
        for j, chip in enumerate(chips):
            for t in range(n):
                copy(t, 1 + j, (*chip, c), me).wait_recv()
                cp = copy(t, 4 + j, (*chip, c), sib)
                cp.start()
                passed.append(cp)
        for t in range(n):
            copy(t, 0, sib, me).wait_recv()
            for j, chip in enumerate(chips):
                copy(t, 4 + j, (*chip, 1 - c), me).wait_recv()
        for cp in first + passed:
            cp.wait_send()
        for cp in mine:
            cp.wait()

    return pl.pallas_call(
        body, name=name, out_shape=[_sds((NDEV,) + a.shape, a.dtype) for a in xs],
        in_specs=[_any()] * n, out_specs=[_any()] * n,
        scratch_shapes=[pltpu.SemaphoreType.DMA((7 * n,)), pltpu.SemaphoreType.DMA((7 * n,)),
                        pltpu.SemaphoreType.DMA((n,))],
        interpret=_INTERPRET)(*xs)


def _a2a_sems(n):
    return [pltpu.SemaphoreType.DMA((7 * n,)), pltpu.SemaphoreType.DMA((7 * n,)), pltpu.SemaphoreType.DMA((n,))]


def _a2a_copies(x_refs, o_refs, send_sems, recv_sems, local_sems):
    n = len(x_refs)
    x, y, c = _place()
    me = (x, y, c)
    flip = lambda v, b: (1 - v) if b else v
    peers = [(flip(x, k >> 2 & 1), flip(y, k >> 1 & 1), flip(c, k & 1)) for k in range(1, NDEV)]
    mine = [pltpu.make_async_copy(x_refs[t].at[_slot(me)], o_refs[t].at[_slot(me)], local_sems.at[t])
            for t in range(n)]

    def copy(t, k, src_slot, dst_slot, to):
        return pltpu.make_async_remote_copy(
            src_ref=x_refs[t].at[src_slot], dst_ref=o_refs[t].at[dst_slot], send_sem=send_sems.at[7 * t + k],
            recv_sem=recv_sems.at[7 * t + k], device_id=to, device_id_type=MESH_T)

    sends = [copy(t, k, _slot(p), _slot(me), p) for t in range(n) for k, p in enumerate(peers)]
    recvs = [copy(t, k, _slot(p), _slot(p), me) for t in range(n) for k, p in enumerate(peers)]
    return mine, sends, recvs


def _ag_copies(x_refs, o_refs, send_sems, recv_sems, local_sems):
    n = len(x_refs)
    x, y, c = _place()
    me = (x, y, c)
    flip = lambda v, b: (1 - v) if b else v
    peers = [(flip(x, k >> 2 & 1), flip(y, k >> 1 & 1), flip(c, k & 1)) for k in range(1, NDEV)]
    mine = [pltpu.make_async_copy(x_refs[t], o_refs[t].at[_slot(me)], local_sems.at[t]) for t in range(n)]

    def copy(t, k, dst_slot, to):
        return pltpu.make_async_remote_copy(
            src_ref=x_refs[t], dst_ref=o_refs[t].at[dst_slot], send_sem=send_sems.at[7 * t + k],
            recv_sem=recv_sems.at[7 * t + k], device_id=to, device_id_type=MESH_T)

    sends = [copy(t, k, _slot(me), p) for t in range(n) for k, p in enumerate(peers)]
    recvs = [copy(t, k, _slot(p), me) for t in range(n) for k, p in enumerate(peers)]
    return mine, sends, recvs


def _ag_start(x_refs, o_refs, send_sems, recv_sems, local_sems):
    mine, sends, _ = _ag_copies(x_refs, o_refs, send_sems, recv_sems, local_sems)
    for cp in mine + sends:
        cp.start()


def _ag_wait(x_refs, o_refs, send_sems, recv_sems, local_sems):
    mine, sends, recvs = _ag_copies(x_refs, o_refs, send_sems, recv_sems, local_sems)
    for cp in recvs:
        cp.wait_recv()
    for cp in sends:
        cp.wait_send()
    for cp in mine:
        cp.wait()


def _a2a_start(x_refs, o_refs, send_sems, recv_sems, local_sems):
    mine, sends, _ = _a2a_copies(x_refs, o_refs, send_sems, recv_sems, local_sems)
    for cp in mine + sends:
        cp.start()


def _a2a_wait(x_refs, o_refs, send_sems, recv_sems, local_sems):
    mine, sends, recvs = _a2a_copies(x_refs, o_refs, send_sems, recv_sems, local_sems)
    for cp in recvs:
        cp.wait_recv()
    for cp in sends:
        cp.wait_send()
    for cp in mine:
        cp.wait()


def adam_reduce(P, w, m, v, name, sends=()):
    n, R, C = P.shape
    br = R // 4 if R % 64 == 0 else R

    def body(p_ref, w_ref, m_ref, v_ref, g_o, d_o, m_o, v_o):
        g = p_ref[0].astype(F32)
        for k in range(1, n):
            g = g + p_ref[k].astype(F32)
        m1 = ADAM_B1 * m_ref[...] + (1.0 - ADAM_B1) * g
        v1 = ADAM_B2 * v_ref[...] + (1.0 - ADAM_B2) * jnp.square(g)
        m_hat = m1 / (1.0 - ADAM_B1 ** ADAM_STEP)
        v_hat = v1 / (1.0 - ADAM_B2 ** ADAM_STEP)
        g_o[...] = g
        d_o[...] = -ADAM_LR * (m_hat / (jnp.sqrt(v_hat) + ADAM_EPS) + ADAM_WD * w_ref[...])
        m_o[...] = m1
        v_o[...] = v1

    blk = pl.BlockSpec((br, C), lambda i: (i, 0))
    return _pc(body, name, [_sds((R, C))] * 4, grid=(R // br,),
               in_specs=[pl.BlockSpec((n, br, C), lambda i: (0, i, 0)), blk, blk, blk], out_specs=[blk] * 4,
               sends=sends)(P, w, m, v)


def adam_layers(P0, P1, w, m, v, name, sends=()):
    n, R, C = P0.shape
    br = R // 4 if R % 64 == 0 else R
    nb = R // br

    def body(p0_ref, p1_ref, w_ref, m_ref, v_ref, g_o, d_o, m_o, v_o):
        def total(p_ref):
            g = p_ref[0].astype(F32)
            for k in range(1, n):
                g = g + p_ref[k].astype(F32)
            return g

        g = jnp.where(pl.program_id(0) == 0, total(p0_ref), total(p1_ref))
        m1 = ADAM_B1 * m_ref[0] + (1.0 - ADAM_B1) * g
        v1 = ADAM_B2 * v_ref[0] + (1.0 - ADAM_B2) * jnp.square(g)
        m_hat = m1 / (1.0 - ADAM_B1 ** ADAM_STEP)
        v_hat = v1 / (1.0 - ADAM_B2 ** ADAM_STEP)
        g_o[0] = g
        d_o[0] = -ADAM_LR * (m_hat / (jnp.sqrt(v_hat) + ADAM_EPS) + ADAM_WD * w_ref[0])
        m_o[0] = m1
        v_o[0] = v1

    blk = pl.BlockSpec((1, br, C), lambda l, i: (l, i, 0))
    p0 = pl.BlockSpec((n, br, C), lambda l, i: (0, jnp.where(l == 0, i, nb - 1), 0))
    p1 = pl.BlockSpec((n, br, C), lambda l, i: (0, jnp.where(l == 1, i, 0), 0))
    return _pc(body, name, [_sds((2, R, C))] * 4, grid=(2, nb), in_specs=[p0, p1, blk, blk, blk],
               out_specs=[blk] * 4, sends=sends)(P0, P1, w, m, v)


def mod_fwd(scin, wmod, bcol):
    def body(s_ref, w_ref, b_ref, o_ref):
        o_ref[0] = mm(_silu(s_ref[...]), w_ref[0]) + b_ref[0]

    return _pc(body, "mod_fwd", _sds((2, 16, 768)), grid=(2,),
               in_specs=[pl.BlockSpec((16, D), lambda l: (0, 0)), pl.BlockSpec((1, D, 768), lambda l: (l, 0, 0)),
                         pl.BlockSpec((1, 1, 768), lambda l: (l, 0, 0))],
               out_specs=pl.BlockSpec((1, 16, 768), lambda l: (l, 0, 0)))(scin, wmod, bcol)


def mod_bwd(scin, wmod, G):
    def body(s_ref, w_ref, g_ref, dw_o, ds_o):
        _, vjp = jax.vjp(lambda s, w: mm(_silu(s), w), s_ref[...], w_ref[0])
        ds, dw = vjp(g_ref[0])
        dw_o[0] = dw
        _acc_init(pl.program_id(0) == 0, [ds_o])
        ds_o[...] += ds

    full = pl.BlockSpec((16, D), lambda l: (0, 0))
    wsp = pl.BlockSpec((1, D, 768), lambda l: (l, 0, 0))
    return _pc(body, "mod_bwd", [_sds((2, D, 768)), _sds((16, D))], grid=(2,),
               in_specs=[full, wsp, pl.BlockSpec((1, 16, 768), lambda l: (l, 0, 0))], out_specs=[wsp, full])(
        scin, wmod, G)


_SMALL = ["b_mod", "g_mix", "wa_sink", "na_rpb", "ssm_conv_w", "ssm_conv_b", "ssm_dt_bias", "ssm_a_log", "ssm_d",
          "ssm_norm_g", "g_ffn", "g_final", "dmod_s", "dmod_c", "loss"]


def _pack(parts):
    rows = []
    for a in parts:
        f = a.reshape(-1).astype(F32)
        rows.append(jnp.pad(f, (0, (-f.shape[0]) % 1024)).reshape(-1, 128))
    return jnp.concatenate(rows, axis=0)


def _unpack(packed, shapes):
    out, r = [], 0
    for s in shapes:
        nel = int(np.prod(s))
        nr = -(-nel // 1024) * 8
        out.append(packed[r:r + nr].reshape(-1)[:nel].reshape(s))
        r += nr
    return out


def kernel(x, c, ctx, c_ctx, w_mod, b_mod, g_mix, w_in, wa_sink, na_rpb, ssm_conv_w, ssm_conv_b, ssm_dt_bias, ssm_a_log, ssm_d, ssm_norm_g, w_out, g_ffn, w_ffn_in, w_ffn_out, g_final, loss_target, m_c_ctx, m_w_mod, m_b_mod, m_g_mix, m_w_in, m_wa_sink, m_na_rpb, m_ssm_conv_w, m_ssm_conv_b, m_ssm_dt_bias, m_ssm_a_log, m_ssm_d, m_ssm_norm_g, m_w_out, m_g_ffn, m_w_ffn_in, m_w_ffn_out, m_g_final, v_c_ctx, v_w_mod, v_b_mod, v_g_mix, v_w_in, v_wa_sink, v_na_rpb, v_ssm_conv_w, v_ssm_conv_b, v_ssm_dt_bias, v_ssm_a_log, v_ssm_d, v_ssm_norm_g, v_w_out, v_g_ffn, v_w_ffn_in, v_w_ffn_out, v_g_final):
    L = x.shape[1]
    px, py, pc = _place()
    me = 4 * px + 2 * py + pc
    W = dict(c_ctx=c_ctx, w_mod=w_mod, b_mod=b_mod, g_mix=g_mix, w_in=w_in, wa_sink=wa_sink, na_rpb=na_rpb,
             ssm_conv_w=ssm_conv_w, ssm_conv_b=ssm_conv_b, ssm_dt_bias=ssm_dt_bias, ssm_a_log=ssm_a_log, ssm_d=ssm_d,
             ssm_norm_g=ssm_norm_g, w_out=w_out, g_ffn=g_ffn, w_ffn_in=w_ffn_in, w_ffn_out=w_ffn_out, g_final=g_final)
    M = dict(c_ctx=m_c_ctx, w_mod=m_w_mod, b_mod=m_b_mod, g_mix=m_g_mix, w_in=m_w_in, wa_sink=m_wa_sink,
             na_rpb=m_na_rpb, ssm_conv_w=m_ssm_conv_w, ssm_conv_b=m_ssm_conv_b, ssm_dt_bias=m_ssm_dt_bias,
             ssm_a_log=m_ssm_a_log, ssm_d=m_ssm_d, ssm_norm_g=m_ssm_norm_g, w_out=m_w_out, g_ffn=m_g_ffn,
             w_ffn_in=m_w_ffn_in, w_ffn_out=m_w_ffn_out, g_final=m_g_final)
    V = dict(c_ctx=v_c_ctx, w_mod=v_w_mod, b_mod=v_b_mod, g_mix=v_g_mix, w_in=v_w_in, wa_sink=v_wa_sink,
             na_rpb=v_na_rpb, ssm_conv_w=v_ssm_conv_w, ssm_conv_b=v_ssm_conv_b, ssm_dt_bias=v_ssm_dt_bias,
             ssm_a_log=v_ssm_a_log, ssm_d=v_ssm_d, ssm_norm_g=v_ssm_norm_g, w_out=v_w_out, g_ffn=v_g_ffn,
             w_ffn_in=v_w_ffn_in, w_ffn_out=v_w_ffn_out, g_final=v_g_final)

    tr = lambda a: a.transpose(0, 2, 1)
    shards = dict(w_in=tr(w_in).astype(MXU), w_out=w_out.astype(MXU), w_ffn_in=tr(w_ffn_in).astype(MXU),
                  w_ffn_out=w_ffn_out.astype(MXU))
    c_all, conv_all, shards["w_in_first"] = all_gather([c, ssm_conv_w, shards["w_in"][0]], "gather_first")
    conv_f = conv_all.transpose(1, 2, 0, 3).reshape(2, 7, 1024)

    scin = jnp.concatenate([c_all.reshape(NDEV, D), c_ctx.reshape(1, D), jnp.zeros((7, D), F32)], axis=0)
    bcol = lax.dynamic_slice_in_dim(b_mod, me * 768, 768, axis=1).reshape(2, 1, 768)
    mod_all, = all_gather([mod_fwd(scin, w_mod, bcol)], "gather_mod")
    mod_rows = mod_all.transpose(1, 2, 0, 3).reshape(2, 16, 6 * D)
    mods = jnp.stack([lax.dynamic_index_in_dim(mod_rows, me, axis=1, keepdims=False), mod_rows[:, 8]], axis=1)

    layers = [dict(g_mix=g_mix[i], wa_sink=wa_sink[i], na_rpb=na_rpb[i], ssm_conv_w=conv_f[i],
                   ssm_conv_b=ssm_conv_b[i], ssm_dt_bias=ssm_dt_bias[i], ssm_a_log=ssm_a_log[i], ssm_d=ssm_d[i],
                   ssm_norm_g=ssm_norm_g[i], g_ffn=g_ffn[i]) for i in range(2)]
    loss, dx, grads, dmods, dgfin, gin0 = local_step(x[0], ctx[0], loss_target[0], mods, layers, shards, g_final, L)

    stk = lambda n: jnp.stack([grads[0][n], grads[1][n]])
    small = dict(b_mod=dmods[:, 0] + dmods[:, 1], g_final=dgfin, dmod_s=dmods[:, 0], dmod_c=dmods[:, 1],
                 loss=loss.reshape(1))
    for nme in _SMALL:
        if nme not in small:
            small[nme] = stk(nme)
    shapes = [small[nme].shape for nme in _SMALL]
    zero_like = lambda nme: jnp.zeros(small[nme].shape, F32)
    own = lambda S, nme: S[nme] if (nme in S and S[nme].shape == small[nme].shape) else zero_like(nme)
    gath, = all_gather([_pack([small[nme] for nme in _SMALL])], "gather_grads")
    sm = adam_reduce(gath, _pack([own(W, nme) for nme in _SMALL]), _pack([own(M, nme) for nme in _SMALL]),
                     _pack([own(V, nme) for nme in _SMALL]), "adam_small")
    res = {nme: vals for nme, vals in zip(_SMALL, zip(*[_unpack(a, shapes) for a in sm]))}
    loss = res["loss"][0][0]

    cols = lambda a: lax.dynamic_slice_in_dim(a, me * 768, 768, axis=-1)
    rows_of = lambda s: -(-int(np.prod(s)) // 1024) * 8
    r0 = sum(rows_of(s) for s in shapes[:_SMALL.index("dmod_s")])
    dmod_s_all = gath[:, r0:r0 + rows_of(small["dmod_s"].shape)].reshape(NDEV, 2, 6 * D).transpose(1, 0, 2)
    G = jnp.concatenate([cols(dmod_s_all), cols(res["dmod_c"][0])[:, None, :], jnp.zeros((2, 7, 768), F32)], axis=1)
    dwmod, dscin = mod_bwd(scin, w_mod, G)
    cc_g, = all_gather([dscin[8].reshape(8, 128)], "gather_cctx")
    out = {}
    out["c_ctx"] = [a.reshape(D) for a in adam_reduce(cc_g, c_ctx.reshape(8, 128), m_c_ctx.reshape(8, 128),
                                                      v_c_ctx.reshape(8, 128), "adam_cctx")]
    out["w_mod"] = [a.reshape(2, D, 768) for a in adam_reduce(
        dwmod.reshape(1, 2 * D, 768), w_mod.reshape(2 * D, 768), m_w_mod.reshape(2 * D, 768),
        v_w_mod.reshape(2 * D, 768), "adam_wmod")]
    gconv = lax.dynamic_slice_in_dim(res["ssm_conv_w"][0], me * 128, 128, axis=2)
    out["ssm_conv_w"] = [a.reshape(2, 7, 128) for a in adam_reduce(
        gconv.reshape(1, 14, 128), ssm_conv_w.reshape(14, 128), m_ssm_conv_w.reshape(14, 128),
        v_ssm_conv_w.reshape(14, 128), "adam_conv")]
    for nme in _SMALL:
        if nme not in ("ssm_conv_w", "dmod_s", "dmod_c", "loss"):
            out[nme] = list(res[nme])

    adam_big = lambda nme, t, **kw: adam_layers(grads[0][nme], grads[1][nme], t(W[nme]), t(M[nme]), t(V[nme]),
                                                "adam_" + nme, **kw)
    same = lambda a: a
    res_fi, (got1,) = adam_big("w_ffn_in", tr, sends=(gin0[1],))
    grads[0]["w_in"] = jnp.concatenate([gin0[0], got1], axis=2)
    out["w_ffn_in"] = [tr(a) for a in res_fi]
    out["w_ffn_out"] = list(adam_big("w_ffn_out", same))
    out["w_out"] = list(adam_big("w_out", same))
    out["w_in"] = [tr(a) for a in adam_big("w_in", tr)]
    order = ["c_ctx", "w_mod", "b_mod", "g_mix", "w_in", "wa_sink", "na_rpb", "ssm_conv_w", "ssm_conv_b",
             "ssm_dt_bias", "ssm_a_log", "ssm_d", "ssm_norm_g", "w_out", "g_ffn", "w_ffn_in", "w_ffn_out", "g_final"]
    return (loss, dx.reshape(1, L, D), *[out[nme][0] for nme in order], *[out[nme][1] for nme in order],
            *[out[nme][2] for nme in order], *[out[nme][3] for nme in order])
```

```python
import functools

import numpy as np
import jax
import jax.numpy as jnp
from jax import lax
from jax.experimental import pallas as pl
from jax.experimental.pallas import tpu as pltpu

F32 = jnp.float32
MXU = jnp.bfloat16
_INTERPRET = False
VMEM_LIMIT = 60 * 1024 * 1024

D = 1024
LC = 256
GW = 64
HD = 64
EPS = 1e-6
NEG = -1e30
NDEV = 8
Q = 128
NSTATE = 128
DFF = 2816
IN_COLS = 2832
NP_IN = 3072
C_QA, C_QB, C_Z, C_KA, C_VA, C_KB, C_VB, C_XBC, C_DT = 0, 256, 512, 1024, 1152, 1280, 1536, 1792, 2816
ADAM_LR, ADAM_B1, ADAM_B2, ADAM_EPS, ADAM_WD, ADAM_STEP = 0.001, 0.9, 0.999, 1e-08, 0.01, 10
MESH_T = pl.DeviceIdType.MESH


def _dg(a, b, ca, cb):
    return lax.dot_general(a.astype(MXU), b.astype(MXU), (((ca,), (cb,)), ((), ())), preferred_element_type=F32)


@jax.custom_vjp
def mm(a, b):
    return _dg(a, b, 1, 0)


def _mm_f(a, b):
    return _dg(a, b, 1, 0), (a, b)


def _mm_b(res, g):
    a, b = res
    return _dg(g, b, 1, 1).astype(a.dtype), _dg(a, g, 0, 0).astype(b.dtype)


mm.defvjp(_mm_f, _mm_b)


@jax.custom_vjp
def mm_nt(a, b):
    return _dg(a, b, 1, 1)


def _mmnt_f(a, b):
    return _dg(a, b, 1, 1), (a, b)


def _mmnt_b(res, g):
    a, b = res
    return _dg(g, b, 1, 0).astype(a.dtype), _dg(g, a, 0, 0).astype(b.dtype)


mm_nt.defvjp(_mmnt_f, _mmnt_b)


@jax.custom_vjp
def mm_tn(a, b):
    return _dg(a, b, 0, 0)


def _mmtn_f(a, b):
    return _dg(a, b, 0, 0), (a, b)


def _mmtn_b(res, g):
    a, b = res
    return _dg(b, g, 1, 1).astype(a.dtype), _dg(a, g, 1, 0).astype(b.dtype)


mm_tn.defvjp(_mmtn_f, _mmtn_b)


@jax.custom_vjp
def mmw(a, w):
    return _dg(a, w, 1, 0)


mmw.defvjp(lambda a, w: (_dg(a, w, 1, 0), w), lambda w, g: (_dg(g, w, 1, 1), None))


@jax.custom_vjp
def mmw_nt(a, w):
    return _dg(a, w, 1, 1)


mmw_nt.defvjp(lambda a, w: (_dg(a, w, 1, 1), w), lambda w, g: (_dg(g, w, 1, 0), None))


def _exact(a, b):
    return lax.dot_general(a, b, (((1,), (0,)), ((), ())), precision=lax.Precision.HIGHEST,
                           preferred_element_type=F32)


def _pc(body, name, out_shape, grid=None, in_specs=None, out_specs=None, scratch=(), sends=(), gather=False):
    params = pltpu.CompilerParams(vmem_limit_bytes=VMEM_LIMIT)
    if sends and not isinstance(out_shape, (list, tuple)):
        out_shape, out_specs = [out_shape], [out_specs]
    start, wait = (_ag_start, _ag_wait) if gather else (_a2a_start, _a2a_wait)
    if not sends:
        kw = {}
        if grid is not None:
            kw = dict(grid=grid, in_specs=in_specs, out_specs=out_specs)
        elif in_specs is not None:
            kw = dict(in_specs=in_specs, out_specs=out_specs)
        return pl.pallas_call(body, name=name, out_shape=out_shape, scratch_shapes=list(scratch),
                              compiler_params=params, interpret=_INTERPRET, **kw)
    n, nin, nout, nscr = len(sends), len(in_specs), len(out_shape), len(scratch)

    def body2(*refs):
        cin, xs = refs[:nin], refs[nin:nin + n]
        couts, os_ = refs[nin + n:nin + n + nout], refs[nin + n + nout:nin + 2 * n + nout]
        cscr, sems = refs[nin + 2 * n + nout:nin + 2 * n + nout + nscr], refs[nin + 2 * n + nout + nscr:]
        ids = [pl.program_id(a) for a in range(len(grid))]
        first = functools.reduce(lambda a, b: a & b, [i == 0 for i in ids])
        last = functools.reduce(lambda a, b: a & b, [i == g - 1 for i, g in zip(ids, grid)])

        @pl.when(first)
        def _():
            start(xs, os_, *sems)

        body(*cin, *couts, *cscr)

        @pl.when(last)
        def _():
            wait(xs, os_, *sems)

    call = pl.pallas_call(
        body2, name=name,
        out_shape=list(out_shape) + [_sds(((NDEV,) if gather else ()) + a.shape, a.dtype) for a in sends],
        grid=grid, in_specs=list(in_specs) + [_any()] * n, out_specs=list(out_specs) + [_any()] * n,
        scratch_shapes=list(scratch) + _a2a_sems(n), compiler_params=params, interpret=_INTERPRET)

    def run(*args):
        res = call(*args, *sends)
        return res[:nout], res[nout:]

    return run


def _vm():
    return pl.BlockSpec(memory_space=pltpu.VMEM)


def _sds(shape, dt=F32):
    return jax.ShapeDtypeStruct(shape, dt)


def _iota(shape, dim):
    return lax.broadcasted_iota(jnp.int32, shape, dim)


def _silu(x):
    return x * jax.nn.sigmoid(x)


def _softplus(x):
    return jnp.maximum(x, 0.0) + jnp.log1p(jnp.exp(-jnp.abs(x)))


def _normmod(x, g, sh, sc):
    r = lax.rsqrt(jnp.mean(x * x, axis=-1, keepdims=True) + EPS)
    return (x * r * g) * (1.0 + sc) + sh


def _rope(x, cos, sin, rm):
    return x * cos + _exact(x, rm) * sin


def _swap12(x):
    lane = _iota(x.shape, 1)
    up, down = pltpu.roll(x, 192, 1), pltpu.roll(x, 64, 1)
    return jnp.where((lane >= 64) & (lane < 128), up, jnp.where((lane >= 128) & (lane < 192), down, x))


def _skip_ctx_tile(skip, nlt, compute, outs):
    if not skip:
        compute()
        return
    i = pl.program_id(0)
    pl.when(i < nlt)(compute)

    @pl.when(i >= nlt)
    def _():
        for r in outs:
            r[...] = jnp.zeros_like(r)


def _acc_init(first, refs):
    @pl.when(first)
    def _():
        for r in refs:
            r[...] = jnp.zeros_like(r)


def _stream(X, TR, nlt):
    if not isinstance(X, tuple):
        return (X,), [pl.BlockSpec((TR, D), lambda i: (i, 0))], lambda refs: refs[0][...]
    specs = [pl.BlockSpec((TR, D), lambda i: (jnp.minimum(i, nlt - 1), 0)), pl.BlockSpec((TR, D), lambda i: (0, 0))]
    return X, specs, lambda refs: jnp.where(pl.program_id(0) < nlt, refs[0][...], refs[1][...])


def in_fwd(X, g, sh, sc, W, cos, sin, rm, L, sends=()):
    T = L + LC
    TR = 256
    nlt = L // TR
    xs, xspecs, xread = _stream(X, TR, nlt)

    def body(*refs):
        (g_ref, sh_ref, sc_ref, w_ref, cos_ref, sin_ref, rm_ref,
         qa, qb, z, ka, va, kb, vb, xbc, dt, hout) = refs[len(xs):]
        h = _normmod(xread(refs), g_ref[...], sh_ref[0], sc_ref[0]).astype(MXU)
        hout[...] = h
        y = lax.dot_general(h, w_ref[...], (((1,), (1,)), ((), ())), preferred_element_type=F32)
        cs, sn, r = cos_ref[...], sin_ref[...], rm_ref[...]
        qa[...] = _rope(_swap12(y[:, C_QA:C_QB]), cs, sn, r).astype(MXU)
        qb[...] = y[:, C_QB:C_Z].astype(MXU)
        z[...] = y[:, C_Z:C_KA]
        ka[...] = _rope(y[:, C_KA:C_VA], cs[:, :128], sn[:, :128], r[:128, :128]).astype(MXU)
        va[...] = y[:, C_VA:C_KB].astype(MXU)
        kb[...] = y[:, C_KB:C_VB].astype(MXU)
        vb[...] = y[:, C_VB:C_XBC].astype(MXU)
        xbc[...] = y[:, C_XBC:C_DT]
        dt[...] = y[:, C_DT:C_DT + 128]

    row = lambda w: pl.BlockSpec((TR, w), lambda i: (i, 0))
    cls = pl.BlockSpec((1, 1, D), lambda i: (i // nlt, 0, 0))
    widths = [(256, MXU), (256, MXU), (512, F32), (128, MXU), (128, MXU), (256, MXU), (256, MXU), (1024, F32),
              (128, F32), (D, MXU)]
    return _pc(body, "in_fwd", [_sds((T, w), d) for w, d in widths], grid=(T // TR,),
               in_specs=xspecs + [pl.BlockSpec((1, D), lambda i: (0, 0)), cls, cls, _vm(), row(256), row(256), _vm()],
               out_specs=[row(w) for w, _ in widths], sends=sends, gather=True)(*xs, g, sh, sc, W, cos, sin, rm)


def in_bwd(X, g, sh, sc, W, cos, sin, rm, dxres, dqa, dqb, dz, dka, dva, dkb, dvb, dxbc, ddt2, L, latent_only):
    T = L + LC
    TR = 256
    nlt = L // TR
    xs, xspecs, xread = _stream(X, TR, nlt)

    def body(*refs):
        (g_ref, sh_ref, sc_ref, w_ref, cos_ref, sin_ref, rm_ref, dxres_ref, dqa_r, dqb_r, dz_r, dka_r,
         dva_r, dkb_r, dvb_r, dxbc_r, ddt0_r, ddt1_r, dx_o, dy_o, dg_o, dsh_o, dsc_o) = refs[len(xs):]
        i = pl.program_id(0)
        cs, sn, r = cos_ref[...], sin_ref[...], rm_ref[...]
        _, vq = jax.vjp(lambda t: _rope(t, cs, sn, r), dqa_r[...])
        _, vk = jax.vjp(lambda t: _rope(t, cs[:, :128], sn[:, :128], r[:128, :128]), dka_r[...])
        dyqa = _swap12(vq(dqa_r[...])[0])
        dyka, = vk(dka_r[...])
        ddt = ddt0_r[0] + ddt1_r[0]
        dy = jnp.concatenate([dyqa, dqb_r[...], dz_r[...], dyka, dva_r[...], dkb_r[...], dvb_r[...], dxbc_r[...],
                              ddt, jnp.zeros((TR, NP_IN - C_DT - 128), F32)], axis=1).astype(MXU)
        dy_o[...] = dy
        dh = jnp.dot(dy, w_ref[...], preferred_element_type=F32)
        _, vp = jax.vjp(_normmod, xread(refs), g_ref[...], sh_ref[0], sc_ref[0])
        dx, dg, dsh, dsc = vp(dh)
        if latent_only:
            @pl.when(i < nlt)
            def _():
                dx_o[...] = dx + dxres_ref[...]
        else:
            dx_o[...] = dx + dxres_ref[...]
        _acc_init(i == 0, [dg_o])
        _acc_init((i == 0) | (i == nlt), [dsh_o, dsc_o])
        dg_o[...] += dg
        dsh_o[0] += dsh
        dsc_o[0] += dsc

    row = lambda w: pl.BlockSpec((TR, w), lambda i: (i, 0))
    cls = pl.BlockSpec((1, 1, D), lambda i: (i // nlt, 0, 0))
    vec = pl.BlockSpec((1, D), lambda i: (0, 0))
    dts = lambda d: pl.BlockSpec((1, TR, 128), lambda i: (d, i, 0))
    dxs = pl.BlockSpec((TR, D), lambda i: (jnp.minimum(i, nlt - 1), 0)) if latent_only else row(D)
    return _pc(body, "in_bwd",
               [_sds((L if latent_only else T, D)), _sds((T, NP_IN), MXU), _sds((1, D)), _sds((2, 1, D)),
                _sds((2, 1, D))],
               grid=(T // TR,),
               in_specs=xspecs + [vec, cls, cls, _vm(), row(256), row(256), _vm(), row(D), row(256), row(256),
                                  row(512), row(128), row(128), row(256), row(256), row(1024), dts(0), dts(1)],
               out_specs=[dxs, row(NP_IN), vec, cls, cls])(
        *xs, g, sh, sc, W, cos, sin, rm, dxres, dqa, dqb, dz, dka, dva, dkb, dvb, dxbc, ddt2, ddt2)


def tn_mm(A, G, bk, bn, out_dtype, ncol=None, col0=0, rows=None, sends=()):
    T, K = A.shape
    T = T if rows is None else rows
    N = G.shape[1] if ncol is None else ncol
    first = col0 * (N // bn)
    bt = T
    nt = T // bt

    def body(a_ref, g_ref, o_ref, acc):
        t = pl.program_id(2)
        _acc_init(t == 0, [acc])
        acc[...] += lax.dot_general(a_ref[...], g_ref[...], (((0,), (0,)), ((), ())), preferred_element_type=F32)

        @pl.when(t == nt - 1)
        def _():
            o_ref[...] = acc[...].astype(out_dtype)

    return _pc(body, "tn_mm", _sds((K, N), out_dtype), grid=(K // bk, N // bn, nt),
               in_specs=[pl.BlockSpec((bt, bk), lambda k, n, t: (t, k)),
                         pl.BlockSpec((bt, bn), lambda k, n, t: (t, first + n))],
               out_specs=pl.BlockSpec((bk, bn), lambda k, n, t: (k, n)),
               scratch=[pltpu.VMEM((bk, bn), F32)], sends=sends)(A, G)


def _ssm_out(yf, yb, xs, z, dsk, gs):
    y = (yf + yb + dsk * xs) * _silu(z)
    r = lax.rsqrt(jnp.mean(y * y, axis=-1, keepdims=True) + EPS)
    return y * r * gs


def out_fwd(oa, ob, y2, act, z, dsk, gs, W, X, gate, L, sends=()):
    T = L + LC
    TR = 256
    nlt = L // TR
    xs, xspecs, xread = _stream(X, TR, nlt)

    def body(*refs):
        oa_r, ob_r, yf_r, yb_r, xs_r, z_r, dsk_r, gs_r, w_ref, gt_ref, x1_o, cat_o = refs[len(xs):]
        oc = _ssm_out(yf_r[0], yb_r[0], xs_r[...], z_r[...], dsk_r[...], gs_r[...])
        cat = jnp.concatenate([_swap12(oa_r[...]), ob_r[...], oc], axis=1).astype(MXU)
        cat_o[...] = cat
        x1_o[...] = xread(refs) + gt_ref[0] * jnp.dot(cat, w_ref[...], preferred_element_type=F32)

    row = lambda w: pl.BlockSpec((TR, w), lambda i: (i, 0))
    ys = lambda d: pl.BlockSpec((1, TR, 512), lambda i: (d, i, 0))
    cls = pl.BlockSpec((1, 1, D), lambda i: (i // nlt, 0, 0))
    v512 = pl.BlockSpec((1, 512), lambda i: (0, 0))
    return _pc(body, "out_fwd", [_sds((T, D)), _sds((T, D), MXU)], grid=(T // TR,),
               in_specs=xspecs + [row(256), row(256), ys(0), ys(1), row(512), row(512), v512, v512, _vm(), cls],
               out_specs=[row(D), row(D)], sends=sends, gather=True)(*xs, oa, ob, y2, y2, act, z, dsk, gs, W, gate)


def out_bwd(oa, ob, y2, act, z, dsk, gs, W, gate, dX1, L):
    T = dX1.shape[0]
    TR = 256
    nlt = L // TR

    def body(oa_r, ob_r, yf_r, yb_r, xs_r, z_r, dsk_r, gs_r, w_ref, gt_ref, dx1_r,
             doa_o, dob_o, dy_o, dxs_o, dz_o, dmix_o, ddsk_o, dgs_o, dgt_o):
        i = pl.program_id(0)
        w = w_ref[...]

        def f(oa_, ob_, yf, yb, xs, z_, dsk_, gs_, gt):
            oc = _ssm_out(yf, yb, xs, z_, dsk_, gs_)
            return gt * mmw(jnp.concatenate([oa_, ob_, oc], axis=1), w)

        _, vjp = jax.vjp(f, _swap12(oa_r[...]), ob_r[...], yf_r[0], yb_r[0], xs_r[...], z_r[...], dsk_r[...],
                         gs_r[...], gt_ref[0])
        dx1 = dx1_r[...]
        doa, dob, dyf, _, dxs, dz, ddsk, dgs, dgt = vjp(dx1)
        doa_o[...] = _swap12(doa)
        dob_o[...] = dob
        dy_o[...] = dyf
        dxs_o[...] = dxs
        dz_o[...] = dz
        dmix_o[...] = (gt_ref[0] * dx1).astype(MXU)
        _acc_init(i == 0, [ddsk_o, dgs_o])
        _acc_init((i == 0) | (i == nlt), [dgt_o])
        ddsk_o[...] += ddsk
        dgs_o[...] += dgs
        dgt_o[0] += dgt

    row = lambda w: pl.BlockSpec((TR, w), lambda i: (i, 0))
    ys = lambda d: pl.BlockSpec((1, TR, 512), lambda i: (d, i, 0))
    cls = pl.BlockSpec((1, 1, D), lambda i: (i // nlt, 0, 0))
    v512 = pl.BlockSpec((1, 512), lambda i: (0, 0))
    return _pc(body, "out_bwd",
               [_sds((T, 256)), _sds((T, 256)), _sds((T, 512)), _sds((T, 512)), _sds((T, 512)), _sds((T, D), MXU),
                _sds((1, 512)), _sds((1, 512)), _sds((2, 1, D))],
               grid=(T // TR,),
               in_specs=[row(256), row(256), ys(0), ys(1), row(512), row(512), v512, v512, _vm(), cls, row(D)],
               out_specs=[row(256), row(256), row(512), row(512), row(512), row(D), v512, v512, cls])(
        oa, ob, y2, y2, act, z, dsk, gs, W, gate, dX1)


def ffn_fwd(X, g, sh, sc, gate, Win, Wout, L, sends=(), skip_ctx=False):
    T = X.shape[0]
    TR = 256
    nlt = L // TR

    def body(x_ref, g_ref, sh_ref, sc_ref, gt_ref, wi_ref, wo_ref, o_ref, f_ref):
        def compute():
            h = _normmod(x_ref[...], g_ref[...], sh_ref[0], sc_ref[0]).astype(MXU)
            nt = (((1,), (1,)), ((), ()))
            a = lax.dot_general(h, wi_ref[0:DFF, :], nt, preferred_element_type=F32)
            u = lax.dot_general(h, wi_ref[DFF:2 * DFF, :], nt, preferred_element_type=F32)
            act = (_silu(a) * u).astype(MXU)
            ff = jnp.dot(act, wo_ref[...], preferred_element_type=F32)
            f_ref[...] = ff
            o_ref[...] = x_ref[...] + gt_ref[0] * ff

        _skip_ctx_tile(skip_ctx, nlt, compute, [o_ref, f_ref])

    row = lambda w: pl.BlockSpec((TR, w), lambda i: (i, 0))
    cls = pl.BlockSpec((1, 1, D), lambda i: (i // nlt, 0, 0))
    vec = pl.BlockSpec((1, D), lambda i: (0, 0))
    return _pc(body, "ffn_fwd", [_sds((T, D)), _sds((T, D))], grid=(T // TR,),
               in_specs=[row(D), vec, cls, cls, cls, _vm(), _vm()], out_specs=[row(D), row(D)], sends=sends,
               gather=True)(X, g, sh, sc, gate, Win, Wout)


def ffn_bwd(X, g, sh, sc, gate, Win, Wout, FF, dX2, L, sends=(), nchunk=2, skip_ctx=False):
    T = X.shape[0]
    TR = 256
    nlt = L // TR
    CH = DFF // nchunk

    def body(x_ref, g_ref, sh_ref, sc_ref, gt_ref, wi_ref, wo_ref, ff_r, dx2_r,
             dx_o, h_o, du_o, act_o, dout_o, dg_o, dsh_o, dsc_o, dgt_o):
        i = pl.program_id(0)
        _acc_init(i == 0, [dg_o])
        _acc_init((i == 0) | (i == nlt), [dsh_o, dsc_o, dgt_o])

        def compute():
            h, vp = jax.vjp(_normmod, x_ref[...], g_ref[...], sh_ref[0], sc_ref[0])
            dx2 = dx2_r[...]
            dout = gt_ref[0] * dx2
            zero = jnp.zeros((TR, CH), F32)
            dh = jnp.zeros((TR, D), F32)
            for c in range(nchunk):
                lo, hi = c * CH, (c + 1) * CH
                wg, wu, wo = wi_ref[lo:hi, :], wi_ref[DFF + lo:DFF + hi, :], wo_ref[lo:hi, :]

                def f(h_, eg, eu):
                    act = _silu(mmw_nt(h_, wg) + eg) * (mmw_nt(h_, wu) + eu)
                    return mmw(act, wo), act

                _, vjp_c, act = jax.vjp(f, h, zero, zero, has_aux=True)
                dh_c, da, du = vjp_c(dout)
                dh = dh + dh_c
                du_o[:, lo:hi] = da.astype(MXU)
                du_o[:, DFF + lo:DFF + hi] = du.astype(MXU)
                act_o[:, lo:hi] = act.astype(MXU)
            dx, dg, dsh, dsc = vp(dh)
            dx_o[...] = dx + dx2
            h_o[...] = h.astype(MXU)
            dout_o[...] = dout.astype(MXU)
            dg_o[...] += dg
            dsh_o[0] += dsh
            dsc_o[0] += dsc
            dgt_o[0] += jnp.sum(dx2 * ff_r[...], axis=0, keepdims=True)

        _skip_ctx_tile(skip_ctx, nlt, compute, [dx_o, h_o, du_o, act_o, dout_o])

    row = lambda w: pl.BlockSpec((TR, w), lambda i: (i, 0))
    cls = pl.BlockSpec((1, 1, D), lambda i: (i // nlt, 0, 0))
    vec = pl.BlockSpec((1, D), lambda i: (0, 0))
    return _pc(body, "ffn_bwd",
               [_sds((T, D)), _sds((T, D), MXU), _sds((T, 2 * DFF), MXU), _sds((T, DFF), MXU), _sds((T, D), MXU),
                _sds((1, D)), _sds((2, 1, D)), _sds((2, 1, D)), _sds((2, 1, D))],
               grid=(T // TR,),
               in_specs=[row(D), vec, cls, cls, cls, _vm(), _vm(), row(D), row(D)],
               out_specs=[row(D), row(D), row(2 * DFF), row(DFF), row(D), vec, cls, cls, cls], sends=sends)(
        X, g, sh, sc, gate, Win, Wout, FF, dX2)


def loss_head(X2, g, tgt, L):
    T = X2.shape[0]
    TR = 256
    nlt = L // TR

    def body(x_ref, g_ref, t_ref, loss_o, dx_o, dg_o):
        i = pl.program_id(0)
        _acc_init(i == 0, [loss_o, dg_o])

        @pl.when(i < nlt)
        def _():
            def f(x, g_):
                y = x * lax.rsqrt(jnp.mean(x * x, axis=-1, keepdims=True) + EPS) * g_
                return 0.5 * jnp.sum(jnp.mean(jnp.square(y - t_ref[...]), axis=-1, keepdims=True), axis=0,
                                     keepdims=True)

            val, vjp = jax.vjp(f, x_ref[...], g_ref[...])
            dx, dg = vjp(jnp.ones((1, 1), F32))
            dx_o[...] = dx
            loss_o[...] += jnp.broadcast_to(val, (8, 128))
            dg_o[...] += dg

        @pl.when(i >= nlt)
        def _():
            dx_o[...] = jnp.zeros_like(dx_o)

    row = pl.BlockSpec((TR, D), lambda i: (i, 0))
    vec = pl.BlockSpec((1, D), lambda i: (0, 0))
    return _pc(body, "loss_head", [_sds((8, 128)), _sds((T, D)), _sds((1, D))], grid=(T // TR,),
               in_specs=[row, vec, pl.BlockSpec((TR, D), lambda i: (jnp.minimum(i, nlt - 1), 0))],
               out_specs=[pl.BlockSpec((8, 128), lambda i: (0, 0)), row, vec])(X2, g, tgt)


def _stack_impl(q):
    lane = _iota(q.shape, 1)
    return jnp.concatenate([jnp.where(lane < HD, q, 0.0), jnp.where(lane >= HD, q, 0.0)], axis=0)


def _unstack_impl(o):
    M = o.shape[0] // 2
    return jnp.where(_iota((M, o.shape[1]), 1) < HD, o[:M], o[M:])


@jax.custom_vjp
def _stack(q):
    return _stack_impl(q)


_stack.defvjp(lambda q: (_stack_impl(q), None), lambda _, g: (_unstack_impl(g),))


@jax.custom_vjp
def _unstack(o):
    return _unstack_impl(o)


_unstack.defvjp(lambda o: (_unstack_impl(o), None), lambda _, g: (_stack_impl(g),))


def _softmax_av(q, ks, vs, biases, sink):
    q2 = _stack(q)
    ss = []
    for k, b in zip(ks, biases):
        s = mm_nt(q2, k) * (HD ** -0.5)
        ss.append(s if b is None else s + b)
    m = functools.reduce(jnp.maximum, [jnp.max(s, axis=1, keepdims=True) for s in ss])
    if sink is not None:
        m = jnp.maximum(m, sink)
    m = lax.stop_gradient(m)
    es = [jnp.exp(s - m) for s in ss]
    den = functools.reduce(lambda a, b_: a + b_, [jnp.sum(e, axis=1, keepdims=True) for e in es])
    if sink is not None:
        den = den + jnp.exp(sink - m)
    inv = 1.0 / den
    return _unstack(functools.reduce(lambda a, b_: a + b_, [mm(e * inv, v) for e, v in zip(es, vs)]))


def _sink_col(s0, s1, M):
    return jnp.concatenate([jnp.broadcast_to(jnp.mean(s0, axis=1, keepdims=True), (M, 1)),
                            jnp.broadcast_to(jnp.mean(s1, axis=1, keepdims=True), (M, 1))], axis=0)


def _stack4_impl(q):
    lane = _iota((q.shape[0], 128), 1)
    parts = []
    for p in range(2):
        qp = q[:, 128 * p:128 * (p + 1)]
        parts += [jnp.where(lane < HD, qp, 0.0), jnp.where(lane >= HD, qp, 0.0)]
    return jnp.concatenate(parts, axis=0)


def _unstack4_impl(o):
    M = o.shape[0] // 4
    lane = _iota((M, 128), 1)
    return jnp.concatenate([jnp.where(lane < HD, o[0:M], o[M:2 * M]),
                            jnp.where(lane < HD, o[2 * M:3 * M], o[3 * M:4 * M])], axis=1)


@jax.custom_vjp
def _stack4(q):
    return _stack4_impl(q)


_stack4.defvjp(lambda q: (_stack4_impl(q), None), lambda _, g: (_unstack4_impl(g),))


@jax.custom_vjp
def _unstack4(o):
    return _unstack4_impl(o)


_unstack4.defvjp(lambda o: (_unstack4_impl(o), None), lambda _, g: (_stack4_impl(g),))


WA_NB = 4


def _wa_blocks(qs, kws, vws, kx, vx, sks, n0, L):
    sc = HD ** -0.5
    sink = jnp.concatenate([jnp.broadcast_to(jnp.mean(s_, axis=1, keepdims=True), (Q, 1)) for s_ in sks], axis=0)
    bias = []
    for b_ in range(len(qs)):
        n = n0 + b_
        qpos = n * Q + (_iota((4 * Q, 3 * Q), 0) & (Q - 1))
        kpos = (n - 1) * Q + _iota((4 * Q, 3 * Q), 1)
        bias.append(jnp.where((jnp.abs(qpos - kpos) <= Q) & (kpos >= 0) & (kpos < L), 0.0, NEG))
    q4 = [_stack4(q) for q in qs]
    sl = [mm_nt(a, k) * sc + b_ for a, k, b_ in zip(q4, kws, bias)]
    sx = [mm_nt(a, kx) * sc for a in q4]
    m = [lax.stop_gradient(jnp.maximum(jnp.maximum(jnp.max(a, axis=1, keepdims=True),
                                                   jnp.max(b_, axis=1, keepdims=True)), sink))
         for a, b_ in zip(sl, sx)]
    el = [jnp.exp(a - c) for a, c in zip(sl, m)]
    ex = [jnp.exp(a - c) for a, c in zip(sx, m)]
    inv = [1.0 / (jnp.sum(a, axis=1, keepdims=True) + jnp.sum(b_, axis=1, keepdims=True) + jnp.exp(sink - c))
           for a, b_, c in zip(el, ex, m)]
    return [_unstack4(mm(a * i, v) + mm(b_ * i, vx)) for a, b_, i, v in zip(el, ex, inv, vws)]


def _wa_load(q_r, k_r, v_r, n0):
    f = lambda t: t.astype(F32)
    qs = [f(q_r[b_ * Q:(b_ + 1) * Q, :]) for b_ in range(WA_NB)]
    wins = [pl.ds(pl.multiple_of((n0 + b_) * Q, Q), 3 * Q) for b_ in range(WA_NB)]
    return qs, [f(k_r[w, :]) for w in wins], [f(v_r[w, :]) for w in wins], wins


def _wa_specs(L):
    nb = L // Q
    qs = pl.BlockSpec((WA_NB * Q, 256), lambda n: (n, 0))
    kfull = pl.BlockSpec((L + LC + Q, 128), lambda n: (0, 0))
    sks = pl.BlockSpec((2, 2, 1, 128), lambda n: (0, 0, 0, 0))
    return nb, qs, kfull, sks


def wa_fwd(QA, KA, VA, sinkp, L, sends=()):
    nb, qs, kfull, sks = _wa_specs(L)
    pad = lambda a: jnp.concatenate([jnp.zeros((Q, 128), a.dtype), a], axis=0)

    def body(q_r, k_r, v_r, sk_r, o_ref):
        n0 = pl.program_id(0) * WA_NB
        qs_, kws, vws, _ = _wa_load(q_r, k_r, v_r, n0)
        cx = pl.ds(Q + L, LC)
        outs = _wa_blocks(qs_, kws, vws, k_r[cx, :].astype(F32), v_r[cx, :].astype(F32),
                          [sk_r[0, 0], sk_r[0, 1], sk_r[1, 0], sk_r[1, 1]], n0, L)
        o_ref[...] = jnp.concatenate(outs, axis=0)

    return _pc(body, "wa_fwd", _sds((L, 256)), grid=(nb // WA_NB,), in_specs=[qs, kfull, kfull, sks], out_specs=qs,
               sends=sends, gather=True)(QA, pad(KA), pad(VA), sinkp)


def wa_bwd(QA, KA, VA, sinkp, dO, L, sends=()):
    nb, qs, kfull, sks = _wa_specs(L)
    pad = lambda a: jnp.concatenate([jnp.zeros((Q, 128), a.dtype), a], axis=0)

    def body(q_r, k_r, v_r, sk_r, do_r, dq_o, dk_o, dv_o, dsk_o):
        n0 = pl.program_id(0) * WA_NB
        _acc_init(n0 == 0, [dk_o, dv_o, dsk_o])
        qs_, kws, vws, wins = _wa_load(q_r, k_r, v_r, n0)
        cx = pl.ds(Q + L, LC)
        fn = lambda a, b, c, d, e, s_: _wa_blocks(a, b, c, d, e, s_, n0, L)
        _, vjp = jax.vjp(fn, qs_, kws, vws, k_r[cx, :].astype(F32), v_r[cx, :].astype(F32),
                         [sk_r[0, 0], sk_r[0, 1], sk_r[1, 0], sk_r[1, 1]])
        dqs, dkws, dvws, dkx, dvx, ds = vjp([do_r[b_ * Q:(b_ + 1) * Q, :] for b_ in range(WA_NB)])
        dq_o[...] = jnp.concatenate(dqs, axis=0)
        for w, dk, dv in zip(wins, dkws, dvws):
            dk_o[w, :] += dk
            dv_o[w, :] += dv
        dk_o[cx, :] += dkx
        dv_o[cx, :] += dvx
        for i_ in range(4):
            dsk_o[i_ // 2, i_ % 2] += ds[i_]

    return _pc(body, "wa_bwd", [_sds((L, 256)), _sds((L + LC + Q, 128)), _sds((L + LC + Q, 128)),
                                _sds((2, 2, 1, 128))],
               grid=(nb // WA_NB,), in_specs=[qs, kfull, kfull, sks, qs], out_specs=[qs, kfull, kfull, sks],
               sends=sends)(QA, pad(KA), pad(VA), sinkp, dO)


def _ctx_block(q, kx, vx, s0, s1):
    return _softmax_av(q, [kx], [vx], [None], _sink_col(s0, s1, LC))


def ctx_fwd(Qx, Kx, Vx, sinkp, shared, L):
    cq = pl.BlockSpec((LC, 128), lambda p: (L // LC, p))
    ck = pl.BlockSpec((LC, 128), lambda p: (L // LC, 0 if shared else p))
    sks = pl.BlockSpec((1, 2, 1, 128), lambda p: (p, 0, 0, 0))

    def body(q_r, k_r, v_r, sk_r, o_ref):
        f = lambda t: t[...].astype(F32)
        o_ref[...] = _ctx_block(f(q_r), f(k_r), f(v_r), sk_r[0, 0], sk_r[0, 1])

    return _pc(body, "ctx_fwd", _sds((LC, 256)), grid=(2,), in_specs=[cq, ck, ck, sks],
               out_specs=pl.BlockSpec((LC, 128), lambda p: (0, p)))(Qx, Kx, Vx, sinkp)


def ctx_bwd(Qx, Kx, Vx, sinkp, dO, shared, L):
    cq = pl.BlockSpec((LC, 128), lambda p: (L // LC, p))
    ck = pl.BlockSpec((LC, 128), lambda p: (L // LC, 0 if shared else p))
    sks = pl.BlockSpec((1, 2, 1, 128), lambda p: (p, 0, 0, 0))
    op = pl.BlockSpec((LC, 128), lambda p: (0, p))
    ok = pl.BlockSpec((LC, 128), lambda p: (0, 0 if shared else p))
    dos = pl.BlockSpec((LC, 128), lambda p: (L // LC, p))

    def body(q_r, k_r, v_r, sk_r, do_r, dq_o, dk_o, dv_o, dsk_o):
        p = pl.program_id(0)
        f = lambda t: t[...].astype(F32)
        _, vjp = jax.vjp(_ctx_block, f(q_r), f(k_r), f(v_r), sk_r[0, 0], sk_r[0, 1])
        dq, dk, dv, ds0, ds1 = vjp(do_r[...])
        dq_o[...] = dq
        _acc_init((p == 0) if shared else (p >= 0), [dk_o, dv_o])
        dk_o[...] += dk
        dv_o[...] += dv
        dsk_o[0, 0] = ds0
        dsk_o[0, 1] = ds1

    kw = 128 if shared else 256
    return _pc(body, "ctx_bwd", [_sds((LC, 256)), _sds((LC, kw)), _sds((LC, kw)), _sds((2, 2, 1, 128))],
               grid=(2,), in_specs=[cq, ck, ck, sks, dos], out_specs=[op, ok, ok, sks])(Qx, Kx, Vx, sinkp, dO)


def _na_rows(qs, kws, vws, kx, vx, bs):
    sc = HD ** -0.5
    q2 = [_stack(q) for q in qs]
    sl = [mm_nt(a, k) * sc + b for a, k, b in zip(q2, kws, bs)]
    sx = [mm_nt(a, kx) * sc for a in q2]
    m = [lax.stop_gradient(jnp.maximum(jnp.max(a, axis=1, keepdims=True), jnp.max(b, axis=1, keepdims=True)))
         for a, b in zip(sl, sx)]
    el = [jnp.exp(a - c) for a, c in zip(sl, m)]
    ex = [jnp.exp(a - c) for a, c in zip(sx, m)]
    inv = [1.0 / (jnp.sum(a, axis=1, keepdims=True) + jnp.sum(b, axis=1, keepdims=True)) for a, b in zip(el, ex)]
    o2 = [mm(a * i, v) + mm(b * i, vx) for a, b, i, v in zip(el, ex, inv, vws)]
    return [_unstack(o) for o in o2]


NA_ROWS = 16


def _na_geom(r, R):
    s = jnp.clip(r - 4, 0, R - 8)
    cls = jnp.where(r < 4, r, jnp.where(r > R - 4, r - (R - 8), 4))
    return pl.ds(pl.multiple_of(s * GW, GW), 8 * GW), cls


def _na_load(q_r, k_r, v_r, b_r, rb, R):
    nr = min(NA_ROWS, R)
    geo = [_na_geom(rb * nr + j, R) for j in range(nr)]
    qs = [q_r[j * GW:(j + 1) * GW, :].astype(F32) for j in range(nr)]
    kws = [k_r[win, :].astype(F32) for win, _ in geo]
    vws = [v_r[win, :].astype(F32) for win, _ in geo]
    bs = [jnp.concatenate([b_r[0, cls], b_r[1, cls]], axis=0) for _, cls in geo]
    return geo, qs, kws, vws, bs


def na_fwd(QB, KB, VB, biasd, L, sends=()):
    R = L // GW
    nr = min(NA_ROWS, R)
    qs = pl.BlockSpec((nr * GW, 128), lambda p, rb: (rb, p))
    kfull = pl.BlockSpec((L, 128), lambda p, rb: (0, p))
    kctx = pl.BlockSpec((LC, 128), lambda p, rb: (L // LC, p))
    bs = pl.BlockSpec((2, 8, GW, 8 * GW), lambda p, rb: (p, 0, 0, 0))

    def body(q_r, k_r, v_r, kx_r, vx_r, b_r, o_ref):
        _, qs_, kws, vws, bs_ = _na_load(q_r, k_r, v_r, b_r, pl.program_id(1), R)
        outs = _na_rows(qs_, kws, vws, kx_r[...].astype(F32), vx_r[...].astype(F32), bs_)
        o_ref[...] = jnp.concatenate(outs, axis=0)

    return _pc(body, "na_fwd", _sds((L, 256)), grid=(2, R // nr), in_specs=[qs, kfull, kfull, kctx, kctx, bs],
               out_specs=qs, sends=sends, gather=True)(QB, KB, VB, KB, VB, biasd)


def na_bwd(QB, KB, VB, biasd, dO, L):
    R = L // GW
    nr = min(NA_ROWS, R)
    qs = pl.BlockSpec((nr * GW, 128), lambda p, rb: (rb, p))
    kfull = pl.BlockSpec((L, 128), lambda p, rb: (0, p))
    kctx = pl.BlockSpec((LC, 128), lambda p, rb: (L // LC, p))
    bs = pl.BlockSpec((2, 8, GW, 8 * GW), lambda p, rb: (p, 0, 0, 0))
    oc = pl.BlockSpec((LC, 128), lambda p, rb: (0, p))

    def body(q_r, k_r, v_r, kx_r, vx_r, b_r, do_r, dq_o, dk_o, dv_o, dkx_o, dvx_o, db_o):
        rb = pl.program_id(1)
        _acc_init(rb == 0, [dk_o, dv_o, dkx_o, dvx_o, db_o])
        geo, qs_, kws, vws, bs_ = _na_load(q_r, k_r, v_r, b_r, rb, R)
        _, vjp = jax.vjp(_na_rows, qs_, kws, vws, kx_r[...].astype(F32), vx_r[...].astype(F32), bs_)
        dqs, dkws, dvws, dkx, dvx, dbs = vjp([do_r[j * GW:(j + 1) * GW, :] for j in range(nr)])
        dq_o[...] = jnp.concatenate(dqs, axis=0)
        dkx_o[...] += dkx
        dvx_o[...] += dvx
        for j, (win, cls) in enumerate(geo):
            dk_o[win, :] += dkws[j]
            dv_o[win, :] += dvws[j]
            db_o[0, cls] += dbs[j][:GW]
            db_o[1, cls] += dbs[j][GW:]

    return _pc(body, "na_bwd",
               [_sds((L, 256)), _sds((L, 256)), _sds((L, 256)), _sds((LC, 256)), _sds((LC, 256)),
                _sds((4, 8, GW, 8 * GW))],
               grid=(2, R // nr), in_specs=[qs, kfull, kfull, kctx, kctx, bs, qs],
               out_specs=[qs, kfull, kfull, oc, oc, bs])(QB, KB, VB, KB, VB, biasd, dO)


def exact_mm_call(A, B):
    def body(a_ref, b_ref, o_ref):
        o_ref[...] = _exact(a_ref[...], b_ref[...])

    return _pc(body, "exact_mm", _sds((A.shape[0], B.shape[1])))(A, B)


def _conv_shift(x, d, L):
    T = x.shape[0]
    if d == 0:
        return x
    t = _iota(x.shape, 0)
    src = t + d
    ok = (src >= 0) & (src < T) & ((src >= L) == (t >= L))
    return jnp.where(ok, pltpu.roll(x, (-d) % T, 0), 0.0)


def conv_fwd(XBC, w8, b, L, sends=()):
    T = XBC.shape[0]

    def body(x_ref, w_ref, b_ref, o_ref):
        x = x_ref[...]
        pre = b_ref[...] + functools.reduce(
            lambda a, c: a + c, [_conv_shift(x, k - 3, L) * w_ref[k:k + 1, :] for k in range(7)])
        o_ref[...] = _silu(pre)

    col = pl.BlockSpec((T, 128), lambda j: (0, j))
    return _pc(body, "conv_fwd", _sds((T, 1024)), grid=(8,),
               in_specs=[col, pl.BlockSpec((8, 128), lambda j: (0, j)), pl.BlockSpec((1, 128), lambda j: (0, j))],
               out_specs=col, sends=sends, gather=True)(XBC, w8, b)


def conv_bwd(XBC, w8, b, dS, dxs_skip, L, sends=()):
    T = XBC.shape[0]

    def body(x_ref, w_ref, b_ref, d0_r, d1_r, dsk_r, dx_o, dw_o, db_o):
        j = pl.program_id(0)
        x = x_ref[...]
        xs = [_conv_shift(x, k - 3, L) for k in range(7)]
        pre = b_ref[...] + functools.reduce(lambda a, c: a + c, [xs[k] * w_ref[k:k + 1, :] for k in range(7)])
        _, vjp = jax.vjp(_silu, pre)
        dact = d0_r[0] + d1_r[0] + jnp.where(j < 4, dsk_r[...], 0.0)
        dpre, = vjp(dact)
        dx_o[...] = functools.reduce(
            lambda a, c: a + c, [_conv_shift(dpre, 3 - k, L) * w_ref[k:k + 1, :] for k in range(7)])
        dw_o[...] = jnp.concatenate([jnp.sum(dpre * xs[k], axis=0, keepdims=True) for k in range(7)]
                                    + [jnp.zeros((1, 128), F32)], axis=0)
        db_o[...] = jnp.sum(dpre, axis=0, keepdims=True)

    col = pl.BlockSpec((T, 128), lambda j: (0, j))
    w_s = pl.BlockSpec((8, 128), lambda j: (0, j))
    b_s = pl.BlockSpec((1, 128), lambda j: (0, j))
    ds = lambda d: pl.BlockSpec((1, T, 128), lambda j: (d, 0, j))
    return _pc(body, "conv_bwd", [_sds((T, 1024)), _sds((8, 1024)), _sds((1, 1024))], grid=(8,),
               in_specs=[col, w_s, b_s, ds(0), ds(1), pl.BlockSpec((T, 128), lambda j: (0, jnp.minimum(j, 3)))],
               out_specs=[col, w_s, b_s], sends=sends)(XBC, w8, b, dS, dS, dxs_skip)


def _ssd_chunk(xs, bs, cs, dtraw, dtb, alog, hs, tri, d):
    dt = _softplus(dtraw + dtb)
    a = dt * (-jnp.exp(alog))
    acum = _exact(tri, a)
    tot = jnp.sum(a, axis=0, keepdims=True)
    wcol = jnp.exp(tot - acum) * dt
    ea = jnp.exp(acum)
    cd = jnp.exp(tot)
    acum_t, dt_t = acum.T, dt.T
    lane = _iota((Q, 128), 1)
    srow = _iota((128, Q), 0)
    lane1 = _iota((1, 128), 1)
    prow = _iota((128, NSTATE), 0)
    mask = tri > 0.5
    cbs = [mm_nt(cs[g], bs[g]) for g in range(2)]
    ys, hn = [], []
    for j in range(4):
        g = j // 2
        x = xs[j]
        yi, st, eac, cdl = [], [], [], []
        for u in range(2):
            slot = d * 8 + 2 * j + u
            col = lambda m: jnp.sum(jnp.where(lane == slot, m, 0.0), axis=1, keepdims=True)
            rowv = lambda m: jnp.sum(jnp.where(srow == slot, m, 0.0), axis=0, keepdims=True)
            seg = col(acum) - rowv(acum_t)
            dcy = jnp.where(mask, jnp.exp(jnp.where(mask, seg, 0.0)), 0.0)
            yi.append(mm(cbs[g] * dcy * rowv(dt_t), x))
            st.append(mm_tn(x, bs[g] * col(wcol)))
            eac.append(col(ea))
            cdl.append(jnp.sum(jnp.where(lane1 == slot, cd, 0.0), axis=1, keepdims=True))
        yin = mm_nt(cs[g], hs[j])
        ys.append(jnp.where(lane < HD, yi[0] + yin * eac[0], yi[1] + yin * eac[1]))
        hn.append(hs[j] * jnp.where(prow < HD, cdl[0], cdl[1]) + jnp.where(prow < HD, st[0], st[1]))
    return ys, hn


SSD_SUB = 2


def _ssd_block_idx(d, s, nlb, nbk):
    return jnp.where(d == 0, (s + nlb) % nbk, nbk - 1 - s)


def _ssd_rows(d, i):
    return pl.ds(pl.multiple_of(jnp.where(d == 0, i, SSD_SUB - 1 - i) * Q, Q), Q)


def _ssd_split(a):
    return ([a[:, 128 * j:128 * (j + 1)] for j in range(4)], [a[:, 512 + 128 * g:640 + 128 * g] for g in range(2)],
            [a[:, 768 + 128 * g:896 + 128 * g] for g in range(2)])


def ssd_fwd(ACT, DT, dtb, alog, tri2, L, sends=()):
    T = ACT.shape[0]
    RB = SSD_SUB * Q
    nlb, nbk = L // RB, T // RB

    def body(a_ref, dt_ref, dtb_ref, al_ref, tri_ref, y_o, hs_o, hst):
        d, s = pl.program_id(0), pl.program_id(1)
        _acc_init(s == 0, [hst])
        for i in range(SSD_SUB):
            rows = _ssd_rows(d, i)
            xs, bs, cs = _ssd_split(a_ref[rows, :])
            hs_o[0, i] = hst[...]
            ys, hn = _ssd_chunk(xs, bs, cs, dt_ref[rows, :], dtb_ref[...], al_ref[...], [hst[j] for j in range(4)],
                                tri_ref[0], d)
            y_o[0, rows, :] = jnp.concatenate(ys, axis=1)
            for j in range(4):
                hst[j] = hn[j]

    bk = lambda w: pl.BlockSpec((RB, w), lambda d, s: (_ssd_block_idx(d, s, nlb, nbk), 0))
    v128 = pl.BlockSpec((1, 128), lambda d, s: (0, 0))
    return _pc(body, "ssd_fwd", [_sds((2, T, 512)), _sds((2, T // Q, 4, 128, NSTATE))], grid=(2, nbk),
               in_specs=[bk(1024), bk(128), v128, v128, pl.BlockSpec((1, Q, Q), lambda d, s: (d, 0, 0))],
               out_specs=[pl.BlockSpec((1, RB, 512), lambda d, s: (d, _ssd_block_idx(d, s, nlb, nbk), 0)),
                          pl.BlockSpec((1, SSD_SUB, 4, 128, NSTATE), lambda d, s: (d, s, 0, 0, 0))],
               scratch=[pltpu.VMEM((4, 128, NSTATE), F32)], sends=sends, gather=True)(ACT, DT, dtb, alog, tri2)


def ssd_bwd(ACT, DT, dtb, alog, tri2, HS, dY, L, sends=()):
    T = ACT.shape[0]
    RB = SSD_SUB * Q
    nlb, nbk = L // RB, T // RB

    def body(a_ref, dt_ref, dtb_ref, al_ref, tri_ref, hs_ref, dy_ref, da_o, ddt_o, ddtb_o, dal_o, dh):
        d, sr = pl.program_id(0), pl.program_id(1)
        _acc_init(sr == 0, [dh, ddtb_o, dal_o])
        tri = tri_ref[0]
        fn = lambda xs_, bs_, cs_, dtr, dtb_, al, hs_: _ssd_chunk(xs_, bs_, cs_, dtr, dtb_, al, hs_, tri, d)
        for i in reversed(range(SSD_SUB)):
            rows = _ssd_rows(d, i)
            xs, bs, cs = _ssd_split(a_ref[rows, :])
            _, vjp = jax.vjp(fn, xs, bs, cs, dt_ref[rows, :], dtb_ref[...], al_ref[...],
                             [hs_ref[0, i, j] for j in range(4)])
            dy = dy_ref[rows, :]
            dxs, dbs, dcs, ddt, ddtb, dal, dhs = vjp(([dy[:, 128 * j:128 * (j + 1)] for j in range(4)],
                                                      [dh[j] for j in range(4)]))
            da_o[0, rows, :] = jnp.concatenate(dxs + dbs + dcs, axis=1)
            ddt_o[0, rows, :] = ddt
            ddtb_o[0] += ddtb
            dal_o[0] += dal
            for j in range(4):
                dh[j] = dhs[j]

    bidx = lambda d, sr: _ssd_block_idx(d, nbk - 1 - sr, nlb, nbk)
    bk = lambda w: pl.BlockSpec((RB, w), lambda d, sr: (bidx(d, sr), 0))
    v128 = pl.BlockSpec((1, 128), lambda d, sr: (0, 0))
    o128 = pl.BlockSpec((1, 1, 128), lambda d, sr: (d, 0, 0))
    return _pc(body, "ssd_bwd", [_sds((2, T, 1024)), _sds((2, T, 128)), _sds((2, 1, 128)), _sds((2, 1, 128))],
               grid=(2, nbk),
               in_specs=[bk(1024), bk(128), v128, v128, pl.BlockSpec((1, Q, Q), lambda d, sr: (d, 0, 0)),
                         pl.BlockSpec((1, SSD_SUB, 4, 128, NSTATE), lambda d, sr: (d, nbk - 1 - sr, 0, 0, 0)), bk(512)],
               out_specs=[pl.BlockSpec((1, RB, 1024), lambda d, sr: (d, bidx(d, sr), 0)),
                          pl.BlockSpec((1, RB, 128), lambda d, sr: (d, bidx(d, sr), 0)), o128, o128],
               scratch=[pltpu.VMEM((4, 128, NSTATE), F32)], sends=sends)(ACT, DT, dtb, alog, tri2, HS, dY)


_PAIR_HEADS = np.array([[0, 2], [1, 3]])


def _tables(L):
    t = jnp.arange(L)
    inv = 10000.0 ** (-jnp.arange(16, dtype=F32) / 16)

    def half(pos):
        ang = pos.astype(F32)[:, None] * inv[None, :]
        return jnp.concatenate([ang, ang], axis=1)

    ang = jnp.tile(jnp.concatenate([half(t // GW), half(t % GW)], axis=1), (1, 4))
    cos = jnp.concatenate([jnp.cos(ang), jnp.ones((LC, 256), F32)], axis=0)
    sin = jnp.concatenate([jnp.sin(ang), jnp.zeros((LC, 256), F32)], axis=0)
    rm = np.zeros((256, 256), np.float32)
    for j in range(256):
        if j % 32 < 16:
            rm[j + 16, j] = -1.0
        else:
            rm[j - 16, j] = 1.0
    tri = np.tril(np.ones((Q, Q), np.float32))
    return cos, sin, jnp.asarray(rm), jnp.asarray(np.stack([tri, tri.T]))


def _na_index(R):
    rc = np.array([0, 1, 2, 3, 4, R - 3, R - 2, R - 1])
    dy = np.clip(rc - 4, 0, R - 8)[:, None] + np.arange(8)[None, :] - rc[:, None] + 7
    qc, cc = np.arange(GW)[:, None], np.arange(GW)[None, :]
    dx = np.clip(cc - qc, -15, 15) + 15
    cstart = np.clip(qc - 8, 0, GW - 16)
    cmask = (cc >= cstart) & (cc < cstart + 16)
    idx = dy[:, None, :, None] * 31 + dx[None, :, None, :]
    return idx.reshape(8, GW, 8 * GW), np.broadcast_to(cmask[None, :, None, :], idx.shape).reshape(8, GW, 8 * GW), \
        dy, dx, cmask


def _na_bias(rpb, R):
    _, cm, dy, _, _ = _na_index(R)
    rows = rpb[:, dy.reshape(-1), :].reshape(4, 8, 4, 2, 31)
    p2 = jnp.pad(jnp.pad(rows, ((0, 0),) * 4 + ((0, 33),)).reshape(4, 8, 4, 128), ((0, 0), (0, 0), (0, 4), (0, 0)))
    negmask = jnp.asarray(np.where(cm[0], 0.0, NEG).astype(np.float32))

    def body(p_ref, m_ref, o_ref):
        for c in range(8):
            tiles = [pltpu.roll(jnp.broadcast_to(p_ref[0, c, jp:jp + 1, :], (GW, 128)), 113, 1, stride=1,
                                stride_axis=0) for jp in range(4)]
            o_ref[0, c] = jnp.where(m_ref[...] < 0.0, NEG, jnp.concatenate(tiles, axis=1))

    return _pc(body, "na_bias", _sds((4, 8, GW, 8 * GW)), grid=(4,),
               in_specs=[pl.BlockSpec((1, 8, 8, 128), lambda h: (h, 0, 0, 0)),
                         pl.BlockSpec((GW, 8 * GW), lambda h: (0, 0))],
               out_specs=pl.BlockSpec((1, 8, GW, 8 * GW), lambda h: (h, 0, 0, 0)))(p2, negmask)


def _na_bias_grad(dbias, R):
    _, _, dy, dx, cmask = _na_index(R)
    e1 = np.zeros((GW * GW, 128), np.float32)
    e1[np.arange(GW * GW), dx.reshape(-1)] = cmask.reshape(-1)
    a1 = dbias.reshape(4, 8, GW, 8, GW).transpose(0, 1, 3, 2, 4).reshape(256, GW * GW)
    v = exact_mm_call(a1, jnp.asarray(e1))[:, :31].reshape(4, 64, 31)
    e2 = np.zeros((64, 128), np.float32)
    e2[np.arange(64), dy.reshape(-1)] = 1.0
    a2 = jnp.pad(v.transpose(0, 2, 1).reshape(124, 64), ((0, 4), (0, 0)))
    return exact_mm_call(a2, jnp.asarray(e2))[:124, :15].reshape(4, 31, 15).transpose(0, 2, 1)


def _lanes(v, n=128):
    v = v.reshape(1, -1)
    return jnp.pad(v, ((0, 0), (0, n - v.shape[1])))


def _cls2(a, b):
    return jnp.stack([a, b]).reshape(2, 1, D)


def _win_p(g):
    return jnp.concatenate([g.reshape(IN_COLS, D), jnp.zeros((NP_IN - IN_COLS, D), g.dtype)], axis=0)


def _layer_consts(p):
    sinkp = jnp.broadcast_to(p["wa_sink"][_PAIR_HEADS][:, :, None, None], (2, 2, 1, 128))
    return dict(
        sinkp=sinkp, nosink=jnp.full((2, 2, 1, 128), NEG, F32),
        w8=jnp.concatenate([p["ssm_conv_w"], jnp.zeros((1, 1024), F32)], axis=0),
        cb=p["ssm_conv_b"].reshape(1, 1024), dtb=_lanes(p["ssm_dt_bias"]), alog=_lanes(p["ssm_a_log"]),
        dsk=jnp.repeat(p["ssm_d"], HD).reshape(1, 512), gs=p["ssm_norm_g"].reshape(1, 512),
        gmix=p["g_mix"].reshape(1, D), gffn=p["g_ffn"].reshape(1, D))


def _mods(mod2):
    return [_cls2(mod2[0, D * k:D * (k + 1)], mod2[1, D * k:D * (k + 1)]) for k in range(6)]


def _layer_fwd(X, mod2, c, rpb, tabs, L, ctx_out, shards, nxt):
    cos, sin, rm, tri2 = tabs
    sh1, sc1, gt1, sh2, sc2, gt2 = _mods(mod2)
    biasd = _na_bias(rpb, L // GW)
    fi, fo, wo = shards
    fcut, fcut2, ocut = 384, 608, 224
    (qa, qb, z, ka, va, kb, vb, xbc, dt, h1), (gfo_a,) = in_fwd(X, c["gmix"], sh1, sc1, c["win"], cos, sin, rm, L,
                                                                sends=(fo[:ocut],))
    (oa,), (gfi_b,) = wa_fwd(qa, ka, va, c["sinkp"], L, sends=(fi[fcut:fcut2],))
    (ob,), (gwo,) = na_fwd(qb, kb, vb, biasd, L, sends=(wo,))
    c = dict(c, wout=gwo.reshape(D, D))
    if ctx_out:
        oa_c = ctx_fwd(qa, ka, va, c["sinkp"], True, L)
        ob_c = ctx_fwd(qb, kb, vb, c["nosink"], False, L)
    else:
        oa_c = ob_c = jnp.zeros((LC, 256), F32)
    oa = jnp.concatenate([oa, oa_c], axis=0)
    ob = jnp.concatenate([ob, ob_c], axis=0)
    (act,), (gfi_c,) = conv_fwd(xbc, c["w8"], c["cb"], L, sends=(fi[fcut2:],))
    (y2, hs), (gfi_a,) = ssd_fwd(act, dt, c["dtb"], c["alog"], tri2, L, sends=(fi[:fcut],))
    (X1, cat), (gfo_b,) = out_fwd(oa, ob, y2, act, z, c["dsk"], c["gs"], c["wout"], X, gt1, L, sends=(fo[ocut:],))
    c = dict(c, wfi=jnp.concatenate([gfi_a, gfi_b, gfi_c], axis=1).reshape(2 * DFF, D),
             wfo=jnp.concatenate([gfo_a, gfo_b], axis=1).reshape(DFF, D))
    res = ffn_fwd(X1, c["gffn"], sh2, sc2, gt2, c["wfi"], c["wfo"], L, sends=nxt, skip_ctx=not ctx_out)
    (X2, ff), got = res if nxt else (res, ())
    saved = dict(X=X, X1=X1, ff=ff, qa=qa, qb=qb, z=z, ka=ka, va=va, kb=kb, vb=vb, xbc=xbc, dt=dt, h1=h1, oa=oa, ob=ob,
                 act=act, y2=y2, hs=hs, cat=cat, biasd=biasd)
    return X2, saved, c, got


def _row_blocks(gw):
    return gw.reshape(NDEV, gw.shape[0] // NDEV, gw.shape[1])


def _layer_bwd(dX2, s, mod2, c, tabs, L, ctx_out, carry):
    cos, sin, rm, tri2 = tabs
    sh1, sc1, gt1, sh2, sc2, gt2 = _mods(mod2)
    R = L // GW
    res = ffn_bwd(s["X1"], c["gffn"], sh2, sc2, gt2, c["wfi"], c["wfo"], s["ff"], dX2, L, sends=carry,
                  skip_ctx=not ctx_out)
    (dX1, h2, dU, actf, dOut, dgffn, dsh2, dsc2, dgt2), got = res if carry else (res, ())
    g = {}
    lat = None if ctx_out else L
    gfi = _row_blocks(tn_mm(dU, h2, 512, 1024, MXU, rows=lat))
    gfo = _row_blocks(tn_mm(actf, dOut, 256, 1024, MXU, rows=lat))
    doa, dob, dy, dxs_skip, dz, dmix, ddsk, dgs, dgt1 = out_bwd(s["oa"], s["ob"], s["y2"], s["act"], s["z"], c["dsk"],
                                                                c["gs"], c["wout"], gt1, dX1, L)
    gout = _row_blocks(tn_mm(s["cat"], dmix, 512, 1024, MXU, rows=lat))
    (dS, ddt2, ddtb, dal), (g["w_ffn_in"],) = ssd_bwd(
        s["act"], s["dt"], c["dtb"], c["alog"], tri2, s["hs"], dy, L, sends=(gfi,))
    (dxbc, dw8, dcb), (g["w_ffn_out"],) = conv_bwd(s["xbc"], c["w8"], c["cb"], dS, dxs_skip, L, sends=(gfo,))
    (dqa, dka, dva, dska), (g["w_out"],) = wa_bwd(s["qa"], s["ka"], s["va"], c["sinkp"], doa, L, sends=(gout,))
    dka, dva = dka[Q:], dva[Q:]
    dqb, dkb, dvb, dkxb, dvxb, dbias = na_bwd(s["qb"], s["kb"], s["vb"], s["biasd"], dob, L)
    if ctx_out:
        dqa_c, dk1, dv1, dsk1 = ctx_bwd(s["qa"], s["ka"], s["va"], c["sinkp"], doa, True, L)
        dqb_c, dk2, dv2, _ = ctx_bwd(s["qb"], s["kb"], s["vb"], c["nosink"], dob, False, L)
        dka = jnp.concatenate([dka[:L], dka[L:] + dk1], axis=0)
        dva = jnp.concatenate([dva[:L], dva[L:] + dv1], axis=0)
        dska = dska + dsk1
        dkxb, dvxb = dkxb + dk2, dvxb + dv2
    else:
        dqa_c = dqb_c = jnp.zeros((LC, 256), F32)
    cat0 = lambda a, b: jnp.concatenate([a, b], axis=0)
    dX, dycat, dgmix, dsh1, dsc1 = in_bwd(
        s["X"], c["gmix"], sh1, sc1, c["win"], cos, sin, rm, dX1, cat0(dqa, dqa_c), cat0(dqb, dqb_c), dz,
        dka, dva, cat0(dkb, dkxb), cat0(dvb, dvxb), dxbc, ddt2, L,
        latent_only=ctx_out)
    if ctx_out:
        half = lambda k, **kw: tn_mm(dycat, s["h1"], 512, D // 2, MXU, ncol=D // 2, col0=k, **kw)
        (g1,), (got0,) = half(1, sends=(_row_blocks(half(0)[:IN_COLS]),))
        gin = (got0, _row_blocks(g1[:IN_COLS]))
    else:
        gin = _row_blocks(tn_mm(dycat, s["h1"], 512, 1024, MXU)[:IN_COLS])
    g["g_mix"] = dgmix.reshape(D)
    g["g_ffn"] = dgffn.reshape(D)
    sk = jnp.sum(dska, axis=(2, 3))
    g["wa_sink"] = jnp.zeros((4,), F32).at[_PAIR_HEADS.reshape(-1)].set(sk.reshape(-1))
    g["na_rpb"] = _na_bias_grad(dbias, R)
    g["ssm_conv_w"] = dw8[:7]
    g["ssm_conv_b"] = dcb.reshape(1024)
    g["ssm_dt_bias"] = (ddtb[0] + ddtb[1])[0, :16].reshape(2, 8)
    g["ssm_a_log"] = (dal[0] + dal[1])[0, :16].reshape(2, 8)
    g["ssm_d"] = jnp.sum(ddsk.reshape(8, HD), axis=1)
    g["ssm_norm_g"] = dgs.reshape(512)
    dmod2 = jnp.concatenate([dsh1, dsc1, dgt1, dsh2, dsc2, dgt2], axis=2).reshape(2, 6 * D)
    return dX, g, dmod2, gin, got


def local_step(x, ctx, tgt, mods, layers, shards, g_final, L):
    tabs = _tables(L)
    X = (x, ctx)
    consts = [_layer_consts(p) for p in layers]
    saved = []
    got = (shards["w_in_first"],)
    for i in range(2):
        consts[i] = dict(consts[i], win=_win_p(got[0]))
        nxt = (shards["w_in"][1],) if i == 0 else ()
        X, s, consts[i], got = _layer_fwd(X, mods[i], consts[i], layers[i]["na_rpb"], tabs, L, i == 0,
                                          (shards["w_ffn_in"][i], shards["w_ffn_out"][i], shards["w_out"][i]), nxt)
        saved.append(s)
    loss8, dX, dgfin = loss_head(X, g_final.reshape(1, D), tgt, L)
    grads, dmods = [None, None], [None, None]
    dX, grads[1], dmods[1], gin1, _ = _layer_bwd(dX, saved[1], mods[1], consts[1], tabs, L, False, ())
    dX, grads[0], dmods[0], gin0, (grads[1]["w_in"],) = _layer_bwd(dX, saved[0], mods[0], consts[0], tabs, L, True,
                                                                   (gin1,))
    return loss8[0, 0], dX, grads, jnp.stack(dmods), dgfin.reshape(D), gin0


def _place():
    x, y, c = lax.axis_index("x"), lax.axis_index("y"), lax.axis_index("c")
    return x, y, c


def _slot(b):
    return 4 * b[0] + 2 * b[1] + b[2]


def _any():
    return pl.BlockSpec(memory_space=pl.ANY)


def all_gather(xs, name):
    n = len(xs)

    def body(*refs):
        x_refs, o_refs = refs[:n], refs[n:2 * n]
        send_sems, recv_sems, local_sems = refs[2 * n:]
        x, y, c = _place()
        me, sib = (x, y, c), (x, y, 1 - c)
        chips = [(1 - x, y), (x, 1 - y), (1 - x, 1 - y)]

        def copy(t, k, blk, to, src=None):
            dst = o_refs[t].at[_slot(blk)]
            return pltpu.make_async_remote_copy(
                src_ref=dst if src is None else src, dst_ref=dst, send_sem=send_sems.at[7 * t + k],
                recv_sem=recv_sems.at[7 * t + k], device_id=to, device_id_type=MESH_T)

        mine = [pltpu.make_async_copy(x_refs[t], o_refs[t].at[_slot(me)], local_sems.at[t]) for t in range(n)]
        for cp in mine:
            cp.start()
        first = []
        for t in range(n):
            first.append(copy(t, 0, me, sib, src=x_refs[t]))
            first += [copy(t, 1 + j, me, (*chip, c), src=x_refs[t]) for j, chip in enumerate(chips)]
        for cp in first:
            cp.start()
        passed = []
        for j, chip in enumerate(chips):
            for t in range(n):
                copy(t, 1 + j, (*chip, c), me).wait_recv()
                cp = copy(t, 4 + j, (*chip, c), sib)
                cp.start()
                passed.append(cp)
        for t in range(n):
            copy(t, 0, sib, me).wait_recv()
            for j, chip in enumerate(chips):
                copy(t, 4 + j, (*chip, 1 - c), me).wait_recv()
        for cp in first + passed:
            cp.wait_send()
        for cp in mine:
            cp.wait()

    return pl.pallas_call(
        body, name=name, out_shape=[_sds((NDEV,) + a.shape, a.dtype) for a in xs],
        in_specs=[_any()] * n, out_specs=[_any()] * n,
        scratch_shapes=[pltpu.SemaphoreType.DMA((7 * n,)), pltpu.SemaphoreType.DMA((7 * n,)),
                        pltpu.SemaphoreType.DMA((n,))],
        interpret=_INTERPRET)(*xs)


def _a2a_sems(n):
    return [pltpu.SemaphoreType.DMA((7 * n,)), pltpu.SemaphoreType.DMA((7 * n,)), pltpu.SemaphoreType.DMA((n,))]


def _a2a_copies(x_refs, o_refs, send_sems, recv_sems, local_sems):
    n = len(x_refs)
    x, y, c = _place()
    me = (x, y, c)
    flip = lambda v, b: (1 - v) if b else v
    peers = [(flip(x, k >> 2 & 1), flip(y, k >> 1 & 1), flip(c, k & 1)) for k in range(1, NDEV)]
    mine = [pltpu.make_async_copy(x_refs[t].at[_slot(me)], o_refs[t].at[_slot(me)], local_sems.at[t])
            for t in range(n)]

    def copy(t, k, src_slot, dst_slot, to):
        return pltpu.make_async_remote_copy(
            src_ref=x_refs[t].at[src_slot], dst_ref=o_refs[t].at[dst_slot], send_sem=send_sems.at[7 * t + k],
            recv_sem=recv_sems.at[7 * t + k], device_id=to, device_id_type=MESH_T)

    sends = [copy(t, k, _slot(p), _slot(me), p) for t in range(n) for k, p in enumerate(peers)]
    recvs = [copy(t, k, _slot(p), _slot(p), me) for t in range(n) for k, p in enumerate(peers)]
    return mine, sends, recvs


def _ag_copies(x_refs, o_refs, send_sems, recv_sems, local_sems):
    n = len(x_refs)
    x, y, c = _place()
    me = (x, y, c)
    flip = lambda v, b: (1 - v) if b else v
    peers = [(flip(x, k >> 2 & 1), flip(y, k >> 1 & 1), flip(c, k & 1)) for k in range(1, NDEV)]
    mine = [pltpu.make_async_copy(x_refs[t], o_refs[t].at[_slot(me)], local_sems.at[t]) for t in range(n)]

    def copy(t, k, dst_slot, to):
        return pltpu.make_async_remote_copy(
            src_ref=x_refs[t], dst_ref=o_refs[t].at[dst_slot], send_sem=send_sems.at[7 * t + k],
            recv_sem=recv_sems.at[7 * t + k], device_id=to, device_id_type=MESH_T)

    sends = [copy(t, k, _slot(me), p) for t in range(n) for k, p in enumerate(peers)]
    recvs = [copy(t, k, _slot(p), me) for t in range(n) for k, p in enumerate(peers)]
    return mine, sends, recvs


def _ag_start(x_refs, o_refs, send_sems, recv_sems, local_sems):
    mine, sends, _ = _ag_copies(x_refs, o_refs, send_sems, recv_sems, local_sems)
    for cp in mine + sends:
        cp.start()


def _ag_wait(x_refs, o_refs, send_sems, recv_sems, local_sems):
    mine, sends, recvs = _ag_copies(x_refs, o_refs, send_sems, recv_sems, local_sems)
    for cp in recvs:
        cp.wait_recv()
    for cp in sends:
        cp.wait_send()
    for cp in mine:
        cp.wait()


def _a2a_start(x_refs, o_refs, send_sems, recv_sems, local_sems):
    mine, sends, _ = _a2a_copies(x_refs, o_refs, send_sems, recv_sems, local_sems)
    for cp in mine + sends:
        cp.start()


def _a2a_wait(x_refs, o_refs, send_sems, recv_sems, local_sems):
    mine, sends, recvs = _a2a_copies(x_refs, o_refs, send_sems, recv_sems, local_sems)
    for cp in recvs:
        cp.wait_recv()
    for cp in sends:
        cp.wait_send()
    for cp in mine:
        cp.wait()


def adam_reduce(P, w, m, v, name, sends=()):
    n, R, C = P.shape
    br = R // 4 if R % 64 == 0 else R

    def body(p_ref, w_ref, m_ref, v_ref, g_o, d_o, m_o, v_o):
        g = p_ref[0].astype(F32)
        for k in range(1, n):
            g = g + p_ref[k].astype(F32)
        m1 = ADAM_B1 * m_ref[...] + (1.0 - ADAM_B1) * g
        v1 = ADAM_B2 * v_ref[...] + (1.0 - ADAM_B2) * jnp.square(g)
        m_hat = m1 / (1.0 - ADAM_B1 ** ADAM_STEP)
        v_hat = v1 / (1.0 - ADAM_B2 ** ADAM_STEP)
        g_o[...] = g
        d_o[...] = -ADAM_LR * (m_hat / (jnp.sqrt(v_hat) + ADAM_EPS) + ADAM_WD * w_ref[...])
        m_o[...] = m1
        v_o[...] = v1

    blk = pl.BlockSpec((br, C), lambda i: (i, 0))
    return _pc(body, name, [_sds((R, C))] * 4, grid=(R // br,),
               in_specs=[pl.BlockSpec((n, br, C), lambda i: (0, i, 0)), blk, blk, blk], out_specs=[blk] * 4,
               sends=sends)(P, w, m, v)


def adam_layers(P0, P1, w, m, v, name, sends=()):
    n, R, C = P0.shape
    br = R // 4 if R % 64 == 0 else R
    nb = R // br

    def body(p0_ref, p1_ref, w_ref, m_ref, v_ref, g_o, d_o, m_o, v_o):
        def total(p_ref):
            g = p_ref[0].astype(F32)
            for k in range(1, n):
                g = g + p_ref[k].astype(F32)
            return g

        g = jnp.where(pl.program_id(0) == 0, total(p0_ref), total(p1_ref))
        m1 = ADAM_B1 * m_ref[0] + (1.0 - ADAM_B1) * g
        v1 = ADAM_B2 * v_ref[0] + (1.0 - ADAM_B2) * jnp.square(g)
        m_hat = m1 / (1.0 - ADAM_B1 ** ADAM_STEP)
        v_hat = v1 / (1.0 - ADAM_B2 ** ADAM_STEP)
        g_o[0] = g
        d_o[0] = -ADAM_LR * (m_hat / (jnp.sqrt(v_hat) + ADAM_EPS) + ADAM_WD * w_ref[0])
        m_o[0] = m1
        v_o[0] = v1

    blk = pl.BlockSpec((1, br, C), lambda l, i: (l, i, 0))
    p0 = pl.BlockSpec((n, br, C), lambda l, i: (0, jnp.where(l == 0, i, nb - 1), 0))
    p1 = pl.BlockSpec((n, br, C), lambda l, i: (0, jnp.where(l == 1, i, 0), 0))
    return _pc(body, name, [_sds((2, R, C))] * 4, grid=(2, nb), in_specs=[p0, p1, blk, blk, blk],
               out_specs=[blk] * 4, sends=sends)(P0, P1, w, m, v)


def mod_fwd(scin, wmod, bcol):
    def body(s_ref, w_ref, b_ref, o_ref):
        o_ref[0] = mm(_silu(s_ref[...]), w_ref[0]) + b_ref[0]

    return _pc(body, "mod_fwd", _sds((2, 16, 768)), grid=(2,),
               in_specs=[pl.BlockSpec((16, D), lambda l: (0, 0)), pl.BlockSpec((1, D, 768), lambda l: (l, 0, 0)),
                         pl.BlockSpec((1, 1, 768), lambda l: (l, 0, 0))],
               out_specs=pl.BlockSpec((1, 16, 768), lambda l: (l, 0, 0)))(scin, wmod, bcol)


def mod_bwd(scin, wmod, G):
    def body(s_ref, w_ref, g_ref, dw_o, ds_o):
        _, vjp = jax.vjp(lambda s, w: mm(_silu(s), w), s_ref[...], w_ref[0])
        ds, dw = vjp(g_ref[0])
        dw_o[0] = dw
        _acc_init(pl.program_id(0) == 0, [ds_o])
        ds_o[...] += ds

    full = pl.BlockSpec((16, D), lambda l: (0, 0))
    wsp = pl.BlockSpec((1, D, 768), lambda l: (l, 0, 0))
    return _pc(body, "mod_bwd", [_sds((2, D, 768)), _sds((16, D))], grid=(2,),
               in_specs=[full, wsp, pl.BlockSpec((1, 16, 768), lambda l: (l, 0, 0))], out_specs=[wsp, full])(
        scin, wmod, G)


_SMALL = ["b_mod", "g_mix", "wa_sink", "na_rpb", "ssm_conv_w", "ssm_conv_b", "ssm_dt_bias", "ssm_a_log", "ssm_d",
          "ssm_norm_g", "g_ffn", "g_final", "dmod_s", "dmod_c", "loss"]


def _pack(parts):
    rows = []
    for a in parts:
        f = a.reshape(-1).astype(F32)
        rows.append(jnp.pad(f, (0, (-f.shape[0]) % 1024)).reshape(-1, 128))
    return jnp.concatenate(rows, axis=0)


def _unpack(packed, shapes):
    out, r = [], 0
    for s in shapes:
        nel = int(np.prod(s))
        nr = -(-nel // 1024) * 8
        out.append(packed[r:r + nr].reshape(-1)[:nel].reshape(s))
        r += nr
    return out


def kernel(x, c, ctx, c_ctx, w_mod, b_mod, g_mix, w_in, wa_sink, na_rpb, ssm_conv_w, ssm_conv_b, ssm_dt_bias, ssm_a_log, ssm_d, ssm_norm_g, w_out, g_ffn, w_ffn_in, w_ffn_out, g_final, loss_target, m_c_ctx, m_w_mod, m_b_mod, m_g_mix, m_w_in, m_wa_sink, m_na_rpb, m_ssm_conv_w, m_ssm_conv_b, m_ssm_dt_bias, m_ssm_a_log, m_ssm_d, m_ssm_norm_g, m_w_out, m_g_ffn, m_w_ffn_in, m_w_ffn_out, m_g_final, v_c_ctx, v_w_mod, v_b_mod, v_g_mix, v_w_in, v_wa_sink, v_na_rpb, v_ssm_conv_w, v_ssm_conv_b, v_ssm_dt_bias, v_ssm_a_log, v_ssm_d, v_ssm_norm_g, v_w_out, v_g_ffn, v_w_ffn_in, v_w_ffn_out, v_g_final):
    L = x.shape[1]
    px, py, pc = _place()
    me = 4 * px + 2 * py + pc
    W = dict(c_ctx=c_ctx, w_mod=w_mod, b_mod=b_mod, g_mix=g_mix, w_in=w_in, wa_sink=wa_sink, na_rpb=na_rpb,
             ssm_conv_w=ssm_conv_w, ssm_conv_b=ssm_conv_b, ssm_dt_bias=ssm_dt_bias, ssm_a_log=ssm_a_log, ssm_d=ssm_d,
             ssm_norm_g=ssm_norm_g, w_out=w_out, g_ffn=g_ffn, w_ffn_in=w_ffn_in, w_ffn_out=w_ffn_out, g_final=g_final)
    M = dict(c_ctx=m_c_ctx, w_mod=m_w_mod, b_mod=m_b_mod, g_mix=m_g_mix, w_in=m_w_in, wa_sink=m_wa_sink,
             na_rpb=m_na_rpb, ssm_conv_w=m_ssm_conv_w, ssm_conv_b=m_ssm_conv_b, ssm_dt_bias=m_ssm_dt_bias,
             ssm_a_log=m_ssm_a_log, ssm_d=m_ssm_d, ssm_norm_g=m_ssm_norm_g, w_out=m_w_out, g_ffn=m_g_ffn,
             w_ffn_in=m_w_ffn_in, w_ffn_out=m_w_ffn_out, g_final=m_g_final)
    V = dict(c_ctx=v_c_ctx, w_mod=v_w_mod, b_mod=v_b_mod, g_mix=v_g_mix, w_in=v_w_in, wa_sink=v_wa_sink,
             na_rpb=v_na_rpb, ssm_conv_w=v_ssm_conv_w, ssm_conv_b=v_ssm_conv_b, ssm_dt_bias=v_ssm_dt_bias,
             ssm_a_log=v_ssm_a_log, ssm_d=v_ssm_d, ssm_norm_g=v_ssm_norm_g, w_out=v_w_out, g_ffn=v_g_ffn,
             w_ffn_in=v_w_ffn_in, w_ffn_out=v_w_ffn_out, g_final=v_g_final)

    tr = lambda a: a.transpose(0, 2, 1)
    shards = dict(w_in=tr(w_in).astype(MXU), w_out=w_out.astype(MXU), w_ffn_in=tr(w_ffn_in).astype(MXU),
                  w_ffn_out=w_ffn_out.astype(MXU))
    c_all, conv_all, shards["w_in_first"] = all_gather([c, ssm_conv_w, shards["w_in"][0]], "gather_first")
    conv_f = conv_all.transpose(1, 2, 0, 3).reshape(2, 7, 1024)

    scin = jnp.concatenate([c_all.reshape(NDEV, D), c_ctx.reshape(1, D), jnp.zeros((7, D), F32)], axis=0)
    bcol = lax.dynamic_slice_in_dim(b_mod, me * 768, 768, axis=1).reshape(2, 1, 768)
    mod_all, = all_gather([mod_fwd(scin, w_mod, bcol)], "gather_mod")
    mod_rows = mod_all.transpose(1, 2, 0, 3).reshape(2, 16, 6 * D)
    mods = jnp.stack([lax.dynamic_index_in_dim(mod_rows, me, axis=1, keepdims=False), mod_rows[:, 8]], axis=1)

    layers = [dict(g_mix=g_mix[i], wa_sink=wa_sink[i], na_rpb=na_rpb[i], ssm_conv_w=conv_f[i],
                   ssm_conv_b=ssm_conv_b[i], ssm_dt_bias=ssm_dt_bias[i], ssm_a_log=ssm_a_log[i], ssm_d=ssm_d[i],
                   ssm_norm_g=ssm_norm_g[i], g_ffn=g_ffn[i]) for i in range(2)]
    loss, dx, grads, dmods, dgfin, gin0 = local_step(x[0], ctx[0], loss_target[0], mods, layers, shards, g_final, L)

    stk = lambda n: jnp.stack([grads[0][n], grads[1][n]])
    small = dict(b_mod=dmods[:, 0] + dmods[:, 1], g_final=dgfin, dmod_s=dmods[:, 0], dmod_c=dmods[:, 1],
                 loss=loss.reshape(1))
    for nme in _SMALL:
        if nme not in small:
            small[nme] = stk(nme)
    shapes = [small[nme].shape for nme in _SMALL]
    zero_like = lambda nme: jnp.zeros(small[nme].shape, F32)
    own = lambda S, nme: S[nme] if (nme in S and S[nme].shape == small[nme].shape) else zero_like(nme)
    gath, = all_gather([_pack([small[nme] for nme in _SMALL])], "gather_grads")
    sm = adam_reduce(gath, _pack([own(W, nme) for nme in _SMALL]), _pack([own(M, nme) for nme in _SMALL]),
                     _pack([own(V, nme) for nme in _SMALL]), "adam_small")
    res = {nme: vals for nme, vals in zip(_SMALL, zip(*[_unpack(a, shapes) for a in sm]))}
    loss = res["loss"][0][0]

    cols = lambda a: lax.dynamic_slice_in_dim(a, me * 768, 768, axis=-1)
    rows_of = lambda s: -(-int(np.prod(s)) // 1024) * 8
    r0 = sum(rows_of(s) for s in shapes[:_SMALL.index("dmod_s")])
    dmod_s_all = gath[:, r0:r0 + rows_of(small["dmod_s"].shape)].reshape(NDEV, 2, 6 * D).transpose(1, 0, 2)
    G = jnp.concatenate([cols(dmod_s_all), cols(res["dmod_c"][0])[:, None, :], jnp.zeros((2, 7, 768), F32)], axis=1)
    dwmod, dscin = mod_bwd(scin, w_mod, G)
    cc_g, = all_gather([dscin[8].reshape(8, 128)], "gather_cctx")
    out = {}
    out["c_ctx"] = [a.reshape(D) for a in adam_reduce(cc_g, c_ctx.reshape(8, 128), m_c_ctx.reshape(8, 128),
                                                      v_c_ctx.reshape(8, 128), "adam_cctx")]
    out["w_mod"] = [a.reshape(2, D, 768) for a in adam_reduce(
        dwmod.reshape(1, 2 * D, 768), w_mod.reshape(2 * D, 768), m_w_mod.reshape(2 * D, 768),
        v_w_mod.reshape(2 * D, 768), "adam_wmod")]
    gconv = lax.dynamic_slice_in_dim(res["ssm_conv_w"][0], me * 128, 128, axis=2)
    out["ssm_conv_w"] = [a.reshape(2, 7, 128) for a in adam_reduce(
        gconv.reshape(1, 14, 128), ssm_conv_w.reshape(14, 128), m_ssm_conv_w.reshape(14, 128),
        v_ssm_conv_w.reshape(14, 128), "adam_conv")]
    for nme in _SMALL:
        if nme not in ("ssm_conv_w", "dmod_s", "dmod_c", "loss"):
            out[nme] = list(res[nme])

    adam_big = lambda nme, t, **kw: adam_layers(grads[0][nme], grads[1][nme], t(W[nme]), t(M[nme]), t(V[nme]),
                                                "adam_" + nme, **kw)
    same = lambda a: a
    res_fi, (got1,) = adam_big("w_ffn_in", tr, sends=(gin0[1],))
    grads[0]["w_in"] = jnp.concatenate([gin0[0], got1], axis=2)
    out["w_ffn_in"] = [tr(a) for a in res_fi]
    out["w_ffn_out"] = list(adam_big("w_ffn_out", same))
    out["w_out"] = list(adam_big("w_out", same))
    out["w_in"] = [tr(a) for a in adam_big("w_in", tr)]
    order = ["c_ctx", "w_mod", "b_mod", "g_mix", "w_in", "wa_sink", "na_rpb", "ssm_conv_w", "ssm_conv_b",
             "ssm_dt_bias", "ssm_a_log", "ssm_d", "ssm_norm_g", "w_out", "g_ffn", "w_ffn_in", "w_ffn_out", "g_final"]
    return (loss, dx.reshape(1, L, D), *[out[nme][0] for nme in order], *[out[nme][1] for nme in order],
            *[out[nme][2] for nme in order], *[out[nme][3] for nme in order])
```
